```python
import jax, jax.numpy as jnp
from jax import lax
import numpy as np


D_MODEL = 1024
BATCH = 8
SEQ = 16384
DEPTH = 2

N_MIXERS = 2
FOX_HEADS = 16
FOX_HEAD_DIM = D_MODEL // FOX_HEADS
Q_BLOCK = 128
SSM_D_INNER = 2 * D_MODEL
SSM_HEAD_DIM = 64
SSM_HEADS = SSM_D_INNER // SSM_HEAD_DIM
SSM_GROUPS = 8
SSM_HEADS_PER_GROUP = SSM_HEADS // SSM_GROUPS
SSM_STATE = 128
SSM_CONV = 4
SSM_CHUNK = 128
SSM_CONV_DIM = SSM_D_INNER + 2 * SSM_GROUPS * SSM_STATE
SSM_IN_DIM = SSM_D_INNER + SSM_CONV_DIM + SSM_HEADS
D_FF = 4 * D_MODEL
LN_EPS = 1e-5
RMS_EPS = 1e-5

kernel_name = 'hybrid_fox_ssd_sqrelu_deepnorm_adaln'


def layer_norm(x, g, b):
    xf = x.astype(jnp.float32)
    mu = jnp.mean(xf, axis=-1, keepdims=True)
    var = jnp.mean(jnp.square(xf - mu), axis=-1, keepdims=True)
    return ((xf - mu) * lax.rsqrt(var + LN_EPS) * g.astype(jnp.float32) + b.astype(jnp.float32)).astype(x.dtype)


def fox_attention(u, w_in, b_f, w_o):
    bsz, s, _ = u.shape
    h, dh = FOX_HEADS, FOX_HEAD_DIM
    proj = u @ w_in
    q, k, v, f = jnp.split(proj, [D_MODEL, 2 * D_MODEL, 3 * D_MODEL], axis=-1)
    q = q.reshape(bsz, s, h, dh).transpose(0, 2, 1, 3)
    k = k.reshape(bsz, s, h, dh).transpose(0, 2, 1, 3)
    v = v.reshape(bsz, s, h, dh).transpose(0, 2, 1, 3)
    log_f = jax.nn.log_sigmoid((f + b_f).astype(jnp.float32))
    cum = jnp.cumsum(log_f, axis=1).transpose(0, 2, 1)
    nb = s // Q_BLOCK
    q_blocks = q.reshape(bsz, h, nb, Q_BLOCK, dh).transpose(2, 0, 1, 3, 4)
    cum_blocks = cum.reshape(bsz, h, nb, Q_BLOCK).transpose(2, 0, 1, 3)
    pos_blocks = jnp.arange(s, dtype=jnp.int32).reshape(nb, Q_BLOCK)
    key_pos = jnp.arange(s, dtype=jnp.int32)
    scale = dh ** -0.5

    def block(args):
        qb, cb, pb = args
        logits = jnp.einsum('bhqd,bhkd->bhqk', qb, k).astype(jnp.float32) * scale
        logits = logits + cb[..., :, None] - cum[:, :, None, :]
        causal = pb[:, None] >= key_pos[None, :]
        logits = jnp.where(causal, logits, -jnp.inf)
        p = jax.nn.softmax(logits, axis=-1).astype(v.dtype)
        return jnp.einsum('bhqk,bhkd->bhqd', p, v)

    out = lax.map(block, (q_blocks, cum_blocks, pos_blocks))
    out = out.transpose(1, 0, 3, 2, 4).reshape(bsz, s, D_MODEL)
    return out @ w_o


def causal_depthwise_conv(x, w, b):
    kw, ch = w.shape
    y = lax.conv_general_dilated(x, w[:, None, :], window_strides=(1,), padding=[(kw - 1, 0)],
                                 dimension_numbers=('NWC', 'WIO', 'NWC'), feature_group_count=ch)
    return y + b


def ssd_mixer(u, w_in, conv_w, conv_b, dt_bias, a_log, d_skip, norm_w, w_out):
    bsz, s, _ = u.shape
    g, r, n, p, l = SSM_GROUPS, SSM_HEADS_PER_GROUP, SSM_STATE, SSM_HEAD_DIM, SSM_CHUNK
    f32 = jnp.float32
    proj = u @ w_in
    z, xbc, dt = jnp.split(proj, [SSM_D_INNER, SSM_D_INNER + SSM_CONV_DIM], axis=-1)
    xbc = jax.nn.silu(causal_depthwise_conv(xbc, conv_w, conv_b))
    xs, bm, cm = jnp.split(xbc, [SSM_D_INNER, SSM_D_INNER + g * n], axis=-1)
    nc = s // l
    xc = xs.astype(f32).reshape(bsz, nc, l, g, r, p)
    bc = bm.astype(f32).reshape(bsz, nc, l, g, n)
    cc = cm.astype(f32).reshape(bsz, nc, l, g, n)
    dt = jax.nn.softplus((dt + dt_bias).astype(f32))
    a = -jnp.exp(a_log.astype(f32))
    dt_c = dt.reshape(bsz, nc, l, g, r)
    a_cs = jnp.cumsum((dt_c * a.reshape(g, r)).transpose(0, 1, 3, 4, 2), axis=-1)
    xdt = xc * dt_c[..., None]
    cb = jnp.einsum('bclgn,bcsgn->bcgls', cc, bc)
    seg = a_cs[..., :, None] - a_cs[..., None, :]
    tril = jnp.tril(jnp.ones((l, l), dtype=bool))
    decay_in = jnp.exp(jnp.where(tril, seg, -jnp.inf))
    y_diag = jnp.einsum('bcgls,bcgrls,bcsgrp->bclgrp', cb, decay_in, xdt)
    decay_to_end = jnp.exp(a_cs[..., -1:] - a_cs)
    states = jnp.einsum('bclgn,bcgrl,bclgrp->bcgrpn', bc, decay_to_end, xdt)
    chunk_decay = jnp.exp(a_cs[..., -1])

    def step(hst, inp):
        s_c, d_c = inp
        return hst * d_c[..., None, None] + s_c, hst

    h0 = jnp.zeros_like(states[:, 0])
    _, prev = lax.scan(step, h0, (states.transpose(1, 0, 2, 3, 4, 5), chunk_decay.transpose(1, 0, 2, 3)))
    prev = prev.transpose(1, 0, 2, 3, 4, 5)
    y_off = jnp.einsum('bclgn,bcgrpn,bcgrl->bclgrp', cc, prev, jnp.exp(a_cs))
    y = y_diag + y_off + xc * d_skip.astype(f32).reshape(g, r)[:, :, None]
    y = y.reshape(bsz, s, SSM_D_INNER)
    yg = (y * jax.nn.silu(z.astype(f32))).reshape(bsz, s, g, SSM_D_INNER // g)
    yg = yg * lax.rsqrt(jnp.mean(jnp.square(yg), axis=-1, keepdims=True) + RMS_EPS)
    y = (yg.reshape(bsz, s, SSM_D_INNER) * norm_w.astype(f32)).astype(u.dtype)
    return y @ w_out


def sq_relu_mlp(u, w1, w2):
    return jnp.square(jax.nn.relu(u @ w1)) @ w2


def _fwd_setup_inputs(seed: int = 0) -> dict:
    key = jax.random.key(seed)
    ks = jax.random.split(key, 24)
    n_a = (DEPTH + 1) // 2
    n_b = DEPTH // 2
    beta = (8.0 * DEPTH) ** -0.25
    f32 = jnp.float32

    def nrm(k, shape, fan_in, s=1.0):
        return jax.random.normal(k, shape, f32) * (s * fan_in ** -0.5)

    def small(k, shape, s=0.02):
        return jax.random.normal(k, shape, f32) * s

    dt0 = jnp.exp(jax.random.uniform(ks[17], (n_b, SSM_HEADS), f32, np.log(1e-3), np.log(1e-1)))
    dt_bias = dt0 + jnp.log(-jnp.expm1(-dt0))
    a_log = jnp.log(jax.random.uniform(ks[18], (n_b, SSM_HEADS), f32, 1.0, 16.0))
    return {
        'x': jax.random.normal(ks[0], (BATCH, SEQ, D_MODEL), f32),
        'c': jax.random.normal(ks[1], (BATCH, D_MODEL), f32),
        'ada_w': nrm(ks[2], (DEPTH, D_MODEL, 6 * D_MODEL), D_MODEL, 0.1),
        'ada_b': small(ks[3], (DEPTH, 6 * D_MODEL), 0.01),
        'ln_mix_g': 1.0 + small(ks[4], (DEPTH, D_MODEL)),
        'ln_mix_b': small(ks[5], (DEPTH, D_MODEL)),
        'ln_mlp_g': 1.0 + small(ks[6], (DEPTH, D_MODEL)),
        'ln_mlp_b': small(ks[7], (DEPTH, D_MODEL)),
        'mlp_w1': nrm(ks[8], (DEPTH, D_MODEL, D_FF), D_MODEL),
        'mlp_w2': nrm(ks[9], (DEPTH, D_FF, D_MODEL), D_FF, beta),
        'fox_w_in': nrm(ks[10], (n_a, D_MODEL, 3 * D_MODEL + FOX_HEADS), D_MODEL),
        'fox_b_f': 2.0 + small(ks[11], (n_a, FOX_HEADS), 0.1),
        'fox_w_o': nrm(ks[12], (n_a, D_MODEL, D_MODEL), D_MODEL, beta),
        'ssm_w_in': nrm(ks[13], (n_b, D_MODEL, SSM_IN_DIM), D_MODEL),
        'ssm_conv_w': nrm(ks[14], (n_b, SSM_CONV, SSM_CONV_DIM), SSM_CONV),
        'ssm_conv_b': small(ks[15], (n_b, SSM_CONV_DIM)),
        'ssm_dt_bias': dt_bias,
        'ssm_a_log': a_log,
        'ssm_d': 1.0 + small(ks[16], (n_b, SSM_HEADS)),
        'ssm_norm_w': 1.0 + small(ks[19], (n_b, SSM_D_INNER)),
        'ssm_w_out': nrm(ks[20], (n_b, SSM_D_INNER, D_MODEL), SSM_D_INNER, beta),
    }


def _fwd_reference(x, c, ada_w, ada_b, ln_mix_g, ln_mix_b, ln_mlp_g, ln_mlp_b, mlp_w1, mlp_w2,
              fox_w_in, fox_b_f, fox_w_o, ssm_w_in, ssm_conv_w, ssm_conv_b, ssm_dt_bias,
              ssm_a_log, ssm_d, ssm_norm_w, ssm_w_out):
    alpha = (2.0 * DEPTH) ** 0.25
    cond = jax.nn.silu(c)
    for i in range(DEPTH):
        mod = (cond @ ada_w[i] + ada_b[i])[:, None, :]
        sh_a, sc_a, g_a, sh_m, sc_m, g_m = jnp.split(mod, 6, axis=-1)
        u = x * (1.0 + sc_a) + sh_a
        j = i // N_MIXERS
        if i % N_MIXERS == 0:
            y = fox_attention(u, fox_w_in[j], fox_b_f[j], fox_w_o[j])
        else:
            y = ssd_mixer(u, ssm_w_in[j], ssm_conv_w[j], ssm_conv_b[j], ssm_dt_bias[j],
                          ssm_a_log[j], ssm_d[j], ssm_norm_w[j], ssm_w_out[j])
        x = layer_norm(alpha * x + (1.0 + g_a) * y, ln_mix_g[i], ln_mix_b[i])
        u = x * (1.0 + sc_m) + sh_m
        y = sq_relu_mlp(u, mlp_w1[i], mlp_w2[i])
        x = layer_norm(alpha * x + (1.0 + g_m) * y, ln_mlp_g[i], ln_mlp_b[i])
    return x


import jax as _jax
import jax.numpy as _jnp

TWIN_FORMAT = 'train_step'
FWD_PARAMS = ['x', 'c', 'ada_w', 'ada_b', 'ln_mix_g', 'ln_mix_b', 'ln_mlp_g', 'ln_mlp_b', 'mlp_w1', 'mlp_w2', 'fox_w_in', 'fox_b_f', 'fox_w_o', 'ssm_w_in', 'ssm_conv_w', 'ssm_conv_b', 'ssm_dt_bias', 'ssm_a_log', 'ssm_d', 'ssm_norm_w', 'ssm_w_out']
TWIN_WEIGHTS = ['ada_w', 'ada_b', 'ln_mix_g', 'ln_mix_b', 'ln_mlp_g', 'ln_mlp_b', 'mlp_w1', 'mlp_w2', 'fox_w_in', 'fox_b_f', 'fox_w_o', 'ssm_w_in', 'ssm_conv_w', 'ssm_conv_b', 'ssm_dt_bias', 'ssm_a_log', 'ssm_d', 'ssm_norm_w', 'ssm_w_out']
TWIN_DIFF_INPUT = 'x'
TWIN_INPUTS = ['x', 'c', 'ada_w', 'ada_b', 'ln_mix_g', 'ln_mix_b', 'ln_mlp_g', 'ln_mlp_b', 'mlp_w1', 'mlp_w2', 'fox_w_in', 'fox_b_f', 'fox_w_o', 'ssm_w_in', 'ssm_conv_w', 'ssm_conv_b', 'ssm_dt_bias', 'ssm_a_log', 'ssm_d', 'ssm_norm_w', 'ssm_w_out', 'loss_target', 'm_ada_w', 'm_ada_b', 'm_ln_mix_g', 'm_ln_mix_b', 'm_ln_mlp_g', 'm_ln_mlp_b', 'm_mlp_w1', 'm_mlp_w2', 'm_fox_w_in', 'm_fox_b_f', 'm_fox_w_o', 'm_ssm_w_in', 'm_ssm_conv_w', 'm_ssm_conv_b', 'm_ssm_dt_bias', 'm_ssm_a_log', 'm_ssm_d', 'm_ssm_norm_w', 'm_ssm_w_out', 'v_ada_w', 'v_ada_b', 'v_ln_mix_g', 'v_ln_mix_b', 'v_ln_mlp_g', 'v_ln_mlp_b', 'v_mlp_w1', 'v_mlp_w2', 'v_fox_w_in', 'v_fox_b_f', 'v_fox_w_o', 'v_ssm_w_in', 'v_ssm_conv_w', 'v_ssm_conv_b', 'v_ssm_dt_bias', 'v_ssm_a_log', 'v_ssm_d', 'v_ssm_norm_w', 'v_ssm_w_out']
TWIN_OUTPUTS = ['loss', 'grad_x', 'grad_ada_w', 'grad_ada_b', 'grad_ln_mix_g', 'grad_ln_mix_b', 'grad_ln_mlp_g', 'grad_ln_mlp_b', 'grad_mlp_w1', 'grad_mlp_w2', 'grad_fox_w_in', 'grad_fox_b_f', 'grad_fox_w_o', 'grad_ssm_w_in', 'grad_ssm_conv_w', 'grad_ssm_conv_b', 'grad_ssm_dt_bias', 'grad_ssm_a_log', 'grad_ssm_d', 'grad_ssm_norm_w', 'grad_ssm_w_out', 'delta_ada_w', 'delta_ada_b', 'delta_ln_mix_g', 'delta_ln_mix_b', 'delta_ln_mlp_g', 'delta_ln_mlp_b', 'delta_mlp_w1', 'delta_mlp_w2', 'delta_fox_w_in', 'delta_fox_b_f', 'delta_fox_w_o', 'delta_ssm_w_in', 'delta_ssm_conv_w', 'delta_ssm_conv_b', 'delta_ssm_dt_bias', 'delta_ssm_a_log', 'delta_ssm_d', 'delta_ssm_norm_w', 'delta_ssm_w_out', 'new_m_ada_w', 'new_m_ada_b', 'new_m_ln_mix_g', 'new_m_ln_mix_b', 'new_m_ln_mlp_g', 'new_m_ln_mlp_b', 'new_m_mlp_w1', 'new_m_mlp_w2', 'new_m_fox_w_in', 'new_m_fox_b_f', 'new_m_fox_w_o', 'new_m_ssm_w_in', 'new_m_ssm_conv_w', 'new_m_ssm_conv_b', 'new_m_ssm_dt_bias', 'new_m_ssm_a_log', 'new_m_ssm_d', 'new_m_ssm_norm_w', 'new_m_ssm_w_out', 'new_v_ada_w', 'new_v_ada_b', 'new_v_ln_mix_g', 'new_v_ln_mix_b', 'new_v_ln_mlp_g', 'new_v_ln_mlp_b', 'new_v_mlp_w1', 'new_v_mlp_w2', 'new_v_fox_w_in', 'new_v_fox_b_f', 'new_v_fox_w_o', 'new_v_ssm_w_in', 'new_v_ssm_conv_w', 'new_v_ssm_conv_b', 'new_v_ssm_dt_bias', 'new_v_ssm_a_log', 'new_v_ssm_d', 'new_v_ssm_norm_w', 'new_v_ssm_w_out']
TWIN_LEAF_KINDS = {'loss': 'loss', 'grad_x': 'grad_x', 'grad_ada_w': 'grad_w', 'grad_ada_b': 'grad_w', 'grad_ln_mix_g': 'grad_w', 'grad_ln_mix_b': 'grad_w', 'grad_ln_mlp_g': 'grad_w', 'grad_ln_mlp_b': 'grad_w', 'grad_mlp_w1': 'grad_w', 'grad_mlp_w2': 'grad_w', 'grad_fox_w_in': 'grad_w', 'grad_fox_b_f': 'grad_w', 'grad_fox_w_o': 'grad_w', 'grad_ssm_w_in': 'grad_w', 'grad_ssm_conv_w': 'grad_w', 'grad_ssm_conv_b': 'grad_w', 'grad_ssm_dt_bias': 'grad_w', 'grad_ssm_a_log': 'grad_w', 'grad_ssm_d': 'grad_w', 'grad_ssm_norm_w': 'grad_w', 'grad_ssm_w_out': 'grad_w', 'delta_ada_w': 'delta_w', 'delta_ada_b': 'delta_w', 'delta_ln_mix_g': 'delta_w', 'delta_ln_mix_b': 'delta_w', 'delta_ln_mlp_g': 'delta_w', 'delta_ln_mlp_b': 'delta_w', 'delta_mlp_w1': 'delta_w', 'delta_mlp_w2': 'delta_w', 'delta_fox_w_in': 'delta_w', 'delta_fox_b_f': 'delta_w', 'delta_fox_w_o': 'delta_w', 'delta_ssm_w_in': 'delta_w', 'delta_ssm_conv_w': 'delta_w', 'delta_ssm_conv_b': 'delta_w', 'delta_ssm_dt_bias': 'delta_w', 'delta_ssm_a_log': 'delta_w', 'delta_ssm_d': 'delta_w', 'delta_ssm_norm_w': 'delta_w', 'delta_ssm_w_out': 'delta_w', 'new_m_ada_w': 'new_m', 'new_m_ada_b': 'new_m', 'new_m_ln_mix_g': 'new_m', 'new_m_ln_mix_b': 'new_m', 'new_m_ln_mlp_g': 'new_m', 'new_m_ln_mlp_b': 'new_m', 'new_m_mlp_w1': 'new_m', 'new_m_mlp_w2': 'new_m', 'new_m_fox_w_in': 'new_m', 'new_m_fox_b_f': 'new_m', 'new_m_fox_w_o': 'new_m', 'new_m_ssm_w_in': 'new_m', 'new_m_ssm_conv_w': 'new_m', 'new_m_ssm_conv_b': 'new_m', 'new_m_ssm_dt_bias': 'new_m', 'new_m_ssm_a_log': 'new_m', 'new_m_ssm_d': 'new_m', 'new_m_ssm_norm_w': 'new_m', 'new_m_ssm_w_out': 'new_m', 'new_v_ada_w': 'new_v', 'new_v_ada_b': 'new_v', 'new_v_ln_mix_g': 'new_v', 'new_v_ln_mix_b': 'new_v', 'new_v_ln_mlp_g': 'new_v', 'new_v_ln_mlp_b': 'new_v', 'new_v_mlp_w1': 'new_v', 'new_v_mlp_w2': 'new_v', 'new_v_fox_w_in': 'new_v', 'new_v_fox_b_f': 'new_v', 'new_v_fox_w_o': 'new_v', 'new_v_ssm_w_in': 'new_v', 'new_v_ssm_conv_w': 'new_v', 'new_v_ssm_conv_b': 'new_v', 'new_v_ssm_dt_bias': 'new_v', 'new_v_ssm_a_log': 'new_v', 'new_v_ssm_d': 'new_v', 'new_v_ssm_norm_w': 'new_v', 'new_v_ssm_w_out': 'new_v'}


def _forward(args):
    return _fwd_reference(*[args[k] for k in FWD_PARAMS])


def _output_shape():
    def fwd():
        inp = _fwd_setup_inputs(0)
        return _fwd_reference(*[inp[k] for k in FWD_PARAMS])
    out = _jax.eval_shape(fwd)
    return out.shape, out.dtype

N_MICROBATCH = 1
ADAM_LR = 0.001
ADAM_B1 = 0.9
ADAM_B2 = 0.999
ADAM_EPS = 1e-08
ADAM_WD = 0.01
ADAM_STEP = 10
PER_EXAMPLE_BATCH_AXIS = {'x': 0, 'c': 0, 'loss_target': 0}
SHARED_INPUTS = []
_WEIGHT_DTYPES = {'ada_w': _jnp.float32, 'ada_b': _jnp.float32, 'ln_mix_g': _jnp.float32, 'ln_mix_b': _jnp.float32, 'ln_mlp_g': _jnp.float32, 'ln_mlp_b': _jnp.float32, 'mlp_w1': _jnp.float32, 'mlp_w2': _jnp.float32, 'fox_w_in': _jnp.float32, 'fox_b_f': _jnp.float32, 'fox_w_o': _jnp.float32, 'ssm_w_in': _jnp.float32, 'ssm_conv_w': _jnp.float32, 'ssm_conv_b': _jnp.float32, 'ssm_dt_bias': _jnp.float32, 'ssm_a_log': _jnp.float32, 'ssm_d': _jnp.float32, 'ssm_norm_w': _jnp.float32, 'ssm_w_out': _jnp.float32}
MOMENT_SCALE = {'ada_w': 1.117075e-01, 'ada_b': 2.937756e-01, 'ln_mix_g': 2.968893e+00, 'ln_mix_b': 1.656973e+00, 'ln_mlp_g': 9.097002e+01, 'ln_mlp_b': 2.112006e+01, 'mlp_w1': 8.813181e-02, 'mlp_w2': 5.085335e-01, 'fox_w_in': 5.036582e-02, 'fox_b_f': 2.462956e-01, 'fox_w_o': 1.192920e-01, 'ssm_w_in': 7.475190e-02, 'ssm_conv_w': 6.607876e-02, 'ssm_conv_b': 1.146597e-01, 'ssm_dt_bias': 1.425299e-01, 'ssm_a_log': 2.157842e-01, 'ssm_d': 3.898535e-01, 'ssm_norm_w': 9.995516e-02, 'ssm_w_out': 2.903554e-01}


def _to_microbatches(a, axis):
    t = _jnp.moveaxis(a, axis, 0)
    t = t.reshape((N_MICROBATCH, t.shape[0] // N_MICROBATCH) + t.shape[1:])
    return _jnp.moveaxis(t, 1, axis + 1)


def setup_inputs(seed: int = 0) -> dict:
    inp = _fwd_setup_inputs(seed)
    key = _jax.random.fold_in(_jax.random.key(seed), 7919)
    shape, _ = _output_shape()
    out = dict(inp)
    out["loss_target"] = _jax.random.normal(_jax.random.fold_in(key, 0), shape, _jnp.float32)
    for i, name in enumerate(TWIN_WEIGHTS):
        w = inp[name].astype(_jnp.float32)
        if MOMENT_SCALE is None:
            s = _jnp.sqrt(_jnp.mean(_jnp.square(w)) + 1e-30)
        else:
            s = MOMENT_SCALE[name]
        km, kv = _jax.random.split(_jax.random.fold_in(key, i + 1))
        out[name] = w
        out["m_" + name] = s * _jax.random.normal(km, w.shape, _jnp.float32)
        out["v_" + name] = (s * s) * _jax.random.uniform(kv, w.shape, _jnp.float32, 0.5, 1.5)
    if N_MICROBATCH > 1:
        for name, axis in PER_EXAMPLE_BATCH_AXIS.items():
            out[name] = _to_microbatches(out[name], axis)
    return {'x': out['x'], 'c': out['c'], 'ada_w': out['ada_w'], 'ada_b': out['ada_b'], 'ln_mix_g': out['ln_mix_g'], 'ln_mix_b': out['ln_mix_b'], 'ln_mlp_g': out['ln_mlp_g'], 'ln_mlp_b': out['ln_mlp_b'], 'mlp_w1': out['mlp_w1'], 'mlp_w2': out['mlp_w2'], 'fox_w_in': out['fox_w_in'], 'fox_b_f': out['fox_b_f'], 'fox_w_o': out['fox_w_o'], 'ssm_w_in': out['ssm_w_in'], 'ssm_conv_w': out['ssm_conv_w'], 'ssm_conv_b': out['ssm_conv_b'], 'ssm_dt_bias': out['ssm_dt_bias'], 'ssm_a_log': out['ssm_a_log'], 'ssm_d': out['ssm_d'], 'ssm_norm_w': out['ssm_norm_w'], 'ssm_w_out': out['ssm_w_out'], 'loss_target': out['loss_target'], 'm_ada_w': out['m_ada_w'], 'm_ada_b': out['m_ada_b'], 'm_ln_mix_g': out['m_ln_mix_g'], 'm_ln_mix_b': out['m_ln_mix_b'], 'm_ln_mlp_g': out['m_ln_mlp_g'], 'm_ln_mlp_b': out['m_ln_mlp_b'], 'm_mlp_w1': out['m_mlp_w1'], 'm_mlp_w2': out['m_mlp_w2'], 'm_fox_w_in': out['m_fox_w_in'], 'm_fox_b_f': out['m_fox_b_f'], 'm_fox_w_o': out['m_fox_w_o'], 'm_ssm_w_in': out['m_ssm_w_in'], 'm_ssm_conv_w': out['m_ssm_conv_w'], 'm_ssm_conv_b': out['m_ssm_conv_b'], 'm_ssm_dt_bias': out['m_ssm_dt_bias'], 'm_ssm_a_log': out['m_ssm_a_log'], 'm_ssm_d': out['m_ssm_d'], 'm_ssm_norm_w': out['m_ssm_norm_w'], 'm_ssm_w_out': out['m_ssm_w_out'], 'v_ada_w': out['v_ada_w'], 'v_ada_b': out['v_ada_b'], 'v_ln_mix_g': out['v_ln_mix_g'], 'v_ln_mix_b': out['v_ln_mix_b'], 'v_ln_mlp_g': out['v_ln_mlp_g'], 'v_ln_mlp_b': out['v_ln_mlp_b'], 'v_mlp_w1': out['v_mlp_w1'], 'v_mlp_w2': out['v_mlp_w2'], 'v_fox_w_in': out['v_fox_w_in'], 'v_fox_b_f': out['v_fox_b_f'], 'v_fox_w_o': out['v_fox_w_o'], 'v_ssm_w_in': out['v_ssm_w_in'], 'v_ssm_conv_w': out['v_ssm_conv_w'], 'v_ssm_conv_b': out['v_ssm_conv_b'], 'v_ssm_dt_bias': out['v_ssm_dt_bias'], 'v_ssm_a_log': out['v_ssm_a_log'], 'v_ssm_d': out['v_ssm_d'], 'v_ssm_norm_w': out['v_ssm_norm_w'], 'v_ssm_w_out': out['v_ssm_w_out']}


def _loss(weights, diff, rest, loss_target):
    with _jax.named_scope("forward"):
        args = {**rest, TWIN_DIFF_INPUT: diff, **{k: w.astype(_WEIGHT_DTYPES[k]) for k, w in weights.items()}}
        y = _forward(args)
    with _jax.named_scope("loss_head"):
        err = _jnp.square(y.astype(_jnp.float32) - loss_target)
        return 0.5 * _jnp.sum(_jnp.mean(err, axis=-1)) if err.ndim else 0.5 * err


def _adamw(w, g, m, v):
    m = ADAM_B1 * m + (1.0 - ADAM_B1) * g
    v = ADAM_B2 * v + (1.0 - ADAM_B2) * _jnp.square(g)
    m_hat = m / (1.0 - ADAM_B1 ** ADAM_STEP)
    v_hat = v / (1.0 - ADAM_B2 ** ADAM_STEP)
    delta = -ADAM_LR * (m_hat / (_jnp.sqrt(v_hat) + ADAM_EPS) + ADAM_WD * w)
    return delta, m, v


def reference(x, c, ada_w, ada_b, ln_mix_g, ln_mix_b, ln_mlp_g, ln_mlp_b, mlp_w1, mlp_w2, fox_w_in, fox_b_f, fox_w_o, ssm_w_in, ssm_conv_w, ssm_conv_b, ssm_dt_bias, ssm_a_log, ssm_d, ssm_norm_w, ssm_w_out, loss_target, m_ada_w, m_ada_b, m_ln_mix_g, m_ln_mix_b, m_ln_mlp_g, m_ln_mlp_b, m_mlp_w1, m_mlp_w2, m_fox_w_in, m_fox_b_f, m_fox_w_o, m_ssm_w_in, m_ssm_conv_w, m_ssm_conv_b, m_ssm_dt_bias, m_ssm_a_log, m_ssm_d, m_ssm_norm_w, m_ssm_w_out, v_ada_w, v_ada_b, v_ln_mix_g, v_ln_mix_b, v_ln_mlp_g, v_ln_mlp_b, v_mlp_w1, v_mlp_w2, v_fox_w_in, v_fox_b_f, v_fox_w_o, v_ssm_w_in, v_ssm_conv_w, v_ssm_conv_b, v_ssm_dt_bias, v_ssm_a_log, v_ssm_d, v_ssm_norm_w, v_ssm_w_out):
    given = dict(x=x, c=c, ada_w=ada_w, ada_b=ada_b, ln_mix_g=ln_mix_g, ln_mix_b=ln_mix_b, ln_mlp_g=ln_mlp_g, ln_mlp_b=ln_mlp_b, mlp_w1=mlp_w1, mlp_w2=mlp_w2, fox_w_in=fox_w_in, fox_b_f=fox_b_f, fox_w_o=fox_w_o, ssm_w_in=ssm_w_in, ssm_conv_w=ssm_conv_w, ssm_conv_b=ssm_conv_b, ssm_dt_bias=ssm_dt_bias, ssm_a_log=ssm_a_log, ssm_d=ssm_d, ssm_norm_w=ssm_norm_w, ssm_w_out=ssm_w_out, loss_target=loss_target, m_ada_w=m_ada_w, m_ada_b=m_ada_b, m_ln_mix_g=m_ln_mix_g, m_ln_mix_b=m_ln_mix_b, m_ln_mlp_g=m_ln_mlp_g, m_ln_mlp_b=m_ln_mlp_b, m_mlp_w1=m_mlp_w1, m_mlp_w2=m_mlp_w2, m_fox_w_in=m_fox_w_in, m_fox_b_f=m_fox_b_f, m_fox_w_o=m_fox_w_o, m_ssm_w_in=m_ssm_w_in, m_ssm_conv_w=m_ssm_conv_w, m_ssm_conv_b=m_ssm_conv_b, m_ssm_dt_bias=m_ssm_dt_bias, m_ssm_a_log=m_ssm_a_log, m_ssm_d=m_ssm_d, m_ssm_norm_w=m_ssm_norm_w, m_ssm_w_out=m_ssm_w_out, v_ada_w=v_ada_w, v_ada_b=v_ada_b, v_ln_mix_g=v_ln_mix_g, v_ln_mix_b=v_ln_mix_b, v_ln_mlp_g=v_ln_mlp_g, v_ln_mlp_b=v_ln_mlp_b, v_mlp_w1=v_mlp_w1, v_mlp_w2=v_mlp_w2, v_fox_w_in=v_fox_w_in, v_fox_b_f=v_fox_b_f, v_fox_w_o=v_fox_w_o, v_ssm_w_in=v_ssm_w_in, v_ssm_conv_w=v_ssm_conv_w, v_ssm_conv_b=v_ssm_conv_b, v_ssm_dt_bias=v_ssm_dt_bias, v_ssm_a_log=v_ssm_a_log, v_ssm_d=v_ssm_d, v_ssm_norm_w=v_ssm_norm_w, v_ssm_w_out=v_ssm_w_out)
    weights = {n: given[n] for n in TWIN_WEIGHTS}
    shared = {n: given[n] for n in SHARED_INPUTS}
    per_example = {n: given[n] for n in ['x', 'c']}
    grad_fn = _jax.value_and_grad(_loss, argnums=(0, 1))

    def one_microbatch(ex, loss_target):
        ex = dict(ex)
        diff = ex.pop(TWIN_DIFF_INPUT)
        return grad_fn(weights, diff, {**shared, **ex}, loss_target)

    if N_MICROBATCH == 1:
        loss, (grad_w, grad_x) = one_microbatch(per_example, given["loss_target"])
    else:
        def body(carry, xs):
            loss_sum, grad_sum = carry
            l_k, (gw_k, gx_k) = one_microbatch(xs[0], xs[1])
            with _jax.named_scope("update"):
                return (loss_sum + l_k, _jax.tree.map(_jnp.add, grad_sum, gw_k)), gx_k

        init = (_jnp.zeros((), _jnp.float32), _jax.tree.map(_jnp.zeros_like, weights))
        (loss, grad_w), grad_x = _jax.lax.scan(body, init, (per_example, given["loss_target"]))
    with _jax.named_scope("update"):
        delta_w, new_m, new_v = {}, {}, {}
        for n in TWIN_WEIGHTS:
            delta_w[n], new_m[n], new_v[n] = _adamw(weights[n], grad_w[n], given["m_" + n], given["v_" + n])
    return (loss, grad_x, *[grad_w[n] for n in TWIN_WEIGHTS], *[delta_w[n] for n in TWIN_WEIGHTS],
            *[new_m[n] for n in TWIN_WEIGHTS], *[new_v[n] for n in TWIN_WEIGHTS])
```

```python
import numpy as np
import jax
import jax.numpy as jnp
from jax import lax
from jax.experimental import pallas as pl
from jax.experimental.pallas import tpu as pltpu

F32 = jnp.float32
MXU_DTYPE = jnp.bfloat16
HIGHEST = lax.Precision.HIGHEST

N_DEV = 8
D_MODEL = 1024
DEPTH = 2
FOX_HEADS = 16
FOX_HEAD_DIM = 64
D_FF = 4096
SSM_D_INNER = 2048
SSM_HEADS = 32
SSM_GROUPS = 8
SSM_STATE = 128
SSM_CHUNK = 128
SSM_CONV = 4
SSM_CONV_DIM = 4096
GROUP_W = SSM_D_INNER // SSM_GROUPS
LN_EPS = 1e-5
RMS_EPS = 1e-5
ALPHA = (2.0 * DEPTH) ** 0.25
LANES = 128
SUBLANES = 8

ADAM_LR = 0.001
ADAM_B1 = 0.9
ADAM_B2 = 0.999
ADAM_EPS = 1e-08
ADAM_WD = 0.01
ADAM_STEP = 10

NN = (((1,), (0,)), ((), ()))
NT = (((1,), (1,)), ((), ()))
TN = (((0,), (0,)), ((), ()))

VMEM_BIG = 56 * 1024 * 1024


def _dot(a, b, dims=NN, precision=None):
    return lax.dot_general(a, b, dims, precision=precision, preferred_element_type=F32)


def _mx(v):
    return v.astype(MXU_DTYPE)


def _params(vmem=None):
    return pltpu.CompilerParams(vmem_limit_bytes=vmem) if vmem else None


def _all_gather(x, name):
    def body(x_ref, out_ref, send_sems, recv_sems, local_sem):
        xi, yi, ci = lax.axis_index("x"), lax.axis_index("y"), lax.axis_index("c")
        me, sibling = (xi, yi, ci), (xi, yi, 1 - ci)
        chips = [(1 - xi, yi), (xi, 1 - yi), (1 - xi, 1 - yi)]

        def slot(px, py, pc):
            return out_ref.at[4 * px + 2 * py + pc]

        def copy(k, block, to, src=None):
            return pltpu.make_async_remote_copy(
                src_ref=slot(*block) if src is None else src, dst_ref=slot(*block),
                send_sem=send_sems.at[k], recv_sem=recv_sems.at[k],
                device_id=to, device_id_type=pl.DeviceIdType.MESH)

        mine = pltpu.make_async_copy(x_ref, slot(*me), local_sem)
        mine.start()
        first = [copy(0, me, sibling, src=x_ref)]
        first += [copy(1 + j, me, (*chip, ci), src=x_ref) for j, chip in enumerate(chips)]
        for cp in first:
            cp.start()
        passed = [copy(4 + j, (*chip, ci), sibling) for j, chip in enumerate(chips)]
        for j, chip in enumerate(chips):
            copy(1 + j, (*chip, ci), me).wait_recv()
            passed[j].start()
        copy(0, sibling, me).wait_recv()
        for j, chip in enumerate(chips):
            copy(4 + j, (*chip, 1 - ci), me).wait_recv()
        for cp in first + passed:
            cp.wait_send()
        mine.wait()

    return pl.pallas_call(
        body, name=name,
        out_shape=jax.ShapeDtypeStruct((N_DEV,) + x.shape, x.dtype),
        in_specs=[pl.BlockSpec(memory_space=pl.ANY)],
        out_specs=pl.BlockSpec(memory_space=pl.ANY),
        scratch_shapes=[pltpu.SemaphoreType.DMA((7,)), pltpu.SemaphoreType.DMA((7,)),
                        pltpu.SemaphoreType.DMA],
    )(x)


def _all_to_all(x, name):
    def body(x_ref, out_ref, send_sems, recv_sems, local_sem):
        xi, yi, ci = lax.axis_index("x"), lax.axis_index("y"), lax.axis_index("c")
        me = 4 * xi + 2 * yi + ci
        mine = pltpu.make_async_copy(x_ref.at[me], out_ref.at[me], local_sem)
        mine.start()
        copies = []
        for k in range(1, N_DEV):
            px = 1 - xi if k & 4 else xi
            py = 1 - yi if k & 2 else yi
            pc = 1 - ci if k & 1 else ci
            cp = pltpu.make_async_remote_copy(
                src_ref=x_ref.at[4 * px + 2 * py + pc], dst_ref=out_ref.at[me],
                send_sem=send_sems.at[k - 1], recv_sem=recv_sems.at[k - 1],
                device_id=(px, py, pc), device_id_type=pl.DeviceIdType.MESH)
            cp.start()
            copies.append(cp)
        for cp in copies:
            cp.wait()
        mine.wait()

    return pl.pallas_call(
        body, name=name,
        out_shape=jax.ShapeDtypeStruct(x.shape, x.dtype),
        in_specs=[pl.BlockSpec(memory_space=pl.ANY)],
        out_specs=pl.BlockSpec(memory_space=pl.ANY),
        scratch_shapes=[pltpu.SemaphoreType.DMA((7,)), pltpu.SemaphoreType.DMA((7,)),
                        pltpu.SemaphoreType.DMA],
    )(x)


def _mm(name, a, b, *, grid, a_spec, b_spec, dims, k_axis, outs, acc=None, extras=(), epi=None, vmem=None):
    nk = grid[k_axis]
    n_ex, n_out = len(extras), len(outs)

    def body(*refs):
        a_ref, b_ref = refs[0], refs[1]
        ex = refs[2:2 + n_ex]
        out = refs[2 + n_ex:2 + n_ex + n_out]

        def finish(val):
            if epi is None:
                out[0][...] = val.astype(out[0].dtype)
            else:
                epi(val, ex, out)

        part = _dot(a_ref[...], b_ref[...], dims)
        if nk == 1:
            finish(part)
        else:
            acc_ref = refs[2 + n_ex + n_out]
            k = pl.program_id(k_axis)

            @pl.when(k == 0)
            def _():
                acc_ref[...] = part

            @pl.when(k > 0)
            def _():
                acc_ref[...] += part

            @pl.when(k == nk - 1)
            def _():
                finish(acc_ref[...])

    return pl.pallas_call(
        body, name=name, grid=grid,
        in_specs=[a_spec, b_spec] + [s for _, s in extras],
        out_specs=[s for _, s in outs],
        out_shape=[o for o, _ in outs],
        scratch_shapes=[pltpu.VMEM(acc, F32)] if nk > 1 else [],
        compiler_params=_params(vmem),
    )(a, b, *[e for e, _ in extras])


def _tile(n, t):
    t = min(n, t)
    assert n % t == 0, (n, t)
    return t


def _mm_nn(name, a, b, out_dtype, *, addend=None, tm=1024, tn=1024, tk=1024, epi=None, extras=(), outs=None):
    m, kk = a.shape
    n = b.shape[1]
    tm, tn, tk = _tile(m, tm), _tile(n, tn), _tile(kk, tk)
    o_spec = pl.BlockSpec((tm, tn), lambda i, j, k: (i, j))
    if outs is None:
        outs = [(jax.ShapeDtypeStruct((m, n), out_dtype), o_spec)]
    extras = list(extras)
    if addend is not None:
        extras = [(addend, o_spec)] + extras

        def epi(val, ex, out):
            out[0][...] = (val + ex[0][...].astype(F32)).astype(out[0].dtype)

    res = _mm(name, a, b, grid=(m // tm, n // tn, kk // tk),
              a_spec=pl.BlockSpec((tm, tk), lambda i, j, k: (i, k)),
              b_spec=pl.BlockSpec((tk, tn), lambda i, j, k: (k, j)),
              dims=NN, k_axis=2, acc=(tm, tn), outs=outs, extras=extras, epi=epi, vmem=VMEM_BIG)
    return res[0] if len(res) == 1 else res


def _mm_tn(name, a, b, out_dtype=F32, *, tm=1024, tn=1024, tk=1024):
    kk, m = a.shape
    n = b.shape[1]
    tm, tn, tk = _tile(m, tm), _tile(n, tn), _tile(kk, tk)
    res = _mm(name, a, b, grid=(m // tm, n // tn, kk // tk),
              a_spec=pl.BlockSpec((tk, tm), lambda i, j, k: (k, i)),
              b_spec=pl.BlockSpec((tk, tn), lambda i, j, k: (k, j)),
              dims=TN, k_axis=2, acc=(tm, tn),
              outs=[(jax.ShapeDtypeStruct((m, n), out_dtype), pl.BlockSpec((tm, tn), lambda i, j, k: (i, j)))],
              vmem=VMEM_BIG)
    return res[0]


def _row_block(s):
    return _tile(s, 512)


def _modulate(x, pv, name):
    s, d = x.shape
    tb = _row_block(s)

    def body(x_ref, pv_ref, u_ref):
        u_ref[...] = _mx(x_ref[...] * pv_ref[0:1, :] + pv_ref[1:2, :])

    return pl.pallas_call(
        body, name=name, grid=(s // tb,),
        in_specs=[pl.BlockSpec((tb, d), lambda i: (i, 0)), pl.BlockSpec((8, d), lambda i: (0, 0))],
        out_specs=pl.BlockSpec((tb, d), lambda i: (i, 0)),
        out_shape=jax.ShapeDtypeStruct((s, d), MXU_DTYPE),
    )(x, pv)


def _ln_stats(r):
    mu = jnp.mean(r, axis=-1, keepdims=True)
    xc = r - mu
    var = jnp.mean(xc * xc, axis=-1, keepdims=True)
    rstd = lax.rsqrt(var + LN_EPS)
    return xc * rstd, rstd


def _ln_fwd(xin, y, pv, name):
    s, d = xin.shape
    tb = _row_block(s)

    def body(x_ref, y_ref, pv_ref, xo_ref, u_ref):
        r = ALPHA * x_ref[...] + pv_ref[0:1, :] * y_ref[...]
        xhat, _ = _ln_stats(r)
        xo = xhat * pv_ref[1:2, :] + pv_ref[2:3, :]
        xo_ref[...] = xo
        u_ref[...] = _mx(xo * pv_ref[3:4, :] + pv_ref[4:5, :])

    row = pl.BlockSpec((tb, d), lambda i: (i, 0))
    return pl.pallas_call(
        body, name=name, grid=(s // tb,),
        in_specs=[row, row, pl.BlockSpec((8, d), lambda i: (0, 0))],
        out_specs=[row, row],
        out_shape=[jax.ShapeDtypeStruct((s, d), F32), jax.ShapeDtypeStruct((s, d), MXU_DTYPE)],
    )(xin, y, pv)


def _ln_bwd(xin, y, pv, name, *, dxo=None, du=None, target=None):
    s, d = xin.shape
    tb = _row_block(s)
    nb = s // tb
    loss_mode = target is not None

    def body(*refs):
        if loss_mode:
            x_ref, y_ref, pv_ref, t_ref, dxin_ref, dy_ref, sums_ref = refs
        else:
            x_ref, y_ref, pv_ref, dxo_ref, du_ref, dxin_ref, dy_ref, sums_ref = refs
        i = pl.program_id(0)

        @pl.when(i == 0)
        def _():
            sums_ref[...] = jnp.zeros_like(sums_ref)

        yv = y_ref[...]
        r = ALPHA * x_ref[...] + pv_ref[0:1, :] * yv
        xhat, rstd = _ln_stats(r)
        xo = xhat * pv_ref[1:2, :] + pv_ref[2:3, :]
        if loss_mode:
            diff = xo - t_ref[...]
            dxo_v = diff * (1.0 / d)
            sums_ref[5:6, :] += jnp.sum(diff * diff, axis=0, keepdims=True) * (0.5 / d)
        else:
            duv = du_ref[...]
            dxo_v = dxo_ref[...] + duv * pv_ref[3:4, :]
            sums_ref[0:1, :] += jnp.sum(duv * xo, axis=0, keepdims=True)
            sums_ref[1:2, :] += jnp.sum(duv, axis=0, keepdims=True)
        sums_ref[2:3, :] += jnp.sum(dxo_v * xhat, axis=0, keepdims=True)
        sums_ref[3:4, :] += jnp.sum(dxo_v, axis=0, keepdims=True)
        dxh = dxo_v * pv_ref[1:2, :]
        dr = rstd * (dxh - jnp.mean(dxh, axis=-1, keepdims=True)
                     - xhat * jnp.mean(dxh * xhat, axis=-1, keepdims=True))
        sums_ref[4:5, :] += jnp.sum(dr * yv, axis=0, keepdims=True)
        dxin_ref[...] = ALPHA * dr
        dy_ref[...] = _mx(pv_ref[0:1, :] * dr)
        if loss_mode:
            @pl.when(i == nb - 1)
            def _():
                sums_ref[5:6, :] = jnp.broadcast_to(jnp.sum(sums_ref[5:6, :], axis=-1, keepdims=True), (1, d))

    row = pl.BlockSpec((tb, d), lambda i: (i, 0))
    par = pl.BlockSpec((8, d), lambda i: (0, 0))
    ins = [xin, y, pv] + ([target] if loss_mode else [dxo, du])
    return pl.pallas_call(
        body, name=name, grid=(nb,),
        in_specs=[row, row, par] + [row] * (len(ins) - 3),
        out_specs=[row, row, par],
        out_shape=[jax.ShapeDtypeStruct((s, d), F32), jax.ShapeDtypeStruct((s, d), MXU_DTYPE),
                   jax.ShapeDtypeStruct((8, d), F32)],
    )(*ins)


def _mod_bwd(dx_direct, du, x, pv, name):
    s, d = x.shape
    tb = _row_block(s)

    def body(dxd_ref, du_ref, x_ref, pv_ref, dx_ref, sums_ref):
        @pl.when(pl.program_id(0) == 0)
        def _():
            sums_ref[...] = jnp.zeros_like(sums_ref)

        duv = du_ref[...]
        dx_ref[...] = dxd_ref[...] + duv * pv_ref[0:1, :]
        sums_ref[0:1, :] += jnp.sum(duv * x_ref[...], axis=0, keepdims=True)
        sums_ref[1:2, :] += jnp.sum(duv, axis=0, keepdims=True)

    row = pl.BlockSpec((tb, d), lambda i: (i, 0))
    par = pl.BlockSpec((8, d), lambda i: (0, 0))
    return pl.pallas_call(
        body, name=name, grid=(s // tb,),
        in_specs=[row, row, row, par], out_specs=[row, par],
        out_shape=[jax.ShapeDtypeStruct((s, d), F32), jax.ShapeDtypeStruct((8, d), F32)],
    )(dx_direct, du, x, pv)


def _mlp_fwd(u, w1, w2, tag):
    s = u.shape[0]

    def epi(val, ex, out):
        out[0][...] = _mx(val)
        out[1][...] = _mx(jnp.square(jnp.maximum(val, 0.0)))

    tm, tn = _tile(s, 1024), 1024
    spec = pl.BlockSpec((tm, tn), lambda i, j, k: (i, j))
    shp = jax.ShapeDtypeStruct((s, D_FF), MXU_DTYPE)
    h, a = _mm_nn(f"mlp_up{tag}", u, w1, None, epi=epi, outs=[(shp, spec), (shp, spec)], tn=tn)
    y = _mm_nn(f"mlp_down{tag}", a, w2, F32)
    return y, (h, a)


def _mlp_bwd(dy, u, h, a, w1t, w2t, tag):
    s = u.shape[0]
    tm, tn = _tile(s, 1024), 1024
    spec = pl.BlockSpec((tm, tn), lambda i, j, k: (i, j))

    def epi(val, ex, out):
        out[0][...] = _mx(val * (2.0 * jnp.maximum(ex[0][...].astype(F32), 0.0)))

    dh = _mm_nn(f"mlp_dh{tag}", dy, w2t, None, epi=epi, extras=[(h, spec)],
                outs=[(jax.ShapeDtypeStruct((s, D_FF), MXU_DTYPE), spec)], tn=tn)
    du = _mm_nn(f"mlp_du{tag}", dh, w1t, F32)
    dw2 = _mm_tn(f"mlp_dw2{tag}", a, dy)
    dw1 = _mm_tn(f"mlp_dw1{tag}", u, dh)
    return du, dw1, dw2


FOX_T = 1024
BIAS_Q = (64, 65, 66)
BIAS_K = (67, 68, 69)


def _fox_constants():
    selq = np.zeros((FOX_HEADS, 512, LANES), np.float32)
    selk = np.zeros((FOX_HEADS, 512, LANES), np.float32)
    selv = np.zeros((2, LANES, LANES), np.float32)
    put = np.zeros((2, 2, LANES, LANES), np.float32)
    for h in range(FOX_HEADS):
        off = FOX_HEAD_DIM * (h % 2)
        for dd in range(FOX_HEAD_DIM):
            selq[h, off + dd, dd] = FOX_HEAD_DIM ** -0.5
            selk[h, off + dd, dd] = 1.0
        for piece in range(3):
            selq[h, LANES * (1 + piece) + h, BIAS_Q[piece]] = 1.0
            selk[h, LANES * (1 + piece) + h, BIAS_K[piece]] = -1.0
    for par in range(2):
        for dd in range(FOX_HEAD_DIM):
            selv[par, FOX_HEAD_DIM * par + dd, dd] = 1.0
            put[par, 0, dd, FOX_HEAD_DIM * par + dd] = FOX_HEAD_DIM ** -0.5
            put[par, 1, dd, FOX_HEAD_DIM * par + dd] = 1.0
    return selq, selk, selv, put


def _pairs(nb, by_key):
    if by_key:
        pr = [(i, j) for j in range(nb) for i in range(j, nb)]
    else:
        pr = [(i, j) for i in range(nb) for j in range(i + 1)]
    return (np.array([p[0] for p in pr], np.int32), np.array([p[1] for p in pr], np.int32))


def _fox_prep(qkv, f, bf):
    s = qkv.shape[0]
    t = _tile(s, FOX_T)
    nb = s // t
    selq, selk, selv, _ = _fox_constants()

    def body(q_ref, k_ref, v_ref, f_ref, bf_ref, selq_ref, selk_ref, selv_ref,
             qa_ref, qat_ref, ka_ref, kat_ref, va_ref, vat_ref, parts_ref, carry_ref):
        i, h = pl.program_id(0), pl.program_id(1)
        lane = lax.broadcasted_iota(jnp.int32, (1, LANES), 1)

        @pl.when(h == 0)
        def _():
            @pl.when(i == 0)
            def _():
                carry_ref[...] = jnp.zeros_like(carry_ref)

            lf = jnp.where(lane < FOX_HEADS, jax.nn.log_sigmoid(f_ref[...] + bf_ref[0:1, :]), 0.0)
            tri = (lax.broadcasted_iota(jnp.int32, (t, t), 0) >= lax.broadcasted_iota(jnp.int32, (t, t), 1)).astype(F32)
            cum = _dot(tri, lf, precision=HIGHEST) + carry_ref[0:1, :]
            carry_ref[0:1, :] = cum[t - 1:t, :]
            hi = _mx(cum)
            r1 = cum - hi.astype(F32)
            mid = _mx(r1)
            parts_ref[:, 0:LANES] = hi
            parts_ref[:, LANES:2 * LANES] = mid
            parts_ref[:, 2 * LANES:3 * LANES] = _mx(r1 - mid.astype(F32))

        parts = parts_ref[...]
        qa = _dot(jnp.concatenate([q_ref[...], parts], axis=1), selq_ref[...])
        qa = qa + jnp.where((lane >= BIAS_K[0]) & (lane <= BIAS_K[2]), 1.0, 0.0)
        ka = _dot(jnp.concatenate([k_ref[...], parts], axis=1), selk_ref[...])
        ka = ka + jnp.where((lane >= BIAS_Q[0]) & (lane <= BIAS_Q[2]), 1.0, 0.0)
        va = _dot(v_ref[...], selv_ref[...])
        qa_ref[...] = _mx(qa)
        qat_ref[...] = _mx(qa.T)
        ka_ref[...] = _mx(ka)
        kat_ref[...] = _mx(ka.T)
        va_ref[...] = _mx(va)
        vat_ref[...] = _mx(va.T)

    rows = jax.ShapeDtypeStruct((FOX_HEADS, nb, t, LANES), MXU_DTYPE)
    cols = jax.ShapeDtypeStruct((FOX_HEADS, nb, LANES, t), MXU_DTYPE)
    rspec = pl.BlockSpec((None, None, t, LANES), lambda i, h: (h, i, 0, 0))
    cspec = pl.BlockSpec((None, None, LANES, t), lambda i, h: (h, i, 0, 0))
    npair = FOX_HEADS // 2
    return pl.pallas_call(
        body, name="fox_prep", grid=(nb, FOX_HEADS),
        in_specs=[pl.BlockSpec((t, LANES), lambda i, h: (i, h // 2)),
                  pl.BlockSpec((t, LANES), lambda i, h: (i, npair + h // 2)),
                  pl.BlockSpec((t, LANES), lambda i, h: (i, 2 * npair + h // 2)),
                  pl.BlockSpec((t, LANES), lambda i, h: (i, 0)),
                  pl.BlockSpec((8, LANES), lambda i, h: (0, 0)),
                  pl.BlockSpec((None, 512, LANES), lambda i, h: (h, 0, 0)),
                  pl.BlockSpec((None, 512, LANES), lambda i, h: (h, 0, 0)),
                  pl.BlockSpec((None, LANES, LANES), lambda i, h: (h % 2, 0, 0))],
        out_specs=[rspec, cspec, rspec, cspec, rspec, cspec],
        out_shape=[rows, cols, rows, cols, rows, cols],
        scratch_shapes=[pltpu.VMEM((t, 3 * LANES), MXU_DTYPE), pltpu.VMEM((8, LANES), F32)],
        compiler_params=_params(VMEM_BIG),
    )(qkv, qkv, qkv, f, bf, _mx(jnp.asarray(selq)), _mx(jnp.asarray(selk)), _mx(jnp.asarray(selv)))


def _causal_allow(t, i, j):
    key = lax.broadcasted_iota(jnp.int32, (t, t), 0)
    qry = lax.broadcasted_iota(jnp.int32, (t, t), 1)
    return (key <= qry) | (j < i)


def _fox_attn_fwd(qat, ka, vat):
    heads, nb, _, t = qat.shape
    im, jm = _pairs(nb, by_key=False)

    def body(im_ref, jm_ref, qat_ref, ka_ref, vat_ref, ot_ref, lse_ref, acc_ref, m_ref, l_ref):
        p = pl.program_id(1)
        i, j = im_ref[p], jm_ref[p]

        @pl.when(j == 0)
        def _():
            m_ref[...] = jnp.full_like(m_ref, -jnp.inf)
            l_ref[...] = jnp.zeros_like(l_ref)
            acc_ref[...] = jnp.zeros_like(acc_ref)

        st = jnp.where(_causal_allow(t, i, j), _dot(ka_ref[...], qat_ref[...]), -jnp.inf)
        m_old = m_ref[...]
        m_new = jnp.maximum(m_old, jnp.max(st, axis=0, keepdims=True))
        scale = jnp.exp(m_old - m_new)
        pt = jnp.exp(st - m_new)
        l_ref[...] = scale * l_ref[...] + jnp.sum(pt, axis=0, keepdims=True)
        acc_ref[...] = acc_ref[...] * scale + _dot(vat_ref[...], _mx(pt))
        m_ref[...] = m_new

        @pl.when(j == i)
        def _():
            ot_ref[...] = _mx(acc_ref[...] / l_ref[...])
            lse_ref[...] = m_ref[...] + jnp.log(l_ref[...])

    qspec = pl.BlockSpec((None, None, LANES, t), lambda h, p, im, jm: (h, im[p], 0, 0))
    grid_spec = pltpu.PrefetchScalarGridSpec(
        num_scalar_prefetch=2, grid=(heads, len(im)),
        in_specs=[qspec,
                  pl.BlockSpec((None, None, t, LANES), lambda h, p, im, jm: (h, jm[p], 0, 0)),
                  pl.BlockSpec((None, None, LANES, t), lambda h, p, im, jm: (h, jm[p], 0, 0))],
        out_specs=[qspec, pl.BlockSpec((None, None, 1, t), lambda h, p, im, jm: (h, im[p], 0, 0))],
        scratch_shapes=[pltpu.VMEM((LANES, t), F32), pltpu.VMEM((1, t), F32), pltpu.VMEM((1, t), F32)])
    return pl.pallas_call(
        body, name="fox_attn_fwd", grid_spec=grid_spec,
        out_shape=[jax.ShapeDtypeStruct((heads, nb, LANES, t), MXU_DTYPE),
                   jax.ShapeDtypeStruct((heads, nb, 1, t), F32)],
        compiler_params=_params(VMEM_BIG),
    )(jnp.asarray(im), jnp.asarray(jm), qat, ka, vat)


def _fox_attn_bwd(qa, qat, ka, kat, va, ot, lse, do, dot_):
    heads, nb, t, _ = qa.shape
    im, jm = _pairs(nb, by_key=True)

    def body(im_ref, jm_ref, qa_ref, qat_ref, ka_ref, kat_ref, va_ref, ot_ref, lse_ref, do_ref, dot_ref,
             dqt_ref, dka_ref, dva_ref):
        p = pl.program_id(1)
        i, j = im_ref[p], jm_ref[p]

        @pl.when(p == 0)
        def _():
            dqt_ref[...] = jnp.zeros_like(dqt_ref)

        @pl.when(i == j)
        def _():
            dka_ref[...] = jnp.zeros_like(dka_ref)
            dva_ref[...] = jnp.zeros_like(dva_ref)

        st = jnp.where(_causal_allow(t, i, j), _dot(ka_ref[...], qat_ref[...]), -jnp.inf)
        pt = jnp.exp(st - lse_ref[...])
        dot_v = dot_ref[...]
        delta = jnp.sum(ot_ref[...].astype(F32) * dot_v.astype(F32), axis=0, keepdims=True)
        dst = pt * (_dot(va_ref[...], dot_v) - delta)
        dva_ref[...] += _dot(_mx(pt), do_ref[...])
        dsm = _mx(dst)
        dka_ref[...] += _dot(dsm, qa_ref[...])
        dqt_ref[i] += _dot(kat_ref[...], dsm)

    def at_q(shape):
        return pl.BlockSpec((None, None) + shape, lambda h, p, im, jm: (h, im[p], 0, 0))

    def at_k(shape):
        return pl.BlockSpec((None, None) + shape, lambda h, p, im, jm: (h, jm[p], 0, 0))

    grid_spec = pltpu.PrefetchScalarGridSpec(
        num_scalar_prefetch=2, grid=(heads, len(im)),
        in_specs=[at_q((t, LANES)), at_q((LANES, t)), at_k((t, LANES)), at_k((LANES, t)), at_k((t, LANES)),
                  at_q((LANES, t)), at_q((1, t)), at_q((t, LANES)), at_q((LANES, t))],
        out_specs=[pl.BlockSpec((None, nb, LANES, t), lambda h, p, im, jm: (h, 0, 0, 0)),
                   at_k((t, LANES)), at_k((t, LANES))])
    return pl.pallas_call(
        body, name="fox_attn_bwd", grid_spec=grid_spec,
        out_shape=[jax.ShapeDtypeStruct((heads, nb, LANES, t), F32),
                   jax.ShapeDtypeStruct((heads, nb, t, LANES), F32),
                   jax.ShapeDtypeStruct((heads, nb, t, LANES), F32)],
        compiler_params=_params(VMEM_BIG),
    )(jnp.asarray(im), jnp.asarray(jm), qa, qat, ka, kat, va, ot, lse, do, dot_)


def _fox_post(dqt, dka, dva, f, bf):
    heads, nb, t, _ = dka.shape
    s = nb * t
    _, _, _, put = _fox_constants()
    npair = heads // 2

    def body(dqt_ref, dka_ref, dva_ref, f_ref, bf_ref, put_ref, dq_ref, dk_ref, dv_ref, df_ref, sums_ref,
             dc_ref, carry_ref):
        i, h = pl.program_id(0), pl.program_id(1)

        @pl.when((i == 0) & (h == 0))
        def _():
            carry_ref[...] = jnp.zeros_like(carry_ref)
            sums_ref[...] = jnp.zeros_like(sums_ref)

        @pl.when(h == 0)
        def _():
            dc_ref[...] = jnp.zeros_like(dc_ref)

        dqt_v = dqt_ref[...]
        dka_v = dka_ref[...]
        term_q = _dot(_mx(dqt_v), put_ref[0], TN)
        term_k = _dot(_mx(dka_v), put_ref[1])
        term_v = _dot(_mx(dva_ref[...]), put_ref[1])

        @pl.when(h % 2 == 0)
        def _():
            dq_ref[...] = _mx(term_q)
            dk_ref[...] = _mx(term_k)
            dv_ref[...] = _mx(term_v)

        @pl.when(h % 2 == 1)
        def _():
            dq_ref[...] += _mx(term_q)
            dk_ref[...] += _mx(term_k)
            dv_ref[...] += _mx(term_v)

        dcum = dqt_v[BIAS_Q[0]:BIAS_Q[0] + 1, :] - dka_v.T[BIAS_K[0]:BIAS_K[0] + 1, :]
        head_row = lax.broadcasted_iota(jnp.int32, (heads, 1), 0) == h
        dc_ref[...] += jnp.where(head_row, dcum, 0.0)

        @pl.when(h == heads - 1)
        def _():
            later = (lax.broadcasted_iota(jnp.int32, (t, t), 0) >= lax.broadcasted_iota(jnp.int32, (t, t), 1)).astype(F32)
            dlf_t = _dot(dc_ref[...], later, precision=HIGHEST) + carry_ref[:, 0:1]
            carry_ref[...] = jnp.broadcast_to(dlf_t[:, 0:1], carry_ref.shape)
            dlf = jnp.concatenate([dlf_t, jnp.zeros((LANES - heads, t), F32)], axis=0).T
            lane = lax.broadcasted_iota(jnp.int32, (1, LANES), 1)
            df = jnp.where(lane < heads, dlf * jax.nn.sigmoid(-(f_ref[...] + bf_ref[0:1, :])), 0.0)
            df_ref[...] = _mx(df)
            sums_ref[0:1, :] += jnp.sum(df, axis=0, keepdims=True)

    rev = lambda i: nb - 1 - i
    pair_spec = pl.BlockSpec((t, LANES), lambda i, h: (rev(i), h // 2))
    blk = pl.BlockSpec((t, LANES), lambda i, h: (rev(i), 0))
    hd = jax.ShapeDtypeStruct((s, D_MODEL), MXU_DTYPE)
    return pl.pallas_call(
        body, name="fox_post", grid=(nb, heads),
        in_specs=[pl.BlockSpec((None, None, LANES, t), lambda i, h: (h, rev(i), 0, 0)),
                  pl.BlockSpec((None, None, t, LANES), lambda i, h: (h, rev(i), 0, 0)),
                  pl.BlockSpec((None, None, t, LANES), lambda i, h: (h, rev(i), 0, 0)),
                  blk, pl.BlockSpec((8, LANES), lambda i, h: (0, 0)),
                  pl.BlockSpec((None, 2, LANES, LANES), lambda i, h: (h % 2, 0, 0, 0))],
        out_specs=[pair_spec, pair_spec, pair_spec, blk, pl.BlockSpec((8, LANES), lambda i, h: (0, 0))],
        out_shape=[hd, hd, hd, jax.ShapeDtypeStruct((s, LANES), MXU_DTYPE), jax.ShapeDtypeStruct((8, LANES), F32)],
        scratch_shapes=[pltpu.VMEM((heads, t), F32), pltpu.VMEM((heads, LANES), F32)],
        compiler_params=_params(VMEM_BIG),
    )(dqt, dka, dva, f, bf, _mx(jnp.asarray(put)))


def _fox_weights(w_in, w_o):
    wqkv = w_in[:, :3 * D_MODEL]
    wf = jnp.pad(w_in[:, 3 * D_MODEL:], ((0, 0), (0, LANES - FOX_HEADS)))
    wo_heads = w_o.reshape(FOX_HEADS, FOX_HEAD_DIM, D_MODEL)
    wo_a = jnp.pad(wo_heads, ((0, 0), (0, LANES - FOX_HEAD_DIM), (0, 0)))
    return dict(wqkv=wqkv, wf=wf, wqkv_t=wqkv.T, wf_t=wf.T, wo_a=wo_a, wo_at=jnp.swapaxes(wo_a, 1, 2))


def _fox_fwd(u, w, bf):
    s = u.shape[0]
    qkv = _mm_nn("fox_qkv", u, w["wqkv"], MXU_DTYPE)
    f = _mm_nn("fox_f", u, w["wf"], F32)
    qa, qat, ka, kat, va, vat = _fox_prep(qkv, f, bf)
    ot, lse = _fox_attn_fwd(qat, ka, vat)
    heads, nb, _, t = ot.shape
    tn = 1024
    y = _mm("fox_out", ot, w["wo_a"], grid=(nb, D_MODEL // tn, heads),
            a_spec=pl.BlockSpec((None, None, LANES, t), lambda i, j, k: (k, i, 0, 0)),
            b_spec=pl.BlockSpec((None, LANES, tn), lambda i, j, k: (k, 0, j)),
            dims=TN, k_axis=2, acc=(t, tn),
            outs=[(jax.ShapeDtypeStruct((s, D_MODEL), F32), pl.BlockSpec((t, tn), lambda i, j, k: (i, j)))],
            vmem=VMEM_BIG)[0]
    return y, dict(f=f, qa=qa, qat=qat, ka=ka, kat=kat, va=va, ot=ot, lse=lse)


def _fox_bwd(dy, u, w, bf, res):
    heads, nb, t, _ = res["qa"].shape
    s = nb * t
    do = _mm("fox_do", dy, w["wo_at"], grid=(nb, heads, 1),
             a_spec=pl.BlockSpec((t, D_MODEL), lambda i, h, k: (i, 0)),
             b_spec=pl.BlockSpec((None, D_MODEL, LANES), lambda i, h, k: (h, 0, 0)),
             dims=NN, k_axis=2,
             outs=[(jax.ShapeDtypeStruct((heads, nb, t, LANES), MXU_DTYPE),
                    pl.BlockSpec((None, None, t, LANES), lambda i, h, k: (h, i, 0, 0)))])[0]
    dot_ = _mm("fox_dot", w["wo_a"], dy, grid=(nb, heads, 1),
               a_spec=pl.BlockSpec((None, LANES, D_MODEL), lambda i, h, k: (h, 0, 0)),
               b_spec=pl.BlockSpec((t, D_MODEL), lambda i, h, k: (i, 0)),
               dims=NT, k_axis=2,
               outs=[(jax.ShapeDtypeStruct((heads, nb, LANES, t), MXU_DTYPE),
                      pl.BlockSpec((None, None, LANES, t), lambda i, h, k: (h, i, 0, 0)))])[0]
    dwo_a = _mm("fox_dwo", res["ot"], dy, grid=(heads, 1, nb),
                a_spec=pl.BlockSpec((None, None, LANES, t), lambda h, j, k: (h, k, 0, 0)),
                b_spec=pl.BlockSpec((t, D_MODEL), lambda h, j, k: (k, 0)),
                dims=NN, k_axis=2, acc=(LANES, D_MODEL),
                outs=[(jax.ShapeDtypeStruct((heads, LANES, D_MODEL), F32),
                       pl.BlockSpec((None, LANES, D_MODEL), lambda h, j, k: (h, 0, 0)))])[0]
    dqt, dka, dva = _fox_attn_bwd(res["qa"], res["qat"], res["ka"], res["kat"], res["va"], res["ot"],
                                  res["lse"], do, dot_)
    dq, dk, dv, df, sums = _fox_post(dqt, dka, dva, res["f"], bf)
    wt = w["wqkv_t"]
    du = _mm_nn("fox_du_q", dq, wt[:D_MODEL], F32)
    du = _mm_nn("fox_du_k", dk, wt[D_MODEL:2 * D_MODEL], F32, addend=du)
    du = _mm_nn("fox_du_v", dv, wt[2 * D_MODEL:], F32, addend=du)
    du = _mm_nn("fox_du_f", df, w["wf_t"], F32, addend=du)
    dw_in = jnp.concatenate(
        [_mm_tn("fox_dw_q", u, dq), _mm_tn("fox_dw_k", u, dk), _mm_tn("fox_dw_v", u, dv),
         _mm_tn("fox_dw_f", u, df)[:, :FOX_HEADS]], axis=1)
    dw_o = dwo_a[:, :FOX_HEAD_DIM, :].reshape(D_MODEL, D_MODEL)
    return du, dw_in, dw_o, sums


def _dsilu(v):
    sg = jax.nn.sigmoid(v)
    return sg * (1.0 + v * (1.0 - sg))


def _conv_taps(scr_ref, w_ref, rows, base):
    acc = None
    for k in range(SSM_CONV):
        term = scr_ref[pl.ds(base - (SSM_CONV - 1) + k, rows), :] * w_ref[k:k + 1, :]
        acc = term if acc is None else acc + term
    return acc


def _conv_fwd(zx, cw, cb):
    s = zx.shape[0]
    tb = _tile(s, 512)
    half = SSM_CONV_DIM // 2
    hb = tb // SUBLANES

    def body(x_ref, halo_ref, w_ref, b_ref, o_ref, scr_ref):
        i = pl.program_id(0)
        scr_ref[pl.ds(0, SUBLANES), :] = jnp.where(i > 0, halo_ref[...], 0.0)
        scr_ref[pl.ds(SUBLANES, tb), :] = x_ref[...]
        o_ref[...] = jax.nn.silu(_conv_taps(scr_ref, w_ref, tb, SUBLANES) + b_ref[0:1, :])

    return pl.pallas_call(
        body, name="ssd_conv_fwd", grid=(s // tb, 2),
        in_specs=[pl.BlockSpec((tb, half), lambda i, j: (i, 1 + j)),
                  pl.BlockSpec((SUBLANES, half), lambda i, j: (jnp.maximum(i * hb - 1, 0), 1 + j)),
                  pl.BlockSpec((8, half), lambda i, j: (0, j)),
                  pl.BlockSpec((8, half), lambda i, j: (0, j))],
        out_specs=pl.BlockSpec((tb, half), lambda i, j: (i, j)),
        out_shape=jax.ShapeDtypeStruct((s, SSM_CONV_DIM), F32),
        scratch_shapes=[pltpu.VMEM((tb + SUBLANES, half), F32)],
    )(zx, zx, cw, cb)


def _conv_bwd_pre(zx, dxc, cw, cb):
    s = zx.shape[0]
    tb = _tile(s, 512)
    half = SSM_CONV_DIM // 2
    hb = tb // SUBLANES

    def body(x_ref, halo_ref, d_ref, w_ref, b_ref, o_ref, sums_ref, scr_ref):
        i = pl.program_id(1)

        @pl.when(i == 0)
        def _():
            sums_ref[...] = jnp.zeros_like(sums_ref)

        scr_ref[pl.ds(0, SUBLANES), :] = jnp.where(i > 0, halo_ref[...], 0.0)
        scr_ref[pl.ds(SUBLANES, tb), :] = x_ref[...]
        pre = _conv_taps(scr_ref, w_ref, tb, SUBLANES) + b_ref[0:1, :]
        dpre = d_ref[...] * _dsilu(pre)
        o_ref[...] = dpre
        for k in range(SSM_CONV):
            shifted = scr_ref[pl.ds(SUBLANES - (SSM_CONV - 1) + k, tb), :]
            sums_ref[k:k + 1, :] += jnp.sum(dpre * shifted, axis=0, keepdims=True)
        sums_ref[SSM_CONV:SSM_CONV + 1, :] += jnp.sum(dpre, axis=0, keepdims=True)

    return pl.pallas_call(
        body, name="ssd_conv_bwd_pre", grid=(2, s // tb),
        in_specs=[pl.BlockSpec((tb, half), lambda j, i: (i, 1 + j)),
                  pl.BlockSpec((SUBLANES, half), lambda j, i: (jnp.maximum(i * hb - 1, 0), 1 + j)),
                  pl.BlockSpec((tb, half), lambda j, i: (i, j)),
                  pl.BlockSpec((8, half), lambda j, i: (0, j)),
                  pl.BlockSpec((8, half), lambda j, i: (0, j))],
        out_specs=[pl.BlockSpec((tb, half), lambda j, i: (i, j)),
                   pl.BlockSpec((8, half), lambda j, i: (0, j))],
        out_shape=[jax.ShapeDtypeStruct((s, SSM_CONV_DIM), F32), jax.ShapeDtypeStruct((8, SSM_CONV_DIM), F32)],
        scratch_shapes=[pltpu.VMEM((tb + SUBLANES, half), F32)],
    )(zx, zx, dxc, cw, cb)


def _conv_bwd_x(dpre, cw):
    s = dpre.shape[0]
    tb = _tile(s, 512)
    hb = tb // SUBLANES
    nb = s // tb

    def body(d_ref, halo_ref, w_ref, o_ref, scr_ref):
        i = pl.program_id(0)
        scr_ref[pl.ds(0, tb), :] = d_ref[...]
        scr_ref[pl.ds(tb, SUBLANES), :] = jnp.where(i < nb - 1, halo_ref[...], 0.0)
        acc = None
        for k in range(SSM_CONV):
            term = scr_ref[pl.ds(SSM_CONV - 1 - k, tb), :] * w_ref[k:k + 1, :]
            acc = term if acc is None else acc + term
        o_ref[...] = _mx(acc)

    return pl.pallas_call(
        body, name="ssd_conv_bwd_x", grid=(nb,),
        in_specs=[pl.BlockSpec((tb, SSM_CONV_DIM), lambda i: (i, 0)),
                  pl.BlockSpec((SUBLANES, SSM_CONV_DIM), lambda i: (jnp.minimum((i + 1) * hb, s // SUBLANES - 1), 0)),
                  pl.BlockSpec((8, SSM_CONV_DIM), lambda i: (0, 0))],
        out_specs=pl.BlockSpec((tb, SSM_CONV_DIM), lambda i: (i, 0)),
        out_shape=jax.ShapeDtypeStruct((s, SSM_CONV_DIM), MXU_DTYPE),
        scratch_shapes=[pltpu.VMEM((tb + SUBLANES, SSM_CONV_DIM), F32)],
        compiler_params=_params(VMEM_BIG),
    )(dpre, dpre, cw)


def _expand_constants():
    ex = np.zeros((LANES, SSM_D_INNER), np.float32)
    for h in range(SSM_HEADS):
        ex[h, h * 64:(h + 1) * 64] = 1.0
    return ex, np.ascontiguousarray(ex.T)


def _ssd_common(dtr_ref, par_ref, ex_ref, xc_ref):
    lc = SSM_CHUNK
    lane = lax.broadcasted_iota(jnp.int32, (1, LANES), 1)
    is_head = lane < SSM_HEADS
    par = par_ref[...]
    pre = dtr_ref[...] + par[0:1, :]
    dt = jnp.where(is_head, jax.nn.softplus(pre), 0.0)
    a = jnp.where(is_head, -jnp.exp(par[1:2, :]), 0.0)
    tri_b = lax.broadcasted_iota(jnp.int32, (lc, lc), 0) >= lax.broadcasted_iota(jnp.int32, (lc, lc), 1)
    tri = tri_b.astype(F32)
    da = dt * a
    acs = _dot(tri, da, precision=HIGHEST)
    acs_t = _dot(da, tri, (((0,), (1,)), ((), ())), precision=HIGHEST)
    ex = ex_ref[...]
    dt_x = _dot(dt, ex, precision=HIGHEST)
    acs_x = _dot(acs, ex, precision=HIGHEST)
    d_x = _dot(par, ex, precision=HIGHEST)[2:3, :]
    last_x = acs_x[lc - 1:lc, :]
    xs = xc_ref[:, 0:SSM_D_INNER]
    return dict(pre=pre, dt=dt, a=a, tri_b=tri_b, tri=tri, acs=acs, acs_t=acs_t, dt_x=dt_x, d_x=d_x, xs=xs,
                xdt=xs * dt_x, e_x=jnp.exp(acs_x), dte_x=jnp.exp(last_x - acs_x), cd_x=jnp.exp(last_x),
                is_head=is_head)


def _decay_in(q, h):
    seg = q["acs"][:, h:h + 1] - q["acs_t"][h:h + 1, :]
    return jnp.exp(jnp.where(q["tri_b"], seg, -jnp.inf))


def _ssd_scan_fwd(xc, dtr, par):
    s = xc.shape[0]
    lc = SSM_CHUNK
    nc = s // lc
    ex, _ = _expand_constants()

    def body(xc_ref, dtr_ref, par_ref, ex_ref, y_ref, prev_ref, st_ref):
        @pl.when(pl.program_id(0) == 0)
        def _():
            st_ref[...] = jnp.zeros_like(st_ref)

        q = _ssd_common(dtr_ref, par_ref, ex_ref, xc_ref)
        lane = lax.broadcasted_iota(jnp.int32, (1, LANES), 1)
        for g in range(SSM_GROUPS):
            sl = slice(g * GROUP_W, (g + 1) * GROUP_W)
            bg = _mx(xc_ref[:, SSM_D_INNER + g * SSM_STATE:SSM_D_INNER + (g + 1) * SSM_STATE])
            cg = _mx(xc_ref[:, SSM_D_INNER + (SSM_GROUPS + g) * SSM_STATE:SSM_D_INNER + (SSM_GROUPS + g + 1) * SSM_STATE])
            gm = _dot(cg, bg, NT)
            prev = st_ref[g]
            prev_ref[g] = prev
            yoff = _dot(cg, _mx(prev)) * q["e_x"][:, sl]
            st_ref[g] = prev * q["cd_x"][:, sl] + _dot(bg, _mx(q["xdt"][:, sl] * q["dte_x"][:, sl]), TN)
            pairs = []
            for pr in range(2):
                xp = _mx(q["xdt"][:, g * GROUP_W + pr * LANES:g * GROUP_W + (pr + 1) * LANES])
                both = [_dot(_mx(gm * _decay_in(q, 4 * g + 2 * pr + r2)), xp) for r2 in range(2)]
                pairs.append(jnp.where(lane < 64, both[0], both[1]))
            y_ref[:, sl] = jnp.concatenate(pairs, axis=1) + yoff + q["xs"][:, sl] * q["d_x"][:, sl]

    return pl.pallas_call(
        body, name="ssd_scan_fwd", grid=(nc,),
        in_specs=[pl.BlockSpec((lc, SSM_CONV_DIM), lambda c: (c, 0)),
                  pl.BlockSpec((lc, LANES), lambda c: (c, 0)),
                  pl.BlockSpec((8, LANES), lambda c: (0, 0)),
                  pl.BlockSpec((LANES, SSM_D_INNER), lambda c: (0, 0))],
        out_specs=[pl.BlockSpec((lc, SSM_D_INNER), lambda c: (c, 0)),
                   pl.BlockSpec((None, SSM_GROUPS, SSM_STATE, GROUP_W), lambda c: (c, 0, 0, 0))],
        out_shape=[jax.ShapeDtypeStruct((s, SSM_D_INNER), F32),
                   jax.ShapeDtypeStruct((nc, SSM_GROUPS, SSM_STATE, GROUP_W), F32)],
        scratch_shapes=[pltpu.VMEM((SSM_GROUPS, SSM_STATE, GROUP_W), F32)],
        compiler_params=_params(VMEM_BIG),
    )(xc, dtr, par, jnp.asarray(ex))


def _ssd_scan_bwd(dy, xc, dtr, par, prev):
    s = xc.shape[0]
    lc = SSM_CHUNK
    nc = s // lc
    ex, ex_t = _expand_constants()

    def body(dy_ref, xc_ref, dtr_ref, par_ref, prev_ref, ex_ref, ext_ref, dxc_ref, ddtr_ref, sums_ref,
             gst_ref, tacs_ref, tdt_ref, tdd_ref):
        @pl.when(pl.program_id(0) == 0)
        def _():
            gst_ref[...] = jnp.zeros_like(gst_ref)
            sums_ref[...] = jnp.zeros_like(sums_ref)

        q = _ssd_common(dtr_ref, par_ref, ex_ref, xc_ref)
        lane = lax.broadcasted_iota(jnp.int32, (1, LANES), 1)
        row = lax.broadcasted_iota(jnp.int32, (lc, 1), 0)
        dacs_rows = jnp.zeros((lc, LANES), F32)
        dacs_cols_t = jnp.zeros((LANES, lc), F32)
        for g in range(SSM_GROUPS):
            sl = slice(g * GROUP_W, (g + 1) * GROUP_W)
            b_lo = SSM_D_INNER + g * SSM_STATE
            c_lo = SSM_D_INNER + (SSM_GROUPS + g) * SSM_STATE
            bg = _mx(xc_ref[:, b_lo:b_lo + SSM_STATE])
            cg = _mx(xc_ref[:, c_lo:c_lo + SSM_STATE])
            dyg = dy_ref[:, sl]
            xsg, xdtg = q["xs"][:, sl], q["xdt"][:, sl]
            eg, dteg, cdg = q["e_x"][:, sl], q["dte_x"][:, sl], q["cd_x"][:, sl]
            prevg = prev_ref[g]
            gs = gst_ref[g]
            prevm, gsm = _mx(prevg), _mx(gs)
            tdd_ref[:, sl] = dyg * xsg
            dxs = dyg * q["d_x"][:, sl]
            t_acs = dyg * _dot(cg, prevm) * eg
            dcp = _mx(dyg * eg)
            dc = _dot(dcp, prevm, NT)
            dprev = _dot(cg, dcp, TN)
            db = _dot(_mx(xdtg * dteg), gsm, NT)
            dx2 = _dot(bg, gsm)
            dxdt = dx2 * dteg
            ddte = dx2 * xdtg * dteg
            t_acs = t_acs - ddte
            last = (jnp.sum(ddte, axis=0, keepdims=True)
                    + jnp.sum(gs * prevg, axis=0, keepdims=True) * cdg)
            gm = _dot(cg, bg, NT)
            dgm = jnp.zeros((lc, lc), F32)
            pair_dx = []
            for pr in range(2):
                lo = g * GROUP_W + pr * LANES
                xp = _mx(q["xdt"][:, lo:lo + LANES])
                dyp = dy_ref[:, lo:lo + LANES]
                both = []
                for r2 in range(2):
                    h = 4 * g + 2 * pr + r2
                    mine = (lane >= 64 * r2) & (lane < 64 * (r2 + 1))
                    lm = _decay_in(q, h)
                    m = gm * lm
                    dm = _dot(_mx(jnp.where(mine, dyp, 0.0)), xp, NT)
                    dgm = dgm + dm * lm
                    w = dm * m
                    dacs_rows = dacs_rows + jnp.sum(w, axis=1, keepdims=True) * (lane == h).astype(F32)
                    head_row = (lax.broadcasted_iota(jnp.int32, (LANES, 1), 0) == h).astype(F32)
                    dacs_cols_t = dacs_cols_t + head_row * jnp.sum(w, axis=0, keepdims=True)
                    both.append(_dot(_mx(m), _mx(dyp), TN))
                pair_dx.append(jnp.where(lane < 64, both[0], both[1]))
            dxdt = dxdt + jnp.concatenate(pair_dx, axis=1)
            dgmm = _mx(dgm)
            dc = dc + _dot(dgmm, bg)
            db = db + _dot(dgmm, cg, TN)
            dxs = dxs + dxdt * q["dt_x"][:, sl]
            tdt_ref[:, sl] = dxdt * xsg
            tacs_ref[:, sl] = t_acs + jnp.where(row == lc - 1, last, 0.0)
            dxc_ref[:, sl] = dxs
            dxc_ref[:, b_lo:b_lo + SSM_STATE] = db
            dxc_ref[:, c_lo:c_lo + SSM_STATE] = dc
            gst_ref[g] = gs * cdg + dprev
        ext = ext_ref[...]
        dacs = _dot(tacs_ref[...], ext, precision=HIGHEST) + dacs_rows - dacs_cols_t.T
        dda = _dot(q["tri"], dacs, TN, precision=HIGHEST)
        ddt = dda * q["a"] + _dot(tdt_ref[...], ext, precision=HIGHEST)
        ddtr = jnp.where(q["is_head"], ddt * jax.nn.sigmoid(q["pre"]), 0.0)
        ddtr_ref[...] = _mx(ddtr)
        tdd = jnp.broadcast_to(jnp.sum(tdd_ref[...], axis=0, keepdims=True), (8, SSM_D_INNER))
        sums_ref[0:1, :] += jnp.sum(ddtr, axis=0, keepdims=True)
        sums_ref[1:2, :] += jnp.sum(dda * q["dt"], axis=0, keepdims=True) * q["a"]
        sums_ref[2:3, :] += _dot(tdd, ext, precision=HIGHEST)[0:1, :]

    rev = lambda c: nc - 1 - c
    wide = pltpu.VMEM((lc, SSM_D_INNER), F32)
    return pl.pallas_call(
        body, name="ssd_scan_bwd", grid=(nc,),
        in_specs=[pl.BlockSpec((lc, SSM_D_INNER), lambda c: (rev(c), 0)),
                  pl.BlockSpec((lc, SSM_CONV_DIM), lambda c: (rev(c), 0)),
                  pl.BlockSpec((lc, LANES), lambda c: (rev(c), 0)),
                  pl.BlockSpec((8, LANES), lambda c: (0, 0)),
                  pl.BlockSpec((None, SSM_GROUPS, SSM_STATE, GROUP_W), lambda c: (rev(c), 0, 0, 0)),
                  pl.BlockSpec((LANES, SSM_D_INNER), lambda c: (0, 0)),
                  pl.BlockSpec((SSM_D_INNER, LANES), lambda c: (0, 0))],
        out_specs=[pl.BlockSpec((lc, SSM_CONV_DIM), lambda c: (rev(c), 0)),
                   pl.BlockSpec((lc, LANES), lambda c: (rev(c), 0)),
                   pl.BlockSpec((8, LANES), lambda c: (0, 0))],
        out_shape=[jax.ShapeDtypeStruct((s, SSM_CONV_DIM), F32), jax.ShapeDtypeStruct((s, LANES), MXU_DTYPE),
                   jax.ShapeDtypeStruct((8, LANES), F32)],
        scratch_shapes=[pltpu.VMEM((SSM_GROUPS, SSM_STATE, GROUP_W), F32), wide, wide, wide],
        compiler_params=_params(VMEM_BIG),
    )(dy, xc, dtr, par, prev, jnp.asarray(ex), jnp.asarray(ex_t))


def _group_norm_parts(yv, zv):
    yg = yv * jax.nn.silu(zv)
    normed, rinvs = [], []
    for g in range(SSM_GROUPS):
        blk = yg[:, g * GROUP_W:(g + 1) * GROUP_W]
        rinv = lax.rsqrt(jnp.mean(blk * blk, axis=-1, keepdims=True) + RMS_EPS)
        normed.append(blk * rinv)
        rinvs.append(rinv)
    return normed, rinvs


def _gnorm_fwd(y, zx, nw):
    s = y.shape[0]
    tb = _tile(s, 512)

    def body(y_ref, z_ref, w_ref, o_ref):
        normed, _ = _group_norm_parts(y_ref[...], z_ref[...])
        for g in range(SSM_GROUPS):
            sl = slice(g * GROUP_W, (g + 1) * GROUP_W)
            o_ref[:, sl] = _mx(normed[g] * w_ref[0:1, sl])

    row = pl.BlockSpec((tb, SSM_D_INNER), lambda i: (i, 0))
    return pl.pallas_call(
        body, name="ssd_gnorm_fwd", grid=(s // tb,),
        in_specs=[row, row, pl.BlockSpec((8, SSM_D_INNER), lambda i: (0, 0))],
        out_specs=row, out_shape=jax.ShapeDtypeStruct((s, SSM_D_INNER), MXU_DTYPE),
    )(y, zx, nw)


def _gnorm_bwd(y, zx, nw, dyn):
    s = y.shape[0]
    tb = _tile(s, 512)

    def body(y_ref, z_ref, w_ref, d_ref, dy_ref, dz_ref, sums_ref):
        @pl.when(pl.program_id(0) == 0)
        def _():
            sums_ref[...] = jnp.zeros_like(sums_ref)

        yv, zv = y_ref[...], z_ref[...]
        normed, rinvs = _group_norm_parts(yv, zv)
        gate = jax.nn.silu(zv)
        dgate = _dsilu(zv)
        for g in range(SSM_GROUPS):
            sl = slice(g * GROUP_W, (g + 1) * GROUP_W)
            dv = d_ref[:, sl]
            n = normed[g]
            sums_ref[0:1, sl] += jnp.sum(dv * n, axis=0, keepdims=True)
            dn = dv * w_ref[0:1, sl]
            dyg = rinvs[g] * (dn - n * jnp.mean(dn * n, axis=-1, keepdims=True))
            dy_ref[:, sl] = dyg * gate[:, sl]
            dz_ref[:, sl] = _mx(dyg * yv[:, sl] * dgate[:, sl])

    row = pl.BlockSpec((tb, SSM_D_INNER), lambda i: (i, 0))
    par = pl.BlockSpec((8, SSM_D_INNER), lambda i: (0, 0))
    return pl.pallas_call(
        body, name="ssd_gnorm_bwd", grid=(s // tb,),
        in_specs=[row, row, par, row], out_specs=[row, row, par],
        out_shape=[jax.ShapeDtypeStruct((s, SSM_D_INNER), F32), jax.ShapeDtypeStruct((s, SSM_D_INNER), MXU_DTYPE),
                   jax.ShapeDtypeStruct((8, SSM_D_INNER), F32)],
    )(y, zx, nw, dyn)


def _rows8(v):
    v = v.reshape(1, -1)
    return jnp.pad(v, ((0, 7), (0, 0)))


def _ssd_weights(w_in, w_out):
    nzx = SSM_D_INNER + SSM_CONV_DIM
    wzx = w_in[:, :nzx]
    wdt = jnp.pad(w_in[:, nzx:], ((0, 0), (0, LANES - SSM_HEADS)))
    return dict(wzx=wzx, wdt=wdt, wzx_t=wzx.T, wdt_t=wdt.T, wout=w_out, wout_t=w_out.T)


def _ssd_fwd(u, w, cw, cb, par, nw):
    zx = _mm_nn("ssd_in_zx", u, w["wzx"], F32)
    dtr = _mm_nn("ssd_in_dt", u, w["wdt"], F32)
    xc = _conv_fwd(zx, cw, cb)
    y, prev = _ssd_scan_fwd(xc, dtr, par)
    yn = _gnorm_fwd(y, zx, nw)
    out = _mm_nn("ssd_out", yn, w["wout"], F32)
    return out, dict(zx=zx, dtr=dtr, xc=xc, y=y, prev=prev, yn=yn)


def _ssd_bwd(dy, u, w, cw, cb, par, nw, res):
    dyn = _mm_nn("ssd_dyn", dy, w["wout_t"], F32)
    dw_out = _mm_tn("ssd_dw_out", res["yn"], dy)
    dys, dz, nsum = _gnorm_bwd(res["y"], res["zx"], nw, dyn)
    dxc, ddtr, ssum = _ssd_scan_bwd(dys, res["xc"], res["dtr"], par, res["prev"])
    dpre, csum = _conv_bwd_pre(res["zx"], dxc, cw, cb)
    dxbc = _conv_bwd_x(dpre, cw)
    wt = w["wzx_t"]
    du = _mm_nn("ssd_du_z", dz, wt[:SSM_D_INNER], F32)
    du = _mm_nn("ssd_du_x", dxbc, wt[SSM_D_INNER:], F32, addend=du)
    du = _mm_nn("ssd_du_dt", ddtr, w["wdt_t"], F32, addend=du)
    dw_in = jnp.concatenate(
        [_mm_tn("ssd_dw_z", u, dz), _mm_tn("ssd_dw_x", u, dxbc), _mm_tn("ssd_dw_dt", u, ddtr)[:, :SSM_HEADS]], axis=1)
    small = dict(conv_w=csum[:SSM_CONV], conv_b=csum[SSM_CONV], dt_bias=ssum[0, :SSM_HEADS],
                 a_log=ssum[1, :SSM_HEADS], d=ssum[2, :SSM_HEADS], norm_w=nsum[0])
    return du, dw_in, dw_out, small


def _ada_fwd(c_all, ada_w, ada_b_mine):
    nl, _, ncol = ada_w.shape

    def body(c_ref, w_ref, b_ref, o_ref):
        cond = _mx(jax.nn.silu(c_ref[...]))
        for i in range(nl):
            o_ref[i] = _dot(cond, _mx(w_ref[i])) + b_ref[i:i + 1, :]

    return pl.pallas_call(
        body, name="ada_fwd", out_shape=jax.ShapeDtypeStruct((nl, 2 * N_DEV, ncol), F32),
        compiler_params=_params(VMEM_BIG),
    )(c_all, ada_w, ada_b_mine)


def _ada_bwd(c_all, dmod_cols):
    nl, _, ncol = dmod_cols.shape

    def body(c_ref, d_ref, o_ref):
        cond = _mx(jax.nn.silu(c_ref[...]))
        for i in range(nl):
            o_ref[i] = _dot(cond, _mx(d_ref[i]), TN)

    return pl.pallas_call(
        body, name="ada_bwd", out_shape=jax.ShapeDtypeStruct((nl, D_MODEL, ncol), F32),
        compiler_params=_params(VMEM_BIG),
    )(c_all, dmod_cols)


def _adamw(gslots, w, m, v, name):
    k, r, c = gslots.shape
    tr = _tile(r, 256) if r % 256 == 0 else r
    c1 = 1.0 - ADAM_B1 ** ADAM_STEP
    c2 = 1.0 - ADAM_B2 ** ADAM_STEP

    def body(g_ref, w_ref, m_ref, v_ref, go_ref, d_ref, mo_ref, vo_ref):
        g = g_ref[0]
        for slot in range(1, k):
            g = g + g_ref[slot]
        mn = ADAM_B1 * m_ref[...] + (1.0 - ADAM_B1) * g
        vn = ADAM_B2 * v_ref[...] + (1.0 - ADAM_B2) * jnp.square(g)
        go_ref[...] = g
        mo_ref[...] = mn
        vo_ref[...] = vn
        d_ref[...] = -ADAM_LR * ((mn / c1) / (jnp.sqrt(vn / c2) + ADAM_EPS) + ADAM_WD * w_ref[...])

    row = pl.BlockSpec((tr, c), lambda i: (i, 0))
    shp = jax.ShapeDtypeStruct((r, c), F32)
    return pl.pallas_call(
        body, name=name, grid=(r // tr,),
        in_specs=[pl.BlockSpec((k, tr, c), lambda i: (0, i, 0)), row, row, row],
        out_specs=[row, row, row, row], out_shape=[shp, shp, shp, shp],
        compiler_params=_params(VMEM_BIG),
    )(gslots, w, m, v)


def _adamw_any(gslots, w, m, v, name):
    shape = w.shape
    two_d = (-1, shape[-1])
    k = gslots.shape[0]
    outs = _adamw(gslots.reshape((k,) + w.reshape(two_d).shape), w.reshape(two_d), m.reshape(two_d),
                  v.reshape(two_d), name)
    return tuple(o.reshape(shape) for o in outs)


def _gather_cols(w, name, dtype=None):
    g = _all_gather(w.astype(dtype or MXU_DTYPE), name)
    g = jnp.moveaxis(g, 0, -2)
    return g.reshape(g.shape[:-2] + (g.shape[-2] * g.shape[-1],))


def _gather_rows(w, name):
    g = _all_gather(_mx(w), name)
    g = jnp.moveaxis(g, 0, -3)
    return g.reshape(g.shape[:-3] + (g.shape[-3] * g.shape[-2], g.shape[-1]))


def _scatter_cols(g, name):
    cs = g.shape[-1] // N_DEV
    parts = g.reshape(g.shape[:-1] + (N_DEV, cs))
    return _all_to_all(jnp.moveaxis(parts, -2, 0), name)


def _scatter_rows(g, name):
    rs = g.shape[-2] // N_DEV
    parts = g.reshape(g.shape[:-2] + (N_DEV, rs, g.shape[-1]))
    return _all_to_all(jnp.moveaxis(parts, -3, 0), name)


def kernel(x, c, ada_w, ada_b, ln_mix_g, ln_mix_b, ln_mlp_g, ln_mlp_b, mlp_w1, mlp_w2, fox_w_in, fox_b_f, fox_w_o, ssm_w_in, ssm_conv_w, ssm_conv_b, ssm_dt_bias, ssm_a_log, ssm_d, ssm_norm_w, ssm_w_out, loss_target, m_ada_w, m_ada_b, m_ln_mix_g, m_ln_mix_b, m_ln_mlp_g, m_ln_mlp_b, m_mlp_w1, m_mlp_w2, m_fox_w_in, m_fox_b_f, m_fox_w_o, m_ssm_w_in, m_ssm_conv_w, m_ssm_conv_b, m_ssm_dt_bias, m_ssm_a_log, m_ssm_d, m_ssm_norm_w, m_ssm_w_out, v_ada_w, v_ada_b, v_ln_mix_g, v_ln_mix_b, v_ln_mlp_g, v_ln_mlp_b, v_mlp_w1, v_mlp_w2, v_fox_w_in, v_fox_b_f, v_fox_w_o, v_ssm_w_in, v_ssm_conv_w, v_ssm_conv_b, v_ssm_dt_bias, v_ssm_a_log, v_ssm_d, v_ssm_norm_w, v_ssm_w_out):
    me = 4 * lax.axis_index("x") + 2 * lax.axis_index("y") + lax.axis_index("c")
    xs = x[0]
    target = loss_target[0]
    d = D_MODEL

    c_all = _all_gather(c, "gather_c").reshape(N_DEV, d)
    c_all = jnp.pad(c_all, ((0, N_DEV), (0, 0)))
    ncol = ada_w.shape[-1]
    ada_b_mine = lax.dynamic_slice_in_dim(ada_b, me * ncol, ncol, axis=1)
    mod_cols = _ada_fwd(c_all, ada_w, ada_b_mine)
    mod_all = _all_gather(mod_cols, "gather_mod")
    mod = lax.dynamic_index_in_dim(mod_all, me, axis=2, keepdims=False)
    mod = jnp.moveaxis(mod, 0, 1).reshape(DEPTH, 6, d)

    def pv_rows(*rows):
        return jnp.pad(jnp.stack(rows), ((0, 8 - len(rows)), (0, 0)))

    w1 = _gather_cols(mlp_w1, "gather_w1")
    w2 = _gather_rows(mlp_w2, "gather_w2")
    fw = _fox_weights(_gather_cols(fox_w_in, "gather_fox_in")[0], _gather_rows(fox_w_o, "gather_fox_o")[0])
    sw = _ssd_weights(_gather_cols(ssm_w_in, "gather_ssm_in")[0], _gather_rows(ssm_w_out, "gather_ssm_out")[0])
    conv_w = _gather_cols(ssm_conv_w, "gather_conv_w", F32)[0]
    small_vec = jnp.concatenate([ssm_conv_b[0], ssm_norm_w[0]]).reshape(1, -1)
    small_all = _all_gather(small_vec.astype(F32), "gather_conv_b").reshape(N_DEV, -1)
    conv_b = small_all[:, :SSM_CONV_DIM // N_DEV].reshape(-1)
    norm_w = small_all[:, SSM_CONV_DIM // N_DEV:].reshape(-1)
    cw8 = jnp.pad(conv_w, ((0, 8 - SSM_CONV), (0, 0)))
    cb8 = _rows8(conv_b)
    nw8 = _rows8(norm_w)
    bf8 = _rows8(jnp.pad(fox_b_f[0], (0, LANES - FOX_HEADS)))
    par8 = jnp.pad(jnp.stack([jnp.pad(p[0], (0, LANES - SSM_HEADS)) for p in (ssm_dt_bias, ssm_a_log, ssm_d)]),
                   ((0, 5), (0, 0)))

    sh_a, sc_a, g_a, sh_m, sc_m, g_m = [mod[:, k] for k in range(6)]
    u0 = _modulate(xs, pv_rows(1.0 + sc_a[0], sh_a[0]), "modulate0")
    y0, fres = _fox_fwd(u0, fw, bf8)
    pv0 = pv_rows(1.0 + g_a[0], ln_mix_g[0], ln_mix_b[0], 1.0 + sc_m[0], sh_m[0])
    x1, u1 = _ln_fwd(xs, y0, pv0, "ln_mix0")
    y1, (h0, a0) = _mlp_fwd(u1, w1[0], w2[0], "0")
    pv1 = pv_rows(1.0 + g_m[0], ln_mlp_g[0], ln_mlp_b[0], 1.0 + sc_a[1], sh_a[1])
    x2, u2 = _ln_fwd(x1, y1, pv1, "ln_mlp0")
    y2, sres = _ssd_fwd(u2, sw, cw8, cb8, par8, nw8)
    pv2 = pv_rows(1.0 + g_a[1], ln_mix_g[1], ln_mix_b[1], 1.0 + sc_m[1], sh_m[1])
    x3, u3 = _ln_fwd(x2, y2, pv2, "ln_mix1")
    y3, (h1, a1) = _mlp_fwd(u3, w1[1], w2[1], "1")
    pv3 = pv_rows(1.0 + g_m[1], ln_mlp_g[1], ln_mlp_b[1])

    dx3, dy3, s3 = _ln_bwd(x3, y3, pv3, "ln_mlp1_bwd", target=target)
    loss = lax.psum(s3[5, 0], ("x", "y", "c"))
    du3, dw1_1, dw2_1 = _mlp_bwd(dy3, u3, h1, a1, w1[1].T, w2[1].T, "1")
    dx2, dy2, s2 = _ln_bwd(x2, y2, pv2, "ln_mix1_bwd", dxo=dx3, du=du3)
    du2, d_ssm_in, d_ssm_out, ssm_small = _ssd_bwd(dy2, u2, sw, cw8, cb8, par8, nw8, sres)
    dx1, dy1, s1 = _ln_bwd(x1, y1, pv1, "ln_mlp0_bwd", dxo=dx2, du=du2)
    du1, dw1_0, dw2_0 = _mlp_bwd(dy1, u1, h0, a0, w1[0].T, w2[0].T, "0")
    dx0, dy0, s0 = _ln_bwd(xs, y0, pv0, "ln_mix0_bwd", dxo=dx1, du=du1)
    du0, d_fox_in, d_fox_o, fox_sums = _fox_bwd(dy0, u0, fw, bf8, fres)
    grad_x, sx = _mod_bwd(dx0, du0, xs, pv_rows(1.0 + sc_a[0], sh_a[0]), "modulate0_bwd")

    dmod = jnp.stack([
        jnp.stack([sx[1], sx[0], s0[4], s0[1], s0[0], s1[4]]),
        jnp.stack([s1[1], s1[0], s2[4], s2[1], s2[0], s3[4]]),
    ]).reshape(DEPTH, 6 * d)

    def pad_rows(v):
        v = v.reshape(-1, LANES) if v.size % LANES == 0 else jnp.pad(v.reshape(-1), (0, LANES - v.size)).reshape(1, LANES)
        return jnp.pad(v, ((0, (-v.shape[0]) % 8), (0, 0)))

    small_parts = [dmod, jnp.stack([s0[2], s2[2]]), jnp.stack([s0[3], s2[3]]), jnp.stack([s1[2], s3[2]]),
                   jnp.stack([s1[3], s3[3]]), fox_sums[0, :FOX_HEADS], ssm_small["dt_bias"], ssm_small["a_log"],
                   ssm_small["d"]]
    packed = [pad_rows(p) for p in small_parts]
    offsets = np.cumsum([0] + [p.shape[0] for p in packed])
    small_all_g = _all_gather(jnp.concatenate(packed, axis=0), "gather_small_grads")

    def unpack(idx, shape):
        n = int(np.prod(shape))
        blk = small_all_g[:, offsets[idx]:offsets[idx + 1]].reshape(N_DEV, -1)[:, :n]
        return blk.reshape((N_DEV,) + tuple(shape))

    dmod_all = unpack(0, (DEPTH, 6 * d))
    dmod_cols = lax.dynamic_slice_in_dim(dmod_all, me * ncol, ncol, axis=2)
    dmod_cols = jnp.pad(jnp.moveaxis(dmod_cols, 0, 1), ((0, 0), (0, N_DEV), (0, 0)))
    g_ada_w = _ada_bwd(c_all, dmod_cols)

    dw1 = jnp.stack([dw1_0, dw1_1])
    dw2 = jnp.stack([dw2_0, dw2_1])
    shards = dict(
        mlp_w1=_scatter_cols(dw1, "scatter_w1"), mlp_w2=_scatter_rows(dw2, "scatter_w2"),
        fox_w_in=_scatter_cols(d_fox_in[None], "scatter_fox_in"), fox_w_o=_scatter_rows(d_fox_o[None], "scatter_fox_o"),
        ssm_w_in=_scatter_cols(d_ssm_in[None], "scatter_ssm_in"), ssm_w_out=_scatter_rows(d_ssm_out[None], "scatter_ssm_out"),
        ssm_conv_w=_scatter_cols(ssm_small["conv_w"][None], "scatter_conv_w"),
        ssm_conv_b=_scatter_cols(ssm_small["conv_b"][None], "scatter_conv_b"),
        ssm_norm_w=_scatter_cols(ssm_small["norm_w"][None], "scatter_norm_w"),
        ada_w=g_ada_w[None], ada_b=dmod_all,
        ln_mix_g=unpack(1, (DEPTH, d)), ln_mix_b=unpack(2, (DEPTH, d)),
        ln_mlp_g=unpack(3, (DEPTH, d)), ln_mlp_b=unpack(4, (DEPTH, d)),
        fox_b_f=unpack(5, (1, FOX_HEADS)), ssm_dt_bias=unpack(6, (1, SSM_HEADS)),
        ssm_a_log=unpack(7, (1, SSM_HEADS)), ssm_d=unpack(8, (1, SSM_HEADS)),
    )
    weights = dict(ada_w=ada_w, ada_b=ada_b, ln_mix_g=ln_mix_g, ln_mix_b=ln_mix_b, ln_mlp_g=ln_mlp_g, ln_mlp_b=ln_mlp_b,
                   mlp_w1=mlp_w1, mlp_w2=mlp_w2, fox_w_in=fox_w_in, fox_b_f=fox_b_f, fox_w_o=fox_w_o, ssm_w_in=ssm_w_in,
                   ssm_conv_w=ssm_conv_w, ssm_conv_b=ssm_conv_b, ssm_dt_bias=ssm_dt_bias, ssm_a_log=ssm_a_log,
                   ssm_d=ssm_d, ssm_norm_w=ssm_norm_w, ssm_w_out=ssm_w_out)
    mom1 = dict(ada_w=m_ada_w, ada_b=m_ada_b, ln_mix_g=m_ln_mix_g, ln_mix_b=m_ln_mix_b, ln_mlp_g=m_ln_mlp_g,
                ln_mlp_b=m_ln_mlp_b, mlp_w1=m_mlp_w1, mlp_w2=m_mlp_w2, fox_w_in=m_fox_w_in, fox_b_f=m_fox_b_f,
                fox_w_o=m_fox_w_o, ssm_w_in=m_ssm_w_in, ssm_conv_w=m_ssm_conv_w, ssm_conv_b=m_ssm_conv_b,
                ssm_dt_bias=m_ssm_dt_bias, ssm_a_log=m_ssm_a_log, ssm_d=m_ssm_d, ssm_norm_w=m_ssm_norm_w,
                ssm_w_out=m_ssm_w_out)
    mom2 = dict(ada_w=v_ada_w, ada_b=v_ada_b, ln_mix_g=v_ln_mix_g, ln_mix_b=v_ln_mix_b, ln_mlp_g=v_ln_mlp_g,
                ln_mlp_b=v_ln_mlp_b, mlp_w1=v_mlp_w1, mlp_w2=v_mlp_w2, fox_w_in=v_fox_w_in, fox_b_f=v_fox_b_f,
                fox_w_o=v_fox_w_o, ssm_w_in=v_ssm_w_in, ssm_conv_w=v_ssm_conv_w, ssm_conv_b=v_ssm_conv_b,
                ssm_dt_bias=v_ssm_dt_bias, ssm_a_log=v_ssm_a_log, ssm_d=v_ssm_d, ssm_norm_w=v_ssm_norm_w,
                ssm_w_out=v_ssm_w_out)
    names = list(weights)
    stepped = {n: _adamw_any(shards[n], weights[n], mom1[n], mom2[n], f"adamw_{n}") for n in names}
    return (loss, grad_x[None], *[stepped[n][0] for n in names], *[stepped[n][1] for n in names],
            *[stepped[n][2] for n in names], *[stepped[n][3] for n in names])
```

```python
import numpy as np
import jax
import jax.numpy as jnp
from jax import lax
from jax.experimental import pallas as pl
from jax.experimental.pallas import tpu as pltpu

F32 = jnp.float32
MXU_DTYPE = jnp.bfloat16
HIGHEST = lax.Precision.HIGHEST

N_DEV = 8
D_MODEL = 1024
DEPTH = 2
FOX_HEADS = 16
FOX_HEAD_DIM = 64
D_FF = 4096
SSM_D_INNER = 2048
SSM_HEADS = 32
SSM_GROUPS = 8
SSM_STATE = 128
SSM_CHUNK = 128
SSM_CONV = 4
SSM_CONV_DIM = 4096
GROUP_W = SSM_D_INNER // SSM_GROUPS
LN_EPS = 1e-5
RMS_EPS = 1e-5
ALPHA = (2.0 * DEPTH) ** 0.25
LANES = 128
SUBLANES = 8

ADAM_LR = 0.001
ADAM_B1 = 0.9
ADAM_B2 = 0.999
ADAM_EPS = 1e-08
ADAM_WD = 0.01
ADAM_STEP = 10

NN = (((1,), (0,)), ((), ()))
NT = (((1,), (1,)), ((), ()))
TN = (((0,), (0,)), ((), ()))

VMEM_BIG = 56 * 1024 * 1024


def _dot(a, b, dims=NN, precision=None):
    return lax.dot_general(a, b, dims, precision=precision, preferred_element_type=F32)


def _mx(v):
    return v.astype(MXU_DTYPE)


def _params(vmem=None):
    return pltpu.CompilerParams(vmem_limit_bytes=vmem) if vmem else None


def _all_gather(x, name):
    def body(x_ref, out_ref, send_sems, recv_sems, local_sem):
        xi, yi, ci = lax.axis_index("x"), lax.axis_index("y"), lax.axis_index("c")
        me, sibling = (xi, yi, ci), (xi, yi, 1 - ci)
        chips = [(1 - xi, yi), (xi, 1 - yi), (1 - xi, 1 - yi)]

        def slot(px, py, pc):
            return out_ref.at[4 * px + 2 * py + pc]

        def copy(k, block, to, src=None):
            return pltpu.make_async_remote_copy(
                src_ref=slot(*block) if src is None else src, dst_ref=slot(*block),
                send_sem=send_sems.at[k], recv_sem=recv_sems.at[k],
                device_id=to, device_id_type=pl.DeviceIdType.MESH)

        mine = pltpu.make_async_copy(x_ref, slot(*me), local_sem)
        mine.start()
        first = [copy(0, me, sibling, src=x_ref)]
        first += [copy(1 + j, me, (*chip, ci), src=x_ref) for j, chip in enumerate(chips)]
        for cp in first:
            cp.start()
        passed = [copy(4 + j, (*chip, ci), sibling) for j, chip in enumerate(chips)]
        for j, chip in enumerate(chips):
            copy(1 + j, (*chip, ci), me).wait_recv()
            passed[j].start()
        copy(0, sibling, me).wait_recv()
        for j, chip in enumerate(chips):
            copy(4 + j, (*chip, 1 - ci), me).wait_recv()
        for cp in first + passed:
            cp.wait_send()
        mine.wait()

    return pl.pallas_call(
        body, name=name,
        out_shape=jax.ShapeDtypeStruct((N_DEV,) + x.shape, x.dtype),
        in_specs=[pl.BlockSpec(memory_space=pl.ANY)],
        out_specs=pl.BlockSpec(memory_space=pl.ANY),
        scratch_shapes=[pltpu.SemaphoreType.DMA((7,)), pltpu.SemaphoreType.DMA((7,)),
                        pltpu.SemaphoreType.DMA],
    )(x)


def _direct_copies(scatter, x_ref, out_ref, send_sems, recv_sems, local_sems, n):
    xi, yi, ci = lax.axis_index("x"), lax.axis_index("y"), lax.axis_index("c")
    me = 4 * xi + 2 * yi + ci
    local = pltpu.make_async_copy(x_ref.at[me] if scatter else x_ref, out_ref.at[me], local_sems.at[n])
    remote = []
    for k in range(1, N_DEV):
        px = 1 - xi if k & 4 else xi
        py = 1 - yi if k & 2 else yi
        pc = 1 - ci if k & 1 else ci
        remote.append(pltpu.make_async_remote_copy(
            src_ref=x_ref.at[4 * px + 2 * py + pc] if scatter else x_ref, dst_ref=out_ref.at[me],
            send_sem=send_sems.at[7 * n + k - 1], recv_sem=recv_sems.at[7 * n + k - 1],
            device_id=(px, py, pc), device_id_type=pl.DeviceIdType.MESH))
    return local, remote


def _exchange_shapes(carry):
    return [jax.ShapeDtypeStruct(a.shape if sc else (N_DEV,) + a.shape, a.dtype) for sc, a in carry]


def _exchange_sems(carry):
    n = max(len(carry), 1)
    return [pltpu.SemaphoreType.DMA((7 * n,)), pltpu.SemaphoreType.DMA((7 * n,)), pltpu.SemaphoreType.DMA((n,))]


def _exchange_start(carry, srcs, dsts, sems):
    for n, (sc, _) in enumerate(carry):
        local, remote = _direct_copies(sc, srcs[n], dsts[n], *sems, n)
        local.start()
        for cp in remote:
            cp.start()


def _exchange_wait(carry, srcs, dsts, sems):
    for n, (sc, _) in enumerate(carry):
        local, remote = _direct_copies(sc, srcs[n], dsts[n], *sems, n)
        for cp in remote:
            cp.wait()
        local.wait()


def _all_to_all(x, name):
    carry = [(True, x)]

    def body(x_ref, out_ref, *sems):
        _exchange_start(carry, [x_ref], [out_ref], sems)
        _exchange_wait(carry, [x_ref], [out_ref], sems)

    return pl.pallas_call(
        body, name=name,
        out_shape=jax.ShapeDtypeStruct(x.shape, x.dtype),
        in_specs=[pl.BlockSpec(memory_space=pl.ANY)],
        out_specs=pl.BlockSpec(memory_space=pl.ANY),
        scratch_shapes=_exchange_sems(carry),
    )(x)


def _mm(name, a, b, *, grid, a_spec, b_spec, dims, k_axis, outs, acc=None, extras=(), epi=None, vmem=None):
    nk = grid[k_axis]
    n_ex, n_out = len(extras), len(outs)

    def body(*refs):
        a_ref, b_ref = refs[0], refs[1]
        ex = refs[2:2 + n_ex]
        out = refs[2 + n_ex:2 + n_ex + n_out]

        def finish(val):
            if epi is None:
                out[0][...] = val.astype(out[0].dtype)
            else:
                epi(val, ex, out)

        part = _dot(a_ref[...], b_ref[...], dims)
        if nk == 1:
            finish(part)
        else:
            acc_ref = refs[2 + n_ex + n_out]
            k = pl.program_id(k_axis)

            @pl.when(k == 0)
            def _():
                acc_ref[...] = part

            @pl.when(k > 0)
            def _():
                acc_ref[...] += part

            @pl.when(k == nk - 1)
            def _():
                finish(acc_ref[...])

    return pl.pallas_call(
        body, name=name, grid=grid,
        in_specs=[a_spec, b_spec] + [s for _, s in extras],
        out_specs=[s for _, s in outs],
        out_shape=[o for o, _ in outs],
        scratch_shapes=[pltpu.VMEM(acc, F32)] if nk > 1 else [],
        compiler_params=_params(vmem),
    )(a, b, *[e for e, _ in extras])


def _tile(n, t):
    t = min(n, t)
    assert n % t == 0, (n, t)
    return t


def _mm_nn(name, a, b, out_dtype, *, addend=None, tm=1024, tn=1024, tk=1024, epi=None, extras=(), outs=None):
    m, kk = a.shape
    n = b.shape[1]
    tm, tn, tk = _tile(m, tm), _tile(n, tn), _tile(kk, tk)
    o_spec = pl.BlockSpec((tm, tn), lambda i, j, k: (i, j))
    if outs is None:
        outs = [(jax.ShapeDtypeStruct((m, n), out_dtype), o_spec)]
    extras = list(extras)
    if addend is not None:
        extras = [(addend, o_spec)] + extras

        def epi(val, ex, out):
            out[0][...] = (val + ex[0][...].astype(F32)).astype(out[0].dtype)

    res = _mm(name, a, b, grid=(m // tm, n // tn, kk // tk),
              a_spec=pl.BlockSpec((tm, tk), lambda i, j, k: (i, k)),
              b_spec=pl.BlockSpec((tk, tn), lambda i, j, k: (k, j)),
              dims=NN, k_axis=2, acc=(tm, tn), outs=outs, extras=extras, epi=epi, vmem=VMEM_BIG)
    return res[0] if len(res) == 1 else res


def _mm_tn(name, a, b, out_dtype=F32, *, tm=1024, tn=1024, tk=1024):
    kk, m = a.shape
    n = b.shape[1]
    tm, tn, tk = _tile(m, tm), _tile(n, tn), _tile(kk, tk)
    res = _mm(name, a, b, grid=(m // tm, n // tn, kk // tk),
              a_spec=pl.BlockSpec((tk, tm), lambda i, j, k: (k, i)),
              b_spec=pl.BlockSpec((tk, tn), lambda i, j, k: (k, j)),
              dims=TN, k_axis=2, acc=(tm, tn),
              outs=[(jax.ShapeDtypeStruct((m, n), out_dtype), pl.BlockSpec((tm, tn), lambda i, j, k: (i, j)))],
              vmem=VMEM_BIG)
    return res[0]


def _row_block(s):
    return _tile(s, 512)


def _modulate(x, pv, name):
    s, d = x.shape
    tb = _row_block(s)

    def body(x_ref, pv_ref, u_ref):
        u_ref[...] = _mx(x_ref[...] * pv_ref[0:1, :] + pv_ref[1:2, :])

    return pl.pallas_call(
        body, name=name, grid=(s // tb,),
        in_specs=[pl.BlockSpec((tb, d), lambda i: (i, 0)), pl.BlockSpec((8, d), lambda i: (0, 0))],
        out_specs=pl.BlockSpec((tb, d), lambda i: (i, 0)),
        out_shape=jax.ShapeDtypeStruct((s, d), MXU_DTYPE),
    )(x, pv)


def _ln_stats(r):
    mu = jnp.mean(r, axis=-1, keepdims=True)
    xc = r - mu
    var = jnp.mean(xc * xc, axis=-1, keepdims=True)
    rstd = lax.rsqrt(var + LN_EPS)
    return xc * rstd, rstd


def _ln_fwd(xin, y, pv, name):
    s, d = xin.shape
    tb = _row_block(s)

    def body(x_ref, y_ref, pv_ref, xo_ref, u_ref):
        r = ALPHA * x_ref[...] + pv_ref[0:1, :] * y_ref[...]
        xhat, _ = _ln_stats(r)
        xo = xhat * pv_ref[1:2, :] + pv_ref[2:3, :]
        xo_ref[...] = xo
        u_ref[...] = _mx(xo * pv_ref[3:4, :] + pv_ref[4:5, :])

    row = pl.BlockSpec((tb, d), lambda i: (i, 0))
    return pl.pallas_call(
        body, name=name, grid=(s // tb,),
        in_specs=[row, row, pl.BlockSpec((8, d), lambda i: (0, 0))],
        out_specs=[row, row],
        out_shape=[jax.ShapeDtypeStruct((s, d), F32), jax.ShapeDtypeStruct((s, d), MXU_DTYPE)],
    )(xin, y, pv)


def _ln_bwd(xin, y, pv, name, *, dxo=None, du=None, target=None):
    s, d = xin.shape
    tb = _row_block(s)
    nb = s // tb
    loss_mode = target is not None

    def body(*refs):
        if loss_mode:
            x_ref, y_ref, pv_ref, t_ref, dxin_ref, dy_ref, sums_ref = refs
        else:
            x_ref, y_ref, pv_ref, dxo_ref, du_ref, dxin_ref, dy_ref, sums_ref = refs
        i = pl.program_id(0)

        @pl.when(i == 0)
        def _():
            sums_ref[...] = jnp.zeros_like(sums_ref)

        yv = y_ref[...]
        r = ALPHA * x_ref[...] + pv_ref[0:1, :] * yv
        xhat, rstd = _ln_stats(r)
        xo = xhat * pv_ref[1:2, :] + pv_ref[2:3, :]
        if loss_mode:
            diff = xo - t_ref[...]
            dxo_v = diff * (1.0 / d)
            sums_ref[5:6, :] += jnp.sum(diff * diff, axis=0, keepdims=True) * (0.5 / d)
        else:
            duv = du_ref[...]
            dxo_v = dxo_ref[...] + duv * pv_ref[3:4, :]
            sums_ref[0:1, :] += jnp.sum(duv * xo, axis=0, keepdims=True)
            sums_ref[1:2, :] += jnp.sum(duv, axis=0, keepdims=True)
        sums_ref[2:3, :] += jnp.sum(dxo_v * xhat, axis=0, keepdims=True)
        sums_ref[3:4, :] += jnp.sum(dxo_v, axis=0, keepdims=True)
        dxh = dxo_v * pv_ref[1:2, :]
        dr = rstd * (dxh - jnp.mean(dxh, axis=-1, keepdims=True)
                     - xhat * jnp.mean(dxh * xhat, axis=-1, keepdims=True))
        sums_ref[4:5, :] += jnp.sum(dr * yv, axis=0, keepdims=True)
        dxin_ref[...] = ALPHA * dr
        dy_ref[...] = _mx(pv_ref[0:1, :] * dr)
        if loss_mode:
            @pl.when(i == nb - 1)
            def _():
                sums_ref[5:6, :] = jnp.broadcast_to(jnp.sum(sums_ref[5:6, :], axis=-1, keepdims=True), (1, d))

    row = pl.BlockSpec((tb, d), lambda i: (i, 0))
    par = pl.BlockSpec((8, d), lambda i: (0, 0))
    ins = [xin, y, pv] + ([target] if loss_mode else [dxo, du])
    return pl.pallas_call(
        body, name=name, grid=(nb,),
        in_specs=[row, row, par] + [row] * (len(ins) - 3),
        out_specs=[row, row, par],
        out_shape=[jax.ShapeDtypeStruct((s, d), F32), jax.ShapeDtypeStruct((s, d), MXU_DTYPE),
                   jax.ShapeDtypeStruct((8, d), F32)],
    )(*ins)


def _mod_bwd(dx_direct, du, x, pv, name):
    s, d = x.shape
    tb = _row_block(s)

    def body(dxd_ref, du_ref, x_ref, pv_ref, dx_ref, sums_ref):
        @pl.when(pl.program_id(0) == 0)
        def _():
            sums_ref[...] = jnp.zeros_like(sums_ref)

        duv = du_ref[...]
        dx_ref[...] = dxd_ref[...] + duv * pv_ref[0:1, :]
        sums_ref[0:1, :] += jnp.sum(duv * x_ref[...], axis=0, keepdims=True)
        sums_ref[1:2, :] += jnp.sum(duv, axis=0, keepdims=True)

    row = pl.BlockSpec((tb, d), lambda i: (i, 0))
    par = pl.BlockSpec((8, d), lambda i: (0, 0))
    return pl.pallas_call(
        body, name=name, grid=(s // tb,),
        in_specs=[row, row, row, par], out_specs=[row, par],
        out_shape=[jax.ShapeDtypeStruct((s, d), F32), jax.ShapeDtypeStruct((8, d), F32)],
    )(dx_direct, du, x, pv)


def _mlp_fwd(u, w1, w2, tag):
    s = u.shape[0]

    def epi(val, ex, out):
        out[0][...] = _mx(val)
        out[1][...] = _mx(jnp.square(jnp.maximum(val, 0.0)))

    tm, tn = _tile(s, 1024), 1024
    spec = pl.BlockSpec((tm, tn), lambda i, j, k: (i, j))
    shp = jax.ShapeDtypeStruct((s, D_FF), MXU_DTYPE)
    h, a = _mm_nn(f"mlp_up{tag}", u, w1, None, epi=epi, outs=[(shp, spec), (shp, spec)], tn=tn)
    y = _mm_nn(f"mlp_down{tag}", a, w2, F32)
    return y, (h, a)


def _mlp_bwd(dy, u, h, a, w1t, w2t, tag):
    s = u.shape[0]
    tm, tn = _tile(s, 1024), 1024
    spec = pl.BlockSpec((tm, tn), lambda i, j, k: (i, j))

    def epi(val, ex, out):
        out[0][...] = _mx(val * (2.0 * jnp.maximum(ex[0][...].astype(F32), 0.0)))

    dh = _mm_nn(f"mlp_dh{tag}", dy, w2t, None, epi=epi, extras=[(h, spec)],
                outs=[(jax.ShapeDtypeStruct((s, D_FF), MXU_DTYPE), spec)], tn=tn)
    du = _mm_nn(f"mlp_du{tag}", dh, w1t, F32)
    dw2 = _mm_tn(f"mlp_dw2{tag}", a, dy)
    dw1 = _mm_tn(f"mlp_dw1{tag}", u, dh)
    return du, dw1, dw2


FOX_T = 1024
BIAS_Q = (64, 65, 66)
BIAS_K = (67, 68, 69)
ONES_V = 64

def _fox_constants():
    selq = np.zeros((FOX_HEADS, 512, LANES), np.float32)
    selk = np.zeros((FOX_HEADS, 512, LANES), np.float32)
    selv = np.zeros((2, LANES, LANES), np.float32)
    put = np.zeros((2, 2, LANES, LANES), np.float32)
    for h in range(FOX_HEADS):
        off = FOX_HEAD_DIM * (h % 2)
        for dd in range(FOX_HEAD_DIM):
            selq[h, off + dd, dd] = FOX_HEAD_DIM ** -0.5
            selk[h, off + dd, dd] = 1.0
        for piece in range(3):
            selq[h, LANES * (1 + piece) + h, BIAS_Q[piece]] = 1.0
            selk[h, LANES * (1 + piece) + h, BIAS_K[piece]] = -1.0
    for par in range(2):
        for dd in range(FOX_HEAD_DIM):
            selv[par, FOX_HEAD_DIM * par + dd, dd] = 1.0
            put[par, 0, dd, FOX_HEAD_DIM * par + dd] = FOX_HEAD_DIM ** -0.5
            put[par, 1, dd, FOX_HEAD_DIM * par + dd] = 1.0
    return selq, selk, selv, put


def _pairs(nb, by_key):
    if by_key:
        pr = [(i, j) for j in range(nb) for i in range(j, nb)]
    else:
        pr = [(i, j) for i in range(nb) for j in range(i + 1)]
    return (np.array([p[0] for p in pr], np.int32), np.array([p[1] for p in pr], np.int32))


def _fox_prep(qkv, f, bf):
    s = qkv.shape[0]
    t = _tile(s, FOX_T)
    nb = s // t
    selq, selk, selv, _ = _fox_constants()

    def body(q_ref, k_ref, v_ref, f_ref, bf_ref, selq_ref, selk_ref, selv_ref,
             qa_ref, qat_ref, ka_ref, kat_ref, va_ref, vat_ref, parts_ref, carry_ref):
        i, h = pl.program_id(0), pl.program_id(1)
        lane = lax.broadcasted_iota(jnp.int32, (1, LANES), 1)

        @pl.when(h == 0)
        def _():
            @pl.when(i == 0)
            def _():
                carry_ref[...] = jnp.zeros_like(carry_ref)

            lf = jnp.where(lane < FOX_HEADS, jax.nn.log_sigmoid(f_ref[...] + bf_ref[0:1, :]), 0.0)
            tri = (lax.broadcasted_iota(jnp.int32, (t, t), 0) >= lax.broadcasted_iota(jnp.int32, (t, t), 1)).astype(F32)
            cum = _dot(tri, lf, precision=HIGHEST) + carry_ref[0:1, :]
            carry_ref[0:1, :] = cum[t - 1:t, :]
            hi = _mx(cum)
            r1 = cum - hi.astype(F32)
            mid = _mx(r1)
            parts_ref[:, 0:LANES] = hi
            parts_ref[:, LANES:2 * LANES] = mid
            parts_ref[:, 2 * LANES:3 * LANES] = _mx(r1 - mid.astype(F32))

        parts = parts_ref[...]
        qa = _dot(jnp.concatenate([q_ref[...], parts], axis=1), selq_ref[...])
        qa = qa + jnp.where((lane >= BIAS_K[0]) & (lane <= BIAS_K[2]), 1.0, 0.0)
        ka = _dot(jnp.concatenate([k_ref[...], parts], axis=1), selk_ref[...])
        ka = ka + jnp.where((lane >= BIAS_Q[0]) & (lane <= BIAS_Q[2]), 1.0, 0.0)
        va = _dot(v_ref[...], selv_ref[...]) + jnp.where(lane == ONES_V, 1.0, 0.0)
        qa_ref[...] = _mx(qa)
        qat_ref[...] = _mx(qa.T)
        ka_ref[...] = _mx(ka)
        kat_ref[...] = _mx(ka.T)
        va_ref[...] = _mx(va)
        vat_ref[...] = _mx(va.T)

    rows = jax.ShapeDtypeStruct((FOX_HEADS, nb, t, LANES), MXU_DTYPE)
    cols = jax.ShapeDtypeStruct((FOX_HEADS, nb, LANES, t), MXU_DTYPE)
    rspec = pl.BlockSpec((None, None, t, LANES), lambda i, h: (h, i, 0, 0))
    cspec = pl.BlockSpec((None, None, LANES, t), lambda i, h: (h, i, 0, 0))
    npair = FOX_HEADS // 2
    return pl.pallas_call(
        body, name="fox_prep", grid=(nb, FOX_HEADS),
        in_specs=[pl.BlockSpec((t, LANES), lambda i, h: (i, h // 2)),
                  pl.BlockSpec((t, LANES), lambda i, h: (i, npair + h // 2)),
                  pl.BlockSpec((t, LANES), lambda i, h: (i, 2 * npair + h // 2)),
                  pl.BlockSpec((t, LANES), lambda i, h: (i, 0)),
                  pl.BlockSpec((8, LANES), lambda i, h: (0, 0)),
                  pl.BlockSpec((None, 512, LANES), lambda i, h: (h, 0, 0)),
                  pl.BlockSpec((None, 512, LANES), lambda i, h: (h, 0, 0)),
                  pl.BlockSpec((None, LANES, LANES), lambda i, h: (h % 2, 0, 0))],
        out_specs=[rspec, cspec, rspec, cspec, rspec, cspec],
        out_shape=[rows, cols, rows, cols, rows, cols],
        scratch_shapes=[pltpu.VMEM((t, 3 * LANES), MXU_DTYPE), pltpu.VMEM((8, LANES), F32)],
        compiler_params=_params(VMEM_BIG),
    )(qkv, qkv, qkv, f, bf, _mx(jnp.asarray(selq)), _mx(jnp.asarray(selk)), _mx(jnp.asarray(selv)))


def _causal_allow(t):
    return lax.broadcasted_iota(jnp.int32, (t, t), 0) <= lax.broadcasted_iota(jnp.int32, (t, t), 1)


def _fox_attn_fwd(qat, ka, vat, carry=()):
    heads, nb, _, t = qat.shape
    im, jm = _pairs(nb, by_key=False)
    npairs = len(im)
    carry = list(carry)
    nc = len(carry)

    def body(im_ref, jm_ref, qat_ref, ka_ref, vat_ref, *rest):
        srcs, (ot_ref, lse_ref), dsts = rest[:nc], rest[nc:nc + 2], rest[nc + 2:2 * nc + 2]
        acc_ref, m_ref = rest[2 * nc + 2:2 * nc + 4]
        sems = rest[2 * nc + 4:]
        h, p = pl.program_id(0), pl.program_id(1)
        i, j = im_ref[p], jm_ref[p]

        if nc:
            @pl.when((h == 0) & (p == 0))
            def _():
                _exchange_start(carry, srcs, dsts, sems)

        @pl.when(j == 0)
        def _():
            m_ref[...] = jnp.full_like(m_ref, -jnp.inf)
            acc_ref[...] = jnp.zeros_like(acc_ref)

        def step(diagonal):
            st = _dot(ka_ref[...], qat_ref[...])
            if diagonal:
                st = jnp.where(_causal_allow(t), st, -jnp.inf)
            m_old = m_ref[...]
            m_new = jnp.maximum(m_old, jnp.max(st, axis=0, keepdims=True))
            pt = jnp.exp(st - m_new)
            acc_ref[...] = acc_ref[...] * jnp.exp(m_old - m_new) + _dot(vat_ref[...], _mx(pt))
            m_ref[...] = m_new

        @pl.when(j < i)
        def _():
            step(False)

        @pl.when(j == i)
        def _():
            step(True)
            acc = acc_ref[...]
            denom = acc[ONES_V:ONES_V + 1, :]
            ot_ref[...] = _mx(acc / denom)
            lse_ref[...] = m_ref[...] + jnp.log(denom)

        if nc:
            @pl.when((h == heads - 1) & (p == npairs - 1))
            def _():
                _exchange_wait(carry, srcs, dsts, sems)

    anywhere = pl.BlockSpec(memory_space=pl.ANY)
    qspec = pl.BlockSpec((None, None, LANES, t), lambda h, p, im, jm: (h, im[p], 0, 0))
    grid_spec = pltpu.PrefetchScalarGridSpec(
        num_scalar_prefetch=2, grid=(heads, npairs),
        in_specs=[qspec,
                  pl.BlockSpec((None, None, t, LANES), lambda h, p, im, jm: (h, jm[p], 0, 0)),
                  pl.BlockSpec((None, None, LANES, t), lambda h, p, im, jm: (h, jm[p], 0, 0))] + [anywhere] * nc,
        out_specs=[qspec, pl.BlockSpec((None, None, 1, t), lambda h, p, im, jm: (h, im[p], 0, 0))] + [anywhere] * nc,
        scratch_shapes=[pltpu.VMEM((LANES, t), F32), pltpu.VMEM((1, t), F32)] + (_exchange_sems(carry) if nc else []))
    res = pl.pallas_call(
        body, name="fox_attn_fwd", grid_spec=grid_spec,
        out_shape=[jax.ShapeDtypeStruct((heads, nb, LANES, t), MXU_DTYPE),
                   jax.ShapeDtypeStruct((heads, nb, 1, t), F32)] + _exchange_shapes(carry),
        compiler_params=_params(VMEM_BIG),
    )(jnp.asarray(im), jnp.asarray(jm), qat, ka, vat, *[a for _, a in carry])
    return res[0], res[1], list(res[2:])


def _fox_attn_bwd(qa, qat, ka, kat, va, ot, lse, do, dot_, carry=()):
    heads, nb, t, _ = qa.shape
    im, jm = _pairs(nb, by_key=True)
    npairs = len(im)
    carry = list(carry)
    nc = len(carry)

    def body(im_ref, jm_ref, qa_ref, qat_ref, ka_ref, kat_ref, va_ref, ot_ref, lse_ref, do_ref, dot_ref, *rest):
        srcs, (dqt_ref, dka_ref, dva_ref), dsts = rest[:nc], rest[nc:nc + 3], rest[nc + 3:2 * nc + 3]
        sems = rest[2 * nc + 3:]
        h, p = pl.program_id(0), pl.program_id(1)
        i, j = im_ref[p], jm_ref[p]

        if nc:
            @pl.when((h == 0) & (p == 0))
            def _():
                _exchange_start(carry, srcs, dsts, sems)

        @pl.when(p == 0)
        def _():
            dqt_ref[...] = jnp.zeros_like(dqt_ref)

        def step(diagonal):
            st = _dot(ka_ref[...], qat_ref[...])
            dot_v = dot_ref[...]
            delta = jnp.sum(ot_ref[...].astype(F32) * dot_v.astype(F32), axis=0, keepdims=True)
            pt = jnp.exp(st - lse_ref[...])
            if diagonal:
                pt = jnp.where(_causal_allow(t), pt, 0.0)
            dsm = _mx(pt * (_dot(va_ref[...], dot_v) - delta))
            upd_v = _dot(_mx(pt), do_ref[...])
            upd_k = _dot(dsm, qa_ref[...])
            if diagonal:
                dva_ref[...] = upd_v
                dka_ref[...] = upd_k
            else:
                dva_ref[...] += upd_v
                dka_ref[...] += upd_k
            dqt_ref[i] += _dot(kat_ref[...], dsm)

        @pl.when(i == j)
        def _():
            step(True)

        @pl.when(i > j)
        def _():
            step(False)

        if nc:
            @pl.when((h == heads - 1) & (p == npairs - 1))
            def _():
                _exchange_wait(carry, srcs, dsts, sems)

    def at_q(shape):
        return pl.BlockSpec((None, None) + shape, lambda h, p, im, jm: (h, im[p], 0, 0))

    def at_k(shape):
        return pl.BlockSpec((None, None) + shape, lambda h, p, im, jm: (h, jm[p], 0, 0))

    anywhere = pl.BlockSpec(memory_space=pl.ANY)
    grid_spec = pltpu.PrefetchScalarGridSpec(
        num_scalar_prefetch=2, grid=(heads, npairs),
        in_specs=[at_q((t, LANES)), at_q((LANES, t)), at_k((t, LANES)), at_k((LANES, t)), at_k((t, LANES)),
                  at_q((LANES, t)), at_q((1, t)), at_q((t, LANES)), at_q((LANES, t))] + [anywhere] * nc,
        out_specs=[pl.BlockSpec((None, nb, LANES, t), lambda h, p, im, jm: (h, 0, 0, 0)),
                   at_k((t, LANES)), at_k((t, LANES))] + [anywhere] * nc,
        scratch_shapes=_exchange_sems(carry) if nc else [])
    res = pl.pallas_call(
        body, name="fox_attn_bwd", grid_spec=grid_spec,
        out_shape=[jax.ShapeDtypeStruct((heads, nb, LANES, t), F32),
                   jax.ShapeDtypeStruct((heads, nb, t, LANES), F32),
                   jax.ShapeDtypeStruct((heads, nb, t, LANES), F32)] + _exchange_shapes(carry),
        compiler_params=_params(VMEM_BIG),
    )(jnp.asarray(im), jnp.asarray(jm), qa, qat, ka, kat, va, ot, lse, do, dot_, *[a for _, a in carry])
    return res[0], res[1], res[2], list(res[3:])


def _fox_post(dqt, dka, dva, f, bf):
    heads, nb, t, _ = dka.shape
    s = nb * t
    _, _, _, put = _fox_constants()
    npair = heads // 2

    def body(dqt_ref, dka_ref, dva_ref, f_ref, bf_ref, put_ref, dq_ref, dk_ref, dv_ref, df_ref, sums_ref,
             dc_ref, carry_ref):
        i, h = pl.program_id(0), pl.program_id(1)

        @pl.when((i == 0) & (h == 0))
        def _():
            carry_ref[...] = jnp.zeros_like(carry_ref)
            sums_ref[...] = jnp.zeros_like(sums_ref)

        @pl.when(h == 0)
        def _():
            dc_ref[...] = jnp.zeros_like(dc_ref)

        dqt_v = dqt_ref[...]
        dka_v = dka_ref[...]
        term_q = _dot(_mx(dqt_v), put_ref[0], TN)
        term_k = _dot(_mx(dka_v), put_ref[1])
        term_v = _dot(_mx(dva_ref[...]), put_ref[1])

        @pl.when(h % 2 == 0)
        def _():
            dq_ref[...] = _mx(term_q)
            dk_ref[...] = _mx(term_k)
            dv_ref[...] = _mx(term_v)

        @pl.when(h % 2 == 1)
        def _():
            dq_ref[...] += _mx(term_q)
            dk_ref[...] += _mx(term_k)
            dv_ref[...] += _mx(term_v)

        dcum = dqt_v[BIAS_Q[0]:BIAS_Q[0] + 1, :] - dka_v.T[BIAS_K[0]:BIAS_K[0] + 1, :]
        head_row = lax.broadcasted_iota(jnp.int32, (heads, 1), 0) == h
        dc_ref[...] += jnp.where(head_row, dcum, 0.0)

        @pl.when(h == heads - 1)
        def _():
            later = (lax.broadcasted_iota(jnp.int32, (t, t), 0) >= lax.broadcasted_iota(jnp.int32, (t, t), 1)).astype(F32)
            dlf_t = _dot(dc_ref[...], later, precision=HIGHEST) + carry_ref[:, 0:1]
            carry_ref[...] = jnp.broadcast_to(dlf_t[:, 0:1], carry_ref.shape)
            dlf = jnp.concatenate([dlf_t, jnp.zeros((LANES - heads, t), F32)], axis=0).T
            lane = lax.broadcasted_iota(jnp.int32, (1, LANES), 1)
            df = jnp.where(lane < heads, dlf * jax.nn.sigmoid(-(f_ref[...] + bf_ref[0:1, :])), 0.0)
            df_ref[...] = _mx(df)
            sums_ref[0:1, :] += jnp.sum(df, axis=0, keepdims=True)

    rev = lambda i: nb - 1 - i
    pair_spec = pl.BlockSpec((t, LANES), lambda i, h: (rev(i), h // 2))
    blk = pl.BlockSpec((t, LANES), lambda i, h: (rev(i), 0))
    hd = jax.ShapeDtypeStruct((s, D_MODEL), MXU_DTYPE)
    return pl.pallas_call(
        body, name="fox_post", grid=(nb, heads),
        in_specs=[pl.BlockSpec((None, None, LANES, t), lambda i, h: (h, rev(i), 0, 0)),
                  pl.BlockSpec((None, None, t, LANES), lambda i, h: (h, rev(i), 0, 0)),
                  pl.BlockSpec((None, None, t, LANES), lambda i, h: (h, rev(i), 0, 0)),
                  blk, pl.BlockSpec((8, LANES), lambda i, h: (0, 0)),
                  pl.BlockSpec((None, 2, LANES, LANES), lambda i, h: (h % 2, 0, 0, 0))],
        out_specs=[pair_spec, pair_spec, pair_spec, blk, pl.BlockSpec((8, LANES), lambda i, h: (0, 0))],
        out_shape=[hd, hd, hd, jax.ShapeDtypeStruct((s, LANES), MXU_DTYPE), jax.ShapeDtypeStruct((8, LANES), F32)],
        scratch_shapes=[pltpu.VMEM((heads, t), F32), pltpu.VMEM((heads, LANES), F32)],
        compiler_params=_params(VMEM_BIG),
    )(dqt, dka, dva, f, bf, _mx(jnp.asarray(put)))


def _fox_weights(w_in, w_o):
    wqkv = w_in[:, :3 * D_MODEL]
    wf = jnp.pad(w_in[:, 3 * D_MODEL:], ((0, 0), (0, LANES - FOX_HEADS)))
    wo_heads = w_o.reshape(FOX_HEADS, FOX_HEAD_DIM, D_MODEL)
    wo_a = jnp.pad(wo_heads, ((0, 0), (0, LANES - FOX_HEAD_DIM), (0, 0)))
    return dict(wqkv=wqkv, wf=wf, wqkv_t=wqkv.T, wf_t=wf.T, wo_a=wo_a, wo_at=jnp.swapaxes(wo_a, 1, 2))


def _fox_fwd(u, w, bf, carry=()):
    s = u.shape[0]
    qkv = _mm_nn("fox_qkv", u, w["wqkv"], MXU_DTYPE)
    f = _mm_nn("fox_f", u, w["wf"], F32)
    qa, qat, ka, kat, va, vat = _fox_prep(qkv, f, bf)
    ot, lse, carried = _fox_attn_fwd(qat, ka, vat, carry)
    heads, nb, _, t = ot.shape
    tn = 1024
    y = _mm("fox_out", ot, w["wo_a"], grid=(nb, D_MODEL // tn, heads),
            a_spec=pl.BlockSpec((None, None, LANES, t), lambda i, j, k: (k, i, 0, 0)),
            b_spec=pl.BlockSpec((None, LANES, tn), lambda i, j, k: (k, 0, j)),
            dims=TN, k_axis=2, acc=(t, tn),
            outs=[(jax.ShapeDtypeStruct((s, D_MODEL), F32), pl.BlockSpec((t, tn), lambda i, j, k: (i, j)))],
            vmem=VMEM_BIG)[0]
    return y, dict(f=f, qa=qa, qat=qat, ka=ka, kat=kat, va=va, ot=ot, lse=lse), carried


def _fox_bwd(dy, u, w, bf, res, carry=()):
    heads, nb, t, _ = res["qa"].shape
    s = nb * t
    do = _mm("fox_do", dy, w["wo_at"], grid=(nb, heads, 1),
             a_spec=pl.BlockSpec((t, D_MODEL), lambda i, h, k: (i, 0)),
             b_spec=pl.BlockSpec((None, D_MODEL, LANES), lambda i, h, k: (h, 0, 0)),
             dims=NN, k_axis=2,
             outs=[(jax.ShapeDtypeStruct((heads, nb, t, LANES), MXU_DTYPE),
                    pl.BlockSpec((None, None, t, LANES), lambda i, h, k: (h, i, 0, 0)))])[0]
    dot_ = _mm("fox_dot", w["wo_a"], dy, grid=(nb, heads, 1),
               a_spec=pl.BlockSpec((None, LANES, D_MODEL), lambda i, h, k: (h, 0, 0)),
               b_spec=pl.BlockSpec((t, D_MODEL), lambda i, h, k: (i, 0)),
               dims=NT, k_axis=2,
               outs=[(jax.ShapeDtypeStruct((heads, nb, LANES, t), MXU_DTYPE),
                      pl.BlockSpec((None, None, LANES, t), lambda i, h, k: (h, i, 0, 0)))])[0]
    dwo_a = _mm("fox_dwo", res["ot"], dy, grid=(heads, 1, nb),
                a_spec=pl.BlockSpec((None, None, LANES, t), lambda h, j, k: (h, k, 0, 0)),
                b_spec=pl.BlockSpec((t, D_MODEL), lambda h, j, k: (k, 0)),
                dims=NN, k_axis=2, acc=(LANES, D_MODEL),
                outs=[(jax.ShapeDtypeStruct((heads, LANES, D_MODEL), F32),
                       pl.BlockSpec((None, LANES, D_MODEL), lambda h, j, k: (h, 0, 0)))])[0]
    dqt, dka, dva, carried = _fox_attn_bwd(res["qa"], res["qat"], res["ka"], res["kat"], res["va"], res["ot"],
                                           res["lse"], do, dot_, carry)
    dq, dk, dv, df, sums = _fox_post(dqt, dka, dva, res["f"], bf)
    wt = w["wqkv_t"]
    du = _mm_nn("fox_du_q", dq, wt[:D_MODEL], F32)
    du = _mm_nn("fox_du_k", dk, wt[D_MODEL:2 * D_MODEL], F32, addend=du)
    du = _mm_nn("fox_du_v", dv, wt[2 * D_MODEL:], F32, addend=du)
    du = _mm_nn("fox_du_f", df, w["wf_t"], F32, addend=du)
    dw_in = jnp.concatenate(
        [_mm_tn("fox_dw_q", u, dq), _mm_tn("fox_dw_k", u, dk), _mm_tn("fox_dw_v", u, dv),
         _mm_tn("fox_dw_f", u, df)[:, :FOX_HEADS]], axis=1)
    dw_o = dwo_a[:, :FOX_HEAD_DIM, :].reshape(D_MODEL, D_MODEL)
    return du, dw_in, dw_o, sums, carried


def _dsilu(v):
    sg = jax.nn.sigmoid(v)
    return sg * (1.0 + v * (1.0 - sg))


def _conv_taps(scr_ref, w_ref, rows, base):
    acc = None
    for k in range(SSM_CONV):
        term = scr_ref[pl.ds(base - (SSM_CONV - 1) + k, rows), :] * w_ref[k:k + 1, :]
        acc = term if acc is None else acc + term
    return acc


def _conv_fwd(zx, cw, cb):
    s = zx.shape[0]
    tb = _tile(s, 512)
    half = SSM_CONV_DIM // 2
    hb = tb // SUBLANES

    def body(x_ref, halo_ref, w_ref, b_ref, o_ref, scr_ref):
        i = pl.program_id(0)
        scr_ref[pl.ds(0, SUBLANES), :] = jnp.where(i > 0, halo_ref[...], 0.0)
        scr_ref[pl.ds(SUBLANES, tb), :] = x_ref[...]
        o_ref[...] = jax.nn.silu(_conv_taps(scr_ref, w_ref, tb, SUBLANES) + b_ref[0:1, :])

    return pl.pallas_call(
        body, name="ssd_conv_fwd", grid=(s // tb, 2),
        in_specs=[pl.BlockSpec((tb, half), lambda i, j: (i, 1 + j)),
                  pl.BlockSpec((SUBLANES, half), lambda i, j: (jnp.maximum(i * hb - 1, 0), 1 + j)),
                  pl.BlockSpec((8, half), lambda i, j: (0, j)),
                  pl.BlockSpec((8, half), lambda i, j: (0, j))],
        out_specs=pl.BlockSpec((tb, half), lambda i, j: (i, j)),
        out_shape=jax.ShapeDtypeStruct((s, SSM_CONV_DIM), F32),
        scratch_shapes=[pltpu.VMEM((tb + SUBLANES, half), F32)],
    )(zx, zx, cw, cb)


def _conv_bwd_pre(zx, dxc, cw, cb):
    s = zx.shape[0]
    tb = _tile(s, 512)
    half = SSM_CONV_DIM // 2
    hb = tb // SUBLANES

    def body(x_ref, halo_ref, d_ref, w_ref, b_ref, o_ref, sums_ref, scr_ref):
        i = pl.program_id(1)

        @pl.when(i == 0)
        def _():
            sums_ref[...] = jnp.zeros_like(sums_ref)

        scr_ref[pl.ds(0, SUBLANES), :] = jnp.where(i > 0, halo_ref[...], 0.0)
        scr_ref[pl.ds(SUBLANES, tb), :] = x_ref[...]
        pre = _conv_taps(scr_ref, w_ref, tb, SUBLANES) + b_ref[0:1, :]
        dpre = d_ref[...] * _dsilu(pre)
        o_ref[...] = dpre
        for k in range(SSM_CONV):
            shifted = scr_ref[pl.ds(SUBLANES - (SSM_CONV - 1) + k, tb), :]
            sums_ref[k:k + 1, :] += jnp.sum(dpre * shifted, axis=0, keepdims=True)
        sums_ref[SSM_CONV:SSM_CONV + 1, :] += jnp.sum(dpre, axis=0, keepdims=True)

    return pl.pallas_call(
        body, name="ssd_conv_bwd_pre", grid=(2, s // tb),
        in_specs=[pl.BlockSpec((tb, half), lambda j, i: (i, 1 + j)),
                  pl.BlockSpec((SUBLANES, half), lambda j, i: (jnp.maximum(i * hb - 1, 0), 1 + j)),
                  pl.BlockSpec((tb, half), lambda j, i: (i, j)),
                  pl.BlockSpec((8, half), lambda j, i: (0, j)),
                  pl.BlockSpec((8, half), lambda j, i: (0, j))],
        out_specs=[pl.BlockSpec((tb, half), lambda j, i: (i, j)),
                   pl.BlockSpec((8, half), lambda j, i: (0, j))],
        out_shape=[jax.ShapeDtypeStruct((s, SSM_CONV_DIM), F32), jax.ShapeDtypeStruct((8, SSM_CONV_DIM), F32)],
        scratch_shapes=[pltpu.VMEM((tb + SUBLANES, half), F32)],
    )(zx, zx, dxc, cw, cb)


def _conv_bwd_x(dpre, cw):
    s = dpre.shape[0]
    tb = _tile(s, 512)
    hb = tb // SUBLANES
    nb = s // tb

    def body(d_ref, halo_ref, w_ref, o_ref, scr_ref):
        i = pl.program_id(0)
        scr_ref[pl.ds(0, tb), :] = d_ref[...]
        scr_ref[pl.ds(tb, SUBLANES), :] = jnp.where(i < nb - 1, halo_ref[...], 0.0)
        acc = None
        for k in range(SSM_CONV):
            term = scr_ref[pl.ds(SSM_CONV - 1 - k, tb), :] * w_ref[k:k + 1, :]
            acc = term if acc is None else acc + term
        o_ref[...] = _mx(acc)

    return pl.pallas_call(
        body, name="ssd_conv_bwd_x", grid=(nb,),
        in_specs=[pl.BlockSpec((tb, SSM_CONV_DIM), lambda i: (i, 0)),
                  pl.BlockSpec((SUBLANES, SSM_CONV_DIM), lambda i: (jnp.minimum((i + 1) * hb, s // SUBLANES - 1), 0)),
                  pl.BlockSpec((8, SSM_CONV_DIM), lambda i: (0, 0))],
        out_specs=pl.BlockSpec((tb, SSM_CONV_DIM), lambda i: (i, 0)),
        out_shape=jax.ShapeDtypeStruct((s, SSM_CONV_DIM), MXU_DTYPE),
        scratch_shapes=[pltpu.VMEM((tb + SUBLANES, SSM_CONV_DIM), F32)],
        compiler_params=_params(VMEM_BIG),
    )(dpre, dpre, cw)


def _expand_constants():
    ex = np.zeros((LANES, SSM_D_INNER), np.float32)
    for h in range(SSM_HEADS):
        ex[h, h * 64:(h + 1) * 64] = 1.0
    return ex, np.ascontiguousarray(ex.T)


def _ssd_common(dtr_ref, par_ref, ex_ref, xc_ref):
    lc = SSM_CHUNK
    lane = lax.broadcasted_iota(jnp.int32, (1, LANES), 1)
    is_head = lane < SSM_HEADS
    par = par_ref[...]
    pre = dtr_ref[...] + par[0:1, :]
    dt = jnp.where(is_head, jax.nn.softplus(pre), 0.0)
    a = jnp.where(is_head, -jnp.exp(par[1:2, :]), 0.0)
    tri_b = lax.broadcasted_iota(jnp.int32, (lc, lc), 0) >= lax.broadcasted_iota(jnp.int32, (lc, lc), 1)
    tri = tri_b.astype(F32)
    da = dt * a
    acs = _dot(tri, da, precision=HIGHEST)
    acs_t = _dot(da, tri, (((0,), (1,)), ((), ())), precision=HIGHEST)
    ex = ex_ref[...]
    dt_x = _dot(dt, ex, precision=HIGHEST)
    acs_x = _dot(acs, ex, precision=HIGHEST)
    d_x = _dot(par, ex, precision=HIGHEST)[2:3, :]
    last_x = acs_x[lc - 1:lc, :]
    xs = xc_ref[:, 0:SSM_D_INNER]
    return dict(pre=pre, dt=dt, a=a, tri_b=tri_b, tri=tri, acs=acs, acs_t=acs_t, dt_x=dt_x, d_x=d_x, xs=xs,
                xdt=xs * dt_x, e_x=jnp.exp(acs_x), dte_x=jnp.exp(last_x - acs_x), cd_x=jnp.exp(last_x),
                is_head=is_head)


def _decay_in(q, h):
    seg = q["acs"][:, h:h + 1] - q["acs_t"][h:h + 1, :]
    return jnp.exp(jnp.where(q["tri_b"], seg, -jnp.inf))


def _ssd_scan_fwd(xc, dtr, par):
    s = xc.shape[0]
    lc = SSM_CHUNK
    nc = s // lc
    ex, _ = _expand_constants()

    def body(xc_ref, dtr_ref, par_ref, ex_ref, y_ref, prev_ref, st_ref):
        @pl.when(pl.program_id(0) == 0)
        def _():
            st_ref[...] = jnp.zeros_like(st_ref)

        q = _ssd_common(dtr_ref, par_ref, ex_ref, xc_ref)
        lane = lax.broadcasted_iota(jnp.int32, (1, LANES), 1)
        for g in range(SSM_GROUPS):
            sl = slice(g * GROUP_W, (g + 1) * GROUP_W)
            bg = _mx(xc_ref[:, SSM_D_INNER + g * SSM_STATE:SSM_D_INNER + (g + 1) * SSM_STATE])
            cg = _mx(xc_ref[:, SSM_D_INNER + (SSM_GROUPS + g) * SSM_STATE:SSM_D_INNER + (SSM_GROUPS + g + 1) * SSM_STATE])
            gm = _dot(cg, bg, NT)
            prev = st_ref[g]
            prev_ref[g] = prev
            yoff = _dot(cg, _mx(prev)) * q["e_x"][:, sl]
            st_ref[g] = prev * q["cd_x"][:, sl] + _dot(bg, _mx(q["xdt"][:, sl] * q["dte_x"][:, sl]), TN)
            pairs = []
            for pr in range(2):
                xp = _mx(q["xdt"][:, g * GROUP_W + pr * LANES:g * GROUP_W + (pr + 1) * LANES])
                both = [_dot(_mx(gm * _decay_in(q, 4 * g + 2 * pr + r2)), xp) for r2 in range(2)]
                pairs.append(jnp.where(lane < 64, both[0], both[1]))
            y_ref[:, sl] = jnp.concatenate(pairs, axis=1) + yoff + q["xs"][:, sl] * q["d_x"][:, sl]

    return pl.pallas_call(
        body, name="ssd_scan_fwd", grid=(nc,),
        in_specs=[pl.BlockSpec((lc, SSM_CONV_DIM), lambda c: (c, 0)),
                  pl.BlockSpec((lc, LANES), lambda c: (c, 0)),
                  pl.BlockSpec((8, LANES), lambda c: (0, 0)),
                  pl.BlockSpec((LANES, SSM_D_INNER), lambda c: (0, 0))],
        out_specs=[pl.BlockSpec((lc, SSM_D_INNER), lambda c: (c, 0)),
                   pl.BlockSpec((None, SSM_GROUPS, SSM_STATE, GROUP_W), lambda c: (c, 0, 0, 0))],
        out_shape=[jax.ShapeDtypeStruct((s, SSM_D_INNER), F32),
                   jax.ShapeDtypeStruct((nc, SSM_GROUPS, SSM_STATE, GROUP_W), F32)],
        scratch_shapes=[pltpu.VMEM((SSM_GROUPS, SSM_STATE, GROUP_W), F32)],
        compiler_params=_params(VMEM_BIG),
    )(xc, dtr, par, jnp.asarray(ex))


def _ssd_scan_bwd(dy, xc, dtr, par, prev):
    s = xc.shape[0]
    lc = SSM_CHUNK
    nc = s // lc
    ex, ex_t = _expand_constants()

    def body(dy_ref, xc_ref, dtr_ref, par_ref, prev_ref, ex_ref, ext_ref, dxc_ref, ddtr_ref, sums_ref,
             gst_ref, tacs_ref, tdt_ref, tdd_ref):
        @pl.when(pl.program_id(0) == 0)
        def _():
            gst_ref[...] = jnp.zeros_like(gst_ref)
            sums_ref[...] = jnp.zeros_like(sums_ref)

        q = _ssd_common(dtr_ref, par_ref, ex_ref, xc_ref)
        lane = lax.broadcasted_iota(jnp.int32, (1, LANES), 1)
        row = lax.broadcasted_iota(jnp.int32, (lc, 1), 0)
        dacs_rows = jnp.zeros((lc, LANES), F32)
        dacs_cols_t = jnp.zeros((LANES, lc), F32)
        for g in range(SSM_GROUPS):
            sl = slice(g * GROUP_W, (g + 1) * GROUP_W)
            b_lo = SSM_D_INNER + g * SSM_STATE
            c_lo = SSM_D_INNER + (SSM_GROUPS + g) * SSM_STATE
            bg = _mx(xc_ref[:, b_lo:b_lo + SSM_STATE])
            cg = _mx(xc_ref[:, c_lo:c_lo + SSM_STATE])
            dyg = dy_ref[:, sl]
            xsg, xdtg = q["xs"][:, sl], q["xdt"][:, sl]
            eg, dteg, cdg = q["e_x"][:, sl], q["dte_x"][:, sl], q["cd_x"][:, sl]
            prevg = prev_ref[g]
            gs = gst_ref[g]
            prevm, gsm = _mx(prevg), _mx(gs)
            tdd_ref[:, sl] = dyg * xsg
            dxs = dyg * q["d_x"][:, sl]
            t_acs = dyg * _dot(cg, prevm) * eg
            dcp = _mx(dyg * eg)
            dc = _dot(dcp, prevm, NT)
            dprev = _dot(cg, dcp, TN)
            db = _dot(_mx(xdtg * dteg), gsm, NT)
            dx2 = _dot(bg, gsm)
            dxdt = dx2 * dteg
            ddte = dx2 * xdtg * dteg
            t_acs = t_acs - ddte
            last = (jnp.sum(ddte, axis=0, keepdims=True)
                    + jnp.sum(gs * prevg, axis=0, keepdims=True) * cdg)
            gm = _dot(cg, bg, NT)
            dgm = jnp.zeros((lc, lc), F32)
            pair_dx = []
            for pr in range(2):
                lo = g * GROUP_W + pr * LANES
                xp = _mx(q["xdt"][:, lo:lo + LANES])
                dyp = dy_ref[:, lo:lo + LANES]
                both = []
                for r2 in range(2):
                    h = 4 * g + 2 * pr + r2
                    mine = (lane >= 64 * r2) & (lane < 64 * (r2 + 1))
                    lm = _decay_in(q, h)
                    m = gm * lm
                    dm = _dot(_mx(jnp.where(mine, dyp, 0.0)), xp, NT)
                    dgm = dgm + dm * lm
                    w = dm * m
                    dacs_rows = dacs_rows + jnp.sum(w, axis=1, keepdims=True) * (lane == h).astype(F32)
                    head_row = (lax.broadcasted_iota(jnp.int32, (LANES, 1), 0) == h).astype(F32)
                    dacs_cols_t = dacs_cols_t + head_row * jnp.sum(w, axis=0, keepdims=True)
                    both.append(_dot(_mx(m), _mx(dyp), TN))
                pair_dx.append(jnp.where(lane < 64, both[0], both[1]))
            dxdt = dxdt + jnp.concatenate(pair_dx, axis=1)
            dgmm = _mx(dgm)
            dc = dc + _dot(dgmm, bg)
            db = db + _dot(dgmm, cg, TN)
            dxs = dxs + dxdt * q["dt_x"][:, sl]
            tdt_ref[:, sl] = dxdt * xsg
            tacs_ref[:, sl] = t_acs + jnp.where(row == lc - 1, last, 0.0)
            dxc_ref[:, sl] = dxs
            dxc_ref[:, b_lo:b_lo + SSM_STATE] = db
            dxc_ref[:, c_lo:c_lo + SSM_STATE] = dc
            gst_ref[g] = gs * cdg + dprev
        ext = ext_ref[...]
        dacs = _dot(tacs_ref[...], ext, precision=HIGHEST) + dacs_rows - dacs_cols_t.T
        dda = _dot(q["tri"], dacs, TN, precision=HIGHEST)
        ddt = dda * q["a"] + _dot(tdt_ref[...], ext, precision=HIGHEST)
        ddtr = jnp.where(q["is_head"], ddt * jax.nn.sigmoid(q["pre"]), 0.0)
        ddtr_ref[...] = _mx(ddtr)
        tdd = jnp.broadcast_to(jnp.sum(tdd_ref[...], axis=0, keepdims=True), (8, SSM_D_INNER))
        sums_ref[0:1, :] += jnp.sum(ddtr, axis=0, keepdims=True)
        sums_ref[1:2, :] += jnp.sum(dda * q["dt"], axis=0, keepdims=True) * q["a"]
        sums_ref[2:3, :] += _dot(tdd, ext, precision=HIGHEST)[0:1, :]

    rev = lambda c: nc - 1 - c
    wide = pltpu.VMEM((lc, SSM_D_INNER), F32)
    return pl.pallas_call(
        body, name="ssd_scan_bwd", grid=(nc,),
        in_specs=[pl.BlockSpec((lc, SSM_D_INNER), lambda c: (rev(c), 0)),
                  pl.BlockSpec((lc, SSM_CONV_DIM), lambda c: (rev(c), 0)),
                  pl.BlockSpec((lc, LANES), lambda c: (rev(c), 0)),
                  pl.BlockSpec((8, LANES), lambda c: (0, 0)),
                  pl.BlockSpec((None, SSM_GROUPS, SSM_STATE, GROUP_W), lambda c: (rev(c), 0, 0, 0)),
                  pl.BlockSpec((LANES, SSM_D_INNER), lambda c: (0, 0)),
                  pl.BlockSpec((SSM_D_INNER, LANES), lambda c: (0, 0))],
        out_specs=[pl.BlockSpec((lc, SSM_CONV_DIM), lambda c: (rev(c), 0)),
                   pl.BlockSpec((lc, LANES), lambda c: (rev(c), 0)),
                   pl.BlockSpec((8, LANES), lambda c: (0, 0))],
        out_shape=[jax.ShapeDtypeStruct((s, SSM_CONV_DIM), F32), jax.ShapeDtypeStruct((s, LANES), MXU_DTYPE),
                   jax.ShapeDtypeStruct((8, LANES), F32)],
        scratch_shapes=[pltpu.VMEM((SSM_GROUPS, SSM_STATE, GROUP_W), F32), wide, wide, wide],
        compiler_params=_params(VMEM_BIG),
    )(dy, xc, dtr, par, prev, jnp.asarray(ex), jnp.asarray(ex_t))


def _group_norm_parts(yv, zv):
    yg = yv * jax.nn.silu(zv)
    normed, rinvs = [], []
    for g in range(SSM_GROUPS):
        blk = yg[:, g * GROUP_W:(g + 1) * GROUP_W]
        rinv = lax.rsqrt(jnp.mean(blk * blk, axis=-1, keepdims=True) + RMS_EPS)
        normed.append(blk * rinv)
        rinvs.append(rinv)
    return normed, rinvs


def _gnorm_fwd(y, zx, nw):
    s = y.shape[0]
    tb = _tile(s, 512)

    def body(y_ref, z_ref, w_ref, o_ref):
        normed, _ = _group_norm_parts(y_ref[...], z_ref[...])
        for g in range(SSM_GROUPS):
            sl = slice(g * GROUP_W, (g + 1) * GROUP_W)
            o_ref[:, sl] = _mx(normed[g] * w_ref[0:1, sl])

    row = pl.BlockSpec((tb, SSM_D_INNER), lambda i: (i, 0))
    return pl.pallas_call(
        body, name="ssd_gnorm_fwd", grid=(s // tb,),
        in_specs=[row, row, pl.BlockSpec((8, SSM_D_INNER), lambda i: (0, 0))],
        out_specs=row, out_shape=jax.ShapeDtypeStruct((s, SSM_D_INNER), MXU_DTYPE),
    )(y, zx, nw)


def _gnorm_bwd(y, zx, nw, dyn):
    s = y.shape[0]
    tb = _tile(s, 512)

    def body(y_ref, z_ref, w_ref, d_ref, dy_ref, dz_ref, sums_ref):
        @pl.when(pl.program_id(0) == 0)
        def _():
            sums_ref[...] = jnp.zeros_like(sums_ref)

        yv, zv = y_ref[...], z_ref[...]
        normed, rinvs = _group_norm_parts(yv, zv)
        gate = jax.nn.silu(zv)
        dgate = _dsilu(zv)
        for g in range(SSM_GROUPS):
            sl = slice(g * GROUP_W, (g + 1) * GROUP_W)
            dv = d_ref[:, sl]
            n = normed[g]
            sums_ref[0:1, sl] += jnp.sum(dv * n, axis=0, keepdims=True)
            dn = dv * w_ref[0:1, sl]
            dyg = rinvs[g] * (dn - n * jnp.mean(dn * n, axis=-1, keepdims=True))
            dy_ref[:, sl] = dyg * gate[:, sl]
            dz_ref[:, sl] = _mx(dyg * yv[:, sl] * dgate[:, sl])

    row = pl.BlockSpec((tb, SSM_D_INNER), lambda i: (i, 0))
    par = pl.BlockSpec((8, SSM_D_INNER), lambda i: (0, 0))
    return pl.pallas_call(
        body, name="ssd_gnorm_bwd", grid=(s // tb,),
        in_specs=[row, row, par, row], out_specs=[row, row, par],
        out_shape=[jax.ShapeDtypeStruct((s, SSM_D_INNER), F32), jax.ShapeDtypeStruct((s, SSM_D_INNER), MXU_DTYPE),
                   jax.ShapeDtypeStruct((8, SSM_D_INNER), F32)],
    )(y, zx, nw, dyn)


def _rows8(v):
    v = v.reshape(1, -1)
    return jnp.pad(v, ((0, 7), (0, 0)))


def _ssd_weights(w_in, w_out):
    nzx = SSM_D_INNER + SSM_CONV_DIM
    wzx = w_in[:, :nzx]
    wdt = jnp.pad(w_in[:, nzx:], ((0, 0), (0, LANES - SSM_HEADS)))
    return dict(wzx=wzx, wdt=wdt, wzx_t=wzx.T, wdt_t=wdt.T, wout=w_out, wout_t=w_out.T)


def _ssd_fwd(u, w, cw, cb, par, nw):
    zx = _mm_nn("ssd_in_zx", u, w["wzx"], F32)
    dtr = _mm_nn("ssd_in_dt", u, w["wdt"], F32)
    xc = _conv_fwd(zx, cw, cb)
    y, prev = _ssd_scan_fwd(xc, dtr, par)
    yn = _gnorm_fwd(y, zx, nw)
    out = _mm_nn("ssd_out", yn, w["wout"], F32)
    return out, dict(zx=zx, dtr=dtr, xc=xc, y=y, prev=prev, yn=yn)


def _ssd_bwd(dy, u, w, cw, cb, par, nw, res):
    dyn = _mm_nn("ssd_dyn", dy, w["wout_t"], F32)
    dw_out = _mm_tn("ssd_dw_out", res["yn"], dy)
    dys, dz, nsum = _gnorm_bwd(res["y"], res["zx"], nw, dyn)
    dxc, ddtr, ssum = _ssd_scan_bwd(dys, res["xc"], res["dtr"], par, res["prev"])
    dpre, csum = _conv_bwd_pre(res["zx"], dxc, cw, cb)
    dxbc = _conv_bwd_x(dpre, cw)
    wt = w["wzx_t"]
    du = _mm_nn("ssd_du_z", dz, wt[:SSM_D_INNER], F32)
    du = _mm_nn("ssd_du_x", dxbc, wt[SSM_D_INNER:], F32, addend=du)
    du = _mm_nn("ssd_du_dt", ddtr, w["wdt_t"], F32, addend=du)
    dw_in = jnp.concatenate(
        [_mm_tn("ssd_dw_z", u, dz), _mm_tn("ssd_dw_x", u, dxbc), _mm_tn("ssd_dw_dt", u, ddtr)[:, :SSM_HEADS]], axis=1)
    small = dict(conv_w=csum[:SSM_CONV], conv_b=csum[SSM_CONV], dt_bias=ssum[0, :SSM_HEADS],
                 a_log=ssum[1, :SSM_HEADS], d=ssum[2, :SSM_HEADS], norm_w=nsum[0])
    return du, dw_in, dw_out, small


def _ada_fwd(c_all, ada_w, ada_b_mine):
    nl, _, ncol = ada_w.shape

    def body(c_ref, w_ref, b_ref, o_ref):
        cond = _mx(jax.nn.silu(c_ref[...]))
        for i in range(nl):
            o_ref[i] = _dot(cond, _mx(w_ref[i])) + b_ref[i:i + 1, :]

    return pl.pallas_call(
        body, name="ada_fwd", out_shape=jax.ShapeDtypeStruct((nl, 2 * N_DEV, ncol), F32),
        compiler_params=_params(VMEM_BIG),
    )(c_all, ada_w, ada_b_mine)


def _ada_bwd(c_all, dmod_cols):
    nl, _, ncol = dmod_cols.shape

    def body(c_ref, d_ref, o_ref):
        cond = _mx(jax.nn.silu(c_ref[...]))
        for i in range(nl):
            o_ref[i] = _dot(cond, _mx(d_ref[i]), TN)

    return pl.pallas_call(
        body, name="ada_bwd", out_shape=jax.ShapeDtypeStruct((nl, D_MODEL, ncol), F32),
        compiler_params=_params(VMEM_BIG),
    )(c_all, dmod_cols)


def _adamw(gslots, w, m, v, name):
    k, r, c = gslots.shape
    tr = _tile(r, 256) if r % 256 == 0 else r
    c1 = 1.0 - ADAM_B1 ** ADAM_STEP
    c2 = 1.0 - ADAM_B2 ** ADAM_STEP

    def body(g_ref, w_ref, m_ref, v_ref, go_ref, d_ref, mo_ref, vo_ref):
        g = g_ref[0]
        for slot in range(1, k):
            g = g + g_ref[slot]
        mn = ADAM_B1 * m_ref[...] + (1.0 - ADAM_B1) * g
        vn = ADAM_B2 * v_ref[...] + (1.0 - ADAM_B2) * jnp.square(g)
        go_ref[...] = g
        mo_ref[...] = mn
        vo_ref[...] = vn
        d_ref[...] = -ADAM_LR * ((mn / c1) / (jnp.sqrt(vn / c2) + ADAM_EPS) + ADAM_WD * w_ref[...])

    row = pl.BlockSpec((tr, c), lambda i: (i, 0))
    shp = jax.ShapeDtypeStruct((r, c), F32)
    return pl.pallas_call(
        body, name=name, grid=(r // tr,),
        in_specs=[pl.BlockSpec((k, tr, c), lambda i: (0, i, 0)), row, row, row],
        out_specs=[row, row, row, row], out_shape=[shp, shp, shp, shp],
        compiler_params=_params(VMEM_BIG),
    )(gslots, w, m, v)


def _adamw_any(gslots, w, m, v, name):
    shape = w.shape
    two_d = (-1, shape[-1])
    k = gslots.shape[0]
    outs = _adamw(gslots.reshape((k,) + w.reshape(two_d).shape), w.reshape(two_d), m.reshape(two_d),
                  v.reshape(two_d), name)
    return tuple(o.reshape(shape) for o in outs)


def _cols_from_slots(g):
    g = jnp.moveaxis(g, 0, -2)
    return g.reshape(g.shape[:-2] + (g.shape[-2] * g.shape[-1],))


def _rows_from_slots(g):
    g = jnp.moveaxis(g, 0, -3)
    return g.reshape(g.shape[:-3] + (g.shape[-3] * g.shape[-2], g.shape[-1]))


def _col_slots(g):
    cs = g.shape[-1] // N_DEV
    return jnp.moveaxis(g.reshape(g.shape[:-1] + (N_DEV, cs)), -2, 0)


def _row_slots(g):
    rs = g.shape[-2] // N_DEV
    return jnp.moveaxis(g.reshape(g.shape[:-2] + (N_DEV, rs, g.shape[-1])), -3, 0)


def _gather_cols(w, name, dtype=None):
    return _cols_from_slots(_all_gather(w.astype(dtype or MXU_DTYPE), name))


def _gather_rows(w, name):
    return _rows_from_slots(_all_gather(_mx(w), name))


def _scatter_cols(g, name):
    return _all_to_all(_col_slots(g), name)


def _scatter_rows(g, name):
    return _all_to_all(_row_slots(g), name)


def kernel(x, c, ada_w, ada_b, ln_mix_g, ln_mix_b, ln_mlp_g, ln_mlp_b, mlp_w1, mlp_w2, fox_w_in, fox_b_f, fox_w_o, ssm_w_in, ssm_conv_w, ssm_conv_b, ssm_dt_bias, ssm_a_log, ssm_d, ssm_norm_w, ssm_w_out, loss_target, m_ada_w, m_ada_b, m_ln_mix_g, m_ln_mix_b, m_ln_mlp_g, m_ln_mlp_b, m_mlp_w1, m_mlp_w2, m_fox_w_in, m_fox_b_f, m_fox_w_o, m_ssm_w_in, m_ssm_conv_w, m_ssm_conv_b, m_ssm_dt_bias, m_ssm_a_log, m_ssm_d, m_ssm_norm_w, m_ssm_w_out, v_ada_w, v_ada_b, v_ln_mix_g, v_ln_mix_b, v_ln_mlp_g, v_ln_mlp_b, v_mlp_w1, v_mlp_w2, v_fox_w_in, v_fox_b_f, v_fox_w_o, v_ssm_w_in, v_ssm_conv_w, v_ssm_conv_b, v_ssm_dt_bias, v_ssm_a_log, v_ssm_d, v_ssm_norm_w, v_ssm_w_out):
    me = 4 * lax.axis_index("x") + 2 * lax.axis_index("y") + lax.axis_index("c")
    xs = x[0]
    target = loss_target[0]
    d = D_MODEL

    c_all = _all_gather(c, "gather_c").reshape(N_DEV, d)
    c_all = jnp.pad(c_all, ((0, N_DEV), (0, 0)))
    ncol = ada_w.shape[-1]
    ada_b_mine = lax.dynamic_slice_in_dim(ada_b, me * ncol, ncol, axis=1)
    mod_cols = _ada_fwd(c_all, ada_w, ada_b_mine)
    mod_all = _all_gather(mod_cols, "gather_mod")
    mod = lax.dynamic_index_in_dim(mod_all, me, axis=2, keepdims=False)
    mod = jnp.moveaxis(mod, 0, 1).reshape(DEPTH, 6, d)

    def pv_rows(*rows):
        return jnp.pad(jnp.stack(rows), ((0, 8 - len(rows)), (0, 0)))

    fw = _fox_weights(_gather_cols(fox_w_in, "gather_fox_in")[0], _gather_rows(fox_w_o, "gather_fox_o")[0])
    conv_w = _gather_cols(ssm_conv_w, "gather_conv_w", F32)[0]
    small_vec = jnp.concatenate([ssm_conv_b[0], ssm_norm_w[0]]).reshape(1, -1)
    small_all = _all_gather(small_vec.astype(F32), "gather_conv_b").reshape(N_DEV, -1)
    conv_b = small_all[:, :SSM_CONV_DIM // N_DEV].reshape(-1)
    norm_w = small_all[:, SSM_CONV_DIM // N_DEV:].reshape(-1)
    cw8 = jnp.pad(conv_w, ((0, 8 - SSM_CONV), (0, 0)))
    cb8 = _rows8(conv_b)
    nw8 = _rows8(norm_w)
    bf8 = _rows8(jnp.pad(fox_b_f[0], (0, LANES - FOX_HEADS)))
    par8 = jnp.pad(jnp.stack([jnp.pad(p[0], (0, LANES - SSM_HEADS)) for p in (ssm_dt_bias, ssm_a_log, ssm_d)]),
                   ((0, 5), (0, 0)))

    sh_a, sc_a, g_a, sh_m, sc_m, g_m = [mod[:, k] for k in range(6)]
    u0 = _modulate(xs, pv_rows(1.0 + sc_a[0], sh_a[0]), "modulate0")
    y0, fres, gathered = _fox_fwd(u0, fw, bf8, [(False, _mx(w)) for w in (mlp_w1, mlp_w2, ssm_w_in, ssm_w_out)])
    w1 = _cols_from_slots(gathered[0])
    w2 = _rows_from_slots(gathered[1])
    sw = _ssd_weights(_cols_from_slots(gathered[2])[0], _rows_from_slots(gathered[3])[0])
    pv0 = pv_rows(1.0 + g_a[0], ln_mix_g[0], ln_mix_b[0], 1.0 + sc_m[0], sh_m[0])
    x1, u1 = _ln_fwd(xs, y0, pv0, "ln_mix0")
    y1, (h0, a0) = _mlp_fwd(u1, w1[0], w2[0], "0")
    pv1 = pv_rows(1.0 + g_m[0], ln_mlp_g[0], ln_mlp_b[0], 1.0 + sc_a[1], sh_a[1])
    x2, u2 = _ln_fwd(x1, y1, pv1, "ln_mlp0")
    y2, sres = _ssd_fwd(u2, sw, cw8, cb8, par8, nw8)
    pv2 = pv_rows(1.0 + g_a[1], ln_mix_g[1], ln_mix_b[1], 1.0 + sc_m[1], sh_m[1])
    x3, u3 = _ln_fwd(x2, y2, pv2, "ln_mix1")
    y3, (h1, a1) = _mlp_fwd(u3, w1[1], w2[1], "1")
    pv3 = pv_rows(1.0 + g_m[1], ln_mlp_g[1], ln_mlp_b[1])

    dx3, dy3, s3 = _ln_bwd(x3, y3, pv3, "ln_mlp1_bwd", target=target)
    loss = lax.psum(s3[5, 0], ("x", "y", "c"))
    du3, dw1_1, dw2_1 = _mlp_bwd(dy3, u3, h1, a1, w1[1].T, w2[1].T, "1")
    dx2, dy2, s2 = _ln_bwd(x2, y2, pv2, "ln_mix1_bwd", dxo=dx3, du=du3)
    du2, d_ssm_in, d_ssm_out, ssm_small = _ssd_bwd(dy2, u2, sw, cw8, cb8, par8, nw8, sres)
    dx1, dy1, s1 = _ln_bwd(x1, y1, pv1, "ln_mlp0_bwd", dxo=dx2, du=du2)
    du1, dw1_0, dw2_0 = _mlp_bwd(dy1, u1, h0, a0, w1[0].T, w2[0].T, "0")
    dx0, dy0, s0 = _ln_bwd(xs, y0, pv0, "ln_mix0_bwd", dxo=dx1, du=du1)
    early = [(True, _col_slots(jnp.stack([dw1_0, dw1_1]))), (True, _row_slots(jnp.stack([dw2_0, dw2_1]))),
             (True, _col_slots(d_ssm_in[None])), (True, _row_slots(d_ssm_out[None])),
             (True, _col_slots(ssm_small["conv_w"][None])), (True, _col_slots(ssm_small["conv_b"][None])),
             (True, _col_slots(ssm_small["norm_w"][None]))]
    du0, d_fox_in, d_fox_o, fox_sums, exchanged = _fox_bwd(dy0, u0, fw, bf8, fres, early)
    grad_x, sx = _mod_bwd(dx0, du0, xs, pv_rows(1.0 + sc_a[0], sh_a[0]), "modulate0_bwd")

    dmod = jnp.stack([
        jnp.stack([sx[1], sx[0], s0[4], s0[1], s0[0], s1[4]]),
        jnp.stack([s1[1], s1[0], s2[4], s2[1], s2[0], s3[4]]),
    ]).reshape(DEPTH, 6 * d)

    def pad_rows(v):
        v = v.reshape(-1, LANES) if v.size % LANES == 0 else jnp.pad(v.reshape(-1), (0, LANES - v.size)).reshape(1, LANES)
        return jnp.pad(v, ((0, (-v.shape[0]) % 8), (0, 0)))

    small_parts = [dmod, jnp.stack([s0[2], s2[2]]), jnp.stack([s0[3], s2[3]]), jnp.stack([s1[2], s3[2]]),
                   jnp.stack([s1[3], s3[3]]), fox_sums[0, :FOX_HEADS], ssm_small["dt_bias"], ssm_small["a_log"],
                   ssm_small["d"]]
    packed = [pad_rows(p) for p in small_parts]
    offsets = np.cumsum([0] + [p.shape[0] for p in packed])
    small_all_g = _all_gather(jnp.concatenate(packed, axis=0), "gather_small_grads")

    def unpack(idx, shape):
        n = int(np.prod(shape))
        blk = small_all_g[:, offsets[idx]:offsets[idx + 1]].reshape(N_DEV, -1)[:, :n]
        return blk.reshape((N_DEV,) + tuple(shape))

    dmod_all = unpack(0, (DEPTH, 6 * d))
    dmod_cols = lax.dynamic_slice_in_dim(dmod_all, me * ncol, ncol, axis=2)
    dmod_cols = jnp.pad(jnp.moveaxis(dmod_cols, 0, 1), ((0, 0), (0, N_DEV), (0, 0)))
    g_ada_w = _ada_bwd(c_all, dmod_cols)

    shards = dict(
        mlp_w1=exchanged[0], mlp_w2=exchanged[1], ssm_w_in=exchanged[2], ssm_w_out=exchanged[3],
        ssm_conv_w=exchanged[4], ssm_conv_b=exchanged[5], ssm_norm_w=exchanged[6],
        fox_w_in=_scatter_cols(d_fox_in[None], "scatter_fox_in"), fox_w_o=_scatter_rows(d_fox_o[None], "scatter_fox_o"),
        ada_w=g_ada_w[None], ada_b=dmod_all,
        ln_mix_g=unpack(1, (DEPTH, d)), ln_mix_b=unpack(2, (DEPTH, d)),
        ln_mlp_g=unpack(3, (DEPTH, d)), ln_mlp_b=unpack(4, (DEPTH, d)),
        fox_b_f=unpack(5, (1, FOX_HEADS)), ssm_dt_bias=unpack(6, (1, SSM_HEADS)),
        ssm_a_log=unpack(7, (1, SSM_HEADS)), ssm_d=unpack(8, (1, SSM_HEADS)),
    )
    weights = dict(ada_w=ada_w, ada_b=ada_b, ln_mix_g=ln_mix_g, ln_mix_b=ln_mix_b, ln_mlp_g=ln_mlp_g, ln_mlp_b=ln_mlp_b,
                   mlp_w1=mlp_w1, mlp_w2=mlp_w2, fox_w_in=fox_w_in, fox_b_f=fox_b_f, fox_w_o=fox_w_o, ssm_w_in=ssm_w_in,
                   ssm_conv_w=ssm_conv_w, ssm_conv_b=ssm_conv_b, ssm_dt_bias=ssm_dt_bias, ssm_a_log=ssm_a_log,
                   ssm_d=ssm_d, ssm_norm_w=ssm_norm_w, ssm_w_out=ssm_w_out)
    mom1 = dict(ada_w=m_ada_w, ada_b=m_ada_b, ln_mix_g=m_ln_mix_g, ln_mix_b=m_ln_mix_b, ln_mlp_g=m_ln_mlp_g,
                ln_mlp_b=m_ln_mlp_b, mlp_w1=m_mlp_w1, mlp_w2=m_mlp_w2, fox_w_in=m_fox_w_in, fox_b_f=m_fox_b_f,
                fox_w_o=m_fox_w_o, ssm_w_in=m_ssm_w_in, ssm_conv_w=m_ssm_conv_w, ssm_conv_b=m_ssm_conv_b,
                ssm_dt_bias=m_ssm_dt_bias, ssm_a_log=m_ssm_a_log, ssm_d=m_ssm_d, ssm_norm_w=m_ssm_norm_w,
                ssm_w_out=m_ssm_w_out)
    mom2 = dict(ada_w=v_ada_w, ada_b=v_ada_b, ln_mix_g=v_ln_mix_g, ln_mix_b=v_ln_mix_b, ln_mlp_g=v_ln_mlp_g,
                ln_mlp_b=v_ln_mlp_b, mlp_w1=v_mlp_w1, mlp_w2=v_mlp_w2, fox_w_in=v_fox_w_in, fox_b_f=v_fox_b_f,
                fox_w_o=v_fox_w_o, ssm_w_in=v_ssm_w_in, ssm_conv_w=v_ssm_conv_w, ssm_conv_b=v_ssm_conv_b,
                ssm_dt_bias=v_ssm_dt_bias, ssm_a_log=v_ssm_a_log, ssm_d=v_ssm_d, ssm_norm_w=v_ssm_norm_w,
                ssm_w_out=v_ssm_w_out)
    names = list(weights)
    stepped = {n: _adamw_any(shards[n], weights[n], mom1[n], mom2[n], f"adamw_{n}") for n in names}
    return (loss, grad_x[None], *[stepped[n][0] for n in names], *[stepped[n][1] for n in names],
            *[stepped[n][2] for n in names], *[stepped[n][3] for n in names])
```

```python
import numpy as np
import jax
import jax.numpy as jnp
from jax import lax
from jax.experimental import pallas as pl
from jax.experimental.pallas import tpu as pltpu

F32 = jnp.float32
MXU_DTYPE = jnp.bfloat16
HIGHEST = lax.Precision.HIGHEST

N_DEV = 8
D_MODEL = 1024
DEPTH = 2
FOX_HEADS = 16
FOX_HEAD_DIM = 64
D_FF = 4096
SSM_D_INNER = 2048
SSM_HEADS = 32
SSM_GROUPS = 8
SSM_STATE = 128
SSM_CHUNK = 128
SSM_CONV = 4
SSM_CONV_DIM = 4096
GROUP_W = SSM_D_INNER // SSM_GROUPS
LN_EPS = 1e-5
RMS_EPS = 1e-5
ALPHA = (2.0 * DEPTH) ** 0.25
LANES = 128
SUBLANES = 8

ADAM_LR = 0.001
ADAM_B1 = 0.9
ADAM_B2 = 0.999
ADAM_EPS = 1e-08
ADAM_WD = 0.01
ADAM_STEP = 10

NN = (((1,), (0,)), ((), ()))
NT = (((1,), (1,)), ((), ()))
TN = (((0,), (0,)), ((), ()))

VMEM_BIG = 56 * 1024 * 1024


def _dot(a, b, dims=NN, precision=None):
    return lax.dot_general(a, b, dims, precision=precision, preferred_element_type=F32)


def _mx(v):
    return v.astype(MXU_DTYPE)


def _params(vmem=None):
    return pltpu.CompilerParams(vmem_limit_bytes=vmem) if vmem else None


def _all_gather(x, name):
    def body(x_ref, out_ref, send_sems, recv_sems, local_sem):
        xi, yi, ci = lax.axis_index("x"), lax.axis_index("y"), lax.axis_index("c")
        me, sibling = (xi, yi, ci), (xi, yi, 1 - ci)
        chips = [(1 - xi, yi), (xi, 1 - yi), (1 - xi, 1 - yi)]

        def slot(px, py, pc):
            return out_ref.at[4 * px + 2 * py + pc]

        def copy(k, block, to, src=None):
            return pltpu.make_async_remote_copy(
                src_ref=slot(*block) if src is None else src, dst_ref=slot(*block),
                send_sem=send_sems.at[k], recv_sem=recv_sems.at[k],
                device_id=to, device_id_type=pl.DeviceIdType.MESH)

        mine = pltpu.make_async_copy(x_ref, slot(*me), local_sem)
        mine.start()
        first = [copy(0, me, sibling, src=x_ref)]
        first += [copy(1 + j, me, (*chip, ci), src=x_ref) for j, chip in enumerate(chips)]
        for cp in first:
            cp.start()
        passed = [copy(4 + j, (*chip, ci), sibling) for j, chip in enumerate(chips)]
        for j, chip in enumerate(chips):
            copy(1 + j, (*chip, ci), me).wait_recv()
            passed[j].start()
        copy(0, sibling, me).wait_recv()
        for j, chip in enumerate(chips):
            copy(4 + j, (*chip, 1 - ci), me).wait_recv()
        for cp in first + passed:
            cp.wait_send()
        mine.wait()

    return pl.pallas_call(
        body, name=name,
        out_shape=jax.ShapeDtypeStruct((N_DEV,) + x.shape, x.dtype),
        in_specs=[pl.BlockSpec(memory_space=pl.ANY)],
        out_specs=pl.BlockSpec(memory_space=pl.ANY),
        scratch_shapes=[pltpu.SemaphoreType.DMA((7,)), pltpu.SemaphoreType.DMA((7,)),
                        pltpu.SemaphoreType.DMA],
    )(x)


EXCHANGE_PIECES = 128


def _direct_copies(scatter, x_ref, out_ref, send_sems, recv_sems, local_sems, n, piece=None):
    xi, yi, ci = lax.axis_index("x"), lax.axis_index("y"), lax.axis_index("c")
    me = 4 * xi + 2 * yi + ci

    def part(ref):
        return ref if piece is None else ref.at[piece]

    local = pltpu.make_async_copy(part(x_ref.at[me] if scatter else x_ref), part(out_ref.at[me]), local_sems.at[n])
    remote = []
    for k in range(1, N_DEV):
        px = 1 - xi if k & 4 else xi
        py = 1 - yi if k & 2 else yi
        pc = 1 - ci if k & 1 else ci
        remote.append(pltpu.make_async_remote_copy(
            src_ref=part(x_ref.at[4 * px + 2 * py + pc] if scatter else x_ref), dst_ref=part(out_ref.at[me]),
            send_sem=send_sems.at[7 * n + k - 1], recv_sem=recv_sems.at[7 * n + k - 1],
            device_id=(px, py, pc), device_id_type=pl.DeviceIdType.MESH))
    return local, remote


def _in_pieces(carry, steps):
    out = []
    for scatter, a in carry:
        body = a.shape[1:] if scatter else a.shape
        rows = int(np.prod(body[:-1])) if len(body) > 1 else 1
        align = SUBLANES * (4 // a.dtype.itemsize)
        pieces = 1
        while (pieces * 2 <= min(EXCHANGE_PIECES, steps // 2) and rows % (pieces * 2 * align) == 0):
            pieces *= 2
        shape = (pieces, rows // pieces, body[-1])
        out.append((scatter, a.reshape(((N_DEV,) if scatter else ()) + shape), pieces, max(1, (steps // 2) // pieces)))
    return out


def _exchange_shapes(carry):
    return [jax.ShapeDtypeStruct(c[1].shape if c[0] else (N_DEV,) + c[1].shape, c[1].dtype) for c in carry]


def _exchange_sems(carry):
    n = max(len(carry), 1)
    return [pltpu.SemaphoreType.DMA((7 * n,)), pltpu.SemaphoreType.DMA((7 * n,)), pltpu.SemaphoreType.DMA((n,))]


def _exchange_start(carry, srcs, dsts, sems, step=None):
    for n, entry in enumerate(carry):
        scatter = entry[0]
        pieces, stride = (entry[2], entry[3]) if len(entry) > 2 else (1, 1)
        local, remote = _direct_copies(scatter, srcs[n], dsts[n], *sems, n)
        if step is None:
            local.start()
            for cp in remote:
                cp.start()
            continue

        @pl.when(step == 0)
        def _():
            local.start()
            if pieces == 1:
                for cp in remote:
                    cp.start()

        if pieces > 1:
            @pl.when((step % stride == 0) & (step // stride < pieces))
            def _():
                for cp in _direct_copies(scatter, srcs[n], dsts[n], *sems, n, piece=step // stride)[1]:
                    cp.start()


def _exchange_wait(carry, srcs, dsts, sems):
    for n, entry in enumerate(carry):
        local, remote = _direct_copies(entry[0], srcs[n], dsts[n], *sems, n)
        for cp in remote:
            cp.wait()
        local.wait()


def _all_to_all(x, name):
    carry = [(True, x)]

    def body(x_ref, out_ref, *sems):
        _exchange_start(carry, [x_ref], [out_ref], sems)
        _exchange_wait(carry, [x_ref], [out_ref], sems)

    return pl.pallas_call(
        body, name=name,
        out_shape=jax.ShapeDtypeStruct(x.shape, x.dtype),
        in_specs=[pl.BlockSpec(memory_space=pl.ANY)],
        out_specs=pl.BlockSpec(memory_space=pl.ANY),
        scratch_shapes=_exchange_sems(carry),
    )(x)


def _mm(name, a, b, *, grid, a_spec, b_spec, dims, k_axis, outs, acc=None, extras=(), epi=None, vmem=None):
    nk = grid[k_axis]
    n_ex, n_out = len(extras), len(outs)

    def body(*refs):
        a_ref, b_ref = refs[0], refs[1]
        ex = refs[2:2 + n_ex]
        out = refs[2 + n_ex:2 + n_ex + n_out]

        def finish(val):
            if epi is None:
                out[0][...] = val.astype(out[0].dtype)
            else:
                epi(val, ex, out)

        part = _dot(a_ref[...], b_ref[...], dims)
        if nk == 1:
            finish(part)
        else:
            acc_ref = refs[2 + n_ex + n_out]
            k = pl.program_id(k_axis)

            @pl.when(k == 0)
            def _():
                acc_ref[...] = part

            @pl.when(k > 0)
            def _():
                acc_ref[...] += part

            @pl.when(k == nk - 1)
            def _():
                finish(acc_ref[...])

    return pl.pallas_call(
        body, name=name, grid=grid,
        in_specs=[a_spec, b_spec] + [s for _, s in extras],
        out_specs=[s for _, s in outs],
        out_shape=[o for o, _ in outs],
        scratch_shapes=[pltpu.VMEM(acc, F32)] if nk > 1 else [],
        compiler_params=_params(vmem),
    )(a, b, *[e for e, _ in extras])


def _tile(n, t):
    t = min(n, t)
    assert n % t == 0, (n, t)
    return t


def _mm_nn(name, a, b, out_dtype, *, addend=None, tm=1024, tn=1024, tk=1024, epi=None, extras=(), outs=None):
    m, kk = a.shape
    n = b.shape[1]
    tm, tn, tk = _tile(m, tm), _tile(n, tn), _tile(kk, tk)
    o_spec = pl.BlockSpec((tm, tn), lambda i, j, k: (i, j))
    if outs is None:
        outs = [(jax.ShapeDtypeStruct((m, n), out_dtype), o_spec)]
    extras = list(extras)
    if addend is not None:
        extras = [(addend, o_spec)] + extras

        def epi(val, ex, out):
            out[0][...] = (val + ex[0][...].astype(F32)).astype(out[0].dtype)

    res = _mm(name, a, b, grid=(m // tm, n // tn, kk // tk),
              a_spec=pl.BlockSpec((tm, tk), lambda i, j, k: (i, k)),
              b_spec=pl.BlockSpec((tk, tn), lambda i, j, k: (k, j)),
              dims=NN, k_axis=2, acc=(tm, tn), outs=outs, extras=extras, epi=epi, vmem=VMEM_BIG)
    return res[0] if len(res) == 1 else res


def _mm_tn(name, a, b, out_dtype=F32, *, tm=1024, tn=1024, tk=1024):
    kk, m = a.shape
    n = b.shape[1]
    tm, tn, tk = _tile(m, tm), _tile(n, tn), _tile(kk, tk)
    res = _mm(name, a, b, grid=(m // tm, n // tn, kk // tk),
              a_spec=pl.BlockSpec((tk, tm), lambda i, j, k: (k, i)),
              b_spec=pl.BlockSpec((tk, tn), lambda i, j, k: (k, j)),
              dims=TN, k_axis=2, acc=(tm, tn),
              outs=[(jax.ShapeDtypeStruct((m, n), out_dtype), pl.BlockSpec((tm, tn), lambda i, j, k: (i, j)))],
              vmem=VMEM_BIG)
    return res[0]


def _row_block(s):
    return _tile(s, 512)


def _modulate(x, pv, name):
    s, d = x.shape
    tb = _row_block(s)

    def body(x_ref, pv_ref, u_ref):
        u_ref[...] = _mx(x_ref[...] * pv_ref[0:1, :] + pv_ref[1:2, :])

    return pl.pallas_call(
        body, name=name, grid=(s // tb,),
        in_specs=[pl.BlockSpec((tb, d), lambda i: (i, 0)), pl.BlockSpec((8, d), lambda i: (0, 0))],
        out_specs=pl.BlockSpec((tb, d), lambda i: (i, 0)),
        out_shape=jax.ShapeDtypeStruct((s, d), MXU_DTYPE),
    )(x, pv)


def _ln_stats(r):
    mu = jnp.mean(r, axis=-1, keepdims=True)
    xc = r - mu
    var = jnp.mean(xc * xc, axis=-1, keepdims=True)
    rstd = lax.rsqrt(var + LN_EPS)
    return xc * rstd, rstd


def _ln_fwd(xin, y, pv, name):
    s, d = xin.shape
    tb = _row_block(s)

    def body(x_ref, y_ref, pv_ref, xo_ref, u_ref):
        r = ALPHA * x_ref[...] + pv_ref[0:1, :] * y_ref[...]
        xhat, _ = _ln_stats(r)
        xo = xhat * pv_ref[1:2, :] + pv_ref[2:3, :]
        xo_ref[...] = xo
        u_ref[...] = _mx(xo * pv_ref[3:4, :] + pv_ref[4:5, :])

    row = pl.BlockSpec((tb, d), lambda i: (i, 0))
    return pl.pallas_call(
        body, name=name, grid=(s // tb,),
        in_specs=[row, row, pl.BlockSpec((8, d), lambda i: (0, 0))],
        out_specs=[row, row],
        out_shape=[jax.ShapeDtypeStruct((s, d), F32), jax.ShapeDtypeStruct((s, d), MXU_DTYPE)],
    )(xin, y, pv)


def _ln_bwd(xin, y, pv, name, *, dxo=None, du=None, target=None):
    s, d = xin.shape
    tb = _row_block(s)
    nb = s // tb
    loss_mode = target is not None

    def body(*refs):
        if loss_mode:
            x_ref, y_ref, pv_ref, t_ref, dxin_ref, dy_ref, sums_ref = refs
        else:
            x_ref, y_ref, pv_ref, dxo_ref, du_ref, dxin_ref, dy_ref, sums_ref = refs
        i = pl.program_id(0)

        @pl.when(i == 0)
        def _():
            sums_ref[...] = jnp.zeros_like(sums_ref)

        yv = y_ref[...]
        r = ALPHA * x_ref[...] + pv_ref[0:1, :] * yv
        xhat, rstd = _ln_stats(r)
        xo = xhat * pv_ref[1:2, :] + pv_ref[2:3, :]
        if loss_mode:
            diff = xo - t_ref[...]
            dxo_v = diff * (1.0 / d)
            sums_ref[5:6, :] += jnp.sum(diff * diff, axis=0, keepdims=True) * (0.5 / d)
        else:
            duv = du_ref[...]
            dxo_v = dxo_ref[...] + duv * pv_ref[3:4, :]
            sums_ref[0:1, :] += jnp.sum(duv * xo, axis=0, keepdims=True)
            sums_ref[1:2, :] += jnp.sum(duv, axis=0, keepdims=True)
        sums_ref[2:3, :] += jnp.sum(dxo_v * xhat, axis=0, keepdims=True)
        sums_ref[3:4, :] += jnp.sum(dxo_v, axis=0, keepdims=True)
        dxh = dxo_v * pv_ref[1:2, :]
        dr = rstd * (dxh - jnp.mean(dxh, axis=-1, keepdims=True)
                     - xhat * jnp.mean(dxh * xhat, axis=-1, keepdims=True))
        sums_ref[4:5, :] += jnp.sum(dr * yv, axis=0, keepdims=True)
        dxin_ref[...] = ALPHA * dr
        dy_ref[...] = _mx(pv_ref[0:1, :] * dr)
        if loss_mode:
            @pl.when(i == nb - 1)
            def _():
                sums_ref[5:6, :] = jnp.broadcast_to(jnp.sum(sums_ref[5:6, :], axis=-1, keepdims=True), (1, d))

    row = pl.BlockSpec((tb, d), lambda i: (i, 0))
    par = pl.BlockSpec((8, d), lambda i: (0, 0))
    ins = [xin, y, pv] + ([target] if loss_mode else [dxo, du])
    return pl.pallas_call(
        body, name=name, grid=(nb,),
        in_specs=[row, row, par] + [row] * (len(ins) - 3),
        out_specs=[row, row, par],
        out_shape=[jax.ShapeDtypeStruct((s, d), F32), jax.ShapeDtypeStruct((s, d), MXU_DTYPE),
                   jax.ShapeDtypeStruct((8, d), F32)],
    )(*ins)


def _mod_bwd(dx_direct, du, x, pv, name):
    s, d = x.shape
    tb = _row_block(s)

    def body(dxd_ref, du_ref, x_ref, pv_ref, dx_ref, sums_ref):
        @pl.when(pl.program_id(0) == 0)
        def _():
            sums_ref[...] = jnp.zeros_like(sums_ref)

        duv = du_ref[...]
        dx_ref[...] = dxd_ref[...] + duv * pv_ref[0:1, :]
        sums_ref[0:1, :] += jnp.sum(duv * x_ref[...], axis=0, keepdims=True)
        sums_ref[1:2, :] += jnp.sum(duv, axis=0, keepdims=True)

    row = pl.BlockSpec((tb, d), lambda i: (i, 0))
    par = pl.BlockSpec((8, d), lambda i: (0, 0))
    return pl.pallas_call(
        body, name=name, grid=(s // tb,),
        in_specs=[row, row, row, par], out_specs=[row, par],
        out_shape=[jax.ShapeDtypeStruct((s, d), F32), jax.ShapeDtypeStruct((8, d), F32)],
    )(dx_direct, du, x, pv)


def _mlp_fwd(u, w1, w2, tag):
    s = u.shape[0]

    def epi(val, ex, out):
        out[0][...] = _mx(val)
        out[1][...] = _mx(jnp.square(jnp.maximum(val, 0.0)))

    tm, tn = _tile(s, 1024), 1024
    spec = pl.BlockSpec((tm, tn), lambda i, j, k: (i, j))
    shp = jax.ShapeDtypeStruct((s, D_FF), MXU_DTYPE)
    h, a = _mm_nn(f"mlp_up{tag}", u, w1, None, epi=epi, outs=[(shp, spec), (shp, spec)], tn=tn)
    y = _mm_nn(f"mlp_down{tag}", a, w2, F32)
    return y, (h, a)


def _mlp_bwd(dy, u, h, a, w1t, w2t, tag):
    s = u.shape[0]
    tm, tn = _tile(s, 1024), 1024
    spec = pl.BlockSpec((tm, tn), lambda i, j, k: (i, j))

    def epi(val, ex, out):
        out[0][...] = _mx(val * (2.0 * jnp.maximum(ex[0][...].astype(F32), 0.0)))

    dh = _mm_nn(f"mlp_dh{tag}", dy, w2t, None, epi=epi, extras=[(h, spec)],
                outs=[(jax.ShapeDtypeStruct((s, D_FF), MXU_DTYPE), spec)], tn=tn)
    du = _mm_nn(f"mlp_du{tag}", dh, w1t, F32)
    dw2 = _mm_tn(f"mlp_dw2{tag}", a, dy)
    dw1 = _mm_tn(f"mlp_dw1{tag}", u, dh)
    return du, dw1, dw2


FOX_T = 1024
BIAS_Q = (64, 65, 66)
BIAS_K = (67, 68, 69)
ONES_V = 64

def _fox_constants():
    selq = np.zeros((FOX_HEADS, 512, LANES), np.float32)
    selk = np.zeros((FOX_HEADS, 512, LANES), np.float32)
    selv = np.zeros((2, LANES, LANES), np.float32)
    put = np.zeros((2, 2, LANES, LANES), np.float32)
    for h in range(FOX_HEADS):
        off = FOX_HEAD_DIM * (h % 2)
        for dd in range(FOX_HEAD_DIM):
            selq[h, off + dd, dd] = FOX_HEAD_DIM ** -0.5
            selk[h, off + dd, dd] = 1.0
        for piece in range(3):
            selq[h, LANES * (1 + piece) + h, BIAS_Q[piece]] = 1.0
            selk[h, LANES * (1 + piece) + h, BIAS_K[piece]] = -1.0
    for par in range(2):
        for dd in range(FOX_HEAD_DIM):
            selv[par, FOX_HEAD_DIM * par + dd, dd] = 1.0
            put[par, 0, dd, FOX_HEAD_DIM * par + dd] = FOX_HEAD_DIM ** -0.5
            put[par, 1, dd, FOX_HEAD_DIM * par + dd] = 1.0
    return selq, selk, selv, put


def _pairs(nb, by_key):
    if by_key:
        pr = [(i, j) for j in range(nb) for i in range(j, nb)]
    else:
        pr = [(i, j) for i in range(nb) for j in range(i + 1)]
    return (np.array([p[0] for p in pr], np.int32), np.array([p[1] for p in pr], np.int32))


def _fox_prep(qkv, f, bf):
    s = qkv.shape[0]
    t = _tile(s, FOX_T)
    nb = s // t
    selq, selk, selv, _ = _fox_constants()

    def body(q_ref, k_ref, v_ref, f_ref, bf_ref, selq_ref, selk_ref, selv_ref,
             qa_ref, qat_ref, ka_ref, kat_ref, va_ref, vat_ref, parts_ref, carry_ref):
        i, h = pl.program_id(0), pl.program_id(1)
        lane = lax.broadcasted_iota(jnp.int32, (1, LANES), 1)

        @pl.when(h == 0)
        def _():
            @pl.when(i == 0)
            def _():
                carry_ref[...] = jnp.zeros_like(carry_ref)

            lf = jnp.where(lane < FOX_HEADS, jax.nn.log_sigmoid(f_ref[...] + bf_ref[0:1, :]), 0.0)
            tri = (lax.broadcasted_iota(jnp.int32, (t, t), 0) >= lax.broadcasted_iota(jnp.int32, (t, t), 1)).astype(F32)
            cum = _dot(tri, lf, precision=HIGHEST) + carry_ref[0:1, :]
            carry_ref[0:1, :] = cum[t - 1:t, :]
            hi = _mx(cum)
            r1 = cum - hi.astype(F32)
            mid = _mx(r1)
            parts_ref[:, 0:LANES] = hi
            parts_ref[:, LANES:2 * LANES] = mid
            parts_ref[:, 2 * LANES:3 * LANES] = _mx(r1 - mid.astype(F32))

        parts = parts_ref[...]
        qa = _dot(jnp.concatenate([q_ref[...], parts], axis=1), selq_ref[...])
        qa = qa + jnp.where((lane >= BIAS_K[0]) & (lane <= BIAS_K[2]), 1.0, 0.0)
        ka = _dot(jnp.concatenate([k_ref[...], parts], axis=1), selk_ref[...])
        ka = ka + jnp.where((lane >= BIAS_Q[0]) & (lane <= BIAS_Q[2]), 1.0, 0.0)
        va = _dot(v_ref[...], selv_ref[...]) + jnp.where(lane == ONES_V, 1.0, 0.0)
        qa_ref[...] = _mx(qa)
        qat_ref[...] = _mx(qa.T)
        ka_ref[...] = _mx(ka)
        kat_ref[...] = _mx(ka.T)
        va_ref[...] = _mx(va)
        vat_ref[...] = _mx(va.T)

    rows = jax.ShapeDtypeStruct((FOX_HEADS, nb, t, LANES), MXU_DTYPE)
    cols = jax.ShapeDtypeStruct((FOX_HEADS, nb, LANES, t), MXU_DTYPE)
    rspec = pl.BlockSpec((None, None, t, LANES), lambda i, h: (h, i, 0, 0))
    cspec = pl.BlockSpec((None, None, LANES, t), lambda i, h: (h, i, 0, 0))
    npair = FOX_HEADS // 2
    return pl.pallas_call(
        body, name="fox_prep", grid=(nb, FOX_HEADS),
        in_specs=[pl.BlockSpec((t, LANES), lambda i, h: (i, h // 2)),
                  pl.BlockSpec((t, LANES), lambda i, h: (i, npair + h // 2)),
                  pl.BlockSpec((t, LANES), lambda i, h: (i, 2 * npair + h // 2)),
                  pl.BlockSpec((t, LANES), lambda i, h: (i, 0)),
                  pl.BlockSpec((8, LANES), lambda i, h: (0, 0)),
                  pl.BlockSpec((None, 512, LANES), lambda i, h: (h, 0, 0)),
                  pl.BlockSpec((None, 512, LANES), lambda i, h: (h, 0, 0)),
                  pl.BlockSpec((None, LANES, LANES), lambda i, h: (h % 2, 0, 0))],
        out_specs=[rspec, cspec, rspec, cspec, rspec, cspec],
        out_shape=[rows, cols, rows, cols, rows, cols],
        scratch_shapes=[pltpu.VMEM((t, 3 * LANES), MXU_DTYPE), pltpu.VMEM((8, LANES), F32)],
        compiler_params=_params(VMEM_BIG),
    )(qkv, qkv, qkv, f, bf, _mx(jnp.asarray(selq)), _mx(jnp.asarray(selk)), _mx(jnp.asarray(selv)))


def _causal_allow(t):
    return lax.broadcasted_iota(jnp.int32, (t, t), 0) <= lax.broadcasted_iota(jnp.int32, (t, t), 1)


def _fox_attn_fwd(qat, ka, vat, carry=()):
    heads, nb, _, t = qat.shape
    im, jm = _pairs(nb, by_key=False)
    npairs = len(im)
    whole = _exchange_shapes(list(carry))
    carry = _in_pieces(list(carry), heads * npairs)
    nc = len(carry)

    def body(im_ref, jm_ref, qat_ref, ka_ref, vat_ref, *rest):
        srcs, (ot_ref, lse_ref), dsts = rest[:nc], rest[nc:nc + 2], rest[nc + 2:2 * nc + 2]
        acc_ref, m_ref = rest[2 * nc + 2:2 * nc + 4]
        sems = rest[2 * nc + 4:]
        h, p = pl.program_id(0), pl.program_id(1)
        i, j = im_ref[p], jm_ref[p]

        if nc:
            _exchange_start(carry, srcs, dsts, sems, step=h * npairs + p)

        @pl.when(j == 0)
        def _():
            m_ref[...] = jnp.full_like(m_ref, -jnp.inf)
            acc_ref[...] = jnp.zeros_like(acc_ref)

        def step(diagonal):
            st = _dot(ka_ref[...], qat_ref[...])
            if diagonal:
                st = jnp.where(_causal_allow(t), st, -jnp.inf)
            m_old = m_ref[...]
            m_new = jnp.maximum(m_old, jnp.max(st, axis=0, keepdims=True))
            pt = jnp.exp(st - m_new)
            acc_ref[...] = acc_ref[...] * jnp.exp(m_old - m_new) + _dot(vat_ref[...], _mx(pt))
            m_ref[...] = m_new

        @pl.when(j < i)
        def _():
            step(False)

        @pl.when(j == i)
        def _():
            step(True)
            acc = acc_ref[...]
            denom = acc[ONES_V:ONES_V + 1, :]
            ot_ref[...] = _mx(acc / denom)
            lse_ref[...] = m_ref[...] + jnp.log(denom)

        if nc:
            @pl.when((h == heads - 1) & (p == npairs - 1))
            def _():
                _exchange_wait(carry, srcs, dsts, sems)

    anywhere = pl.BlockSpec(memory_space=pl.ANY)
    qspec = pl.BlockSpec((None, None, LANES, t), lambda h, p, im, jm: (h, im[p], 0, 0))
    grid_spec = pltpu.PrefetchScalarGridSpec(
        num_scalar_prefetch=2, grid=(heads, npairs),
        in_specs=[qspec,
                  pl.BlockSpec((None, None, t, LANES), lambda h, p, im, jm: (h, jm[p], 0, 0)),
                  pl.BlockSpec((None, None, LANES, t), lambda h, p, im, jm: (h, jm[p], 0, 0))] + [anywhere] * nc,
        out_specs=[qspec, pl.BlockSpec((None, None, 1, t), lambda h, p, im, jm: (h, im[p], 0, 0))] + [anywhere] * nc,
        scratch_shapes=[pltpu.VMEM((LANES, t), F32), pltpu.VMEM((1, t), F32)] + (_exchange_sems(carry) if nc else []))
    res = pl.pallas_call(
        body, name="fox_attn_fwd", grid_spec=grid_spec,
        out_shape=[jax.ShapeDtypeStruct((heads, nb, LANES, t), MXU_DTYPE),
                   jax.ShapeDtypeStruct((heads, nb, 1, t), F32)] + _exchange_shapes(carry),
        compiler_params=_params(VMEM_BIG),
    )(jnp.asarray(im), jnp.asarray(jm), qat, ka, vat, *[c[1] for c in carry])
    return res[0], res[1], [r.reshape(w.shape) for r, w in zip(res[2:], whole)]


def _fox_attn_bwd(qa, qat, ka, kat, va, ot, lse, do, dot_, carry=()):
    heads, nb, t, _ = qa.shape
    im, jm = _pairs(nb, by_key=True)
    npairs = len(im)
    whole = _exchange_shapes(list(carry))
    carry = _in_pieces(list(carry), heads * npairs)
    nc = len(carry)

    def body(im_ref, jm_ref, qa_ref, qat_ref, ka_ref, kat_ref, va_ref, ot_ref, lse_ref, do_ref, dot_ref, *rest):
        srcs, (dqt_ref, dka_ref, dva_ref), dsts = rest[:nc], rest[nc:nc + 3], rest[nc + 3:2 * nc + 3]
        sems = rest[2 * nc + 3:]
        h, p = pl.program_id(0), pl.program_id(1)
        i, j = im_ref[p], jm_ref[p]

        if nc:
            _exchange_start(carry, srcs, dsts, sems, step=h * npairs + p)

        @pl.when(p == 0)
        def _():
            dqt_ref[...] = jnp.zeros_like(dqt_ref)

        def step(diagonal):
            st = _dot(ka_ref[...], qat_ref[...])
            dot_v = dot_ref[...]
            delta = jnp.sum(ot_ref[...].astype(F32) * dot_v.astype(F32), axis=0, keepdims=True)
            pt = jnp.exp(st - lse_ref[...])
            if diagonal:
                pt = jnp.where(_causal_allow(t), pt, 0.0)
            dsm = _mx(pt * (_dot(va_ref[...], dot_v) - delta))
            upd_v = _dot(_mx(pt), do_ref[...])
            upd_k = _dot(dsm, qa_ref[...])
            if diagonal:
                dva_ref[...] = upd_v
                dka_ref[...] = upd_k
            else:
                dva_ref[...] += upd_v
                dka_ref[...] += upd_k
            dqt_ref[i] += _dot(kat_ref[...], dsm)

        @pl.when(i == j)
        def _():
            step(True)

        @pl.when(i > j)
        def _():
            step(False)

        if nc:
            @pl.when((h == heads - 1) & (p == npairs - 1))
            def _():
                _exchange_wait(carry, srcs, dsts, sems)

    def at_q(shape):
        return pl.BlockSpec((None, None) + shape, lambda h, p, im, jm: (h, im[p], 0, 0))

    def at_k(shape):
        return pl.BlockSpec((None, None) + shape, lambda h, p, im, jm: (h, jm[p], 0, 0))

    anywhere = pl.BlockSpec(memory_space=pl.ANY)
    grid_spec = pltpu.PrefetchScalarGridSpec(
        num_scalar_prefetch=2, grid=(heads, npairs),
        in_specs=[at_q((t, LANES)), at_q((LANES, t)), at_k((t, LANES)), at_k((LANES, t)), at_k((t, LANES)),
                  at_q((LANES, t)), at_q((1, t)), at_q((t, LANES)), at_q((LANES, t))] + [anywhere] * nc,
        out_specs=[pl.BlockSpec((None, nb, LANES, t), lambda h, p, im, jm: (h, 0, 0, 0)),
                   at_k((t, LANES)), at_k((t, LANES))] + [anywhere] * nc,
        scratch_shapes=_exchange_sems(carry) if nc else [])
    res = pl.pallas_call(
        body, name="fox_attn_bwd", grid_spec=grid_spec,
        out_shape=[jax.ShapeDtypeStruct((heads, nb, LANES, t), F32),
                   jax.ShapeDtypeStruct((heads, nb, t, LANES), F32),
                   jax.ShapeDtypeStruct((heads, nb, t, LANES), F32)] + _exchange_shapes(carry),
        compiler_params=_params(VMEM_BIG),
    )(jnp.asarray(im), jnp.asarray(jm), qa, qat, ka, kat, va, ot, lse, do, dot_, *[c[1] for c in carry])
    return res[0], res[1], res[2], [r.reshape(w.shape) for r, w in zip(res[3:], whole)]


def _fox_post(dqt, dka, dva, f, bf):
    heads, nb, t, _ = dka.shape
    s = nb * t
    _, _, _, put = _fox_constants()
    npair = heads // 2

    def body(dqt_ref, dka_ref, dva_ref, f_ref, bf_ref, put_ref, dq_ref, dk_ref, dv_ref, df_ref, sums_ref,
             dc_ref, carry_ref):
        i, h = pl.program_id(0), pl.program_id(1)

        @pl.when((i == 0) & (h == 0))
        def _():
            carry_ref[...] = jnp.zeros_like(carry_ref)
            sums_ref[...] = jnp.zeros_like(sums_ref)

        @pl.when(h == 0)
        def _():
            dc_ref[...] = jnp.zeros_like(dc_ref)

        dqt_v = dqt_ref[...]
        dka_v = dka_ref[...]
        term_q = _dot(_mx(dqt_v), put_ref[0], TN)
        term_k = _dot(_mx(dka_v), put_ref[1])
        term_v = _dot(_mx(dva_ref[...]), put_ref[1])

        @pl.when(h % 2 == 0)
        def _():
            dq_ref[...] = _mx(term_q)
            dk_ref[...] = _mx(term_k)
            dv_ref[...] = _mx(term_v)

        @pl.when(h % 2 == 1)
        def _():
            dq_ref[...] += _mx(term_q)
            dk_ref[...] += _mx(term_k)
            dv_ref[...] += _mx(term_v)

        dcum = dqt_v[BIAS_Q[0]:BIAS_Q[0] + 1, :] - dka_v.T[BIAS_K[0]:BIAS_K[0] + 1, :]
        head_row = lax.broadcasted_iota(jnp.int32, (heads, 1), 0) == h
        dc_ref[...] += jnp.where(head_row, dcum, 0.0)

        @pl.when(h == heads - 1)
        def _():
            later = (lax.broadcasted_iota(jnp.int32, (t, t), 0) >= lax.broadcasted_iota(jnp.int32, (t, t), 1)).astype(F32)
            dlf_t = _dot(dc_ref[...], later, precision=HIGHEST) + carry_ref[:, 0:1]
            carry_ref[...] = jnp.broadcast_to(dlf_t[:, 0:1], carry_ref.shape)
            dlf = jnp.concatenate([dlf_t, jnp.zeros((LANES - heads, t), F32)], axis=0).T
            lane = lax.broadcasted_iota(jnp.int32, (1, LANES), 1)
            df = jnp.where(lane < heads, dlf * jax.nn.sigmoid(-(f_ref[...] + bf_ref[0:1, :])), 0.0)
            df_ref[...] = _mx(df)
            sums_ref[0:1, :] += jnp.sum(df, axis=0, keepdims=True)

    rev = lambda i: nb - 1 - i
    pair_spec = pl.BlockSpec((t, LANES), lambda i, h: (rev(i), h // 2))
    blk = pl.BlockSpec((t, LANES), lambda i, h: (rev(i), 0))
    hd = jax.ShapeDtypeStruct((s, D_MODEL), MXU_DTYPE)
    return pl.pallas_call(
        body, name="fox_post", grid=(nb, heads),
        in_specs=[pl.BlockSpec((None, None, LANES, t), lambda i, h: (h, rev(i), 0, 0)),
                  pl.BlockSpec((None, None, t, LANES), lambda i, h: (h, rev(i), 0, 0)),
                  pl.BlockSpec((None, None, t, LANES), lambda i, h: (h, rev(i), 0, 0)),
                  blk, pl.BlockSpec((8, LANES), lambda i, h: (0, 0)),
                  pl.BlockSpec((None, 2, LANES, LANES), lambda i, h: (h % 2, 0, 0, 0))],
        out_specs=[pair_spec, pair_spec, pair_spec, blk, pl.BlockSpec((8, LANES), lambda i, h: (0, 0))],
        out_shape=[hd, hd, hd, jax.ShapeDtypeStruct((s, LANES), MXU_DTYPE), jax.ShapeDtypeStruct((8, LANES), F32)],
        scratch_shapes=[pltpu.VMEM((heads, t), F32), pltpu.VMEM((heads, LANES), F32)],
        compiler_params=_params(VMEM_BIG),
    )(dqt, dka, dva, f, bf, _mx(jnp.asarray(put)))


def _fox_weights(w_in, w_o):
    wqkv = w_in[:, :3 * D_MODEL]
    wf = jnp.pad(w_in[:, 3 * D_MODEL:], ((0, 0), (0, LANES - FOX_HEADS)))
    wo_heads = w_o.reshape(FOX_HEADS, FOX_HEAD_DIM, D_MODEL)
    wo_a = jnp.pad(wo_heads, ((0, 0), (0, LANES - FOX_HEAD_DIM), (0, 0)))
    wo_rows = wo_a.reshape(FOX_HEADS * LANES, D_MODEL)
    return dict(wqkv=wqkv, wf=wf, wqkv_t=wqkv.T, wf_t=wf.T, wo_rows=wo_rows, wo_rows_t=wo_rows.T)


def _fox_out(ot, wo_rows):
    heads, nb, _, t = ot.shape

    def body(ot_ref, w_ref, y_ref):
        y_ref[...] = _dot(ot_ref[...].reshape(heads * LANES, t), w_ref[...], TN)

    return pl.pallas_call(
        body, name="fox_out", grid=(nb,),
        in_specs=[pl.BlockSpec((heads, None, LANES, t), lambda i: (0, i, 0, 0)),
                  pl.BlockSpec((heads * LANES, D_MODEL), lambda i: (0, 0))],
        out_specs=pl.BlockSpec((t, D_MODEL), lambda i: (i, 0)),
        out_shape=jax.ShapeDtypeStruct((nb * t, D_MODEL), F32),
        compiler_params=_params(VMEM_BIG),
    )(ot, wo_rows)


def _fox_do(dy, wo_rows_t, nb, t):
    heads = FOX_HEADS

    def body(dy_ref, w_ref, do_ref, dot_ref):
        val = _dot(dy_ref[...], w_ref[...])
        for h in range(heads):
            blk = val[:, h * LANES:(h + 1) * LANES]
            do_ref[h] = _mx(blk)
            dot_ref[h] = _mx(blk.T)

    return pl.pallas_call(
        body, name="fox_do", grid=(nb,),
        in_specs=[pl.BlockSpec((t, D_MODEL), lambda i: (i, 0)),
                  pl.BlockSpec((D_MODEL, heads * LANES), lambda i: (0, 0))],
        out_specs=[pl.BlockSpec((heads, None, t, LANES), lambda i: (0, i, 0, 0)),
                   pl.BlockSpec((heads, None, LANES, t), lambda i: (0, i, 0, 0))],
        out_shape=[jax.ShapeDtypeStruct((heads, nb, t, LANES), MXU_DTYPE),
                   jax.ShapeDtypeStruct((heads, nb, LANES, t), MXU_DTYPE)],
        compiler_params=_params(VMEM_BIG),
    )(dy, wo_rows_t)


def _fox_dwo(ot, dy):
    heads, nb, _, t = ot.shape

    def body(ot_ref, dy_ref, o_ref):
        part = _dot(ot_ref[...].reshape(heads * LANES, t), dy_ref[...])

        @pl.when(pl.program_id(0) == 0)
        def _():
            o_ref[...] = part

        @pl.when(pl.program_id(0) > 0)
        def _():
            o_ref[...] += part

    return pl.pallas_call(
        body, name="fox_dwo", grid=(nb,),
        in_specs=[pl.BlockSpec((heads, None, LANES, t), lambda i: (0, i, 0, 0)),
                  pl.BlockSpec((t, D_MODEL), lambda i: (i, 0))],
        out_specs=pl.BlockSpec((heads * LANES, D_MODEL), lambda i: (0, 0)),
        out_shape=jax.ShapeDtypeStruct((heads * LANES, D_MODEL), F32),
        compiler_params=_params(VMEM_BIG),
    )(ot, dy)


def _fox_fwd(u, w, bf, carry=()):
    qkv = _mm_nn("fox_qkv", u, w["wqkv"], MXU_DTYPE)
    f = _mm_nn("fox_f", u, w["wf"], F32)
    qa, qat, ka, kat, va, vat = _fox_prep(qkv, f, bf)
    ot, lse, carried = _fox_attn_fwd(qat, ka, vat, carry)
    y = _fox_out(ot, w["wo_rows"])
    return y, dict(f=f, qa=qa, qat=qat, ka=ka, kat=kat, va=va, ot=ot, lse=lse), carried


def _fox_bwd(dy, u, w, bf, res, carry=()):
    heads, nb, t, _ = res["qa"].shape
    do, dot_ = _fox_do(dy, w["wo_rows_t"], nb, t)
    dwo_a = _fox_dwo(res["ot"], dy).reshape(heads, LANES, D_MODEL)
    dqt, dka, dva, carried = _fox_attn_bwd(res["qa"], res["qat"], res["ka"], res["kat"], res["va"], res["ot"],
                                           res["lse"], do, dot_, carry)
    dq, dk, dv, df, sums = _fox_post(dqt, dka, dva, res["f"], bf)
    wt = w["wqkv_t"]
    du = _mm_nn("fox_du_q", dq, wt[:D_MODEL], F32)
    du = _mm_nn("fox_du_k", dk, wt[D_MODEL:2 * D_MODEL], F32, addend=du)
    du = _mm_nn("fox_du_v", dv, wt[2 * D_MODEL:], F32, addend=du)
    du = _mm_nn("fox_du_f", df, w["wf_t"], F32, addend=du)
    dw_in = jnp.concatenate(
        [_mm_tn("fox_dw_q", u, dq), _mm_tn("fox_dw_k", u, dk), _mm_tn("fox_dw_v", u, dv),
         _mm_tn("fox_dw_f", u, df)[:, :FOX_HEADS]], axis=1)
    dw_o = dwo_a[:, :FOX_HEAD_DIM, :].reshape(D_MODEL, D_MODEL)
    return du, dw_in, dw_o, sums, carried


def _dsilu(v):
    sg = jax.nn.sigmoid(v)
    return sg * (1.0 + v * (1.0 - sg))


def _conv_taps(scr_ref, w_ref, rows, base):
    acc = None
    for k in range(SSM_CONV):
        term = scr_ref[pl.ds(base - (SSM_CONV - 1) + k, rows), :] * w_ref[k:k + 1, :]
        acc = term if acc is None else acc + term
    return acc


def _conv_fwd(zx, cw, cb):
    s = zx.shape[0]
    tb = _tile(s, 512)
    half = SSM_CONV_DIM // 2
    hb = tb // SUBLANES

    def body(x_ref, halo_ref, w_ref, b_ref, o_ref, scr_ref):
        i = pl.program_id(0)
        scr_ref[pl.ds(0, SUBLANES), :] = jnp.where(i > 0, halo_ref[...], 0.0)
        scr_ref[pl.ds(SUBLANES, tb), :] = x_ref[...]
        o_ref[...] = jax.nn.silu(_conv_taps(scr_ref, w_ref, tb, SUBLANES) + b_ref[0:1, :])

    return pl.pallas_call(
        body, name="ssd_conv_fwd", grid=(s // tb, 2),
        in_specs=[pl.BlockSpec((tb, half), lambda i, j: (i, 1 + j)),
                  pl.BlockSpec((SUBLANES, half), lambda i, j: (jnp.maximum(i * hb - 1, 0), 1 + j)),
                  pl.BlockSpec((8, half), lambda i, j: (0, j)),
                  pl.BlockSpec((8, half), lambda i, j: (0, j))],
        out_specs=pl.BlockSpec((tb, half), lambda i, j: (i, j)),
        out_shape=jax.ShapeDtypeStruct((s, SSM_CONV_DIM), F32),
        scratch_shapes=[pltpu.VMEM((tb + SUBLANES, half), F32)],
    )(zx, zx, cw, cb)


def _conv_bwd_pre(zx, dxc, cw, cb):
    s = zx.shape[0]
    tb = _tile(s, 512)
    half = SSM_CONV_DIM // 2
    hb = tb // SUBLANES

    def body(x_ref, halo_ref, d_ref, w_ref, b_ref, o_ref, sums_ref, scr_ref):
        i = pl.program_id(1)

        @pl.when(i == 0)
        def _():
            sums_ref[...] = jnp.zeros_like(sums_ref)

        scr_ref[pl.ds(0, SUBLANES), :] = jnp.where(i > 0, halo_ref[...], 0.0)
        scr_ref[pl.ds(SUBLANES, tb), :] = x_ref[...]
        pre = _conv_taps(scr_ref, w_ref, tb, SUBLANES) + b_ref[0:1, :]
        dpre = d_ref[...] * _dsilu(pre)
        o_ref[...] = dpre
        for k in range(SSM_CONV):
            shifted = scr_ref[pl.ds(SUBLANES - (SSM_CONV - 1) + k, tb), :]
            sums_ref[k:k + 1, :] += jnp.sum(dpre * shifted, axis=0, keepdims=True)
        sums_ref[SSM_CONV:SSM_CONV + 1, :] += jnp.sum(dpre, axis=0, keepdims=True)

    return pl.pallas_call(
        body, name="ssd_conv_bwd_pre", grid=(2, s // tb),
        in_specs=[pl.BlockSpec((tb, half), lambda j, i: (i, 1 + j)),
                  pl.BlockSpec((SUBLANES, half), lambda j, i: (jnp.maximum(i * hb - 1, 0), 1 + j)),
                  pl.BlockSpec((tb, half), lambda j, i: (i, j)),
                  pl.BlockSpec((8, half), lambda j, i: (0, j)),
                  pl.BlockSpec((8, half), lambda j, i: (0, j))],
        out_specs=[pl.BlockSpec((tb, half), lambda j, i: (i, j)),
                   pl.BlockSpec((8, half), lambda j, i: (0, j))],
        out_shape=[jax.ShapeDtypeStruct((s, SSM_CONV_DIM), F32), jax.ShapeDtypeStruct((8, SSM_CONV_DIM), F32)],
        scratch_shapes=[pltpu.VMEM((tb + SUBLANES, half), F32)],
    )(zx, zx, dxc, cw, cb)


def _conv_bwd_x(dpre, cw):
    s = dpre.shape[0]
    tb = _tile(s, 512)
    hb = tb // SUBLANES
    nb = s // tb

    def body(d_ref, halo_ref, w_ref, o_ref, scr_ref):
        i = pl.program_id(0)
        scr_ref[pl.ds(0, tb), :] = d_ref[...]
        scr_ref[pl.ds(tb, SUBLANES), :] = jnp.where(i < nb - 1, halo_ref[...], 0.0)
        acc = None
        for k in range(SSM_CONV):
            term = scr_ref[pl.ds(SSM_CONV - 1 - k, tb), :] * w_ref[k:k + 1, :]
            acc = term if acc is None else acc + term
        o_ref[...] = _mx(acc)

    return pl.pallas_call(
        body, name="ssd_conv_bwd_x", grid=(nb,),
        in_specs=[pl.BlockSpec((tb, SSM_CONV_DIM), lambda i: (i, 0)),
                  pl.BlockSpec((SUBLANES, SSM_CONV_DIM), lambda i: (jnp.minimum((i + 1) * hb, s // SUBLANES - 1), 0)),
                  pl.BlockSpec((8, SSM_CONV_DIM), lambda i: (0, 0))],
        out_specs=pl.BlockSpec((tb, SSM_CONV_DIM), lambda i: (i, 0)),
        out_shape=jax.ShapeDtypeStruct((s, SSM_CONV_DIM), MXU_DTYPE),
        scratch_shapes=[pltpu.VMEM((tb + SUBLANES, SSM_CONV_DIM), F32)],
        compiler_params=_params(VMEM_BIG),
    )(dpre, dpre, cw)


def _expand_constants():
    ex = np.zeros((LANES, SSM_D_INNER), np.float32)
    for h in range(SSM_HEADS):
        ex[h, h * 64:(h + 1) * 64] = 1.0
    return ex, np.ascontiguousarray(ex.T)


def _ssd_common(dtr_ref, par_ref, ex_ref, xc_ref):
    lc = SSM_CHUNK
    lane = lax.broadcasted_iota(jnp.int32, (1, LANES), 1)
    is_head = lane < SSM_HEADS
    par = par_ref[...]
    pre = dtr_ref[...] + par[0:1, :]
    dt = jnp.where(is_head, jax.nn.softplus(pre), 0.0)
    a = jnp.where(is_head, -jnp.exp(par[1:2, :]), 0.0)
    tri_b = lax.broadcasted_iota(jnp.int32, (lc, lc), 0) >= lax.broadcasted_iota(jnp.int32, (lc, lc), 1)
    tri = tri_b.astype(F32)
    da = dt * a
    acs = _dot(tri, da, precision=HIGHEST)
    acs_t = _dot(da, tri, (((0,), (1,)), ((), ())), precision=HIGHEST)
    ex = ex_ref[...]
    dt_x = _dot(dt, ex, precision=HIGHEST)
    acs_x = _dot(acs, ex, precision=HIGHEST)
    d_x = _dot(par, ex, precision=HIGHEST)[2:3, :]
    last_x = acs_x[lc - 1:lc, :]
    xs = xc_ref[:, 0:SSM_D_INNER]
    return dict(pre=pre, dt=dt, a=a, tri_b=tri_b, tri=tri, acs=acs, acs_t=acs_t, dt_x=dt_x, d_x=d_x, xs=xs,
                xdt=xs * dt_x, e_x=jnp.exp(acs_x), dte_x=jnp.exp(last_x - acs_x), cd_x=jnp.exp(last_x),
                is_head=is_head)


def _decay_in(q, h):
    seg = q["acs"][:, h:h + 1] - q["acs_t"][h:h + 1, :]
    return jnp.exp(jnp.where(q["tri_b"], seg, -jnp.inf))


def _ssd_scan_fwd(xc, dtr, par):
    s = xc.shape[0]
    lc = SSM_CHUNK
    nc = s // lc
    ex, _ = _expand_constants()

    def body(xc_ref, dtr_ref, par_ref, ex_ref, y_ref, prev_ref, st_ref):
        @pl.when(pl.program_id(0) == 0)
        def _():
            st_ref[...] = jnp.zeros_like(st_ref)

        q = _ssd_common(dtr_ref, par_ref, ex_ref, xc_ref)
        lane = lax.broadcasted_iota(jnp.int32, (1, LANES), 1)
        for g in range(SSM_GROUPS):
            sl = slice(g * GROUP_W, (g + 1) * GROUP_W)
            bg = _mx(xc_ref[:, SSM_D_INNER + g * SSM_STATE:SSM_D_INNER + (g + 1) * SSM_STATE])
            cg = _mx(xc_ref[:, SSM_D_INNER + (SSM_GROUPS + g) * SSM_STATE:SSM_D_INNER + (SSM_GROUPS + g + 1) * SSM_STATE])
            gm = _dot(cg, bg, NT)
            prev = st_ref[g]
            prev_ref[g] = prev
            yoff = _dot(cg, _mx(prev)) * q["e_x"][:, sl]
            st_ref[g] = prev * q["cd_x"][:, sl] + _dot(bg, _mx(q["xdt"][:, sl] * q["dte_x"][:, sl]), TN)
            pairs = []
            for pr in range(2):
                xp = _mx(q["xdt"][:, g * GROUP_W + pr * LANES:g * GROUP_W + (pr + 1) * LANES])
                both = [_dot(_mx(gm * _decay_in(q, 4 * g + 2 * pr + r2)), xp) for r2 in range(2)]
                pairs.append(jnp.where(lane < 64, both[0], both[1]))
            y_ref[:, sl] = jnp.concatenate(pairs, axis=1) + yoff + q["xs"][:, sl] * q["d_x"][:, sl]

    return pl.pallas_call(
        body, name="ssd_scan_fwd", grid=(nc,),
        in_specs=[pl.BlockSpec((lc, SSM_CONV_DIM), lambda c: (c, 0)),
                  pl.BlockSpec((lc, LANES), lambda c: (c, 0)),
                  pl.BlockSpec((8, LANES), lambda c: (0, 0)),
                  pl.BlockSpec((LANES, SSM_D_INNER), lambda c: (0, 0))],
        out_specs=[pl.BlockSpec((lc, SSM_D_INNER), lambda c: (c, 0)),
                   pl.BlockSpec((None, SSM_GROUPS, SSM_STATE, GROUP_W), lambda c: (c, 0, 0, 0))],
        out_shape=[jax.ShapeDtypeStruct((s, SSM_D_INNER), F32),
                   jax.ShapeDtypeStruct((nc, SSM_GROUPS, SSM_STATE, GROUP_W), F32)],
        scratch_shapes=[pltpu.VMEM((SSM_GROUPS, SSM_STATE, GROUP_W), F32)],
        compiler_params=_params(VMEM_BIG),
    )(xc, dtr, par, jnp.asarray(ex))


def _ssd_scan_bwd(dy, xc, dtr, par, prev):
    s = xc.shape[0]
    lc = SSM_CHUNK
    nc = s // lc
    ex, ex_t = _expand_constants()

    def body(dy_ref, xc_ref, dtr_ref, par_ref, prev_ref, ex_ref, ext_ref, dxc_ref, ddtr_ref, sums_ref,
             gst_ref, tacs_ref, tdt_ref, tdd_ref):
        @pl.when(pl.program_id(0) == 0)
        def _():
            gst_ref[...] = jnp.zeros_like(gst_ref)
            sums_ref[...] = jnp.zeros_like(sums_ref)

        q = _ssd_common(dtr_ref, par_ref, ex_ref, xc_ref)
        lane = lax.broadcasted_iota(jnp.int32, (1, LANES), 1)
        row = lax.broadcasted_iota(jnp.int32, (lc, 1), 0)
        dacs_rows = jnp.zeros((lc, LANES), F32)
        dacs_cols_t = jnp.zeros((LANES, lc), F32)
        for g in range(SSM_GROUPS):
            sl = slice(g * GROUP_W, (g + 1) * GROUP_W)
            b_lo = SSM_D_INNER + g * SSM_STATE
            c_lo = SSM_D_INNER + (SSM_GROUPS + g) * SSM_STATE
            bg = _mx(xc_ref[:, b_lo:b_lo + SSM_STATE])
            cg = _mx(xc_ref[:, c_lo:c_lo + SSM_STATE])
            dyg = dy_ref[:, sl]
            xsg, xdtg = q["xs"][:, sl], q["xdt"][:, sl]
            eg, dteg, cdg = q["e_x"][:, sl], q["dte_x"][:, sl], q["cd_x"][:, sl]
            prevg = prev_ref[g]
            gs = gst_ref[g]
            prevm, gsm = _mx(prevg), _mx(gs)
            tdd_ref[:, sl] = dyg * xsg
            dxs = dyg * q["d_x"][:, sl]
            t_acs = dyg * _dot(cg, prevm) * eg
            dcp = _mx(dyg * eg)
            dc = _dot(dcp, prevm, NT)
            dprev = _dot(cg, dcp, TN)
            db = _dot(_mx(xdtg * dteg), gsm, NT)
            dx2 = _dot(bg, gsm)
            dxdt = dx2 * dteg
            ddte = dx2 * xdtg * dteg
            t_acs = t_acs - ddte
            last = (jnp.sum(ddte, axis=0, keepdims=True)
                    + jnp.sum(gs * prevg, axis=0, keepdims=True) * cdg)
            gm = _dot(cg, bg, NT)
            dgm = jnp.zeros((lc, lc), F32)
            pair_dx = []
            for pr in range(2):
                lo = g * GROUP_W + pr * LANES
                xp = _mx(q["xdt"][:, lo:lo + LANES])
                dyp = dy_ref[:, lo:lo + LANES]
                both = []
                for r2 in range(2):
                    h = 4 * g + 2 * pr + r2
                    mine = (lane >= 64 * r2) & (lane < 64 * (r2 + 1))
                    lm = _decay_in(q, h)
                    m = gm * lm
                    dm = _dot(_mx(jnp.where(mine, dyp, 0.0)), xp, NT)
                    dgm = dgm + dm * lm
                    w = dm * m
                    dacs_rows = dacs_rows + jnp.sum(w, axis=1, keepdims=True) * (lane == h).astype(F32)
                    head_row = (lax.broadcasted_iota(jnp.int32, (LANES, 1), 0) == h).astype(F32)
                    dacs_cols_t = dacs_cols_t + head_row * jnp.sum(w, axis=0, keepdims=True)
                    both.append(_dot(_mx(m), _mx(dyp), TN))
                pair_dx.append(jnp.where(lane < 64, both[0], both[1]))
            dxdt = dxdt + jnp.concatenate(pair_dx, axis=1)
            dgmm = _mx(dgm)
            dc = dc + _dot(dgmm, bg)
            db = db + _dot(dgmm, cg, TN)
            dxs = dxs + dxdt * q["dt_x"][:, sl]
            tdt_ref[:, sl] = dxdt * xsg
            tacs_ref[:, sl] = t_acs + jnp.where(row == lc - 1, last, 0.0)
            dxc_ref[:, sl] = dxs
            dxc_ref[:, b_lo:b_lo + SSM_STATE] = db
            dxc_ref[:, c_lo:c_lo + SSM_STATE] = dc
            gst_ref[g] = gs * cdg + dprev
        ext = ext_ref[...]
        dacs = _dot(tacs_ref[...], ext, precision=HIGHEST) + dacs_rows - dacs_cols_t.T
        dda = _dot(q["tri"], dacs, TN, precision=HIGHEST)
        ddt = dda * q["a"] + _dot(tdt_ref[...], ext, precision=HIGHEST)
        ddtr = jnp.where(q["is_head"], ddt * jax.nn.sigmoid(q["pre"]), 0.0)
        ddtr_ref[...] = _mx(ddtr)
        tdd = jnp.broadcast_to(jnp.sum(tdd_ref[...], axis=0, keepdims=True), (8, SSM_D_INNER))
        sums_ref[0:1, :] += jnp.sum(ddtr, axis=0, keepdims=True)
        sums_ref[1:2, :] += jnp.sum(dda * q["dt"], axis=0, keepdims=True) * q["a"]
        sums_ref[2:3, :] += _dot(tdd, ext, precision=HIGHEST)[0:1, :]

    rev = lambda c: nc - 1 - c
    wide = pltpu.VMEM((lc, SSM_D_INNER), F32)
    return pl.pallas_call(
        body, name="ssd_scan_bwd", grid=(nc,),
        in_specs=[pl.BlockSpec((lc, SSM_D_INNER), lambda c: (rev(c), 0)),
                  pl.BlockSpec((lc, SSM_CONV_DIM), lambda c: (rev(c), 0)),
                  pl.BlockSpec((lc, LANES), lambda c: (rev(c), 0)),
                  pl.BlockSpec((8, LANES), lambda c: (0, 0)),
                  pl.BlockSpec((None, SSM_GROUPS, SSM_STATE, GROUP_W), lambda c: (rev(c), 0, 0, 0)),
                  pl.BlockSpec((LANES, SSM_D_INNER), lambda c: (0, 0)),
                  pl.BlockSpec((SSM_D_INNER, LANES), lambda c: (0, 0))],
        out_specs=[pl.BlockSpec((lc, SSM_CONV_DIM), lambda c: (rev(c), 0)),
                   pl.BlockSpec((lc, LANES), lambda c: (rev(c), 0)),
                   pl.BlockSpec((8, LANES), lambda c: (0, 0))],
        out_shape=[jax.ShapeDtypeStruct((s, SSM_CONV_DIM), F32), jax.ShapeDtypeStruct((s, LANES), MXU_DTYPE),
                   jax.ShapeDtypeStruct((8, LANES), F32)],
        scratch_shapes=[pltpu.VMEM((SSM_GROUPS, SSM_STATE, GROUP_W), F32), wide, wide, wide],
        compiler_params=_params(VMEM_BIG),
    )(dy, xc, dtr, par, prev, jnp.asarray(ex), jnp.asarray(ex_t))


def _group_norm_parts(yv, zv):
    yg = yv * jax.nn.silu(zv)
    normed, rinvs = [], []
    for g in range(SSM_GROUPS):
        blk = yg[:, g * GROUP_W:(g + 1) * GROUP_W]
        rinv = lax.rsqrt(jnp.mean(blk * blk, axis=-1, keepdims=True) + RMS_EPS)
        normed.append(blk * rinv)
        rinvs.append(rinv)
    return normed, rinvs


def _gnorm_fwd(y, zx, nw):
    s = y.shape[0]
    tb = _tile(s, 512)

    def body(y_ref, z_ref, w_ref, o_ref):
        normed, _ = _group_norm_parts(y_ref[...], z_ref[...])
        for g in range(SSM_GROUPS):
            sl = slice(g * GROUP_W, (g + 1) * GROUP_W)
            o_ref[:, sl] = _mx(normed[g] * w_ref[0:1, sl])

    row = pl.BlockSpec((tb, SSM_D_INNER), lambda i: (i, 0))
    return pl.pallas_call(
        body, name="ssd_gnorm_fwd", grid=(s // tb,),
        in_specs=[row, row, pl.BlockSpec((8, SSM_D_INNER), lambda i: (0, 0))],
        out_specs=row, out_shape=jax.ShapeDtypeStruct((s, SSM_D_INNER), MXU_DTYPE),
    )(y, zx, nw)


def _gnorm_bwd(y, zx, nw, dyn):
    s = y.shape[0]
    tb = _tile(s, 512)

    def body(y_ref, z_ref, w_ref, d_ref, dy_ref, dz_ref, sums_ref):
        @pl.when(pl.program_id(0) == 0)
        def _():
            sums_ref[...] = jnp.zeros_like(sums_ref)

        yv, zv = y_ref[...], z_ref[...]
        normed, rinvs = _group_norm_parts(yv, zv)
        gate = jax.nn.silu(zv)
        dgate = _dsilu(zv)
        for g in range(SSM_GROUPS):
            sl = slice(g * GROUP_W, (g + 1) * GROUP_W)
            dv = d_ref[:, sl]
            n = normed[g]
            sums_ref[0:1, sl] += jnp.sum(dv * n, axis=0, keepdims=True)
            dn = dv * w_ref[0:1, sl]
            dyg = rinvs[g] * (dn - n * jnp.mean(dn * n, axis=-1, keepdims=True))
            dy_ref[:, sl] = dyg * gate[:, sl]
            dz_ref[:, sl] = _mx(dyg * yv[:, sl] * dgate[:, sl])

    row = pl.BlockSpec((tb, SSM_D_INNER), lambda i: (i, 0))
    par = pl.BlockSpec((8, SSM_D_INNER), lambda i: (0, 0))
    return pl.pallas_call(
        body, name="ssd_gnorm_bwd", grid=(s // tb,),
        in_specs=[row, row, par, row], out_specs=[row, row, par],
        out_shape=[jax.ShapeDtypeStruct((s, SSM_D_INNER), F32), jax.ShapeDtypeStruct((s, SSM_D_INNER), MXU_DTYPE),
                   jax.ShapeDtypeStruct((8, SSM_D_INNER), F32)],
    )(y, zx, nw, dyn)


def _rows8(v):
    v = v.reshape(1, -1)
    return jnp.pad(v, ((0, 7), (0, 0)))


def _ssd_weights(w_in, w_out):
    nzx = SSM_D_INNER + SSM_CONV_DIM
    wzx = w_in[:, :nzx]
    wdt = jnp.pad(w_in[:, nzx:], ((0, 0), (0, LANES - SSM_HEADS)))
    return dict(wzx=wzx, wdt=wdt, wzx_t=wzx.T, wdt_t=wdt.T, wout=w_out, wout_t=w_out.T)


def _ssd_fwd(u, w, cw, cb, par, nw):
    zx = _mm_nn("ssd_in_zx", u, w["wzx"], F32)
    dtr = _mm_nn("ssd_in_dt", u, w["wdt"], F32)
    xc = _conv_fwd(zx, cw, cb)
    y, prev = _ssd_scan_fwd(xc, dtr, par)
    yn = _gnorm_fwd(y, zx, nw)
    out = _mm_nn("ssd_out", yn, w["wout"], F32)
    return out, dict(zx=zx, dtr=dtr, xc=xc, y=y, prev=prev, yn=yn)


def _ssd_bwd(dy, u, w, cw, cb, par, nw, res):
    dyn = _mm_nn("ssd_dyn", dy, w["wout_t"], F32)
    dw_out = _mm_tn("ssd_dw_out", res["yn"], dy)
    dys, dz, nsum = _gnorm_bwd(res["y"], res["zx"], nw, dyn)
    dxc, ddtr, ssum = _ssd_scan_bwd(dys, res["xc"], res["dtr"], par, res["prev"])
    dpre, csum = _conv_bwd_pre(res["zx"], dxc, cw, cb)
    dxbc = _conv_bwd_x(dpre, cw)
    wt = w["wzx_t"]
    du = _mm_nn("ssd_du_z", dz, wt[:SSM_D_INNER], F32)
    du = _mm_nn("ssd_du_x", dxbc, wt[SSM_D_INNER:], F32, addend=du)
    du = _mm_nn("ssd_du_dt", ddtr, w["wdt_t"], F32, addend=du)
    dw_in = jnp.concatenate(
        [_mm_tn("ssd_dw_z", u, dz), _mm_tn("ssd_dw_x", u, dxbc), _mm_tn("ssd_dw_dt", u, ddtr)[:, :SSM_HEADS]], axis=1)
    small = dict(conv_w=csum[:SSM_CONV], conv_b=csum[SSM_CONV], dt_bias=ssum[0, :SSM_HEADS],
                 a_log=ssum[1, :SSM_HEADS], d=ssum[2, :SSM_HEADS], norm_w=nsum[0])
    return du, dw_in, dw_out, small


def _ada_fwd(c_all, ada_w, ada_b_mine):
    nl, _, ncol = ada_w.shape

    def body(c_ref, w_ref, b_ref, o_ref):
        cond = _mx(jax.nn.silu(c_ref[...]))
        for i in range(nl):
            o_ref[i] = _dot(cond, _mx(w_ref[i])) + b_ref[i:i + 1, :]

    return pl.pallas_call(
        body, name="ada_fwd", out_shape=jax.ShapeDtypeStruct((nl, 2 * N_DEV, ncol), F32),
        compiler_params=_params(VMEM_BIG),
    )(c_all, ada_w, ada_b_mine)


def _ada_bwd(c_all, dmod_cols):
    nl, _, ncol = dmod_cols.shape

    def body(c_ref, d_ref, o_ref):
        cond = _mx(jax.nn.silu(c_ref[...]))
        for i in range(nl):
            o_ref[i] = _dot(cond, _mx(d_ref[i]), TN)

    return pl.pallas_call(
        body, name="ada_bwd", out_shape=jax.ShapeDtypeStruct((nl, D_MODEL, ncol), F32),
        compiler_params=_params(VMEM_BIG),
    )(c_all, dmod_cols)


def _adamw(gslots, w, m, v, name):
    k, r, c = gslots.shape
    tr = _tile(r, 256) if r % 256 == 0 else r
    c1 = 1.0 - ADAM_B1 ** ADAM_STEP
    c2 = 1.0 - ADAM_B2 ** ADAM_STEP

    def body(g_ref, w_ref, m_ref, v_ref, go_ref, d_ref, mo_ref, vo_ref):
        g = g_ref[0]
        for slot in range(1, k):
            g = g + g_ref[slot]
        mn = ADAM_B1 * m_ref[...] + (1.0 - ADAM_B1) * g
        vn = ADAM_B2 * v_ref[...] + (1.0 - ADAM_B2) * jnp.square(g)
        go_ref[...] = g
        mo_ref[...] = mn
        vo_ref[...] = vn
        d_ref[...] = -ADAM_LR * ((mn / c1) / (jnp.sqrt(vn / c2) + ADAM_EPS) + ADAM_WD * w_ref[...])

    row = pl.BlockSpec((tr, c), lambda i: (i, 0))
    shp = jax.ShapeDtypeStruct((r, c), F32)
    return pl.pallas_call(
        body, name=name, grid=(r // tr,),
        in_specs=[pl.BlockSpec((k, tr, c), lambda i: (0, i, 0)), row, row, row],
        out_specs=[row, row, row, row], out_shape=[shp, shp, shp, shp],
        compiler_params=_params(VMEM_BIG),
    )(gslots, w, m, v)


def _adamw_any(gslots, w, m, v, name):
    shape = w.shape
    two_d = (-1, shape[-1])
    k = gslots.shape[0]
    outs = _adamw(gslots.reshape((k,) + w.reshape(two_d).shape), w.reshape(two_d), m.reshape(two_d),
                  v.reshape(two_d), name)
    return tuple(o.reshape(shape) for o in outs)


def _cols_from_slots(g):
    g = jnp.moveaxis(g, 0, -2)
    return g.reshape(g.shape[:-2] + (g.shape[-2] * g.shape[-1],))


def _rows_from_slots(g):
    g = jnp.moveaxis(g, 0, -3)
    return g.reshape(g.shape[:-3] + (g.shape[-3] * g.shape[-2], g.shape[-1]))


def _col_slots(g):
    cs = g.shape[-1] // N_DEV
    return jnp.moveaxis(g.reshape(g.shape[:-1] + (N_DEV, cs)), -2, 0)


def _row_slots(g):
    rs = g.shape[-2] // N_DEV
    return jnp.moveaxis(g.reshape(g.shape[:-2] + (N_DEV, rs, g.shape[-1])), -3, 0)


def _gather_cols(w, name, dtype=None):
    return _cols_from_slots(_all_gather(w.astype(dtype or MXU_DTYPE), name))


def _gather_rows(w, name):
    return _rows_from_slots(_all_gather(_mx(w), name))


def _scatter_cols(g, name):
    return _all_to_all(_col_slots(g), name)


def _scatter_rows(g, name):
    return _all_to_all(_row_slots(g), name)


def kernel(x, c, ada_w, ada_b, ln_mix_g, ln_mix_b, ln_mlp_g, ln_mlp_b, mlp_w1, mlp_w2, fox_w_in, fox_b_f, fox_w_o, ssm_w_in, ssm_conv_w, ssm_conv_b, ssm_dt_bias, ssm_a_log, ssm_d, ssm_norm_w, ssm_w_out, loss_target, m_ada_w, m_ada_b, m_ln_mix_g, m_ln_mix_b, m_ln_mlp_g, m_ln_mlp_b, m_mlp_w1, m_mlp_w2, m_fox_w_in, m_fox_b_f, m_fox_w_o, m_ssm_w_in, m_ssm_conv_w, m_ssm_conv_b, m_ssm_dt_bias, m_ssm_a_log, m_ssm_d, m_ssm_norm_w, m_ssm_w_out, v_ada_w, v_ada_b, v_ln_mix_g, v_ln_mix_b, v_ln_mlp_g, v_ln_mlp_b, v_mlp_w1, v_mlp_w2, v_fox_w_in, v_fox_b_f, v_fox_w_o, v_ssm_w_in, v_ssm_conv_w, v_ssm_conv_b, v_ssm_dt_bias, v_ssm_a_log, v_ssm_d, v_ssm_norm_w, v_ssm_w_out):
    me = 4 * lax.axis_index("x") + 2 * lax.axis_index("y") + lax.axis_index("c")
    xs = x[0]
    target = loss_target[0]
    d = D_MODEL

    c_all = _all_gather(c, "gather_c").reshape(N_DEV, d)
    c_all = jnp.pad(c_all, ((0, N_DEV), (0, 0)))
    ncol = ada_w.shape[-1]
    ada_b_mine = lax.dynamic_slice_in_dim(ada_b, me * ncol, ncol, axis=1)
    mod_cols = _ada_fwd(c_all, ada_w, ada_b_mine)
    mod_all = _all_gather(mod_cols, "gather_mod")
    mod = lax.dynamic_index_in_dim(mod_all, me, axis=2, keepdims=False)
    mod = jnp.moveaxis(mod, 0, 1).reshape(DEPTH, 6, d)

    def pv_rows(*rows):
        return jnp.pad(jnp.stack(rows), ((0, 8 - len(rows)), (0, 0)))

    fw = _fox_weights(_gather_cols(fox_w_in, "gather_fox_in")[0], _gather_rows(fox_w_o, "gather_fox_o")[0])
    conv_w = _gather_cols(ssm_conv_w, "gather_conv_w", F32)[0]
    small_vec = jnp.concatenate([ssm_conv_b[0], ssm_norm_w[0]]).reshape(1, -1)
    small_all = _all_gather(small_vec.astype(F32), "gather_conv_b").reshape(N_DEV, -1)
    conv_b = small_all[:, :SSM_CONV_DIM // N_DEV].reshape(-1)
    norm_w = small_all[:, SSM_CONV_DIM // N_DEV:].reshape(-1)
    cw8 = jnp.pad(conv_w, ((0, 8 - SSM_CONV), (0, 0)))
    cb8 = _rows8(conv_b)
    nw8 = _rows8(norm_w)
    bf8 = _rows8(jnp.pad(fox_b_f[0], (0, LANES - FOX_HEADS)))
    par8 = jnp.pad(jnp.stack([jnp.pad(p[0], (0, LANES - SSM_HEADS)) for p in (ssm_dt_bias, ssm_a_log, ssm_d)]),
                   ((0, 5), (0, 0)))

    sh_a, sc_a, g_a, sh_m, sc_m, g_m = [mod[:, k] for k in range(6)]
    u0 = _modulate(xs, pv_rows(1.0 + sc_a[0], sh_a[0]), "modulate0")
    y0, fres, gathered = _fox_fwd(u0, fw, bf8, [(False, _mx(w)) for w in (mlp_w1, mlp_w2, ssm_w_in, ssm_w_out)])
    w1 = _cols_from_slots(gathered[0])
    w2 = _rows_from_slots(gathered[1])
    sw = _ssd_weights(_cols_from_slots(gathered[2])[0], _rows_from_slots(gathered[3])[0])
    pv0 = pv_rows(1.0 + g_a[0], ln_mix_g[0], ln_mix_b[0], 1.0 + sc_m[0], sh_m[0])
    x1, u1 = _ln_fwd(xs, y0, pv0, "ln_mix0")
    y1, (h0, a0) = _mlp_fwd(u1, w1[0], w2[0], "0")
    pv1 = pv_rows(1.0 + g_m[0], ln_mlp_g[0], ln_mlp_b[0], 1.0 + sc_a[1], sh_a[1])
    x2, u2 = _ln_fwd(x1, y1, pv1, "ln_mlp0")
    y2, sres = _ssd_fwd(u2, sw, cw8, cb8, par8, nw8)
    pv2 = pv_rows(1.0 + g_a[1], ln_mix_g[1], ln_mix_b[1], 1.0 + sc_m[1], sh_m[1])
    x3, u3 = _ln_fwd(x2, y2, pv2, "ln_mix1")
    y3, (h1, a1) = _mlp_fwd(u3, w1[1], w2[1], "1")
    pv3 = pv_rows(1.0 + g_m[1], ln_mlp_g[1], ln_mlp_b[1])

    dx3, dy3, s3 = _ln_bwd(x3, y3, pv3, "ln_mlp1_bwd", target=target)
    loss = lax.psum(s3[5, 0], ("x", "y", "c"))
    du3, dw1_1, dw2_1 = _mlp_bwd(dy3, u3, h1, a1, w1[1].T, w2[1].T, "1")
    dx2, dy2, s2 = _ln_bwd(x2, y2, pv2, "ln_mix1_bwd", dxo=dx3, du=du3)
    du2, d_ssm_in, d_ssm_out, ssm_small = _ssd_bwd(dy2, u2, sw, cw8, cb8, par8, nw8, sres)
    dx1, dy1, s1 = _ln_bwd(x1, y1, pv1, "ln_mlp0_bwd", dxo=dx2, du=du2)
    du1, dw1_0, dw2_0 = _mlp_bwd(dy1, u1, h0, a0, w1[0].T, w2[0].T, "0")
    dx0, dy0, s0 = _ln_bwd(xs, y0, pv0, "ln_mix0_bwd", dxo=dx1, du=du1)
    early = [(True, _col_slots(jnp.stack([dw1_0, dw1_1]))), (True, _row_slots(jnp.stack([dw2_0, dw2_1]))),
             (True, _col_slots(d_ssm_in[None])), (True, _row_slots(d_ssm_out[None])),
             (True, _col_slots(ssm_small["conv_w"][None])), (True, _col_slots(ssm_small["conv_b"][None])),
             (True, _col_slots(ssm_small["norm_w"][None]))]
    du0, d_fox_in, d_fox_o, fox_sums, exchanged = _fox_bwd(dy0, u0, fw, bf8, fres, early)
    grad_x, sx = _mod_bwd(dx0, du0, xs, pv_rows(1.0 + sc_a[0], sh_a[0]), "modulate0_bwd")

    dmod = jnp.stack([
        jnp.stack([sx[1], sx[0], s0[4], s0[1], s0[0], s1[4]]),
        jnp.stack([s1[1], s1[0], s2[4], s2[1], s2[0], s3[4]]),
    ]).reshape(DEPTH, 6 * d)

    def pad_rows(v):
        v = v.reshape(-1, LANES) if v.size % LANES == 0 else jnp.pad(v.reshape(-1), (0, LANES - v.size)).reshape(1, LANES)
        return jnp.pad(v, ((0, (-v.shape[0]) % 8), (0, 0)))

    small_parts = [dmod, jnp.stack([s0[2], s2[2]]), jnp.stack([s0[3], s2[3]]), jnp.stack([s1[2], s3[2]]),
                   jnp.stack([s1[3], s3[3]]), fox_sums[0, :FOX_HEADS], ssm_small["dt_bias"], ssm_small["a_log"],
                   ssm_small["d"]]
    packed = [pad_rows(p) for p in small_parts]
    offsets = np.cumsum([0] + [p.shape[0] for p in packed])
    small_all_g = _all_gather(jnp.concatenate(packed, axis=0), "gather_small_grads")

    def unpack(idx, shape):
        n = int(np.prod(shape))
        blk = small_all_g[:, offsets[idx]:offsets[idx + 1]].reshape(N_DEV, -1)[:, :n]
        return blk.reshape((N_DEV,) + tuple(shape))

    dmod_all = unpack(0, (DEPTH, 6 * d))
    dmod_cols = lax.dynamic_slice_in_dim(dmod_all, me * ncol, ncol, axis=2)
    dmod_cols = jnp.pad(jnp.moveaxis(dmod_cols, 0, 1), ((0, 0), (0, N_DEV), (0, 0)))
    g_ada_w = _ada_bwd(c_all, dmod_cols)

    shards = dict(
        mlp_w1=exchanged[0], mlp_w2=exchanged[1], ssm_w_in=exchanged[2], ssm_w_out=exchanged[3],
        ssm_conv_w=exchanged[4], ssm_conv_b=exchanged[5], ssm_norm_w=exchanged[6],
        fox_w_in=_scatter_cols(d_fox_in[None], "scatter_fox_in"), fox_w_o=_scatter_rows(d_fox_o[None], "scatter_fox_o"),
        ada_w=g_ada_w[None], ada_b=dmod_all,
        ln_mix_g=unpack(1, (DEPTH, d)), ln_mix_b=unpack(2, (DEPTH, d)),
        ln_mlp_g=unpack(3, (DEPTH, d)), ln_mlp_b=unpack(4, (DEPTH, d)),
        fox_b_f=unpack(5, (1, FOX_HEADS)), ssm_dt_bias=unpack(6, (1, SSM_HEADS)),
        ssm_a_log=unpack(7, (1, SSM_HEADS)), ssm_d=unpack(8, (1, SSM_HEADS)),
    )
    weights = dict(ada_w=ada_w, ada_b=ada_b, ln_mix_g=ln_mix_g, ln_mix_b=ln_mix_b, ln_mlp_g=ln_mlp_g, ln_mlp_b=ln_mlp_b,
                   mlp_w1=mlp_w1, mlp_w2=mlp_w2, fox_w_in=fox_w_in, fox_b_f=fox_b_f, fox_w_o=fox_w_o, ssm_w_in=ssm_w_in,
                   ssm_conv_w=ssm_conv_w, ssm_conv_b=ssm_conv_b, ssm_dt_bias=ssm_dt_bias, ssm_a_log=ssm_a_log,
                   ssm_d=ssm_d, ssm_norm_w=ssm_norm_w, ssm_w_out=ssm_w_out)
    mom1 = dict(ada_w=m_ada_w, ada_b=m_ada_b, ln_mix_g=m_ln_mix_g, ln_mix_b=m_ln_mix_b, ln_mlp_g=m_ln_mlp_g,
                ln_mlp_b=m_ln_mlp_b, mlp_w1=m_mlp_w1, mlp_w2=m_mlp_w2, fox_w_in=m_fox_w_in, fox_b_f=m_fox_b_f,
                fox_w_o=m_fox_w_o, ssm_w_in=m_ssm_w_in, ssm_conv_w=m_ssm_conv_w, ssm_conv_b=m_ssm_conv_b,
                ssm_dt_bias=m_ssm_dt_bias, ssm_a_log=m_ssm_a_log, ssm_d=m_ssm_d, ssm_norm_w=m_ssm_norm_w,
                ssm_w_out=m_ssm_w_out)
    mom2 = dict(ada_w=v_ada_w, ada_b=v_ada_b, ln_mix_g=v_ln_mix_g, ln_mix_b=v_ln_mix_b, ln_mlp_g=v_ln_mlp_g,
                ln_mlp_b=v_ln_mlp_b, mlp_w1=v_mlp_w1, mlp_w2=v_mlp_w2, fox_w_in=v_fox_w_in, fox_b_f=v_fox_b_f,
                fox_w_o=v_fox_w_o, ssm_w_in=v_ssm_w_in, ssm_conv_w=v_ssm_conv_w, ssm_conv_b=v_ssm_conv_b,
                ssm_dt_bias=v_ssm_dt_bias, ssm_a_log=v_ssm_a_log, ssm_d=v_ssm_d, ssm_norm_w=v_ssm_norm_w,
                ssm_w_out=v_ssm_w_out)
    names = list(weights)
    stepped = {n: _adamw_any(shards[n], weights[n], mom1[n], mom2[n], f"adamw_{n}") for n in names}
    return (loss, grad_x[None], *[stepped[n][0] for n in names], *[stepped[n][1] for n in names],
            *[stepped[n][2] for n in names], *[stepped[n][3] for n in names])
```

```python
import numpy as np
import jax
import jax.numpy as jnp
from jax import lax
from jax.experimental import pallas as pl
from jax.experimental.pallas import tpu as pltpu

F32 = jnp.float32
MXU_DTYPE = jnp.bfloat16
HIGHEST = lax.Precision.HIGHEST

N_DEV = 8
D_MODEL = 1024
DEPTH = 2
FOX_HEADS = 16
FOX_HEAD_DIM = 64
D_FF = 4096
SSM_D_INNER = 2048
SSM_HEADS = 32
SSM_GROUPS = 8
SSM_STATE = 128
SSM_CHUNK = 128
SSM_CONV = 4
SSM_CONV_DIM = 4096
GROUP_W = SSM_D_INNER // SSM_GROUPS
LN_EPS = 1e-5
RMS_EPS = 1e-5
ALPHA = (2.0 * DEPTH) ** 0.25
LANES = 128
SUBLANES = 8

ADAM_LR = 0.001
ADAM_B1 = 0.9
ADAM_B2 = 0.999
ADAM_EPS = 1e-08
ADAM_WD = 0.01
ADAM_STEP = 10

NN = (((1,), (0,)), ((), ()))
NT = (((1,), (1,)), ((), ()))
TN = (((0,), (0,)), ((), ()))

VMEM_BIG = 56 * 1024 * 1024


def _dot(a, b, dims=NN, precision=None):
    return lax.dot_general(a, b, dims, precision=precision, preferred_element_type=F32)


def _mx(v):
    return v.astype(MXU_DTYPE)


def _pieces3(v):
    hi = _mx(v)
    r1 = v - hi.astype(F32)
    mid = _mx(r1)
    return hi, mid, _mx(r1 - mid.astype(F32))


def _dot_onehot(a, b, dims=NN, onehot="b"):
    if onehot == "b":
        return sum(_dot(p, _mx(b), dims) for p in _pieces3(a))
    return sum(_dot(_mx(a), p, dims) for p in _pieces3(b))


def _params(vmem=None):
    return pltpu.CompilerParams(vmem_limit_bytes=vmem) if vmem else None


def _all_gather(x, name):
    def body(x_ref, out_ref, send_sems, recv_sems, local_sem):
        xi, yi, ci = lax.axis_index("x"), lax.axis_index("y"), lax.axis_index("c")
        me, sibling = (xi, yi, ci), (xi, yi, 1 - ci)
        chips = [(1 - xi, yi), (xi, 1 - yi), (1 - xi, 1 - yi)]

        def slot(px, py, pc):
            return out_ref.at[4 * px + 2 * py + pc]

        def copy(k, block, to, src=None):
            return pltpu.make_async_remote_copy(
                src_ref=slot(*block) if src is None else src, dst_ref=slot(*block),
                send_sem=send_sems.at[k], recv_sem=recv_sems.at[k],
                device_id=to, device_id_type=pl.DeviceIdType.MESH)

        mine = pltpu.make_async_copy(x_ref, slot(*me), local_sem)
        mine.start()
        first = [copy(0, me, sibling, src=x_ref)]
        first += [copy(1 + j, me, (*chip, ci), src=x_ref) for j, chip in enumerate(chips)]
        for cp in first:
            cp.start()
        passed = [copy(4 + j, (*chip, ci), sibling) for j, chip in enumerate(chips)]
        for j, chip in enumerate(chips):
            copy(1 + j, (*chip, ci), me).wait_recv()
            passed[j].start()
        copy(0, sibling, me).wait_recv()
        for j, chip in enumerate(chips):
            copy(4 + j, (*chip, 1 - ci), me).wait_recv()
        for cp in first + passed:
            cp.wait_send()
        mine.wait()

    return pl.pallas_call(
        body, name=name,
        out_shape=jax.ShapeDtypeStruct((N_DEV,) + x.shape, x.dtype),
        in_specs=[pl.BlockSpec(memory_space=pl.ANY)],
        out_specs=pl.BlockSpec(memory_space=pl.ANY),
        scratch_shapes=[pltpu.SemaphoreType.DMA((7,)), pltpu.SemaphoreType.DMA((7,)),
                        pltpu.SemaphoreType.DMA],
    )(x)


EXCHANGE_PIECES = 1


def _direct_copies(scatter, x_ref, out_ref, send_sems, recv_sems, local_sems, n, piece=None):
    xi, yi, ci = lax.axis_index("x"), lax.axis_index("y"), lax.axis_index("c")
    me = 4 * xi + 2 * yi + ci

    def part(ref):
        return ref if piece is None else ref.at[piece]

    local = pltpu.make_async_copy(part(x_ref.at[me] if scatter else x_ref), part(out_ref.at[me]), local_sems.at[n])
    remote = []
    for k in range(1, N_DEV):
        px = 1 - xi if k & 4 else xi
        py = 1 - yi if k & 2 else yi
        pc = 1 - ci if k & 1 else ci
        remote.append(pltpu.make_async_remote_copy(
            src_ref=part(x_ref.at[4 * px + 2 * py + pc] if scatter else x_ref), dst_ref=part(out_ref.at[me]),
            send_sem=send_sems.at[7 * n + k - 1], recv_sem=recv_sems.at[7 * n + k - 1],
            device_id=(px, py, pc), device_id_type=pl.DeviceIdType.MESH))
    return local, remote


def _in_pieces(carry, steps):
    out = []
    for scatter, a in carry:
        body = a.shape[1:] if scatter else a.shape
        rows = int(np.prod(body[:-1])) if len(body) > 1 else 1
        align = SUBLANES * (4 // a.dtype.itemsize)
        pieces = 1
        while (pieces * 2 <= min(EXCHANGE_PIECES, steps // 2) and rows % (pieces * 2 * align) == 0):
            pieces *= 2
        shape = (pieces, rows // pieces, body[-1])
        out.append((scatter, a.reshape(((N_DEV,) if scatter else ()) + shape), pieces, max(1, (steps // 2) // pieces)))
    return out


def _exchange_shapes(carry):
    return [jax.ShapeDtypeStruct(c[1].shape if c[0] else (N_DEV,) + c[1].shape, c[1].dtype) for c in carry]


def _exchange_sems(carry):
    n = max(len(carry), 1)
    return [pltpu.SemaphoreType.DMA((7 * n,)), pltpu.SemaphoreType.DMA((7 * n,)), pltpu.SemaphoreType.DMA((n,))]


def _exchange_start(carry, srcs, dsts, sems, step=None):
    for n, entry in enumerate(carry):
        scatter = entry[0]
        pieces, stride = (entry[2], entry[3]) if len(entry) > 2 else (1, 1)
        local, remote = _direct_copies(scatter, srcs[n], dsts[n], *sems, n)
        if step is None:
            local.start()
            for cp in remote:
                cp.start()
            continue

        @pl.when(step == 0)
        def _():
            local.start()
            if pieces == 1:
                for cp in remote:
                    cp.start()

        if pieces > 1:
            @pl.when((step % stride == 0) & (step // stride < pieces))
            def _():
                for cp in _direct_copies(scatter, srcs[n], dsts[n], *sems, n, piece=step // stride)[1]:
                    cp.start()


def _exchange_wait(carry, srcs, dsts, sems):
    for n, entry in enumerate(carry):
        local, remote = _direct_copies(entry[0], srcs[n], dsts[n], *sems, n)
        for cp in remote:
            cp.wait()
        local.wait()


def _all_to_all(x, name):
    carry = [(True, x)]

    def body(x_ref, out_ref, *sems):
        _exchange_start(carry, [x_ref], [out_ref], sems)
        _exchange_wait(carry, [x_ref], [out_ref], sems)

    return pl.pallas_call(
        body, name=name,
        out_shape=jax.ShapeDtypeStruct(x.shape, x.dtype),
        in_specs=[pl.BlockSpec(memory_space=pl.ANY)],
        out_specs=pl.BlockSpec(memory_space=pl.ANY),
        scratch_shapes=_exchange_sems(carry),
    )(x)


def _mm(name, a, b, *, grid, a_spec, b_spec, dims, k_axis, outs, acc=None, extras=(), epi=None, vmem=None):
    nk = grid[k_axis]
    n_ex, n_out = len(extras), len(outs)

    def body(*refs):
        a_ref, b_ref = refs[0], refs[1]
        ex = refs[2:2 + n_ex]
        out = refs[2 + n_ex:2 + n_ex + n_out]

        def finish(val):
            if epi is None:
                out[0][...] = val.astype(out[0].dtype)
            else:
                epi(val, ex, out)

        part = _dot(a_ref[...], b_ref[...], dims)
        if nk == 1:
            finish(part)
        else:
            acc_ref = refs[2 + n_ex + n_out]
            k = pl.program_id(k_axis)

            @pl.when(k == 0)
            def _():
                acc_ref[...] = part

            @pl.when(k > 0)
            def _():
                acc_ref[...] += part

            @pl.when(k == nk - 1)
            def _():
                finish(acc_ref[...])

    return pl.pallas_call(
        body, name=name, grid=grid,
        in_specs=[a_spec, b_spec] + [s for _, s in extras],
        out_specs=[s for _, s in outs],
        out_shape=[o for o, _ in outs],
        scratch_shapes=[pltpu.VMEM(acc, F32)] if nk > 1 else [],
        compiler_params=_params(vmem),
    )(a, b, *[e for e, _ in extras])


def _tile(n, t):
    t = min(n, t)
    assert n % t == 0, (n, t)
    return t


def _mm_nn(name, a, b, out_dtype, *, addend=None, tm=1024, tn=1024, tk=1024, epi=None, extras=(), outs=None):
    m, kk = a.shape
    n = b.shape[1]
    tm, tn, tk = _tile(m, tm), _tile(n, tn), _tile(kk, tk)
    o_spec = pl.BlockSpec((tm, tn), lambda i, j, k: (i, j))
    if outs is None:
        outs = [(jax.ShapeDtypeStruct((m, n), out_dtype), o_spec)]
    extras = list(extras)
    if addend is not None:
        extras = [(addend, o_spec)] + extras

        def epi(val, ex, out):
            out[0][...] = (val + ex[0][...].astype(F32)).astype(out[0].dtype)

    res = _mm(name, a, b, grid=(m // tm, n // tn, kk // tk),
              a_spec=pl.BlockSpec((tm, tk), lambda i, j, k: (i, k)),
              b_spec=pl.BlockSpec((tk, tn), lambda i, j, k: (k, j)),
              dims=NN, k_axis=2, acc=(tm, tn), outs=outs, extras=extras, epi=epi, vmem=VMEM_BIG)
    return res[0] if len(res) == 1 else res


def _mm_tn(name, a, b, out_dtype=F32, *, tm=1024, tn=1024, tk=1024):
    kk, m = a.shape
    n = b.shape[1]
    tm, tn, tk = _tile(m, tm), _tile(n, tn), _tile(kk, tk)
    res = _mm(name, a, b, grid=(m // tm, n // tn, kk // tk),
              a_spec=pl.BlockSpec((tk, tm), lambda i, j, k: (k, i)),
              b_spec=pl.BlockSpec((tk, tn), lambda i, j, k: (k, j)),
              dims=TN, k_axis=2, acc=(tm, tn),
              outs=[(jax.ShapeDtypeStruct((m, n), out_dtype), pl.BlockSpec((tm, tn), lambda i, j, k: (i, j)))],
              vmem=VMEM_BIG)
    return res[0]


def _row_block(s):
    return _tile(s, 512)


def _modulate(x, pv, name):
    s, d = x.shape
    tb = _row_block(s)

    def body(x_ref, pv_ref, u_ref):
        u_ref[...] = _mx(x_ref[...] * pv_ref[0:1, :] + pv_ref[1:2, :])

    return pl.pallas_call(
        body, name=name, grid=(s // tb,),
        in_specs=[pl.BlockSpec((tb, d), lambda i: (i, 0)), pl.BlockSpec((8, d), lambda i: (0, 0))],
        out_specs=pl.BlockSpec((tb, d), lambda i: (i, 0)),
        out_shape=jax.ShapeDtypeStruct((s, d), MXU_DTYPE),
    )(x, pv)


def _ln_stats(r):
    mu = jnp.mean(r, axis=-1, keepdims=True)
    xc = r - mu
    var = jnp.mean(xc * xc, axis=-1, keepdims=True)
    rstd = lax.rsqrt(var + LN_EPS)
    return xc * rstd, rstd


def _ln_fwd(xin, y, pv, name):
    s, d = xin.shape
    tb = _row_block(s)

    def body(x_ref, y_ref, pv_ref, xo_ref, u_ref):
        r = ALPHA * x_ref[...] + pv_ref[0:1, :] * y_ref[...]
        xhat, _ = _ln_stats(r)
        xo = xhat * pv_ref[1:2, :] + pv_ref[2:3, :]
        xo_ref[...] = xo
        u_ref[...] = _mx(xo * pv_ref[3:4, :] + pv_ref[4:5, :])

    row = pl.BlockSpec((tb, d), lambda i: (i, 0))
    return pl.pallas_call(
        body, name=name, grid=(s // tb,),
        in_specs=[row, row, pl.BlockSpec((8, d), lambda i: (0, 0))],
        out_specs=[row, row],
        out_shape=[jax.ShapeDtypeStruct((s, d), F32), jax.ShapeDtypeStruct((s, d), MXU_DTYPE)],
    )(xin, y, pv)


def _ln_bwd(xin, y, pv, name, *, dxo=None, du=None, target=None):
    s, d = xin.shape
    tb = _row_block(s)
    nb = s // tb
    loss_mode = target is not None

    def body(*refs):
        if loss_mode:
            x_ref, y_ref, pv_ref, t_ref, dxin_ref, dy_ref, sums_ref = refs
        else:
            x_ref, y_ref, pv_ref, dxo_ref, du_ref, dxin_ref, dy_ref, sums_ref = refs
        i = pl.program_id(0)

        @pl.when(i == 0)
        def _():
            sums_ref[...] = jnp.zeros_like(sums_ref)

        yv = y_ref[...]
        r = ALPHA * x_ref[...] + pv_ref[0:1, :] * yv
        xhat, rstd = _ln_stats(r)
        xo = xhat * pv_ref[1:2, :] + pv_ref[2:3, :]
        if loss_mode:
            diff = xo - t_ref[...]
            dxo_v = diff * (1.0 / d)
            sums_ref[5:6, :] += jnp.sum(diff * diff, axis=0, keepdims=True) * (0.5 / d)
        else:
            duv = du_ref[...]
            dxo_v = dxo_ref[...] + duv * pv_ref[3:4, :]
            sums_ref[0:1, :] += jnp.sum(duv * xo, axis=0, keepdims=True)
            sums_ref[1:2, :] += jnp.sum(duv, axis=0, keepdims=True)
        sums_ref[2:3, :] += jnp.sum(dxo_v * xhat, axis=0, keepdims=True)
        sums_ref[3:4, :] += jnp.sum(dxo_v, axis=0, keepdims=True)
        dxh = dxo_v * pv_ref[1:2, :]
        dr = rstd * (dxh - jnp.mean(dxh, axis=-1, keepdims=True)
                     - xhat * jnp.mean(dxh * xhat, axis=-1, keepdims=True))
        sums_ref[4:5, :] += jnp.sum(dr * yv, axis=0, keepdims=True)
        dxin_ref[...] = ALPHA * dr
        dy_ref[...] = _mx(pv_ref[0:1, :] * dr)
        if loss_mode:
            @pl.when(i == nb - 1)
            def _():
                sums_ref[5:6, :] = jnp.broadcast_to(jnp.sum(sums_ref[5:6, :], axis=-1, keepdims=True), (1, d))

    row = pl.BlockSpec((tb, d), lambda i: (i, 0))
    par = pl.BlockSpec((8, d), lambda i: (0, 0))
    ins = [xin, y, pv] + ([target] if loss_mode else [dxo, du])
    return pl.pallas_call(
        body, name=name, grid=(nb,),
        in_specs=[row, row, par] + [row] * (len(ins) - 3),
        out_specs=[row, row, par],
        out_shape=[jax.ShapeDtypeStruct((s, d), F32), jax.ShapeDtypeStruct((s, d), MXU_DTYPE),
                   jax.ShapeDtypeStruct((8, d), F32)],
    )(*ins)


def _mod_bwd(dx_direct, du, x, pv, name):
    s, d = x.shape
    tb = _row_block(s)

    def body(dxd_ref, du_ref, x_ref, pv_ref, dx_ref, sums_ref):
        @pl.when(pl.program_id(0) == 0)
        def _():
            sums_ref[...] = jnp.zeros_like(sums_ref)

        duv = du_ref[...]
        dx_ref[...] = dxd_ref[...] + duv * pv_ref[0:1, :]
        sums_ref[0:1, :] += jnp.sum(duv * x_ref[...], axis=0, keepdims=True)
        sums_ref[1:2, :] += jnp.sum(duv, axis=0, keepdims=True)

    row = pl.BlockSpec((tb, d), lambda i: (i, 0))
    par = pl.BlockSpec((8, d), lambda i: (0, 0))
    return pl.pallas_call(
        body, name=name, grid=(s // tb,),
        in_specs=[row, row, row, par], out_specs=[row, par],
        out_shape=[jax.ShapeDtypeStruct((s, d), F32), jax.ShapeDtypeStruct((8, d), F32)],
    )(dx_direct, du, x, pv)


def _mlp_fwd(u, w1, w2, tag):
    s = u.shape[0]

    def epi(val, ex, out):
        out[0][...] = _mx(val)
        out[1][...] = _mx(jnp.square(jnp.maximum(val, 0.0)))

    tm, tn = _tile(s, 1024), 1024
    spec = pl.BlockSpec((tm, tn), lambda i, j, k: (i, j))
    shp = jax.ShapeDtypeStruct((s, D_FF), MXU_DTYPE)
    h, a = _mm_nn(f"mlp_up{tag}", u, w1, None, epi=epi, outs=[(shp, spec), (shp, spec)], tn=tn)
    y = _mm_nn(f"mlp_down{tag}", a, w2, F32)
    return y, (h, a)


def _mlp_bwd(dy, u, h, a, w1t, w2t, tag):
    s = u.shape[0]
    tm, tn = _tile(s, 1024), 1024
    spec = pl.BlockSpec((tm, tn), lambda i, j, k: (i, j))

    def epi(val, ex, out):
        out[0][...] = _mx(val * (2.0 * jnp.maximum(ex[0][...].astype(F32), 0.0)))

    dh = _mm_nn(f"mlp_dh{tag}", dy, w2t, None, epi=epi, extras=[(h, spec)],
                outs=[(jax.ShapeDtypeStruct((s, D_FF), MXU_DTYPE), spec)], tn=tn)
    du = _mm_nn(f"mlp_du{tag}", dh, w1t, F32)
    dw2 = _mm_tn(f"mlp_dw2{tag}", a, dy)
    dw1 = _mm_tn(f"mlp_dw1{tag}", u, dh)
    return du, dw1, dw2


FOX_T = 1024
BIAS_Q = (64, 65, 66)
BIAS_K = (67, 68, 69)
ONES_V = 64

def _fox_constants():
    selq = np.zeros((FOX_HEADS, 512, LANES), np.float32)
    selk = np.zeros((FOX_HEADS, 512, LANES), np.float32)
    selv = np.zeros((2, LANES, LANES), np.float32)
    put = np.zeros((2, 2, LANES, LANES), np.float32)
    for h in range(FOX_HEADS):
        off = FOX_HEAD_DIM * (h % 2)
        for dd in range(FOX_HEAD_DIM):
            selq[h, off + dd, dd] = FOX_HEAD_DIM ** -0.5
            selk[h, off + dd, dd] = 1.0
        for piece in range(3):
            selq[h, LANES * (1 + piece) + h, BIAS_Q[piece]] = 1.0
            selk[h, LANES * (1 + piece) + h, BIAS_K[piece]] = -1.0
    for par in range(2):
        for dd in range(FOX_HEAD_DIM):
            selv[par, FOX_HEAD_DIM * par + dd, dd] = 1.0
            put[par, 0, dd, FOX_HEAD_DIM * par + dd] = FOX_HEAD_DIM ** -0.5
            put[par, 1, dd, FOX_HEAD_DIM * par + dd] = 1.0
    return selq, selk, selv, put


def _pairs(nb, by_key):
    if by_key:
        pr = [(i, j) for j in range(nb) for i in range(j, nb)]
    else:
        pr = [(i, j) for i in range(nb) for j in range(i + 1)]
    return (np.array([p[0] for p in pr], np.int32), np.array([p[1] for p in pr], np.int32))


def _fox_prep(qkv, f, bf):
    s = qkv.shape[0]
    t = _tile(s, FOX_T)
    nb = s // t
    selq, selk, selv, _ = _fox_constants()

    def body(q_ref, k_ref, v_ref, f_ref, bf_ref, selq_ref, selk_ref, selv_ref,
             qa_ref, qat_ref, ka_ref, kat_ref, va_ref, vat_ref, parts_ref, carry_ref):
        i, h = pl.program_id(0), pl.program_id(1)
        lane = lax.broadcasted_iota(jnp.int32, (1, LANES), 1)

        @pl.when(h == 0)
        def _():
            @pl.when(i == 0)
            def _():
                carry_ref[...] = jnp.zeros_like(carry_ref)

            lf = jnp.where(lane < FOX_HEADS, jax.nn.log_sigmoid(f_ref[...] + bf_ref[0:1, :]), 0.0)
            tri = (lax.broadcasted_iota(jnp.int32, (t, t), 0) >= lax.broadcasted_iota(jnp.int32, (t, t), 1)).astype(F32)
            cum = _dot_onehot(tri, lf, onehot="a") + carry_ref[0:1, :]
            carry_ref[0:1, :] = cum[t - 1:t, :]
            hi = _mx(cum)
            r1 = cum - hi.astype(F32)
            mid = _mx(r1)
            parts_ref[:, 0:LANES] = hi
            parts_ref[:, LANES:2 * LANES] = mid
            parts_ref[:, 2 * LANES:3 * LANES] = _mx(r1 - mid.astype(F32))

        parts = parts_ref[...]
        qa = _dot(jnp.concatenate([q_ref[...], parts], axis=1), selq_ref[...])
        qa = qa + jnp.where((lane >= BIAS_K[0]) & (lane <= BIAS_K[2]), 1.0, 0.0)
        ka = _dot(jnp.concatenate([k_ref[...], parts], axis=1), selk_ref[...])
        ka = ka + jnp.where((lane >= BIAS_Q[0]) & (lane <= BIAS_Q[2]), 1.0, 0.0)
        va = _dot(v_ref[...], selv_ref[...]) + jnp.where(lane == ONES_V, 1.0, 0.0)
        qa_ref[...] = _mx(qa)
        qat_ref[...] = _mx(qa.T)
        ka_ref[...] = _mx(ka)
        kat_ref[...] = _mx(ka.T)
        va_ref[...] = _mx(va)
        vat_ref[...] = _mx(va.T)

    rows = jax.ShapeDtypeStruct((FOX_HEADS, nb, t, LANES), MXU_DTYPE)
    cols = jax.ShapeDtypeStruct((FOX_HEADS, nb, LANES, t), MXU_DTYPE)
    rspec = pl.BlockSpec((None, None, t, LANES), lambda i, h: (h, i, 0, 0))
    cspec = pl.BlockSpec((None, None, LANES, t), lambda i, h: (h, i, 0, 0))
    npair = FOX_HEADS // 2
    return pl.pallas_call(
        body, name="fox_prep", grid=(nb, FOX_HEADS),
        in_specs=[pl.BlockSpec((t, LANES), lambda i, h: (i, h // 2)),
                  pl.BlockSpec((t, LANES), lambda i, h: (i, npair + h // 2)),
                  pl.BlockSpec((t, LANES), lambda i, h: (i, 2 * npair + h // 2)),
                  pl.BlockSpec((t, LANES), lambda i, h: (i, 0)),
                  pl.BlockSpec((8, LANES), lambda i, h: (0, 0)),
                  pl.BlockSpec((None, 512, LANES), lambda i, h: (h, 0, 0)),
                  pl.BlockSpec((None, 512, LANES), lambda i, h: (h, 0, 0)),
                  pl.BlockSpec((None, LANES, LANES), lambda i, h: (h % 2, 0, 0))],
        out_specs=[rspec, cspec, rspec, cspec, rspec, cspec],
        out_shape=[rows, cols, rows, cols, rows, cols],
        scratch_shapes=[pltpu.VMEM((t, 3 * LANES), MXU_DTYPE), pltpu.VMEM((8, LANES), F32)],
        compiler_params=_params(VMEM_BIG),
    )(qkv, qkv, qkv, f, bf, _mx(jnp.asarray(selq)), _mx(jnp.asarray(selk)), _mx(jnp.asarray(selv)))


def _causal_allow(t):
    return lax.broadcasted_iota(jnp.int32, (t, t), 0) <= lax.broadcasted_iota(jnp.int32, (t, t), 1)


def _fox_attn_fwd(qat, ka, vat, carry=()):
    heads, nb, _, t = qat.shape
    im, jm = _pairs(nb, by_key=False)
    npairs = len(im)
    whole = _exchange_shapes(list(carry))
    carry = _in_pieces(list(carry), heads * npairs)
    nc = len(carry)

    def body(im_ref, jm_ref, qat_ref, ka_ref, vat_ref, *rest):
        srcs, (ot_ref, lse_ref), dsts = rest[:nc], rest[nc:nc + 2], rest[nc + 2:2 * nc + 2]
        acc_ref, m_ref = rest[2 * nc + 2:2 * nc + 4]
        sems = rest[2 * nc + 4:]
        h, p = pl.program_id(0), pl.program_id(1)
        i, j = im_ref[p], jm_ref[p]

        if nc:
            _exchange_start(carry, srcs, dsts, sems, step=h * npairs + p)

        @pl.when(j == 0)
        def _():
            m_ref[...] = jnp.full_like(m_ref, -jnp.inf)
            acc_ref[...] = jnp.zeros_like(acc_ref)

        def step(diagonal):
            st = _dot(ka_ref[...], qat_ref[...])
            if diagonal:
                st = jnp.where(_causal_allow(t), st, -jnp.inf)
            m_old = m_ref[...]
            m_new = jnp.maximum(m_old, jnp.max(st, axis=0, keepdims=True))
            pt = jnp.exp(st - m_new)
            acc_ref[...] = acc_ref[...] * jnp.exp(m_old - m_new) + _dot(vat_ref[...], _mx(pt))
            m_ref[...] = m_new

        @pl.when(j < i)
        def _():
            step(False)

        @pl.when(j == i)
        def _():
            step(True)
            acc = acc_ref[...]
            denom = acc[ONES_V:ONES_V + 1, :]
            ot_ref[...] = _mx(acc / denom)
            lse_ref[...] = m_ref[...] + jnp.log(denom)

        if nc:
            @pl.when((h == heads - 1) & (p == npairs - 1))
            def _():
                _exchange_wait(carry, srcs, dsts, sems)

    anywhere = pl.BlockSpec(memory_space=pl.ANY)
    qspec = pl.BlockSpec((None, None, LANES, t), lambda h, p, im, jm: (h, im[p], 0, 0))
    grid_spec = pltpu.PrefetchScalarGridSpec(
        num_scalar_prefetch=2, grid=(heads, npairs),
        in_specs=[qspec,
                  pl.BlockSpec((None, None, t, LANES), lambda h, p, im, jm: (h, jm[p], 0, 0)),
                  pl.BlockSpec((None, None, LANES, t), lambda h, p, im, jm: (h, jm[p], 0, 0))] + [anywhere] * nc,
        out_specs=[qspec, pl.BlockSpec((None, None, 1, t), lambda h, p, im, jm: (h, im[p], 0, 0))] + [anywhere] * nc,
        scratch_shapes=[pltpu.VMEM((LANES, t), F32), pltpu.VMEM((1, t), F32)] + (_exchange_sems(carry) if nc else []))
    res = pl.pallas_call(
        body, name="fox_attn_fwd", grid_spec=grid_spec,
        out_shape=[jax.ShapeDtypeStruct((heads, nb, LANES, t), MXU_DTYPE),
                   jax.ShapeDtypeStruct((heads, nb, 1, t), F32)] + _exchange_shapes(carry),
        compiler_params=_params(VMEM_BIG),
    )(jnp.asarray(im), jnp.asarray(jm), qat, ka, vat, *[c[1] for c in carry])
    return res[0], res[1], [r.reshape(w.shape) for r, w in zip(res[2:], whole)]


def _fox_attn_bwd(qa, qat, ka, kat, va, ot, lse, do, dot_, carry=()):
    heads, nb, t, _ = qa.shape
    im, jm = _pairs(nb, by_key=True)
    npairs = len(im)
    whole = _exchange_shapes(list(carry))
    carry = _in_pieces(list(carry), heads * npairs)
    nc = len(carry)

    def body(im_ref, jm_ref, qa_ref, qat_ref, ka_ref, kat_ref, va_ref, ot_ref, lse_ref, do_ref, dot_ref, *rest):
        srcs, (dqt_ref, dka_ref, dva_ref), dsts = rest[:nc], rest[nc:nc + 3], rest[nc + 3:2 * nc + 3]
        sems = rest[2 * nc + 3:]
        h, p = pl.program_id(0), pl.program_id(1)
        i, j = im_ref[p], jm_ref[p]

        if nc:
            _exchange_start(carry, srcs, dsts, sems, step=h * npairs + p)

        @pl.when(p == 0)
        def _():
            dqt_ref[...] = jnp.zeros_like(dqt_ref)

        def step(diagonal):
            st = _dot(ka_ref[...], qat_ref[...])
            dot_v = dot_ref[...]
            delta = jnp.sum(ot_ref[...].astype(F32) * dot_v.astype(F32), axis=0, keepdims=True)
            pt = jnp.exp(st - lse_ref[...])
            if diagonal:
                pt = jnp.where(_causal_allow(t), pt, 0.0)
            dsm = _mx(pt * (_dot(va_ref[...], dot_v) - delta))
            upd_v = _dot(_mx(pt), do_ref[...])
            upd_k = _dot(dsm, qa_ref[...])
            if diagonal:
                dva_ref[...] = upd_v
                dka_ref[...] = upd_k
            else:
                dva_ref[...] += upd_v
                dka_ref[...] += upd_k
            dqt_ref[i] += _dot(kat_ref[...], dsm)

        @pl.when(i == j)
        def _():
            step(True)

        @pl.when(i > j)
        def _():
            step(False)

        if nc:
            @pl.when((h == heads - 1) & (p == npairs - 1))
            def _():
                _exchange_wait(carry, srcs, dsts, sems)

    def at_q(shape):
        return pl.BlockSpec((None, None) + shape, lambda h, p, im, jm: (h, im[p], 0, 0))

    def at_k(shape):
        return pl.BlockSpec((None, None) + shape, lambda h, p, im, jm: (h, jm[p], 0, 0))

    anywhere = pl.BlockSpec(memory_space=pl.ANY)
    grid_spec = pltpu.PrefetchScalarGridSpec(
        num_scalar_prefetch=2, grid=(heads, npairs),
        in_specs=[at_q((t, LANES)), at_q((LANES, t)), at_k((t, LANES)), at_k((LANES, t)), at_k((t, LANES)),
                  at_q((LANES, t)), at_q((1, t)), at_q((t, LANES)), at_q((LANES, t))] + [anywhere] * nc,
        out_specs=[pl.BlockSpec((None, nb, LANES, t), lambda h, p, im, jm: (h, 0, 0, 0)),
                   at_k((t, LANES)), at_k((t, LANES))] + [anywhere] * nc,
        scratch_shapes=_exchange_sems(carry) if nc else [])
    res = pl.pallas_call(
        body, name="fox_attn_bwd", grid_spec=grid_spec,
        out_shape=[jax.ShapeDtypeStruct((heads, nb, LANES, t), F32),
                   jax.ShapeDtypeStruct((heads, nb, t, LANES), F32),
                   jax.ShapeDtypeStruct((heads, nb, t, LANES), F32)] + _exchange_shapes(carry),
        compiler_params=_params(VMEM_BIG),
    )(jnp.asarray(im), jnp.asarray(jm), qa, qat, ka, kat, va, ot, lse, do, dot_, *[c[1] for c in carry])
    return res[0], res[1], res[2], [r.reshape(w.shape) for r, w in zip(res[3:], whole)]


def _fox_post(dqt, dka, dva, f, bf):
    heads, nb, t, _ = dka.shape
    s = nb * t
    _, _, _, put = _fox_constants()
    npair = heads // 2

    def body(dqt_ref, dka_ref, dva_ref, f_ref, bf_ref, put_ref, dq_ref, dk_ref, dv_ref, df_ref, sums_ref,
             dc_ref, carry_ref):
        i, h = pl.program_id(0), pl.program_id(1)

        @pl.when((i == 0) & (h == 0))
        def _():
            carry_ref[...] = jnp.zeros_like(carry_ref)
            sums_ref[...] = jnp.zeros_like(sums_ref)

        @pl.when(h == 0)
        def _():
            dc_ref[...] = jnp.zeros_like(dc_ref)

        dqt_v = dqt_ref[...]
        dka_v = dka_ref[...]
        term_q = _dot(_mx(dqt_v), put_ref[0], TN)
        term_k = _dot(_mx(dka_v), put_ref[1])
        term_v = _dot(_mx(dva_ref[...]), put_ref[1])

        @pl.when(h % 2 == 0)
        def _():
            dq_ref[...] = _mx(term_q)
            dk_ref[...] = _mx(term_k)
            dv_ref[...] = _mx(term_v)

        @pl.when(h % 2 == 1)
        def _():
            dq_ref[...] += _mx(term_q)
            dk_ref[...] += _mx(term_k)
            dv_ref[...] += _mx(term_v)

        dcum = dqt_v[BIAS_Q[0]:BIAS_Q[0] + 1, :] - dka_v.T[BIAS_K[0]:BIAS_K[0] + 1, :]
        head_row = lax.broadcasted_iota(jnp.int32, (heads, 1), 0) == h
        dc_ref[...] += jnp.where(head_row, dcum, 0.0)

        @pl.when(h == heads - 1)
        def _():
            later = (lax.broadcasted_iota(jnp.int32, (t, t), 0) >= lax.broadcasted_iota(jnp.int32, (t, t), 1)).astype(F32)
            dlf_t = _dot_onehot(dc_ref[...], later) + carry_ref[:, 0:1]
            carry_ref[...] = jnp.broadcast_to(dlf_t[:, 0:1], carry_ref.shape)
            dlf = jnp.concatenate([dlf_t, jnp.zeros((LANES - heads, t), F32)], axis=0).T
            lane = lax.broadcasted_iota(jnp.int32, (1, LANES), 1)
            df = jnp.where(lane < heads, dlf * jax.nn.sigmoid(-(f_ref[...] + bf_ref[0:1, :])), 0.0)
            df_ref[...] = _mx(df)
            sums_ref[0:1, :] += jnp.sum(df, axis=0, keepdims=True)

    rev = lambda i: nb - 1 - i
    pair_spec = pl.BlockSpec((t, LANES), lambda i, h: (rev(i), h // 2))
    blk = pl.BlockSpec((t, LANES), lambda i, h: (rev(i), 0))
    hd = jax.ShapeDtypeStruct((s, D_MODEL), MXU_DTYPE)
    return pl.pallas_call(
        body, name="fox_post", grid=(nb, heads),
        in_specs=[pl.BlockSpec((None, None, LANES, t), lambda i, h: (h, rev(i), 0, 0)),
                  pl.BlockSpec((None, None, t, LANES), lambda i, h: (h, rev(i), 0, 0)),
                  pl.BlockSpec((None, None, t, LANES), lambda i, h: (h, rev(i), 0, 0)),
                  blk, pl.BlockSpec((8, LANES), lambda i, h: (0, 0)),
                  pl.BlockSpec((None, 2, LANES, LANES), lambda i, h: (h % 2, 0, 0, 0))],
        out_specs=[pair_spec, pair_spec, pair_spec, blk, pl.BlockSpec((8, LANES), lambda i, h: (0, 0))],
        out_shape=[hd, hd, hd, jax.ShapeDtypeStruct((s, LANES), MXU_DTYPE), jax.ShapeDtypeStruct((8, LANES), F32)],
        scratch_shapes=[pltpu.VMEM((heads, t), F32), pltpu.VMEM((heads, LANES), F32)],
        compiler_params=_params(VMEM_BIG),
    )(dqt, dka, dva, f, bf, _mx(jnp.asarray(put)))


def _fox_weights(w_in, w_o):
    wqkv = w_in[:, :3 * D_MODEL]
    wf = jnp.pad(w_in[:, 3 * D_MODEL:], ((0, 0), (0, LANES - FOX_HEADS)))
    wo_heads = w_o.reshape(FOX_HEADS, FOX_HEAD_DIM, D_MODEL)
    wo_a = jnp.pad(wo_heads, ((0, 0), (0, LANES - FOX_HEAD_DIM), (0, 0)))
    wo_rows = wo_a.reshape(FOX_HEADS * LANES, D_MODEL)
    return dict(wqkv=wqkv, wf=wf, wqkv_t=wqkv.T, wf_t=wf.T, wo_rows=wo_rows, wo_rows_t=wo_rows.T)


def _fox_out(ot, wo_rows):
    heads, nb, _, t = ot.shape

    def body(ot_ref, w_ref, y_ref):
        y_ref[...] = _dot(ot_ref[...].reshape(heads * LANES, t), w_ref[...], TN)

    return pl.pallas_call(
        body, name="fox_out", grid=(nb,),
        in_specs=[pl.BlockSpec((heads, None, LANES, t), lambda i: (0, i, 0, 0)),
                  pl.BlockSpec((heads * LANES, D_MODEL), lambda i: (0, 0))],
        out_specs=pl.BlockSpec((t, D_MODEL), lambda i: (i, 0)),
        out_shape=jax.ShapeDtypeStruct((nb * t, D_MODEL), F32),
        compiler_params=_params(VMEM_BIG),
    )(ot, wo_rows)


def _fox_do(dy, wo_rows_t, nb, t):
    heads = FOX_HEADS

    def body(dy_ref, w_ref, do_ref, dot_ref):
        val = _dot(dy_ref[...], w_ref[...])
        for h in range(heads):
            blk = val[:, h * LANES:(h + 1) * LANES]
            do_ref[h] = _mx(blk)
            dot_ref[h] = _mx(blk.T)

    return pl.pallas_call(
        body, name="fox_do", grid=(nb,),
        in_specs=[pl.BlockSpec((t, D_MODEL), lambda i: (i, 0)),
                  pl.BlockSpec((D_MODEL, heads * LANES), lambda i: (0, 0))],
        out_specs=[pl.BlockSpec((heads, None, t, LANES), lambda i: (0, i, 0, 0)),
                   pl.BlockSpec((heads, None, LANES, t), lambda i: (0, i, 0, 0))],
        out_shape=[jax.ShapeDtypeStruct((heads, nb, t, LANES), MXU_DTYPE),
                   jax.ShapeDtypeStruct((heads, nb, LANES, t), MXU_DTYPE)],
        compiler_params=_params(VMEM_BIG),
    )(dy, wo_rows_t)


def _fox_dwo(ot, dy):
    heads, nb, _, t = ot.shape

    def body(ot_ref, dy_ref, o_ref):
        part = _dot(ot_ref[...].reshape(heads * LANES, t), dy_ref[...])

        @pl.when(pl.program_id(0) == 0)
        def _():
            o_ref[...] = part

        @pl.when(pl.program_id(0) > 0)
        def _():
            o_ref[...] += part

    return pl.pallas_call(
        body, name="fox_dwo", grid=(nb,),
        in_specs=[pl.BlockSpec((heads, None, LANES, t), lambda i: (0, i, 0, 0)),
                  pl.BlockSpec((t, D_MODEL), lambda i: (i, 0))],
        out_specs=pl.BlockSpec((heads * LANES, D_MODEL), lambda i: (0, 0)),
        out_shape=jax.ShapeDtypeStruct((heads * LANES, D_MODEL), F32),
        compiler_params=_params(VMEM_BIG),
    )(ot, dy)


def _fox_fwd(u, w, bf, carry=()):
    qkv = _mm_nn("fox_qkv", u, w["wqkv"], MXU_DTYPE)
    f = _mm_nn("fox_f", u, w["wf"], F32)
    qa, qat, ka, kat, va, vat = _fox_prep(qkv, f, bf)
    ot, lse, carried = _fox_attn_fwd(qat, ka, vat, carry)
    y = _fox_out(ot, w["wo_rows"])
    return y, dict(f=f, qa=qa, qat=qat, ka=ka, kat=kat, va=va, ot=ot, lse=lse), carried


def _fox_bwd(dy, u, w, bf, res, carry=()):
    heads, nb, t, _ = res["qa"].shape
    do, dot_ = _fox_do(dy, w["wo_rows_t"], nb, t)
    dwo_a = _fox_dwo(res["ot"], dy).reshape(heads, LANES, D_MODEL)
    dqt, dka, dva, carried = _fox_attn_bwd(res["qa"], res["qat"], res["ka"], res["kat"], res["va"], res["ot"],
                                           res["lse"], do, dot_, carry)
    dq, dk, dv, df, sums = _fox_post(dqt, dka, dva, res["f"], bf)
    wt = w["wqkv_t"]
    du = _mm_nn("fox_du_q", dq, wt[:D_MODEL], F32)
    du = _mm_nn("fox_du_k", dk, wt[D_MODEL:2 * D_MODEL], F32, addend=du)
    du = _mm_nn("fox_du_v", dv, wt[2 * D_MODEL:], F32, addend=du)
    du = _mm_nn("fox_du_f", df, w["wf_t"], F32, addend=du)
    dw_in = jnp.concatenate(
        [_mm_tn("fox_dw_q", u, dq), _mm_tn("fox_dw_k", u, dk), _mm_tn("fox_dw_v", u, dv),
         _mm_tn("fox_dw_f", u, df)[:, :FOX_HEADS]], axis=1)
    dw_o = dwo_a[:, :FOX_HEAD_DIM, :].reshape(D_MODEL, D_MODEL)
    return du, dw_in, dw_o, sums, carried


def _dsilu(v):
    sg = jax.nn.sigmoid(v)
    return sg * (1.0 + v * (1.0 - sg))


def _conv_taps(scr_ref, w_ref, rows, base):
    acc = None
    for k in range(SSM_CONV):
        term = scr_ref[pl.ds(base - (SSM_CONV - 1) + k, rows), :] * w_ref[k:k + 1, :]
        acc = term if acc is None else acc + term
    return acc


def _conv_fwd(zx, cw, cb):
    s = zx.shape[0]
    tb = _tile(s, 512)
    half = SSM_CONV_DIM // 2
    hb = tb // SUBLANES

    def body(x_ref, halo_ref, w_ref, b_ref, o_ref, scr_ref):
        i = pl.program_id(0)
        scr_ref[pl.ds(0, SUBLANES), :] = jnp.where(i > 0, halo_ref[...], 0.0)
        scr_ref[pl.ds(SUBLANES, tb), :] = x_ref[...]
        o_ref[...] = jax.nn.silu(_conv_taps(scr_ref, w_ref, tb, SUBLANES) + b_ref[0:1, :])

    return pl.pallas_call(
        body, name="ssd_conv_fwd", grid=(s // tb, 2),
        in_specs=[pl.BlockSpec((tb, half), lambda i, j: (i, 1 + j)),
                  pl.BlockSpec((SUBLANES, half), lambda i, j: (jnp.maximum(i * hb - 1, 0), 1 + j)),
                  pl.BlockSpec((8, half), lambda i, j: (0, j)),
                  pl.BlockSpec((8, half), lambda i, j: (0, j))],
        out_specs=pl.BlockSpec((tb, half), lambda i, j: (i, j)),
        out_shape=jax.ShapeDtypeStruct((s, SSM_CONV_DIM), F32),
        scratch_shapes=[pltpu.VMEM((tb + SUBLANES, half), F32)],
    )(zx, zx, cw, cb)


def _conv_bwd_pre(zx, dxc, cw, cb):
    s = zx.shape[0]
    tb = _tile(s, 512)
    half = SSM_CONV_DIM // 2
    hb = tb // SUBLANES

    def body(x_ref, halo_ref, d_ref, w_ref, b_ref, o_ref, sums_ref, scr_ref):
        i = pl.program_id(1)

        @pl.when(i == 0)
        def _():
            sums_ref[...] = jnp.zeros_like(sums_ref)

        scr_ref[pl.ds(0, SUBLANES), :] = jnp.where(i > 0, halo_ref[...], 0.0)
        scr_ref[pl.ds(SUBLANES, tb), :] = x_ref[...]
        pre = _conv_taps(scr_ref, w_ref, tb, SUBLANES) + b_ref[0:1, :]
        dpre = d_ref[...] * _dsilu(pre)
        o_ref[...] = dpre
        for k in range(SSM_CONV):
            shifted = scr_ref[pl.ds(SUBLANES - (SSM_CONV - 1) + k, tb), :]
            sums_ref[k:k + 1, :] += jnp.sum(dpre * shifted, axis=0, keepdims=True)
        sums_ref[SSM_CONV:SSM_CONV + 1, :] += jnp.sum(dpre, axis=0, keepdims=True)

    return pl.pallas_call(
        body, name="ssd_conv_bwd_pre", grid=(2, s // tb),
        in_specs=[pl.BlockSpec((tb, half), lambda j, i: (i, 1 + j)),
                  pl.BlockSpec((SUBLANES, half), lambda j, i: (jnp.maximum(i * hb - 1, 0), 1 + j)),
                  pl.BlockSpec((tb, half), lambda j, i: (i, j)),
                  pl.BlockSpec((8, half), lambda j, i: (0, j)),
                  pl.BlockSpec((8, half), lambda j, i: (0, j))],
        out_specs=[pl.BlockSpec((tb, half), lambda j, i: (i, j)),
                   pl.BlockSpec((8, half), lambda j, i: (0, j))],
        out_shape=[jax.ShapeDtypeStruct((s, SSM_CONV_DIM), F32), jax.ShapeDtypeStruct((8, SSM_CONV_DIM), F32)],
        scratch_shapes=[pltpu.VMEM((tb + SUBLANES, half), F32)],
    )(zx, zx, dxc, cw, cb)


def _conv_bwd_x(dpre, cw):
    s = dpre.shape[0]
    tb = _tile(s, 512)
    hb = tb // SUBLANES
    nb = s // tb

    def body(d_ref, halo_ref, w_ref, o_ref, scr_ref):
        i = pl.program_id(0)
        scr_ref[pl.ds(0, tb), :] = d_ref[...]
        scr_ref[pl.ds(tb, SUBLANES), :] = jnp.where(i < nb - 1, halo_ref[...], 0.0)
        acc = None
        for k in range(SSM_CONV):
            term = scr_ref[pl.ds(SSM_CONV - 1 - k, tb), :] * w_ref[k:k + 1, :]
            acc = term if acc is None else acc + term
        o_ref[...] = _mx(acc)

    return pl.pallas_call(
        body, name="ssd_conv_bwd_x", grid=(nb,),
        in_specs=[pl.BlockSpec((tb, SSM_CONV_DIM), lambda i: (i, 0)),
                  pl.BlockSpec((SUBLANES, SSM_CONV_DIM), lambda i: (jnp.minimum((i + 1) * hb, s // SUBLANES - 1), 0)),
                  pl.BlockSpec((8, SSM_CONV_DIM), lambda i: (0, 0))],
        out_specs=pl.BlockSpec((tb, SSM_CONV_DIM), lambda i: (i, 0)),
        out_shape=jax.ShapeDtypeStruct((s, SSM_CONV_DIM), MXU_DTYPE),
        scratch_shapes=[pltpu.VMEM((tb + SUBLANES, SSM_CONV_DIM), F32)],
        compiler_params=_params(VMEM_BIG),
    )(dpre, dpre, cw)


def _expand_constants():
    ex = np.zeros((LANES, SSM_D_INNER), np.float32)
    for h in range(SSM_HEADS):
        ex[h, h * 64:(h + 1) * 64] = 1.0
    return ex, np.ascontiguousarray(ex.T)


def _ssd_common(dtr_ref, par_ref, ex_ref, xc_ref):
    lc = SSM_CHUNK
    lane = lax.broadcasted_iota(jnp.int32, (1, LANES), 1)
    is_head = lane < SSM_HEADS
    par = par_ref[...]
    pre = dtr_ref[...] + par[0:1, :]
    dt = jnp.where(is_head, jax.nn.softplus(pre), 0.0)
    a = jnp.where(is_head, -jnp.exp(par[1:2, :]), 0.0)
    tri_b = lax.broadcasted_iota(jnp.int32, (lc, lc), 0) >= lax.broadcasted_iota(jnp.int32, (lc, lc), 1)
    tri = tri_b.astype(F32)
    da = dt * a
    acs = _dot_onehot(tri, da, onehot="a")
    acs_t = _dot_onehot(da, tri, (((0,), (1,)), ((), ())))
    wide = _dot_onehot(jnp.concatenate([dt, acs, par], axis=0), ex_ref[...])
    dt_x, acs_x, d_x = wide[0:lc], wide[lc:2 * lc], wide[2 * lc + 2:2 * lc + 3]
    last_x = acs_x[lc - 1:lc, :]
    xs = xc_ref[:, 0:SSM_D_INNER]
    return dict(pre=pre, dt=dt, a=a, tri_b=tri_b, tri=tri, acs=acs, acs_t=acs_t, dt_x=dt_x, d_x=d_x, xs=xs,
                xdt=xs * dt_x, e_x=jnp.exp(acs_x), dte_x=jnp.exp(last_x - acs_x), cd_x=jnp.exp(last_x),
                is_head=is_head)


def _decay_in(q, h):
    seg = q["acs"][:, h:h + 1] - q["acs_t"][h:h + 1, :]
    return jnp.exp(jnp.where(q["tri_b"], seg, -jnp.inf))


def _ssd_scan_fwd(xc, dtr, par):
    s = xc.shape[0]
    lc = SSM_CHUNK
    nc = s // lc
    ex, _ = _expand_constants()

    def body(xc_ref, dtr_ref, par_ref, ex_ref, y_ref, prev_ref, st_ref):
        @pl.when(pl.program_id(0) == 0)
        def _():
            st_ref[...] = jnp.zeros_like(st_ref)

        q = _ssd_common(dtr_ref, par_ref, ex_ref, xc_ref)
        lane = lax.broadcasted_iota(jnp.int32, (1, LANES), 1)
        for g in range(SSM_GROUPS):
            sl = slice(g * GROUP_W, (g + 1) * GROUP_W)
            bg = _mx(xc_ref[:, SSM_D_INNER + g * SSM_STATE:SSM_D_INNER + (g + 1) * SSM_STATE])
            cg = _mx(xc_ref[:, SSM_D_INNER + (SSM_GROUPS + g) * SSM_STATE:SSM_D_INNER + (SSM_GROUPS + g + 1) * SSM_STATE])
            gm = _dot(cg, bg, NT)
            prev = st_ref[g]
            prev_ref[g] = prev
            yoff = _dot(cg, _mx(prev)) * q["e_x"][:, sl]
            st_ref[g] = prev * q["cd_x"][:, sl] + _dot(bg, _mx(q["xdt"][:, sl] * q["dte_x"][:, sl]), TN)
            pairs = []
            for pr in range(2):
                xp = _mx(q["xdt"][:, g * GROUP_W + pr * LANES:g * GROUP_W + (pr + 1) * LANES])
                both = [_dot(_mx(gm * _decay_in(q, 4 * g + 2 * pr + r2)), xp) for r2 in range(2)]
                pairs.append(jnp.where(lane < 64, both[0], both[1]))
            y_ref[:, sl] = jnp.concatenate(pairs, axis=1) + yoff + q["xs"][:, sl] * q["d_x"][:, sl]

    return pl.pallas_call(
        body, name="ssd_scan_fwd", grid=(nc,),
        in_specs=[pl.BlockSpec((lc, SSM_CONV_DIM), lambda c: (c, 0)),
                  pl.BlockSpec((lc, LANES), lambda c: (c, 0)),
                  pl.BlockSpec((8, LANES), lambda c: (0, 0)),
                  pl.BlockSpec((LANES, SSM_D_INNER), lambda c: (0, 0))],
        out_specs=[pl.BlockSpec((lc, SSM_D_INNER), lambda c: (c, 0)),
                   pl.BlockSpec((None, SSM_GROUPS, SSM_STATE, GROUP_W), lambda c: (c, 0, 0, 0))],
        out_shape=[jax.ShapeDtypeStruct((s, SSM_D_INNER), F32),
                   jax.ShapeDtypeStruct((nc, SSM_GROUPS, SSM_STATE, GROUP_W), F32)],
        scratch_shapes=[pltpu.VMEM((SSM_GROUPS, SSM_STATE, GROUP_W), F32)],
        compiler_params=_params(VMEM_BIG),
    )(xc, dtr, par, _mx(jnp.asarray(ex)))


def _ssd_scan_bwd(dy, xc, dtr, par, prev):
    s = xc.shape[0]
    lc = SSM_CHUNK
    nc = s // lc
    ex, ex_t = _expand_constants()

    def body(dy_ref, xc_ref, dtr_ref, par_ref, prev_ref, ex_ref, ext_ref, dxc_ref, ddtr_ref, sums_ref,
             gst_ref, tacs_ref, tdt_ref, tdd_ref):
        @pl.when(pl.program_id(0) == 0)
        def _():
            gst_ref[...] = jnp.zeros_like(gst_ref)
            sums_ref[...] = jnp.zeros_like(sums_ref)

        q = _ssd_common(dtr_ref, par_ref, ex_ref, xc_ref)
        lane = lax.broadcasted_iota(jnp.int32, (1, LANES), 1)
        row = lax.broadcasted_iota(jnp.int32, (lc, 1), 0)
        dacs_rows = jnp.zeros((lc, LANES), F32)
        dacs_cols_t = jnp.zeros((LANES, lc), F32)
        for g in range(SSM_GROUPS):
            sl = slice(g * GROUP_W, (g + 1) * GROUP_W)
            b_lo = SSM_D_INNER + g * SSM_STATE
            c_lo = SSM_D_INNER + (SSM_GROUPS + g) * SSM_STATE
            bg = _mx(xc_ref[:, b_lo:b_lo + SSM_STATE])
            cg = _mx(xc_ref[:, c_lo:c_lo + SSM_STATE])
            dyg = dy_ref[:, sl]
            xsg, xdtg = q["xs"][:, sl], q["xdt"][:, sl]
            eg, dteg, cdg = q["e_x"][:, sl], q["dte_x"][:, sl], q["cd_x"][:, sl]
            prevg = prev_ref[g]
            gs = gst_ref[g]
            prevm, gsm = _mx(prevg), _mx(gs)
            tdd_ref[:, sl] = dyg * xsg
            dxs = dyg * q["d_x"][:, sl]
            t_acs = dyg * _dot(cg, prevm) * eg
            dcp = _mx(dyg * eg)
            dc = _dot(dcp, prevm, NT)
            dprev = _dot(cg, dcp, TN)
            db = _dot(_mx(xdtg * dteg), gsm, NT)
            dx2 = _dot(bg, gsm)
            dxdt = dx2 * dteg
            ddte = dx2 * xdtg * dteg
            t_acs = t_acs - ddte
            last = (jnp.sum(ddte, axis=0, keepdims=True)
                    + jnp.sum(gs * prevg, axis=0, keepdims=True) * cdg)
            gm = _dot(cg, bg, NT)
            dgm = jnp.zeros((lc, lc), F32)
            pair_dx = []
            for pr in range(2):
                lo = g * GROUP_W + pr * LANES
                xp = _mx(q["xdt"][:, lo:lo + LANES])
                dyp = dy_ref[:, lo:lo + LANES]
                both = []
                for r2 in range(2):
                    h = 4 * g + 2 * pr + r2
                    mine = (lane >= 64 * r2) & (lane < 64 * (r2 + 1))
                    lm = _decay_in(q, h)
                    m = gm * lm
                    dm = _dot(_mx(jnp.where(mine, dyp, 0.0)), xp, NT)
                    dgm = dgm + dm * lm
                    w = dm * m
                    dacs_rows = dacs_rows + jnp.sum(w, axis=1, keepdims=True) * (lane == h).astype(F32)
                    head_row = (lax.broadcasted_iota(jnp.int32, (LANES, 1), 0) == h).astype(F32)
                    dacs_cols_t = dacs_cols_t + head_row * jnp.sum(w, axis=0, keepdims=True)
                    both.append(_dot(_mx(m), _mx(dyp), TN))
                pair_dx.append(jnp.where(lane < 64, both[0], both[1]))
            dxdt = dxdt + jnp.concatenate(pair_dx, axis=1)
            dgmm = _mx(dgm)
            dc = dc + _dot(dgmm, bg)
            db = db + _dot(dgmm, cg, TN)
            dxs = dxs + dxdt * q["dt_x"][:, sl]
            tdt_ref[:, sl] = dxdt * xsg
            tacs_ref[:, sl] = t_acs + jnp.where(row == lc - 1, last, 0.0)
            dxc_ref[:, sl] = dxs
            dxc_ref[:, b_lo:b_lo + SSM_STATE] = db
            dxc_ref[:, c_lo:c_lo + SSM_STATE] = dc
            gst_ref[g] = gs * cdg + dprev
        tdd = jnp.broadcast_to(jnp.sum(tdd_ref[...], axis=0, keepdims=True), (8, SSM_D_INNER))
        heads_of = _dot_onehot(jnp.concatenate([tacs_ref[...], tdt_ref[...], tdd], axis=0), ext_ref[...])
        dacs = heads_of[0:lc] + dacs_rows - dacs_cols_t.T
        dda = _dot_onehot(q["tri"], dacs, TN, onehot="a")
        ddt = dda * q["a"] + heads_of[lc:2 * lc]
        ddtr = jnp.where(q["is_head"], ddt * jax.nn.sigmoid(q["pre"]), 0.0)
        ddtr_ref[...] = _mx(ddtr)
        sums_ref[0:1, :] += jnp.sum(ddtr, axis=0, keepdims=True)
        sums_ref[1:2, :] += jnp.sum(dda * q["dt"], axis=0, keepdims=True) * q["a"]
        sums_ref[2:3, :] += heads_of[2 * lc:2 * lc + 1]

    rev = lambda c: nc - 1 - c
    wide = pltpu.VMEM((lc, SSM_D_INNER), F32)
    return pl.pallas_call(
        body, name="ssd_scan_bwd", grid=(nc,),
        in_specs=[pl.BlockSpec((lc, SSM_D_INNER), lambda c: (rev(c), 0)),
                  pl.BlockSpec((lc, SSM_CONV_DIM), lambda c: (rev(c), 0)),
                  pl.BlockSpec((lc, LANES), lambda c: (rev(c), 0)),
                  pl.BlockSpec((8, LANES), lambda c: (0, 0)),
                  pl.BlockSpec((None, SSM_GROUPS, SSM_STATE, GROUP_W), lambda c: (rev(c), 0, 0, 0)),
                  pl.BlockSpec((LANES, SSM_D_INNER), lambda c: (0, 0)),
                  pl.BlockSpec((SSM_D_INNER, LANES), lambda c: (0, 0))],
        out_specs=[pl.BlockSpec((lc, SSM_CONV_DIM), lambda c: (rev(c), 0)),
                   pl.BlockSpec((lc, LANES), lambda c: (rev(c), 0)),
                   pl.BlockSpec((8, LANES), lambda c: (0, 0))],
        out_shape=[jax.ShapeDtypeStruct((s, SSM_CONV_DIM), F32), jax.ShapeDtypeStruct((s, LANES), MXU_DTYPE),
                   jax.ShapeDtypeStruct((8, LANES), F32)],
        scratch_shapes=[pltpu.VMEM((SSM_GROUPS, SSM_STATE, GROUP_W), F32), wide, wide, wide],
        compiler_params=_params(VMEM_BIG),
    )(dy, xc, dtr, par, prev, _mx(jnp.asarray(ex)), _mx(jnp.asarray(ex_t)))


def _group_norm_parts(yv, zv):
    yg = yv * jax.nn.silu(zv)
    normed, rinvs = [], []
    for g in range(SSM_GROUPS):
        blk = yg[:, g * GROUP_W:(g + 1) * GROUP_W]
        rinv = lax.rsqrt(jnp.mean(blk * blk, axis=-1, keepdims=True) + RMS_EPS)
        normed.append(blk * rinv)
        rinvs.append(rinv)
    return normed, rinvs


def _gnorm_fwd(y, zx, nw):
    s = y.shape[0]
    tb = _tile(s, 512)

    def body(y_ref, z_ref, w_ref, o_ref):
        normed, _ = _group_norm_parts(y_ref[...], z_ref[...])
        for g in range(SSM_GROUPS):
            sl = slice(g * GROUP_W, (g + 1) * GROUP_W)
            o_ref[:, sl] = _mx(normed[g] * w_ref[0:1, sl])

    row = pl.BlockSpec((tb, SSM_D_INNER), lambda i: (i, 0))
    return pl.pallas_call(
        body, name="ssd_gnorm_fwd", grid=(s // tb,),
        in_specs=[row, row, pl.BlockSpec((8, SSM_D_INNER), lambda i: (0, 0))],
        out_specs=row, out_shape=jax.ShapeDtypeStruct((s, SSM_D_INNER), MXU_DTYPE),
    )(y, zx, nw)


def _gnorm_bwd(y, zx, nw, dyn):
    s = y.shape[0]
    tb = _tile(s, 512)

    def body(y_ref, z_ref, w_ref, d_ref, dy_ref, dz_ref, sums_ref):
        @pl.when(pl.program_id(0) == 0)
        def _():
            sums_ref[...] = jnp.zeros_like(sums_ref)

        yv, zv = y_ref[...], z_ref[...]
        normed, rinvs = _group_norm_parts(yv, zv)
        gate = jax.nn.silu(zv)
        dgate = _dsilu(zv)
        for g in range(SSM_GROUPS):
            sl = slice(g * GROUP_W, (g + 1) * GROUP_W)
            dv = d_ref[:, sl]
            n = normed[g]
            sums_ref[0:1, sl] += jnp.sum(dv * n, axis=0, keepdims=True)
            dn = dv * w_ref[0:1, sl]
            dyg = rinvs[g] * (dn - n * jnp.mean(dn * n, axis=-1, keepdims=True))
            dy_ref[:, sl] = dyg * gate[:, sl]
            dz_ref[:, sl] = _mx(dyg * yv[:, sl] * dgate[:, sl])

    row = pl.BlockSpec((tb, SSM_D_INNER), lambda i: (i, 0))
    par = pl.BlockSpec((8, SSM_D_INNER), lambda i: (0, 0))
    return pl.pallas_call(
        body, name="ssd_gnorm_bwd", grid=(s // tb,),
        in_specs=[row, row, par, row], out_specs=[row, row, par],
        out_shape=[jax.ShapeDtypeStruct((s, SSM_D_INNER), F32), jax.ShapeDtypeStruct((s, SSM_D_INNER), MXU_DTYPE),
                   jax.ShapeDtypeStruct((8, SSM_D_INNER), F32)],
    )(y, zx, nw, dyn)


def _rows8(v):
    v = v.reshape(1, -1)
    return jnp.pad(v, ((0, 7), (0, 0)))


def _ssd_weights(w_in, w_out):
    nzx = SSM_D_INNER + SSM_CONV_DIM
    wzx = w_in[:, :nzx]
    wdt = jnp.pad(w_in[:, nzx:], ((0, 0), (0, LANES - SSM_HEADS)))
    return dict(wzx=wzx, wdt=wdt, wzx_t=wzx.T, wdt_t=wdt.T, wout=w_out, wout_t=w_out.T)


def _ssd_fwd(u, w, cw, cb, par, nw):
    zx = _mm_nn("ssd_in_zx", u, w["wzx"], F32)
    dtr = _mm_nn("ssd_in_dt", u, w["wdt"], F32)
    xc = _conv_fwd(zx, cw, cb)
    y, prev = _ssd_scan_fwd(xc, dtr, par)
    yn = _gnorm_fwd(y, zx, nw)
    out = _mm_nn("ssd_out", yn, w["wout"], F32)
    return out, dict(zx=zx, dtr=dtr, xc=xc, y=y, prev=prev, yn=yn)


def _ssd_bwd(dy, u, w, cw, cb, par, nw, res):
    dyn = _mm_nn("ssd_dyn", dy, w["wout_t"], F32)
    dw_out = _mm_tn("ssd_dw_out", res["yn"], dy)
    dys, dz, nsum = _gnorm_bwd(res["y"], res["zx"], nw, dyn)
    dxc, ddtr, ssum = _ssd_scan_bwd(dys, res["xc"], res["dtr"], par, res["prev"])
    dpre, csum = _conv_bwd_pre(res["zx"], dxc, cw, cb)
    dxbc = _conv_bwd_x(dpre, cw)
    wt = w["wzx_t"]
    du = _mm_nn("ssd_du_z", dz, wt[:SSM_D_INNER], F32)
    du = _mm_nn("ssd_du_x", dxbc, wt[SSM_D_INNER:], F32, addend=du)
    du = _mm_nn("ssd_du_dt", ddtr, w["wdt_t"], F32, addend=du)
    dw_in = jnp.concatenate(
        [_mm_tn("ssd_dw_z", u, dz), _mm_tn("ssd_dw_x", u, dxbc), _mm_tn("ssd_dw_dt", u, ddtr)[:, :SSM_HEADS]], axis=1)
    small = dict(conv_w=csum[:SSM_CONV], conv_b=csum[SSM_CONV], dt_bias=ssum[0, :SSM_HEADS],
                 a_log=ssum[1, :SSM_HEADS], d=ssum[2, :SSM_HEADS], norm_w=nsum[0])
    return du, dw_in, dw_out, small


def _ada_fwd(c_all, ada_w, ada_b_mine):
    nl, _, ncol = ada_w.shape

    def body(c_ref, w_ref, b_ref, o_ref):
        cond = _mx(jax.nn.silu(c_ref[...]))
        for i in range(nl):
            o_ref[i] = _dot(cond, _mx(w_ref[i])) + b_ref[i:i + 1, :]

    return pl.pallas_call(
        body, name="ada_fwd", out_shape=jax.ShapeDtypeStruct((nl, 2 * N_DEV, ncol), F32),
        compiler_params=_params(VMEM_BIG),
    )(c_all, ada_w, ada_b_mine)


def _ada_bwd(c_all, dmod_cols):
    nl, _, ncol = dmod_cols.shape

    def body(c_ref, d_ref, o_ref):
        cond = _mx(jax.nn.silu(c_ref[...]))
        for i in range(nl):
            o_ref[i] = _dot(cond, _mx(d_ref[i]), TN)

    return pl.pallas_call(
        body, name="ada_bwd", out_shape=jax.ShapeDtypeStruct((nl, D_MODEL, ncol), F32),
        compiler_params=_params(VMEM_BIG),
    )(c_all, dmod_cols)


def _adamw(gslots, w, m, v, name):
    k, r, c = gslots.shape
    tr = _tile(r, 256) if r % 256 == 0 else r
    c1 = 1.0 - ADAM_B1 ** ADAM_STEP
    c2 = 1.0 - ADAM_B2 ** ADAM_STEP

    def body(g_ref, w_ref, m_ref, v_ref, go_ref, d_ref, mo_ref, vo_ref):
        g = g_ref[0]
        for slot in range(1, k):
            g = g + g_ref[slot]
        mn = ADAM_B1 * m_ref[...] + (1.0 - ADAM_B1) * g
        vn = ADAM_B2 * v_ref[...] + (1.0 - ADAM_B2) * jnp.square(g)
        go_ref[...] = g
        mo_ref[...] = mn
        vo_ref[...] = vn
        d_ref[...] = -ADAM_LR * ((mn / c1) / (jnp.sqrt(vn / c2) + ADAM_EPS) + ADAM_WD * w_ref[...])

    row = pl.BlockSpec((tr, c), lambda i: (i, 0))
    shp = jax.ShapeDtypeStruct((r, c), F32)
    return pl.pallas_call(
        body, name=name, grid=(r // tr,),
        in_specs=[pl.BlockSpec((k, tr, c), lambda i: (0, i, 0)), row, row, row],
        out_specs=[row, row, row, row], out_shape=[shp, shp, shp, shp],
        compiler_params=_params(VMEM_BIG),
    )(gslots, w, m, v)


def _adamw_any(gslots, w, m, v, name):
    shape = w.shape
    two_d = (-1, shape[-1])
    k = gslots.shape[0]
    outs = _adamw(gslots.reshape((k,) + w.reshape(two_d).shape), w.reshape(two_d), m.reshape(two_d),
                  v.reshape(two_d), name)
    return tuple(o.reshape(shape) for o in outs)


def _cols_from_slots(g):
    g = jnp.moveaxis(g, 0, -2)
    return g.reshape(g.shape[:-2] + (g.shape[-2] * g.shape[-1],))


def _rows_from_slots(g):
    g = jnp.moveaxis(g, 0, -3)
    return g.reshape(g.shape[:-3] + (g.shape[-3] * g.shape[-2], g.shape[-1]))


def _col_slots(g):
    cs = g.shape[-1] // N_DEV
    return jnp.moveaxis(g.reshape(g.shape[:-1] + (N_DEV, cs)), -2, 0)


def _row_slots(g):
    rs = g.shape[-2] // N_DEV
    return jnp.moveaxis(g.reshape(g.shape[:-2] + (N_DEV, rs, g.shape[-1])), -3, 0)


def _gather_cols(w, name, dtype=None):
    return _cols_from_slots(_all_gather(w.astype(dtype or MXU_DTYPE), name))


def _gather_rows(w, name):
    return _rows_from_slots(_all_gather(_mx(w), name))


def _scatter_cols(g, name):
    return _all_to_all(_col_slots(g), name)


def _scatter_rows(g, name):
    return _all_to_all(_row_slots(g), name)


def kernel(x, c, ada_w, ada_b, ln_mix_g, ln_mix_b, ln_mlp_g, ln_mlp_b, mlp_w1, mlp_w2, fox_w_in, fox_b_f, fox_w_o, ssm_w_in, ssm_conv_w, ssm_conv_b, ssm_dt_bias, ssm_a_log, ssm_d, ssm_norm_w, ssm_w_out, loss_target, m_ada_w, m_ada_b, m_ln_mix_g, m_ln_mix_b, m_ln_mlp_g, m_ln_mlp_b, m_mlp_w1, m_mlp_w2, m_fox_w_in, m_fox_b_f, m_fox_w_o, m_ssm_w_in, m_ssm_conv_w, m_ssm_conv_b, m_ssm_dt_bias, m_ssm_a_log, m_ssm_d, m_ssm_norm_w, m_ssm_w_out, v_ada_w, v_ada_b, v_ln_mix_g, v_ln_mix_b, v_ln_mlp_g, v_ln_mlp_b, v_mlp_w1, v_mlp_w2, v_fox_w_in, v_fox_b_f, v_fox_w_o, v_ssm_w_in, v_ssm_conv_w, v_ssm_conv_b, v_ssm_dt_bias, v_ssm_a_log, v_ssm_d, v_ssm_norm_w, v_ssm_w_out):
    me = 4 * lax.axis_index("x") + 2 * lax.axis_index("y") + lax.axis_index("c")
    xs = x[0]
    target = loss_target[0]
    d = D_MODEL

    c_all = _all_gather(c, "gather_c").reshape(N_DEV, d)
    c_all = jnp.pad(c_all, ((0, N_DEV), (0, 0)))
    ncol = ada_w.shape[-1]
    ada_b_mine = lax.dynamic_slice_in_dim(ada_b, me * ncol, ncol, axis=1)
    mod_cols = _ada_fwd(c_all, ada_w, ada_b_mine)
    mod_all = _all_gather(mod_cols, "gather_mod")
    mod = lax.dynamic_index_in_dim(mod_all, me, axis=2, keepdims=False)
    mod = jnp.moveaxis(mod, 0, 1).reshape(DEPTH, 6, d)

    def pv_rows(*rows):
        return jnp.pad(jnp.stack(rows), ((0, 8 - len(rows)), (0, 0)))

    fw = _fox_weights(_gather_cols(fox_w_in, "gather_fox_in")[0], _gather_rows(fox_w_o, "gather_fox_o")[0])
    conv_w = _gather_cols(ssm_conv_w, "gather_conv_w", F32)[0]
    small_vec = jnp.concatenate([ssm_conv_b[0], ssm_norm_w[0]]).reshape(1, -1)
    small_all = _all_gather(small_vec.astype(F32), "gather_conv_b").reshape(N_DEV, -1)
    conv_b = small_all[:, :SSM_CONV_DIM // N_DEV].reshape(-1)
    norm_w = small_all[:, SSM_CONV_DIM // N_DEV:].reshape(-1)
    cw8 = jnp.pad(conv_w, ((0, 8 - SSM_CONV), (0, 0)))
    cb8 = _rows8(conv_b)
    nw8 = _rows8(norm_w)
    bf8 = _rows8(jnp.pad(fox_b_f[0], (0, LANES - FOX_HEADS)))
    par8 = jnp.pad(jnp.stack([jnp.pad(p[0], (0, LANES - SSM_HEADS)) for p in (ssm_dt_bias, ssm_a_log, ssm_d)]),
                   ((0, 5), (0, 0)))

    sh_a, sc_a, g_a, sh_m, sc_m, g_m = [mod[:, k] for k in range(6)]
    u0 = _modulate(xs, pv_rows(1.0 + sc_a[0], sh_a[0]), "modulate0")
    y0, fres, gathered = _fox_fwd(u0, fw, bf8, [(False, _mx(w)) for w in (mlp_w1, mlp_w2, ssm_w_in, ssm_w_out)])
    w1 = _cols_from_slots(gathered[0])
    w2 = _rows_from_slots(gathered[1])
    sw = _ssd_weights(_cols_from_slots(gathered[2])[0], _rows_from_slots(gathered[3])[0])
    pv0 = pv_rows(1.0 + g_a[0], ln_mix_g[0], ln_mix_b[0], 1.0 + sc_m[0], sh_m[0])
    x1, u1 = _ln_fwd(xs, y0, pv0, "ln_mix0")
    y1, (h0, a0) = _mlp_fwd(u1, w1[0], w2[0], "0")
    pv1 = pv_rows(1.0 + g_m[0], ln_mlp_g[0], ln_mlp_b[0], 1.0 + sc_a[1], sh_a[1])
    x2, u2 = _ln_fwd(x1, y1, pv1, "ln_mlp0")
    y2, sres = _ssd_fwd(u2, sw, cw8, cb8, par8, nw8)
    pv2 = pv_rows(1.0 + g_a[1], ln_mix_g[1], ln_mix_b[1], 1.0 + sc_m[1], sh_m[1])
    x3, u3 = _ln_fwd(x2, y2, pv2, "ln_mix1")
    y3, (h1, a1) = _mlp_fwd(u3, w1[1], w2[1], "1")
    pv3 = pv_rows(1.0 + g_m[1], ln_mlp_g[1], ln_mlp_b[1])

    dx3, dy3, s3 = _ln_bwd(x3, y3, pv3, "ln_mlp1_bwd", target=target)
    loss = lax.psum(s3[5, 0], ("x", "y", "c"))
    du3, dw1_1, dw2_1 = _mlp_bwd(dy3, u3, h1, a1, w1[1].T, w2[1].T, "1")
    dx2, dy2, s2 = _ln_bwd(x2, y2, pv2, "ln_mix1_bwd", dxo=dx3, du=du3)
    du2, d_ssm_in, d_ssm_out, ssm_small = _ssd_bwd(dy2, u2, sw, cw8, cb8, par8, nw8, sres)
    dx1, dy1, s1 = _ln_bwd(x1, y1, pv1, "ln_mlp0_bwd", dxo=dx2, du=du2)
    du1, dw1_0, dw2_0 = _mlp_bwd(dy1, u1, h0, a0, w1[0].T, w2[0].T, "0")
    dx0, dy0, s0 = _ln_bwd(xs, y0, pv0, "ln_mix0_bwd", dxo=dx1, du=du1)
    early = [(True, _col_slots(jnp.stack([dw1_0, dw1_1]))), (True, _row_slots(jnp.stack([dw2_0, dw2_1]))),
             (True, _col_slots(d_ssm_in[None])), (True, _row_slots(d_ssm_out[None])),
             (True, _col_slots(ssm_small["conv_w"][None])), (True, _col_slots(ssm_small["conv_b"][None])),
             (True, _col_slots(ssm_small["norm_w"][None]))]
    du0, d_fox_in, d_fox_o, fox_sums, exchanged = _fox_bwd(dy0, u0, fw, bf8, fres, early)
    grad_x, sx = _mod_bwd(dx0, du0, xs, pv_rows(1.0 + sc_a[0], sh_a[0]), "modulate0_bwd")

    dmod = jnp.stack([
        jnp.stack([sx[1], sx[0], s0[4], s0[1], s0[0], s1[4]]),
        jnp.stack([s1[1], s1[0], s2[4], s2[1], s2[0], s3[4]]),
    ]).reshape(DEPTH, 6 * d)

    def pad_rows(v):
        v = v.reshape(-1, LANES) if v.size % LANES == 0 else jnp.pad(v.reshape(-1), (0, LANES - v.size)).reshape(1, LANES)
        return jnp.pad(v, ((0, (-v.shape[0]) % 8), (0, 0)))

    small_parts = [dmod, jnp.stack([s0[2], s2[2]]), jnp.stack([s0[3], s2[3]]), jnp.stack([s1[2], s3[2]]),
                   jnp.stack([s1[3], s3[3]]), fox_sums[0, :FOX_HEADS], ssm_small["dt_bias"], ssm_small["a_log"],
                   ssm_small["d"]]
    packed = [pad_rows(p) for p in small_parts]
    offsets = np.cumsum([0] + [p.shape[0] for p in packed])
    small_all_g = _all_gather(jnp.concatenate(packed, axis=0), "gather_small_grads")

    def unpack(idx, shape):
        n = int(np.prod(shape))
        blk = small_all_g[:, offsets[idx]:offsets[idx + 1]].reshape(N_DEV, -1)[:, :n]
        return blk.reshape((N_DEV,) + tuple(shape))

    dmod_all = unpack(0, (DEPTH, 6 * d))
    dmod_cols = lax.dynamic_slice_in_dim(dmod_all, me * ncol, ncol, axis=2)
    dmod_cols = jnp.pad(jnp.moveaxis(dmod_cols, 0, 1), ((0, 0), (0, N_DEV), (0, 0)))
    g_ada_w = _ada_bwd(c_all, dmod_cols)

    shards = dict(
        mlp_w1=exchanged[0], mlp_w2=exchanged[1], ssm_w_in=exchanged[2], ssm_w_out=exchanged[3],
        ssm_conv_w=exchanged[4], ssm_conv_b=exchanged[5], ssm_norm_w=exchanged[6],
        fox_w_in=_scatter_cols(d_fox_in[None], "scatter_fox_in"), fox_w_o=_scatter_rows(d_fox_o[None], "scatter_fox_o"),
        ada_w=g_ada_w[None], ada_b=dmod_all,
        ln_mix_g=unpack(1, (DEPTH, d)), ln_mix_b=unpack(2, (DEPTH, d)),
        ln_mlp_g=unpack(3, (DEPTH, d)), ln_mlp_b=unpack(4, (DEPTH, d)),
        fox_b_f=unpack(5, (1, FOX_HEADS)), ssm_dt_bias=unpack(6, (1, SSM_HEADS)),
        ssm_a_log=unpack(7, (1, SSM_HEADS)), ssm_d=unpack(8, (1, SSM_HEADS)),
    )
    weights = dict(ada_w=ada_w, ada_b=ada_b, ln_mix_g=ln_mix_g, ln_mix_b=ln_mix_b, ln_mlp_g=ln_mlp_g, ln_mlp_b=ln_mlp_b,
                   mlp_w1=mlp_w1, mlp_w2=mlp_w2, fox_w_in=fox_w_in, fox_b_f=fox_b_f, fox_w_o=fox_w_o, ssm_w_in=ssm_w_in,
                   ssm_conv_w=ssm_conv_w, ssm_conv_b=ssm_conv_b, ssm_dt_bias=ssm_dt_bias, ssm_a_log=ssm_a_log,
                   ssm_d=ssm_d, ssm_norm_w=ssm_norm_w, ssm_w_out=ssm_w_out)
    mom1 = dict(ada_w=m_ada_w, ada_b=m_ada_b, ln_mix_g=m_ln_mix_g, ln_mix_b=m_ln_mix_b, ln_mlp_g=m_ln_mlp_g,
                ln_mlp_b=m_ln_mlp_b, mlp_w1=m_mlp_w1, mlp_w2=m_mlp_w2, fox_w_in=m_fox_w_in, fox_b_f=m_fox_b_f,
                fox_w_o=m_fox_w_o, ssm_w_in=m_ssm_w_in, ssm_conv_w=m_ssm_conv_w, ssm_conv_b=m_ssm_conv_b,
                ssm_dt_bias=m_ssm_dt_bias, ssm_a_log=m_ssm_a_log, ssm_d=m_ssm_d, ssm_norm_w=m_ssm_norm_w,
                ssm_w_out=m_ssm_w_out)
    mom2 = dict(ada_w=v_ada_w, ada_b=v_ada_b, ln_mix_g=v_ln_mix_g, ln_mix_b=v_ln_mix_b, ln_mlp_g=v_ln_mlp_g,
                ln_mlp_b=v_ln_mlp_b, mlp_w1=v_mlp_w1, mlp_w2=v_mlp_w2, fox_w_in=v_fox_w_in, fox_b_f=v_fox_b_f,
                fox_w_o=v_fox_w_o, ssm_w_in=v_ssm_w_in, ssm_conv_w=v_ssm_conv_w, ssm_conv_b=v_ssm_conv_b,
                ssm_dt_bias=v_ssm_dt_bias, ssm_a_log=v_ssm_a_log, ssm_d=v_ssm_d, ssm_norm_w=v_ssm_norm_w,
                ssm_w_out=v_ssm_w_out)
    names = list(weights)
    stepped = {n: _adamw_any(shards[n], weights[n], mom1[n], mom2[n], f"adamw_{n}") for n in names}
    return (loss, grad_x[None], *[stepped[n][0] for n in names], *[stepped[n][1] for n in names],
            *[stepped[n][2] for n in names], *[stepped[n][3] for n in names])
```

```python
import numpy as np
import jax
import jax.numpy as jnp
from jax import lax
from jax.experimental import pallas as pl
from jax.experimental.pallas import tpu as pltpu

F32 = jnp.float32
MXU_DTYPE = jnp.bfloat16
HIGHEST = lax.Precision.HIGHEST

N_DEV = 8
D_MODEL = 1024
DEPTH = 2
FOX_HEADS = 16
FOX_HEAD_DIM = 64
D_FF = 4096
SSM_D_INNER = 2048
SSM_HEADS = 32
SSM_GROUPS = 8
SSM_STATE = 128
SSM_CHUNK = 128
SSM_CONV = 4
SSM_CONV_DIM = 4096
GROUP_W = SSM_D_INNER // SSM_GROUPS
LN_EPS = 1e-5
RMS_EPS = 1e-5
ALPHA = (2.0 * DEPTH) ** 0.25
LANES = 128
SUBLANES = 8

ADAM_LR = 0.001
ADAM_B1 = 0.9
ADAM_B2 = 0.999
ADAM_EPS = 1e-08
ADAM_WD = 0.01
ADAM_STEP = 10

NN = (((1,), (0,)), ((), ()))
NT = (((1,), (1,)), ((), ()))
TN = (((0,), (0,)), ((), ()))

VMEM_BIG = 56 * 1024 * 1024


def _dot(a, b, dims=NN, precision=None):
    return lax.dot_general(a, b, dims, precision=precision, preferred_element_type=F32)


def _mx(v):
    return v.astype(MXU_DTYPE)


def _pieces3(v):
    hi = _mx(v)
    r1 = v - hi.astype(F32)
    mid = _mx(r1)
    return hi, mid, _mx(r1 - mid.astype(F32))


def _dot_onehot(a, b, dims=NN, onehot="b"):
    if onehot == "b":
        return sum(_dot(p, _mx(b), dims) for p in _pieces3(a))
    return sum(_dot(_mx(a), p, dims) for p in _pieces3(b))


def _params(vmem=None):
    return pltpu.CompilerParams(vmem_limit_bytes=vmem) if vmem else None


def _all_gather(x, name):
    def body(x_ref, out_ref, send_sems, recv_sems, local_sem):
        xi, yi, ci = lax.axis_index("x"), lax.axis_index("y"), lax.axis_index("c")
        me, sibling = (xi, yi, ci), (xi, yi, 1 - ci)
        chips = [(1 - xi, yi), (xi, 1 - yi), (1 - xi, 1 - yi)]

        def slot(px, py, pc):
            return out_ref.at[4 * px + 2 * py + pc]

        def copy(k, block, to, src=None):
            return pltpu.make_async_remote_copy(
                src_ref=slot(*block) if src is None else src, dst_ref=slot(*block),
                send_sem=send_sems.at[k], recv_sem=recv_sems.at[k],
                device_id=to, device_id_type=pl.DeviceIdType.MESH)

        mine = pltpu.make_async_copy(x_ref, slot(*me), local_sem)
        mine.start()
        first = [copy(0, me, sibling, src=x_ref)]
        first += [copy(1 + j, me, (*chip, ci), src=x_ref) for j, chip in enumerate(chips)]
        for cp in first:
            cp.start()
        passed = [copy(4 + j, (*chip, ci), sibling) for j, chip in enumerate(chips)]
        for j, chip in enumerate(chips):
            copy(1 + j, (*chip, ci), me).wait_recv()
            passed[j].start()
        copy(0, sibling, me).wait_recv()
        for j, chip in enumerate(chips):
            copy(4 + j, (*chip, 1 - ci), me).wait_recv()
        for cp in first + passed:
            cp.wait_send()
        mine.wait()

    return pl.pallas_call(
        body, name=name,
        out_shape=jax.ShapeDtypeStruct((N_DEV,) + x.shape, x.dtype),
        in_specs=[pl.BlockSpec(memory_space=pl.ANY)],
        out_specs=pl.BlockSpec(memory_space=pl.ANY),
        scratch_shapes=[pltpu.SemaphoreType.DMA((7,)), pltpu.SemaphoreType.DMA((7,)),
                        pltpu.SemaphoreType.DMA],
    )(x)


EXCHANGE_PIECES = 1


def _direct_copies(scatter, x_ref, out_ref, send_sems, recv_sems, local_sems, n, piece=None):
    xi, yi, ci = lax.axis_index("x"), lax.axis_index("y"), lax.axis_index("c")
    me = 4 * xi + 2 * yi + ci

    def part(ref):
        return ref if piece is None else ref.at[piece]

    local = pltpu.make_async_copy(part(x_ref.at[me] if scatter else x_ref), part(out_ref.at[me]), local_sems.at[n])
    remote = []
    for k in range(1, N_DEV):
        px = 1 - xi if k & 4 else xi
        py = 1 - yi if k & 2 else yi
        pc = 1 - ci if k & 1 else ci
        remote.append(pltpu.make_async_remote_copy(
            src_ref=part(x_ref.at[4 * px + 2 * py + pc] if scatter else x_ref), dst_ref=part(out_ref.at[me]),
            send_sem=send_sems.at[7 * n + k - 1], recv_sem=recv_sems.at[7 * n + k - 1],
            device_id=(px, py, pc), device_id_type=pl.DeviceIdType.MESH))
    return local, remote


def _in_pieces(carry, steps):
    out = []
    for scatter, a in carry:
        body = a.shape[1:] if scatter else a.shape
        rows = int(np.prod(body[:-1])) if len(body) > 1 else 1
        align = SUBLANES * (4 // a.dtype.itemsize)
        pieces = 1
        while (pieces * 2 <= min(EXCHANGE_PIECES, steps // 2) and rows % (pieces * 2 * align) == 0):
            pieces *= 2
        shape = (pieces, rows // pieces, body[-1])
        out.append((scatter, a.reshape(((N_DEV,) if scatter else ()) + shape), pieces, max(1, (steps // 2) // pieces)))
    return out


def _exchange_shapes(carry):
    return [jax.ShapeDtypeStruct(c[1].shape if c[0] else (N_DEV,) + c[1].shape, c[1].dtype) for c in carry]


def _exchange_sems(carry):
    n = max(len(carry), 1)
    return [pltpu.SemaphoreType.DMA((7 * n,)), pltpu.SemaphoreType.DMA((7 * n,)), pltpu.SemaphoreType.DMA((n,))]


def _exchange_start(carry, srcs, dsts, sems, step=None):
    for n, entry in enumerate(carry):
        scatter = entry[0]
        pieces, stride = (entry[2], entry[3]) if len(entry) > 2 else (1, 1)
        local, remote = _direct_copies(scatter, srcs[n], dsts[n], *sems, n)
        if step is None:
            local.start()
            for cp in remote:
                cp.start()
            continue

        @pl.when(step == 0)
        def _():
            local.start()
            if pieces == 1:
                for cp in remote:
                    cp.start()

        if pieces > 1:
            @pl.when((step % stride == 0) & (step // stride < pieces))
            def _():
                for cp in _direct_copies(scatter, srcs[n], dsts[n], *sems, n, piece=step // stride)[1]:
                    cp.start()


def _exchange_wait(carry, srcs, dsts, sems):
    for n, entry in enumerate(carry):
        local, remote = _direct_copies(entry[0], srcs[n], dsts[n], *sems, n)
        for cp in remote:
            cp.wait()
        local.wait()


def _all_to_all(x, name):
    carry = [(True, x)]

    def body(x_ref, out_ref, *sems):
        _exchange_start(carry, [x_ref], [out_ref], sems)
        _exchange_wait(carry, [x_ref], [out_ref], sems)

    return pl.pallas_call(
        body, name=name,
        out_shape=jax.ShapeDtypeStruct(x.shape, x.dtype),
        in_specs=[pl.BlockSpec(memory_space=pl.ANY)],
        out_specs=pl.BlockSpec(memory_space=pl.ANY),
        scratch_shapes=_exchange_sems(carry),
    )(x)


def _mm(name, a, b, *, grid, a_spec, b_spec, dims, k_axis, outs, acc=None, extras=(), epi=None, vmem=None):
    nk = grid[k_axis]
    n_ex, n_out = len(extras), len(outs)

    def body(*refs):
        a_ref, b_ref = refs[0], refs[1]
        ex = refs[2:2 + n_ex]
        out = refs[2 + n_ex:2 + n_ex + n_out]

        def finish(val):
            if epi is None:
                out[0][...] = val.astype(out[0].dtype)
            else:
                epi(val, ex, out)

        part = _dot(a_ref[...], b_ref[...], dims)
        if nk == 1:
            finish(part)
        else:
            acc_ref = refs[2 + n_ex + n_out]
            k = pl.program_id(k_axis)

            @pl.when(k == 0)
            def _():
                acc_ref[...] = part

            @pl.when(k > 0)
            def _():
                acc_ref[...] += part

            @pl.when(k == nk - 1)
            def _():
                finish(acc_ref[...])

    return pl.pallas_call(
        body, name=name, grid=grid,
        in_specs=[a_spec, b_spec] + [s for _, s in extras],
        out_specs=[s for _, s in outs],
        out_shape=[o for o, _ in outs],
        scratch_shapes=[pltpu.VMEM(acc, F32)] if nk > 1 else [],
        compiler_params=_params(vmem),
    )(a, b, *[e for e, _ in extras])


def _tile(n, t):
    t = min(n, t)
    assert n % t == 0, (n, t)
    return t


def _mm_nn(name, a, b, out_dtype, *, addend=None, tm=1024, tn=1024, tk=1024, epi=None, extras=(), outs=None):
    m, kk = a.shape
    n = b.shape[1]
    tm, tn, tk = _tile(m, tm), _tile(n, tn), _tile(kk, tk)
    o_spec = pl.BlockSpec((tm, tn), lambda i, j, k: (i, j))
    if outs is None:
        outs = [(jax.ShapeDtypeStruct((m, n), out_dtype), o_spec)]
    extras = list(extras)
    if addend is not None:
        extras = [(addend, o_spec)] + extras

        def epi(val, ex, out):
            out[0][...] = (val + ex[0][...].astype(F32)).astype(out[0].dtype)

    res = _mm(name, a, b, grid=(m // tm, n // tn, kk // tk),
              a_spec=pl.BlockSpec((tm, tk), lambda i, j, k: (i, k)),
              b_spec=pl.BlockSpec((tk, tn), lambda i, j, k: (k, j)),
              dims=NN, k_axis=2, acc=(tm, tn), outs=outs, extras=extras, epi=epi, vmem=VMEM_BIG)
    return res[0] if len(res) == 1 else res


def _mm_tn(name, a, b, out_dtype=F32, *, tm=1024, tn=1024, tk=1024):
    kk, m = a.shape
    n = b.shape[1]
    tm, tn, tk = _tile(m, tm), _tile(n, tn), _tile(kk, tk)
    res = _mm(name, a, b, grid=(m // tm, n // tn, kk // tk),
              a_spec=pl.BlockSpec((tk, tm), lambda i, j, k: (k, i)),
              b_spec=pl.BlockSpec((tk, tn), lambda i, j, k: (k, j)),
              dims=TN, k_axis=2, acc=(tm, tn),
              outs=[(jax.ShapeDtypeStruct((m, n), out_dtype), pl.BlockSpec((tm, tn), lambda i, j, k: (i, j)))],
              vmem=VMEM_BIG)
    return res[0]


def _row_block(s):
    return _tile(s, 512)


def _modulate(x, pv, name):
    s, d = x.shape
    tb = _row_block(s)

    def body(x_ref, pv_ref, u_ref):
        u_ref[...] = _mx(x_ref[...] * pv_ref[0:1, :] + pv_ref[1:2, :])

    return pl.pallas_call(
        body, name=name, grid=(s // tb,),
        in_specs=[pl.BlockSpec((tb, d), lambda i: (i, 0)), pl.BlockSpec((8, d), lambda i: (0, 0))],
        out_specs=pl.BlockSpec((tb, d), lambda i: (i, 0)),
        out_shape=jax.ShapeDtypeStruct((s, d), MXU_DTYPE),
    )(x, pv)


def _ln_stats(r):
    mu = jnp.mean(r, axis=-1, keepdims=True)
    xc = r - mu
    var = jnp.mean(xc * xc, axis=-1, keepdims=True)
    rstd = lax.rsqrt(var + LN_EPS)
    return xc * rstd, rstd


def _ln_fwd(xin, y, pv, name):
    s, d = xin.shape
    tb = _row_block(s)

    def body(x_ref, y_ref, pv_ref, xo_ref, u_ref):
        r = ALPHA * x_ref[...] + pv_ref[0:1, :] * y_ref[...]
        xhat, _ = _ln_stats(r)
        xo = xhat * pv_ref[1:2, :] + pv_ref[2:3, :]
        xo_ref[...] = xo
        u_ref[...] = _mx(xo * pv_ref[3:4, :] + pv_ref[4:5, :])

    row = pl.BlockSpec((tb, d), lambda i: (i, 0))
    return pl.pallas_call(
        body, name=name, grid=(s // tb,),
        in_specs=[row, row, pl.BlockSpec((8, d), lambda i: (0, 0))],
        out_specs=[row, row],
        out_shape=[jax.ShapeDtypeStruct((s, d), F32), jax.ShapeDtypeStruct((s, d), MXU_DTYPE)],
    )(xin, y, pv)


def _ln_bwd(xin, y, pv, name, *, dxo=None, du=None, target=None):
    s, d = xin.shape
    tb = _row_block(s)
    nb = s // tb
    loss_mode = target is not None

    def body(*refs):
        if loss_mode:
            x_ref, y_ref, pv_ref, t_ref, dxin_ref, dy_ref, sums_ref = refs
        else:
            x_ref, y_ref, pv_ref, dxo_ref, du_ref, dxin_ref, dy_ref, sums_ref = refs
        i = pl.program_id(0)

        @pl.when(i == 0)
        def _():
            sums_ref[...] = jnp.zeros_like(sums_ref)

        yv = y_ref[...]
        r = ALPHA * x_ref[...] + pv_ref[0:1, :] * yv
        xhat, rstd = _ln_stats(r)
        xo = xhat * pv_ref[1:2, :] + pv_ref[2:3, :]
        if loss_mode:
            diff = xo - t_ref[...]
            dxo_v = diff * (1.0 / d)
            sums_ref[5:6, :] += jnp.sum(diff * diff, axis=0, keepdims=True) * (0.5 / d)
        else:
            duv = du_ref[...]
            dxo_v = dxo_ref[...] + duv * pv_ref[3:4, :]
            sums_ref[0:1, :] += jnp.sum(duv * xo, axis=0, keepdims=True)
            sums_ref[1:2, :] += jnp.sum(duv, axis=0, keepdims=True)
        sums_ref[2:3, :] += jnp.sum(dxo_v * xhat, axis=0, keepdims=True)
        sums_ref[3:4, :] += jnp.sum(dxo_v, axis=0, keepdims=True)
        dxh = dxo_v * pv_ref[1:2, :]
        dr = rstd * (dxh - jnp.mean(dxh, axis=-1, keepdims=True)
                     - xhat * jnp.mean(dxh * xhat, axis=-1, keepdims=True))
        sums_ref[4:5, :] += jnp.sum(dr * yv, axis=0, keepdims=True)
        dxin_ref[...] = ALPHA * dr
        dy_ref[...] = _mx(pv_ref[0:1, :] * dr)
        if loss_mode:
            @pl.when(i == nb - 1)
            def _():
                sums_ref[5:6, :] = jnp.broadcast_to(jnp.sum(sums_ref[5:6, :], axis=-1, keepdims=True), (1, d))

    row = pl.BlockSpec((tb, d), lambda i: (i, 0))
    par = pl.BlockSpec((8, d), lambda i: (0, 0))
    ins = [xin, y, pv] + ([target] if loss_mode else [dxo, du])
    return pl.pallas_call(
        body, name=name, grid=(nb,),
        in_specs=[row, row, par] + [row] * (len(ins) - 3),
        out_specs=[row, row, par],
        out_shape=[jax.ShapeDtypeStruct((s, d), F32), jax.ShapeDtypeStruct((s, d), MXU_DTYPE),
                   jax.ShapeDtypeStruct((8, d), F32)],
    )(*ins)


def _mod_bwd(dx_direct, du, x, pv, name):
    s, d = x.shape
    tb = _row_block(s)

    def body(dxd_ref, du_ref, x_ref, pv_ref, dx_ref, sums_ref):
        @pl.when(pl.program_id(0) == 0)
        def _():
            sums_ref[...] = jnp.zeros_like(sums_ref)

        duv = du_ref[...]
        dx_ref[...] = dxd_ref[...] + duv * pv_ref[0:1, :]
        sums_ref[0:1, :] += jnp.sum(duv * x_ref[...], axis=0, keepdims=True)
        sums_ref[1:2, :] += jnp.sum(duv, axis=0, keepdims=True)

    row = pl.BlockSpec((tb, d), lambda i: (i, 0))
    par = pl.BlockSpec((8, d), lambda i: (0, 0))
    return pl.pallas_call(
        body, name=name, grid=(s // tb,),
        in_specs=[row, row, row, par], out_specs=[row, par],
        out_shape=[jax.ShapeDtypeStruct((s, d), F32), jax.ShapeDtypeStruct((8, d), F32)],
    )(dx_direct, du, x, pv)


def _mlp_fwd(u, w1, w2, tag):
    s = u.shape[0]

    def epi(val, ex, out):
        out[0][...] = _mx(val)
        out[1][...] = _mx(jnp.square(jnp.maximum(val, 0.0)))

    tm, tn = _tile(s, 1024), 1024
    spec = pl.BlockSpec((tm, tn), lambda i, j, k: (i, j))
    shp = jax.ShapeDtypeStruct((s, D_FF), MXU_DTYPE)
    h, a = _mm_nn(f"mlp_up{tag}", u, w1, None, epi=epi, outs=[(shp, spec), (shp, spec)], tn=tn)
    y = _mm_nn(f"mlp_down{tag}", a, w2, F32)
    return y, (h, a)


def _mlp_bwd(dy, u, h, a, w1t, w2t, tag):
    s = u.shape[0]
    tm, tn = _tile(s, 1024), 1024
    spec = pl.BlockSpec((tm, tn), lambda i, j, k: (i, j))

    def epi(val, ex, out):
        out[0][...] = _mx(val * (2.0 * jnp.maximum(ex[0][...].astype(F32), 0.0)))

    dh = _mm_nn(f"mlp_dh{tag}", dy, w2t, None, epi=epi, extras=[(h, spec)],
                outs=[(jax.ShapeDtypeStruct((s, D_FF), MXU_DTYPE), spec)], tn=tn)
    du = _mm_nn(f"mlp_du{tag}", dh, w1t, F32)
    dw2 = _mm_tn(f"mlp_dw2{tag}", a, dy)
    dw1 = _mm_tn(f"mlp_dw1{tag}", u, dh)
    return du, dw1, dw2


FOX_T = 1024
BIAS_Q = (64, 65, 66)
BIAS_K = (67, 68, 69)
SKIP_MARGIN = 110.0
ONES_V = 64

def _fox_constants():
    selq = np.zeros((FOX_HEADS, 512, LANES), np.float32)
    selk = np.zeros((FOX_HEADS, 512, LANES), np.float32)
    selv = np.zeros((2, LANES, LANES), np.float32)
    put = np.zeros((2, 2, LANES, LANES), np.float32)
    for h in range(FOX_HEADS):
        off = FOX_HEAD_DIM * (h % 2)
        for dd in range(FOX_HEAD_DIM):
            selq[h, off + dd, dd] = FOX_HEAD_DIM ** -0.5
            selk[h, off + dd, dd] = 1.0
        for piece in range(3):
            selq[h, LANES * (1 + piece) + h, BIAS_Q[piece]] = 1.0
            selk[h, LANES * (1 + piece) + h, BIAS_K[piece]] = -1.0
    for par in range(2):
        for dd in range(FOX_HEAD_DIM):
            selv[par, FOX_HEAD_DIM * par + dd, dd] = 1.0
            put[par, 0, dd, FOX_HEAD_DIM * par + dd] = FOX_HEAD_DIM ** -0.5
            put[par, 1, dd, FOX_HEAD_DIM * par + dd] = 1.0
    return selq, selk, selv, put


def _pairs(nb, by_key):
    if by_key:
        pr = [(i, j) for j in range(nb) for i in range(j, nb)]
    else:
        pr = [(i, j) for i in range(nb) for j in range(i + 1)]
    return (np.array([p[0] for p in pr], np.int32), np.array([p[1] for p in pr], np.int32))


def _fox_prep(qkv, f, bf):
    s = qkv.shape[0]
    t = _tile(s, FOX_T)
    nb = s // t
    selq, selk, selv, _ = _fox_constants()

    def body(q_ref, k_ref, v_ref, f_ref, bf_ref, selq_ref, selk_ref, selv_ref,
             qa_ref, qat_ref, ka_ref, kat_ref, va_ref, vat_ref, stats_ref, parts_ref, carry_ref, cum_ref):
        i, h = pl.program_id(0), pl.program_id(1)
        lane = lax.broadcasted_iota(jnp.int32, (1, LANES), 1)

        @pl.when(h == 0)
        def _():
            @pl.when(i == 0)
            def _():
                carry_ref[...] = jnp.zeros_like(carry_ref)

            lf = jnp.where(lane < FOX_HEADS, jax.nn.log_sigmoid(f_ref[...] + bf_ref[0:1, :]), 0.0)
            tri = (lax.broadcasted_iota(jnp.int32, (t, t), 0) >= lax.broadcasted_iota(jnp.int32, (t, t), 1)).astype(F32)
            cum = _dot_onehot(tri, lf, onehot="a") + carry_ref[0:1, :]
            carry_ref[0:1, :] = cum[t - 1:t, :]
            cum_ref[...] = cum
            hi = _mx(cum)
            r1 = cum - hi.astype(F32)
            mid = _mx(r1)
            parts_ref[:, 0:LANES] = hi
            parts_ref[:, LANES:2 * LANES] = mid
            parts_ref[:, 2 * LANES:3 * LANES] = _mx(r1 - mid.astype(F32))

        parts = parts_ref[...]
        qa = _dot(jnp.concatenate([q_ref[...], parts], axis=1), selq_ref[...])
        qa = qa + jnp.where((lane >= BIAS_K[0]) & (lane <= BIAS_K[2]), 1.0, 0.0)
        ka = _dot(jnp.concatenate([k_ref[...], parts], axis=1), selk_ref[...])
        ka = ka + jnp.where((lane >= BIAS_Q[0]) & (lane <= BIAS_Q[2]), 1.0, 0.0)
        va = _dot(v_ref[...], selv_ref[...]) + jnp.where(lane == ONES_V, 1.0, 0.0)
        qa_ref[...] = _mx(qa)
        qat_ref[...] = _mx(qa.T)
        ka_ref[...] = _mx(ka)
        kat_ref[...] = _mx(ka.T)
        va_ref[...] = _mx(va)
        vat_ref[...] = _mx(va.T)

        def longest(rows_):
            sq = jnp.where(lane < FOX_HEAD_DIM, rows_ * rows_, 0.0)
            return jnp.sqrt(jnp.max(jnp.sum(sq, axis=1, keepdims=True), axis=0, keepdims=True))

        mine = lane == h
        cum = cum_ref[...]
        top = jnp.max(jnp.max(jnp.where(mine, cum, -jnp.inf), axis=1, keepdims=True), axis=0, keepdims=True)
        low = jnp.min(jnp.min(jnp.where(mine, cum, jnp.inf), axis=1, keepdims=True), axis=0, keepdims=True)
        row = lax.broadcasted_iota(jnp.int32, (8, LANES), 0)
        stats_ref[...] = jnp.where(row == 0, longest(qa), jnp.where(row == 1, longest(ka),
                                                                    jnp.where(row == 2, top, low)))

    rows = jax.ShapeDtypeStruct((FOX_HEADS, nb, t, LANES), MXU_DTYPE)
    cols = jax.ShapeDtypeStruct((FOX_HEADS, nb, LANES, t), MXU_DTYPE)
    rspec = pl.BlockSpec((None, None, t, LANES), lambda i, h: (h, i, 0, 0))
    cspec = pl.BlockSpec((None, None, LANES, t), lambda i, h: (h, i, 0, 0))
    npair = FOX_HEADS // 2
    return pl.pallas_call(
        body, name="fox_prep", grid=(nb, FOX_HEADS),
        in_specs=[pl.BlockSpec((t, LANES), lambda i, h: (i, h // 2)),
                  pl.BlockSpec((t, LANES), lambda i, h: (i, npair + h // 2)),
                  pl.BlockSpec((t, LANES), lambda i, h: (i, 2 * npair + h // 2)),
                  pl.BlockSpec((t, LANES), lambda i, h: (i, 0)),
                  pl.BlockSpec((8, LANES), lambda i, h: (0, 0)),
                  pl.BlockSpec((None, 512, LANES), lambda i, h: (h, 0, 0)),
                  pl.BlockSpec((None, 512, LANES), lambda i, h: (h, 0, 0)),
                  pl.BlockSpec((None, LANES, LANES), lambda i, h: (h % 2, 0, 0))],
        out_specs=[rspec, cspec, rspec, cspec, rspec, cspec,
                   pl.BlockSpec((None, None, 8, LANES), lambda i, h: (h, i, 0, 0))],
        out_shape=[rows, cols, rows, cols, rows, cols, jax.ShapeDtypeStruct((FOX_HEADS, nb, 8, LANES), F32)],
        scratch_shapes=[pltpu.VMEM((t, 3 * LANES), MXU_DTYPE), pltpu.VMEM((8, LANES), F32),
                        pltpu.VMEM((t, LANES), F32)],
        compiler_params=_params(VMEM_BIG),
    )(qkv, qkv, qkv, f, bf, _mx(jnp.asarray(selq)), _mx(jnp.asarray(selk)), _mx(jnp.asarray(selv)))


def _fox_active(stats, by_key):
    qn, kn, top, low = (stats[:, :, r, 0] for r in range(4))
    im, jm = _pairs(qn.shape[1], by_key)
    cols = []
    for i, j in zip(im.tolist(), jm.tolist()):
        gap = qn[:, i] * kn[:, j] + top[:, i] - low[:, j] + qn[:, i] * kn[:, i]
        cols.append(jnp.where((gap > -SKIP_MARGIN) | (i == j), 1.0, 0.0))
    return jnp.stack(cols, axis=1).astype(F32)


def _causal_allow(t):
    return lax.broadcasted_iota(jnp.int32, (t, t), 0) <= lax.broadcasted_iota(jnp.int32, (t, t), 1)


def _fox_attn_fwd(qat, ka, vat, active, carry=()):
    heads, nb, _, t = qat.shape
    im, jm = _pairs(nb, by_key=False)
    npairs = len(im)
    whole = _exchange_shapes(list(carry))
    carry = _in_pieces(list(carry), heads * npairs)
    nc = len(carry)

    def body(im_ref, jm_ref, act_ref, qat_ref, ka_ref, vat_ref, *rest):
        srcs, (ot_ref, lse_ref), dsts = rest[:nc], rest[nc:nc + 2], rest[nc + 2:2 * nc + 2]
        acc_ref, m_ref = rest[2 * nc + 2:2 * nc + 4]
        sems = rest[2 * nc + 4:]
        h, p = pl.program_id(0), pl.program_id(1)
        i, j = im_ref[p], jm_ref[p]

        if nc:
            _exchange_start(carry, srcs, dsts, sems, step=h * npairs + p)

        @pl.when(j == 0)
        def _():
            m_ref[...] = jnp.full_like(m_ref, -jnp.inf)
            acc_ref[...] = jnp.zeros_like(acc_ref)

        def step(diagonal):
            st = _dot(ka_ref[...], qat_ref[...])
            if diagonal:
                st = jnp.where(_causal_allow(t), st, -jnp.inf)
            m_old = m_ref[...]
            m_new = jnp.maximum(m_old, jnp.max(st, axis=0, keepdims=True))
            pt = jnp.exp(st - m_new)
            acc_ref[...] = acc_ref[...] * jnp.exp(m_old - m_new) + _dot(vat_ref[...], _mx(pt))
            m_ref[...] = m_new

        @pl.when((j < i) & (act_ref[h, p] > 0.5))
        def _():
            step(False)

        @pl.when(j == i)
        def _():
            step(True)
            acc = acc_ref[...]
            denom = acc[ONES_V:ONES_V + 1, :]
            ot_ref[...] = _mx(acc / denom)
            lse_ref[...] = m_ref[...] + jnp.log(denom)

        if nc:
            @pl.when((h == heads - 1) & (p == npairs - 1))
            def _():
                _exchange_wait(carry, srcs, dsts, sems)

    anywhere = pl.BlockSpec(memory_space=pl.ANY)
    qspec = pl.BlockSpec((None, None, LANES, t), lambda h, p, im, jm: (h, im[p], 0, 0))
    grid_spec = pltpu.PrefetchScalarGridSpec(
        num_scalar_prefetch=2, grid=(heads, npairs),
        in_specs=[pl.BlockSpec(memory_space=pltpu.SMEM), qspec,
                  pl.BlockSpec((None, None, t, LANES), lambda h, p, im, jm: (h, jm[p], 0, 0)),
                  pl.BlockSpec((None, None, LANES, t), lambda h, p, im, jm: (h, jm[p], 0, 0))] + [anywhere] * nc,
        out_specs=[qspec, pl.BlockSpec((None, None, 1, t), lambda h, p, im, jm: (h, im[p], 0, 0))] + [anywhere] * nc,
        scratch_shapes=[pltpu.VMEM((LANES, t), F32), pltpu.VMEM((1, t), F32)] + (_exchange_sems(carry) if nc else []))
    res = pl.pallas_call(
        body, name="fox_attn_fwd", grid_spec=grid_spec,
        out_shape=[jax.ShapeDtypeStruct((heads, nb, LANES, t), MXU_DTYPE),
                   jax.ShapeDtypeStruct((heads, nb, 1, t), F32)] + _exchange_shapes(carry),
        compiler_params=_params(VMEM_BIG),
    )(jnp.asarray(im), jnp.asarray(jm), active, qat, ka, vat, *[c[1] for c in carry])
    return res[0], res[1], [r.reshape(w.shape) for r, w in zip(res[2:], whole)]


def _fox_attn_bwd(qa, qat, ka, kat, va, ot, lse, do, dot_, active, carry=()):
    heads, nb, t, _ = qa.shape
    im, jm = _pairs(nb, by_key=True)
    npairs = len(im)
    whole = _exchange_shapes(list(carry))
    carry = _in_pieces(list(carry), heads * npairs)
    nc = len(carry)

    def body(im_ref, jm_ref, act_ref, qa_ref, qat_ref, ka_ref, kat_ref, va_ref, ot_ref, lse_ref, do_ref, dot_ref,
             *rest):
        srcs, (dqt_ref, dka_ref, dva_ref), dsts = rest[:nc], rest[nc:nc + 3], rest[nc + 3:2 * nc + 3]
        sems = rest[2 * nc + 3:]
        h, p = pl.program_id(0), pl.program_id(1)
        i, j = im_ref[p], jm_ref[p]

        if nc:
            _exchange_start(carry, srcs, dsts, sems, step=h * npairs + p)

        @pl.when(p == 0)
        def _():
            dqt_ref[...] = jnp.zeros_like(dqt_ref)

        def step(diagonal):
            st = _dot(ka_ref[...], qat_ref[...])
            dot_v = dot_ref[...]
            delta = jnp.sum(ot_ref[...].astype(F32) * dot_v.astype(F32), axis=0, keepdims=True)
            pt = jnp.exp(st - lse_ref[...])
            if diagonal:
                pt = jnp.where(_causal_allow(t), pt, 0.0)
            dsm = _mx(pt * (_dot(va_ref[...], dot_v) - delta))
            upd_v = _dot(_mx(pt), do_ref[...])
            upd_k = _dot(dsm, qa_ref[...])
            if diagonal:
                dva_ref[...] = upd_v
                dka_ref[...] = upd_k
            else:
                dva_ref[...] += upd_v
                dka_ref[...] += upd_k
            dqt_ref[i] += _dot(kat_ref[...], dsm)

        @pl.when(i == j)
        def _():
            step(True)

        @pl.when((i > j) & (act_ref[h, p] > 0.5))
        def _():
            step(False)

        if nc:
            @pl.when((h == heads - 1) & (p == npairs - 1))
            def _():
                _exchange_wait(carry, srcs, dsts, sems)

    def at_q(shape):
        return pl.BlockSpec((None, None) + shape, lambda h, p, im, jm: (h, im[p], 0, 0))

    def at_k(shape):
        return pl.BlockSpec((None, None) + shape, lambda h, p, im, jm: (h, jm[p], 0, 0))

    anywhere = pl.BlockSpec(memory_space=pl.ANY)
    grid_spec = pltpu.PrefetchScalarGridSpec(
        num_scalar_prefetch=2, grid=(heads, npairs),
        in_specs=[pl.BlockSpec(memory_space=pltpu.SMEM),
                  at_q((t, LANES)), at_q((LANES, t)), at_k((t, LANES)), at_k((LANES, t)), at_k((t, LANES)),
                  at_q((LANES, t)), at_q((1, t)), at_q((t, LANES)), at_q((LANES, t))] + [anywhere] * nc,
        out_specs=[pl.BlockSpec((None, nb, LANES, t), lambda h, p, im, jm: (h, 0, 0, 0)),
                   at_k((t, LANES)), at_k((t, LANES))] + [anywhere] * nc,
        scratch_shapes=_exchange_sems(carry) if nc else [])
    res = pl.pallas_call(
        body, name="fox_attn_bwd", grid_spec=grid_spec,
        out_shape=[jax.ShapeDtypeStruct((heads, nb, LANES, t), F32),
                   jax.ShapeDtypeStruct((heads, nb, t, LANES), F32),
                   jax.ShapeDtypeStruct((heads, nb, t, LANES), F32)] + _exchange_shapes(carry),
        compiler_params=_params(VMEM_BIG),
    )(jnp.asarray(im), jnp.asarray(jm), active, qa, qat, ka, kat, va, ot, lse, do, dot_, *[c[1] for c in carry])
    return res[0], res[1], res[2], [r.reshape(w.shape) for r, w in zip(res[3:], whole)]


def _fox_post(dqt, dka, dva, f, bf):
    heads, nb, t, _ = dka.shape
    s = nb * t
    _, _, _, put = _fox_constants()
    npair = heads // 2

    def body(dqt_ref, dka_ref, dva_ref, f_ref, bf_ref, put_ref, dq_ref, dk_ref, dv_ref, df_ref, sums_ref,
             dc_ref, carry_ref):
        i, h = pl.program_id(0), pl.program_id(1)

        @pl.when((i == 0) & (h == 0))
        def _():
            carry_ref[...] = jnp.zeros_like(carry_ref)
            sums_ref[...] = jnp.zeros_like(sums_ref)

        @pl.when(h == 0)
        def _():
            dc_ref[...] = jnp.zeros_like(dc_ref)

        dqt_v = dqt_ref[...]
        dka_v = dka_ref[...]
        term_q = _dot(_mx(dqt_v), put_ref[0], TN)
        term_k = _dot(_mx(dka_v), put_ref[1])
        term_v = _dot(_mx(dva_ref[...]), put_ref[1])

        @pl.when(h % 2 == 0)
        def _():
            dq_ref[...] = _mx(term_q)
            dk_ref[...] = _mx(term_k)
            dv_ref[...] = _mx(term_v)

        @pl.when(h % 2 == 1)
        def _():
            dq_ref[...] += _mx(term_q)
            dk_ref[...] += _mx(term_k)
            dv_ref[...] += _mx(term_v)

        dcum = dqt_v[BIAS_Q[0]:BIAS_Q[0] + 1, :] - dka_v.T[BIAS_K[0]:BIAS_K[0] + 1, :]
        head_row = lax.broadcasted_iota(jnp.int32, (heads, 1), 0) == h
        dc_ref[...] += jnp.where(head_row, dcum, 0.0)

        @pl.when(h == heads - 1)
        def _():
            later = (lax.broadcasted_iota(jnp.int32, (t, t), 0) >= lax.broadcasted_iota(jnp.int32, (t, t), 1)).astype(F32)
            dlf_t = _dot_onehot(dc_ref[...], later) + carry_ref[:, 0:1]
            carry_ref[...] = jnp.broadcast_to(dlf_t[:, 0:1], carry_ref.shape)
            dlf = jnp.concatenate([dlf_t, jnp.zeros((LANES - heads, t), F32)], axis=0).T
            lane = lax.broadcasted_iota(jnp.int32, (1, LANES), 1)
            df = jnp.where(lane < heads, dlf * jax.nn.sigmoid(-(f_ref[...] + bf_ref[0:1, :])), 0.0)
            df_ref[...] = _mx(df)
            sums_ref[0:1, :] += jnp.sum(df, axis=0, keepdims=True)

    rev = lambda i: nb - 1 - i
    pair_spec = pl.BlockSpec((t, LANES), lambda i, h: (rev(i), h // 2))
    blk = pl.BlockSpec((t, LANES), lambda i, h: (rev(i), 0))
    hd = jax.ShapeDtypeStruct((s, D_MODEL), MXU_DTYPE)
    return pl.pallas_call(
        body, name="fox_post", grid=(nb, heads),
        in_specs=[pl.BlockSpec((None, None, LANES, t), lambda i, h: (h, rev(i), 0, 0)),
                  pl.BlockSpec((None, None, t, LANES), lambda i, h: (h, rev(i), 0, 0)),
                  pl.BlockSpec((None, None, t, LANES), lambda i, h: (h, rev(i), 0, 0)),
                  blk, pl.BlockSpec((8, LANES), lambda i, h: (0, 0)),
                  pl.BlockSpec((None, 2, LANES, LANES), lambda i, h: (h % 2, 0, 0, 0))],
        out_specs=[pair_spec, pair_spec, pair_spec, blk, pl.BlockSpec((8, LANES), lambda i, h: (0, 0))],
        out_shape=[hd, hd, hd, jax.ShapeDtypeStruct((s, LANES), MXU_DTYPE), jax.ShapeDtypeStruct((8, LANES), F32)],
        scratch_shapes=[pltpu.VMEM((heads, t), F32), pltpu.VMEM((heads, LANES), F32)],
        compiler_params=_params(VMEM_BIG),
    )(dqt, dka, dva, f, bf, _mx(jnp.asarray(put)))


def _fox_weights(w_in, w_o):
    wqkv = w_in[:, :3 * D_MODEL]
    wf = jnp.pad(w_in[:, 3 * D_MODEL:], ((0, 0), (0, LANES - FOX_HEADS)))
    wo_heads = w_o.reshape(FOX_HEADS, FOX_HEAD_DIM, D_MODEL)
    wo_a = jnp.pad(wo_heads, ((0, 0), (0, LANES - FOX_HEAD_DIM), (0, 0)))
    wo_rows = wo_a.reshape(FOX_HEADS * LANES, D_MODEL)
    return dict(wqkv=wqkv, wf=wf, wqkv_t=wqkv.T, wf_t=wf.T, wo_rows=wo_rows, wo_rows_t=wo_rows.T)


def _fox_out(ot, wo_rows):
    heads, nb, _, t = ot.shape

    def body(ot_ref, w_ref, y_ref):
        y_ref[...] = _dot(ot_ref[...].reshape(heads * LANES, t), w_ref[...], TN)

    return pl.pallas_call(
        body, name="fox_out", grid=(nb,),
        in_specs=[pl.BlockSpec((heads, None, LANES, t), lambda i: (0, i, 0, 0)),
                  pl.BlockSpec((heads * LANES, D_MODEL), lambda i: (0, 0))],
        out_specs=pl.BlockSpec((t, D_MODEL), lambda i: (i, 0)),
        out_shape=jax.ShapeDtypeStruct((nb * t, D_MODEL), F32),
        compiler_params=_params(VMEM_BIG),
    )(ot, wo_rows)


def _fox_do(dy, wo_rows_t, nb, t):
    heads = FOX_HEADS

    def body(dy_ref, w_ref, do_ref, dot_ref):
        val = _dot(dy_ref[...], w_ref[...])
        for h in range(heads):
            blk = val[:, h * LANES:(h + 1) * LANES]
            do_ref[h] = _mx(blk)
            dot_ref[h] = _mx(blk.T)

    return pl.pallas_call(
        body, name="fox_do", grid=(nb,),
        in_specs=[pl.BlockSpec((t, D_MODEL), lambda i: (i, 0)),
                  pl.BlockSpec((D_MODEL, heads * LANES), lambda i: (0, 0))],
        out_specs=[pl.BlockSpec((heads, None, t, LANES), lambda i: (0, i, 0, 0)),
                   pl.BlockSpec((heads, None, LANES, t), lambda i: (0, i, 0, 0))],
        out_shape=[jax.ShapeDtypeStruct((heads, nb, t, LANES), MXU_DTYPE),
                   jax.ShapeDtypeStruct((heads, nb, LANES, t), MXU_DTYPE)],
        compiler_params=_params(VMEM_BIG),
    )(dy, wo_rows_t)


def _fox_dwo(ot, dy):
    heads, nb, _, t = ot.shape

    def body(ot_ref, dy_ref, o_ref):
        part = _dot(ot_ref[...].reshape(heads * LANES, t), dy_ref[...])

        @pl.when(pl.program_id(0) == 0)
        def _():
            o_ref[...] = part

        @pl.when(pl.program_id(0) > 0)
        def _():
            o_ref[...] += part

    return pl.pallas_call(
        body, name="fox_dwo", grid=(nb,),
        in_specs=[pl.BlockSpec((heads, None, LANES, t), lambda i: (0, i, 0, 0)),
                  pl.BlockSpec((t, D_MODEL), lambda i: (i, 0))],
        out_specs=pl.BlockSpec((heads * LANES, D_MODEL), lambda i: (0, 0)),
        out_shape=jax.ShapeDtypeStruct((heads * LANES, D_MODEL), F32),
        compiler_params=_params(VMEM_BIG),
    )(ot, dy)


def _fox_fwd(u, w, bf, carry=()):
    qkv = _mm_nn("fox_qkv", u, w["wqkv"], MXU_DTYPE)
    f = _mm_nn("fox_f", u, w["wf"], F32)
    qa, qat, ka, kat, va, vat, stats = _fox_prep(qkv, f, bf)
    ot, lse, carried = _fox_attn_fwd(qat, ka, vat, _fox_active(stats, by_key=False), carry)
    y = _fox_out(ot, w["wo_rows"])
    return y, dict(f=f, qa=qa, qat=qat, ka=ka, kat=kat, va=va, ot=ot, lse=lse, stats=stats), carried


def _fox_bwd(dy, u, w, bf, res, carry=()):
    heads, nb, t, _ = res["qa"].shape
    do, dot_ = _fox_do(dy, w["wo_rows_t"], nb, t)
    dwo_a = _fox_dwo(res["ot"], dy).reshape(heads, LANES, D_MODEL)
    dqt, dka, dva, carried = _fox_attn_bwd(res["qa"], res["qat"], res["ka"], res["kat"], res["va"], res["ot"],
                                           res["lse"], do, dot_, _fox_active(res["stats"], by_key=True), carry)
    dq, dk, dv, df, sums = _fox_post(dqt, dka, dva, res["f"], bf)
    wt = w["wqkv_t"]
    du = _mm_nn("fox_du_q", dq, wt[:D_MODEL], F32)
    du = _mm_nn("fox_du_k", dk, wt[D_MODEL:2 * D_MODEL], F32, addend=du)
    du = _mm_nn("fox_du_v", dv, wt[2 * D_MODEL:], F32, addend=du)
    du = _mm_nn("fox_du_f", df, w["wf_t"], F32, addend=du)
    dw_in = jnp.concatenate(
        [_mm_tn("fox_dw_q", u, dq), _mm_tn("fox_dw_k", u, dk), _mm_tn("fox_dw_v", u, dv),
         _mm_tn("fox_dw_f", u, df)[:, :FOX_HEADS]], axis=1)
    dw_o = dwo_a[:, :FOX_HEAD_DIM, :].reshape(D_MODEL, D_MODEL)
    return du, dw_in, dw_o, sums, carried


def _dsilu(v):
    sg = jax.nn.sigmoid(v)
    return sg * (1.0 + v * (1.0 - sg))


def _conv_taps(scr_ref, w_ref, rows, base):
    acc = None
    for k in range(SSM_CONV):
        term = scr_ref[pl.ds(base - (SSM_CONV - 1) + k, rows), :] * w_ref[k:k + 1, :]
        acc = term if acc is None else acc + term
    return acc


def _conv_fwd(zx, cw, cb):
    s = zx.shape[0]
    tb = _tile(s, 512)
    half = SSM_CONV_DIM // 2
    hb = tb // SUBLANES

    def body(x_ref, halo_ref, w_ref, b_ref, o_ref, scr_ref):
        i = pl.program_id(0)
        scr_ref[pl.ds(0, SUBLANES), :] = jnp.where(i > 0, halo_ref[...], 0.0)
        scr_ref[pl.ds(SUBLANES, tb), :] = x_ref[...]
        o_ref[...] = jax.nn.silu(_conv_taps(scr_ref, w_ref, tb, SUBLANES) + b_ref[0:1, :])

    return pl.pallas_call(
        body, name="ssd_conv_fwd", grid=(s // tb, 2),
        in_specs=[pl.BlockSpec((tb, half), lambda i, j: (i, 1 + j)),
                  pl.BlockSpec((SUBLANES, half), lambda i, j: (jnp.maximum(i * hb - 1, 0), 1 + j)),
                  pl.BlockSpec((8, half), lambda i, j: (0, j)),
                  pl.BlockSpec((8, half), lambda i, j: (0, j))],
        out_specs=pl.BlockSpec((tb, half), lambda i, j: (i, j)),
        out_shape=jax.ShapeDtypeStruct((s, SSM_CONV_DIM), F32),
        scratch_shapes=[pltpu.VMEM((tb + SUBLANES, half), F32)],
    )(zx, zx, cw, cb)


def _conv_bwd_pre(zx, dxc, cw, cb):
    s = zx.shape[0]
    tb = _tile(s, 512)
    half = SSM_CONV_DIM // 2
    hb = tb // SUBLANES

    def body(x_ref, halo_ref, d_ref, w_ref, b_ref, o_ref, sums_ref, scr_ref):
        i = pl.program_id(1)

        @pl.when(i == 0)
        def _():
            sums_ref[...] = jnp.zeros_like(sums_ref)

        scr_ref[pl.ds(0, SUBLANES), :] = jnp.where(i > 0, halo_ref[...], 0.0)
        scr_ref[pl.ds(SUBLANES, tb), :] = x_ref[...]
        pre = _conv_taps(scr_ref, w_ref, tb, SUBLANES) + b_ref[0:1, :]
        dpre = d_ref[...] * _dsilu(pre)
        o_ref[...] = dpre
        for k in range(SSM_CONV):
            shifted = scr_ref[pl.ds(SUBLANES - (SSM_CONV - 1) + k, tb), :]
            sums_ref[k:k + 1, :] += jnp.sum(dpre * shifted, axis=0, keepdims=True)
        sums_ref[SSM_CONV:SSM_CONV + 1, :] += jnp.sum(dpre, axis=0, keepdims=True)

    return pl.pallas_call(
        body, name="ssd_conv_bwd_pre", grid=(2, s // tb),
        in_specs=[pl.BlockSpec((tb, half), lambda j, i: (i, 1 + j)),
                  pl.BlockSpec((SUBLANES, half), lambda j, i: (jnp.maximum(i * hb - 1, 0), 1 + j)),
                  pl.BlockSpec((tb, half), lambda j, i: (i, j)),
                  pl.BlockSpec((8, half), lambda j, i: (0, j)),
                  pl.BlockSpec((8, half), lambda j, i: (0, j))],
        out_specs=[pl.BlockSpec((tb, half), lambda j, i: (i, j)),
                   pl.BlockSpec((8, half), lambda j, i: (0, j))],
        out_shape=[jax.ShapeDtypeStruct((s, SSM_CONV_DIM), F32), jax.ShapeDtypeStruct((8, SSM_CONV_DIM), F32)],
        scratch_shapes=[pltpu.VMEM((tb + SUBLANES, half), F32)],
    )(zx, zx, dxc, cw, cb)


def _conv_bwd_x(dpre, cw):
    s = dpre.shape[0]
    tb = _tile(s, 512)
    hb = tb // SUBLANES
    nb = s // tb

    def body(d_ref, halo_ref, w_ref, o_ref, scr_ref):
        i = pl.program_id(0)
        scr_ref[pl.ds(0, tb), :] = d_ref[...]
        scr_ref[pl.ds(tb, SUBLANES), :] = jnp.where(i < nb - 1, halo_ref[...], 0.0)
        acc = None
        for k in range(SSM_CONV):
            term = scr_ref[pl.ds(SSM_CONV - 1 - k, tb), :] * w_ref[k:k + 1, :]
            acc = term if acc is None else acc + term
        o_ref[...] = _mx(acc)

    return pl.pallas_call(
        body, name="ssd_conv_bwd_x", grid=(nb,),
        in_specs=[pl.BlockSpec((tb, SSM_CONV_DIM), lambda i: (i, 0)),
                  pl.BlockSpec((SUBLANES, SSM_CONV_DIM), lambda i: (jnp.minimum((i + 1) * hb, s // SUBLANES - 1), 0)),
                  pl.BlockSpec((8, SSM_CONV_DIM), lambda i: (0, 0))],
        out_specs=pl.BlockSpec((tb, SSM_CONV_DIM), lambda i: (i, 0)),
        out_shape=jax.ShapeDtypeStruct((s, SSM_CONV_DIM), MXU_DTYPE),
        scratch_shapes=[pltpu.VMEM((tb + SUBLANES, SSM_CONV_DIM), F32)],
        compiler_params=_params(VMEM_BIG),
    )(dpre, dpre, cw)


def _expand_constants():
    ex = np.zeros((LANES, SSM_D_INNER), np.float32)
    for h in range(SSM_HEADS):
        ex[h, h * 64:(h + 1) * 64] = 1.0
    return ex, np.ascontiguousarray(ex.T)


def _ssd_common(dtr_ref, par_ref, ex_ref, xc_ref):
    lc = SSM_CHUNK
    lane = lax.broadcasted_iota(jnp.int32, (1, LANES), 1)
    is_head = lane < SSM_HEADS
    par = par_ref[...]
    pre = dtr_ref[...] + par[0:1, :]
    dt = jnp.where(is_head, jax.nn.softplus(pre), 0.0)
    a = jnp.where(is_head, -jnp.exp(par[1:2, :]), 0.0)
    tri_b = lax.broadcasted_iota(jnp.int32, (lc, lc), 0) >= lax.broadcasted_iota(jnp.int32, (lc, lc), 1)
    tri = tri_b.astype(F32)
    da = dt * a
    acs = _dot_onehot(tri, da, onehot="a")
    acs_t = _dot_onehot(da, tri, (((0,), (1,)), ((), ())))
    wide = _dot_onehot(jnp.concatenate([dt, acs, par], axis=0), ex_ref[...])
    dt_x, acs_x, d_x = wide[0:lc], wide[lc:2 * lc], wide[2 * lc + 2:2 * lc + 3]
    last_x = acs_x[lc - 1:lc, :]
    xs = xc_ref[:, 0:SSM_D_INNER]
    return dict(pre=pre, dt=dt, a=a, tri_b=tri_b, tri=tri, acs=acs, acs_t=acs_t, dt_x=dt_x, d_x=d_x, xs=xs,
                xdt=xs * dt_x, e_x=jnp.exp(acs_x), dte_x=jnp.exp(last_x - acs_x), cd_x=jnp.exp(last_x),
                is_head=is_head)


def _decay_in(q, h):
    seg = q["acs"][:, h:h + 1] - q["acs_t"][h:h + 1, :]
    return jnp.exp(jnp.where(q["tri_b"], seg, -jnp.inf))


def _ssd_scan_fwd(xc, dtr, par):
    s = xc.shape[0]
    lc = SSM_CHUNK
    nc = s // lc
    ex, _ = _expand_constants()

    def body(xc_ref, dtr_ref, par_ref, ex_ref, y_ref, prev_ref, st_ref):
        @pl.when(pl.program_id(0) == 0)
        def _():
            st_ref[...] = jnp.zeros_like(st_ref)

        q = _ssd_common(dtr_ref, par_ref, ex_ref, xc_ref)
        lane = lax.broadcasted_iota(jnp.int32, (1, LANES), 1)
        for g in range(SSM_GROUPS):
            sl = slice(g * GROUP_W, (g + 1) * GROUP_W)
            bg = _mx(xc_ref[:, SSM_D_INNER + g * SSM_STATE:SSM_D_INNER + (g + 1) * SSM_STATE])
            cg = _mx(xc_ref[:, SSM_D_INNER + (SSM_GROUPS + g) * SSM_STATE:SSM_D_INNER + (SSM_GROUPS + g + 1) * SSM_STATE])
            gm = _dot(cg, bg, NT)
            prev = st_ref[g]
            prev_ref[g] = prev
            yoff = _dot(cg, _mx(prev)) * q["e_x"][:, sl]
            st_ref[g] = prev * q["cd_x"][:, sl] + _dot(bg, _mx(q["xdt"][:, sl] * q["dte_x"][:, sl]), TN)
            pairs = []
            for pr in range(2):
                xp = _mx(q["xdt"][:, g * GROUP_W + pr * LANES:g * GROUP_W + (pr + 1) * LANES])
                both = [_dot(_mx(gm * _decay_in(q, 4 * g + 2 * pr + r2)), xp) for r2 in range(2)]
                pairs.append(jnp.where(lane < 64, both[0], both[1]))
            y_ref[:, sl] = jnp.concatenate(pairs, axis=1) + yoff + q["xs"][:, sl] * q["d_x"][:, sl]

    return pl.pallas_call(
        body, name="ssd_scan_fwd", grid=(nc,),
        in_specs=[pl.BlockSpec((lc, SSM_CONV_DIM), lambda c: (c, 0)),
                  pl.BlockSpec((lc, LANES), lambda c: (c, 0)),
                  pl.BlockSpec((8, LANES), lambda c: (0, 0)),
                  pl.BlockSpec((LANES, SSM_D_INNER), lambda c: (0, 0))],
        out_specs=[pl.BlockSpec((lc, SSM_D_INNER), lambda c: (c, 0)),
                   pl.BlockSpec((None, SSM_GROUPS, SSM_STATE, GROUP_W), lambda c: (c, 0, 0, 0))],
        out_shape=[jax.ShapeDtypeStruct((s, SSM_D_INNER), F32),
                   jax.ShapeDtypeStruct((nc, SSM_GROUPS, SSM_STATE, GROUP_W), F32)],
        scratch_shapes=[pltpu.VMEM((SSM_GROUPS, SSM_STATE, GROUP_W), F32)],
        compiler_params=_params(VMEM_BIG),
    )(xc, dtr, par, _mx(jnp.asarray(ex)))


def _ssd_scan_bwd(dy, xc, dtr, par, prev):
    s = xc.shape[0]
    lc = SSM_CHUNK
    nc = s // lc
    ex, ex_t = _expand_constants()

    def body(dy_ref, xc_ref, dtr_ref, par_ref, prev_ref, ex_ref, ext_ref, dxc_ref, ddtr_ref, sums_ref,
             gst_ref, tacs_ref, tdt_ref, tdd_ref):
        @pl.when(pl.program_id(0) == 0)
        def _():
            gst_ref[...] = jnp.zeros_like(gst_ref)
            sums_ref[...] = jnp.zeros_like(sums_ref)

        q = _ssd_common(dtr_ref, par_ref, ex_ref, xc_ref)
        lane = lax.broadcasted_iota(jnp.int32, (1, LANES), 1)
        row = lax.broadcasted_iota(jnp.int32, (lc, 1), 0)
        dacs_rows = jnp.zeros((lc, LANES), F32)
        dacs_cols_t = jnp.zeros((LANES, lc), F32)
        for g in range(SSM_GROUPS):
            sl = slice(g * GROUP_W, (g + 1) * GROUP_W)
            b_lo = SSM_D_INNER + g * SSM_STATE
            c_lo = SSM_D_INNER + (SSM_GROUPS + g) * SSM_STATE
            bg = _mx(xc_ref[:, b_lo:b_lo + SSM_STATE])
            cg = _mx(xc_ref[:, c_lo:c_lo + SSM_STATE])
            dyg = dy_ref[:, sl]
            xsg, xdtg = q["xs"][:, sl], q["xdt"][:, sl]
            eg, dteg, cdg = q["e_x"][:, sl], q["dte_x"][:, sl], q["cd_x"][:, sl]
            prevg = prev_ref[g]
            gs = gst_ref[g]
            prevm, gsm = _mx(prevg), _mx(gs)
            tdd_ref[:, sl] = dyg * xsg
            dxs = dyg * q["d_x"][:, sl]
            t_acs = dyg * _dot(cg, prevm) * eg
            dcp = _mx(dyg * eg)
            dc = _dot(dcp, prevm, NT)
            dprev = _dot(cg, dcp, TN)
            db = _dot(_mx(xdtg * dteg), gsm, NT)
            dx2 = _dot(bg, gsm)
            dxdt = dx2 * dteg
            ddte = dx2 * xdtg * dteg
            t_acs = t_acs - ddte
            last = (jnp.sum(ddte, axis=0, keepdims=True)
                    + jnp.sum(gs * prevg, axis=0, keepdims=True) * cdg)
            gm = _dot(cg, bg, NT)
            dgm = jnp.zeros((lc, lc), F32)
            pair_dx = []
            for pr in range(2):
                lo = g * GROUP_W + pr * LANES
                xp = _mx(q["xdt"][:, lo:lo + LANES])
                dyp = dy_ref[:, lo:lo + LANES]
                both = []
                for r2 in range(2):
                    h = 4 * g + 2 * pr + r2
                    mine = (lane >= 64 * r2) & (lane < 64 * (r2 + 1))
                    lm = _decay_in(q, h)
                    m = gm * lm
                    dm = _dot(_mx(jnp.where(mine, dyp, 0.0)), xp, NT)
                    dgm = dgm + dm * lm
                    w = dm * m
                    dacs_rows = dacs_rows + jnp.sum(w, axis=1, keepdims=True) * (lane == h).astype(F32)
                    head_row = (lax.broadcasted_iota(jnp.int32, (LANES, 1), 0) == h).astype(F32)
                    dacs_cols_t = dacs_cols_t + head_row * jnp.sum(w, axis=0, keepdims=True)
                    both.append(_dot(_mx(m), _mx(dyp), TN))
                pair_dx.append(jnp.where(lane < 64, both[0], both[1]))
            dxdt = dxdt + jnp.concatenate(pair_dx, axis=1)
            dgmm = _mx(dgm)
            dc = dc + _dot(dgmm, bg)
            db = db + _dot(dgmm, cg, TN)
            dxs = dxs + dxdt * q["dt_x"][:, sl]
            tdt_ref[:, sl] = dxdt * xsg
            tacs_ref[:, sl] = t_acs + jnp.where(row == lc - 1, last, 0.0)
            dxc_ref[:, sl] = dxs
            dxc_ref[:, b_lo:b_lo + SSM_STATE] = db
            dxc_ref[:, c_lo:c_lo + SSM_STATE] = dc
            gst_ref[g] = gs * cdg + dprev
        tdd = jnp.broadcast_to(jnp.sum(tdd_ref[...], axis=0, keepdims=True), (8, SSM_D_INNER))
        heads_of = _dot_onehot(jnp.concatenate([tacs_ref[...], tdt_ref[...], tdd], axis=0), ext_ref[...])
        dacs = heads_of[0:lc] + dacs_rows - dacs_cols_t.T
        dda = _dot_onehot(q["tri"], dacs, TN, onehot="a")
        ddt = dda * q["a"] + heads_of[lc:2 * lc]
        ddtr = jnp.where(q["is_head"], ddt * jax.nn.sigmoid(q["pre"]), 0.0)
        ddtr_ref[...] = _mx(ddtr)
        sums_ref[0:1, :] += jnp.sum(ddtr, axis=0, keepdims=True)
        sums_ref[1:2, :] += jnp.sum(dda * q["dt"], axis=0, keepdims=True) * q["a"]
        sums_ref[2:3, :] += heads_of[2 * lc:2 * lc + 1]

    rev = lambda c: nc - 1 - c
    wide = pltpu.VMEM((lc, SSM_D_INNER), F32)
    return pl.pallas_call(
        body, name="ssd_scan_bwd", grid=(nc,),
        in_specs=[pl.BlockSpec((lc, SSM_D_INNER), lambda c: (rev(c), 0)),
                  pl.BlockSpec((lc, SSM_CONV_DIM), lambda c: (rev(c), 0)),
                  pl.BlockSpec((lc, LANES), lambda c: (rev(c), 0)),
                  pl.BlockSpec((8, LANES), lambda c: (0, 0)),
                  pl.BlockSpec((None, SSM_GROUPS, SSM_STATE, GROUP_W), lambda c: (rev(c), 0, 0, 0)),
                  pl.BlockSpec((LANES, SSM_D_INNER), lambda c: (0, 0)),
                  pl.BlockSpec((SSM_D_INNER, LANES), lambda c: (0, 0))],
        out_specs=[pl.BlockSpec((lc, SSM_CONV_DIM), lambda c: (rev(c), 0)),
                   pl.BlockSpec((lc, LANES), lambda c: (rev(c), 0)),
                   pl.BlockSpec((8, LANES), lambda c: (0, 0))],
        out_shape=[jax.ShapeDtypeStruct((s, SSM_CONV_DIM), F32), jax.ShapeDtypeStruct((s, LANES), MXU_DTYPE),
                   jax.ShapeDtypeStruct((8, LANES), F32)],
        scratch_shapes=[pltpu.VMEM((SSM_GROUPS, SSM_STATE, GROUP_W), F32), wide, wide, wide],
        compiler_params=_params(VMEM_BIG),
    )(dy, xc, dtr, par, prev, _mx(jnp.asarray(ex)), _mx(jnp.asarray(ex_t)))


def _group_norm_parts(yv, zv):
    yg = yv * jax.nn.silu(zv)
    normed, rinvs = [], []
    for g in range(SSM_GROUPS):
        blk = yg[:, g * GROUP_W:(g + 1) * GROUP_W]
        rinv = lax.rsqrt(jnp.mean(blk * blk, axis=-1, keepdims=True) + RMS_EPS)
        normed.append(blk * rinv)
        rinvs.append(rinv)
    return normed, rinvs


def _gnorm_fwd(y, zx, nw):
    s = y.shape[0]
    tb = _tile(s, 512)

    def body(y_ref, z_ref, w_ref, o_ref):
        normed, _ = _group_norm_parts(y_ref[...], z_ref[...])
        for g in range(SSM_GROUPS):
            sl = slice(g * GROUP_W, (g + 1) * GROUP_W)
            o_ref[:, sl] = _mx(normed[g] * w_ref[0:1, sl])

    row = pl.BlockSpec((tb, SSM_D_INNER), lambda i: (i, 0))
    return pl.pallas_call(
        body, name="ssd_gnorm_fwd", grid=(s // tb,),
        in_specs=[row, row, pl.BlockSpec((8, SSM_D_INNER), lambda i: (0, 0))],
        out_specs=row, out_shape=jax.ShapeDtypeStruct((s, SSM_D_INNER), MXU_DTYPE),
    )(y, zx, nw)


def _gnorm_bwd(y, zx, nw, dyn):
    s = y.shape[0]
    tb = _tile(s, 512)

    def body(y_ref, z_ref, w_ref, d_ref, dy_ref, dz_ref, sums_ref):
        @pl.when(pl.program_id(0) == 0)
        def _():
            sums_ref[...] = jnp.zeros_like(sums_ref)

        yv, zv = y_ref[...], z_ref[...]
        normed, rinvs = _group_norm_parts(yv, zv)
        gate = jax.nn.silu(zv)
        dgate = _dsilu(zv)
        for g in range(SSM_GROUPS):
            sl = slice(g * GROUP_W, (g + 1) * GROUP_W)
            dv = d_ref[:, sl]
            n = normed[g]
            sums_ref[0:1, sl] += jnp.sum(dv * n, axis=0, keepdims=True)
            dn = dv * w_ref[0:1, sl]
            dyg = rinvs[g] * (dn - n * jnp.mean(dn * n, axis=-1, keepdims=True))
            dy_ref[:, sl] = dyg * gate[:, sl]
            dz_ref[:, sl] = _mx(dyg * yv[:, sl] * dgate[:, sl])

    row = pl.BlockSpec((tb, SSM_D_INNER), lambda i: (i, 0))
    par = pl.BlockSpec((8, SSM_D_INNER), lambda i: (0, 0))
    return pl.pallas_call(
        body, name="ssd_gnorm_bwd", grid=(s // tb,),
        in_specs=[row, row, par, row], out_specs=[row, row, par],
        out_shape=[jax.ShapeDtypeStruct((s, SSM_D_INNER), F32), jax.ShapeDtypeStruct((s, SSM_D_INNER), MXU_DTYPE),
                   jax.ShapeDtypeStruct((8, SSM_D_INNER), F32)],
    )(y, zx, nw, dyn)


def _rows8(v):
    v = v.reshape(1, -1)
    return jnp.pad(v, ((0, 7), (0, 0)))


def _ssd_weights(w_in, w_out):
    nzx = SSM_D_INNER + SSM_CONV_DIM
    wzx = w_in[:, :nzx]
    wdt = jnp.pad(w_in[:, nzx:], ((0, 0), (0, LANES - SSM_HEADS)))
    return dict(wzx=wzx, wdt=wdt, wzx_t=wzx.T, wdt_t=wdt.T, wout=w_out, wout_t=w_out.T)


def _ssd_fwd(u, w, cw, cb, par, nw):
    zx = _mm_nn("ssd_in_zx", u, w["wzx"], F32)
    dtr = _mm_nn("ssd_in_dt", u, w["wdt"], F32)
    xc = _conv_fwd(zx, cw, cb)
    y, prev = _ssd_scan_fwd(xc, dtr, par)
    yn = _gnorm_fwd(y, zx, nw)
    out = _mm_nn("ssd_out", yn, w["wout"], F32)
    return out, dict(zx=zx, dtr=dtr, xc=xc, y=y, prev=prev, yn=yn)


def _ssd_bwd(dy, u, w, cw, cb, par, nw, res):
    dyn = _mm_nn("ssd_dyn", dy, w["wout_t"], F32)
    dw_out = _mm_tn("ssd_dw_out", res["yn"], dy)
    dys, dz, nsum = _gnorm_bwd(res["y"], res["zx"], nw, dyn)
    dxc, ddtr, ssum = _ssd_scan_bwd(dys, res["xc"], res["dtr"], par, res["prev"])
    dpre, csum = _conv_bwd_pre(res["zx"], dxc, cw, cb)
    dxbc = _conv_bwd_x(dpre, cw)
    wt = w["wzx_t"]
    du = _mm_nn("ssd_du_z", dz, wt[:SSM_D_INNER], F32)
    du = _mm_nn("ssd_du_x", dxbc, wt[SSM_D_INNER:], F32, addend=du)
    du = _mm_nn("ssd_du_dt", ddtr, w["wdt_t"], F32, addend=du)
    dw_in = jnp.concatenate(
        [_mm_tn("ssd_dw_z", u, dz), _mm_tn("ssd_dw_x", u, dxbc), _mm_tn("ssd_dw_dt", u, ddtr)[:, :SSM_HEADS]], axis=1)
    small = dict(conv_w=csum[:SSM_CONV], conv_b=csum[SSM_CONV], dt_bias=ssum[0, :SSM_HEADS],
                 a_log=ssum[1, :SSM_HEADS], d=ssum[2, :SSM_HEADS], norm_w=nsum[0])
    return du, dw_in, dw_out, small


def _ada_fwd(c_all, ada_w, ada_b_mine):
    nl, _, ncol = ada_w.shape

    def body(c_ref, w_ref, b_ref, o_ref):
        cond = _mx(jax.nn.silu(c_ref[...]))
        for i in range(nl):
            o_ref[i] = _dot(cond, _mx(w_ref[i])) + b_ref[i:i + 1, :]

    return pl.pallas_call(
        body, name="ada_fwd", out_shape=jax.ShapeDtypeStruct((nl, 2 * N_DEV, ncol), F32),
        compiler_params=_params(VMEM_BIG),
    )(c_all, ada_w, ada_b_mine)


def _ada_bwd(c_all, dmod_cols):
    nl, _, ncol = dmod_cols.shape

    def body(c_ref, d_ref, o_ref):
        cond = _mx(jax.nn.silu(c_ref[...]))
        for i in range(nl):
            o_ref[i] = _dot(cond, _mx(d_ref[i]), TN)

    return pl.pallas_call(
        body, name="ada_bwd", out_shape=jax.ShapeDtypeStruct((nl, D_MODEL, ncol), F32),
        compiler_params=_params(VMEM_BIG),
    )(c_all, dmod_cols)


def _adamw(gslots, w, m, v, name):
    k, r, c = gslots.shape
    tr = _tile(r, 256) if r % 256 == 0 else r
    c1 = 1.0 - ADAM_B1 ** ADAM_STEP
    c2 = 1.0 - ADAM_B2 ** ADAM_STEP

    def body(g_ref, w_ref, m_ref, v_ref, go_ref, d_ref, mo_ref, vo_ref):
        g = g_ref[0]
        for slot in range(1, k):
            g = g + g_ref[slot]
        mn = ADAM_B1 * m_ref[...] + (1.0 - ADAM_B1) * g
        vn = ADAM_B2 * v_ref[...] + (1.0 - ADAM_B2) * jnp.square(g)
        go_ref[...] = g
        mo_ref[...] = mn
        vo_ref[...] = vn
        d_ref[...] = -ADAM_LR * ((mn / c1) / (jnp.sqrt(vn / c2) + ADAM_EPS) + ADAM_WD * w_ref[...])

    row = pl.BlockSpec((tr, c), lambda i: (i, 0))
    shp = jax.ShapeDtypeStruct((r, c), F32)
    return pl.pallas_call(
        body, name=name, grid=(r // tr,),
        in_specs=[pl.BlockSpec((k, tr, c), lambda i: (0, i, 0)), row, row, row],
        out_specs=[row, row, row, row], out_shape=[shp, shp, shp, shp],
        compiler_params=_params(VMEM_BIG),
    )(gslots, w, m, v)


def _adamw_any(gslots, w, m, v, name):
    shape = w.shape
    two_d = (-1, shape[-1])
    k = gslots.shape[0]
    outs = _adamw(gslots.reshape((k,) + w.reshape(two_d).shape), w.reshape(two_d), m.reshape(two_d),
                  v.reshape(two_d), name)
    return tuple(o.reshape(shape) for o in outs)


def _cols_from_slots(g):
    g = jnp.moveaxis(g, 0, -2)
    return g.reshape(g.shape[:-2] + (g.shape[-2] * g.shape[-1],))


def _rows_from_slots(g):
    g = jnp.moveaxis(g, 0, -3)
    return g.reshape(g.shape[:-3] + (g.shape[-3] * g.shape[-2], g.shape[-1]))


def _col_slots(g):
    cs = g.shape[-1] // N_DEV
    return jnp.moveaxis(g.reshape(g.shape[:-1] + (N_DEV, cs)), -2, 0)


def _row_slots(g):
    rs = g.shape[-2] // N_DEV
    return jnp.moveaxis(g.reshape(g.shape[:-2] + (N_DEV, rs, g.shape[-1])), -3, 0)


def _gather_cols(w, name, dtype=None):
    return _cols_from_slots(_all_gather(w.astype(dtype or MXU_DTYPE), name))


def _gather_rows(w, name):
    return _rows_from_slots(_all_gather(_mx(w), name))


def _scatter_cols(g, name):
    return _all_to_all(_col_slots(g), name)


def _scatter_rows(g, name):
    return _all_to_all(_row_slots(g), name)


def kernel(x, c, ada_w, ada_b, ln_mix_g, ln_mix_b, ln_mlp_g, ln_mlp_b, mlp_w1, mlp_w2, fox_w_in, fox_b_f, fox_w_o, ssm_w_in, ssm_conv_w, ssm_conv_b, ssm_dt_bias, ssm_a_log, ssm_d, ssm_norm_w, ssm_w_out, loss_target, m_ada_w, m_ada_b, m_ln_mix_g, m_ln_mix_b, m_ln_mlp_g, m_ln_mlp_b, m_mlp_w1, m_mlp_w2, m_fox_w_in, m_fox_b_f, m_fox_w_o, m_ssm_w_in, m_ssm_conv_w, m_ssm_conv_b, m_ssm_dt_bias, m_ssm_a_log, m_ssm_d, m_ssm_norm_w, m_ssm_w_out, v_ada_w, v_ada_b, v_ln_mix_g, v_ln_mix_b, v_ln_mlp_g, v_ln_mlp_b, v_mlp_w1, v_mlp_w2, v_fox_w_in, v_fox_b_f, v_fox_w_o, v_ssm_w_in, v_ssm_conv_w, v_ssm_conv_b, v_ssm_dt_bias, v_ssm_a_log, v_ssm_d, v_ssm_norm_w, v_ssm_w_out):
    me = 4 * lax.axis_index("x") + 2 * lax.axis_index("y") + lax.axis_index("c")
    xs = x[0]
    target = loss_target[0]
    d = D_MODEL

    c_all = _all_gather(c, "gather_c").reshape(N_DEV, d)
    c_all = jnp.pad(c_all, ((0, N_DEV), (0, 0)))
    ncol = ada_w.shape[-1]
    ada_b_mine = lax.dynamic_slice_in_dim(ada_b, me * ncol, ncol, axis=1)
    mod_cols = _ada_fwd(c_all, ada_w, ada_b_mine)
    mod_all = _all_gather(mod_cols, "gather_mod")
    mod = lax.dynamic_index_in_dim(mod_all, me, axis=2, keepdims=False)
    mod = jnp.moveaxis(mod, 0, 1).reshape(DEPTH, 6, d)

    def pv_rows(*rows):
        return jnp.pad(jnp.stack(rows), ((0, 8 - len(rows)), (0, 0)))

    fw = _fox_weights(_gather_cols(fox_w_in, "gather_fox_in")[0], _gather_rows(fox_w_o, "gather_fox_o")[0])
    conv_w = _gather_cols(ssm_conv_w, "gather_conv_w", F32)[0]
    small_vec = jnp.concatenate([ssm_conv_b[0], ssm_norm_w[0]]).reshape(1, -1)
    small_all = _all_gather(small_vec.astype(F32), "gather_conv_b").reshape(N_DEV, -1)
    conv_b = small_all[:, :SSM_CONV_DIM // N_DEV].reshape(-1)
    norm_w = small_all[:, SSM_CONV_DIM // N_DEV:].reshape(-1)
    cw8 = jnp.pad(conv_w, ((0, 8 - SSM_CONV), (0, 0)))
    cb8 = _rows8(conv_b)
    nw8 = _rows8(norm_w)
    bf8 = _rows8(jnp.pad(fox_b_f[0], (0, LANES - FOX_HEADS)))
    par8 = jnp.pad(jnp.stack([jnp.pad(p[0], (0, LANES - SSM_HEADS)) for p in (ssm_dt_bias, ssm_a_log, ssm_d)]),
                   ((0, 5), (0, 0)))

    sh_a, sc_a, g_a, sh_m, sc_m, g_m = [mod[:, k] for k in range(6)]
    u0 = _modulate(xs, pv_rows(1.0 + sc_a[0], sh_a[0]), "modulate0")
    y0, fres, gathered = _fox_fwd(u0, fw, bf8, [(False, _mx(w)) for w in (mlp_w1, mlp_w2, ssm_w_in, ssm_w_out)])
    w1 = _cols_from_slots(gathered[0])
    w2 = _rows_from_slots(gathered[1])
    sw = _ssd_weights(_cols_from_slots(gathered[2])[0], _rows_from_slots(gathered[3])[0])
    pv0 = pv_rows(1.0 + g_a[0], ln_mix_g[0], ln_mix_b[0], 1.0 + sc_m[0], sh_m[0])
    x1, u1 = _ln_fwd(xs, y0, pv0, "ln_mix0")
    y1, (h0, a0) = _mlp_fwd(u1, w1[0], w2[0], "0")
    pv1 = pv_rows(1.0 + g_m[0], ln_mlp_g[0], ln_mlp_b[0], 1.0 + sc_a[1], sh_a[1])
    x2, u2 = _ln_fwd(x1, y1, pv1, "ln_mlp0")
    y2, sres = _ssd_fwd(u2, sw, cw8, cb8, par8, nw8)
    pv2 = pv_rows(1.0 + g_a[1], ln_mix_g[1], ln_mix_b[1], 1.0 + sc_m[1], sh_m[1])
    x3, u3 = _ln_fwd(x2, y2, pv2, "ln_mix1")
    y3, (h1, a1) = _mlp_fwd(u3, w1[1], w2[1], "1")
    pv3 = pv_rows(1.0 + g_m[1], ln_mlp_g[1], ln_mlp_b[1])

    dx3, dy3, s3 = _ln_bwd(x3, y3, pv3, "ln_mlp1_bwd", target=target)
    loss = lax.psum(s3[5, 0], ("x", "y", "c"))
    du3, dw1_1, dw2_1 = _mlp_bwd(dy3, u3, h1, a1, w1[1].T, w2[1].T, "1")
    dx2, dy2, s2 = _ln_bwd(x2, y2, pv2, "ln_mix1_bwd", dxo=dx3, du=du3)
    du2, d_ssm_in, d_ssm_out, ssm_small = _ssd_bwd(dy2, u2, sw, cw8, cb8, par8, nw8, sres)
    dx1, dy1, s1 = _ln_bwd(x1, y1, pv1, "ln_mlp0_bwd", dxo=dx2, du=du2)
    du1, dw1_0, dw2_0 = _mlp_bwd(dy1, u1, h0, a0, w1[0].T, w2[0].T, "0")
    dx0, dy0, s0 = _ln_bwd(xs, y0, pv0, "ln_mix0_bwd", dxo=dx1, du=du1)
    early = [(True, _col_slots(jnp.stack([dw1_0, dw1_1]))), (True, _row_slots(jnp.stack([dw2_0, dw2_1]))),
             (True, _col_slots(d_ssm_in[None])), (True, _row_slots(d_ssm_out[None])),
             (True, _col_slots(ssm_small["conv_w"][None])), (True, _col_slots(ssm_small["conv_b"][None])),
             (True, _col_slots(ssm_small["norm_w"][None]))]
    du0, d_fox_in, d_fox_o, fox_sums, exchanged = _fox_bwd(dy0, u0, fw, bf8, fres, early)
    grad_x, sx = _mod_bwd(dx0, du0, xs, pv_rows(1.0 + sc_a[0], sh_a[0]), "modulate0_bwd")

    dmod = jnp.stack([
        jnp.stack([sx[1], sx[0], s0[4], s0[1], s0[0], s1[4]]),
        jnp.stack([s1[1], s1[0], s2[4], s2[1], s2[0], s3[4]]),
    ]).reshape(DEPTH, 6 * d)

    def pad_rows(v):
        v = v.reshape(-1, LANES) if v.size % LANES == 0 else jnp.pad(v.reshape(-1), (0, LANES - v.size)).reshape(1, LANES)
        return jnp.pad(v, ((0, (-v.shape[0]) % 8), (0, 0)))

    small_parts = [dmod, jnp.stack([s0[2], s2[2]]), jnp.stack([s0[3], s2[3]]), jnp.stack([s1[2], s3[2]]),
                   jnp.stack([s1[3], s3[3]]), fox_sums[0, :FOX_HEADS], ssm_small["dt_bias"], ssm_small["a_log"],
                   ssm_small["d"]]
    packed = [pad_rows(p) for p in small_parts]
    offsets = np.cumsum([0] + [p.shape[0] for p in packed])
    small_all_g = _all_gather(jnp.concatenate(packed, axis=0), "gather_small_grads")

    def unpack(idx, shape):
        n = int(np.prod(shape))
        blk = small_all_g[:, offsets[idx]:offsets[idx + 1]].reshape(N_DEV, -1)[:, :n]
        return blk.reshape((N_DEV,) + tuple(shape))

    dmod_all = unpack(0, (DEPTH, 6 * d))
    dmod_cols = lax.dynamic_slice_in_dim(dmod_all, me * ncol, ncol, axis=2)
    dmod_cols = jnp.pad(jnp.moveaxis(dmod_cols, 0, 1), ((0, 0), (0, N_DEV), (0, 0)))
    g_ada_w = _ada_bwd(c_all, dmod_cols)

    shards = dict(
        mlp_w1=exchanged[0], mlp_w2=exchanged[1], ssm_w_in=exchanged[2], ssm_w_out=exchanged[3],
        ssm_conv_w=exchanged[4], ssm_conv_b=exchanged[5], ssm_norm_w=exchanged[6],
        fox_w_in=_scatter_cols(d_fox_in[None], "scatter_fox_in"), fox_w_o=_scatter_rows(d_fox_o[None], "scatter_fox_o"),
        ada_w=g_ada_w[None], ada_b=dmod_all,
        ln_mix_g=unpack(1, (DEPTH, d)), ln_mix_b=unpack(2, (DEPTH, d)),
        ln_mlp_g=unpack(3, (DEPTH, d)), ln_mlp_b=unpack(4, (DEPTH, d)),
        fox_b_f=unpack(5, (1, FOX_HEADS)), ssm_dt_bias=unpack(6, (1, SSM_HEADS)),
        ssm_a_log=unpack(7, (1, SSM_HEADS)), ssm_d=unpack(8, (1, SSM_HEADS)),
    )
    weights = dict(ada_w=ada_w, ada_b=ada_b, ln_mix_g=ln_mix_g, ln_mix_b=ln_mix_b, ln_mlp_g=ln_mlp_g, ln_mlp_b=ln_mlp_b,
                   mlp_w1=mlp_w1, mlp_w2=mlp_w2, fox_w_in=fox_w_in, fox_b_f=fox_b_f, fox_w_o=fox_w_o, ssm_w_in=ssm_w_in,
                   ssm_conv_w=ssm_conv_w, ssm_conv_b=ssm_conv_b, ssm_dt_bias=ssm_dt_bias, ssm_a_log=ssm_a_log,
                   ssm_d=ssm_d, ssm_norm_w=ssm_norm_w, ssm_w_out=ssm_w_out)
    mom1 = dict(ada_w=m_ada_w, ada_b=m_ada_b, ln_mix_g=m_ln_mix_g, ln_mix_b=m_ln_mix_b, ln_mlp_g=m_ln_mlp_g,
                ln_mlp_b=m_ln_mlp_b, mlp_w1=m_mlp_w1, mlp_w2=m_mlp_w2, fox_w_in=m_fox_w_in, fox_b_f=m_fox_b_f,
                fox_w_o=m_fox_w_o, ssm_w_in=m_ssm_w_in, ssm_conv_w=m_ssm_conv_w, ssm_conv_b=m_ssm_conv_b,
                ssm_dt_bias=m_ssm_dt_bias, ssm_a_log=m_ssm_a_log, ssm_d=m_ssm_d, ssm_norm_w=m_ssm_norm_w,
                ssm_w_out=m_ssm_w_out)
    mom2 = dict(ada_w=v_ada_w, ada_b=v_ada_b, ln_mix_g=v_ln_mix_g, ln_mix_b=v_ln_mix_b, ln_mlp_g=v_ln_mlp_g,
                ln_mlp_b=v_ln_mlp_b, mlp_w1=v_mlp_w1, mlp_w2=v_mlp_w2, fox_w_in=v_fox_w_in, fox_b_f=v_fox_b_f,
                fox_w_o=v_fox_w_o, ssm_w_in=v_ssm_w_in, ssm_conv_w=v_ssm_conv_w, ssm_conv_b=v_ssm_conv_b,
                ssm_dt_bias=v_ssm_dt_bias, ssm_a_log=v_ssm_a_log, ssm_d=v_ssm_d, ssm_norm_w=v_ssm_norm_w,
                ssm_w_out=v_ssm_w_out)
    names = list(weights)
    stepped = {n: _adamw_any(shards[n], weights[n], mom1[n], mom2[n], f"adamw_{n}") for n in names}
    return (loss, grad_x[None], *[stepped[n][0] for n in names], *[stepped[n][1] for n in names],
            *[stepped[n][2] for n in names], *[stepped[n][3] for n in names])
```

```python
import numpy as np
import jax
import jax.numpy as jnp
from jax import lax
from jax.experimental import pallas as pl
from jax.experimental.pallas import tpu as pltpu

F32 = jnp.float32
MXU_DTYPE = jnp.bfloat16
HIGHEST = lax.Precision.HIGHEST

N_DEV = 8
D_MODEL = 1024
DEPTH = 2
FOX_HEADS = 16
FOX_HEAD_DIM = 64
D_FF = 4096
SSM_D_INNER = 2048
SSM_HEADS = 32
SSM_GROUPS = 8
SSM_STATE = 128
SSM_CHUNK = 128
SSM_CONV = 4
SSM_CONV_DIM = 4096
GROUP_W = SSM_D_INNER // SSM_GROUPS
LN_EPS = 1e-5
RMS_EPS = 1e-5
ALPHA = (2.0 * DEPTH) ** 0.25
LANES = 128
SUBLANES = 8

ADAM_LR = 0.001
ADAM_B1 = 0.9
ADAM_B2 = 0.999
ADAM_EPS = 1e-08
ADAM_WD = 0.01
ADAM_STEP = 10

NN = (((1,), (0,)), ((), ()))
NT = (((1,), (1,)), ((), ()))
TN = (((0,), (0,)), ((), ()))

VMEM_BIG = 56 * 1024 * 1024


def _dot(a, b, dims=NN, precision=None):
    return lax.dot_general(a, b, dims, precision=precision, preferred_element_type=F32)


def _mx(v):
    return v.astype(MXU_DTYPE)


def _pieces3(v):
    hi = _mx(v)
    r1 = v - hi.astype(F32)
    mid = _mx(r1)
    return hi, mid, _mx(r1 - mid.astype(F32))


def _dot_onehot(a, b, dims=NN, onehot="b"):
    if onehot == "b":
        return sum(_dot(p, _mx(b), dims) for p in _pieces3(a))
    return sum(_dot(_mx(a), p, dims) for p in _pieces3(b))


def _params(vmem=None):
    return pltpu.CompilerParams(vmem_limit_bytes=vmem) if vmem else None


def _all_gather(x, name):
    def body(x_ref, out_ref, send_sems, recv_sems, local_sem):
        xi, yi, ci = lax.axis_index("x"), lax.axis_index("y"), lax.axis_index("c")
        me, sibling = (xi, yi, ci), (xi, yi, 1 - ci)
        chips = [(1 - xi, yi), (xi, 1 - yi), (1 - xi, 1 - yi)]

        def slot(px, py, pc):
            return out_ref.at[4 * px + 2 * py + pc]

        def copy(k, block, to, src=None):
            return pltpu.make_async_remote_copy(
                src_ref=slot(*block) if src is None else src, dst_ref=slot(*block),
                send_sem=send_sems.at[k], recv_sem=recv_sems.at[k],
                device_id=to, device_id_type=pl.DeviceIdType.MESH)

        mine = pltpu.make_async_copy(x_ref, slot(*me), local_sem)
        mine.start()
        first = [copy(0, me, sibling, src=x_ref)]
        first += [copy(1 + j, me, (*chip, ci), src=x_ref) for j, chip in enumerate(chips)]
        for cp in first:
            cp.start()
        passed = [copy(4 + j, (*chip, ci), sibling) for j, chip in enumerate(chips)]
        for j, chip in enumerate(chips):
            copy(1 + j, (*chip, ci), me).wait_recv()
            passed[j].start()
        copy(0, sibling, me).wait_recv()
        for j, chip in enumerate(chips):
            copy(4 + j, (*chip, 1 - ci), me).wait_recv()
        for cp in first + passed:
            cp.wait_send()
        mine.wait()

    return pl.pallas_call(
        body, name=name,
        out_shape=jax.ShapeDtypeStruct((N_DEV,) + x.shape, x.dtype),
        in_specs=[pl.BlockSpec(memory_space=pl.ANY)],
        out_specs=pl.BlockSpec(memory_space=pl.ANY),
        scratch_shapes=[pltpu.SemaphoreType.DMA((7,)), pltpu.SemaphoreType.DMA((7,)),
                        pltpu.SemaphoreType.DMA],
    )(x)


EXCHANGE_PIECES = 1


def _direct_copies(scatter, x_ref, out_ref, send_sems, recv_sems, local_sems, n, piece=None):
    xi, yi, ci = lax.axis_index("x"), lax.axis_index("y"), lax.axis_index("c")
    me = 4 * xi + 2 * yi + ci

    def part(ref):
        return ref if piece is None else ref.at[piece]

    local = pltpu.make_async_copy(part(x_ref.at[me] if scatter else x_ref), part(out_ref.at[me]), local_sems.at[n])
    remote = []
    for k in range(1, N_DEV):
        px = 1 - xi if k & 4 else xi
        py = 1 - yi if k & 2 else yi
        pc = 1 - ci if k & 1 else ci
        remote.append(pltpu.make_async_remote_copy(
            src_ref=part(x_ref.at[4 * px + 2 * py + pc] if scatter else x_ref), dst_ref=part(out_ref.at[me]),
            send_sem=send_sems.at[7 * n + k - 1], recv_sem=recv_sems.at[7 * n + k - 1],
            device_id=(px, py, pc), device_id_type=pl.DeviceIdType.MESH))
    return local, remote


def _in_pieces(carry, steps):
    out = []
    for scatter, a in carry:
        body = a.shape[1:] if scatter else a.shape
        rows = int(np.prod(body[:-1])) if len(body) > 1 else 1
        align = SUBLANES * (4 // a.dtype.itemsize)
        pieces = 1
        while (pieces * 2 <= min(EXCHANGE_PIECES, steps // 2) and rows % (pieces * 2 * align) == 0):
            pieces *= 2
        shape = (pieces, rows // pieces, body[-1])
        out.append((scatter, a.reshape(((N_DEV,) if scatter else ()) + shape), pieces, max(1, (steps // 2) // pieces)))
    return out


def _exchange_shapes(carry):
    return [jax.ShapeDtypeStruct(c[1].shape if c[0] else (N_DEV,) + c[1].shape, c[1].dtype) for c in carry]


def _exchange_sems(carry):
    n = max(len(carry), 1)
    return [pltpu.SemaphoreType.DMA((7 * n,)), pltpu.SemaphoreType.DMA((7 * n,)), pltpu.SemaphoreType.DMA((n,))]


def _exchange_start(carry, srcs, dsts, sems, step=None):
    for n, entry in enumerate(carry):
        scatter = entry[0]
        pieces, stride = (entry[2], entry[3]) if len(entry) > 2 else (1, 1)
        local, remote = _direct_copies(scatter, srcs[n], dsts[n], *sems, n)
        if step is None:
            local.start()
            for cp in remote:
                cp.start()
            continue

        @pl.when(step == 0)
        def _():
            local.start()
            if pieces == 1:
                for cp in remote:
                    cp.start()

        if pieces > 1:
            @pl.when((step % stride == 0) & (step // stride < pieces))
            def _():
                for cp in _direct_copies(scatter, srcs[n], dsts[n], *sems, n, piece=step // stride)[1]:
                    cp.start()


def _exchange_wait(carry, srcs, dsts, sems):
    for n, entry in enumerate(carry):
        local, remote = _direct_copies(entry[0], srcs[n], dsts[n], *sems, n)
        for cp in remote:
            cp.wait()
        local.wait()


def _all_to_all(x, name):
    carry = [(True, x)]

    def body(x_ref, out_ref, *sems):
        _exchange_start(carry, [x_ref], [out_ref], sems)
        _exchange_wait(carry, [x_ref], [out_ref], sems)

    return pl.pallas_call(
        body, name=name,
        out_shape=jax.ShapeDtypeStruct(x.shape, x.dtype),
        in_specs=[pl.BlockSpec(memory_space=pl.ANY)],
        out_specs=pl.BlockSpec(memory_space=pl.ANY),
        scratch_shapes=_exchange_sems(carry),
    )(x)


def _mm(name, a, b, *, grid, a_spec, b_spec, dims, k_axis, outs, acc=None, extras=(), epi=None, vmem=None):
    nk = grid[k_axis]
    n_ex, n_out = len(extras), len(outs)

    def body(*refs):
        a_ref, b_ref = refs[0], refs[1]
        ex = refs[2:2 + n_ex]
        out = refs[2 + n_ex:2 + n_ex + n_out]

        def finish(val):
            if epi is None:
                out[0][...] = val.astype(out[0].dtype)
            else:
                epi(val, ex, out)

        part = _dot(a_ref[...], b_ref[...], dims)
        if nk == 1:
            finish(part)
        else:
            acc_ref = refs[2 + n_ex + n_out]
            k = pl.program_id(k_axis)

            @pl.when(k == 0)
            def _():
                acc_ref[...] = part

            @pl.when(k > 0)
            def _():
                acc_ref[...] += part

            @pl.when(k == nk - 1)
            def _():
                finish(acc_ref[...])

    return pl.pallas_call(
        body, name=name, grid=grid,
        in_specs=[a_spec, b_spec] + [s for _, s in extras],
        out_specs=[s for _, s in outs],
        out_shape=[o for o, _ in outs],
        scratch_shapes=[pltpu.VMEM(acc, F32)] if nk > 1 else [],
        compiler_params=_params(vmem),
    )(a, b, *[e for e, _ in extras])


def _tile(n, t):
    t = min(n, t)
    assert n % t == 0, (n, t)
    return t


def _mm_nn(name, a, b, out_dtype, *, addend=None, tm=1024, tn=1024, tk=1024, epi=None, extras=(), outs=None):
    m, kk = a.shape
    n = b.shape[1]
    tm, tn, tk = _tile(m, tm), _tile(n, tn), _tile(kk, tk)
    o_spec = pl.BlockSpec((tm, tn), lambda i, j, k: (i, j))
    if outs is None:
        outs = [(jax.ShapeDtypeStruct((m, n), out_dtype), o_spec)]
    extras = list(extras)
    if addend is not None:
        extras = [(addend, o_spec)] + extras

        def epi(val, ex, out):
            out[0][...] = (val + ex[0][...].astype(F32)).astype(out[0].dtype)

    res = _mm(name, a, b, grid=(m // tm, n // tn, kk // tk),
              a_spec=pl.BlockSpec((tm, tk), lambda i, j, k: (i, k)),
              b_spec=pl.BlockSpec((tk, tn), lambda i, j, k: (k, j)),
              dims=NN, k_axis=2, acc=(tm, tn), outs=outs, extras=extras, epi=epi, vmem=VMEM_BIG)
    return res[0] if len(res) == 1 else res


def _mm_tn(name, a, b, out_dtype=F32, *, tm=1024, tn=1024, tk=1024):
    kk, m = a.shape
    n = b.shape[1]
    tm, tn, tk = _tile(m, tm), _tile(n, tn), _tile(kk, tk)
    res = _mm(name, a, b, grid=(m // tm, n // tn, kk // tk),
              a_spec=pl.BlockSpec((tk, tm), lambda i, j, k: (k, i)),
              b_spec=pl.BlockSpec((tk, tn), lambda i, j, k: (k, j)),
              dims=TN, k_axis=2, acc=(tm, tn),
              outs=[(jax.ShapeDtypeStruct((m, n), out_dtype), pl.BlockSpec((tm, tn), lambda i, j, k: (i, j)))],
              vmem=VMEM_BIG)
    return res[0]


def _row_block(s):
    return _tile(s, 512)


def _modulate(x, pv, name):
    s, d = x.shape
    tb = _row_block(s)

    def body(x_ref, pv_ref, u_ref):
        u_ref[...] = _mx(x_ref[...] * pv_ref[0:1, :] + pv_ref[1:2, :])

    return pl.pallas_call(
        body, name=name, grid=(s // tb,),
        in_specs=[pl.BlockSpec((tb, d), lambda i: (i, 0)), pl.BlockSpec((8, d), lambda i: (0, 0))],
        out_specs=pl.BlockSpec((tb, d), lambda i: (i, 0)),
        out_shape=jax.ShapeDtypeStruct((s, d), MXU_DTYPE),
    )(x, pv)


def _ln_stats(r):
    mu = jnp.mean(r, axis=-1, keepdims=True)
    xc = r - mu
    var = jnp.mean(xc * xc, axis=-1, keepdims=True)
    rstd = lax.rsqrt(var + LN_EPS)
    return xc * rstd, rstd


def _ln_fwd(xin, y, pv, name):
    s, d = xin.shape
    tb = _row_block(s)

    def body(x_ref, y_ref, pv_ref, xo_ref, u_ref):
        r = ALPHA * x_ref[...] + pv_ref[0:1, :] * y_ref[...]
        xhat, _ = _ln_stats(r)
        xo = xhat * pv_ref[1:2, :] + pv_ref[2:3, :]
        xo_ref[...] = xo
        u_ref[...] = _mx(xo * pv_ref[3:4, :] + pv_ref[4:5, :])

    row = pl.BlockSpec((tb, d), lambda i: (i, 0))
    return pl.pallas_call(
        body, name=name, grid=(s // tb,),
        in_specs=[row, row, pl.BlockSpec((8, d), lambda i: (0, 0))],
        out_specs=[row, row],
        out_shape=[jax.ShapeDtypeStruct((s, d), F32), jax.ShapeDtypeStruct((s, d), MXU_DTYPE)],
    )(xin, y, pv)


def _ln_bwd(xin, y, pv, name, *, dxo=None, du=None, target=None):
    s, d = xin.shape
    tb = _row_block(s)
    nb = s // tb
    loss_mode = target is not None

    def body(*refs):
        if loss_mode:
            x_ref, y_ref, pv_ref, t_ref, dxin_ref, dy_ref, sums_ref = refs
        else:
            x_ref, y_ref, pv_ref, dxo_ref, du_ref, dxin_ref, dy_ref, sums_ref = refs
        i = pl.program_id(0)

        @pl.when(i == 0)
        def _():
            sums_ref[...] = jnp.zeros_like(sums_ref)

        yv = y_ref[...]
        r = ALPHA * x_ref[...] + pv_ref[0:1, :] * yv
        xhat, rstd = _ln_stats(r)
        xo = xhat * pv_ref[1:2, :] + pv_ref[2:3, :]
        if loss_mode:
            diff = xo - t_ref[...]
            dxo_v = diff * (1.0 / d)
            sums_ref[5:6, :] += jnp.sum(diff * diff, axis=0, keepdims=True) * (0.5 / d)
        else:
            duv = du_ref[...]
            dxo_v = dxo_ref[...] + duv * pv_ref[3:4, :]
            sums_ref[0:1, :] += jnp.sum(duv * xo, axis=0, keepdims=True)
            sums_ref[1:2, :] += jnp.sum(duv, axis=0, keepdims=True)
        sums_ref[2:3, :] += jnp.sum(dxo_v * xhat, axis=0, keepdims=True)
        sums_ref[3:4, :] += jnp.sum(dxo_v, axis=0, keepdims=True)
        dxh = dxo_v * pv_ref[1:2, :]
        dr = rstd * (dxh - jnp.mean(dxh, axis=-1, keepdims=True)
                     - xhat * jnp.mean(dxh * xhat, axis=-1, keepdims=True))
        sums_ref[4:5, :] += jnp.sum(dr * yv, axis=0, keepdims=True)
        dxin_ref[...] = ALPHA * dr
        dy_ref[...] = _mx(pv_ref[0:1, :] * dr)
        if loss_mode:
            @pl.when(i == nb - 1)
            def _():
                sums_ref[5:6, :] = jnp.broadcast_to(jnp.sum(sums_ref[5:6, :], axis=-1, keepdims=True), (1, d))

    row = pl.BlockSpec((tb, d), lambda i: (i, 0))
    par = pl.BlockSpec((8, d), lambda i: (0, 0))
    ins = [xin, y, pv] + ([target] if loss_mode else [dxo, du])
    return pl.pallas_call(
        body, name=name, grid=(nb,),
        in_specs=[row, row, par] + [row] * (len(ins) - 3),
        out_specs=[row, row, par],
        out_shape=[jax.ShapeDtypeStruct((s, d), F32), jax.ShapeDtypeStruct((s, d), MXU_DTYPE),
                   jax.ShapeDtypeStruct((8, d), F32)],
    )(*ins)


def _mod_bwd(dx_direct, du, x, pv, name):
    s, d = x.shape
    tb = _row_block(s)

    def body(dxd_ref, du_ref, x_ref, pv_ref, dx_ref, sums_ref):
        @pl.when(pl.program_id(0) == 0)
        def _():
            sums_ref[...] = jnp.zeros_like(sums_ref)

        duv = du_ref[...]
        dx_ref[...] = dxd_ref[...] + duv * pv_ref[0:1, :]
        sums_ref[0:1, :] += jnp.sum(duv * x_ref[...], axis=0, keepdims=True)
        sums_ref[1:2, :] += jnp.sum(duv, axis=0, keepdims=True)

    row = pl.BlockSpec((tb, d), lambda i: (i, 0))
    par = pl.BlockSpec((8, d), lambda i: (0, 0))
    return pl.pallas_call(
        body, name=name, grid=(s // tb,),
        in_specs=[row, row, row, par], out_specs=[row, par],
        out_shape=[jax.ShapeDtypeStruct((s, d), F32), jax.ShapeDtypeStruct((8, d), F32)],
    )(dx_direct, du, x, pv)


def _mlp_fwd(u, w1, w2, tag):
    s = u.shape[0]

    def epi(val, ex, out):
        out[0][...] = _mx(val)
        out[1][...] = _mx(jnp.square(jnp.maximum(val, 0.0)))

    tm, tn = _tile(s, 1024), 1024
    spec = pl.BlockSpec((tm, tn), lambda i, j, k: (i, j))
    shp = jax.ShapeDtypeStruct((s, D_FF), MXU_DTYPE)
    h, a = _mm_nn(f"mlp_up{tag}", u, w1, None, epi=epi, outs=[(shp, spec), (shp, spec)], tn=tn)
    y = _mm_nn(f"mlp_down{tag}", a, w2, F32)
    return y, (h, a)


def _mlp_bwd(dy, u, h, a, w1t, w2t, tag):
    s = u.shape[0]
    tm, tn = _tile(s, 1024), 1024
    spec = pl.BlockSpec((tm, tn), lambda i, j, k: (i, j))

    def epi(val, ex, out):
        out[0][...] = _mx(val * (2.0 * jnp.maximum(ex[0][...].astype(F32), 0.0)))

    dh = _mm_nn(f"mlp_dh{tag}", dy, w2t, None, epi=epi, extras=[(h, spec)],
                outs=[(jax.ShapeDtypeStruct((s, D_FF), MXU_DTYPE), spec)], tn=tn)
    du = _mm_nn(f"mlp_du{tag}", dh, w1t, F32)
    dw2 = _mm_tn(f"mlp_dw2{tag}", a, dy)
    dw1 = _mm_tn(f"mlp_dw1{tag}", u, dh)
    return du, dw1, dw2


FOX_T = 1024
BIAS_Q = (64, 65, 66)
BIAS_K = (67, 68, 69)
SKIP_MARGIN = 110.0
ONES_V = 64

def _fox_constants():
    selq = np.zeros((FOX_HEADS, 512, LANES), np.float32)
    selk = np.zeros((FOX_HEADS, 512, LANES), np.float32)
    selv = np.zeros((2, LANES, LANES), np.float32)
    put = np.zeros((2, 2, LANES, LANES), np.float32)
    for h in range(FOX_HEADS):
        off = FOX_HEAD_DIM * (h % 2)
        for dd in range(FOX_HEAD_DIM):
            selq[h, off + dd, dd] = FOX_HEAD_DIM ** -0.5
            selk[h, off + dd, dd] = 1.0
        for piece in range(3):
            selq[h, LANES * (1 + piece) + h, BIAS_Q[piece]] = 1.0
            selk[h, LANES * (1 + piece) + h, BIAS_K[piece]] = -1.0
    for par in range(2):
        for dd in range(FOX_HEAD_DIM):
            selv[par, FOX_HEAD_DIM * par + dd, dd] = 1.0
            put[par, 0, dd, FOX_HEAD_DIM * par + dd] = FOX_HEAD_DIM ** -0.5
            put[par, 1, dd, FOX_HEAD_DIM * par + dd] = 1.0
    return selq, selk, selv, put


def _pairs(nb, by_key):
    if by_key:
        pr = [(i, j) for j in range(nb) for i in range(j, nb)]
    else:
        pr = [(i, j) for i in range(nb) for j in range(i + 1)]
    return (np.array([p[0] for p in pr], np.int32), np.array([p[1] for p in pr], np.int32))


def _fox_prep(qkv, f, bf):
    s = qkv.shape[0]
    t = _tile(s, FOX_T)
    nb = s // t
    selq, selk, selv, _ = _fox_constants()

    def body(q_ref, k_ref, v_ref, f_ref, bf_ref, selq_ref, selk_ref, selv_ref,
             qa_ref, qat_ref, ka_ref, kat_ref, va_ref, vat_ref, stats_ref, parts_ref, carry_ref, cum_ref):
        i, h = pl.program_id(0), pl.program_id(1)
        lane = lax.broadcasted_iota(jnp.int32, (1, LANES), 1)

        @pl.when(h == 0)
        def _():
            @pl.when(i == 0)
            def _():
                carry_ref[...] = jnp.zeros_like(carry_ref)

            lf = jnp.where(lane < FOX_HEADS, jax.nn.log_sigmoid(f_ref[...] + bf_ref[0:1, :]), 0.0)
            tri = (lax.broadcasted_iota(jnp.int32, (t, t), 0) >= lax.broadcasted_iota(jnp.int32, (t, t), 1)).astype(F32)
            cum = _dot_onehot(tri, lf, onehot="a") + carry_ref[0:1, :]
            carry_ref[0:1, :] = cum[t - 1:t, :]
            cum_ref[...] = cum
            hi = _mx(cum)
            r1 = cum - hi.astype(F32)
            mid = _mx(r1)
            parts_ref[:, 0:LANES] = hi
            parts_ref[:, LANES:2 * LANES] = mid
            parts_ref[:, 2 * LANES:3 * LANES] = _mx(r1 - mid.astype(F32))

        parts = parts_ref[...]
        qa = _dot(jnp.concatenate([q_ref[...], parts], axis=1), selq_ref[...])
        qa = qa + jnp.where((lane >= BIAS_K[0]) & (lane <= BIAS_K[2]), 1.0, 0.0)
        ka = _dot(jnp.concatenate([k_ref[...], parts], axis=1), selk_ref[...])
        ka = ka + jnp.where((lane >= BIAS_Q[0]) & (lane <= BIAS_Q[2]), 1.0, 0.0)
        va = _dot(v_ref[...], selv_ref[...]) + jnp.where(lane == ONES_V, 1.0, 0.0)
        qa_ref[...] = _mx(qa)
        qat_ref[...] = _mx(qa.T)
        ka_ref[...] = _mx(ka)
        kat_ref[...] = _mx(ka.T)
        va_ref[...] = _mx(va)
        vat_ref[...] = _mx(va.T)

        def longest(rows_):
            sq = jnp.where(lane < FOX_HEAD_DIM, rows_ * rows_, 0.0)
            return jnp.sqrt(jnp.max(jnp.sum(sq, axis=1, keepdims=True), axis=0, keepdims=True))

        mine = lane == h
        cum = cum_ref[...]
        top = jnp.max(jnp.max(jnp.where(mine, cum, -jnp.inf), axis=1, keepdims=True), axis=0, keepdims=True)
        low = jnp.min(jnp.min(jnp.where(mine, cum, jnp.inf), axis=1, keepdims=True), axis=0, keepdims=True)
        row = lax.broadcasted_iota(jnp.int32, (8, LANES), 0)
        stats_ref[...] = jnp.where(row == 0, longest(qa), jnp.where(row == 1, longest(ka),
                                                                    jnp.where(row == 2, top, low)))

    rows = jax.ShapeDtypeStruct((FOX_HEADS, nb, t, LANES), MXU_DTYPE)
    cols = jax.ShapeDtypeStruct((FOX_HEADS, nb, LANES, t), MXU_DTYPE)
    rspec = pl.BlockSpec((None, None, t, LANES), lambda i, h: (h, i, 0, 0))
    cspec = pl.BlockSpec((None, None, LANES, t), lambda i, h: (h, i, 0, 0))
    npair = FOX_HEADS // 2
    return pl.pallas_call(
        body, name="fox_prep", grid=(nb, FOX_HEADS),
        in_specs=[pl.BlockSpec((t, LANES), lambda i, h: (i, h // 2)),
                  pl.BlockSpec((t, LANES), lambda i, h: (i, npair + h // 2)),
                  pl.BlockSpec((t, LANES), lambda i, h: (i, 2 * npair + h // 2)),
                  pl.BlockSpec((t, LANES), lambda i, h: (i, 0)),
                  pl.BlockSpec((8, LANES), lambda i, h: (0, 0)),
                  pl.BlockSpec((None, 512, LANES), lambda i, h: (h, 0, 0)),
                  pl.BlockSpec((None, 512, LANES), lambda i, h: (h, 0, 0)),
                  pl.BlockSpec((None, LANES, LANES), lambda i, h: (h % 2, 0, 0))],
        out_specs=[rspec, cspec, rspec, cspec, rspec, cspec,
                   pl.BlockSpec((None, None, 8, LANES), lambda i, h: (h, i, 0, 0))],
        out_shape=[rows, cols, rows, cols, rows, cols, jax.ShapeDtypeStruct((FOX_HEADS, nb, 8, LANES), F32)],
        scratch_shapes=[pltpu.VMEM((t, 3 * LANES), MXU_DTYPE), pltpu.VMEM((8, LANES), F32),
                        pltpu.VMEM((t, LANES), F32)],
        compiler_params=_params(VMEM_BIG),
    )(qkv, qkv, qkv, f, bf, _mx(jnp.asarray(selq)), _mx(jnp.asarray(selk)), _mx(jnp.asarray(selv)))


def _fox_active(stats, by_key):
    qn, kn, top, low = (stats[:, :, r, 0] for r in range(4))
    im, jm = _pairs(qn.shape[1], by_key)
    cols = []
    for i, j in zip(im.tolist(), jm.tolist()):
        gap = qn[:, i] * kn[:, j] + top[:, i] - low[:, j] + qn[:, i] * kn[:, i]
        cols.append(jnp.where((gap > -SKIP_MARGIN) | (i == j), 1.0, 0.0))
    return jnp.stack(cols, axis=1).astype(F32)


def _causal_allow(t):
    return lax.broadcasted_iota(jnp.int32, (t, t), 0) <= lax.broadcasted_iota(jnp.int32, (t, t), 1)


def _fetch_ahead(im_ref, jm_ref, act_ref, h, p, heads, npairs, copies_of):
    step = h * npairs + p
    slot = step % 2
    i, j = im_ref[p], jm_ref[p]
    runs = (i == j) | (act_ref[h, p] > 0.5)
    wrap = p == npairs - 1
    last = wrap & (h == heads - 1)
    p_next = jnp.where(wrap, 0, p + 1)
    h_next = jnp.where(wrap & jnp.logical_not(last), h + 1, h)
    i_next, j_next = im_ref[p_next], jm_ref[p_next]
    runs_next = jnp.logical_not(last) & ((i_next == j_next) | (act_ref[h_next, p_next] > 0.5))

    @pl.when(step == 0)
    def _():
        for cp in copies_of(h, i, j, slot):
            cp.start()

    @pl.when(runs_next)
    def _():
        for cp in copies_of(h_next, i_next, j_next, 1 - slot):
            cp.start()

    @pl.when(runs)
    def _():
        for cp in copies_of(h, i, j, slot):
            cp.wait()

    return runs, slot


def _fox_attn_fwd(qat, ka, vat, active, carry=()):
    heads, nb, _, t = qat.shape
    im, jm = _pairs(nb, by_key=False)
    npairs = len(im)
    whole = _exchange_shapes(list(carry))
    carry = _in_pieces(list(carry), heads * npairs)
    nc = len(carry)

    def body(im_ref, jm_ref, act_ref, qat_ref, ka_ref, vat_ref, *rest):
        srcs, (ot_ref, lse_ref), dsts = rest[:nc], rest[nc:nc + 2], rest[nc + 2:2 * nc + 2]
        acc_ref, m_ref, kbuf_ref, vbuf_ref, fsems = rest[2 * nc + 2:2 * nc + 7]
        sems = rest[2 * nc + 7:]
        h, p = pl.program_id(0), pl.program_id(1)
        i, j = im_ref[p], jm_ref[p]

        if nc:
            _exchange_start(carry, srcs, dsts, sems, step=h * npairs + p)

        def key_blocks(hh, ii, jj, slot):
            return [pltpu.make_async_copy(ka_ref.at[hh, jj], kbuf_ref.at[slot], fsems.at[0, slot]),
                    pltpu.make_async_copy(vat_ref.at[hh, jj], vbuf_ref.at[slot], fsems.at[1, slot])]

        runs, slot = _fetch_ahead(im_ref, jm_ref, act_ref, h, p, heads, npairs, key_blocks)

        @pl.when(j == 0)
        def _():
            m_ref[...] = jnp.full_like(m_ref, -jnp.inf)
            acc_ref[...] = jnp.zeros_like(acc_ref)

        def step(diagonal):
            st = _dot(kbuf_ref[slot], qat_ref[...])
            if diagonal:
                st = jnp.where(_causal_allow(t), st, -jnp.inf)
            m_old = m_ref[...]
            m_new = jnp.maximum(m_old, jnp.max(st, axis=0, keepdims=True))
            pt = jnp.exp(st - m_new)
            acc_ref[...] = acc_ref[...] * jnp.exp(m_old - m_new) + _dot(vbuf_ref[slot], _mx(pt))
            m_ref[...] = m_new

        @pl.when((j < i) & runs)
        def _():
            step(False)

        @pl.when(j == i)
        def _():
            step(True)
            acc = acc_ref[...]
            denom = acc[ONES_V:ONES_V + 1, :]
            ot_ref[...] = _mx(acc / denom)
            lse_ref[...] = m_ref[...] + jnp.log(denom)

        if nc:
            @pl.when((h == heads - 1) & (p == npairs - 1))
            def _():
                _exchange_wait(carry, srcs, dsts, sems)

    anywhere = pl.BlockSpec(memory_space=pl.ANY)
    qspec = pl.BlockSpec((None, None, LANES, t), lambda h, p, im, jm: (h, im[p], 0, 0))
    grid_spec = pltpu.PrefetchScalarGridSpec(
        num_scalar_prefetch=2, grid=(heads, npairs),
        in_specs=[pl.BlockSpec(memory_space=pltpu.SMEM), qspec, anywhere, anywhere] + [anywhere] * nc,
        out_specs=[qspec, pl.BlockSpec((None, None, 1, t), lambda h, p, im, jm: (h, im[p], 0, 0))] + [anywhere] * nc,
        scratch_shapes=[pltpu.VMEM((LANES, t), F32), pltpu.VMEM((1, t), F32), pltpu.VMEM((2, t, LANES), MXU_DTYPE),
                        pltpu.VMEM((2, LANES, t), MXU_DTYPE), pltpu.SemaphoreType.DMA((2, 2))]
        + (_exchange_sems(carry) if nc else []))
    res = pl.pallas_call(
        body, name="fox_attn_fwd", grid_spec=grid_spec,
        out_shape=[jax.ShapeDtypeStruct((heads, nb, LANES, t), MXU_DTYPE),
                   jax.ShapeDtypeStruct((heads, nb, 1, t), F32)] + _exchange_shapes(carry),
        compiler_params=_params(VMEM_BIG),
    )(jnp.asarray(im), jnp.asarray(jm), active, qat, ka, vat, *[c[1] for c in carry])
    return res[0], res[1], [r.reshape(w.shape) for r, w in zip(res[2:], whole)]


def _fox_attn_bwd(qa, qat, ka, kat, va, ot, lse, do, dot_, active, carry=()):
    heads, nb, t, _ = qa.shape
    im, jm = _pairs(nb, by_key=True)
    npairs = len(im)
    whole = _exchange_shapes(list(carry))
    carry = _in_pieces(list(carry), heads * npairs)
    nc = len(carry)

    def body(im_ref, jm_ref, act_ref, qa_ref, qat_ref, ka_ref, kat_ref, va_ref, ot_ref, lse_ref, do_ref, dot_ref,
             *rest):
        srcs, (dqt_ref, dka_ref, dva_ref), dsts = rest[:nc], rest[nc:nc + 3], rest[nc + 3:2 * nc + 3]
        rows_ref, cols_ref, lseb_ref, fsems = rest[2 * nc + 3:2 * nc + 7]
        sems = rest[2 * nc + 7:]
        h, p = pl.program_id(0), pl.program_id(1)
        i, j = im_ref[p], jm_ref[p]

        if nc:
            _exchange_start(carry, srcs, dsts, sems, step=h * npairs + p)

        def query_blocks(hh, ii, jj, slot):
            cps = [pltpu.make_async_copy(src.at[hh, ii], rows_ref.at[slot, n], fsems.at[n, slot])
                   for n, src in enumerate((qa_ref, do_ref))]
            cps += [pltpu.make_async_copy(src.at[hh, ii], cols_ref.at[slot, n], fsems.at[2 + n, slot])
                    for n, src in enumerate((qat_ref, ot_ref, dot_ref))]
            return cps + [pltpu.make_async_copy(lse_ref.at[hh, ii], lseb_ref.at[slot], fsems.at[5, slot])]

        runs, slot = _fetch_ahead(im_ref, jm_ref, act_ref, h, p, heads, npairs, query_blocks)

        @pl.when(p == 0)
        def _():
            dqt_ref[...] = jnp.zeros_like(dqt_ref)

        def step(diagonal):
            st = _dot(ka_ref[...], cols_ref[slot, 0])
            dot_v = cols_ref[slot, 2]
            delta = jnp.sum(cols_ref[slot, 1].astype(F32) * dot_v.astype(F32), axis=0, keepdims=True)
            pt = jnp.exp(st - lseb_ref[slot])
            if diagonal:
                pt = jnp.where(_causal_allow(t), pt, 0.0)
            dsm = _mx(pt * (_dot(va_ref[...], dot_v) - delta))
            upd_v = _dot(_mx(pt), rows_ref[slot, 1])
            upd_k = _dot(dsm, rows_ref[slot, 0])
            if diagonal:
                dva_ref[...] = upd_v
                dka_ref[...] = upd_k
            else:
                dva_ref[...] += upd_v
                dka_ref[...] += upd_k
            dqt_ref[i] += _dot(kat_ref[...], dsm)

        @pl.when(i == j)
        def _():
            step(True)

        @pl.when((i > j) & runs)
        def _():
            step(False)

        if nc:
            @pl.when((h == heads - 1) & (p == npairs - 1))
            def _():
                _exchange_wait(carry, srcs, dsts, sems)

    def at_q(shape):
        return pl.BlockSpec((None, None) + shape, lambda h, p, im, jm: (h, im[p], 0, 0))

    def at_k(shape):
        return pl.BlockSpec((None, None) + shape, lambda h, p, im, jm: (h, jm[p], 0, 0))

    anywhere = pl.BlockSpec(memory_space=pl.ANY)
    grid_spec = pltpu.PrefetchScalarGridSpec(
        num_scalar_prefetch=2, grid=(heads, npairs),
        in_specs=[pl.BlockSpec(memory_space=pltpu.SMEM),
                  anywhere, anywhere, at_k((t, LANES)), at_k((LANES, t)), at_k((t, LANES)),
                  anywhere, anywhere, anywhere, anywhere] + [anywhere] * nc,
        out_specs=[pl.BlockSpec((None, nb, LANES, t), lambda h, p, im, jm: (h, 0, 0, 0)),
                   at_k((t, LANES)), at_k((t, LANES))] + [anywhere] * nc,
        scratch_shapes=[pltpu.VMEM((2, 2, t, LANES), MXU_DTYPE), pltpu.VMEM((2, 3, LANES, t), MXU_DTYPE),
                        pltpu.VMEM((2, 1, t), F32), pltpu.SemaphoreType.DMA((6, 2))]
        + (_exchange_sems(carry) if nc else []))
    res = pl.pallas_call(
        body, name="fox_attn_bwd", grid_spec=grid_spec,
        out_shape=[jax.ShapeDtypeStruct((heads, nb, LANES, t), F32),
                   jax.ShapeDtypeStruct((heads, nb, t, LANES), F32),
                   jax.ShapeDtypeStruct((heads, nb, t, LANES), F32)] + _exchange_shapes(carry),
        compiler_params=_params(VMEM_BIG),
    )(jnp.asarray(im), jnp.asarray(jm), active, qa, qat, ka, kat, va, ot, lse, do, dot_, *[c[1] for c in carry])
    return res[0], res[1], res[2], [r.reshape(w.shape) for r, w in zip(res[3:], whole)]


def _fox_post(dqt, dka, dva, f, bf):
    heads, nb, t, _ = dka.shape
    s = nb * t
    _, _, _, put = _fox_constants()
    npair = heads // 2

    def body(dqt_ref, dka_ref, dva_ref, f_ref, bf_ref, put_ref, dq_ref, dk_ref, dv_ref, df_ref, sums_ref,
             dc_ref, carry_ref):
        i, h = pl.program_id(0), pl.program_id(1)

        @pl.when((i == 0) & (h == 0))
        def _():
            carry_ref[...] = jnp.zeros_like(carry_ref)
            sums_ref[...] = jnp.zeros_like(sums_ref)

        @pl.when(h == 0)
        def _():
            dc_ref[...] = jnp.zeros_like(dc_ref)

        dqt_v = dqt_ref[...]
        dka_v = dka_ref[...]
        term_q = _dot(_mx(dqt_v), put_ref[0], TN)
        term_k = _dot(_mx(dka_v), put_ref[1])
        term_v = _dot(_mx(dva_ref[...]), put_ref[1])

        @pl.when(h % 2 == 0)
        def _():
            dq_ref[...] = _mx(term_q)
            dk_ref[...] = _mx(term_k)
            dv_ref[...] = _mx(term_v)

        @pl.when(h % 2 == 1)
        def _():
            dq_ref[...] += _mx(term_q)
            dk_ref[...] += _mx(term_k)
            dv_ref[...] += _mx(term_v)

        dcum = dqt_v[BIAS_Q[0]:BIAS_Q[0] + 1, :] - dka_v.T[BIAS_K[0]:BIAS_K[0] + 1, :]
        head_row = lax.broadcasted_iota(jnp.int32, (heads, 1), 0) == h
        dc_ref[...] += jnp.where(head_row, dcum, 0.0)

        @pl.when(h == heads - 1)
        def _():
            later = (lax.broadcasted_iota(jnp.int32, (t, t), 0) >= lax.broadcasted_iota(jnp.int32, (t, t), 1)).astype(F32)
            dlf_t = _dot_onehot(dc_ref[...], later) + carry_ref[:, 0:1]
            carry_ref[...] = jnp.broadcast_to(dlf_t[:, 0:1], carry_ref.shape)
            dlf = jnp.concatenate([dlf_t, jnp.zeros((LANES - heads, t), F32)], axis=0).T
            lane = lax.broadcasted_iota(jnp.int32, (1, LANES), 1)
            df = jnp.where(lane < heads, dlf * jax.nn.sigmoid(-(f_ref[...] + bf_ref[0:1, :])), 0.0)
            df_ref[...] = _mx(df)
            sums_ref[0:1, :] += jnp.sum(df, axis=0, keepdims=True)

    rev = lambda i: nb - 1 - i
    pair_spec = pl.BlockSpec((t, LANES), lambda i, h: (rev(i), h // 2))
    blk = pl.BlockSpec((t, LANES), lambda i, h: (rev(i), 0))
    hd = jax.ShapeDtypeStruct((s, D_MODEL), MXU_DTYPE)
    return pl.pallas_call(
        body, name="fox_post", grid=(nb, heads),
        in_specs=[pl.BlockSpec((None, None, LANES, t), lambda i, h: (h, rev(i), 0, 0)),
                  pl.BlockSpec((None, None, t, LANES), lambda i, h: (h, rev(i), 0, 0)),
                  pl.BlockSpec((None, None, t, LANES), lambda i, h: (h, rev(i), 0, 0)),
                  blk, pl.BlockSpec((8, LANES), lambda i, h: (0, 0)),
                  pl.BlockSpec((None, 2, LANES, LANES), lambda i, h: (h % 2, 0, 0, 0))],
        out_specs=[pair_spec, pair_spec, pair_spec, blk, pl.BlockSpec((8, LANES), lambda i, h: (0, 0))],
        out_shape=[hd, hd, hd, jax.ShapeDtypeStruct((s, LANES), MXU_DTYPE), jax.ShapeDtypeStruct((8, LANES), F32)],
        scratch_shapes=[pltpu.VMEM((heads, t), F32), pltpu.VMEM((heads, LANES), F32)],
        compiler_params=_params(VMEM_BIG),
    )(dqt, dka, dva, f, bf, _mx(jnp.asarray(put)))


def _fox_weights(w_in, w_o):
    wqkv = w_in[:, :3 * D_MODEL]
    wf = jnp.pad(w_in[:, 3 * D_MODEL:], ((0, 0), (0, LANES - FOX_HEADS)))
    wo_heads = w_o.reshape(FOX_HEADS, FOX_HEAD_DIM, D_MODEL)
    wo_a = jnp.pad(wo_heads, ((0, 0), (0, LANES - FOX_HEAD_DIM), (0, 0)))
    wo_rows = wo_a.reshape(FOX_HEADS * LANES, D_MODEL)
    return dict(wqkv=wqkv, wf=wf, wqkv_t=wqkv.T, wf_t=wf.T, wo_rows=wo_rows, wo_rows_t=wo_rows.T)


def _fox_out(ot, wo_rows):
    heads, nb, _, t = ot.shape

    def body(ot_ref, w_ref, y_ref):
        y_ref[...] = _dot(ot_ref[...].reshape(heads * LANES, t), w_ref[...], TN)

    return pl.pallas_call(
        body, name="fox_out", grid=(nb,),
        in_specs=[pl.BlockSpec((heads, None, LANES, t), lambda i: (0, i, 0, 0)),
                  pl.BlockSpec((heads * LANES, D_MODEL), lambda i: (0, 0))],
        out_specs=pl.BlockSpec((t, D_MODEL), lambda i: (i, 0)),
        out_shape=jax.ShapeDtypeStruct((nb * t, D_MODEL), F32),
        compiler_params=_params(VMEM_BIG),
    )(ot, wo_rows)


def _fox_do(dy, wo_rows_t, nb, t):
    heads = FOX_HEADS

    def body(dy_ref, w_ref, do_ref, dot_ref):
        val = _dot(dy_ref[...], w_ref[...])
        for h in range(heads):
            blk = val[:, h * LANES:(h + 1) * LANES]
            do_ref[h] = _mx(blk)
            dot_ref[h] = _mx(blk.T)

    return pl.pallas_call(
        body, name="fox_do", grid=(nb,),
        in_specs=[pl.BlockSpec((t, D_MODEL), lambda i: (i, 0)),
                  pl.BlockSpec((D_MODEL, heads * LANES), lambda i: (0, 0))],
        out_specs=[pl.BlockSpec((heads, None, t, LANES), lambda i: (0, i, 0, 0)),
                   pl.BlockSpec((heads, None, LANES, t), lambda i: (0, i, 0, 0))],
        out_shape=[jax.ShapeDtypeStruct((heads, nb, t, LANES), MXU_DTYPE),
                   jax.ShapeDtypeStruct((heads, nb, LANES, t), MXU_DTYPE)],
        compiler_params=_params(VMEM_BIG),
    )(dy, wo_rows_t)


def _fox_dwo(ot, dy):
    heads, nb, _, t = ot.shape

    def body(ot_ref, dy_ref, o_ref):
        part = _dot(ot_ref[...].reshape(heads * LANES, t), dy_ref[...])

        @pl.when(pl.program_id(0) == 0)
        def _():
            o_ref[...] = part

        @pl.when(pl.program_id(0) > 0)
        def _():
            o_ref[...] += part

    return pl.pallas_call(
        body, name="fox_dwo", grid=(nb,),
        in_specs=[pl.BlockSpec((heads, None, LANES, t), lambda i: (0, i, 0, 0)),
                  pl.BlockSpec((t, D_MODEL), lambda i: (i, 0))],
        out_specs=pl.BlockSpec((heads * LANES, D_MODEL), lambda i: (0, 0)),
        out_shape=jax.ShapeDtypeStruct((heads * LANES, D_MODEL), F32),
        compiler_params=_params(VMEM_BIG),
    )(ot, dy)


def _fox_fwd(u, w, bf, carry=()):
    qkv = _mm_nn("fox_qkv", u, w["wqkv"], MXU_DTYPE)
    f = _mm_nn("fox_f", u, w["wf"], F32)
    qa, qat, ka, kat, va, vat, stats = _fox_prep(qkv, f, bf)
    ot, lse, carried = _fox_attn_fwd(qat, ka, vat, _fox_active(stats, by_key=False), carry)
    y = _fox_out(ot, w["wo_rows"])
    return y, dict(f=f, qa=qa, qat=qat, ka=ka, kat=kat, va=va, ot=ot, lse=lse, stats=stats), carried


def _fox_bwd(dy, u, w, bf, res, carry=()):
    heads, nb, t, _ = res["qa"].shape
    do, dot_ = _fox_do(dy, w["wo_rows_t"], nb, t)
    dwo_a = _fox_dwo(res["ot"], dy).reshape(heads, LANES, D_MODEL)
    dqt, dka, dva, carried = _fox_attn_bwd(res["qa"], res["qat"], res["ka"], res["kat"], res["va"], res["ot"],
                                           res["lse"], do, dot_, _fox_active(res["stats"], by_key=True), carry)
    dq, dk, dv, df, sums = _fox_post(dqt, dka, dva, res["f"], bf)
    wt = w["wqkv_t"]
    du = _mm_nn("fox_du_q", dq, wt[:D_MODEL], F32)
    du = _mm_nn("fox_du_k", dk, wt[D_MODEL:2 * D_MODEL], F32, addend=du)
    du = _mm_nn("fox_du_v", dv, wt[2 * D_MODEL:], F32, addend=du)
    du = _mm_nn("fox_du_f", df, w["wf_t"], F32, addend=du)
    dw_in = jnp.concatenate(
        [_mm_tn("fox_dw_q", u, dq), _mm_tn("fox_dw_k", u, dk), _mm_tn("fox_dw_v", u, dv),
         _mm_tn("fox_dw_f", u, df)[:, :FOX_HEADS]], axis=1)
    dw_o = dwo_a[:, :FOX_HEAD_DIM, :].reshape(D_MODEL, D_MODEL)
    return du, dw_in, dw_o, sums, carried


def _dsilu(v):
    sg = jax.nn.sigmoid(v)
    return sg * (1.0 + v * (1.0 - sg))


def _conv_taps(scr_ref, w_ref, rows, base):
    acc = None
    for k in range(SSM_CONV):
        term = scr_ref[pl.ds(base - (SSM_CONV - 1) + k, rows), :] * w_ref[k:k + 1, :]
        acc = term if acc is None else acc + term
    return acc


def _conv_fwd(zx, cw, cb):
    s = zx.shape[0]
    tb = _tile(s, 512)
    half = SSM_CONV_DIM // 2
    hb = tb // SUBLANES

    def body(x_ref, halo_ref, w_ref, b_ref, o_ref, scr_ref):
        i = pl.program_id(0)
        scr_ref[pl.ds(0, SUBLANES), :] = jnp.where(i > 0, halo_ref[...], 0.0)
        scr_ref[pl.ds(SUBLANES, tb), :] = x_ref[...]
        o_ref[...] = jax.nn.silu(_conv_taps(scr_ref, w_ref, tb, SUBLANES) + b_ref[0:1, :])

    return pl.pallas_call(
        body, name="ssd_conv_fwd", grid=(s // tb, 2),
        in_specs=[pl.BlockSpec((tb, half), lambda i, j: (i, 1 + j)),
                  pl.BlockSpec((SUBLANES, half), lambda i, j: (jnp.maximum(i * hb - 1, 0), 1 + j)),
                  pl.BlockSpec((8, half), lambda i, j: (0, j)),
                  pl.BlockSpec((8, half), lambda i, j: (0, j))],
        out_specs=pl.BlockSpec((tb, half), lambda i, j: (i, j)),
        out_shape=jax.ShapeDtypeStruct((s, SSM_CONV_DIM), F32),
        scratch_shapes=[pltpu.VMEM((tb + SUBLANES, half), F32)],
    )(zx, zx, cw, cb)


def _conv_bwd_pre(zx, dxc, cw, cb):
    s = zx.shape[0]
    tb = _tile(s, 512)
    half = SSM_CONV_DIM // 2
    hb = tb // SUBLANES

    def body(x_ref, halo_ref, d_ref, w_ref, b_ref, o_ref, sums_ref, scr_ref):
        i = pl.program_id(1)

        @pl.when(i == 0)
        def _():
            sums_ref[...] = jnp.zeros_like(sums_ref)

        scr_ref[pl.ds(0, SUBLANES), :] = jnp.where(i > 0, halo_ref[...], 0.0)
        scr_ref[pl.ds(SUBLANES, tb), :] = x_ref[...]
        pre = _conv_taps(scr_ref, w_ref, tb, SUBLANES) + b_ref[0:1, :]
        dpre = d_ref[...] * _dsilu(pre)
        o_ref[...] = dpre
        for k in range(SSM_CONV):
            shifted = scr_ref[pl.ds(SUBLANES - (SSM_CONV - 1) + k, tb), :]
            sums_ref[k:k + 1, :] += jnp.sum(dpre * shifted, axis=0, keepdims=True)
        sums_ref[SSM_CONV:SSM_CONV + 1, :] += jnp.sum(dpre, axis=0, keepdims=True)

    return pl.pallas_call(
        body, name="ssd_conv_bwd_pre", grid=(2, s // tb),
        in_specs=[pl.BlockSpec((tb, half), lambda j, i: (i, 1 + j)),
                  pl.BlockSpec((SUBLANES, half), lambda j, i: (jnp.maximum(i * hb - 1, 0), 1 + j)),
                  pl.BlockSpec((tb, half), lambda j, i: (i, j)),
                  pl.BlockSpec((8, half), lambda j, i: (0, j)),
                  pl.BlockSpec((8, half), lambda j, i: (0, j))],
        out_specs=[pl.BlockSpec((tb, half), lambda j, i: (i, j)),
                   pl.BlockSpec((8, half), lambda j, i: (0, j))],
        out_shape=[jax.ShapeDtypeStruct((s, SSM_CONV_DIM), F32), jax.ShapeDtypeStruct((8, SSM_CONV_DIM), F32)],
        scratch_shapes=[pltpu.VMEM((tb + SUBLANES, half), F32)],
    )(zx, zx, dxc, cw, cb)


def _conv_bwd_x(dpre, cw):
    s = dpre.shape[0]
    tb = _tile(s, 512)
    hb = tb // SUBLANES
    nb = s // tb

    def body(d_ref, halo_ref, w_ref, o_ref, scr_ref):
        i = pl.program_id(0)
        scr_ref[pl.ds(0, tb), :] = d_ref[...]
        scr_ref[pl.ds(tb, SUBLANES), :] = jnp.where(i < nb - 1, halo_ref[...], 0.0)
        acc = None
        for k in range(SSM_CONV):
            term = scr_ref[pl.ds(SSM_CONV - 1 - k, tb), :] * w_ref[k:k + 1, :]
            acc = term if acc is None else acc + term
        o_ref[...] = _mx(acc)

    return pl.pallas_call(
        body, name="ssd_conv_bwd_x", grid=(nb,),
        in_specs=[pl.BlockSpec((tb, SSM_CONV_DIM), lambda i: (i, 0)),
                  pl.BlockSpec((SUBLANES, SSM_CONV_DIM), lambda i: (jnp.minimum((i + 1) * hb, s // SUBLANES - 1), 0)),
                  pl.BlockSpec((8, SSM_CONV_DIM), lambda i: (0, 0))],
        out_specs=pl.BlockSpec((tb, SSM_CONV_DIM), lambda i: (i, 0)),
        out_shape=jax.ShapeDtypeStruct((s, SSM_CONV_DIM), MXU_DTYPE),
        scratch_shapes=[pltpu.VMEM((tb + SUBLANES, SSM_CONV_DIM), F32)],
        compiler_params=_params(VMEM_BIG),
    )(dpre, dpre, cw)


def _expand_constants():
    ex = np.zeros((LANES, SSM_D_INNER), np.float32)
    for h in range(SSM_HEADS):
        ex[h, h * 64:(h + 1) * 64] = 1.0
    return ex, np.ascontiguousarray(ex.T)


def _ssd_common(dtr_ref, par_ref, ex_ref, xc_ref):
    lc = SSM_CHUNK
    lane = lax.broadcasted_iota(jnp.int32, (1, LANES), 1)
    is_head = lane < SSM_HEADS
    par = par_ref[...]
    pre = dtr_ref[...] + par[0:1, :]
    dt = jnp.where(is_head, jax.nn.softplus(pre), 0.0)
    a = jnp.where(is_head, -jnp.exp(par[1:2, :]), 0.0)
    tri_b = lax.broadcasted_iota(jnp.int32, (lc, lc), 0) >= lax.broadcasted_iota(jnp.int32, (lc, lc), 1)
    tri = tri_b.astype(F32)
    da = dt * a
    acs = _dot_onehot(tri, da, onehot="a")
    acs_t = _dot_onehot(da, tri, (((0,), (1,)), ((), ())))
    wide = _dot_onehot(jnp.concatenate([dt, acs, par], axis=0), ex_ref[...])
    dt_x, acs_x, d_x = wide[0:lc], wide[lc:2 * lc], wide[2 * lc + 2:2 * lc + 3]
    last_x = acs_x[lc - 1:lc, :]
    xs = xc_ref[:, 0:SSM_D_INNER]
    return dict(pre=pre, dt=dt, a=a, tri_b=tri_b, tri=tri, acs=acs, acs_t=acs_t, dt_x=dt_x, d_x=d_x, xs=xs,
                xdt=xs * dt_x, e_x=jnp.exp(acs_x), dte_x=jnp.exp(last_x - acs_x), cd_x=jnp.exp(last_x),
                is_head=is_head)


def _decay_in(q, h):
    seg = q["acs"][:, h:h + 1] - q["acs_t"][h:h + 1, :]
    return jnp.exp(jnp.where(q["tri_b"], seg, -jnp.inf))


def _ssd_scan_fwd(xc, dtr, par):
    s = xc.shape[0]
    lc = SSM_CHUNK
    nc = s // lc
    ex, _ = _expand_constants()

    def body(xc_ref, dtr_ref, par_ref, ex_ref, y_ref, prev_ref, st_ref):
        @pl.when(pl.program_id(0) == 0)
        def _():
            st_ref[...] = jnp.zeros_like(st_ref)

        q = _ssd_common(dtr_ref, par_ref, ex_ref, xc_ref)
        lane = lax.broadcasted_iota(jnp.int32, (1, LANES), 1)
        for g in range(SSM_GROUPS):
            sl = slice(g * GROUP_W, (g + 1) * GROUP_W)
            bg = _mx(xc_ref[:, SSM_D_INNER + g * SSM_STATE:SSM_D_INNER + (g + 1) * SSM_STATE])
            cg = _mx(xc_ref[:, SSM_D_INNER + (SSM_GROUPS + g) * SSM_STATE:SSM_D_INNER + (SSM_GROUPS + g + 1) * SSM_STATE])
            gm = _dot(cg, bg, NT)
            prev = st_ref[g]
            prev_ref[g] = prev
            yoff = _dot(cg, _mx(prev)) * q["e_x"][:, sl]
            st_ref[g] = prev * q["cd_x"][:, sl] + _dot(bg, _mx(q["xdt"][:, sl] * q["dte_x"][:, sl]), TN)
            pairs = []
            for pr in range(2):
                xp = _mx(q["xdt"][:, g * GROUP_W + pr * LANES:g * GROUP_W + (pr + 1) * LANES])
                both = [_dot(_mx(gm * _decay_in(q, 4 * g + 2 * pr + r2)), xp) for r2 in range(2)]
                pairs.append(jnp.where(lane < 64, both[0], both[1]))
            y_ref[:, sl] = jnp.concatenate(pairs, axis=1) + yoff + q["xs"][:, sl] * q["d_x"][:, sl]

    return pl.pallas_call(
        body, name="ssd_scan_fwd", grid=(nc,),
        in_specs=[pl.BlockSpec((lc, SSM_CONV_DIM), lambda c: (c, 0)),
                  pl.BlockSpec((lc, LANES), lambda c: (c, 0)),
                  pl.BlockSpec((8, LANES), lambda c: (0, 0)),
                  pl.BlockSpec((LANES, SSM_D_INNER), lambda c: (0, 0))],
        out_specs=[pl.BlockSpec((lc, SSM_D_INNER), lambda c: (c, 0)),
                   pl.BlockSpec((None, SSM_GROUPS, SSM_STATE, GROUP_W), lambda c: (c, 0, 0, 0))],
        out_shape=[jax.ShapeDtypeStruct((s, SSM_D_INNER), F32),
                   jax.ShapeDtypeStruct((nc, SSM_GROUPS, SSM_STATE, GROUP_W), F32)],
        scratch_shapes=[pltpu.VMEM((SSM_GROUPS, SSM_STATE, GROUP_W), F32)],
        compiler_params=_params(VMEM_BIG),
    )(xc, dtr, par, _mx(jnp.asarray(ex)))


def _ssd_scan_bwd(dy, xc, dtr, par, prev):
    s = xc.shape[0]
    lc = SSM_CHUNK
    nc = s // lc
    ex, ex_t = _expand_constants()

    def body(dy_ref, xc_ref, dtr_ref, par_ref, prev_ref, ex_ref, ext_ref, dxc_ref, ddtr_ref, sums_ref,
             gst_ref, tacs_ref, tdt_ref, tdd_ref):
        @pl.when(pl.program_id(0) == 0)
        def _():
            gst_ref[...] = jnp.zeros_like(gst_ref)
            sums_ref[...] = jnp.zeros_like(sums_ref)

        q = _ssd_common(dtr_ref, par_ref, ex_ref, xc_ref)
        lane = lax.broadcasted_iota(jnp.int32, (1, LANES), 1)
        row = lax.broadcasted_iota(jnp.int32, (lc, 1), 0)
        dacs_rows = jnp.zeros((lc, LANES), F32)
        dacs_cols_t = jnp.zeros((LANES, lc), F32)
        for g in range(SSM_GROUPS):
            sl = slice(g * GROUP_W, (g + 1) * GROUP_W)
            b_lo = SSM_D_INNER + g * SSM_STATE
            c_lo = SSM_D_INNER + (SSM_GROUPS + g) * SSM_STATE
            bg = _mx(xc_ref[:, b_lo:b_lo + SSM_STATE])
            cg = _mx(xc_ref[:, c_lo:c_lo + SSM_STATE])
            dyg = dy_ref[:, sl]
            xsg, xdtg = q["xs"][:, sl], q["xdt"][:, sl]
            eg, dteg, cdg = q["e_x"][:, sl], q["dte_x"][:, sl], q["cd_x"][:, sl]
            prevg = prev_ref[g]
            gs = gst_ref[g]
            prevm, gsm = _mx(prevg), _mx(gs)
            tdd_ref[:, sl] = dyg * xsg
            dxs = dyg * q["d_x"][:, sl]
            t_acs = dyg * _dot(cg, prevm) * eg
            dcp = _mx(dyg * eg)
            dc = _dot(dcp, prevm, NT)
            dprev = _dot(cg, dcp, TN)
            db = _dot(_mx(xdtg * dteg), gsm, NT)
            dx2 = _dot(bg, gsm)
            dxdt = dx2 * dteg
            ddte = dx2 * xdtg * dteg
            t_acs = t_acs - ddte
            last = (jnp.sum(ddte, axis=0, keepdims=True)
                    + jnp.sum(gs * prevg, axis=0, keepdims=True) * cdg)
            gm = _dot(cg, bg, NT)
            dgm = jnp.zeros((lc, lc), F32)
            pair_dx = []
            for pr in range(2):
                lo = g * GROUP_W + pr * LANES
                xp = _mx(q["xdt"][:, lo:lo + LANES])
                dyp = dy_ref[:, lo:lo + LANES]
                both = []
                for r2 in range(2):
                    h = 4 * g + 2 * pr + r2
                    mine = (lane >= 64 * r2) & (lane < 64 * (r2 + 1))
                    lm = _decay_in(q, h)
                    m = gm * lm
                    dm = _dot(_mx(jnp.where(mine, dyp, 0.0)), xp, NT)
                    dgm = dgm + dm * lm
                    w = dm * m
                    dacs_rows = dacs_rows + jnp.sum(w, axis=1, keepdims=True) * (lane == h).astype(F32)
                    head_row = (lax.broadcasted_iota(jnp.int32, (LANES, 1), 0) == h).astype(F32)
                    dacs_cols_t = dacs_cols_t + head_row * jnp.sum(w, axis=0, keepdims=True)
                    both.append(_dot(_mx(m), _mx(dyp), TN))
                pair_dx.append(jnp.where(lane < 64, both[0], both[1]))
            dxdt = dxdt + jnp.concatenate(pair_dx, axis=1)
            dgmm = _mx(dgm)
            dc = dc + _dot(dgmm, bg)
            db = db + _dot(dgmm, cg, TN)
            dxs = dxs + dxdt * q["dt_x"][:, sl]
            tdt_ref[:, sl] = dxdt * xsg
            tacs_ref[:, sl] = t_acs + jnp.where(row == lc - 1, last, 0.0)
            dxc_ref[:, sl] = dxs
            dxc_ref[:, b_lo:b_lo + SSM_STATE] = db
            dxc_ref[:, c_lo:c_lo + SSM_STATE] = dc
            gst_ref[g] = gs * cdg + dprev
        tdd = jnp.broadcast_to(jnp.sum(tdd_ref[...], axis=0, keepdims=True), (8, SSM_D_INNER))
        heads_of = _dot_onehot(jnp.concatenate([tacs_ref[...], tdt_ref[...], tdd], axis=0), ext_ref[...])
        dacs = heads_of[0:lc] + dacs_rows - dacs_cols_t.T
        dda = _dot_onehot(q["tri"], dacs, TN, onehot="a")
        ddt = dda * q["a"] + heads_of[lc:2 * lc]
        ddtr = jnp.where(q["is_head"], ddt * jax.nn.sigmoid(q["pre"]), 0.0)
        ddtr_ref[...] = _mx(ddtr)
        sums_ref[0:1, :] += jnp.sum(ddtr, axis=0, keepdims=True)
        sums_ref[1:2, :] += jnp.sum(dda * q["dt"], axis=0, keepdims=True) * q["a"]
        sums_ref[2:3, :] += heads_of[2 * lc:2 * lc + 1]

    rev = lambda c: nc - 1 - c
    wide = pltpu.VMEM((lc, SSM_D_INNER), F32)
    return pl.pallas_call(
        body, name="ssd_scan_bwd", grid=(nc,),
        in_specs=[pl.BlockSpec((lc, SSM_D_INNER), lambda c: (rev(c), 0)),
                  pl.BlockSpec((lc, SSM_CONV_DIM), lambda c: (rev(c), 0)),
                  pl.BlockSpec((lc, LANES), lambda c: (rev(c), 0)),
                  pl.BlockSpec((8, LANES), lambda c: (0, 0)),
                  pl.BlockSpec((None, SSM_GROUPS, SSM_STATE, GROUP_W), lambda c: (rev(c), 0, 0, 0)),
                  pl.BlockSpec((LANES, SSM_D_INNER), lambda c: (0, 0)),
                  pl.BlockSpec((SSM_D_INNER, LANES), lambda c: (0, 0))],
        out_specs=[pl.BlockSpec((lc, SSM_CONV_DIM), lambda c: (rev(c), 0)),
                   pl.BlockSpec((lc, LANES), lambda c: (rev(c), 0)),
                   pl.BlockSpec((8, LANES), lambda c: (0, 0))],
        out_shape=[jax.ShapeDtypeStruct((s, SSM_CONV_DIM), F32), jax.ShapeDtypeStruct((s, LANES), MXU_DTYPE),
                   jax.ShapeDtypeStruct((8, LANES), F32)],
        scratch_shapes=[pltpu.VMEM((SSM_GROUPS, SSM_STATE, GROUP_W), F32), wide, wide, wide],
        compiler_params=_params(VMEM_BIG),
    )(dy, xc, dtr, par, prev, _mx(jnp.asarray(ex)), _mx(jnp.asarray(ex_t)))


def _group_norm_parts(yv, zv):
    yg = yv * jax.nn.silu(zv)
    normed, rinvs = [], []
    for g in range(SSM_GROUPS):
        blk = yg[:, g * GROUP_W:(g + 1) * GROUP_W]
        rinv = lax.rsqrt(jnp.mean(blk * blk, axis=-1, keepdims=True) + RMS_EPS)
        normed.append(blk * rinv)
        rinvs.append(rinv)
    return normed, rinvs


def _gnorm_fwd(y, zx, nw):
    s = y.shape[0]
    tb = _tile(s, 512)

    def body(y_ref, z_ref, w_ref, o_ref):
        normed, _ = _group_norm_parts(y_ref[...], z_ref[...])
        for g in range(SSM_GROUPS):
            sl = slice(g * GROUP_W, (g + 1) * GROUP_W)
            o_ref[:, sl] = _mx(normed[g] * w_ref[0:1, sl])

    row = pl.BlockSpec((tb, SSM_D_INNER), lambda i: (i, 0))
    return pl.pallas_call(
        body, name="ssd_gnorm_fwd", grid=(s // tb,),
        in_specs=[row, row, pl.BlockSpec((8, SSM_D_INNER), lambda i: (0, 0))],
        out_specs=row, out_shape=jax.ShapeDtypeStruct((s, SSM_D_INNER), MXU_DTYPE),
    )(y, zx, nw)


def _gnorm_bwd(y, zx, nw, dyn):
    s = y.shape[0]
    tb = _tile(s, 512)

    def body(y_ref, z_ref, w_ref, d_ref, dy_ref, dz_ref, sums_ref):
        @pl.when(pl.program_id(0) == 0)
        def _():
            sums_ref[...] = jnp.zeros_like(sums_ref)

        yv, zv = y_ref[...], z_ref[...]
        normed, rinvs = _group_norm_parts(yv, zv)
        gate = jax.nn.silu(zv)
        dgate = _dsilu(zv)
        for g in range(SSM_GROUPS):
            sl = slice(g * GROUP_W, (g + 1) * GROUP_W)
            dv = d_ref[:, sl]
            n = normed[g]
            sums_ref[0:1, sl] += jnp.sum(dv * n, axis=0, keepdims=True)
            dn = dv * w_ref[0:1, sl]
            dyg = rinvs[g] * (dn - n * jnp.mean(dn * n, axis=-1, keepdims=True))
            dy_ref[:, sl] = dyg * gate[:, sl]
            dz_ref[:, sl] = _mx(dyg * yv[:, sl] * dgate[:, sl])

    row = pl.BlockSpec((tb, SSM_D_INNER), lambda i: (i, 0))
    par = pl.BlockSpec((8, SSM_D_INNER), lambda i: (0, 0))
    return pl.pallas_call(
        body, name="ssd_gnorm_bwd", grid=(s // tb,),
        in_specs=[row, row, par, row], out_specs=[row, row, par],
        out_shape=[jax.ShapeDtypeStruct((s, SSM_D_INNER), F32), jax.ShapeDtypeStruct((s, SSM_D_INNER), MXU_DTYPE),
                   jax.ShapeDtypeStruct((8, SSM_D_INNER), F32)],
    )(y, zx, nw, dyn)


def _rows8(v):
    v = v.reshape(1, -1)
    return jnp.pad(v, ((0, 7), (0, 0)))


def _ssd_weights(w_in, w_out):
    nzx = SSM_D_INNER + SSM_CONV_DIM
    wzx = w_in[:, :nzx]
    wdt = jnp.pad(w_in[:, nzx:], ((0, 0), (0, LANES - SSM_HEADS)))
    return dict(wzx=wzx, wdt=wdt, wzx_t=wzx.T, wdt_t=wdt.T, wout=w_out, wout_t=w_out.T)


def _ssd_fwd(u, w, cw, cb, par, nw):
    zx = _mm_nn("ssd_in_zx", u, w["wzx"], F32)
    dtr = _mm_nn("ssd_in_dt", u, w["wdt"], F32)
    xc = _conv_fwd(zx, cw, cb)
    y, prev = _ssd_scan_fwd(xc, dtr, par)
    yn = _gnorm_fwd(y, zx, nw)
    out = _mm_nn("ssd_out", yn, w["wout"], F32)
    return out, dict(zx=zx, dtr=dtr, xc=xc, y=y, prev=prev, yn=yn)


def _ssd_bwd(dy, u, w, cw, cb, par, nw, res):
    dyn = _mm_nn("ssd_dyn", dy, w["wout_t"], F32)
    dw_out = _mm_tn("ssd_dw_out", res["yn"], dy)
    dys, dz, nsum = _gnorm_bwd(res["y"], res["zx"], nw, dyn)
    dxc, ddtr, ssum = _ssd_scan_bwd(dys, res["xc"], res["dtr"], par, res["prev"])
    dpre, csum = _conv_bwd_pre(res["zx"], dxc, cw, cb)
    dxbc = _conv_bwd_x(dpre, cw)
    wt = w["wzx_t"]
    du = _mm_nn("ssd_du_z", dz, wt[:SSM_D_INNER], F32)
    du = _mm_nn("ssd_du_x", dxbc, wt[SSM_D_INNER:], F32, addend=du)
    du = _mm_nn("ssd_du_dt", ddtr, w["wdt_t"], F32, addend=du)
    dw_in = jnp.concatenate(
        [_mm_tn("ssd_dw_z", u, dz), _mm_tn("ssd_dw_x", u, dxbc), _mm_tn("ssd_dw_dt", u, ddtr)[:, :SSM_HEADS]], axis=1)
    small = dict(conv_w=csum[:SSM_CONV], conv_b=csum[SSM_CONV], dt_bias=ssum[0, :SSM_HEADS],
                 a_log=ssum[1, :SSM_HEADS], d=ssum[2, :SSM_HEADS], norm_w=nsum[0])
    return du, dw_in, dw_out, small


def _ada_fwd(c_all, ada_w, ada_b_mine):
    nl, _, ncol = ada_w.shape

    def body(c_ref, w_ref, b_ref, o_ref):
        cond = _mx(jax.nn.silu(c_ref[...]))
        for i in range(nl):
            o_ref[i] = _dot(cond, _mx(w_ref[i])) + b_ref[i:i + 1, :]

    return pl.pallas_call(
        body, name="ada_fwd", out_shape=jax.ShapeDtypeStruct((nl, 2 * N_DEV, ncol), F32),
        compiler_params=_params(VMEM_BIG),
    )(c_all, ada_w, ada_b_mine)


def _ada_bwd(c_all, dmod_cols):
    nl, _, ncol = dmod_cols.shape

    def body(c_ref, d_ref, o_ref):
        cond = _mx(jax.nn.silu(c_ref[...]))
        for i in range(nl):
            o_ref[i] = _dot(cond, _mx(d_ref[i]), TN)

    return pl.pallas_call(
        body, name="ada_bwd", out_shape=jax.ShapeDtypeStruct((nl, D_MODEL, ncol), F32),
        compiler_params=_params(VMEM_BIG),
    )(c_all, dmod_cols)


def _adamw(gslots, w, m, v, name):
    k, r, c = gslots.shape
    tr = _tile(r, 256) if r % 256 == 0 else r
    c1 = 1.0 - ADAM_B1 ** ADAM_STEP
    c2 = 1.0 - ADAM_B2 ** ADAM_STEP

    def body(g_ref, w_ref, m_ref, v_ref, go_ref, d_ref, mo_ref, vo_ref):
        g = g_ref[0]
        for slot in range(1, k):
            g = g + g_ref[slot]
        mn = ADAM_B1 * m_ref[...] + (1.0 - ADAM_B1) * g
        vn = ADAM_B2 * v_ref[...] + (1.0 - ADAM_B2) * jnp.square(g)
        go_ref[...] = g
        mo_ref[...] = mn
        vo_ref[...] = vn
        d_ref[...] = -ADAM_LR * ((mn / c1) / (jnp.sqrt(vn / c2) + ADAM_EPS) + ADAM_WD * w_ref[...])

    row = pl.BlockSpec((tr, c), lambda i: (i, 0))
    shp = jax.ShapeDtypeStruct((r, c), F32)
    return pl.pallas_call(
        body, name=name, grid=(r // tr,),
        in_specs=[pl.BlockSpec((k, tr, c), lambda i: (0, i, 0)), row, row, row],
        out_specs=[row, row, row, row], out_shape=[shp, shp, shp, shp],
        compiler_params=_params(VMEM_BIG),
    )(gslots, w, m, v)


def _adamw_any(gslots, w, m, v, name):
    shape = w.shape
    two_d = (-1, shape[-1])
    k = gslots.shape[0]
    outs = _adamw(gslots.reshape((k,) + w.reshape(two_d).shape), w.reshape(two_d), m.reshape(two_d),
                  v.reshape(two_d), name)
    return tuple(o.reshape(shape) for o in outs)


def _cols_from_slots(g):
    g = jnp.moveaxis(g, 0, -2)
    return g.reshape(g.shape[:-2] + (g.shape[-2] * g.shape[-1],))


def _rows_from_slots(g):
    g = jnp.moveaxis(g, 0, -3)
    return g.reshape(g.shape[:-3] + (g.shape[-3] * g.shape[-2], g.shape[-1]))


def _col_slots(g):
    cs = g.shape[-1] // N_DEV
    return jnp.moveaxis(g.reshape(g.shape[:-1] + (N_DEV, cs)), -2, 0)


def _row_slots(g):
    rs = g.shape[-2] // N_DEV
    return jnp.moveaxis(g.reshape(g.shape[:-2] + (N_DEV, rs, g.shape[-1])), -3, 0)


def _gather_cols(w, name, dtype=None):
    return _cols_from_slots(_all_gather(w.astype(dtype or MXU_DTYPE), name))


def _gather_rows(w, name):
    return _rows_from_slots(_all_gather(_mx(w), name))


def _scatter_cols(g, name):
    return _all_to_all(_col_slots(g), name)


def _scatter_rows(g, name):
    return _all_to_all(_row_slots(g), name)


def kernel(x, c, ada_w, ada_b, ln_mix_g, ln_mix_b, ln_mlp_g, ln_mlp_b, mlp_w1, mlp_w2, fox_w_in, fox_b_f, fox_w_o, ssm_w_in, ssm_conv_w, ssm_conv_b, ssm_dt_bias, ssm_a_log, ssm_d, ssm_norm_w, ssm_w_out, loss_target, m_ada_w, m_ada_b, m_ln_mix_g, m_ln_mix_b, m_ln_mlp_g, m_ln_mlp_b, m_mlp_w1, m_mlp_w2, m_fox_w_in, m_fox_b_f, m_fox_w_o, m_ssm_w_in, m_ssm_conv_w, m_ssm_conv_b, m_ssm_dt_bias, m_ssm_a_log, m_ssm_d, m_ssm_norm_w, m_ssm_w_out, v_ada_w, v_ada_b, v_ln_mix_g, v_ln_mix_b, v_ln_mlp_g, v_ln_mlp_b, v_mlp_w1, v_mlp_w2, v_fox_w_in, v_fox_b_f, v_fox_w_o, v_ssm_w_in, v_ssm_conv_w, v_ssm_conv_b, v_ssm_dt_bias, v_ssm_a_log, v_ssm_d, v_ssm_norm_w, v_ssm_w_out):
    me = 4 * lax.axis_index("x") + 2 * lax.axis_index("y") + lax.axis_index("c")
    xs = x[0]
    target = loss_target[0]
    d = D_MODEL

    c_all = _all_gather(c, "gather_c").reshape(N_DEV, d)
    c_all = jnp.pad(c_all, ((0, N_DEV), (0, 0)))
    ncol = ada_w.shape[-1]
    ada_b_mine = lax.dynamic_slice_in_dim(ada_b, me * ncol, ncol, axis=1)
    mod_cols = _ada_fwd(c_all, ada_w, ada_b_mine)
    mod_all = _all_gather(mod_cols, "gather_mod")
    mod = lax.dynamic_index_in_dim(mod_all, me, axis=2, keepdims=False)
    mod = jnp.moveaxis(mod, 0, 1).reshape(DEPTH, 6, d)

    def pv_rows(*rows):
        return jnp.pad(jnp.stack(rows), ((0, 8 - len(rows)), (0, 0)))

    fw = _fox_weights(_gather_cols(fox_w_in, "gather_fox_in")[0], _gather_rows(fox_w_o, "gather_fox_o")[0])
    conv_w = _gather_cols(ssm_conv_w, "gather_conv_w", F32)[0]
    small_vec = jnp.concatenate([ssm_conv_b[0], ssm_norm_w[0]]).reshape(1, -1)
    small_all = _all_gather(small_vec.astype(F32), "gather_conv_b").reshape(N_DEV, -1)
    conv_b = small_all[:, :SSM_CONV_DIM // N_DEV].reshape(-1)
    norm_w = small_all[:, SSM_CONV_DIM // N_DEV:].reshape(-1)
    cw8 = jnp.pad(conv_w, ((0, 8 - SSM_CONV), (0, 0)))
    cb8 = _rows8(conv_b)
    nw8 = _rows8(norm_w)
    bf8 = _rows8(jnp.pad(fox_b_f[0], (0, LANES - FOX_HEADS)))
    par8 = jnp.pad(jnp.stack([jnp.pad(p[0], (0, LANES - SSM_HEADS)) for p in (ssm_dt_bias, ssm_a_log, ssm_d)]),
                   ((0, 5), (0, 0)))

    sh_a, sc_a, g_a, sh_m, sc_m, g_m = [mod[:, k] for k in range(6)]
    u0 = _modulate(xs, pv_rows(1.0 + sc_a[0], sh_a[0]), "modulate0")
    y0, fres, gathered = _fox_fwd(u0, fw, bf8, [(False, _mx(w)) for w in (mlp_w1, mlp_w2, ssm_w_in, ssm_w_out)])
    w1 = _cols_from_slots(gathered[0])
    w2 = _rows_from_slots(gathered[1])
    sw = _ssd_weights(_cols_from_slots(gathered[2])[0], _rows_from_slots(gathered[3])[0])
    pv0 = pv_rows(1.0 + g_a[0], ln_mix_g[0], ln_mix_b[0], 1.0 + sc_m[0], sh_m[0])
    x1, u1 = _ln_fwd(xs, y0, pv0, "ln_mix0")
    y1, (h0, a0) = _mlp_fwd(u1, w1[0], w2[0], "0")
    pv1 = pv_rows(1.0 + g_m[0], ln_mlp_g[0], ln_mlp_b[0], 1.0 + sc_a[1], sh_a[1])
    x2, u2 = _ln_fwd(x1, y1, pv1, "ln_mlp0")
    y2, sres = _ssd_fwd(u2, sw, cw8, cb8, par8, nw8)
    pv2 = pv_rows(1.0 + g_a[1], ln_mix_g[1], ln_mix_b[1], 1.0 + sc_m[1], sh_m[1])
    x3, u3 = _ln_fwd(x2, y2, pv2, "ln_mix1")
    y3, (h1, a1) = _mlp_fwd(u3, w1[1], w2[1], "1")
    pv3 = pv_rows(1.0 + g_m[1], ln_mlp_g[1], ln_mlp_b[1])

    dx3, dy3, s3 = _ln_bwd(x3, y3, pv3, "ln_mlp1_bwd", target=target)
    loss = lax.psum(s3[5, 0], ("x", "y", "c"))
    du3, dw1_1, dw2_1 = _mlp_bwd(dy3, u3, h1, a1, w1[1].T, w2[1].T, "1")
    dx2, dy2, s2 = _ln_bwd(x2, y2, pv2, "ln_mix1_bwd", dxo=dx3, du=du3)
    du2, d_ssm_in, d_ssm_out, ssm_small = _ssd_bwd(dy2, u2, sw, cw8, cb8, par8, nw8, sres)
    dx1, dy1, s1 = _ln_bwd(x1, y1, pv1, "ln_mlp0_bwd", dxo=dx2, du=du2)
    du1, dw1_0, dw2_0 = _mlp_bwd(dy1, u1, h0, a0, w1[0].T, w2[0].T, "0")
    dx0, dy0, s0 = _ln_bwd(xs, y0, pv0, "ln_mix0_bwd", dxo=dx1, du=du1)
    early = [(True, _col_slots(jnp.stack([dw1_0, dw1_1]))), (True, _row_slots(jnp.stack([dw2_0, dw2_1]))),
             (True, _col_slots(d_ssm_in[None])), (True, _row_slots(d_ssm_out[None])),
             (True, _col_slots(ssm_small["conv_w"][None])), (True, _col_slots(ssm_small["conv_b"][None])),
             (True, _col_slots(ssm_small["norm_w"][None]))]
    du0, d_fox_in, d_fox_o, fox_sums, exchanged = _fox_bwd(dy0, u0, fw, bf8, fres, early)
    grad_x, sx = _mod_bwd(dx0, du0, xs, pv_rows(1.0 + sc_a[0], sh_a[0]), "modulate0_bwd")

    dmod = jnp.stack([
        jnp.stack([sx[1], sx[0], s0[4], s0[1], s0[0], s1[4]]),
        jnp.stack([s1[1], s1[0], s2[4], s2[1], s2[0], s3[4]]),
    ]).reshape(DEPTH, 6 * d)

    def pad_rows(v):
        v = v.reshape(-1, LANES) if v.size % LANES == 0 else jnp.pad(v.reshape(-1), (0, LANES - v.size)).reshape(1, LANES)
        return jnp.pad(v, ((0, (-v.shape[0]) % 8), (0, 0)))

    small_parts = [dmod, jnp.stack([s0[2], s2[2]]), jnp.stack([s0[3], s2[3]]), jnp.stack([s1[2], s3[2]]),
                   jnp.stack([s1[3], s3[3]]), fox_sums[0, :FOX_HEADS], ssm_small["dt_bias"], ssm_small["a_log"],
                   ssm_small["d"]]
    packed = [pad_rows(p) for p in small_parts]
    offsets = np.cumsum([0] + [p.shape[0] for p in packed])
    small_all_g = _all_gather(jnp.concatenate(packed, axis=0), "gather_small_grads")

    def unpack(idx, shape):
        n = int(np.prod(shape))
        blk = small_all_g[:, offsets[idx]:offsets[idx + 1]].reshape(N_DEV, -1)[:, :n]
        return blk.reshape((N_DEV,) + tuple(shape))

    dmod_all = unpack(0, (DEPTH, 6 * d))
    dmod_cols = lax.dynamic_slice_in_dim(dmod_all, me * ncol, ncol, axis=2)
    dmod_cols = jnp.pad(jnp.moveaxis(dmod_cols, 0, 1), ((0, 0), (0, N_DEV), (0, 0)))
    g_ada_w = _ada_bwd(c_all, dmod_cols)

    shards = dict(
        mlp_w1=exchanged[0], mlp_w2=exchanged[1], ssm_w_in=exchanged[2], ssm_w_out=exchanged[3],
        ssm_conv_w=exchanged[4], ssm_conv_b=exchanged[5], ssm_norm_w=exchanged[6],
        fox_w_in=_scatter_cols(d_fox_in[None], "scatter_fox_in"), fox_w_o=_scatter_rows(d_fox_o[None], "scatter_fox_o"),
        ada_w=g_ada_w[None], ada_b=dmod_all,
        ln_mix_g=unpack(1, (DEPTH, d)), ln_mix_b=unpack(2, (DEPTH, d)),
        ln_mlp_g=unpack(3, (DEPTH, d)), ln_mlp_b=unpack(4, (DEPTH, d)),
        fox_b_f=unpack(5, (1, FOX_HEADS)), ssm_dt_bias=unpack(6, (1, SSM_HEADS)),
        ssm_a_log=unpack(7, (1, SSM_HEADS)), ssm_d=unpack(8, (1, SSM_HEADS)),
    )
    weights = dict(ada_w=ada_w, ada_b=ada_b, ln_mix_g=ln_mix_g, ln_mix_b=ln_mix_b, ln_mlp_g=ln_mlp_g, ln_mlp_b=ln_mlp_b,
                   mlp_w1=mlp_w1, mlp_w2=mlp_w2, fox_w_in=fox_w_in, fox_b_f=fox_b_f, fox_w_o=fox_w_o, ssm_w_in=ssm_w_in,
                   ssm_conv_w=ssm_conv_w, ssm_conv_b=ssm_conv_b, ssm_dt_bias=ssm_dt_bias, ssm_a_log=ssm_a_log,
                   ssm_d=ssm_d, ssm_norm_w=ssm_norm_w, ssm_w_out=ssm_w_out)
    mom1 = dict(ada_w=m_ada_w, ada_b=m_ada_b, ln_mix_g=m_ln_mix_g, ln_mix_b=m_ln_mix_b, ln_mlp_g=m_ln_mlp_g,
                ln_mlp_b=m_ln_mlp_b, mlp_w1=m_mlp_w1, mlp_w2=m_mlp_w2, fox_w_in=m_fox_w_in, fox_b_f=m_fox_b_f,
                fox_w_o=m_fox_w_o, ssm_w_in=m_ssm_w_in, ssm_conv_w=m_ssm_conv_w, ssm_conv_b=m_ssm_conv_b,
                ssm_dt_bias=m_ssm_dt_bias, ssm_a_log=m_ssm_a_log, ssm_d=m_ssm_d, ssm_norm_w=m_ssm_norm_w,
                ssm_w_out=m_ssm_w_out)
    mom2 = dict(ada_w=v_ada_w, ada_b=v_ada_b, ln_mix_g=v_ln_mix_g, ln_mix_b=v_ln_mix_b, ln_mlp_g=v_ln_mlp_g,
                ln_mlp_b=v_ln_mlp_b, mlp_w1=v_mlp_w1, mlp_w2=v_mlp_w2, fox_w_in=v_fox_w_in, fox_b_f=v_fox_b_f,
                fox_w_o=v_fox_w_o, ssm_w_in=v_ssm_w_in, ssm_conv_w=v_ssm_conv_w, ssm_conv_b=v_ssm_conv_b,
                ssm_dt_bias=v_ssm_dt_bias, ssm_a_log=v_ssm_a_log, ssm_d=v_ssm_d, ssm_norm_w=v_ssm_norm_w,
                ssm_w_out=v_ssm_w_out)
    names = list(weights)
    stepped = {n: _adamw_any(shards[n], weights[n], mom1[n], mom2[n], f"adamw_{n}") for n in names}
    return (loss, grad_x[None], *[stepped[n][0] for n in names], *[stepped[n][1] for n in names],
            *[stepped[n][2] for n in names], *[stepped[n][3] for n in names])
```

```python
import numpy as np
import jax
import jax.numpy as jnp
from jax import lax
from jax.experimental import pallas as pl
from jax.experimental.pallas import tpu as pltpu

F32 = jnp.float32
MXU_DTYPE = jnp.bfloat16
HIGHEST = lax.Precision.HIGHEST

N_DEV = 8
D_MODEL = 1024
DEPTH = 2
FOX_HEADS = 16
FOX_HEAD_DIM = 64
D_FF = 4096
SSM_D_INNER = 2048
SSM_HEADS = 32
SSM_GROUPS = 8
SSM_STATE = 128
SSM_CHUNK = 128
SSM_CONV = 4
SSM_CONV_DIM = 4096
GROUP_W = SSM_D_INNER // SSM_GROUPS
LN_EPS = 1e-5
RMS_EPS = 1e-5
ALPHA = (2.0 * DEPTH) ** 0.25
LANES = 128
SUBLANES = 8

ADAM_LR = 0.001
ADAM_B1 = 0.9
ADAM_B2 = 0.999
ADAM_EPS = 1e-08
ADAM_WD = 0.01
ADAM_STEP = 10

NN = (((1,), (0,)), ((), ()))
NT = (((1,), (1,)), ((), ()))
TN = (((0,), (0,)), ((), ()))

VMEM_BIG = 56 * 1024 * 1024
MM_ROWS = 2048
MM_DEPTH = 2048


def _dot(a, b, dims=NN, precision=None):
    return lax.dot_general(a, b, dims, precision=precision, preferred_element_type=F32)


def _mx(v):
    return v.astype(MXU_DTYPE)


def _pieces3(v):
    hi = _mx(v)
    r1 = v - hi.astype(F32)
    mid = _mx(r1)
    return hi, mid, _mx(r1 - mid.astype(F32))


def _dot_onehot(a, b, dims=NN, onehot="b"):
    if onehot == "b":
        return sum(_dot(p, _mx(b), dims) for p in _pieces3(a))
    return sum(_dot(_mx(a), p, dims) for p in _pieces3(b))


def _params(vmem=None):
    return pltpu.CompilerParams(vmem_limit_bytes=vmem) if vmem else None


def _all_gather(x, name):
    def body(x_ref, out_ref, send_sems, recv_sems, local_sem):
        xi, yi, ci = lax.axis_index("x"), lax.axis_index("y"), lax.axis_index("c")
        me, sibling = (xi, yi, ci), (xi, yi, 1 - ci)
        chips = [(1 - xi, yi), (xi, 1 - yi), (1 - xi, 1 - yi)]

        def slot(px, py, pc):
            return out_ref.at[4 * px + 2 * py + pc]

        def copy(k, block, to, src=None):
            return pltpu.make_async_remote_copy(
                src_ref=slot(*block) if src is None else src, dst_ref=slot(*block),
                send_sem=send_sems.at[k], recv_sem=recv_sems.at[k],
                device_id=to, device_id_type=pl.DeviceIdType.MESH)

        mine = pltpu.make_async_copy(x_ref, slot(*me), local_sem)
        mine.start()
        first = [copy(0, me, sibling, src=x_ref)]
        first += [copy(1 + j, me, (*chip, ci), src=x_ref) for j, chip in enumerate(chips)]
        for cp in first:
            cp.start()
        passed = [copy(4 + j, (*chip, ci), sibling) for j, chip in enumerate(chips)]
        for j, chip in enumerate(chips):
            copy(1 + j, (*chip, ci), me).wait_recv()
            passed[j].start()
        copy(0, sibling, me).wait_recv()
        for j, chip in enumerate(chips):
            copy(4 + j, (*chip, 1 - ci), me).wait_recv()
        for cp in first + passed:
            cp.wait_send()
        mine.wait()

    return pl.pallas_call(
        body, name=name,
        out_shape=jax.ShapeDtypeStruct((N_DEV,) + x.shape, x.dtype),
        in_specs=[pl.BlockSpec(memory_space=pl.ANY)],
        out_specs=pl.BlockSpec(memory_space=pl.ANY),
        scratch_shapes=[pltpu.SemaphoreType.DMA((7,)), pltpu.SemaphoreType.DMA((7,)),
                        pltpu.SemaphoreType.DMA],
    )(x)


EXCHANGE_PIECES = 1


def _direct_copies(scatter, x_ref, out_ref, send_sems, recv_sems, local_sems, n, piece=None):
    xi, yi, ci = lax.axis_index("x"), lax.axis_index("y"), lax.axis_index("c")
    me = 4 * xi + 2 * yi + ci

    def part(ref):
        return ref if piece is None else ref.at[piece]

    local = pltpu.make_async_copy(part(x_ref.at[me] if scatter else x_ref), part(out_ref.at[me]), local_sems.at[n])
    remote = []
    for k in range(1, N_DEV):
        px = 1 - xi if k & 4 else xi
        py = 1 - yi if k & 2 else yi
        pc = 1 - ci if k & 1 else ci
        remote.append(pltpu.make_async_remote_copy(
            src_ref=part(x_ref.at[4 * px + 2 * py + pc] if scatter else x_ref), dst_ref=part(out_ref.at[me]),
            send_sem=send_sems.at[7 * n + k - 1], recv_sem=recv_sems.at[7 * n + k - 1],
            device_id=(px, py, pc), device_id_type=pl.DeviceIdType.MESH))
    return local, remote


def _in_pieces(carry, steps):
    out = []
    for scatter, a in carry:
        body = a.shape[1:] if scatter else a.shape
        rows = int(np.prod(body[:-1])) if len(body) > 1 else 1
        align = SUBLANES * (4 // a.dtype.itemsize)
        pieces = 1
        while (pieces * 2 <= min(EXCHANGE_PIECES, steps // 2) and rows % (pieces * 2 * align) == 0):
            pieces *= 2
        shape = (pieces, rows // pieces, body[-1])
        out.append((scatter, a.reshape(((N_DEV,) if scatter else ()) + shape), pieces, max(1, (steps // 2) // pieces)))
    return out


def _exchange_shapes(carry):
    return [jax.ShapeDtypeStruct(c[1].shape if c[0] else (N_DEV,) + c[1].shape, c[1].dtype) for c in carry]


def _exchange_sems(carry):
    n = max(len(carry), 1)
    return [pltpu.SemaphoreType.DMA((7 * n,)), pltpu.SemaphoreType.DMA((7 * n,)), pltpu.SemaphoreType.DMA((n,))]


def _exchange_start(carry, srcs, dsts, sems, step=None):
    for n, entry in enumerate(carry):
        scatter = entry[0]
        pieces, stride = (entry[2], entry[3]) if len(entry) > 2 else (1, 1)
        local, remote = _direct_copies(scatter, srcs[n], dsts[n], *sems, n)
        if step is None:
            local.start()
            for cp in remote:
                cp.start()
            continue

        @pl.when(step == 0)
        def _():
            local.start()
            if pieces == 1:
                for cp in remote:
                    cp.start()

        if pieces > 1:
            @pl.when((step % stride == 0) & (step // stride < pieces))
            def _():
                for cp in _direct_copies(scatter, srcs[n], dsts[n], *sems, n, piece=step // stride)[1]:
                    cp.start()


def _exchange_wait(carry, srcs, dsts, sems):
    for n, entry in enumerate(carry):
        local, remote = _direct_copies(entry[0], srcs[n], dsts[n], *sems, n)
        for cp in remote:
            cp.wait()
        local.wait()


def _carrying(body, n_in, n_out, n_scratch, carry, grid):
    nc = len(carry)

    def wrapped(*refs):
        ins, srcs = refs[:n_in], refs[n_in:n_in + nc]
        outs = refs[n_in + nc:n_in + nc + n_out]
        dsts = refs[n_in + nc + n_out:n_in + 2 * nc + n_out]
        scratch = refs[n_in + 2 * nc + n_out:n_in + 2 * nc + n_out + n_scratch]
        sems = refs[n_in + 2 * nc + n_out + n_scratch:]
        step = 0
        for axis, extent in enumerate(grid):
            step = step * extent + pl.program_id(axis)
        _exchange_start(carry, srcs, dsts, sems, step=step)
        body(*ins, *outs, *scratch)

        @pl.when(step == int(np.prod(grid)) - 1)
        def _():
            _exchange_wait(carry, srcs, dsts, sems)

    return wrapped if nc else body


def _all_to_all(x, name):
    carry = [(True, x)]

    def body(x_ref, out_ref, *sems):
        _exchange_start(carry, [x_ref], [out_ref], sems)
        _exchange_wait(carry, [x_ref], [out_ref], sems)

    return pl.pallas_call(
        body, name=name,
        out_shape=jax.ShapeDtypeStruct(x.shape, x.dtype),
        in_specs=[pl.BlockSpec(memory_space=pl.ANY)],
        out_specs=pl.BlockSpec(memory_space=pl.ANY),
        scratch_shapes=_exchange_sems(carry),
    )(x)


def _mm(name, a, b, *, grid, a_spec, b_spec, dims, k_axis, outs, acc=None, extras=(), epi=None, vmem=None):
    nk = grid[k_axis]
    n_ex, n_out = len(extras), len(outs)

    def body(*refs):
        a_ref, b_ref = refs[0], refs[1]
        ex = refs[2:2 + n_ex]
        out = refs[2 + n_ex:2 + n_ex + n_out]

        def finish(val):
            if epi is None:
                out[0][...] = val.astype(out[0].dtype)
            else:
                epi(val, ex, out)

        part = _dot(a_ref[...], b_ref[...], dims)
        if nk == 1:
            finish(part)
        else:
            acc_ref = refs[2 + n_ex + n_out]
            k = pl.program_id(k_axis)

            @pl.when(k == 0)
            def _():
                acc_ref[...] = part

            @pl.when(k > 0)
            def _():
                acc_ref[...] += part

            @pl.when(k == nk - 1)
            def _():
                finish(acc_ref[...])

    return pl.pallas_call(
        body, name=name, grid=grid,
        in_specs=[a_spec, b_spec] + [s for _, s in extras],
        out_specs=[s for _, s in outs],
        out_shape=[o for o, _ in outs],
        scratch_shapes=[pltpu.VMEM(acc, F32)] if nk > 1 else [],
        compiler_params=_params(vmem),
    )(a, b, *[e for e, _ in extras])


def _tile(n, t):
    t = min(n, t)
    assert n % t == 0, (n, t)
    return t


def _mm_nn(name, a, b, out_dtype, *, addend=None, tm=MM_ROWS, tn=1024, tk=1024, epi=None, extras=(), outs=None):
    m, kk = a.shape
    n = b.shape[1]
    if addend is not None:
        tm = tm // 2
    tm, tn, tk = _tile(m, tm), _tile(n, tn), _tile(kk, tk)
    o_spec = pl.BlockSpec((tm, tn), lambda i, j, k: (i, j))
    if outs is None:
        outs = [(jax.ShapeDtypeStruct((m, n), out_dtype), o_spec)]
    extras = list(extras)
    if addend is not None:
        extras = [(addend, o_spec)] + extras

        def epi(val, ex, out):
            out[0][...] = (val + ex[0][...].astype(F32)).astype(out[0].dtype)

    res = _mm(name, a, b, grid=(m // tm, n // tn, kk // tk),
              a_spec=pl.BlockSpec((tm, tk), lambda i, j, k: (i, k)),
              b_spec=pl.BlockSpec((tk, tn), lambda i, j, k: (k, j)),
              dims=NN, k_axis=2, acc=(tm, tn), outs=outs, extras=extras, epi=epi, vmem=VMEM_BIG)
    return res[0] if len(res) == 1 else res


def _mm_tn(name, a, b, out_dtype=F32, *, tm=1024, tn=1024, tk=MM_DEPTH):
    kk, m = a.shape
    n = b.shape[1]
    tm, tn, tk = _tile(m, tm), _tile(n, tn), _tile(kk, tk)
    res = _mm(name, a, b, grid=(m // tm, n // tn, kk // tk),
              a_spec=pl.BlockSpec((tk, tm), lambda i, j, k: (k, i)),
              b_spec=pl.BlockSpec((tk, tn), lambda i, j, k: (k, j)),
              dims=TN, k_axis=2, acc=(tm, tn),
              outs=[(jax.ShapeDtypeStruct((m, n), out_dtype), pl.BlockSpec((tm, tn), lambda i, j, k: (i, j)))],
              vmem=VMEM_BIG)
    return res[0]


def _row_block(s):
    return _tile(s, 512)


def _modulate(x, pv, name):
    s, d = x.shape
    tb = _row_block(s)

    def body(x_ref, pv_ref, u_ref):
        u_ref[...] = _mx(x_ref[...] * pv_ref[0:1, :] + pv_ref[1:2, :])

    return pl.pallas_call(
        body, name=name, grid=(s // tb,),
        in_specs=[pl.BlockSpec((tb, d), lambda i: (i, 0)), pl.BlockSpec((8, d), lambda i: (0, 0))],
        out_specs=pl.BlockSpec((tb, d), lambda i: (i, 0)),
        out_shape=jax.ShapeDtypeStruct((s, d), MXU_DTYPE),
    )(x, pv)


def _ln_stats(r):
    mu = jnp.mean(r, axis=-1, keepdims=True)
    xc = r - mu
    var = jnp.mean(xc * xc, axis=-1, keepdims=True)
    rstd = lax.rsqrt(var + LN_EPS)
    return xc * rstd, rstd


def _ln_fwd(xin, y, pv, name):
    s, d = xin.shape
    tb = _row_block(s)

    def body(x_ref, y_ref, pv_ref, xo_ref, u_ref):
        r = ALPHA * x_ref[...] + pv_ref[0:1, :] * y_ref[...]
        xhat, _ = _ln_stats(r)
        xo = xhat * pv_ref[1:2, :] + pv_ref[2:3, :]
        xo_ref[...] = xo
        u_ref[...] = _mx(xo * pv_ref[3:4, :] + pv_ref[4:5, :])

    row = pl.BlockSpec((tb, d), lambda i: (i, 0))
    return pl.pallas_call(
        body, name=name, grid=(s // tb,),
        in_specs=[row, row, pl.BlockSpec((8, d), lambda i: (0, 0))],
        out_specs=[row, row],
        out_shape=[jax.ShapeDtypeStruct((s, d), F32), jax.ShapeDtypeStruct((s, d), MXU_DTYPE)],
    )(xin, y, pv)


def _ln_bwd(xin, y, pv, name, *, dxo=None, du=None, target=None):
    s, d = xin.shape
    tb = _row_block(s)
    nb = s // tb
    loss_mode = target is not None

    def body(*refs):
        if loss_mode:
            x_ref, y_ref, pv_ref, t_ref, dxin_ref, dy_ref, sums_ref = refs
        else:
            x_ref, y_ref, pv_ref, dxo_ref, du_ref, dxin_ref, dy_ref, sums_ref = refs
        i = pl.program_id(0)

        @pl.when(i == 0)
        def _():
            sums_ref[...] = jnp.zeros_like(sums_ref)

        yv = y_ref[...]
        r = ALPHA * x_ref[...] + pv_ref[0:1, :] * yv
        xhat, rstd = _ln_stats(r)
        xo = xhat * pv_ref[1:2, :] + pv_ref[2:3, :]
        if loss_mode:
            diff = xo - t_ref[...]
            dxo_v = diff * (1.0 / d)
            sums_ref[5:6, :] += jnp.sum(diff * diff, axis=0, keepdims=True) * (0.5 / d)
        else:
            duv = du_ref[...]
            dxo_v = dxo_ref[...] + duv * pv_ref[3:4, :]
            sums_ref[0:1, :] += jnp.sum(duv * xo, axis=0, keepdims=True)
            sums_ref[1:2, :] += jnp.sum(duv, axis=0, keepdims=True)
        sums_ref[2:3, :] += jnp.sum(dxo_v * xhat, axis=0, keepdims=True)
        sums_ref[3:4, :] += jnp.sum(dxo_v, axis=0, keepdims=True)
        dxh = dxo_v * pv_ref[1:2, :]
        dr = rstd * (dxh - jnp.mean(dxh, axis=-1, keepdims=True)
                     - xhat * jnp.mean(dxh * xhat, axis=-1, keepdims=True))
        sums_ref[4:5, :] += jnp.sum(dr * yv, axis=0, keepdims=True)
        dxin_ref[...] = ALPHA * dr
        dy_ref[...] = _mx(pv_ref[0:1, :] * dr)
        if loss_mode:
            @pl.when(i == nb - 1)
            def _():
                sums_ref[5:6, :] = jnp.broadcast_to(jnp.sum(sums_ref[5:6, :], axis=-1, keepdims=True), (1, d))

    row = pl.BlockSpec((tb, d), lambda i: (i, 0))
    par = pl.BlockSpec((8, d), lambda i: (0, 0))
    ins = [xin, y, pv] + ([target] if loss_mode else [dxo, du])
    return pl.pallas_call(
        body, name=name, grid=(nb,),
        in_specs=[row, row, par] + [row] * (len(ins) - 3),
        out_specs=[row, row, par],
        out_shape=[jax.ShapeDtypeStruct((s, d), F32), jax.ShapeDtypeStruct((s, d), MXU_DTYPE),
                   jax.ShapeDtypeStruct((8, d), F32)],
    )(*ins)


def _mod_bwd(dx_direct, du, x, pv, name):
    s, d = x.shape
    tb = _row_block(s)

    def body(dxd_ref, du_ref, x_ref, pv_ref, dx_ref, sums_ref):
        @pl.when(pl.program_id(0) == 0)
        def _():
            sums_ref[...] = jnp.zeros_like(sums_ref)

        duv = du_ref[...]
        dx_ref[...] = dxd_ref[...] + duv * pv_ref[0:1, :]
        sums_ref[0:1, :] += jnp.sum(duv * x_ref[...], axis=0, keepdims=True)
        sums_ref[1:2, :] += jnp.sum(duv, axis=0, keepdims=True)

    row = pl.BlockSpec((tb, d), lambda i: (i, 0))
    par = pl.BlockSpec((8, d), lambda i: (0, 0))
    return pl.pallas_call(
        body, name=name, grid=(s // tb,),
        in_specs=[row, row, row, par], out_specs=[row, par],
        out_shape=[jax.ShapeDtypeStruct((s, d), F32), jax.ShapeDtypeStruct((8, d), F32)],
    )(dx_direct, du, x, pv)


def _mlp_fwd(u, w1, w2, tag):
    s = u.shape[0]

    def epi(val, ex, out):
        out[0][...] = _mx(val)
        out[1][...] = _mx(jnp.square(jnp.maximum(val, 0.0)))

    tm, tn = _tile(s, MM_ROWS), 1024
    spec = pl.BlockSpec((tm, tn), lambda i, j, k: (i, j))
    shp = jax.ShapeDtypeStruct((s, D_FF), MXU_DTYPE)
    h, a = _mm_nn(f"mlp_up{tag}", u, w1, None, epi=epi, outs=[(shp, spec), (shp, spec)], tn=tn)
    y = _mm_nn(f"mlp_down{tag}", a, w2, F32)
    return y, (h, a)


def _mlp_bwd(dy, u, h, a, w1t, w2t, tag):
    s = u.shape[0]
    tm, tn = _tile(s, MM_ROWS), 1024
    spec = pl.BlockSpec((tm, tn), lambda i, j, k: (i, j))

    def epi(val, ex, out):
        out[0][...] = _mx(val * (2.0 * jnp.maximum(ex[0][...].astype(F32), 0.0)))

    dh = _mm_nn(f"mlp_dh{tag}", dy, w2t, None, epi=epi, extras=[(h, spec)],
                outs=[(jax.ShapeDtypeStruct((s, D_FF), MXU_DTYPE), spec)], tn=tn)
    du = _mm_nn(f"mlp_du{tag}", dh, w1t, F32)
    dw2 = _mm_tn(f"mlp_dw2{tag}", a, dy)
    dw1 = _mm_tn(f"mlp_dw1{tag}", u, dh)
    return du, dw1, dw2


FOX_T = 1024
BIAS_Q = (64, 65, 66)
BIAS_K = (67, 68, 69)
SKIP_MARGIN = 110.0
ONES_V = 64

def _fox_constants():
    selq = np.zeros((FOX_HEADS, 512, LANES), np.float32)
    selk = np.zeros((FOX_HEADS, 512, LANES), np.float32)
    selv = np.zeros((2, LANES, LANES), np.float32)
    put = np.zeros((2, 2, LANES, LANES), np.float32)
    for h in range(FOX_HEADS):
        off = FOX_HEAD_DIM * (h % 2)
        for dd in range(FOX_HEAD_DIM):
            selq[h, off + dd, dd] = FOX_HEAD_DIM ** -0.5
            selk[h, off + dd, dd] = 1.0
        for piece in range(3):
            selq[h, LANES * (1 + piece) + h, BIAS_Q[piece]] = 1.0
            selk[h, LANES * (1 + piece) + h, BIAS_K[piece]] = -1.0
    for par in range(2):
        for dd in range(FOX_HEAD_DIM):
            selv[par, FOX_HEAD_DIM * par + dd, dd] = 1.0
            put[par, 0, dd, FOX_HEAD_DIM * par + dd] = FOX_HEAD_DIM ** -0.5
            put[par, 1, dd, FOX_HEAD_DIM * par + dd] = 1.0
    return selq, selk, selv, put


def _fox_prep(qkv, f, bf, carry=()):
    s = qkv.shape[0]
    t = _tile(s, FOX_T)
    nb = s // t
    selq, selk, selv, _ = _fox_constants()
    whole = _exchange_shapes(list(carry))
    carry = _in_pieces(list(carry), nb * FOX_HEADS)
    anywhere = pl.BlockSpec(memory_space=pl.ANY)

    def body(q_ref, k_ref, v_ref, f_ref, bf_ref, selq_ref, selk_ref, selv_ref,
             qa_ref, qat_ref, ka_ref, kat_ref, va_ref, vat_ref, stats_ref, parts_ref, carry_ref, cum_ref):
        i, h = pl.program_id(0), pl.program_id(1)
        lane = lax.broadcasted_iota(jnp.int32, (1, LANES), 1)

        @pl.when(h == 0)
        def _():
            @pl.when(i == 0)
            def _():
                carry_ref[...] = jnp.zeros_like(carry_ref)

            lf = jnp.where(lane < FOX_HEADS, jax.nn.log_sigmoid(f_ref[...] + bf_ref[0:1, :]), 0.0)
            tri = (lax.broadcasted_iota(jnp.int32, (t, t), 0) >= lax.broadcasted_iota(jnp.int32, (t, t), 1)).astype(F32)
            cum = _dot_onehot(tri, lf, onehot="a") + carry_ref[0:1, :]
            carry_ref[0:1, :] = cum[t - 1:t, :]
            cum_ref[...] = cum
            hi = _mx(cum)
            r1 = cum - hi.astype(F32)
            mid = _mx(r1)
            parts_ref[:, 0:LANES] = hi
            parts_ref[:, LANES:2 * LANES] = mid
            parts_ref[:, 2 * LANES:3 * LANES] = _mx(r1 - mid.astype(F32))

        parts = parts_ref[...]
        qa = _dot(jnp.concatenate([q_ref[...], parts], axis=1), selq_ref[...])
        qa = qa + jnp.where((lane >= BIAS_K[0]) & (lane <= BIAS_K[2]), 1.0, 0.0)
        ka = _dot(jnp.concatenate([k_ref[...], parts], axis=1), selk_ref[...])
        ka = ka + jnp.where((lane >= BIAS_Q[0]) & (lane <= BIAS_Q[2]), 1.0, 0.0)
        va = _dot(v_ref[...], selv_ref[...]) + jnp.where(lane == ONES_V, 1.0, 0.0)
        qa_ref[...] = _mx(qa)
        qat_ref[...] = _mx(qa.T)
        ka_ref[...] = _mx(ka)
        kat_ref[...] = _mx(ka.T)
        va_ref[...] = _mx(va)
        vat_ref[...] = _mx(va.T)

        def longest(rows_):
            sq = jnp.where(lane < FOX_HEAD_DIM, rows_ * rows_, 0.0)
            return jnp.sqrt(jnp.max(jnp.sum(sq, axis=1, keepdims=True), axis=0, keepdims=True))

        mine = lane == h
        cum = cum_ref[...]
        top = jnp.max(jnp.max(jnp.where(mine, cum, -jnp.inf), axis=1, keepdims=True), axis=0, keepdims=True)
        low = jnp.min(jnp.min(jnp.where(mine, cum, jnp.inf), axis=1, keepdims=True), axis=0, keepdims=True)
        row = lax.broadcasted_iota(jnp.int32, (8, LANES), 0)
        stats_ref[...] = jnp.where(row == 0, longest(qa), jnp.where(row == 1, longest(ka),
                                                                    jnp.where(row == 2, top, low)))

    rows = jax.ShapeDtypeStruct((FOX_HEADS, nb, t, LANES), MXU_DTYPE)
    cols = jax.ShapeDtypeStruct((FOX_HEADS, nb, LANES, t), MXU_DTYPE)
    rspec = pl.BlockSpec((None, None, t, LANES), lambda i, h: (h, i, 0, 0))
    cspec = pl.BlockSpec((None, None, LANES, t), lambda i, h: (h, i, 0, 0))
    npair = FOX_HEADS // 2
    res = pl.pallas_call(
        _carrying(body, 8, 7, 3, carry, (nb, FOX_HEADS)), name="fox_prep", grid=(nb, FOX_HEADS),
        in_specs=[pl.BlockSpec((t, LANES), lambda i, h: (i, h // 2)),
                  pl.BlockSpec((t, LANES), lambda i, h: (i, npair + h // 2)),
                  pl.BlockSpec((t, LANES), lambda i, h: (i, 2 * npair + h // 2)),
                  pl.BlockSpec((t, LANES), lambda i, h: (i, 0)),
                  pl.BlockSpec((8, LANES), lambda i, h: (0, 0)),
                  pl.BlockSpec((None, 512, LANES), lambda i, h: (h, 0, 0)),
                  pl.BlockSpec((None, 512, LANES), lambda i, h: (h, 0, 0)),
                  pl.BlockSpec((None, LANES, LANES), lambda i, h: (h % 2, 0, 0))] + [anywhere] * len(carry),
        out_specs=[rspec, cspec, rspec, cspec, rspec, cspec,
                   pl.BlockSpec((None, None, 8, LANES), lambda i, h: (h, i, 0, 0))] + [anywhere] * len(carry),
        out_shape=[rows, cols, rows, cols, rows, cols, jax.ShapeDtypeStruct((FOX_HEADS, nb, 8, LANES), F32)]
        + _exchange_shapes(carry),
        scratch_shapes=[pltpu.VMEM((t, 3 * LANES), MXU_DTYPE), pltpu.VMEM((8, LANES), F32),
                        pltpu.VMEM((t, LANES), F32)] + (_exchange_sems(carry) if carry else []),
        compiler_params=_params(VMEM_BIG),
    )(qkv, qkv, qkv, f, bf, _mx(jnp.asarray(selq)), _mx(jnp.asarray(selk)), _mx(jnp.asarray(selv)),
      *[c[1] for c in carry])
    return (*res[:7], [r.reshape(w.shape) for r, w in zip(res[7:], whole)])


def _fox_active(stats):
    qn, kn, top, low = (stats[:, :, r, 0] for r in range(4))
    nb = qn.shape[1]
    gap = qn[:, :, None] * kn[:, None, :] + top[:, :, None] - low[:, None, :] + (qn * kn)[:, :, None]
    keep = (gap > -SKIP_MARGIN) | jnp.eye(nb, dtype=bool)[None]
    return jnp.where(keep, 1.0, 0.0).astype(F32).reshape(qn.shape[0], nb * nb)


def _causal_allow(t):
    return lax.broadcasted_iota(jnp.int32, (t, t), 0) <= lax.broadcasted_iota(jnp.int32, (t, t), 1)


def _fox_attn_fwd(qat, ka, vat, active, carry=()):
    heads, nb, _, t = qat.shape
    whole = _exchange_shapes(list(carry))
    carry = _in_pieces(list(carry), heads * nb)
    nc = len(carry)

    def body(act_ref, qat_ref, ka_ref, vat_ref, *rest):
        srcs, (ot_ref, lse_ref), dsts = rest[:nc], rest[nc:nc + 2], rest[nc + 2:2 * nc + 2]
        acc_ref, m_ref, kbuf_ref, vbuf_ref, fsems = rest[2 * nc + 2:2 * nc + 7]
        sems = rest[2 * nc + 7:]
        h, i = pl.program_id(0), pl.program_id(1)

        if nc:
            _exchange_start(carry, srcs, dsts, sems, step=h * nb + i)
        m_ref[...] = jnp.full_like(m_ref, -jnp.inf)
        acc_ref[...] = jnp.zeros_like(acc_ref)

        def key_blocks(j, slot):
            return [pltpu.make_async_copy(ka_ref.at[h, j], kbuf_ref.at[slot], fsems.at[0, slot]),
                    pltpu.make_async_copy(vat_ref.at[h, j], vbuf_ref.at[slot], fsems.at[1, slot])]

        def runs(j):
            return (j == i) | (act_ref[h, i * nb + j] > 0.5)

        def step(slot, diagonal):
            st = _dot(kbuf_ref[slot], qat_ref[...])
            if diagonal:
                st = jnp.where(_causal_allow(t), st, -jnp.inf)
            m_old = m_ref[...]
            m_new = jnp.maximum(m_old, jnp.max(st, axis=0, keepdims=True))
            pt = jnp.exp(st - m_new)
            acc_ref[...] = acc_ref[...] * jnp.exp(m_old - m_new) + _dot(vbuf_ref[slot], _mx(pt))
            m_ref[...] = m_new

        @pl.when(runs(0))
        def _():
            for cp in key_blocks(0, 0):
                cp.start()

        def earlier(j, c):
            slot = j % 2

            @pl.when(runs(j + 1))
            def _():
                for cp in key_blocks(j + 1, 1 - slot):
                    cp.start()

            @pl.when(runs(j))
            def _():
                for cp in key_blocks(j, slot):
                    cp.wait()
                step(slot, False)

            return c

        lax.fori_loop(0, i, earlier, 0)
        for cp in key_blocks(i, i % 2):
            cp.wait()
        step(i % 2, True)
        acc = acc_ref[...]
        denom = acc[ONES_V:ONES_V + 1, :]
        ot_ref[...] = _mx(acc / denom)
        lse_ref[...] = m_ref[...] + jnp.log(denom)

        if nc:
            @pl.when((h == heads - 1) & (i == nb - 1))
            def _():
                _exchange_wait(carry, srcs, dsts, sems)

    anywhere = pl.BlockSpec(memory_space=pl.ANY)
    qspec = pl.BlockSpec((None, None, LANES, t), lambda h, i: (h, i, 0, 0))
    res = pl.pallas_call(
        body, name="fox_attn_fwd", grid=(heads, nb),
        in_specs=[pl.BlockSpec(memory_space=pltpu.SMEM), qspec, anywhere, anywhere] + [anywhere] * nc,
        out_specs=[qspec, pl.BlockSpec((None, None, 1, t), lambda h, i: (h, i, 0, 0))] + [anywhere] * nc,
        out_shape=[jax.ShapeDtypeStruct((heads, nb, LANES, t), MXU_DTYPE),
                   jax.ShapeDtypeStruct((heads, nb, 1, t), F32)] + _exchange_shapes(carry),
        scratch_shapes=[pltpu.VMEM((LANES, t), F32), pltpu.VMEM((1, t), F32), pltpu.VMEM((2, t, LANES), MXU_DTYPE),
                        pltpu.VMEM((2, LANES, t), MXU_DTYPE), pltpu.SemaphoreType.DMA((2, 2))]
        + (_exchange_sems(carry) if nc else []),
        compiler_params=_params(VMEM_BIG),
    )(active, qat, ka, vat, *[c[1] for c in carry])
    return res[0], res[1], [r.reshape(w.shape) for r, w in zip(res[2:], whole)]


def _fox_attn_bwd(qa, qat, ka, kat, va, ot, lse, do, dot_, active, carry=()):
    heads, nb, t, _ = qa.shape
    whole = _exchange_shapes(list(carry))
    carry = _in_pieces(list(carry), heads * nb)
    nc = len(carry)

    def body(act_ref, qa_ref, qat_ref, ka_ref, kat_ref, va_ref, ot_ref, lse_ref, do_ref, dot_ref, *rest):
        srcs, (dqt_ref, dka_ref, dva_ref), dsts = rest[:nc], rest[nc:nc + 3], rest[nc + 3:2 * nc + 3]
        rows_ref, cols_ref, lseb_ref, fsems = rest[2 * nc + 3:2 * nc + 7]
        sems = rest[2 * nc + 7:]
        h, j = pl.program_id(0), pl.program_id(1)

        if nc:
            _exchange_start(carry, srcs, dsts, sems, step=h * nb + j)

        @pl.when(j == 0)
        def _():
            dqt_ref[...] = jnp.zeros_like(dqt_ref)

        def query_blocks(i, slot):
            cps = [pltpu.make_async_copy(src.at[h, i], rows_ref.at[slot, n], fsems.at[n, slot])
                   for n, src in enumerate((qa_ref, do_ref))]
            cps += [pltpu.make_async_copy(src.at[h, i], cols_ref.at[slot, n], fsems.at[2 + n, slot])
                    for n, src in enumerate((qat_ref, ot_ref, dot_ref))]
            return cps + [pltpu.make_async_copy(lse_ref.at[h, i], lseb_ref.at[slot], fsems.at[5, slot])]

        def runs(i):
            return (i == j) | (act_ref[h, i * nb + j] > 0.5)

        def step(i, slot, diagonal):
            st = _dot(ka_ref[...], cols_ref[slot, 0])
            dot_v = cols_ref[slot, 2]
            delta = jnp.sum(cols_ref[slot, 1].astype(F32) * dot_v.astype(F32), axis=0, keepdims=True)
            pt = jnp.exp(st - lseb_ref[slot])
            if diagonal:
                pt = jnp.where(_causal_allow(t), pt, 0.0)
            dsm = _mx(pt * (_dot(va_ref[...], dot_v) - delta))
            upd_v = _dot(_mx(pt), rows_ref[slot, 1])
            upd_k = _dot(dsm, rows_ref[slot, 0])
            if diagonal:
                dva_ref[...] = upd_v
                dka_ref[...] = upd_k
            else:
                dva_ref[...] += upd_v
                dka_ref[...] += upd_k
            dqt_ref[i] += _dot(kat_ref[...], dsm)

        def visit(i, slot, diagonal):
            nxt = jnp.minimum(i + 1, nb - 1)

            @pl.when((i + 1 < nb) & runs(nxt))
            def _():
                for cp in query_blocks(nxt, 1 - slot):
                    cp.start()

            @pl.when(runs(i))
            def _():
                for cp in query_blocks(i, slot):
                    cp.wait()
                step(i, slot, diagonal)

        for cp in query_blocks(j, 0):
            cp.start()
        visit(j, 0, True)

        def later(i, c):
            visit(i, (i - j) % 2, False)
            return c

        lax.fori_loop(j + 1, nb, later, 0)

        if nc:
            @pl.when((h == heads - 1) & (j == nb - 1))
            def _():
                _exchange_wait(carry, srcs, dsts, sems)

    def at_k(shape):
        return pl.BlockSpec((None, None) + shape, lambda h, j: (h, j, 0, 0))

    anywhere = pl.BlockSpec(memory_space=pl.ANY)
    res = pl.pallas_call(
        body, name="fox_attn_bwd", grid=(heads, nb),
        in_specs=[pl.BlockSpec(memory_space=pltpu.SMEM),
                  anywhere, anywhere, at_k((t, LANES)), at_k((LANES, t)), at_k((t, LANES)),
                  anywhere, anywhere, anywhere, anywhere] + [anywhere] * nc,
        out_specs=[pl.BlockSpec((None, nb, LANES, t), lambda h, j: (h, 0, 0, 0)),
                   at_k((t, LANES)), at_k((t, LANES))] + [anywhere] * nc,
        out_shape=[jax.ShapeDtypeStruct((heads, nb, LANES, t), F32),
                   jax.ShapeDtypeStruct((heads, nb, t, LANES), F32),
                   jax.ShapeDtypeStruct((heads, nb, t, LANES), F32)] + _exchange_shapes(carry),
        scratch_shapes=[pltpu.VMEM((2, 2, t, LANES), MXU_DTYPE), pltpu.VMEM((2, 3, LANES, t), MXU_DTYPE),
                        pltpu.VMEM((2, 1, t), F32), pltpu.SemaphoreType.DMA((6, 2))]
        + (_exchange_sems(carry) if nc else []),
        compiler_params=_params(VMEM_BIG),
    )(active, qa, qat, ka, kat, va, ot, lse, do, dot_, *[c[1] for c in carry])
    return res[0], res[1], res[2], [r.reshape(w.shape) for r, w in zip(res[3:], whole)]


def _fox_post(dqt, dka, dva, f, bf, carry=()):
    heads, nb, t, _ = dka.shape
    s = nb * t
    _, _, _, put = _fox_constants()
    whole = _exchange_shapes(list(carry))
    carry = _in_pieces(list(carry), nb * heads)
    anywhere = pl.BlockSpec(memory_space=pl.ANY)

    def body(dqt_ref, dka_ref, dva_ref, f_ref, bf_ref, put_ref, dq_ref, dk_ref, dv_ref, df_ref, sums_ref,
             dc_ref, carry_ref):
        i, h = pl.program_id(0), pl.program_id(1)

        @pl.when((i == 0) & (h == 0))
        def _():
            carry_ref[...] = jnp.zeros_like(carry_ref)
            sums_ref[...] = jnp.zeros_like(sums_ref)

        @pl.when(h == 0)
        def _():
            dc_ref[...] = jnp.zeros_like(dc_ref)

        dqt_v = dqt_ref[...]
        dka_v = dka_ref[...]
        term_q = _dot(_mx(dqt_v), put_ref[0], TN)
        term_k = _dot(_mx(dka_v), put_ref[1])
        term_v = _dot(_mx(dva_ref[...]), put_ref[1])

        @pl.when(h % 2 == 0)
        def _():
            dq_ref[...] = _mx(term_q)
            dk_ref[...] = _mx(term_k)
            dv_ref[...] = _mx(term_v)

        @pl.when(h % 2 == 1)
        def _():
            dq_ref[...] += _mx(term_q)
            dk_ref[...] += _mx(term_k)
            dv_ref[...] += _mx(term_v)

        dcum = dqt_v[BIAS_Q[0]:BIAS_Q[0] + 1, :] - dka_v.T[BIAS_K[0]:BIAS_K[0] + 1, :]
        head_row = lax.broadcasted_iota(jnp.int32, (heads, 1), 0) == h
        dc_ref[...] += jnp.where(head_row, dcum, 0.0)

        @pl.when(h == heads - 1)
        def _():
            later = (lax.broadcasted_iota(jnp.int32, (t, t), 0) >= lax.broadcasted_iota(jnp.int32, (t, t), 1)).astype(F32)
            dlf_t = _dot_onehot(dc_ref[...], later) + carry_ref[:, 0:1]
            carry_ref[...] = jnp.broadcast_to(dlf_t[:, 0:1], carry_ref.shape)
            dlf = jnp.concatenate([dlf_t, jnp.zeros((LANES - heads, t), F32)], axis=0).T
            lane = lax.broadcasted_iota(jnp.int32, (1, LANES), 1)
            df = jnp.where(lane < heads, dlf * jax.nn.sigmoid(-(f_ref[...] + bf_ref[0:1, :])), 0.0)
            df_ref[...] = _mx(df)
            sums_ref[0:1, :] += jnp.sum(df, axis=0, keepdims=True)

    rev = lambda i: nb - 1 - i
    pair_spec = pl.BlockSpec((t, LANES), lambda i, h: (rev(i), h // 2))
    blk = pl.BlockSpec((t, LANES), lambda i, h: (rev(i), 0))
    hd = jax.ShapeDtypeStruct((s, D_MODEL), MXU_DTYPE)
    res = pl.pallas_call(
        _carrying(body, 6, 5, 2, carry, (nb, heads)), name="fox_post", grid=(nb, heads),
        in_specs=[pl.BlockSpec((None, None, LANES, t), lambda i, h: (h, rev(i), 0, 0)),
                  pl.BlockSpec((None, None, t, LANES), lambda i, h: (h, rev(i), 0, 0)),
                  pl.BlockSpec((None, None, t, LANES), lambda i, h: (h, rev(i), 0, 0)),
                  blk, pl.BlockSpec((8, LANES), lambda i, h: (0, 0)),
                  pl.BlockSpec((None, 2, LANES, LANES), lambda i, h: (h % 2, 0, 0, 0))] + [anywhere] * len(carry),
        out_specs=[pair_spec, pair_spec, pair_spec, blk, pl.BlockSpec((8, LANES), lambda i, h: (0, 0))]
        + [anywhere] * len(carry),
        out_shape=[hd, hd, hd, jax.ShapeDtypeStruct((s, LANES), MXU_DTYPE), jax.ShapeDtypeStruct((8, LANES), F32)]
        + _exchange_shapes(carry),
        scratch_shapes=[pltpu.VMEM((heads, t), F32), pltpu.VMEM((heads, LANES), F32)]
        + (_exchange_sems(carry) if carry else []),
        compiler_params=_params(VMEM_BIG),
    )(dqt, dka, dva, f, bf, _mx(jnp.asarray(put)), *[c[1] for c in carry])
    return (*res[:5], [r.reshape(w.shape) for r, w in zip(res[5:], whole)])


def _fox_weights(w_in, w_o):
    wqkv = w_in[:, :3 * D_MODEL]
    wf = jnp.pad(w_in[:, 3 * D_MODEL:], ((0, 0), (0, LANES - FOX_HEADS)))
    wo_heads = w_o.reshape(FOX_HEADS, FOX_HEAD_DIM, D_MODEL)
    wo_a = jnp.pad(wo_heads, ((0, 0), (0, LANES - FOX_HEAD_DIM), (0, 0)))
    wo_rows = wo_a.reshape(FOX_HEADS * LANES, D_MODEL)
    return dict(wqkv=wqkv, wf=wf, wqkv_t=wqkv.T, wf_t=wf.T, wo_rows=wo_rows, wo_rows_t=wo_rows.T)


def _fox_out(ot, wo_rows):
    heads, nb, _, t = ot.shape

    def body(ot_ref, w_ref, y_ref):
        y_ref[...] = _dot(ot_ref[...].reshape(heads * LANES, t), w_ref[...], TN)

    return pl.pallas_call(
        body, name="fox_out", grid=(nb,),
        in_specs=[pl.BlockSpec((heads, None, LANES, t), lambda i: (0, i, 0, 0)),
                  pl.BlockSpec((heads * LANES, D_MODEL), lambda i: (0, 0))],
        out_specs=pl.BlockSpec((t, D_MODEL), lambda i: (i, 0)),
        out_shape=jax.ShapeDtypeStruct((nb * t, D_MODEL), F32),
        compiler_params=_params(VMEM_BIG),
    )(ot, wo_rows)


def _fox_do(dy, wo_rows_t, nb, t):
    heads = FOX_HEADS

    def body(dy_ref, w_ref, do_ref, dot_ref):
        val = _dot(dy_ref[...], w_ref[...])
        for h in range(heads):
            blk = val[:, h * LANES:(h + 1) * LANES]
            do_ref[h] = _mx(blk)
            dot_ref[h] = _mx(blk.T)

    return pl.pallas_call(
        body, name="fox_do", grid=(nb,),
        in_specs=[pl.BlockSpec((t, D_MODEL), lambda i: (i, 0)),
                  pl.BlockSpec((D_MODEL, heads * LANES), lambda i: (0, 0))],
        out_specs=[pl.BlockSpec((heads, None, t, LANES), lambda i: (0, i, 0, 0)),
                   pl.BlockSpec((heads, None, LANES, t), lambda i: (0, i, 0, 0))],
        out_shape=[jax.ShapeDtypeStruct((heads, nb, t, LANES), MXU_DTYPE),
                   jax.ShapeDtypeStruct((heads, nb, LANES, t), MXU_DTYPE)],
        compiler_params=_params(VMEM_BIG),
    )(dy, wo_rows_t)


def _fox_dwo(ot, dy):
    heads, nb, _, t = ot.shape

    def body(ot_ref, dy_ref, o_ref):
        part = _dot(ot_ref[...].reshape(heads * LANES, t), dy_ref[...])

        @pl.when(pl.program_id(0) == 0)
        def _():
            o_ref[...] = part

        @pl.when(pl.program_id(0) > 0)
        def _():
            o_ref[...] += part

    return pl.pallas_call(
        body, name="fox_dwo", grid=(nb,),
        in_specs=[pl.BlockSpec((heads, None, LANES, t), lambda i: (0, i, 0, 0)),
                  pl.BlockSpec((t, D_MODEL), lambda i: (i, 0))],
        out_specs=pl.BlockSpec((heads * LANES, D_MODEL), lambda i: (0, 0)),
        out_shape=jax.ShapeDtypeStruct((heads * LANES, D_MODEL), F32),
        compiler_params=_params(VMEM_BIG),
    )(ot, dy)


def _fox_fwd(u, w, bf, carry=()):
    qkv = _mm_nn("fox_qkv", u, w["wqkv"], MXU_DTYPE)
    f = _mm_nn("fox_f", u, w["wf"], F32)
    qa, qat, ka, kat, va, vat, stats, carried = _fox_prep(qkv, f, bf, carry)
    ot, lse, _ = _fox_attn_fwd(qat, ka, vat, _fox_active(stats))
    y = _fox_out(ot, w["wo_rows"])
    return y, dict(f=f, qa=qa, qat=qat, ka=ka, kat=kat, va=va, ot=ot, lse=lse, stats=stats), carried


def _fox_bwd(dy, u, w, bf, res, carry=()):
    heads, nb, t, _ = res["qa"].shape
    do, dot_ = _fox_do(dy, w["wo_rows_t"], nb, t)
    dwo_a = _fox_dwo(res["ot"], dy).reshape(heads, LANES, D_MODEL)
    dqt, dka, dva, _ = _fox_attn_bwd(res["qa"], res["qat"], res["ka"], res["kat"], res["va"], res["ot"],
                                     res["lse"], do, dot_, _fox_active(res["stats"]))
    dq, dk, dv, df, sums, carried = _fox_post(dqt, dka, dva, res["f"], bf, carry)
    wt = w["wqkv_t"]
    du = _mm_nn("fox_du_q", dq, wt[:D_MODEL], F32)
    du = _mm_nn("fox_du_k", dk, wt[D_MODEL:2 * D_MODEL], F32, addend=du)
    du = _mm_nn("fox_du_v", dv, wt[2 * D_MODEL:], F32, addend=du)
    du = _mm_nn("fox_du_f", df, w["wf_t"], F32, addend=du)
    dw_in = jnp.concatenate(
        [_mm_tn("fox_dw_q", u, dq), _mm_tn("fox_dw_k", u, dk), _mm_tn("fox_dw_v", u, dv),
         _mm_tn("fox_dw_f", u, df)[:, :FOX_HEADS]], axis=1)
    dw_o = dwo_a[:, :FOX_HEAD_DIM, :].reshape(D_MODEL, D_MODEL)
    return du, dw_in, dw_o, sums, carried


def _dsilu(v):
    sg = jax.nn.sigmoid(v)
    return sg * (1.0 + v * (1.0 - sg))


def _conv_taps(scr_ref, w_ref, rows, base):
    acc = None
    for k in range(SSM_CONV):
        term = scr_ref[pl.ds(base - (SSM_CONV - 1) + k, rows), :] * w_ref[k:k + 1, :]
        acc = term if acc is None else acc + term
    return acc


def _conv_fwd(zx, cw, cb):
    s = zx.shape[0]
    tb = _tile(s, 512)
    half = SSM_CONV_DIM // 2
    hb = tb // SUBLANES

    def body(x_ref, halo_ref, w_ref, b_ref, o_ref, scr_ref):
        i = pl.program_id(0)
        scr_ref[pl.ds(0, SUBLANES), :] = jnp.where(i > 0, halo_ref[...], 0.0)
        scr_ref[pl.ds(SUBLANES, tb), :] = x_ref[...]
        o_ref[...] = jax.nn.silu(_conv_taps(scr_ref, w_ref, tb, SUBLANES) + b_ref[0:1, :])

    return pl.pallas_call(
        body, name="ssd_conv_fwd", grid=(s // tb, 2),
        in_specs=[pl.BlockSpec((tb, half), lambda i, j: (i, 1 + j)),
                  pl.BlockSpec((SUBLANES, half), lambda i, j: (jnp.maximum(i * hb - 1, 0), 1 + j)),
                  pl.BlockSpec((8, half), lambda i, j: (0, j)),
                  pl.BlockSpec((8, half), lambda i, j: (0, j))],
        out_specs=pl.BlockSpec((tb, half), lambda i, j: (i, j)),
        out_shape=jax.ShapeDtypeStruct((s, SSM_CONV_DIM), F32),
        scratch_shapes=[pltpu.VMEM((tb + SUBLANES, half), F32)],
    )(zx, zx, cw, cb)


def _conv_bwd_pre(zx, dxc, cw, cb):
    s = zx.shape[0]
    tb = _tile(s, 512)
    half = SSM_CONV_DIM // 2
    hb = tb // SUBLANES

    def body(x_ref, halo_ref, d_ref, w_ref, b_ref, o_ref, sums_ref, scr_ref):
        i = pl.program_id(1)

        @pl.when(i == 0)
        def _():
            sums_ref[...] = jnp.zeros_like(sums_ref)

        scr_ref[pl.ds(0, SUBLANES), :] = jnp.where(i > 0, halo_ref[...], 0.0)
        scr_ref[pl.ds(SUBLANES, tb), :] = x_ref[...]
        pre = _conv_taps(scr_ref, w_ref, tb, SUBLANES) + b_ref[0:1, :]
        dpre = d_ref[...] * _dsilu(pre)
        o_ref[...] = dpre
        for k in range(SSM_CONV):
            shifted = scr_ref[pl.ds(SUBLANES - (SSM_CONV - 1) + k, tb), :]
            sums_ref[k:k + 1, :] += jnp.sum(dpre * shifted, axis=0, keepdims=True)
        sums_ref[SSM_CONV:SSM_CONV + 1, :] += jnp.sum(dpre, axis=0, keepdims=True)

    return pl.pallas_call(
        body, name="ssd_conv_bwd_pre", grid=(2, s // tb),
        in_specs=[pl.BlockSpec((tb, half), lambda j, i: (i, 1 + j)),
                  pl.BlockSpec((SUBLANES, half), lambda j, i: (jnp.maximum(i * hb - 1, 0), 1 + j)),
                  pl.BlockSpec((tb, half), lambda j, i: (i, j)),
                  pl.BlockSpec((8, half), lambda j, i: (0, j)),
                  pl.BlockSpec((8, half), lambda j, i: (0, j))],
        out_specs=[pl.BlockSpec((tb, half), lambda j, i: (i, j)),
                   pl.BlockSpec((8, half), lambda j, i: (0, j))],
        out_shape=[jax.ShapeDtypeStruct((s, SSM_CONV_DIM), F32), jax.ShapeDtypeStruct((8, SSM_CONV_DIM), F32)],
        scratch_shapes=[pltpu.VMEM((tb + SUBLANES, half), F32)],
    )(zx, zx, dxc, cw, cb)


def _conv_bwd_x(dpre, cw):
    s = dpre.shape[0]
    tb = _tile(s, 512)
    hb = tb // SUBLANES
    nb = s // tb

    def body(d_ref, halo_ref, w_ref, o_ref, scr_ref):
        i = pl.program_id(0)
        scr_ref[pl.ds(0, tb), :] = d_ref[...]
        scr_ref[pl.ds(tb, SUBLANES), :] = jnp.where(i < nb - 1, halo_ref[...], 0.0)
        acc = None
        for k in range(SSM_CONV):
            term = scr_ref[pl.ds(SSM_CONV - 1 - k, tb), :] * w_ref[k:k + 1, :]
            acc = term if acc is None else acc + term
        o_ref[...] = _mx(acc)

    return pl.pallas_call(
        body, name="ssd_conv_bwd_x", grid=(nb,),
        in_specs=[pl.BlockSpec((tb, SSM_CONV_DIM), lambda i: (i, 0)),
                  pl.BlockSpec((SUBLANES, SSM_CONV_DIM), lambda i: (jnp.minimum((i + 1) * hb, s // SUBLANES - 1), 0)),
                  pl.BlockSpec((8, SSM_CONV_DIM), lambda i: (0, 0))],
        out_specs=pl.BlockSpec((tb, SSM_CONV_DIM), lambda i: (i, 0)),
        out_shape=jax.ShapeDtypeStruct((s, SSM_CONV_DIM), MXU_DTYPE),
        scratch_shapes=[pltpu.VMEM((tb + SUBLANES, SSM_CONV_DIM), F32)],
        compiler_params=_params(VMEM_BIG),
    )(dpre, dpre, cw)


def _expand_constants():
    ex = np.zeros((LANES, SSM_D_INNER), np.float32)
    for h in range(SSM_HEADS):
        ex[h, h * 64:(h + 1) * 64] = 1.0
    return ex, np.ascontiguousarray(ex.T)


def _ssd_common(dtr_ref, par_ref, ex_ref, xc_ref):
    lc = SSM_CHUNK
    lane = lax.broadcasted_iota(jnp.int32, (1, LANES), 1)
    is_head = lane < SSM_HEADS
    par = par_ref[...]
    pre = dtr_ref[...] + par[0:1, :]
    dt = jnp.where(is_head, jax.nn.softplus(pre), 0.0)
    a = jnp.where(is_head, -jnp.exp(par[1:2, :]), 0.0)
    tri_b = lax.broadcasted_iota(jnp.int32, (lc, lc), 0) >= lax.broadcasted_iota(jnp.int32, (lc, lc), 1)
    tri = tri_b.astype(F32)
    da = dt * a
    acs = _dot_onehot(tri, da, onehot="a")
    acs_t = _dot_onehot(da, tri, (((0,), (1,)), ((), ())))
    wide = _dot_onehot(jnp.concatenate([dt, acs, par], axis=0), ex_ref[...])
    dt_x, acs_x, d_x = wide[0:lc], wide[lc:2 * lc], wide[2 * lc + 2:2 * lc + 3]
    last_x = acs_x[lc - 1:lc, :]
    xs = xc_ref[:, 0:SSM_D_INNER]
    return dict(pre=pre, dt=dt, a=a, tri_b=tri_b, tri=tri, acs=acs, acs_t=acs_t, dt_x=dt_x, d_x=d_x, xs=xs,
                xdt=xs * dt_x, e_x=jnp.exp(acs_x), dte_x=jnp.exp(last_x - acs_x), cd_x=jnp.exp(last_x),
                is_head=is_head)


def _decay_in(q, h):
    seg = q["acs"][:, h:h + 1] - q["acs_t"][h:h + 1, :]
    return jnp.exp(jnp.where(q["tri_b"], seg, -jnp.inf))


def _ssd_scan_fwd(xc, dtr, par):
    s = xc.shape[0]
    lc = SSM_CHUNK
    nc = s // lc
    ex, _ = _expand_constants()

    def body(xc_ref, dtr_ref, par_ref, ex_ref, y_ref, prev_ref, st_ref):
        @pl.when(pl.program_id(0) == 0)
        def _():
            st_ref[...] = jnp.zeros_like(st_ref)

        q = _ssd_common(dtr_ref, par_ref, ex_ref, xc_ref)
        lane = lax.broadcasted_iota(jnp.int32, (1, LANES), 1)
        for g in range(SSM_GROUPS):
            sl = slice(g * GROUP_W, (g + 1) * GROUP_W)
            bg = _mx(xc_ref[:, SSM_D_INNER + g * SSM_STATE:SSM_D_INNER + (g + 1) * SSM_STATE])
            cg = _mx(xc_ref[:, SSM_D_INNER + (SSM_GROUPS + g) * SSM_STATE:SSM_D_INNER + (SSM_GROUPS + g + 1) * SSM_STATE])
            gm = _dot(cg, bg, NT)
            prev = st_ref[g]
            prev_ref[g] = prev
            yoff = _dot(cg, _mx(prev)) * q["e_x"][:, sl]
            st_ref[g] = prev * q["cd_x"][:, sl] + _dot(bg, _mx(q["xdt"][:, sl] * q["dte_x"][:, sl]), TN)
            pairs = []
            for pr in range(2):
                xp = _mx(q["xdt"][:, g * GROUP_W + pr * LANES:g * GROUP_W + (pr + 1) * LANES])
                both = [_dot(_mx(gm * _decay_in(q, 4 * g + 2 * pr + r2)), xp) for r2 in range(2)]
                pairs.append(jnp.where(lane < 64, both[0], both[1]))
            y_ref[:, sl] = jnp.concatenate(pairs, axis=1) + yoff + q["xs"][:, sl] * q["d_x"][:, sl]

    return pl.pallas_call(
        body, name="ssd_scan_fwd", grid=(nc,),
        in_specs=[pl.BlockSpec((lc, SSM_CONV_DIM), lambda c: (c, 0)),
                  pl.BlockSpec((lc, LANES), lambda c: (c, 0)),
                  pl.BlockSpec((8, LANES), lambda c: (0, 0)),
                  pl.BlockSpec((LANES, SSM_D_INNER), lambda c: (0, 0))],
        out_specs=[pl.BlockSpec((lc, SSM_D_INNER), lambda c: (c, 0)),
                   pl.BlockSpec((None, SSM_GROUPS, SSM_STATE, GROUP_W), lambda c: (c, 0, 0, 0))],
        out_shape=[jax.ShapeDtypeStruct((s, SSM_D_INNER), F32),
                   jax.ShapeDtypeStruct((nc, SSM_GROUPS, SSM_STATE, GROUP_W), F32)],
        scratch_shapes=[pltpu.VMEM((SSM_GROUPS, SSM_STATE, GROUP_W), F32)],
        compiler_params=_params(VMEM_BIG),
    )(xc, dtr, par, _mx(jnp.asarray(ex)))


def _ssd_scan_bwd(dy, xc, dtr, par, prev):
    s = xc.shape[0]
    lc = SSM_CHUNK
    nc = s // lc
    ex, ex_t = _expand_constants()

    def body(dy_ref, xc_ref, dtr_ref, par_ref, prev_ref, ex_ref, ext_ref, dxc_ref, ddtr_ref, sums_ref,
             gst_ref, tacs_ref, tdt_ref, tdd_ref):
        @pl.when(pl.program_id(0) == 0)
        def _():
            gst_ref[...] = jnp.zeros_like(gst_ref)
            sums_ref[...] = jnp.zeros_like(sums_ref)

        q = _ssd_common(dtr_ref, par_ref, ex_ref, xc_ref)
        lane = lax.broadcasted_iota(jnp.int32, (1, LANES), 1)
        row = lax.broadcasted_iota(jnp.int32, (lc, 1), 0)
        dacs_rows = jnp.zeros((lc, LANES), F32)
        dacs_cols_t = jnp.zeros((LANES, lc), F32)
        for g in range(SSM_GROUPS):
            sl = slice(g * GROUP_W, (g + 1) * GROUP_W)
            b_lo = SSM_D_INNER + g * SSM_STATE
            c_lo = SSM_D_INNER + (SSM_GROUPS + g) * SSM_STATE
            bg = _mx(xc_ref[:, b_lo:b_lo + SSM_STATE])
            cg = _mx(xc_ref[:, c_lo:c_lo + SSM_STATE])
            dyg = dy_ref[:, sl]
            xsg, xdtg = q["xs"][:, sl], q["xdt"][:, sl]
            eg, dteg, cdg = q["e_x"][:, sl], q["dte_x"][:, sl], q["cd_x"][:, sl]
            prevg = prev_ref[g]
            gs = gst_ref[g]
            prevm, gsm = _mx(prevg), _mx(gs)
            tdd_ref[:, sl] = dyg * xsg
            dxs = dyg * q["d_x"][:, sl]
            t_acs = dyg * _dot(cg, prevm) * eg
            dcp = _mx(dyg * eg)
            dc = _dot(dcp, prevm, NT)
            dprev = _dot(cg, dcp, TN)
            db = _dot(_mx(xdtg * dteg), gsm, NT)
            dx2 = _dot(bg, gsm)
            dxdt = dx2 * dteg
            ddte = dx2 * xdtg * dteg
            t_acs = t_acs - ddte
            last = (jnp.sum(ddte, axis=0, keepdims=True)
                    + jnp.sum(gs * prevg, axis=0, keepdims=True) * cdg)
            gm = _dot(cg, bg, NT)
            dgm = jnp.zeros((lc, lc), F32)
            pair_dx = []
            for pr in range(2):
                lo = g * GROUP_W + pr * LANES
                xp = _mx(q["xdt"][:, lo:lo + LANES])
                dyp = dy_ref[:, lo:lo + LANES]
                both = []
                for r2 in range(2):
                    h = 4 * g + 2 * pr + r2
                    mine = (lane >= 64 * r2) & (lane < 64 * (r2 + 1))
                    lm = _decay_in(q, h)
                    m = gm * lm
                    dm = _dot(_mx(jnp.where(mine, dyp, 0.0)), xp, NT)
                    dgm = dgm + dm * lm
                    w = dm * m
                    dacs_rows = dacs_rows + jnp.sum(w, axis=1, keepdims=True) * (lane == h).astype(F32)
                    head_row = (lax.broadcasted_iota(jnp.int32, (LANES, 1), 0) == h).astype(F32)
                    dacs_cols_t = dacs_cols_t + head_row * jnp.sum(w, axis=0, keepdims=True)
                    both.append(_dot(_mx(m), _mx(dyp), TN))
                pair_dx.append(jnp.where(lane < 64, both[0], both[1]))
            dxdt = dxdt + jnp.concatenate(pair_dx, axis=1)
            dgmm = _mx(dgm)
            dc = dc + _dot(dgmm, bg)
            db = db + _dot(dgmm, cg, TN)
            dxs = dxs + dxdt * q["dt_x"][:, sl]
            tdt_ref[:, sl] = dxdt * xsg
            tacs_ref[:, sl] = t_acs + jnp.where(row == lc - 1, last, 0.0)
            dxc_ref[:, sl] = dxs
            dxc_ref[:, b_lo:b_lo + SSM_STATE] = db
            dxc_ref[:, c_lo:c_lo + SSM_STATE] = dc
            gst_ref[g] = gs * cdg + dprev
        tdd = jnp.broadcast_to(jnp.sum(tdd_ref[...], axis=0, keepdims=True), (8, SSM_D_INNER))
        heads_of = _dot_onehot(jnp.concatenate([tacs_ref[...], tdt_ref[...], tdd], axis=0), ext_ref[...])
        dacs = heads_of[0:lc] + dacs_rows - dacs_cols_t.T
        dda = _dot_onehot(q["tri"], dacs, TN, onehot="a")
        ddt = dda * q["a"] + heads_of[lc:2 * lc]
        ddtr = jnp.where(q["is_head"], ddt * jax.nn.sigmoid(q["pre"]), 0.0)
        ddtr_ref[...] = _mx(ddtr)
        sums_ref[0:1, :] += jnp.sum(ddtr, axis=0, keepdims=True)
        sums_ref[1:2, :] += jnp.sum(dda * q["dt"], axis=0, keepdims=True) * q["a"]
        sums_ref[2:3, :] += heads_of[2 * lc:2 * lc + 1]

    rev = lambda c: nc - 1 - c
    wide = pltpu.VMEM((lc, SSM_D_INNER), F32)
    return pl.pallas_call(
        body, name="ssd_scan_bwd", grid=(nc,),
        in_specs=[pl.BlockSpec((lc, SSM_D_INNER), lambda c: (rev(c), 0)),
                  pl.BlockSpec((lc, SSM_CONV_DIM), lambda c: (rev(c), 0)),
                  pl.BlockSpec((lc, LANES), lambda c: (rev(c), 0)),
                  pl.BlockSpec((8, LANES), lambda c: (0, 0)),
                  pl.BlockSpec((None, SSM_GROUPS, SSM_STATE, GROUP_W), lambda c: (rev(c), 0, 0, 0)),
                  pl.BlockSpec((LANES, SSM_D_INNER), lambda c: (0, 0)),
                  pl.BlockSpec((SSM_D_INNER, LANES), lambda c: (0, 0))],
        out_specs=[pl.BlockSpec((lc, SSM_CONV_DIM), lambda c: (rev(c), 0)),
                   pl.BlockSpec((lc, LANES), lambda c: (rev(c), 0)),
                   pl.BlockSpec((8, LANES), lambda c: (0, 0))],
        out_shape=[jax.ShapeDtypeStruct((s, SSM_CONV_DIM), F32), jax.ShapeDtypeStruct((s, LANES), MXU_DTYPE),
                   jax.ShapeDtypeStruct((8, LANES), F32)],
        scratch_shapes=[pltpu.VMEM((SSM_GROUPS, SSM_STATE, GROUP_W), F32), wide, wide, wide],
        compiler_params=_params(VMEM_BIG),
    )(dy, xc, dtr, par, prev, _mx(jnp.asarray(ex)), _mx(jnp.asarray(ex_t)))


def _group_norm_parts(yv, zv):
    yg = yv * jax.nn.silu(zv)
    normed, rinvs = [], []
    for g in range(SSM_GROUPS):
        blk = yg[:, g * GROUP_W:(g + 1) * GROUP_W]
        rinv = lax.rsqrt(jnp.mean(blk * blk, axis=-1, keepdims=True) + RMS_EPS)
        normed.append(blk * rinv)
        rinvs.append(rinv)
    return normed, rinvs


def _gnorm_fwd(y, zx, nw):
    s = y.shape[0]
    tb = _tile(s, 512)

    def body(y_ref, z_ref, w_ref, o_ref):
        normed, _ = _group_norm_parts(y_ref[...], z_ref[...])
        for g in range(SSM_GROUPS):
            sl = slice(g * GROUP_W, (g + 1) * GROUP_W)
            o_ref[:, sl] = _mx(normed[g] * w_ref[0:1, sl])

    row = pl.BlockSpec((tb, SSM_D_INNER), lambda i: (i, 0))
    return pl.pallas_call(
        body, name="ssd_gnorm_fwd", grid=(s // tb,),
        in_specs=[row, row, pl.BlockSpec((8, SSM_D_INNER), lambda i: (0, 0))],
        out_specs=row, out_shape=jax.ShapeDtypeStruct((s, SSM_D_INNER), MXU_DTYPE),
    )(y, zx, nw)


def _gnorm_bwd(y, zx, nw, dyn):
    s = y.shape[0]
    tb = _tile(s, 512)

    def body(y_ref, z_ref, w_ref, d_ref, dy_ref, dz_ref, sums_ref):
        @pl.when(pl.program_id(0) == 0)
        def _():
            sums_ref[...] = jnp.zeros_like(sums_ref)

        yv, zv = y_ref[...], z_ref[...]
        normed, rinvs = _group_norm_parts(yv, zv)
        gate = jax.nn.silu(zv)
        dgate = _dsilu(zv)
        for g in range(SSM_GROUPS):
            sl = slice(g * GROUP_W, (g + 1) * GROUP_W)
            dv = d_ref[:, sl]
            n = normed[g]
            sums_ref[0:1, sl] += jnp.sum(dv * n, axis=0, keepdims=True)
            dn = dv * w_ref[0:1, sl]
            dyg = rinvs[g] * (dn - n * jnp.mean(dn * n, axis=-1, keepdims=True))
            dy_ref[:, sl] = dyg * gate[:, sl]
            dz_ref[:, sl] = _mx(dyg * yv[:, sl] * dgate[:, sl])

    row = pl.BlockSpec((tb, SSM_D_INNER), lambda i: (i, 0))
    par = pl.BlockSpec((8, SSM_D_INNER), lambda i: (0, 0))
    return pl.pallas_call(
        body, name="ssd_gnorm_bwd", grid=(s // tb,),
        in_specs=[row, row, par, row], out_specs=[row, row, par],
        out_shape=[jax.ShapeDtypeStruct((s, SSM_D_INNER), F32), jax.ShapeDtypeStruct((s, SSM_D_INNER), MXU_DTYPE),
                   jax.ShapeDtypeStruct((8, SSM_D_INNER), F32)],
    )(y, zx, nw, dyn)


def _rows8(v):
    v = v.reshape(1, -1)
    return jnp.pad(v, ((0, 7), (0, 0)))


def _ssd_weights(w_in, w_out):
    nzx = SSM_D_INNER + SSM_CONV_DIM
    wzx = w_in[:, :nzx]
    wdt = jnp.pad(w_in[:, nzx:], ((0, 0), (0, LANES - SSM_HEADS)))
    return dict(wzx=wzx, wdt=wdt, wzx_t=wzx.T, wdt_t=wdt.T, wout=w_out, wout_t=w_out.T)


def _ssd_fwd(u, w, cw, cb, par, nw):
    zx = _mm_nn("ssd_in_zx", u, w["wzx"], F32)
    dtr = _mm_nn("ssd_in_dt", u, w["wdt"], F32)
    xc = _conv_fwd(zx, cw, cb)
    y, prev = _ssd_scan_fwd(xc, dtr, par)
    yn = _gnorm_fwd(y, zx, nw)
    out = _mm_nn("ssd_out", yn, w["wout"], F32)
    return out, dict(zx=zx, dtr=dtr, xc=xc, y=y, prev=prev, yn=yn)


def _ssd_bwd(dy, u, w, cw, cb, par, nw, res):
    dyn = _mm_nn("ssd_dyn", dy, w["wout_t"], F32)
    dw_out = _mm_tn("ssd_dw_out", res["yn"], dy)
    dys, dz, nsum = _gnorm_bwd(res["y"], res["zx"], nw, dyn)
    dxc, ddtr, ssum = _ssd_scan_bwd(dys, res["xc"], res["dtr"], par, res["prev"])
    dpre, csum = _conv_bwd_pre(res["zx"], dxc, cw, cb)
    dxbc = _conv_bwd_x(dpre, cw)
    wt = w["wzx_t"]
    du = _mm_nn("ssd_du_z", dz, wt[:SSM_D_INNER], F32)
    du = _mm_nn("ssd_du_x", dxbc, wt[SSM_D_INNER:], F32, addend=du)
    du = _mm_nn("ssd_du_dt", ddtr, w["wdt_t"], F32, addend=du)
    dw_in = jnp.concatenate(
        [_mm_tn("ssd_dw_z", u, dz), _mm_tn("ssd_dw_x", u, dxbc), _mm_tn("ssd_dw_dt", u, ddtr)[:, :SSM_HEADS]], axis=1)
    small = dict(conv_w=csum[:SSM_CONV], conv_b=csum[SSM_CONV], dt_bias=ssum[0, :SSM_HEADS],
                 a_log=ssum[1, :SSM_HEADS], d=ssum[2, :SSM_HEADS], norm_w=nsum[0])
    return du, dw_in, dw_out, small


def _ada_fwd(c_all, ada_w, ada_b_mine):
    nl, _, ncol = ada_w.shape

    def body(c_ref, w_ref, b_ref, o_ref):
        cond = _mx(jax.nn.silu(c_ref[...]))
        for i in range(nl):
            o_ref[i] = _dot(cond, _mx(w_ref[i])) + b_ref[i:i + 1, :]

    return pl.pallas_call(
        body, name="ada_fwd", out_shape=jax.ShapeDtypeStruct((nl, 2 * N_DEV, ncol), F32),
        compiler_params=_params(VMEM_BIG),
    )(c_all, ada_w, ada_b_mine)


def _ada_bwd(c_all, dmod_cols):
    nl, _, ncol = dmod_cols.shape

    def body(c_ref, d_ref, o_ref):
        cond = _mx(jax.nn.silu(c_ref[...]))
        for i in range(nl):
            o_ref[i] = _dot(cond, _mx(d_ref[i]), TN)

    return pl.pallas_call(
        body, name="ada_bwd", out_shape=jax.ShapeDtypeStruct((nl, D_MODEL, ncol), F32),
        compiler_params=_params(VMEM_BIG),
    )(c_all, dmod_cols)


def _adamw(gslots, w, m, v, name):
    k, r, c = gslots.shape
    tr = _tile(r, 256) if r % 256 == 0 else r
    c1 = 1.0 - ADAM_B1 ** ADAM_STEP
    c2 = 1.0 - ADAM_B2 ** ADAM_STEP

    def body(g_ref, w_ref, m_ref, v_ref, go_ref, d_ref, mo_ref, vo_ref):
        g = g_ref[0]
        for slot in range(1, k):
            g = g + g_ref[slot]
        mn = ADAM_B1 * m_ref[...] + (1.0 - ADAM_B1) * g
        vn = ADAM_B2 * v_ref[...] + (1.0 - ADAM_B2) * jnp.square(g)
        go_ref[...] = g
        mo_ref[...] = mn
        vo_ref[...] = vn
        d_ref[...] = -ADAM_LR * ((mn / c1) / (jnp.sqrt(vn / c2) + ADAM_EPS) + ADAM_WD * w_ref[...])

    row = pl.BlockSpec((tr, c), lambda i: (i, 0))
    shp = jax.ShapeDtypeStruct((r, c), F32)
    return pl.pallas_call(
        body, name=name, grid=(r // tr,),
        in_specs=[pl.BlockSpec((k, tr, c), lambda i: (0, i, 0)), row, row, row],
        out_specs=[row, row, row, row], out_shape=[shp, shp, shp, shp],
        compiler_params=_params(VMEM_BIG),
    )(gslots, w, m, v)


def _adamw_any(gslots, w, m, v, name):
    shape = w.shape
    two_d = (-1, shape[-1])
    k = gslots.shape[0]
    outs = _adamw(gslots.reshape((k,) + w.reshape(two_d).shape), w.reshape(two_d), m.reshape(two_d),
                  v.reshape(two_d), name)
    return tuple(o.reshape(shape) for o in outs)


def _cols_from_slots(g):
    g = jnp.moveaxis(g, 0, -2)
    return g.reshape(g.shape[:-2] + (g.shape[-2] * g.shape[-1],))


def _rows_from_slots(g):
    g = jnp.moveaxis(g, 0, -3)
    return g.reshape(g.shape[:-3] + (g.shape[-3] * g.shape[-2], g.shape[-1]))


def _col_slots(g):
    cs = g.shape[-1] // N_DEV
    return jnp.moveaxis(g.reshape(g.shape[:-1] + (N_DEV, cs)), -2, 0)


def _row_slots(g):
    rs = g.shape[-2] // N_DEV
    return jnp.moveaxis(g.reshape(g.shape[:-2] + (N_DEV, rs, g.shape[-1])), -3, 0)


def _gather_cols(w, name, dtype=None):
    return _cols_from_slots(_all_gather(w.astype(dtype or MXU_DTYPE), name))


def _gather_rows(w, name):
    return _rows_from_slots(_all_gather(_mx(w), name))


def _scatter_cols(g, name):
    return _all_to_all(_col_slots(g), name)


def _scatter_rows(g, name):
    return _all_to_all(_row_slots(g), name)


def kernel(x, c, ada_w, ada_b, ln_mix_g, ln_mix_b, ln_mlp_g, ln_mlp_b, mlp_w1, mlp_w2, fox_w_in, fox_b_f, fox_w_o, ssm_w_in, ssm_conv_w, ssm_conv_b, ssm_dt_bias, ssm_a_log, ssm_d, ssm_norm_w, ssm_w_out, loss_target, m_ada_w, m_ada_b, m_ln_mix_g, m_ln_mix_b, m_ln_mlp_g, m_ln_mlp_b, m_mlp_w1, m_mlp_w2, m_fox_w_in, m_fox_b_f, m_fox_w_o, m_ssm_w_in, m_ssm_conv_w, m_ssm_conv_b, m_ssm_dt_bias, m_ssm_a_log, m_ssm_d, m_ssm_norm_w, m_ssm_w_out, v_ada_w, v_ada_b, v_ln_mix_g, v_ln_mix_b, v_ln_mlp_g, v_ln_mlp_b, v_mlp_w1, v_mlp_w2, v_fox_w_in, v_fox_b_f, v_fox_w_o, v_ssm_w_in, v_ssm_conv_w, v_ssm_conv_b, v_ssm_dt_bias, v_ssm_a_log, v_ssm_d, v_ssm_norm_w, v_ssm_w_out):
    me = 4 * lax.axis_index("x") + 2 * lax.axis_index("y") + lax.axis_index("c")
    xs = x[0]
    target = loss_target[0]
    d = D_MODEL

    c_all = _all_gather(c, "gather_c").reshape(N_DEV, d)
    c_all = jnp.pad(c_all, ((0, N_DEV), (0, 0)))
    ncol = ada_w.shape[-1]
    ada_b_mine = lax.dynamic_slice_in_dim(ada_b, me * ncol, ncol, axis=1)
    mod_cols = _ada_fwd(c_all, ada_w, ada_b_mine)
    mod_all = _all_gather(mod_cols, "gather_mod")
    mod = lax.dynamic_index_in_dim(mod_all, me, axis=2, keepdims=False)
    mod = jnp.moveaxis(mod, 0, 1).reshape(DEPTH, 6, d)

    def pv_rows(*rows):
        return jnp.pad(jnp.stack(rows), ((0, 8 - len(rows)), (0, 0)))

    fw = _fox_weights(_gather_cols(fox_w_in, "gather_fox_in")[0], _gather_rows(fox_w_o, "gather_fox_o")[0])
    conv_w = _gather_cols(ssm_conv_w, "gather_conv_w", F32)[0]
    small_vec = jnp.concatenate([ssm_conv_b[0], ssm_norm_w[0]]).reshape(1, -1)
    small_all = _all_gather(small_vec.astype(F32), "gather_conv_b").reshape(N_DEV, -1)
    conv_b = small_all[:, :SSM_CONV_DIM // N_DEV].reshape(-1)
    norm_w = small_all[:, SSM_CONV_DIM // N_DEV:].reshape(-1)
    cw8 = jnp.pad(conv_w, ((0, 8 - SSM_CONV), (0, 0)))
    cb8 = _rows8(conv_b)
    nw8 = _rows8(norm_w)
    bf8 = _rows8(jnp.pad(fox_b_f[0], (0, LANES - FOX_HEADS)))
    par8 = jnp.pad(jnp.stack([jnp.pad(p[0], (0, LANES - SSM_HEADS)) for p in (ssm_dt_bias, ssm_a_log, ssm_d)]),
                   ((0, 5), (0, 0)))

    sh_a, sc_a, g_a, sh_m, sc_m, g_m = [mod[:, k] for k in range(6)]
    u0 = _modulate(xs, pv_rows(1.0 + sc_a[0], sh_a[0]), "modulate0")
    y0, fres, gathered = _fox_fwd(u0, fw, bf8, [(False, _mx(w)) for w in (mlp_w1, mlp_w2, ssm_w_in, ssm_w_out)])
    w1 = _cols_from_slots(gathered[0])
    w2 = _rows_from_slots(gathered[1])
    sw = _ssd_weights(_cols_from_slots(gathered[2])[0], _rows_from_slots(gathered[3])[0])
    pv0 = pv_rows(1.0 + g_a[0], ln_mix_g[0], ln_mix_b[0], 1.0 + sc_m[0], sh_m[0])
    x1, u1 = _ln_fwd(xs, y0, pv0, "ln_mix0")
    y1, (h0, a0) = _mlp_fwd(u1, w1[0], w2[0], "0")
    pv1 = pv_rows(1.0 + g_m[0], ln_mlp_g[0], ln_mlp_b[0], 1.0 + sc_a[1], sh_a[1])
    x2, u2 = _ln_fwd(x1, y1, pv1, "ln_mlp0")
    y2, sres = _ssd_fwd(u2, sw, cw8, cb8, par8, nw8)
    pv2 = pv_rows(1.0 + g_a[1], ln_mix_g[1], ln_mix_b[1], 1.0 + sc_m[1], sh_m[1])
    x3, u3 = _ln_fwd(x2, y2, pv2, "ln_mix1")
    y3, (h1, a1) = _mlp_fwd(u3, w1[1], w2[1], "1")
    pv3 = pv_rows(1.0 + g_m[1], ln_mlp_g[1], ln_mlp_b[1])

    dx3, dy3, s3 = _ln_bwd(x3, y3, pv3, "ln_mlp1_bwd", target=target)
    loss = lax.psum(s3[5, 0], ("x", "y", "c"))
    du3, dw1_1, dw2_1 = _mlp_bwd(dy3, u3, h1, a1, w1[1].T, w2[1].T, "1")
    dx2, dy2, s2 = _ln_bwd(x2, y2, pv2, "ln_mix1_bwd", dxo=dx3, du=du3)
    du2, d_ssm_in, d_ssm_out, ssm_small = _ssd_bwd(dy2, u2, sw, cw8, cb8, par8, nw8, sres)
    dx1, dy1, s1 = _ln_bwd(x1, y1, pv1, "ln_mlp0_bwd", dxo=dx2, du=du2)
    du1, dw1_0, dw2_0 = _mlp_bwd(dy1, u1, h0, a0, w1[0].T, w2[0].T, "0")
    dx0, dy0, s0 = _ln_bwd(xs, y0, pv0, "ln_mix0_bwd", dxo=dx1, du=du1)
    early = [(True, _col_slots(jnp.stack([dw1_0, dw1_1]))), (True, _row_slots(jnp.stack([dw2_0, dw2_1]))),
             (True, _col_slots(d_ssm_in[None])), (True, _row_slots(d_ssm_out[None])),
             (True, _col_slots(ssm_small["conv_w"][None])), (True, _col_slots(ssm_small["conv_b"][None])),
             (True, _col_slots(ssm_small["norm_w"][None]))]
    du0, d_fox_in, d_fox_o, fox_sums, exchanged = _fox_bwd(dy0, u0, fw, bf8, fres, early)
    grad_x, sx = _mod_bwd(dx0, du0, xs, pv_rows(1.0 + sc_a[0], sh_a[0]), "modulate0_bwd")

    dmod = jnp.stack([
        jnp.stack([sx[1], sx[0], s0[4], s0[1], s0[0], s1[4]]),
        jnp.stack([s1[1], s1[0], s2[4], s2[1], s2[0], s3[4]]),
    ]).reshape(DEPTH, 6 * d)

    def pad_rows(v):
        v = v.reshape(-1, LANES) if v.size % LANES == 0 else jnp.pad(v.reshape(-1), (0, LANES - v.size)).reshape(1, LANES)
        return jnp.pad(v, ((0, (-v.shape[0]) % 8), (0, 0)))

    small_parts = [dmod, jnp.stack([s0[2], s2[2]]), jnp.stack([s0[3], s2[3]]), jnp.stack([s1[2], s3[2]]),
                   jnp.stack([s1[3], s3[3]]), fox_sums[0, :FOX_HEADS], ssm_small["dt_bias"], ssm_small["a_log"],
                   ssm_small["d"]]
    packed = [pad_rows(p) for p in small_parts]
    offsets = np.cumsum([0] + [p.shape[0] for p in packed])
    small_all_g = _all_gather(jnp.concatenate(packed, axis=0), "gather_small_grads")

    def unpack(idx, shape):
        n = int(np.prod(shape))
        blk = small_all_g[:, offsets[idx]:offsets[idx + 1]].reshape(N_DEV, -1)[:, :n]
        return blk.reshape((N_DEV,) + tuple(shape))

    dmod_all = unpack(0, (DEPTH, 6 * d))
    dmod_cols = lax.dynamic_slice_in_dim(dmod_all, me * ncol, ncol, axis=2)
    dmod_cols = jnp.pad(jnp.moveaxis(dmod_cols, 0, 1), ((0, 0), (0, N_DEV), (0, 0)))
    g_ada_w = _ada_bwd(c_all, dmod_cols)

    shards = dict(
        mlp_w1=exchanged[0], mlp_w2=exchanged[1], ssm_w_in=exchanged[2], ssm_w_out=exchanged[3],
        ssm_conv_w=exchanged[4], ssm_conv_b=exchanged[5], ssm_norm_w=exchanged[6],
        fox_w_in=_scatter_cols(d_fox_in[None], "scatter_fox_in"), fox_w_o=_scatter_rows(d_fox_o[None], "scatter_fox_o"),
        ada_w=g_ada_w[None], ada_b=dmod_all,
        ln_mix_g=unpack(1, (DEPTH, d)), ln_mix_b=unpack(2, (DEPTH, d)),
        ln_mlp_g=unpack(3, (DEPTH, d)), ln_mlp_b=unpack(4, (DEPTH, d)),
        fox_b_f=unpack(5, (1, FOX_HEADS)), ssm_dt_bias=unpack(6, (1, SSM_HEADS)),
        ssm_a_log=unpack(7, (1, SSM_HEADS)), ssm_d=unpack(8, (1, SSM_HEADS)),
    )
    weights = dict(ada_w=ada_w, ada_b=ada_b, ln_mix_g=ln_mix_g, ln_mix_b=ln_mix_b, ln_mlp_g=ln_mlp_g, ln_mlp_b=ln_mlp_b,
                   mlp_w1=mlp_w1, mlp_w2=mlp_w2, fox_w_in=fox_w_in, fox_b_f=fox_b_f, fox_w_o=fox_w_o, ssm_w_in=ssm_w_in,
                   ssm_conv_w=ssm_conv_w, ssm_conv_b=ssm_conv_b, ssm_dt_bias=ssm_dt_bias, ssm_a_log=ssm_a_log,
                   ssm_d=ssm_d, ssm_norm_w=ssm_norm_w, ssm_w_out=ssm_w_out)
    mom1 = dict(ada_w=m_ada_w, ada_b=m_ada_b, ln_mix_g=m_ln_mix_g, ln_mix_b=m_ln_mix_b, ln_mlp_g=m_ln_mlp_g,
                ln_mlp_b=m_ln_mlp_b, mlp_w1=m_mlp_w1, mlp_w2=m_mlp_w2, fox_w_in=m_fox_w_in, fox_b_f=m_fox_b_f,
                fox_w_o=m_fox_w_o, ssm_w_in=m_ssm_w_in, ssm_conv_w=m_ssm_conv_w, ssm_conv_b=m_ssm_conv_b,
                ssm_dt_bias=m_ssm_dt_bias, ssm_a_log=m_ssm_a_log, ssm_d=m_ssm_d, ssm_norm_w=m_ssm_norm_w,
                ssm_w_out=m_ssm_w_out)
    mom2 = dict(ada_w=v_ada_w, ada_b=v_ada_b, ln_mix_g=v_ln_mix_g, ln_mix_b=v_ln_mix_b, ln_mlp_g=v_ln_mlp_g,
                ln_mlp_b=v_ln_mlp_b, mlp_w1=v_mlp_w1, mlp_w2=v_mlp_w2, fox_w_in=v_fox_w_in, fox_b_f=v_fox_b_f,
                fox_w_o=v_fox_w_o, ssm_w_in=v_ssm_w_in, ssm_conv_w=v_ssm_conv_w, ssm_conv_b=v_ssm_conv_b,
                ssm_dt_bias=v_ssm_dt_bias, ssm_a_log=v_ssm_a_log, ssm_d=v_ssm_d, ssm_norm_w=v_ssm_norm_w,
                ssm_w_out=v_ssm_w_out)
    names = list(weights)
    stepped = {n: _adamw_any(shards[n], weights[n], mom1[n], mom2[n], f"adamw_{n}") for n in names}
    return (loss, grad_x[None], *[stepped[n][0] for n in names], *[stepped[n][1] for n in names],
            *[stepped[n][2] for n in names], *[stepped[n][3] for n in names])
```

```python
import numpy as np
import jax
import jax.numpy as jnp
from jax import lax
from jax.experimental import pallas as pl
from jax.experimental.pallas import tpu as pltpu

F32 = jnp.float32
MXU_DTYPE = jnp.bfloat16
HIGHEST = lax.Precision.HIGHEST

N_DEV = 8
D_MODEL = 1024
DEPTH = 2
FOX_HEADS = 16
FOX_HEAD_DIM = 64
D_FF = 4096
SSM_D_INNER = 2048
SSM_HEADS = 32
SSM_GROUPS = 8
SSM_STATE = 128
SSM_CHUNK = 128
SSM_CONV = 4
SSM_CONV_DIM = 4096
GROUP_W = SSM_D_INNER // SSM_GROUPS
LN_EPS = 1e-5
RMS_EPS = 1e-5
ALPHA = (2.0 * DEPTH) ** 0.25
LANES = 128
SUBLANES = 8

ADAM_LR = 0.001
ADAM_B1 = 0.9
ADAM_B2 = 0.999
ADAM_EPS = 1e-08
ADAM_WD = 0.01
ADAM_STEP = 10

NN = (((1,), (0,)), ((), ()))
NT = (((1,), (1,)), ((), ()))
TN = (((0,), (0,)), ((), ()))

VMEM_BIG = 56 * 1024 * 1024
MM_ROWS = 2048
MM_DEPTH = 2048


def _dot(a, b, dims=NN, precision=None):
    return lax.dot_general(a, b, dims, precision=precision, preferred_element_type=F32)


def _mx(v):
    return v.astype(MXU_DTYPE)


def _pieces3(v):
    hi = _mx(v)
    r1 = v - hi.astype(F32)
    mid = _mx(r1)
    return hi, mid, _mx(r1 - mid.astype(F32))


def _dot_onehot(a, b, dims=NN, onehot="b"):
    if onehot == "b":
        return sum(_dot(p, _mx(b), dims) for p in _pieces3(a))
    return sum(_dot(_mx(a), p, dims) for p in _pieces3(b))


def _params(vmem=None):
    return pltpu.CompilerParams(vmem_limit_bytes=vmem) if vmem else None


def _all_gather(x, name):
    def body(x_ref, out_ref, send_sems, recv_sems, local_sem):
        xi, yi, ci = lax.axis_index("x"), lax.axis_index("y"), lax.axis_index("c")
        me, sibling = (xi, yi, ci), (xi, yi, 1 - ci)
        chips = [(1 - xi, yi), (xi, 1 - yi), (1 - xi, 1 - yi)]

        def slot(px, py, pc):
            return out_ref.at[4 * px + 2 * py + pc]

        def copy(k, block, to, src=None):
            return pltpu.make_async_remote_copy(
                src_ref=slot(*block) if src is None else src, dst_ref=slot(*block),
                send_sem=send_sems.at[k], recv_sem=recv_sems.at[k],
                device_id=to, device_id_type=pl.DeviceIdType.MESH)

        mine = pltpu.make_async_copy(x_ref, slot(*me), local_sem)
        mine.start()
        first = [copy(0, me, sibling, src=x_ref)]
        first += [copy(1 + j, me, (*chip, ci), src=x_ref) for j, chip in enumerate(chips)]
        for cp in first:
            cp.start()
        passed = [copy(4 + j, (*chip, ci), sibling) for j, chip in enumerate(chips)]
        for j, chip in enumerate(chips):
            copy(1 + j, (*chip, ci), me).wait_recv()
            passed[j].start()
        copy(0, sibling, me).wait_recv()
        for j, chip in enumerate(chips):
            copy(4 + j, (*chip, 1 - ci), me).wait_recv()
        for cp in first + passed:
            cp.wait_send()
        mine.wait()

    return pl.pallas_call(
        body, name=name,
        out_shape=jax.ShapeDtypeStruct((N_DEV,) + x.shape, x.dtype),
        in_specs=[pl.BlockSpec(memory_space=pl.ANY)],
        out_specs=pl.BlockSpec(memory_space=pl.ANY),
        scratch_shapes=[pltpu.SemaphoreType.DMA((7,)), pltpu.SemaphoreType.DMA((7,)),
                        pltpu.SemaphoreType.DMA],
    )(x)


EXCHANGE_PIECES = 1


def _direct_copies(scatter, x_ref, out_ref, send_sems, recv_sems, local_sems, n, piece=None):
    xi, yi, ci = lax.axis_index("x"), lax.axis_index("y"), lax.axis_index("c")
    me = 4 * xi + 2 * yi + ci

    def part(ref):
        return ref if piece is None else ref.at[piece]

    local = pltpu.make_async_copy(part(x_ref.at[me] if scatter else x_ref), part(out_ref.at[me]), local_sems.at[n])
    remote = []
    for k in range(1, N_DEV):
        px = 1 - xi if k & 4 else xi
        py = 1 - yi if k & 2 else yi
        pc = 1 - ci if k & 1 else ci
        remote.append(pltpu.make_async_remote_copy(
            src_ref=part(x_ref.at[4 * px + 2 * py + pc] if scatter else x_ref), dst_ref=part(out_ref.at[me]),
            send_sem=send_sems.at[7 * n + k - 1], recv_sem=recv_sems.at[7 * n + k - 1],
            device_id=(px, py, pc), device_id_type=pl.DeviceIdType.MESH))
    return local, remote


def _in_pieces(carry, steps):
    out = []
    for scatter, a in carry:
        body = a.shape[1:] if scatter else a.shape
        rows = int(np.prod(body[:-1])) if len(body) > 1 else 1
        align = SUBLANES * (4 // a.dtype.itemsize)
        pieces = 1
        while (pieces * 2 <= min(EXCHANGE_PIECES, steps // 2) and rows % (pieces * 2 * align) == 0):
            pieces *= 2
        shape = (pieces, rows // pieces, body[-1])
        out.append((scatter, a.reshape(((N_DEV,) if scatter else ()) + shape), pieces, max(1, (steps // 2) // pieces)))
    return out


def _exchange_shapes(carry):
    return [jax.ShapeDtypeStruct(c[1].shape if c[0] else (N_DEV,) + c[1].shape, c[1].dtype) for c in carry]


def _exchange_sems(carry):
    n = max(len(carry), 1)
    return [pltpu.SemaphoreType.DMA((7 * n,)), pltpu.SemaphoreType.DMA((7 * n,)), pltpu.SemaphoreType.DMA((n,))]


def _exchange_start(carry, srcs, dsts, sems, step=None):
    for n, entry in enumerate(carry):
        scatter = entry[0]
        pieces, stride = (entry[2], entry[3]) if len(entry) > 2 else (1, 1)
        local, remote = _direct_copies(scatter, srcs[n], dsts[n], *sems, n)
        if step is None:
            local.start()
            for cp in remote:
                cp.start()
            continue

        @pl.when(step == 0)
        def _():
            local.start()
            if pieces == 1:
                for cp in remote:
                    cp.start()

        if pieces > 1:
            @pl.when((step % stride == 0) & (step // stride < pieces))
            def _():
                for cp in _direct_copies(scatter, srcs[n], dsts[n], *sems, n, piece=step // stride)[1]:
                    cp.start()


def _exchange_wait(carry, srcs, dsts, sems):
    for n, entry in enumerate(carry):
        local, remote = _direct_copies(entry[0], srcs[n], dsts[n], *sems, n)
        for cp in remote:
            cp.wait()
        local.wait()


def _carrying(body, n_in, n_out, n_scratch, carry, grid):
    nc = len(carry)

    def wrapped(*refs):
        ins, srcs = refs[:n_in], refs[n_in:n_in + nc]
        outs = refs[n_in + nc:n_in + nc + n_out]
        dsts = refs[n_in + nc + n_out:n_in + 2 * nc + n_out]
        scratch = refs[n_in + 2 * nc + n_out:n_in + 2 * nc + n_out + n_scratch]
        sems = refs[n_in + 2 * nc + n_out + n_scratch:]
        step = 0
        for axis, extent in enumerate(grid):
            step = step * extent + pl.program_id(axis)
        _exchange_start(carry, srcs, dsts, sems, step=step)
        body(*ins, *outs, *scratch)

        @pl.when(step == int(np.prod(grid)) - 1)
        def _():
            _exchange_wait(carry, srcs, dsts, sems)

    return wrapped if nc else body


def _all_to_all(x, name):
    carry = [(True, x)]

    def body(x_ref, out_ref, *sems):
        _exchange_start(carry, [x_ref], [out_ref], sems)
        _exchange_wait(carry, [x_ref], [out_ref], sems)

    return pl.pallas_call(
        body, name=name,
        out_shape=jax.ShapeDtypeStruct(x.shape, x.dtype),
        in_specs=[pl.BlockSpec(memory_space=pl.ANY)],
        out_specs=pl.BlockSpec(memory_space=pl.ANY),
        scratch_shapes=_exchange_sems(carry),
    )(x)


def _mm(name, a, b, *, grid, a_spec, b_spec, dims, k_axis, outs, acc=None, extras=(), epi=None, vmem=None):
    nk = grid[k_axis]
    n_ex, n_out = len(extras), len(outs)

    def body(*refs):
        a_ref, b_ref = refs[0], refs[1]
        ex = refs[2:2 + n_ex]
        out = refs[2 + n_ex:2 + n_ex + n_out]

        def finish(val):
            if epi is None:
                out[0][...] = val.astype(out[0].dtype)
            else:
                epi(val, ex, out)

        part = _dot(a_ref[...], b_ref[...], dims)
        if nk == 1:
            finish(part)
        else:
            acc_ref = refs[2 + n_ex + n_out]
            k = pl.program_id(k_axis)

            @pl.when(k == 0)
            def _():
                acc_ref[...] = part

            @pl.when(k > 0)
            def _():
                acc_ref[...] += part

            @pl.when(k == nk - 1)
            def _():
                finish(acc_ref[...])

    return pl.pallas_call(
        body, name=name, grid=grid,
        in_specs=[a_spec, b_spec] + [s for _, s in extras],
        out_specs=[s for _, s in outs],
        out_shape=[o for o, _ in outs],
        scratch_shapes=[pltpu.VMEM(acc, F32)] if nk > 1 else [],
        compiler_params=_params(vmem),
    )(a, b, *[e for e, _ in extras])


def _tile(n, t):
    t = min(n, t)
    assert n % t == 0, (n, t)
    return t


def _mm_nn(name, a, b, out_dtype, *, addend=None, tm=MM_ROWS, tn=1024, tk=1024, epi=None, extras=(), outs=None):
    m, kk = a.shape
    n = b.shape[1]
    if addend is not None:
        tm = tm // 2
    tm, tn, tk = _tile(m, tm), _tile(n, tn), _tile(kk, tk)
    o_spec = pl.BlockSpec((tm, tn), lambda i, j, k: (i, j))
    if outs is None:
        outs = [(jax.ShapeDtypeStruct((m, n), out_dtype), o_spec)]
    extras = list(extras)
    if addend is not None:
        extras = [(addend, o_spec)] + extras

        def epi(val, ex, out):
            out[0][...] = (val + ex[0][...].astype(F32)).astype(out[0].dtype)

    res = _mm(name, a, b, grid=(m // tm, n // tn, kk // tk),
              a_spec=pl.BlockSpec((tm, tk), lambda i, j, k: (i, k)),
              b_spec=pl.BlockSpec((tk, tn), lambda i, j, k: (k, j)),
              dims=NN, k_axis=2, acc=(tm, tn), outs=outs, extras=extras, epi=epi, vmem=VMEM_BIG)
    return res[0] if len(res) == 1 else res


def _mm_tn(name, a, b, out_dtype=F32, *, tm=1024, tn=1024, tk=MM_DEPTH):
    kk, m = a.shape
    n = b.shape[1]
    tm, tn, tk = _tile(m, tm), _tile(n, tn), _tile(kk, tk)
    res = _mm(name, a, b, grid=(m // tm, n // tn, kk // tk),
              a_spec=pl.BlockSpec((tk, tm), lambda i, j, k: (k, i)),
              b_spec=pl.BlockSpec((tk, tn), lambda i, j, k: (k, j)),
              dims=TN, k_axis=2, acc=(tm, tn),
              outs=[(jax.ShapeDtypeStruct((m, n), out_dtype), pl.BlockSpec((tm, tn), lambda i, j, k: (i, j)))],
              vmem=VMEM_BIG)
    return res[0]


def _row_block(s):
    return _tile(s, 512)


def _modulate(x, pv, name):
    s, d = x.shape
    tb = _row_block(s)

    def body(x_ref, pv_ref, u_ref):
        u_ref[...] = _mx(x_ref[...] * pv_ref[0:1, :] + pv_ref[1:2, :])

    return pl.pallas_call(
        body, name=name, grid=(s // tb,),
        in_specs=[pl.BlockSpec((tb, d), lambda i: (i, 0)), pl.BlockSpec((8, d), lambda i: (0, 0))],
        out_specs=pl.BlockSpec((tb, d), lambda i: (i, 0)),
        out_shape=jax.ShapeDtypeStruct((s, d), MXU_DTYPE),
    )(x, pv)


def _ln_stats(r):
    mu = jnp.mean(r, axis=-1, keepdims=True)
    xc = r - mu
    var = jnp.mean(xc * xc, axis=-1, keepdims=True)
    rstd = lax.rsqrt(var + LN_EPS)
    return xc * rstd, rstd


def _ln_fwd(xin, y, pv, name):
    s, d = xin.shape
    tb = _row_block(s)

    def body(x_ref, y_ref, pv_ref, xo_ref, u_ref):
        r = ALPHA * x_ref[...] + pv_ref[0:1, :] * y_ref[...]
        xhat, _ = _ln_stats(r)
        xo = xhat * pv_ref[1:2, :] + pv_ref[2:3, :]
        xo_ref[...] = xo
        u_ref[...] = _mx(xo * pv_ref[3:4, :] + pv_ref[4:5, :])

    row = pl.BlockSpec((tb, d), lambda i: (i, 0))
    return pl.pallas_call(
        body, name=name, grid=(s // tb,),
        in_specs=[row, row, pl.BlockSpec((8, d), lambda i: (0, 0))],
        out_specs=[row, row],
        out_shape=[jax.ShapeDtypeStruct((s, d), F32), jax.ShapeDtypeStruct((s, d), MXU_DTYPE)],
    )(xin, y, pv)


def _ln_bwd(xin, y, pv, name, *, dxo=None, du=None, target=None):
    s, d = xin.shape
    tb = _row_block(s)
    nb = s // tb
    loss_mode = target is not None

    def body(*refs):
        if loss_mode:
            x_ref, y_ref, pv_ref, t_ref, dxin_ref, dy_ref, sums_ref = refs
        else:
            x_ref, y_ref, pv_ref, dxo_ref, du_ref, dxin_ref, dy_ref, sums_ref = refs
        i = pl.program_id(0)

        @pl.when(i == 0)
        def _():
            sums_ref[...] = jnp.zeros_like(sums_ref)

        yv = y_ref[...]
        r = ALPHA * x_ref[...] + pv_ref[0:1, :] * yv
        xhat, rstd = _ln_stats(r)
        xo = xhat * pv_ref[1:2, :] + pv_ref[2:3, :]
        if loss_mode:
            diff = xo - t_ref[...]
            dxo_v = diff * (1.0 / d)
            sums_ref[5:6, :] += jnp.sum(diff * diff, axis=0, keepdims=True) * (0.5 / d)
        else:
            duv = du_ref[...]
            dxo_v = dxo_ref[...] + duv * pv_ref[3:4, :]
            sums_ref[0:1, :] += jnp.sum(duv * xo, axis=0, keepdims=True)
            sums_ref[1:2, :] += jnp.sum(duv, axis=0, keepdims=True)
        sums_ref[2:3, :] += jnp.sum(dxo_v * xhat, axis=0, keepdims=True)
        sums_ref[3:4, :] += jnp.sum(dxo_v, axis=0, keepdims=True)
        dxh = dxo_v * pv_ref[1:2, :]
        dr = rstd * (dxh - jnp.mean(dxh, axis=-1, keepdims=True)
                     - xhat * jnp.mean(dxh * xhat, axis=-1, keepdims=True))
        sums_ref[4:5, :] += jnp.sum(dr * yv, axis=0, keepdims=True)
        dxin_ref[...] = ALPHA * dr
        dy_ref[...] = _mx(pv_ref[0:1, :] * dr)
        if loss_mode:
            @pl.when(i == nb - 1)
            def _():
                sums_ref[5:6, :] = jnp.broadcast_to(jnp.sum(sums_ref[5:6, :], axis=-1, keepdims=True), (1, d))

    row = pl.BlockSpec((tb, d), lambda i: (i, 0))
    par = pl.BlockSpec((8, d), lambda i: (0, 0))
    ins = [xin, y, pv] + ([target] if loss_mode else [dxo, du])
    return pl.pallas_call(
        body, name=name, grid=(nb,),
        in_specs=[row, row, par] + [row] * (len(ins) - 3),
        out_specs=[row, row, par],
        out_shape=[jax.ShapeDtypeStruct((s, d), F32), jax.ShapeDtypeStruct((s, d), MXU_DTYPE),
                   jax.ShapeDtypeStruct((8, d), F32)],
    )(*ins)


def _mod_bwd(dx_direct, du, x, pv, name):
    s, d = x.shape
    tb = _row_block(s)

    def body(dxd_ref, du_ref, x_ref, pv_ref, dx_ref, sums_ref):
        @pl.when(pl.program_id(0) == 0)
        def _():
            sums_ref[...] = jnp.zeros_like(sums_ref)

        duv = du_ref[...]
        dx_ref[...] = dxd_ref[...] + duv * pv_ref[0:1, :]
        sums_ref[0:1, :] += jnp.sum(duv * x_ref[...], axis=0, keepdims=True)
        sums_ref[1:2, :] += jnp.sum(duv, axis=0, keepdims=True)

    row = pl.BlockSpec((tb, d), lambda i: (i, 0))
    par = pl.BlockSpec((8, d), lambda i: (0, 0))
    return pl.pallas_call(
        body, name=name, grid=(s // tb,),
        in_specs=[row, row, row, par], out_specs=[row, par],
        out_shape=[jax.ShapeDtypeStruct((s, d), F32), jax.ShapeDtypeStruct((8, d), F32)],
    )(dx_direct, du, x, pv)


def _mlp_fwd(u, w1, w2, tag):
    s = u.shape[0]

    def epi(val, ex, out):
        out[0][...] = _mx(val)
        out[1][...] = _mx(jnp.square(jnp.maximum(val, 0.0)))

    tm, tn = _tile(s, MM_ROWS), 1024
    spec = pl.BlockSpec((tm, tn), lambda i, j, k: (i, j))
    shp = jax.ShapeDtypeStruct((s, D_FF), MXU_DTYPE)
    h, a = _mm_nn(f"mlp_up{tag}", u, w1, None, epi=epi, outs=[(shp, spec), (shp, spec)], tn=tn)
    y = _mm_nn(f"mlp_down{tag}", a, w2, F32)
    return y, (h, a)


def _mlp_bwd(dy, u, h, a, w1t, w2t, tag):
    s = u.shape[0]
    tm, tn = _tile(s, MM_ROWS), 1024
    spec = pl.BlockSpec((tm, tn), lambda i, j, k: (i, j))

    def epi(val, ex, out):
        out[0][...] = _mx(val * (2.0 * jnp.maximum(ex[0][...].astype(F32), 0.0)))

    dh = _mm_nn(f"mlp_dh{tag}", dy, w2t, None, epi=epi, extras=[(h, spec)],
                outs=[(jax.ShapeDtypeStruct((s, D_FF), MXU_DTYPE), spec)], tn=tn)
    du = _mm_nn(f"mlp_du{tag}", dh, w1t, F32)
    dw2 = _mm_tn(f"mlp_dw2{tag}", a, dy)
    dw1 = _mm_tn(f"mlp_dw1{tag}", u, dh)
    return du, dw1, dw2


FOX_T = 1024
BIAS_Q = (64, 65, 66)
BIAS_K = (67, 68, 69)
SKIP_MARGIN = 110.0
ONES_V = 64

def _fox_constants():
    selq = np.zeros((FOX_HEADS, 512, LANES), np.float32)
    selk = np.zeros((FOX_HEADS, 512, LANES), np.float32)
    selv = np.zeros((2, LANES, LANES), np.float32)
    put = np.zeros((2, 2, LANES, LANES), np.float32)
    for h in range(FOX_HEADS):
        off = FOX_HEAD_DIM * (h % 2)
        for dd in range(FOX_HEAD_DIM):
            selq[h, off + dd, dd] = FOX_HEAD_DIM ** -0.5
            selk[h, off + dd, dd] = 1.0
        for piece in range(3):
            selq[h, LANES * (1 + piece) + h, BIAS_Q[piece]] = 1.0
            selk[h, LANES * (1 + piece) + h, BIAS_K[piece]] = -1.0
    for par in range(2):
        for dd in range(FOX_HEAD_DIM):
            selv[par, FOX_HEAD_DIM * par + dd, dd] = 1.0
            put[par, 0, dd, FOX_HEAD_DIM * par + dd] = FOX_HEAD_DIM ** -0.5
            put[par, 1, dd, FOX_HEAD_DIM * par + dd] = 1.0
    return selq, selk, selv, put


def _fox_prep(qkv, f, bf, carry=()):
    s = qkv.shape[0]
    t = _tile(s, FOX_T)
    nb = s // t
    selq, selk, selv, _ = _fox_constants()
    whole = _exchange_shapes(list(carry))
    carry = _in_pieces(list(carry), nb * FOX_HEADS)
    anywhere = pl.BlockSpec(memory_space=pl.ANY)

    def body(q_ref, k_ref, v_ref, f_ref, bf_ref, selq_ref, selk_ref, selv_ref,
             qa_ref, qat_ref, ka_ref, kat_ref, va_ref, vat_ref, stats_ref, parts_ref, carry_ref, cum_ref):
        i, h = pl.program_id(0), pl.program_id(1)
        lane = lax.broadcasted_iota(jnp.int32, (1, LANES), 1)

        @pl.when(h == 0)
        def _():
            @pl.when(i == 0)
            def _():
                carry_ref[...] = jnp.zeros_like(carry_ref)

            lf = jnp.where(lane < FOX_HEADS, jax.nn.log_sigmoid(f_ref[...] + bf_ref[0:1, :]), 0.0)
            tri = (lax.broadcasted_iota(jnp.int32, (t, t), 0) >= lax.broadcasted_iota(jnp.int32, (t, t), 1)).astype(F32)
            cum = _dot_onehot(tri, lf, onehot="a") + carry_ref[0:1, :]
            carry_ref[0:1, :] = cum[t - 1:t, :]
            cum_ref[...] = cum
            hi = _mx(cum)
            r1 = cum - hi.astype(F32)
            mid = _mx(r1)
            parts_ref[:, 0:LANES] = hi
            parts_ref[:, LANES:2 * LANES] = mid
            parts_ref[:, 2 * LANES:3 * LANES] = _mx(r1 - mid.astype(F32))

        parts = parts_ref[...]
        qa = _dot(jnp.concatenate([q_ref[...], parts], axis=1), selq_ref[...])
        qa = qa + jnp.where((lane >= BIAS_K[0]) & (lane <= BIAS_K[2]), 1.0, 0.0)
        ka = _dot(jnp.concatenate([k_ref[...], parts], axis=1), selk_ref[...])
        ka = ka + jnp.where((lane >= BIAS_Q[0]) & (lane <= BIAS_Q[2]), 1.0, 0.0)
        va = _dot(v_ref[...], selv_ref[...]) + jnp.where(lane == ONES_V, 1.0, 0.0)
        qa_ref[...] = _mx(qa)
        qat_ref[...] = _mx(qa.T)
        ka_ref[...] = _mx(ka)
        kat_ref[...] = _mx(ka.T)
        va_ref[...] = _mx(va)
        vat_ref[...] = _mx(va.T)

        def longest(rows_):
            sq = jnp.where(lane < FOX_HEAD_DIM, rows_ * rows_, 0.0)
            return jnp.sqrt(jnp.max(jnp.sum(sq, axis=1, keepdims=True), axis=0, keepdims=True))

        mine = lane == h
        cum = cum_ref[...]
        top = jnp.max(jnp.max(jnp.where(mine, cum, -jnp.inf), axis=1, keepdims=True), axis=0, keepdims=True)
        low = jnp.min(jnp.min(jnp.where(mine, cum, jnp.inf), axis=1, keepdims=True), axis=0, keepdims=True)
        row = lax.broadcasted_iota(jnp.int32, (8, LANES), 0)
        stats_ref[...] = jnp.where(row == 0, longest(qa), jnp.where(row == 1, longest(ka),
                                                                    jnp.where(row == 2, top, low)))

    rows = jax.ShapeDtypeStruct((FOX_HEADS, nb, t, LANES), MXU_DTYPE)
    cols = jax.ShapeDtypeStruct((FOX_HEADS, nb, LANES, t), MXU_DTYPE)
    rspec = pl.BlockSpec((None, None, t, LANES), lambda i, h: (h, i, 0, 0))
    cspec = pl.BlockSpec((None, None, LANES, t), lambda i, h: (h, i, 0, 0))
    npair = FOX_HEADS // 2
    res = pl.pallas_call(
        _carrying(body, 8, 7, 3, carry, (nb, FOX_HEADS)), name="fox_prep", grid=(nb, FOX_HEADS),
        in_specs=[pl.BlockSpec((t, LANES), lambda i, h: (i, h // 2)),
                  pl.BlockSpec((t, LANES), lambda i, h: (i, npair + h // 2)),
                  pl.BlockSpec((t, LANES), lambda i, h: (i, 2 * npair + h // 2)),
                  pl.BlockSpec((t, LANES), lambda i, h: (i, 0)),
                  pl.BlockSpec((8, LANES), lambda i, h: (0, 0)),
                  pl.BlockSpec((None, 512, LANES), lambda i, h: (h, 0, 0)),
                  pl.BlockSpec((None, 512, LANES), lambda i, h: (h, 0, 0)),
                  pl.BlockSpec((None, LANES, LANES), lambda i, h: (h % 2, 0, 0))] + [anywhere] * len(carry),
        out_specs=[rspec, cspec, rspec, cspec, rspec, cspec,
                   pl.BlockSpec((None, None, 8, LANES), lambda i, h: (h, i, 0, 0))] + [anywhere] * len(carry),
        out_shape=[rows, cols, rows, cols, rows, cols, jax.ShapeDtypeStruct((FOX_HEADS, nb, 8, LANES), F32)]
        + _exchange_shapes(carry),
        scratch_shapes=[pltpu.VMEM((t, 3 * LANES), MXU_DTYPE), pltpu.VMEM((8, LANES), F32),
                        pltpu.VMEM((t, LANES), F32)] + (_exchange_sems(carry) if carry else []),
        compiler_params=_params(VMEM_BIG),
    )(qkv, qkv, qkv, f, bf, _mx(jnp.asarray(selq)), _mx(jnp.asarray(selk)), _mx(jnp.asarray(selv)),
      *[c[1] for c in carry])
    return (*res[:7], [r.reshape(w.shape) for r, w in zip(res[7:], whole)])


def _fox_active(stats):
    qn, kn, top, low = (stats[:, :, r, 0] for r in range(4))
    nb = qn.shape[1]
    gap = qn[:, :, None] * kn[:, None, :] + top[:, :, None] - low[:, None, :] + (qn * kn)[:, :, None]
    keep = (gap > -SKIP_MARGIN) | jnp.eye(nb, dtype=bool)[None]
    return jnp.where(keep, 1.0, 0.0).astype(F32).reshape(qn.shape[0], nb * nb)


def _causal_allow(t):
    return lax.broadcasted_iota(jnp.int32, (t, t), 0) <= lax.broadcasted_iota(jnp.int32, (t, t), 1)


def _fox_attn_fwd(qat, ka, vat, active, carry=()):
    heads, nb, _, t = qat.shape
    whole = _exchange_shapes(list(carry))
    carry = _in_pieces(list(carry), heads * nb)
    nc = len(carry)

    def body(act_ref, qat_ref, ka_ref, vat_ref, *rest):
        srcs, (ot_ref, lse_ref), dsts = rest[:nc], rest[nc:nc + 2], rest[nc + 2:2 * nc + 2]
        acc_ref, m_ref, kbuf_ref, vbuf_ref, fsems = rest[2 * nc + 2:2 * nc + 7]
        sems = rest[2 * nc + 7:]
        h, i = pl.program_id(0), pl.program_id(1)

        if nc:
            _exchange_start(carry, srcs, dsts, sems, step=h * nb + i)
        m_ref[...] = jnp.full_like(m_ref, -jnp.inf)
        acc_ref[...] = jnp.zeros_like(acc_ref)

        def key_blocks(j, slot):
            return [pltpu.make_async_copy(ka_ref.at[h, j], kbuf_ref.at[slot], fsems.at[0, slot]),
                    pltpu.make_async_copy(vat_ref.at[h, j], vbuf_ref.at[slot], fsems.at[1, slot])]

        def runs(j):
            return (j == i) | (act_ref[h, i * nb + j] > 0.5)

        def step(slot, diagonal):
            st = _dot(kbuf_ref[slot], qat_ref[...])
            if diagonal:
                st = jnp.where(_causal_allow(t), st, -jnp.inf)
            m_old = m_ref[...]
            m_new = jnp.maximum(m_old, jnp.max(st, axis=0, keepdims=True))
            pt = jnp.exp(st - m_new)
            acc_ref[...] = acc_ref[...] * jnp.exp(m_old - m_new) + _dot(vbuf_ref[slot], _mx(pt))
            m_ref[...] = m_new

        @pl.when(runs(0))
        def _():
            for cp in key_blocks(0, 0):
                cp.start()

        def earlier(j, c):
            slot = j % 2

            @pl.when(runs(j + 1))
            def _():
                for cp in key_blocks(j + 1, 1 - slot):
                    cp.start()

            @pl.when(runs(j))
            def _():
                for cp in key_blocks(j, slot):
                    cp.wait()
                step(slot, False)

            return c

        lax.fori_loop(0, i, earlier, 0)
        for cp in key_blocks(i, i % 2):
            cp.wait()
        step(i % 2, True)
        acc = acc_ref[...]
        denom = acc[ONES_V:ONES_V + 1, :]
        ot_ref[...] = _mx(acc / denom)
        lse_ref[...] = m_ref[...] + jnp.log(denom)

        if nc:
            @pl.when((h == heads - 1) & (i == nb - 1))
            def _():
                _exchange_wait(carry, srcs, dsts, sems)

    anywhere = pl.BlockSpec(memory_space=pl.ANY)
    qspec = pl.BlockSpec((None, None, LANES, t), lambda h, i: (h, i, 0, 0))
    res = pl.pallas_call(
        body, name="fox_attn_fwd", grid=(heads, nb),
        in_specs=[pl.BlockSpec(memory_space=pltpu.SMEM), qspec, anywhere, anywhere] + [anywhere] * nc,
        out_specs=[qspec, pl.BlockSpec((None, None, 1, t), lambda h, i: (h, i, 0, 0))] + [anywhere] * nc,
        out_shape=[jax.ShapeDtypeStruct((heads, nb, LANES, t), MXU_DTYPE),
                   jax.ShapeDtypeStruct((heads, nb, 1, t), F32)] + _exchange_shapes(carry),
        scratch_shapes=[pltpu.VMEM((LANES, t), F32), pltpu.VMEM((1, t), F32), pltpu.VMEM((2, t, LANES), MXU_DTYPE),
                        pltpu.VMEM((2, LANES, t), MXU_DTYPE), pltpu.SemaphoreType.DMA((2, 2))]
        + (_exchange_sems(carry) if nc else []),
        compiler_params=_params(VMEM_BIG),
    )(active, qat, ka, vat, *[c[1] for c in carry])
    return res[0], res[1], [r.reshape(w.shape) for r, w in zip(res[2:], whole)]


def _fox_attn_bwd(qa, qat, ka, kat, va, ot, lse, do, dot_, active, carry=()):
    heads, nb, t, _ = qa.shape
    whole = _exchange_shapes(list(carry))
    carry = _in_pieces(list(carry), heads * nb)
    nc = len(carry)

    def body(act_ref, qa_ref, qat_ref, ka_ref, kat_ref, va_ref, ot_ref, lse_ref, do_ref, dot_ref, *rest):
        srcs, (dqt_ref, dka_ref, dva_ref), dsts = rest[:nc], rest[nc:nc + 3], rest[nc + 3:2 * nc + 3]
        rows_ref, cols_ref, lseb_ref, fsems = rest[2 * nc + 3:2 * nc + 7]
        sems = rest[2 * nc + 7:]
        h, j = pl.program_id(0), pl.program_id(1)

        if nc:
            _exchange_start(carry, srcs, dsts, sems, step=h * nb + j)

        @pl.when(j == 0)
        def _():
            dqt_ref[...] = jnp.zeros_like(dqt_ref)

        def query_blocks(i, slot):
            cps = [pltpu.make_async_copy(src.at[h, i], rows_ref.at[slot, n], fsems.at[n, slot])
                   for n, src in enumerate((qa_ref, do_ref))]
            cps += [pltpu.make_async_copy(src.at[h, i], cols_ref.at[slot, n], fsems.at[2 + n, slot])
                    for n, src in enumerate((qat_ref, ot_ref, dot_ref))]
            return cps + [pltpu.make_async_copy(lse_ref.at[h, i], lseb_ref.at[slot], fsems.at[5, slot])]

        def runs(i):
            return (i == j) | (act_ref[h, i * nb + j] > 0.5)

        def step(i, slot, diagonal):
            st = _dot(ka_ref[...], cols_ref[slot, 0])
            dot_v = cols_ref[slot, 2]
            delta = jnp.sum(cols_ref[slot, 1].astype(F32) * dot_v.astype(F32), axis=0, keepdims=True)
            pt = jnp.exp(st - lseb_ref[slot])
            if diagonal:
                pt = jnp.where(_causal_allow(t), pt, 0.0)
            dsm = _mx(pt * (_dot(va_ref[...], dot_v) - delta))
            upd_v = _dot(_mx(pt), rows_ref[slot, 1])
            upd_k = _dot(dsm, rows_ref[slot, 0])
            if diagonal:
                dva_ref[...] = upd_v
                dka_ref[...] = upd_k
            else:
                dva_ref[...] += upd_v
                dka_ref[...] += upd_k
            dqt_ref[i] += _dot(kat_ref[...], dsm)

        def visit(i, slot, diagonal):
            nxt = jnp.minimum(i + 1, nb - 1)

            @pl.when((i + 1 < nb) & runs(nxt))
            def _():
                for cp in query_blocks(nxt, 1 - slot):
                    cp.start()

            @pl.when(runs(i))
            def _():
                for cp in query_blocks(i, slot):
                    cp.wait()
                step(i, slot, diagonal)

        for cp in query_blocks(j, 0):
            cp.start()
        visit(j, 0, True)

        def later(i, c):
            visit(i, (i - j) % 2, False)
            return c

        lax.fori_loop(j + 1, nb, later, 0)

        if nc:
            @pl.when((h == heads - 1) & (j == nb - 1))
            def _():
                _exchange_wait(carry, srcs, dsts, sems)

    def at_k(shape):
        return pl.BlockSpec((None, None) + shape, lambda h, j: (h, j, 0, 0))

    anywhere = pl.BlockSpec(memory_space=pl.ANY)
    res = pl.pallas_call(
        body, name="fox_attn_bwd", grid=(heads, nb),
        in_specs=[pl.BlockSpec(memory_space=pltpu.SMEM),
                  anywhere, anywhere, at_k((t, LANES)), at_k((LANES, t)), at_k((t, LANES)),
                  anywhere, anywhere, anywhere, anywhere] + [anywhere] * nc,
        out_specs=[pl.BlockSpec((None, nb, LANES, t), lambda h, j: (h, 0, 0, 0)),
                   at_k((t, LANES)), at_k((t, LANES))] + [anywhere] * nc,
        out_shape=[jax.ShapeDtypeStruct((heads, nb, LANES, t), F32),
                   jax.ShapeDtypeStruct((heads, nb, t, LANES), F32),
                   jax.ShapeDtypeStruct((heads, nb, t, LANES), F32)] + _exchange_shapes(carry),
        scratch_shapes=[pltpu.VMEM((2, 2, t, LANES), MXU_DTYPE), pltpu.VMEM((2, 3, LANES, t), MXU_DTYPE),
                        pltpu.VMEM((2, 1, t), F32), pltpu.SemaphoreType.DMA((6, 2))]
        + (_exchange_sems(carry) if nc else []),
        compiler_params=_params(VMEM_BIG),
    )(active, qa, qat, ka, kat, va, ot, lse, do, dot_, *[c[1] for c in carry])
    return res[0], res[1], res[2], [r.reshape(w.shape) for r, w in zip(res[3:], whole)]


def _fox_post(dqt, dka, dva, f, bf, carry=()):
    heads, nb, t, _ = dka.shape
    s = nb * t
    _, _, _, put = _fox_constants()
    whole = _exchange_shapes(list(carry))
    carry = _in_pieces(list(carry), nb * heads)
    anywhere = pl.BlockSpec(memory_space=pl.ANY)

    def body(dqt_ref, dka_ref, dva_ref, f_ref, bf_ref, put_ref, dq_ref, dk_ref, dv_ref, df_ref, sums_ref,
             dc_ref, carry_ref):
        i, h = pl.program_id(0), pl.program_id(1)

        @pl.when((i == 0) & (h == 0))
        def _():
            carry_ref[...] = jnp.zeros_like(carry_ref)
            sums_ref[...] = jnp.zeros_like(sums_ref)

        @pl.when(h == 0)
        def _():
            dc_ref[...] = jnp.zeros_like(dc_ref)

        dqt_v = dqt_ref[...]
        dka_v = dka_ref[...]
        term_q = _dot(_mx(dqt_v), put_ref[0], TN)
        term_k = _dot(_mx(dka_v), put_ref[1])
        term_v = _dot(_mx(dva_ref[...]), put_ref[1])

        @pl.when(h % 2 == 0)
        def _():
            dq_ref[...] = _mx(term_q)
            dk_ref[...] = _mx(term_k)
            dv_ref[...] = _mx(term_v)

        @pl.when(h % 2 == 1)
        def _():
            dq_ref[...] += _mx(term_q)
            dk_ref[...] += _mx(term_k)
            dv_ref[...] += _mx(term_v)

        dcum = dqt_v[BIAS_Q[0]:BIAS_Q[0] + 1, :] - dka_v.T[BIAS_K[0]:BIAS_K[0] + 1, :]
        head_row = lax.broadcasted_iota(jnp.int32, (heads, 1), 0) == h
        dc_ref[...] += jnp.where(head_row, dcum, 0.0)

        @pl.when(h == heads - 1)
        def _():
            later = (lax.broadcasted_iota(jnp.int32, (t, t), 0) >= lax.broadcasted_iota(jnp.int32, (t, t), 1)).astype(F32)
            dlf_t = _dot_onehot(dc_ref[...], later) + carry_ref[:, 0:1]
            carry_ref[...] = jnp.broadcast_to(dlf_t[:, 0:1], carry_ref.shape)
            dlf = jnp.concatenate([dlf_t, jnp.zeros((LANES - heads, t), F32)], axis=0).T
            lane = lax.broadcasted_iota(jnp.int32, (1, LANES), 1)
            df = jnp.where(lane < heads, dlf * jax.nn.sigmoid(-(f_ref[...] + bf_ref[0:1, :])), 0.0)
            df_ref[...] = _mx(df)
            sums_ref[0:1, :] += jnp.sum(df, axis=0, keepdims=True)

    rev = lambda i: nb - 1 - i
    pair_spec = pl.BlockSpec((t, LANES), lambda i, h: (rev(i), h // 2))
    blk = pl.BlockSpec((t, LANES), lambda i, h: (rev(i), 0))
    hd = jax.ShapeDtypeStruct((s, D_MODEL), MXU_DTYPE)
    res = pl.pallas_call(
        _carrying(body, 6, 5, 2, carry, (nb, heads)), name="fox_post", grid=(nb, heads),
        in_specs=[pl.BlockSpec((None, None, LANES, t), lambda i, h: (h, rev(i), 0, 0)),
                  pl.BlockSpec((None, None, t, LANES), lambda i, h: (h, rev(i), 0, 0)),
                  pl.BlockSpec((None, None, t, LANES), lambda i, h: (h, rev(i), 0, 0)),
                  blk, pl.BlockSpec((8, LANES), lambda i, h: (0, 0)),
                  pl.BlockSpec((None, 2, LANES, LANES), lambda i, h: (h % 2, 0, 0, 0))] + [anywhere] * len(carry),
        out_specs=[pair_spec, pair_spec, pair_spec, blk, pl.BlockSpec((8, LANES), lambda i, h: (0, 0))]
        + [anywhere] * len(carry),
        out_shape=[hd, hd, hd, jax.ShapeDtypeStruct((s, LANES), MXU_DTYPE), jax.ShapeDtypeStruct((8, LANES), F32)]
        + _exchange_shapes(carry),
        scratch_shapes=[pltpu.VMEM((heads, t), F32), pltpu.VMEM((heads, LANES), F32)]
        + (_exchange_sems(carry) if carry else []),
        compiler_params=_params(VMEM_BIG),
    )(dqt, dka, dva, f, bf, _mx(jnp.asarray(put)), *[c[1] for c in carry])
    return (*res[:5], [r.reshape(w.shape) for r, w in zip(res[5:], whole)])


def _fox_weights(w_in, w_o):
    wqkv = w_in[:, :3 * D_MODEL]
    wf = jnp.pad(w_in[:, 3 * D_MODEL:], ((0, 0), (0, LANES - FOX_HEADS)))
    wo_heads = w_o.reshape(FOX_HEADS, FOX_HEAD_DIM, D_MODEL)
    wo_a = jnp.pad(wo_heads, ((0, 0), (0, LANES - FOX_HEAD_DIM), (0, 0)))
    wo_rows = wo_a.reshape(FOX_HEADS * LANES, D_MODEL)
    return dict(wqkv=wqkv, wf=wf, wqkv_t=wqkv.T, wf_t=wf.T, wo_rows=wo_rows, wo_rows_t=wo_rows.T)


def _fox_out(ot, wo_rows):
    heads, nb, _, t = ot.shape

    def body(ot_ref, w_ref, y_ref):
        y_ref[...] = _dot(ot_ref[...].reshape(heads * LANES, t), w_ref[...], TN)

    return pl.pallas_call(
        body, name="fox_out", grid=(nb,),
        in_specs=[pl.BlockSpec((heads, None, LANES, t), lambda i: (0, i, 0, 0)),
                  pl.BlockSpec((heads * LANES, D_MODEL), lambda i: (0, 0))],
        out_specs=pl.BlockSpec((t, D_MODEL), lambda i: (i, 0)),
        out_shape=jax.ShapeDtypeStruct((nb * t, D_MODEL), F32),
        compiler_params=_params(VMEM_BIG),
    )(ot, wo_rows)


def _fox_do(dy, wo_rows_t, nb, t):
    heads = FOX_HEADS

    def body(dy_ref, w_ref, do_ref, dot_ref):
        val = _dot(dy_ref[...], w_ref[...])
        for h in range(heads):
            blk = val[:, h * LANES:(h + 1) * LANES]
            do_ref[h] = _mx(blk)
            dot_ref[h] = _mx(blk.T)

    return pl.pallas_call(
        body, name="fox_do", grid=(nb,),
        in_specs=[pl.BlockSpec((t, D_MODEL), lambda i: (i, 0)),
                  pl.BlockSpec((D_MODEL, heads * LANES), lambda i: (0, 0))],
        out_specs=[pl.BlockSpec((heads, None, t, LANES), lambda i: (0, i, 0, 0)),
                   pl.BlockSpec((heads, None, LANES, t), lambda i: (0, i, 0, 0))],
        out_shape=[jax.ShapeDtypeStruct((heads, nb, t, LANES), MXU_DTYPE),
                   jax.ShapeDtypeStruct((heads, nb, LANES, t), MXU_DTYPE)],
        compiler_params=_params(VMEM_BIG),
    )(dy, wo_rows_t)


def _fox_dwo(ot, dy):
    heads, nb, _, t = ot.shape

    def body(ot_ref, dy_ref, o_ref):
        part = _dot(ot_ref[...].reshape(heads * LANES, t), dy_ref[...])

        @pl.when(pl.program_id(0) == 0)
        def _():
            o_ref[...] = part

        @pl.when(pl.program_id(0) > 0)
        def _():
            o_ref[...] += part

    return pl.pallas_call(
        body, name="fox_dwo", grid=(nb,),
        in_specs=[pl.BlockSpec((heads, None, LANES, t), lambda i: (0, i, 0, 0)),
                  pl.BlockSpec((t, D_MODEL), lambda i: (i, 0))],
        out_specs=pl.BlockSpec((heads * LANES, D_MODEL), lambda i: (0, 0)),
        out_shape=jax.ShapeDtypeStruct((heads * LANES, D_MODEL), F32),
        compiler_params=_params(VMEM_BIG),
    )(ot, dy)


def _fox_fwd(u, w, bf, carry=()):
    qkv = _mm_nn("fox_qkv", u, w["wqkv"], MXU_DTYPE)
    f = _mm_nn("fox_f", u, w["wf"], F32)
    qa, qat, ka, kat, va, vat, stats, carried = _fox_prep(qkv, f, bf, carry)
    ot, lse, _ = _fox_attn_fwd(qat, ka, vat, _fox_active(stats))
    y = _fox_out(ot, w["wo_rows"])
    return y, dict(f=f, qa=qa, qat=qat, ka=ka, kat=kat, va=va, ot=ot, lse=lse, stats=stats), carried


def _fox_bwd(dy, u, w, bf, res, carry=()):
    heads, nb, t, _ = res["qa"].shape
    do, dot_ = _fox_do(dy, w["wo_rows_t"], nb, t)
    dwo_a = _fox_dwo(res["ot"], dy).reshape(heads, LANES, D_MODEL)
    dqt, dka, dva, _ = _fox_attn_bwd(res["qa"], res["qat"], res["ka"], res["kat"], res["va"], res["ot"],
                                     res["lse"], do, dot_, _fox_active(res["stats"]))
    dq, dk, dv, df, sums, carried = _fox_post(dqt, dka, dva, res["f"], bf, carry)
    wt = w["wqkv_t"]
    du = _mm_nn("fox_du_q", dq, wt[:D_MODEL], F32)
    du = _mm_nn("fox_du_k", dk, wt[D_MODEL:2 * D_MODEL], F32, addend=du)
    du = _mm_nn("fox_du_v", dv, wt[2 * D_MODEL:], F32, addend=du)
    du = _mm_nn("fox_du_f", df, w["wf_t"], F32, addend=du)
    dw_in = jnp.concatenate(
        [_mm_tn("fox_dw_q", u, dq), _mm_tn("fox_dw_k", u, dk), _mm_tn("fox_dw_v", u, dv),
         _mm_tn("fox_dw_f", u, df)[:, :FOX_HEADS]], axis=1)
    dw_o = dwo_a[:, :FOX_HEAD_DIM, :].reshape(D_MODEL, D_MODEL)
    return du, dw_in, dw_o, sums, carried


def _dsilu(v):
    sg = jax.nn.sigmoid(v)
    return sg * (1.0 + v * (1.0 - sg))


def _conv_taps(scr_ref, w_ref, rows, base):
    acc = None
    for k in range(SSM_CONV):
        term = scr_ref[pl.ds(base - (SSM_CONV - 1) + k, rows), :] * w_ref[k:k + 1, :]
        acc = term if acc is None else acc + term
    return acc


def _conv_fwd(zx, cw, cb):
    s = zx.shape[0]
    tb = _tile(s, 512)
    half = SSM_CONV_DIM // 2
    hb = tb // SUBLANES

    def body(x_ref, halo_ref, w_ref, b_ref, o_ref, scr_ref):
        i = pl.program_id(0)
        scr_ref[pl.ds(0, SUBLANES), :] = jnp.where(i > 0, halo_ref[...], 0.0)
        scr_ref[pl.ds(SUBLANES, tb), :] = x_ref[...]
        o_ref[...] = jax.nn.silu(_conv_taps(scr_ref, w_ref, tb, SUBLANES) + b_ref[0:1, :])

    return pl.pallas_call(
        body, name="ssd_conv_fwd", grid=(s // tb, 2),
        in_specs=[pl.BlockSpec((tb, half), lambda i, j: (i, 1 + j)),
                  pl.BlockSpec((SUBLANES, half), lambda i, j: (jnp.maximum(i * hb - 1, 0), 1 + j)),
                  pl.BlockSpec((8, half), lambda i, j: (0, j)),
                  pl.BlockSpec((8, half), lambda i, j: (0, j))],
        out_specs=pl.BlockSpec((tb, half), lambda i, j: (i, j)),
        out_shape=jax.ShapeDtypeStruct((s, SSM_CONV_DIM), F32),
        scratch_shapes=[pltpu.VMEM((tb + SUBLANES, half), F32)],
    )(zx, zx, cw, cb)


def _conv_bwd_pre(zx, dxc, cw, cb):
    s = zx.shape[0]
    tb = _tile(s, 512)
    half = SSM_CONV_DIM // 2
    hb = tb // SUBLANES

    def body(x_ref, halo_ref, d_ref, w_ref, b_ref, o_ref, sums_ref, scr_ref):
        i = pl.program_id(1)

        @pl.when(i == 0)
        def _():
            sums_ref[...] = jnp.zeros_like(sums_ref)

        scr_ref[pl.ds(0, SUBLANES), :] = jnp.where(i > 0, halo_ref[...], 0.0)
        scr_ref[pl.ds(SUBLANES, tb), :] = x_ref[...]
        pre = _conv_taps(scr_ref, w_ref, tb, SUBLANES) + b_ref[0:1, :]
        dpre = d_ref[...] * _dsilu(pre)
        o_ref[...] = dpre
        for k in range(SSM_CONV):
            shifted = scr_ref[pl.ds(SUBLANES - (SSM_CONV - 1) + k, tb), :]
            sums_ref[k:k + 1, :] += jnp.sum(dpre * shifted, axis=0, keepdims=True)
        sums_ref[SSM_CONV:SSM_CONV + 1, :] += jnp.sum(dpre, axis=0, keepdims=True)

    return pl.pallas_call(
        body, name="ssd_conv_bwd_pre", grid=(2, s // tb),
        in_specs=[pl.BlockSpec((tb, half), lambda j, i: (i, 1 + j)),
                  pl.BlockSpec((SUBLANES, half), lambda j, i: (jnp.maximum(i * hb - 1, 0), 1 + j)),
                  pl.BlockSpec((tb, half), lambda j, i: (i, j)),
                  pl.BlockSpec((8, half), lambda j, i: (0, j)),
                  pl.BlockSpec((8, half), lambda j, i: (0, j))],
        out_specs=[pl.BlockSpec((tb, half), lambda j, i: (i, j)),
                   pl.BlockSpec((8, half), lambda j, i: (0, j))],
        out_shape=[jax.ShapeDtypeStruct((s, SSM_CONV_DIM), F32), jax.ShapeDtypeStruct((8, SSM_CONV_DIM), F32)],
        scratch_shapes=[pltpu.VMEM((tb + SUBLANES, half), F32)],
    )(zx, zx, dxc, cw, cb)


def _conv_bwd_x(dpre, cw):
    s = dpre.shape[0]
    tb = _tile(s, 512)
    hb = tb // SUBLANES
    nb = s // tb

    def body(d_ref, halo_ref, w_ref, o_ref, scr_ref):
        i = pl.program_id(0)
        scr_ref[pl.ds(0, tb), :] = d_ref[...]
        scr_ref[pl.ds(tb, SUBLANES), :] = jnp.where(i < nb - 1, halo_ref[...], 0.0)
        acc = None
        for k in range(SSM_CONV):
            term = scr_ref[pl.ds(SSM_CONV - 1 - k, tb), :] * w_ref[k:k + 1, :]
            acc = term if acc is None else acc + term
        o_ref[...] = _mx(acc)

    return pl.pallas_call(
        body, name="ssd_conv_bwd_x", grid=(nb,),
        in_specs=[pl.BlockSpec((tb, SSM_CONV_DIM), lambda i: (i, 0)),
                  pl.BlockSpec((SUBLANES, SSM_CONV_DIM), lambda i: (jnp.minimum((i + 1) * hb, s // SUBLANES - 1), 0)),
                  pl.BlockSpec((8, SSM_CONV_DIM), lambda i: (0, 0))],
        out_specs=pl.BlockSpec((tb, SSM_CONV_DIM), lambda i: (i, 0)),
        out_shape=jax.ShapeDtypeStruct((s, SSM_CONV_DIM), MXU_DTYPE),
        scratch_shapes=[pltpu.VMEM((tb + SUBLANES, SSM_CONV_DIM), F32)],
        compiler_params=_params(VMEM_BIG),
    )(dpre, dpre, cw)


def _expand_constants():
    ex = np.zeros((LANES, SSM_D_INNER), np.float32)
    for h in range(SSM_HEADS):
        ex[h, h * 64:(h + 1) * 64] = 1.0
    return ex, np.ascontiguousarray(ex.T)


def _ssd_common(dtr_ref, par_ref, ex_ref, xc_ref):
    lc = SSM_CHUNK
    lane = lax.broadcasted_iota(jnp.int32, (1, LANES), 1)
    is_head = lane < SSM_HEADS
    par = par_ref[...]
    pre = dtr_ref[...] + par[0:1, :]
    dt = jnp.where(is_head, jax.nn.softplus(pre), 0.0)
    a = jnp.where(is_head, -jnp.exp(par[1:2, :]), 0.0)
    tri_b = lax.broadcasted_iota(jnp.int32, (lc, lc), 0) >= lax.broadcasted_iota(jnp.int32, (lc, lc), 1)
    tri = tri_b.astype(F32)
    da = dt * a
    acs = _dot_onehot(tri, da, onehot="a")
    acs_t = _dot_onehot(da, tri, (((0,), (1,)), ((), ())))
    wide = _dot_onehot(jnp.concatenate([dt, acs, par], axis=0), ex_ref[...])
    dt_x, acs_x, d_x = wide[0:lc], wide[lc:2 * lc], wide[2 * lc + 2:2 * lc + 3]
    last_x = acs_x[lc - 1:lc, :]
    xs = xc_ref[:, 0:SSM_D_INNER]
    return dict(pre=pre, dt=dt, a=a, tri_b=tri_b, tri=tri, acs=acs, acs_t=acs_t, dt_x=dt_x, d_x=d_x, xs=xs,
                xdt=xs * dt_x, e_x=jnp.exp(acs_x), dte_x=jnp.exp(last_x - acs_x), cd_x=jnp.exp(last_x),
                is_head=is_head)


def _decay_in(q, h):
    seg = q["acs"][:, h:h + 1] - q["acs_t"][h:h + 1, :]
    return jnp.exp(jnp.where(q["tri_b"], seg, -jnp.inf))


def _ssd_scan_fwd(xc, dtr, par):
    s = xc.shape[0]
    lc = SSM_CHUNK
    nc = s // lc
    ex, _ = _expand_constants()

    def body(xc_ref, dtr_ref, par_ref, ex_ref, y_ref, prev_ref, st_ref):
        @pl.when(pl.program_id(0) == 0)
        def _():
            st_ref[...] = jnp.zeros_like(st_ref)

        q = _ssd_common(dtr_ref, par_ref, ex_ref, xc_ref)
        lane = lax.broadcasted_iota(jnp.int32, (1, LANES), 1)
        for g in range(SSM_GROUPS):
            sl = slice(g * GROUP_W, (g + 1) * GROUP_W)
            bg = _mx(xc_ref[:, SSM_D_INNER + g * SSM_STATE:SSM_D_INNER + (g + 1) * SSM_STATE])
            cg = _mx(xc_ref[:, SSM_D_INNER + (SSM_GROUPS + g) * SSM_STATE:SSM_D_INNER + (SSM_GROUPS + g + 1) * SSM_STATE])
            gm = _dot(cg, bg, NT)
            prev = st_ref[g]
            prev_ref[g] = prev
            yoff = _dot(cg, _mx(prev)) * q["e_x"][:, sl]
            st_ref[g] = prev * q["cd_x"][:, sl] + _dot(bg, _mx(q["xdt"][:, sl] * q["dte_x"][:, sl]), TN)
            pairs = []
            for pr in range(2):
                xp = _mx(q["xdt"][:, g * GROUP_W + pr * LANES:g * GROUP_W + (pr + 1) * LANES])
                both = [_dot(_mx(gm * _decay_in(q, 4 * g + 2 * pr + r2)), xp) for r2 in range(2)]
                pairs.append(jnp.where(lane < 64, both[0], both[1]))
            y_ref[:, sl] = jnp.concatenate(pairs, axis=1) + yoff + q["xs"][:, sl] * q["d_x"][:, sl]

    return pl.pallas_call(
        body, name="ssd_scan_fwd", grid=(nc,),
        in_specs=[pl.BlockSpec((lc, SSM_CONV_DIM), lambda c: (c, 0)),
                  pl.BlockSpec((lc, LANES), lambda c: (c, 0)),
                  pl.BlockSpec((8, LANES), lambda c: (0, 0)),
                  pl.BlockSpec((LANES, SSM_D_INNER), lambda c: (0, 0))],
        out_specs=[pl.BlockSpec((lc, SSM_D_INNER), lambda c: (c, 0)),
                   pl.BlockSpec((None, SSM_GROUPS, SSM_STATE, GROUP_W), lambda c: (c, 0, 0, 0))],
        out_shape=[jax.ShapeDtypeStruct((s, SSM_D_INNER), F32),
                   jax.ShapeDtypeStruct((nc, SSM_GROUPS, SSM_STATE, GROUP_W), F32)],
        scratch_shapes=[pltpu.VMEM((SSM_GROUPS, SSM_STATE, GROUP_W), F32)],
        compiler_params=_params(VMEM_BIG),
    )(xc, dtr, par, _mx(jnp.asarray(ex)))


def _ssd_scan_bwd(dy, xc, dtr, par, prev, carry=()):
    s = xc.shape[0]
    lc = SSM_CHUNK
    nc = s // lc
    ex, ex_t = _expand_constants()
    whole = _exchange_shapes(list(carry))
    carry = _in_pieces(list(carry), nc)
    anywhere = pl.BlockSpec(memory_space=pl.ANY)

    def body(dy_ref, xc_ref, dtr_ref, par_ref, prev_ref, ex_ref, ext_ref, dxc_ref, ddtr_ref, sums_ref,
             gst_ref, tacs_ref, tdt_ref, tdd_ref):
        @pl.when(pl.program_id(0) == 0)
        def _():
            gst_ref[...] = jnp.zeros_like(gst_ref)
            sums_ref[...] = jnp.zeros_like(sums_ref)

        q = _ssd_common(dtr_ref, par_ref, ex_ref, xc_ref)
        lane = lax.broadcasted_iota(jnp.int32, (1, LANES), 1)
        row = lax.broadcasted_iota(jnp.int32, (lc, 1), 0)
        dacs_rows = jnp.zeros((lc, LANES), F32)
        dacs_cols_t = jnp.zeros((LANES, lc), F32)
        for g in range(SSM_GROUPS):
            sl = slice(g * GROUP_W, (g + 1) * GROUP_W)
            b_lo = SSM_D_INNER + g * SSM_STATE
            c_lo = SSM_D_INNER + (SSM_GROUPS + g) * SSM_STATE
            bg = _mx(xc_ref[:, b_lo:b_lo + SSM_STATE])
            cg = _mx(xc_ref[:, c_lo:c_lo + SSM_STATE])
            dyg = dy_ref[:, sl]
            xsg, xdtg = q["xs"][:, sl], q["xdt"][:, sl]
            eg, dteg, cdg = q["e_x"][:, sl], q["dte_x"][:, sl], q["cd_x"][:, sl]
            prevg = prev_ref[g]
            gs = gst_ref[g]
            prevm, gsm = _mx(prevg), _mx(gs)
            tdd_ref[:, sl] = dyg * xsg
            dxs = dyg * q["d_x"][:, sl]
            t_acs = dyg * _dot(cg, prevm) * eg
            dcp = _mx(dyg * eg)
            dc = _dot(dcp, prevm, NT)
            dprev = _dot(cg, dcp, TN)
            db = _dot(_mx(xdtg * dteg), gsm, NT)
            dx2 = _dot(bg, gsm)
            dxdt = dx2 * dteg
            ddte = dx2 * xdtg * dteg
            t_acs = t_acs - ddte
            last = (jnp.sum(ddte, axis=0, keepdims=True)
                    + jnp.sum(gs * prevg, axis=0, keepdims=True) * cdg)
            gm = _dot(cg, bg, NT)
            dgm = jnp.zeros((lc, lc), F32)
            pair_dx = []
            for pr in range(2):
                lo = g * GROUP_W + pr * LANES
                xp = _mx(q["xdt"][:, lo:lo + LANES])
                dyp = dy_ref[:, lo:lo + LANES]
                both = []
                for r2 in range(2):
                    h = 4 * g + 2 * pr + r2
                    mine = (lane >= 64 * r2) & (lane < 64 * (r2 + 1))
                    lm = _decay_in(q, h)
                    m = gm * lm
                    dm = _dot(_mx(jnp.where(mine, dyp, 0.0)), xp, NT)
                    dgm = dgm + dm * lm
                    w = dm * m
                    dacs_rows = dacs_rows + jnp.sum(w, axis=1, keepdims=True) * (lane == h).astype(F32)
                    head_row = (lax.broadcasted_iota(jnp.int32, (LANES, 1), 0) == h).astype(F32)
                    dacs_cols_t = dacs_cols_t + head_row * jnp.sum(w, axis=0, keepdims=True)
                    both.append(_dot(_mx(m), _mx(dyp), TN))
                pair_dx.append(jnp.where(lane < 64, both[0], both[1]))
            dxdt = dxdt + jnp.concatenate(pair_dx, axis=1)
            dgmm = _mx(dgm)
            dc = dc + _dot(dgmm, bg)
            db = db + _dot(dgmm, cg, TN)
            dxs = dxs + dxdt * q["dt_x"][:, sl]
            tdt_ref[:, sl] = dxdt * xsg
            tacs_ref[:, sl] = t_acs + jnp.where(row == lc - 1, last, 0.0)
            dxc_ref[:, sl] = dxs
            dxc_ref[:, b_lo:b_lo + SSM_STATE] = db
            dxc_ref[:, c_lo:c_lo + SSM_STATE] = dc
            gst_ref[g] = gs * cdg + dprev
        tdd = jnp.broadcast_to(jnp.sum(tdd_ref[...], axis=0, keepdims=True), (8, SSM_D_INNER))
        heads_of = _dot_onehot(jnp.concatenate([tacs_ref[...], tdt_ref[...], tdd], axis=0), ext_ref[...])
        dacs = heads_of[0:lc] + dacs_rows - dacs_cols_t.T
        dda = _dot_onehot(q["tri"], dacs, TN, onehot="a")
        ddt = dda * q["a"] + heads_of[lc:2 * lc]
        ddtr = jnp.where(q["is_head"], ddt * jax.nn.sigmoid(q["pre"]), 0.0)
        ddtr_ref[...] = _mx(ddtr)
        sums_ref[0:1, :] += jnp.sum(ddtr, axis=0, keepdims=True)
        sums_ref[1:2, :] += jnp.sum(dda * q["dt"], axis=0, keepdims=True) * q["a"]
        sums_ref[2:3, :] += heads_of[2 * lc:2 * lc + 1]

    rev = lambda c: nc - 1 - c
    wide = pltpu.VMEM((lc, SSM_D_INNER), F32)
    res = pl.pallas_call(
        _carrying(body, 7, 3, 4, carry, (nc,)), name="ssd_scan_bwd", grid=(nc,),
        in_specs=[pl.BlockSpec((lc, SSM_D_INNER), lambda c: (rev(c), 0)),
                  pl.BlockSpec((lc, SSM_CONV_DIM), lambda c: (rev(c), 0)),
                  pl.BlockSpec((lc, LANES), lambda c: (rev(c), 0)),
                  pl.BlockSpec((8, LANES), lambda c: (0, 0)),
                  pl.BlockSpec((None, SSM_GROUPS, SSM_STATE, GROUP_W), lambda c: (rev(c), 0, 0, 0)),
                  pl.BlockSpec((LANES, SSM_D_INNER), lambda c: (0, 0)),
                  pl.BlockSpec((SSM_D_INNER, LANES), lambda c: (0, 0))] + [anywhere] * len(carry),
        out_specs=[pl.BlockSpec((lc, SSM_CONV_DIM), lambda c: (rev(c), 0)),
                   pl.BlockSpec((lc, LANES), lambda c: (rev(c), 0)),
                   pl.BlockSpec((8, LANES), lambda c: (0, 0))] + [anywhere] * len(carry),
        out_shape=[jax.ShapeDtypeStruct((s, SSM_CONV_DIM), F32), jax.ShapeDtypeStruct((s, LANES), MXU_DTYPE),
                   jax.ShapeDtypeStruct((8, LANES), F32)] + _exchange_shapes(carry),
        scratch_shapes=[pltpu.VMEM((SSM_GROUPS, SSM_STATE, GROUP_W), F32), wide, wide, wide]
        + (_exchange_sems(carry) if carry else []),
        compiler_params=_params(VMEM_BIG),
    )(dy, xc, dtr, par, prev, _mx(jnp.asarray(ex)), _mx(jnp.asarray(ex_t)), *[c[1] for c in carry])
    return (*res[:3], [r.reshape(w.shape) for r, w in zip(res[3:], whole)])


def _group_norm_parts(yv, zv):
    yg = yv * jax.nn.silu(zv)
    normed, rinvs = [], []
    for g in range(SSM_GROUPS):
        blk = yg[:, g * GROUP_W:(g + 1) * GROUP_W]
        rinv = lax.rsqrt(jnp.mean(blk * blk, axis=-1, keepdims=True) + RMS_EPS)
        normed.append(blk * rinv)
        rinvs.append(rinv)
    return normed, rinvs


def _gnorm_fwd(y, zx, nw):
    s = y.shape[0]
    tb = _tile(s, 512)

    def body(y_ref, z_ref, w_ref, o_ref):
        normed, _ = _group_norm_parts(y_ref[...], z_ref[...])
        for g in range(SSM_GROUPS):
            sl = slice(g * GROUP_W, (g + 1) * GROUP_W)
            o_ref[:, sl] = _mx(normed[g] * w_ref[0:1, sl])

    row = pl.BlockSpec((tb, SSM_D_INNER), lambda i: (i, 0))
    return pl.pallas_call(
        body, name="ssd_gnorm_fwd", grid=(s // tb,),
        in_specs=[row, row, pl.BlockSpec((8, SSM_D_INNER), lambda i: (0, 0))],
        out_specs=row, out_shape=jax.ShapeDtypeStruct((s, SSM_D_INNER), MXU_DTYPE),
    )(y, zx, nw)


def _gnorm_bwd(y, zx, nw, dyn):
    s = y.shape[0]
    tb = _tile(s, 512)

    def body(y_ref, z_ref, w_ref, d_ref, dy_ref, dz_ref, sums_ref):
        @pl.when(pl.program_id(0) == 0)
        def _():
            sums_ref[...] = jnp.zeros_like(sums_ref)

        yv, zv = y_ref[...], z_ref[...]
        normed, rinvs = _group_norm_parts(yv, zv)
        gate = jax.nn.silu(zv)
        dgate = _dsilu(zv)
        for g in range(SSM_GROUPS):
            sl = slice(g * GROUP_W, (g + 1) * GROUP_W)
            dv = d_ref[:, sl]
            n = normed[g]
            sums_ref[0:1, sl] += jnp.sum(dv * n, axis=0, keepdims=True)
            dn = dv * w_ref[0:1, sl]
            dyg = rinvs[g] * (dn - n * jnp.mean(dn * n, axis=-1, keepdims=True))
            dy_ref[:, sl] = dyg * gate[:, sl]
            dz_ref[:, sl] = _mx(dyg * yv[:, sl] * dgate[:, sl])

    row = pl.BlockSpec((tb, SSM_D_INNER), lambda i: (i, 0))
    par = pl.BlockSpec((8, SSM_D_INNER), lambda i: (0, 0))
    return pl.pallas_call(
        body, name="ssd_gnorm_bwd", grid=(s // tb,),
        in_specs=[row, row, par, row], out_specs=[row, row, par],
        out_shape=[jax.ShapeDtypeStruct((s, SSM_D_INNER), F32), jax.ShapeDtypeStruct((s, SSM_D_INNER), MXU_DTYPE),
                   jax.ShapeDtypeStruct((8, SSM_D_INNER), F32)],
    )(y, zx, nw, dyn)


def _rows8(v):
    v = v.reshape(1, -1)
    return jnp.pad(v, ((0, 7), (0, 0)))


def _ssd_weights(w_in, w_out):
    nzx = SSM_D_INNER + SSM_CONV_DIM
    wzx = w_in[:, :nzx]
    wdt = jnp.pad(w_in[:, nzx:], ((0, 0), (0, LANES - SSM_HEADS)))
    return dict(wzx=wzx, wdt=wdt, wzx_t=wzx.T, wdt_t=wdt.T, wout=w_out, wout_t=w_out.T)


def _ssd_fwd(u, w, cw, cb, par, nw):
    zx = _mm_nn("ssd_in_zx", u, w["wzx"], F32)
    dtr = _mm_nn("ssd_in_dt", u, w["wdt"], F32)
    xc = _conv_fwd(zx, cw, cb)
    y, prev = _ssd_scan_fwd(xc, dtr, par)
    yn = _gnorm_fwd(y, zx, nw)
    out = _mm_nn("ssd_out", yn, w["wout"], F32)
    return out, dict(zx=zx, dtr=dtr, xc=xc, y=y, prev=prev, yn=yn)


def _ssd_bwd(dy, u, w, cw, cb, par, nw, res, carry=()):
    dyn = _mm_nn("ssd_dyn", dy, w["wout_t"], F32)
    dw_out = _mm_tn("ssd_dw_out", res["yn"], dy)
    dys, dz, nsum = _gnorm_bwd(res["y"], res["zx"], nw, dyn)
    dxc, ddtr, ssum, carried = _ssd_scan_bwd(dys, res["xc"], res["dtr"], par, res["prev"],
                                             list(carry) + [(True, _row_slots(dw_out[None]))])
    dpre, csum = _conv_bwd_pre(res["zx"], dxc, cw, cb)
    dxbc = _conv_bwd_x(dpre, cw)
    wt = w["wzx_t"]
    du = _mm_nn("ssd_du_z", dz, wt[:SSM_D_INNER], F32)
    du = _mm_nn("ssd_du_x", dxbc, wt[SSM_D_INNER:], F32, addend=du)
    du = _mm_nn("ssd_du_dt", ddtr, w["wdt_t"], F32, addend=du)
    dw_in = jnp.concatenate(
        [_mm_tn("ssd_dw_z", u, dz), _mm_tn("ssd_dw_x", u, dxbc), _mm_tn("ssd_dw_dt", u, ddtr)[:, :SSM_HEADS]], axis=1)
    small = dict(conv_w=csum[:SSM_CONV], conv_b=csum[SSM_CONV], dt_bias=ssum[0, :SSM_HEADS],
                 a_log=ssum[1, :SSM_HEADS], d=ssum[2, :SSM_HEADS], norm_w=nsum[0])
    return du, dw_in, small, carried


def _ada_fwd(c_all, ada_w, ada_b_mine):
    nl, _, ncol = ada_w.shape

    def body(c_ref, w_ref, b_ref, o_ref):
        cond = _mx(jax.nn.silu(c_ref[...]))
        for i in range(nl):
            o_ref[i] = _dot(cond, _mx(w_ref[i])) + b_ref[i:i + 1, :]

    return pl.pallas_call(
        body, name="ada_fwd", out_shape=jax.ShapeDtypeStruct((nl, 2 * N_DEV, ncol), F32),
        compiler_params=_params(VMEM_BIG),
    )(c_all, ada_w, ada_b_mine)


def _ada_bwd(c_all, dmod_cols):
    nl, _, ncol = dmod_cols.shape

    def body(c_ref, d_ref, o_ref):
        cond = _mx(jax.nn.silu(c_ref[...]))
        for i in range(nl):
            o_ref[i] = _dot(cond, _mx(d_ref[i]), TN)

    return pl.pallas_call(
        body, name="ada_bwd", out_shape=jax.ShapeDtypeStruct((nl, D_MODEL, ncol), F32),
        compiler_params=_params(VMEM_BIG),
    )(c_all, dmod_cols)


def _adamw(gslots, w, m, v, name):
    k, r, c = gslots.shape
    tr = _tile(r, 256) if r % 256 == 0 else r
    c1 = 1.0 - ADAM_B1 ** ADAM_STEP
    c2 = 1.0 - ADAM_B2 ** ADAM_STEP

    def body(g_ref, w_ref, m_ref, v_ref, go_ref, d_ref, mo_ref, vo_ref):
        g = g_ref[0]
        for slot in range(1, k):
            g = g + g_ref[slot]
        mn = ADAM_B1 * m_ref[...] + (1.0 - ADAM_B1) * g
        vn = ADAM_B2 * v_ref[...] + (1.0 - ADAM_B2) * jnp.square(g)
        go_ref[...] = g
        mo_ref[...] = mn
        vo_ref[...] = vn
        d_ref[...] = -ADAM_LR * ((mn / c1) / (jnp.sqrt(vn / c2) + ADAM_EPS) + ADAM_WD * w_ref[...])

    row = pl.BlockSpec((tr, c), lambda i: (i, 0))
    shp = jax.ShapeDtypeStruct((r, c), F32)
    return pl.pallas_call(
        body, name=name, grid=(r // tr,),
        in_specs=[pl.BlockSpec((k, tr, c), lambda i: (0, i, 0)), row, row, row],
        out_specs=[row, row, row, row], out_shape=[shp, shp, shp, shp],
        compiler_params=_params(VMEM_BIG),
    )(gslots, w, m, v)


def _adamw_any(gslots, w, m, v, name):
    shape = w.shape
    two_d = (-1, shape[-1])
    k = gslots.shape[0]
    outs = _adamw(gslots.reshape((k,) + w.reshape(two_d).shape), w.reshape(two_d), m.reshape(two_d),
                  v.reshape(two_d), name)
    return tuple(o.reshape(shape) for o in outs)


def _cols_from_slots(g):
    g = jnp.moveaxis(g, 0, -2)
    return g.reshape(g.shape[:-2] + (g.shape[-2] * g.shape[-1],))


def _rows_from_slots(g):
    g = jnp.moveaxis(g, 0, -3)
    return g.reshape(g.shape[:-3] + (g.shape[-3] * g.shape[-2], g.shape[-1]))


def _col_slots(g):
    cs = g.shape[-1] // N_DEV
    return jnp.moveaxis(g.reshape(g.shape[:-1] + (N_DEV, cs)), -2, 0)


def _row_slots(g):
    rs = g.shape[-2] // N_DEV
    return jnp.moveaxis(g.reshape(g.shape[:-2] + (N_DEV, rs, g.shape[-1])), -3, 0)


def _gather_cols(w, name, dtype=None):
    return _cols_from_slots(_all_gather(w.astype(dtype or MXU_DTYPE), name))


def _gather_rows(w, name):
    return _rows_from_slots(_all_gather(_mx(w), name))


def _scatter_cols(g, name):
    return _all_to_all(_col_slots(g), name)


def _scatter_rows(g, name):
    return _all_to_all(_row_slots(g), name)


def kernel(x, c, ada_w, ada_b, ln_mix_g, ln_mix_b, ln_mlp_g, ln_mlp_b, mlp_w1, mlp_w2, fox_w_in, fox_b_f, fox_w_o, ssm_w_in, ssm_conv_w, ssm_conv_b, ssm_dt_bias, ssm_a_log, ssm_d, ssm_norm_w, ssm_w_out, loss_target, m_ada_w, m_ada_b, m_ln_mix_g, m_ln_mix_b, m_ln_mlp_g, m_ln_mlp_b, m_mlp_w1, m_mlp_w2, m_fox_w_in, m_fox_b_f, m_fox_w_o, m_ssm_w_in, m_ssm_conv_w, m_ssm_conv_b, m_ssm_dt_bias, m_ssm_a_log, m_ssm_d, m_ssm_norm_w, m_ssm_w_out, v_ada_w, v_ada_b, v_ln_mix_g, v_ln_mix_b, v_ln_mlp_g, v_ln_mlp_b, v_mlp_w1, v_mlp_w2, v_fox_w_in, v_fox_b_f, v_fox_w_o, v_ssm_w_in, v_ssm_conv_w, v_ssm_conv_b, v_ssm_dt_bias, v_ssm_a_log, v_ssm_d, v_ssm_norm_w, v_ssm_w_out):
    me = 4 * lax.axis_index("x") + 2 * lax.axis_index("y") + lax.axis_index("c")
    xs = x[0]
    target = loss_target[0]
    d = D_MODEL

    c_all = _all_gather(c, "gather_c").reshape(N_DEV, d)
    c_all = jnp.pad(c_all, ((0, N_DEV), (0, 0)))
    ncol = ada_w.shape[-1]
    ada_b_mine = lax.dynamic_slice_in_dim(ada_b, me * ncol, ncol, axis=1)
    mod_cols = _ada_fwd(c_all, ada_w, ada_b_mine)
    mod_all = _all_gather(mod_cols, "gather_mod")
    mod = lax.dynamic_index_in_dim(mod_all, me, axis=2, keepdims=False)
    mod = jnp.moveaxis(mod, 0, 1).reshape(DEPTH, 6, d)

    def pv_rows(*rows):
        return jnp.pad(jnp.stack(rows), ((0, 8 - len(rows)), (0, 0)))

    fw = _fox_weights(_gather_cols(fox_w_in, "gather_fox_in")[0], _gather_rows(fox_w_o, "gather_fox_o")[0])
    conv_w = _gather_cols(ssm_conv_w, "gather_conv_w", F32)[0]
    small_vec = jnp.concatenate([ssm_conv_b[0], ssm_norm_w[0]]).reshape(1, -1)
    small_all = _all_gather(small_vec.astype(F32), "gather_conv_b").reshape(N_DEV, -1)
    conv_b = small_all[:, :SSM_CONV_DIM // N_DEV].reshape(-1)
    norm_w = small_all[:, SSM_CONV_DIM // N_DEV:].reshape(-1)
    cw8 = jnp.pad(conv_w, ((0, 8 - SSM_CONV), (0, 0)))
    cb8 = _rows8(conv_b)
    nw8 = _rows8(norm_w)
    bf8 = _rows8(jnp.pad(fox_b_f[0], (0, LANES - FOX_HEADS)))
    par8 = jnp.pad(jnp.stack([jnp.pad(p[0], (0, LANES - SSM_HEADS)) for p in (ssm_dt_bias, ssm_a_log, ssm_d)]),
                   ((0, 5), (0, 0)))

    sh_a, sc_a, g_a, sh_m, sc_m, g_m = [mod[:, k] for k in range(6)]
    u0 = _modulate(xs, pv_rows(1.0 + sc_a[0], sh_a[0]), "modulate0")
    y0, fres, gathered = _fox_fwd(u0, fw, bf8, [(False, _mx(w)) for w in (mlp_w1, mlp_w2, ssm_w_in, ssm_w_out)])
    w1 = _cols_from_slots(gathered[0])
    w2 = _rows_from_slots(gathered[1])
    sw = _ssd_weights(_cols_from_slots(gathered[2])[0], _rows_from_slots(gathered[3])[0])
    pv0 = pv_rows(1.0 + g_a[0], ln_mix_g[0], ln_mix_b[0], 1.0 + sc_m[0], sh_m[0])
    x1, u1 = _ln_fwd(xs, y0, pv0, "ln_mix0")
    y1, (h0, a0) = _mlp_fwd(u1, w1[0], w2[0], "0")
    pv1 = pv_rows(1.0 + g_m[0], ln_mlp_g[0], ln_mlp_b[0], 1.0 + sc_a[1], sh_a[1])
    x2, u2 = _ln_fwd(x1, y1, pv1, "ln_mlp0")
    y2, sres = _ssd_fwd(u2, sw, cw8, cb8, par8, nw8)
    pv2 = pv_rows(1.0 + g_a[1], ln_mix_g[1], ln_mix_b[1], 1.0 + sc_m[1], sh_m[1])
    x3, u3 = _ln_fwd(x2, y2, pv2, "ln_mix1")
    y3, (h1, a1) = _mlp_fwd(u3, w1[1], w2[1], "1")
    pv3 = pv_rows(1.0 + g_m[1], ln_mlp_g[1], ln_mlp_b[1])

    dx3, dy3, s3 = _ln_bwd(x3, y3, pv3, "ln_mlp1_bwd", target=target)
    loss = lax.psum(s3[5, 0], ("x", "y", "c"))
    du3, dw1_1, dw2_1 = _mlp_bwd(dy3, u3, h1, a1, w1[1].T, w2[1].T, "1")
    dx2, dy2, s2 = _ln_bwd(x2, y2, pv2, "ln_mix1_bwd", dxo=dx3, du=du3)
    du2, d_ssm_in, ssm_small, ex_scan = _ssd_bwd(
        dy2, u2, sw, cw8, cb8, par8, nw8, sres,
        [(True, _col_slots(dw1_1[None])), (True, _row_slots(dw2_1[None]))])
    dx1, dy1, s1 = _ln_bwd(x1, y1, pv1, "ln_mlp0_bwd", dxo=dx2, du=du2)
    du1, dw1_0, dw2_0 = _mlp_bwd(dy1, u1, h0, a0, w1[0].T, w2[0].T, "0")
    dx0, dy0, s0 = _ln_bwd(xs, y0, pv0, "ln_mix0_bwd", dxo=dx1, du=du1)
    late = [(True, _col_slots(dw1_0[None])), (True, _row_slots(dw2_0[None])), (True, _col_slots(d_ssm_in[None])),
            (True, _col_slots(ssm_small["conv_w"][None])), (True, _col_slots(ssm_small["conv_b"][None])),
            (True, _col_slots(ssm_small["norm_w"][None]))]
    du0, d_fox_in, d_fox_o, fox_sums, ex_post = _fox_bwd(dy0, u0, fw, bf8, fres, late)
    grad_x, sx = _mod_bwd(dx0, du0, xs, pv_rows(1.0 + sc_a[0], sh_a[0]), "modulate0_bwd")

    dmod = jnp.stack([
        jnp.stack([sx[1], sx[0], s0[4], s0[1], s0[0], s1[4]]),
        jnp.stack([s1[1], s1[0], s2[4], s2[1], s2[0], s3[4]]),
    ]).reshape(DEPTH, 6 * d)

    def pad_rows(v):
        v = v.reshape(-1, LANES) if v.size % LANES == 0 else jnp.pad(v.reshape(-1), (0, LANES - v.size)).reshape(1, LANES)
        return jnp.pad(v, ((0, (-v.shape[0]) % 8), (0, 0)))

    small_parts = [dmod, jnp.stack([s0[2], s2[2]]), jnp.stack([s0[3], s2[3]]), jnp.stack([s1[2], s3[2]]),
                   jnp.stack([s1[3], s3[3]]), fox_sums[0, :FOX_HEADS], ssm_small["dt_bias"], ssm_small["a_log"],
                   ssm_small["d"]]
    packed = [pad_rows(p) for p in small_parts]
    offsets = np.cumsum([0] + [p.shape[0] for p in packed])
    small_all_g = _all_gather(jnp.concatenate(packed, axis=0), "gather_small_grads")

    def unpack(idx, shape):
        n = int(np.prod(shape))
        blk = small_all_g[:, offsets[idx]:offsets[idx + 1]].reshape(N_DEV, -1)[:, :n]
        return blk.reshape((N_DEV,) + tuple(shape))

    dmod_all = unpack(0, (DEPTH, 6 * d))
    dmod_cols = lax.dynamic_slice_in_dim(dmod_all, me * ncol, ncol, axis=2)
    dmod_cols = jnp.pad(jnp.moveaxis(dmod_cols, 0, 1), ((0, 0), (0, N_DEV), (0, 0)))
    g_ada_w = _ada_bwd(c_all, dmod_cols)

    shards = dict(
        mlp_w1=jnp.concatenate([ex_post[0], ex_scan[0]], axis=1),
        mlp_w2=jnp.concatenate([ex_post[1], ex_scan[1]], axis=1),
        ssm_w_in=ex_post[2], ssm_w_out=ex_scan[2],
        ssm_conv_w=ex_post[3], ssm_conv_b=ex_post[4], ssm_norm_w=ex_post[5],
        fox_w_in=_scatter_cols(d_fox_in[None], "scatter_fox_in"), fox_w_o=_scatter_rows(d_fox_o[None], "scatter_fox_o"),
        ada_w=g_ada_w[None], ada_b=dmod_all,
        ln_mix_g=unpack(1, (DEPTH, d)), ln_mix_b=unpack(2, (DEPTH, d)),
        ln_mlp_g=unpack(3, (DEPTH, d)), ln_mlp_b=unpack(4, (DEPTH, d)),
        fox_b_f=unpack(5, (1, FOX_HEADS)), ssm_dt_bias=unpack(6, (1, SSM_HEADS)),
        ssm_a_log=unpack(7, (1, SSM_HEADS)), ssm_d=unpack(8, (1, SSM_HEADS)),
    )
    weights = dict(ada_w=ada_w, ada_b=ada_b, ln_mix_g=ln_mix_g, ln_mix_b=ln_mix_b, ln_mlp_g=ln_mlp_g, ln_mlp_b=ln_mlp_b,
                   mlp_w1=mlp_w1, mlp_w2=mlp_w2, fox_w_in=fox_w_in, fox_b_f=fox_b_f, fox_w_o=fox_w_o, ssm_w_in=ssm_w_in,
                   ssm_conv_w=ssm_conv_w, ssm_conv_b=ssm_conv_b, ssm_dt_bias=ssm_dt_bias, ssm_a_log=ssm_a_log,
                   ssm_d=ssm_d, ssm_norm_w=ssm_norm_w, ssm_w_out=ssm_w_out)
    mom1 = dict(ada_w=m_ada_w, ada_b=m_ada_b, ln_mix_g=m_ln_mix_g, ln_mix_b=m_ln_mix_b, ln_mlp_g=m_ln_mlp_g,
                ln_mlp_b=m_ln_mlp_b, mlp_w1=m_mlp_w1, mlp_w2=m_mlp_w2, fox_w_in=m_fox_w_in, fox_b_f=m_fox_b_f,
                fox_w_o=m_fox_w_o, ssm_w_in=m_ssm_w_in, ssm_conv_w=m_ssm_conv_w, ssm_conv_b=m_ssm_conv_b,
                ssm_dt_bias=m_ssm_dt_bias, ssm_a_log=m_ssm_a_log, ssm_d=m_ssm_d, ssm_norm_w=m_ssm_norm_w,
                ssm_w_out=m_ssm_w_out)
    mom2 = dict(ada_w=v_ada_w, ada_b=v_ada_b, ln_mix_g=v_ln_mix_g, ln_mix_b=v_ln_mix_b, ln_mlp_g=v_ln_mlp_g,
                ln_mlp_b=v_ln_mlp_b, mlp_w1=v_mlp_w1, mlp_w2=v_mlp_w2, fox_w_in=v_fox_w_in, fox_b_f=v_fox_b_f,
                fox_w_o=v_fox_w_o, ssm_w_in=v_ssm_w_in, ssm_conv_w=v_ssm_conv_w, ssm_conv_b=v_ssm_conv_b,
                ssm_dt_bias=v_ssm_dt_bias, ssm_a_log=v_ssm_a_log, ssm_d=v_ssm_d, ssm_norm_w=v_ssm_norm_w,
                ssm_w_out=v_ssm_w_out)
    names = list(weights)
    stepped = {n: _adamw_any(shards[n], weights[n], mom1[n], mom2[n], f"adamw_{n}") for n in names}
    return (loss, grad_x[None], *[stepped[n][0] for n in names], *[stepped[n][1] for n in names],
            *[stepped[n][2] for n in names], *[stepped[n][3] for n in names])
```

```python
import numpy as np
import jax
import jax.numpy as jnp
from jax import lax
from jax.experimental import pallas as pl
from jax.experimental.pallas import tpu as pltpu

F32 = jnp.float32
MXU_DTYPE = jnp.bfloat16
HIGHEST = lax.Precision.HIGHEST

N_DEV = 8
D_MODEL = 1024
DEPTH = 2
FOX_HEADS = 16
FOX_HEAD_DIM = 64
D_FF = 4096
SSM_D_INNER = 2048
SSM_HEADS = 32
SSM_GROUPS = 8
SSM_STATE = 128
SSM_CHUNK = 128
SSM_CONV = 4
SSM_CONV_DIM = 4096
GROUP_W = SSM_D_INNER // SSM_GROUPS
LN_EPS = 1e-5
RMS_EPS = 1e-5
ALPHA = (2.0 * DEPTH) ** 0.25
LANES = 128
SUBLANES = 8

ADAM_LR = 0.001
ADAM_B1 = 0.9
ADAM_B2 = 0.999
ADAM_EPS = 1e-08
ADAM_WD = 0.01
ADAM_STEP = 10

NN = (((1,), (0,)), ((), ()))
NT = (((1,), (1,)), ((), ()))
TN = (((0,), (0,)), ((), ()))

VMEM_BIG = 56 * 1024 * 1024
MM_ROWS = 2048
MM_DEPTH = 2048


def _dot(a, b, dims=NN, precision=None):
    return lax.dot_general(a, b, dims, precision=precision, preferred_element_type=F32)


def _mx(v):
    return v.astype(MXU_DTYPE)


def _pieces3(v):
    hi = _mx(v)
    r1 = v - hi.astype(F32)
    mid = _mx(r1)
    return hi, mid, _mx(r1 - mid.astype(F32))


def _dot_onehot(a, b, dims=NN, onehot="b"):
    if onehot == "b":
        return sum(_dot(p, _mx(b), dims) for p in _pieces3(a))
    return sum(_dot(_mx(a), p, dims) for p in _pieces3(b))


def _params(vmem=None):
    return pltpu.CompilerParams(vmem_limit_bytes=vmem) if vmem else None


def _all_gather(x, name):
    def body(x_ref, out_ref, send_sems, recv_sems, local_sem):
        xi, yi, ci = lax.axis_index("x"), lax.axis_index("y"), lax.axis_index("c")
        me, sibling = (xi, yi, ci), (xi, yi, 1 - ci)
        chips = [(1 - xi, yi), (xi, 1 - yi), (1 - xi, 1 - yi)]

        def slot(px, py, pc):
            return out_ref.at[4 * px + 2 * py + pc]

        def copy(k, block, to, src=None):
            return pltpu.make_async_remote_copy(
                src_ref=slot(*block) if src is None else src, dst_ref=slot(*block),
                send_sem=send_sems.at[k], recv_sem=recv_sems.at[k],
                device_id=to, device_id_type=pl.DeviceIdType.MESH)

        mine = pltpu.make_async_copy(x_ref, slot(*me), local_sem)
        mine.start()
        first = [copy(0, me, sibling, src=x_ref)]
        first += [copy(1 + j, me, (*chip, ci), src=x_ref) for j, chip in enumerate(chips)]
        for cp in first:
            cp.start()
        passed = [copy(4 + j, (*chip, ci), sibling) for j, chip in enumerate(chips)]
        for j, chip in enumerate(chips):
            copy(1 + j, (*chip, ci), me).wait_recv()
            passed[j].start()
        copy(0, sibling, me).wait_recv()
        for j, chip in enumerate(chips):
            copy(4 + j, (*chip, 1 - ci), me).wait_recv()
        for cp in first + passed:
            cp.wait_send()
        mine.wait()

    return pl.pallas_call(
        body, name=name,
        out_shape=jax.ShapeDtypeStruct((N_DEV,) + x.shape, x.dtype),
        in_specs=[pl.BlockSpec(memory_space=pl.ANY)],
        out_specs=pl.BlockSpec(memory_space=pl.ANY),
        scratch_shapes=[pltpu.SemaphoreType.DMA((7,)), pltpu.SemaphoreType.DMA((7,)),
                        pltpu.SemaphoreType.DMA],
    )(x)


EXCHANGE_PIECES = 1


def _direct_copies(scatter, x_ref, out_ref, send_sems, recv_sems, local_sems, n, piece=None):
    xi, yi, ci = lax.axis_index("x"), lax.axis_index("y"), lax.axis_index("c")
    me = 4 * xi + 2 * yi + ci

    def part(ref):
        return ref if piece is None else ref.at[piece]

    local = pltpu.make_async_copy(part(x_ref.at[me] if scatter else x_ref), part(out_ref.at[me]), local_sems.at[n])
    remote = []
    for k in range(1, N_DEV):
        px = 1 - xi if k & 4 else xi
        py = 1 - yi if k & 2 else yi
        pc = 1 - ci if k & 1 else ci
        remote.append(pltpu.make_async_remote_copy(
            src_ref=part(x_ref.at[4 * px + 2 * py + pc] if scatter else x_ref), dst_ref=part(out_ref.at[me]),
            send_sem=send_sems.at[7 * n + k - 1], recv_sem=recv_sems.at[7 * n + k - 1],
            device_id=(px, py, pc), device_id_type=pl.DeviceIdType.MESH))
    return local, remote


def _in_pieces(carry, steps):
    out = []
    for scatter, a in carry:
        body = a.shape[1:] if scatter else a.shape
        rows = int(np.prod(body[:-1])) if len(body) > 1 else 1
        align = SUBLANES * (4 // a.dtype.itemsize)
        pieces = 1
        while (pieces * 2 <= min(EXCHANGE_PIECES, steps // 2) and rows % (pieces * 2 * align) == 0):
            pieces *= 2
        shape = (pieces, rows // pieces, body[-1])
        out.append((scatter, a.reshape(((N_DEV,) if scatter else ()) + shape), pieces, max(1, (steps // 2) // pieces)))
    return out


def _exchange_shapes(carry):
    return [jax.ShapeDtypeStruct(c[1].shape if c[0] else (N_DEV,) + c[1].shape, c[1].dtype) for c in carry]


def _exchange_sems(carry):
    n = max(len(carry), 1)
    return [pltpu.SemaphoreType.DMA((7 * n,)), pltpu.SemaphoreType.DMA((7 * n,)), pltpu.SemaphoreType.DMA((n,))]


def _exchange_start(carry, srcs, dsts, sems, step=None):
    for n, entry in enumerate(carry):
        scatter = entry[0]
        pieces, stride = (entry[2], entry[3]) if len(entry) > 2 else (1, 1)
        local, remote = _direct_copies(scatter, srcs[n], dsts[n], *sems, n)
        if step is None:
            local.start()
            for cp in remote:
                cp.start()
            continue

        @pl.when(step == 0)
        def _():
            local.start()
            if pieces == 1:
                for cp in remote:
                    cp.start()

        if pieces > 1:
            @pl.when((step % stride == 0) & (step // stride < pieces))
            def _():
                for cp in _direct_copies(scatter, srcs[n], dsts[n], *sems, n, piece=step // stride)[1]:
                    cp.start()


def _exchange_wait(carry, srcs, dsts, sems):
    for n, entry in enumerate(carry):
        local, remote = _direct_copies(entry[0], srcs[n], dsts[n], *sems, n)
        for cp in remote:
            cp.wait()
        local.wait()


def _carrying(body, n_in, n_out, n_scratch, carry, grid):
    nc = len(carry)

    def wrapped(*refs):
        ins, srcs = refs[:n_in], refs[n_in:n_in + nc]
        outs = refs[n_in + nc:n_in + nc + n_out]
        dsts = refs[n_in + nc + n_out:n_in + 2 * nc + n_out]
        scratch = refs[n_in + 2 * nc + n_out:n_in + 2 * nc + n_out + n_scratch]
        sems = refs[n_in + 2 * nc + n_out + n_scratch:]
        step = 0
        for axis, extent in enumerate(grid):
            step = step * extent + pl.program_id(axis)
        _exchange_start(carry, srcs, dsts, sems, step=step)
        body(*ins, *outs, *scratch)

        @pl.when(step == int(np.prod(grid)) - 1)
        def _():
            _exchange_wait(carry, srcs, dsts, sems)

    return wrapped if nc else body


def _all_to_all(x, name):
    carry = [(True, x)]

    def body(x_ref, out_ref, *sems):
        _exchange_start(carry, [x_ref], [out_ref], sems)
        _exchange_wait(carry, [x_ref], [out_ref], sems)

    return pl.pallas_call(
        body, name=name,
        out_shape=jax.ShapeDtypeStruct(x.shape, x.dtype),
        in_specs=[pl.BlockSpec(memory_space=pl.ANY)],
        out_specs=pl.BlockSpec(memory_space=pl.ANY),
        scratch_shapes=_exchange_sems(carry),
    )(x)


def _mm(name, a, b, *, grid, a_spec, b_spec, dims, k_axis, outs, acc=None, extras=(), epi=None, vmem=None,
        carry=()):
    nk = grid[k_axis]
    n_ex, n_out = len(extras), len(outs)
    whole = _exchange_shapes(list(carry))
    carry = _in_pieces(list(carry), int(np.prod(grid)))
    anywhere = pl.BlockSpec(memory_space=pl.ANY)

    def body(*refs):
        a_ref, b_ref = refs[0], refs[1]
        ex = refs[2:2 + n_ex]
        out = refs[2 + n_ex:2 + n_ex + n_out]

        def finish(val):
            if epi is None:
                out[0][...] = val.astype(out[0].dtype)
            else:
                epi(val, ex, out)

        part = _dot(a_ref[...], b_ref[...], dims)
        if nk == 1:
            finish(part)
        else:
            acc_ref = refs[2 + n_ex + n_out]
            k = pl.program_id(k_axis)

            @pl.when(k == 0)
            def _():
                acc_ref[...] = part

            @pl.when(k > 0)
            def _():
                acc_ref[...] += part

            @pl.when(k == nk - 1)
            def _():
                finish(acc_ref[...])

    res = pl.pallas_call(
        _carrying(body, 2 + n_ex, n_out, 1 if nk > 1 else 0, carry, grid), name=name, grid=grid,
        in_specs=[a_spec, b_spec] + [s for _, s in extras] + [anywhere] * len(carry),
        out_specs=[s for _, s in outs] + [anywhere] * len(carry),
        out_shape=[o for o, _ in outs] + _exchange_shapes(carry),
        scratch_shapes=([pltpu.VMEM(acc, F32)] if nk > 1 else []) + (_exchange_sems(carry) if carry else []),
        compiler_params=_params(vmem),
    )(a, b, *[e for e, _ in extras], *[c[1] for c in carry])
    return list(res[:n_out]) + [r.reshape(w.shape) for r, w in zip(res[n_out:], whole)]


def _tile(n, t):
    t = min(n, t)
    assert n % t == 0, (n, t)
    return t


def _mm_nn(name, a, b, out_dtype, *, addend=None, tm=MM_ROWS, tn=1024, tk=1024, epi=None, extras=(), outs=None,
           carry=()):
    m, kk = a.shape
    n = b.shape[1]
    if addend is not None:
        tm = tm // 2
    tm, tn, tk = _tile(m, tm), _tile(n, tn), _tile(kk, tk)
    o_spec = pl.BlockSpec((tm, tn), lambda i, j, k: (i, j))
    if outs is None:
        outs = [(jax.ShapeDtypeStruct((m, n), out_dtype), o_spec)]
    extras = list(extras)
    if addend is not None:
        extras = [(addend, o_spec)] + extras

        def epi(val, ex, out):
            out[0][...] = (val + ex[0][...].astype(F32)).astype(out[0].dtype)

    res = _mm(name, a, b, grid=(m // tm, n // tn, kk // tk),
              a_spec=pl.BlockSpec((tm, tk), lambda i, j, k: (i, k)),
              b_spec=pl.BlockSpec((tk, tn), lambda i, j, k: (k, j)),
              dims=NN, k_axis=2, acc=(tm, tn), outs=outs, extras=extras, epi=epi, vmem=VMEM_BIG, carry=carry)
    return res[0] if len(res) == 1 else res


def _mm_tn(name, a, b, out_dtype=F32, *, tm=1024, tn=1024, tk=MM_DEPTH):
    kk, m = a.shape
    n = b.shape[1]
    tm, tn, tk = _tile(m, tm), _tile(n, tn), _tile(kk, tk)
    res = _mm(name, a, b, grid=(m // tm, n // tn, kk // tk),
              a_spec=pl.BlockSpec((tk, tm), lambda i, j, k: (k, i)),
              b_spec=pl.BlockSpec((tk, tn), lambda i, j, k: (k, j)),
              dims=TN, k_axis=2, acc=(tm, tn),
              outs=[(jax.ShapeDtypeStruct((m, n), out_dtype), pl.BlockSpec((tm, tn), lambda i, j, k: (i, j)))],
              vmem=VMEM_BIG)
    return res[0]


def _row_block(s):
    return _tile(s, 512)


def _modulate(x, pv, name):
    s, d = x.shape
    tb = _row_block(s)

    def body(x_ref, pv_ref, u_ref):
        u_ref[...] = _mx(x_ref[...] * pv_ref[0:1, :] + pv_ref[1:2, :])

    return pl.pallas_call(
        body, name=name, grid=(s // tb,),
        in_specs=[pl.BlockSpec((tb, d), lambda i: (i, 0)), pl.BlockSpec((8, d), lambda i: (0, 0))],
        out_specs=pl.BlockSpec((tb, d), lambda i: (i, 0)),
        out_shape=jax.ShapeDtypeStruct((s, d), MXU_DTYPE),
    )(x, pv)


def _ln_stats(r):
    mu = jnp.mean(r, axis=-1, keepdims=True)
    xc = r - mu
    var = jnp.mean(xc * xc, axis=-1, keepdims=True)
    rstd = lax.rsqrt(var + LN_EPS)
    return xc * rstd, rstd


def _ln_fwd(xin, y, pv, name):
    s, d = xin.shape
    tb = _row_block(s)

    def body(x_ref, y_ref, pv_ref, xo_ref, u_ref):
        r = ALPHA * x_ref[...] + pv_ref[0:1, :] * y_ref[...]
        xhat, _ = _ln_stats(r)
        xo = xhat * pv_ref[1:2, :] + pv_ref[2:3, :]
        xo_ref[...] = xo
        u_ref[...] = _mx(xo * pv_ref[3:4, :] + pv_ref[4:5, :])

    row = pl.BlockSpec((tb, d), lambda i: (i, 0))
    return pl.pallas_call(
        body, name=name, grid=(s // tb,),
        in_specs=[row, row, pl.BlockSpec((8, d), lambda i: (0, 0))],
        out_specs=[row, row],
        out_shape=[jax.ShapeDtypeStruct((s, d), F32), jax.ShapeDtypeStruct((s, d), MXU_DTYPE)],
    )(xin, y, pv)


def _ln_bwd(xin, y, pv, name, *, dxo=None, du=None, target=None):
    s, d = xin.shape
    tb = _row_block(s)
    nb = s // tb
    loss_mode = target is not None

    def body(*refs):
        if loss_mode:
            x_ref, y_ref, pv_ref, t_ref, dxin_ref, dy_ref, sums_ref = refs
        else:
            x_ref, y_ref, pv_ref, dxo_ref, du_ref, dxin_ref, dy_ref, sums_ref = refs
        i = pl.program_id(0)

        @pl.when(i == 0)
        def _():
            sums_ref[...] = jnp.zeros_like(sums_ref)

        yv = y_ref[...]
        r = ALPHA * x_ref[...] + pv_ref[0:1, :] * yv
        xhat, rstd = _ln_stats(r)
        xo = xhat * pv_ref[1:2, :] + pv_ref[2:3, :]
        if loss_mode:
            diff = xo - t_ref[...]
            dxo_v = diff * (1.0 / d)
            sums_ref[5:6, :] += jnp.sum(diff * diff, axis=0, keepdims=True) * (0.5 / d)
        else:
            duv = du_ref[...]
            dxo_v = dxo_ref[...] + duv * pv_ref[3:4, :]
            sums_ref[0:1, :] += jnp.sum(duv * xo, axis=0, keepdims=True)
            sums_ref[1:2, :] += jnp.sum(duv, axis=0, keepdims=True)
        sums_ref[2:3, :] += jnp.sum(dxo_v * xhat, axis=0, keepdims=True)
        sums_ref[3:4, :] += jnp.sum(dxo_v, axis=0, keepdims=True)
        dxh = dxo_v * pv_ref[1:2, :]
        dr = rstd * (dxh - jnp.mean(dxh, axis=-1, keepdims=True)
                     - xhat * jnp.mean(dxh * xhat, axis=-1, keepdims=True))
        sums_ref[4:5, :] += jnp.sum(dr * yv, axis=0, keepdims=True)
        dxin_ref[...] = ALPHA * dr
        dy_ref[...] = _mx(pv_ref[0:1, :] * dr)
        if loss_mode:
            @pl.when(i == nb - 1)
            def _():
                sums_ref[5:6, :] = jnp.broadcast_to(jnp.sum(sums_ref[5:6, :], axis=-1, keepdims=True), (1, d))

    row = pl.BlockSpec((tb, d), lambda i: (i, 0))
    par = pl.BlockSpec((8, d), lambda i: (0, 0))
    ins = [xin, y, pv] + ([target] if loss_mode else [dxo, du])
    return pl.pallas_call(
        body, name=name, grid=(nb,),
        in_specs=[row, row, par] + [row] * (len(ins) - 3),
        out_specs=[row, row, par],
        out_shape=[jax.ShapeDtypeStruct((s, d), F32), jax.ShapeDtypeStruct((s, d), MXU_DTYPE),
                   jax.ShapeDtypeStruct((8, d), F32)],
    )(*ins)


def _mod_bwd(dx_direct, du, x, pv, name):
    s, d = x.shape
    tb = _row_block(s)

    def body(dxd_ref, du_ref, x_ref, pv_ref, dx_ref, sums_ref):
        @pl.when(pl.program_id(0) == 0)
        def _():
            sums_ref[...] = jnp.zeros_like(sums_ref)

        duv = du_ref[...]
        dx_ref[...] = dxd_ref[...] + duv * pv_ref[0:1, :]
        sums_ref[0:1, :] += jnp.sum(duv * x_ref[...], axis=0, keepdims=True)
        sums_ref[1:2, :] += jnp.sum(duv, axis=0, keepdims=True)

    row = pl.BlockSpec((tb, d), lambda i: (i, 0))
    par = pl.BlockSpec((8, d), lambda i: (0, 0))
    return pl.pallas_call(
        body, name=name, grid=(s // tb,),
        in_specs=[row, row, row, par], out_specs=[row, par],
        out_shape=[jax.ShapeDtypeStruct((s, d), F32), jax.ShapeDtypeStruct((8, d), F32)],
    )(dx_direct, du, x, pv)


def _mlp_fwd(u, w1, w2, tag):
    s = u.shape[0]

    def epi(val, ex, out):
        out[0][...] = _mx(val)
        out[1][...] = _mx(jnp.square(jnp.maximum(val, 0.0)))

    tm, tn = _tile(s, MM_ROWS), 1024
    spec = pl.BlockSpec((tm, tn), lambda i, j, k: (i, j))
    shp = jax.ShapeDtypeStruct((s, D_FF), MXU_DTYPE)
    h, a = _mm_nn(f"mlp_up{tag}", u, w1, None, epi=epi, outs=[(shp, spec), (shp, spec)], tn=tn)
    y = _mm_nn(f"mlp_down{tag}", a, w2, F32)
    return y, (h, a)


def _mlp_bwd(dy, u, h, a, w1t, w2t, tag, carry=()):
    s = u.shape[0]
    tm, tn = _tile(s, MM_ROWS), 1024
    spec = pl.BlockSpec((tm, tn), lambda i, j, k: (i, j))

    def epi(val, ex, out):
        out[0][...] = _mx(val * (2.0 * jnp.maximum(ex[0][...].astype(F32), 0.0)))

    got = _mm_nn(f"mlp_dh{tag}", dy, w2t, None, epi=epi, extras=[(h, spec)],
                 outs=[(jax.ShapeDtypeStruct((s, D_FF), MXU_DTYPE), spec)], tn=tn, carry=carry)
    dh, carried = (got[0], list(got[1:])) if carry else (got, [])
    du = _mm_nn(f"mlp_du{tag}", dh, w1t, F32)
    dw2 = _mm_tn(f"mlp_dw2{tag}", a, dy)
    dw1 = _mm_tn(f"mlp_dw1{tag}", u, dh)
    return du, dw1, dw2, carried


FOX_T = 1024
BIAS_Q = (64, 65, 66)
BIAS_K = (67, 68, 69)
SKIP_MARGIN = 110.0
ONES_V = 64

def _fox_constants():
    selq = np.zeros((FOX_HEADS, 512, LANES), np.float32)
    selk = np.zeros((FOX_HEADS, 512, LANES), np.float32)
    selv = np.zeros((2, LANES, LANES), np.float32)
    put = np.zeros((2, 2, LANES, LANES), np.float32)
    for h in range(FOX_HEADS):
        off = FOX_HEAD_DIM * (h % 2)
        for dd in range(FOX_HEAD_DIM):
            selq[h, off + dd, dd] = FOX_HEAD_DIM ** -0.5
            selk[h, off + dd, dd] = 1.0
        for piece in range(3):
            selq[h, LANES * (1 + piece) + h, BIAS_Q[piece]] = 1.0
            selk[h, LANES * (1 + piece) + h, BIAS_K[piece]] = -1.0
    for par in range(2):
        for dd in range(FOX_HEAD_DIM):
            selv[par, FOX_HEAD_DIM * par + dd, dd] = 1.0
            put[par, 0, dd, FOX_HEAD_DIM * par + dd] = FOX_HEAD_DIM ** -0.5
            put[par, 1, dd, FOX_HEAD_DIM * par + dd] = 1.0
    return selq, selk, selv, put


def _fox_prep(qkv, f, bf, carry=()):
    s = qkv.shape[0]
    t = _tile(s, FOX_T)
    nb = s // t
    selq, selk, selv, _ = _fox_constants()
    whole = _exchange_shapes(list(carry))
    carry = _in_pieces(list(carry), nb * FOX_HEADS)
    anywhere = pl.BlockSpec(memory_space=pl.ANY)

    def body(q_ref, k_ref, v_ref, f_ref, bf_ref, selq_ref, selk_ref, selv_ref,
             qa_ref, qat_ref, ka_ref, kat_ref, va_ref, vat_ref, stats_ref, parts_ref, carry_ref, cum_ref):
        i, h = pl.program_id(0), pl.program_id(1)
        lane = lax.broadcasted_iota(jnp.int32, (1, LANES), 1)

        @pl.when(h == 0)
        def _():
            @pl.when(i == 0)
            def _():
                carry_ref[...] = jnp.zeros_like(carry_ref)

            lf = jnp.where(lane < FOX_HEADS, jax.nn.log_sigmoid(f_ref[...] + bf_ref[0:1, :]), 0.0)
            tri = (lax.broadcasted_iota(jnp.int32, (t, t), 0) >= lax.broadcasted_iota(jnp.int32, (t, t), 1)).astype(F32)
            cum = _dot_onehot(tri, lf, onehot="a") + carry_ref[0:1, :]
            carry_ref[0:1, :] = cum[t - 1:t, :]
            cum_ref[...] = cum
            hi = _mx(cum)
            r1 = cum - hi.astype(F32)
            mid = _mx(r1)
            parts_ref[:, 0:LANES] = hi
            parts_ref[:, LANES:2 * LANES] = mid
            parts_ref[:, 2 * LANES:3 * LANES] = _mx(r1 - mid.astype(F32))

        parts = parts_ref[...]
        qa = _dot(jnp.concatenate([q_ref[...], parts], axis=1), selq_ref[...])
        qa = qa + jnp.where((lane >= BIAS_K[0]) & (lane <= BIAS_K[2]), 1.0, 0.0)
        ka = _dot(jnp.concatenate([k_ref[...], parts], axis=1), selk_ref[...])
        ka = ka + jnp.where((lane >= BIAS_Q[0]) & (lane <= BIAS_Q[2]), 1.0, 0.0)
        va = _dot(v_ref[...], selv_ref[...]) + jnp.where(lane == ONES_V, 1.0, 0.0)
        qa_ref[...] = _mx(qa)
        qat_ref[...] = _mx(qa.T)
        ka_ref[...] = _mx(ka)
        kat_ref[...] = _mx(ka.T)
        va_ref[...] = _mx(va)
        vat_ref[...] = _mx(va.T)

        def longest(rows_):
            sq = jnp.where(lane < FOX_HEAD_DIM, rows_ * rows_, 0.0)
            return jnp.sqrt(jnp.max(jnp.sum(sq, axis=1, keepdims=True), axis=0, keepdims=True))

        mine = lane == h
        cum = cum_ref[...]
        top = jnp.max(jnp.max(jnp.where(mine, cum, -jnp.inf), axis=1, keepdims=True), axis=0, keepdims=True)
        low = jnp.min(jnp.min(jnp.where(mine, cum, jnp.inf), axis=1, keepdims=True), axis=0, keepdims=True)
        row = lax.broadcasted_iota(jnp.int32, (8, LANES), 0)
        stats_ref[...] = jnp.where(row == 0, longest(qa), jnp.where(row == 1, longest(ka),
                                                                    jnp.where(row == 2, top, low)))

    rows = jax.ShapeDtypeStruct((FOX_HEADS, nb, t, LANES), MXU_DTYPE)
    cols = jax.ShapeDtypeStruct((FOX_HEADS, nb, LANES, t), MXU_DTYPE)
    rspec = pl.BlockSpec((None, None, t, LANES), lambda i, h: (h, i, 0, 0))
    cspec = pl.BlockSpec((None, None, LANES, t), lambda i, h: (h, i, 0, 0))
    npair = FOX_HEADS // 2
    res = pl.pallas_call(
        _carrying(body, 8, 7, 3, carry, (nb, FOX_HEADS)), name="fox_prep", grid=(nb, FOX_HEADS),
        in_specs=[pl.BlockSpec((t, LANES), lambda i, h: (i, h // 2)),
                  pl.BlockSpec((t, LANES), lambda i, h: (i, npair + h // 2)),
                  pl.BlockSpec((t, LANES), lambda i, h: (i, 2 * npair + h // 2)),
                  pl.BlockSpec((t, LANES), lambda i, h: (i, 0)),
                  pl.BlockSpec((8, LANES), lambda i, h: (0, 0)),
                  pl.BlockSpec((None, 512, LANES), lambda i, h: (h, 0, 0)),
                  pl.BlockSpec((None, 512, LANES), lambda i, h: (h, 0, 0)),
                  pl.BlockSpec((None, LANES, LANES), lambda i, h: (h % 2, 0, 0))] + [anywhere] * len(carry),
        out_specs=[rspec, cspec, rspec, cspec, rspec, cspec,
                   pl.BlockSpec((None, None, 8, LANES), lambda i, h: (h, i, 0, 0))] + [anywhere] * len(carry),
        out_shape=[rows, cols, rows, cols, rows, cols, jax.ShapeDtypeStruct((FOX_HEADS, nb, 8, LANES), F32)]
        + _exchange_shapes(carry),
        scratch_shapes=[pltpu.VMEM((t, 3 * LANES), MXU_DTYPE), pltpu.VMEM((8, LANES), F32),
                        pltpu.VMEM((t, LANES), F32)] + (_exchange_sems(carry) if carry else []),
        compiler_params=_params(VMEM_BIG),
    )(qkv, qkv, qkv, f, bf, _mx(jnp.asarray(selq)), _mx(jnp.asarray(selk)), _mx(jnp.asarray(selv)),
      *[c[1] for c in carry])
    return (*res[:7], [r.reshape(w.shape) for r, w in zip(res[7:], whole)])


def _fox_active(stats):
    qn, kn, top, low = (stats[:, :, r, 0] for r in range(4))
    nb = qn.shape[1]
    gap = qn[:, :, None] * kn[:, None, :] + top[:, :, None] - low[:, None, :] + (qn * kn)[:, :, None]
    keep = (gap > -SKIP_MARGIN) | jnp.eye(nb, dtype=bool)[None]
    return jnp.where(keep, 1.0, 0.0).astype(F32).reshape(qn.shape[0], nb * nb)


def _causal_allow(t):
    return lax.broadcasted_iota(jnp.int32, (t, t), 0) <= lax.broadcasted_iota(jnp.int32, (t, t), 1)


def _fox_attn_fwd(qat, ka, vat, active, carry=()):
    heads, nb, _, t = qat.shape
    whole = _exchange_shapes(list(carry))
    carry = _in_pieces(list(carry), heads * nb)
    nc = len(carry)

    def body(act_ref, qat_ref, ka_ref, vat_ref, *rest):
        srcs, (ot_ref, lse_ref), dsts = rest[:nc], rest[nc:nc + 2], rest[nc + 2:2 * nc + 2]
        acc_ref, m_ref, kbuf_ref, vbuf_ref, fsems = rest[2 * nc + 2:2 * nc + 7]
        sems = rest[2 * nc + 7:]
        h, i = pl.program_id(0), pl.program_id(1)

        if nc:
            _exchange_start(carry, srcs, dsts, sems, step=h * nb + i)
        m_ref[...] = jnp.full_like(m_ref, -jnp.inf)
        acc_ref[...] = jnp.zeros_like(acc_ref)

        def key_blocks(j, slot):
            return [pltpu.make_async_copy(ka_ref.at[h, j], kbuf_ref.at[slot], fsems.at[0, slot]),
                    pltpu.make_async_copy(vat_ref.at[h, j], vbuf_ref.at[slot], fsems.at[1, slot])]

        def runs(j):
            return (j == i) | (act_ref[h, i * nb + j] > 0.5)

        def step(slot, diagonal):
            st = _dot(kbuf_ref[slot], qat_ref[...])
            if diagonal:
                st = jnp.where(_causal_allow(t), st, -jnp.inf)
            m_old = m_ref[...]
            m_new = jnp.maximum(m_old, jnp.max(st, axis=0, keepdims=True))
            pt = jnp.exp(st - m_new)
            acc_ref[...] = acc_ref[...] * jnp.exp(m_old - m_new) + _dot(vbuf_ref[slot], _mx(pt))
            m_ref[...] = m_new

        @pl.when(runs(0))
        def _():
            for cp in key_blocks(0, 0):
                cp.start()

        def earlier(j, c):
            slot = j % 2

            @pl.when(runs(j + 1))
            def _():
                for cp in key_blocks(j + 1, 1 - slot):
                    cp.start()

            @pl.when(runs(j))
            def _():
                for cp in key_blocks(j, slot):
                    cp.wait()
                step(slot, False)

            return c

        lax.fori_loop(0, i, earlier, 0)
        for cp in key_blocks(i, i % 2):
            cp.wait()
        step(i % 2, True)
        acc = acc_ref[...]
        denom = acc[ONES_V:ONES_V + 1, :]
        ot_ref[...] = _mx(acc / denom)
        lse_ref[...] = m_ref[...] + jnp.log(denom)

        if nc:
            @pl.when((h == heads - 1) & (i == nb - 1))
            def _():
                _exchange_wait(carry, srcs, dsts, sems)

    anywhere = pl.BlockSpec(memory_space=pl.ANY)
    qspec = pl.BlockSpec((None, None, LANES, t), lambda h, i: (h, i, 0, 0))
    res = pl.pallas_call(
        body, name="fox_attn_fwd", grid=(heads, nb),
        in_specs=[pl.BlockSpec(memory_space=pltpu.SMEM), qspec, anywhere, anywhere] + [anywhere] * nc,
        out_specs=[qspec, pl.BlockSpec((None, None, 1, t), lambda h, i: (h, i, 0, 0))] + [anywhere] * nc,
        out_shape=[jax.ShapeDtypeStruct((heads, nb, LANES, t), MXU_DTYPE),
                   jax.ShapeDtypeStruct((heads, nb, 1, t), F32)] + _exchange_shapes(carry),
        scratch_shapes=[pltpu.VMEM((LANES, t), F32), pltpu.VMEM((1, t), F32), pltpu.VMEM((2, t, LANES), MXU_DTYPE),
                        pltpu.VMEM((2, LANES, t), MXU_DTYPE), pltpu.SemaphoreType.DMA((2, 2))]
        + (_exchange_sems(carry) if nc else []),
        compiler_params=_params(VMEM_BIG),
    )(active, qat, ka, vat, *[c[1] for c in carry])
    return res[0], res[1], [r.reshape(w.shape) for r, w in zip(res[2:], whole)]


def _fox_attn_bwd(qa, qat, ka, kat, va, ot, lse, do, dot_, active, carry=()):
    heads, nb, t, _ = qa.shape
    whole = _exchange_shapes(list(carry))
    carry = _in_pieces(list(carry), heads * nb)
    nc = len(carry)

    def body(act_ref, qa_ref, qat_ref, ka_ref, kat_ref, va_ref, ot_ref, lse_ref, do_ref, dot_ref, *rest):
        srcs, (dqt_ref, dka_ref, dva_ref), dsts = rest[:nc], rest[nc:nc + 3], rest[nc + 3:2 * nc + 3]
        rows_ref, cols_ref, lseb_ref, fsems = rest[2 * nc + 3:2 * nc + 7]
        sems = rest[2 * nc + 7:]
        h, j = pl.program_id(0), pl.program_id(1)

        if nc:
            _exchange_start(carry, srcs, dsts, sems, step=h * nb + j)

        @pl.when(j == 0)
        def _():
            dqt_ref[...] = jnp.zeros_like(dqt_ref)

        def query_blocks(i, slot):
            cps = [pltpu.make_async_copy(src.at[h, i], rows_ref.at[slot, n], fsems.at[n, slot])
                   for n, src in enumerate((qa_ref, do_ref))]
            cps += [pltpu.make_async_copy(src.at[h, i], cols_ref.at[slot, n], fsems.at[2 + n, slot])
                    for n, src in enumerate((qat_ref, ot_ref, dot_ref))]
            return cps + [pltpu.make_async_copy(lse_ref.at[h, i], lseb_ref.at[slot], fsems.at[5, slot])]

        def runs(i):
            return (i == j) | (act_ref[h, i * nb + j] > 0.5)

        def step(i, slot, diagonal):
            st = _dot(ka_ref[...], cols_ref[slot, 0])
            dot_v = cols_ref[slot, 2]
            delta = jnp.sum(cols_ref[slot, 1].astype(F32) * dot_v.astype(F32), axis=0, keepdims=True)
            pt = jnp.exp(st - lseb_ref[slot])
            if diagonal:
                pt = jnp.where(_causal_allow(t), pt, 0.0)
            dsm = _mx(pt * (_dot(va_ref[...], dot_v) - delta))
            upd_v = _dot(_mx(pt), rows_ref[slot, 1])
            upd_k = _dot(dsm, rows_ref[slot, 0])
            if diagonal:
                dva_ref[...] = upd_v
                dka_ref[...] = upd_k
            else:
                dva_ref[...] += upd_v
                dka_ref[...] += upd_k
            dqt_ref[i] += _dot(kat_ref[...], dsm)

        def visit(i, slot, diagonal):
            nxt = jnp.minimum(i + 1, nb - 1)

            @pl.when((i + 1 < nb) & runs(nxt))
            def _():
                for cp in query_blocks(nxt, 1 - slot):
                    cp.start()

            @pl.when(runs(i))
            def _():
                for cp in query_blocks(i, slot):
                    cp.wait()
                step(i, slot, diagonal)

        for cp in query_blocks(j, 0):
            cp.start()
        visit(j, 0, True)

        def later(i, c):
            visit(i, (i - j) % 2, False)
            return c

        lax.fori_loop(j + 1, nb, later, 0)

        if nc:
            @pl.when((h == heads - 1) & (j == nb - 1))
            def _():
                _exchange_wait(carry, srcs, dsts, sems)

    def at_k(shape):
        return pl.BlockSpec((None, None) + shape, lambda h, j: (h, j, 0, 0))

    anywhere = pl.BlockSpec(memory_space=pl.ANY)
    res = pl.pallas_call(
        body, name="fox_attn_bwd", grid=(heads, nb),
        in_specs=[pl.BlockSpec(memory_space=pltpu.SMEM),
                  anywhere, anywhere, at_k((t, LANES)), at_k((LANES, t)), at_k((t, LANES)),
                  anywhere, anywhere, anywhere, anywhere] + [anywhere] * nc,
        out_specs=[pl.BlockSpec((None, nb, LANES, t), lambda h, j: (h, 0, 0, 0)),
                   at_k((t, LANES)), at_k((t, LANES))] + [anywhere] * nc,
        out_shape=[jax.ShapeDtypeStruct((heads, nb, LANES, t), F32),
                   jax.ShapeDtypeStruct((heads, nb, t, LANES), F32),
                   jax.ShapeDtypeStruct((heads, nb, t, LANES), F32)] + _exchange_shapes(carry),
        scratch_shapes=[pltpu.VMEM((2, 2, t, LANES), MXU_DTYPE), pltpu.VMEM((2, 3, LANES, t), MXU_DTYPE),
                        pltpu.VMEM((2, 1, t), F32), pltpu.SemaphoreType.DMA((6, 2))]
        + (_exchange_sems(carry) if nc else []),
        compiler_params=_params(VMEM_BIG),
    )(active, qa, qat, ka, kat, va, ot, lse, do, dot_, *[c[1] for c in carry])
    return res[0], res[1], res[2], [r.reshape(w.shape) for r, w in zip(res[3:], whole)]


def _fox_post(dqt, dka, dva, f, bf, carry=()):
    heads, nb, t, _ = dka.shape
    s = nb * t
    _, _, _, put = _fox_constants()
    whole = _exchange_shapes(list(carry))
    carry = _in_pieces(list(carry), nb * heads)
    anywhere = pl.BlockSpec(memory_space=pl.ANY)

    def body(dqt_ref, dka_ref, dva_ref, f_ref, bf_ref, put_ref, dq_ref, dk_ref, dv_ref, df_ref, sums_ref,
             dc_ref, carry_ref):
        i, h = pl.program_id(0), pl.program_id(1)

        @pl.when((i == 0) & (h == 0))
        def _():
            carry_ref[...] = jnp.zeros_like(carry_ref)
            sums_ref[...] = jnp.zeros_like(sums_ref)

        @pl.when(h == 0)
        def _():
            dc_ref[...] = jnp.zeros_like(dc_ref)

        dqt_v = dqt_ref[...]
        dka_v = dka_ref[...]
        term_q = _dot(_mx(dqt_v), put_ref[0], TN)
        term_k = _dot(_mx(dka_v), put_ref[1])
        term_v = _dot(_mx(dva_ref[...]), put_ref[1])

        @pl.when(h % 2 == 0)
        def _():
            dq_ref[...] = _mx(term_q)
            dk_ref[...] = _mx(term_k)
            dv_ref[...] = _mx(term_v)

        @pl.when(h % 2 == 1)
        def _():
            dq_ref[...] += _mx(term_q)
            dk_ref[...] += _mx(term_k)
            dv_ref[...] += _mx(term_v)

        dcum = dqt_v[BIAS_Q[0]:BIAS_Q[0] + 1, :] - dka_v.T[BIAS_K[0]:BIAS_K[0] + 1, :]
        head_row = lax.broadcasted_iota(jnp.int32, (heads, 1), 0) == h
        dc_ref[...] += jnp.where(head_row, dcum, 0.0)

        @pl.when(h == heads - 1)
        def _():
            later = (lax.broadcasted_iota(jnp.int32, (t, t), 0) >= lax.broadcasted_iota(jnp.int32, (t, t), 1)).astype(F32)
            dlf_t = _dot_onehot(dc_ref[...], later) + carry_ref[:, 0:1]
            carry_ref[...] = jnp.broadcast_to(dlf_t[:, 0:1], carry_ref.shape)
            dlf = jnp.concatenate([dlf_t, jnp.zeros((LANES - heads, t), F32)], axis=0).T
            lane = lax.broadcasted_iota(jnp.int32, (1, LANES), 1)
            df = jnp.where(lane < heads, dlf * jax.nn.sigmoid(-(f_ref[...] + bf_ref[0:1, :])), 0.0)
            df_ref[...] = _mx(df)
            sums_ref[0:1, :] += jnp.sum(df, axis=0, keepdims=True)

    rev = lambda i: nb - 1 - i
    pair_spec = pl.BlockSpec((t, LANES), lambda i, h: (rev(i), h // 2))
    blk = pl.BlockSpec((t, LANES), lambda i, h: (rev(i), 0))
    hd = jax.ShapeDtypeStruct((s, D_MODEL), MXU_DTYPE)
    res = pl.pallas_call(
        _carrying(body, 6, 5, 2, carry, (nb, heads)), name="fox_post", grid=(nb, heads),
        in_specs=[pl.BlockSpec((None, None, LANES, t), lambda i, h: (h, rev(i), 0, 0)),
                  pl.BlockSpec((None, None, t, LANES), lambda i, h: (h, rev(i), 0, 0)),
                  pl.BlockSpec((None, None, t, LANES), lambda i, h: (h, rev(i), 0, 0)),
                  blk, pl.BlockSpec((8, LANES), lambda i, h: (0, 0)),
                  pl.BlockSpec((None, 2, LANES, LANES), lambda i, h: (h % 2, 0, 0, 0))] + [anywhere] * len(carry),
        out_specs=[pair_spec, pair_spec, pair_spec, blk, pl.BlockSpec((8, LANES), lambda i, h: (0, 0))]
        + [anywhere] * len(carry),
        out_shape=[hd, hd, hd, jax.ShapeDtypeStruct((s, LANES), MXU_DTYPE), jax.ShapeDtypeStruct((8, LANES), F32)]
        + _exchange_shapes(carry),
        scratch_shapes=[pltpu.VMEM((heads, t), F32), pltpu.VMEM((heads, LANES), F32)]
        + (_exchange_sems(carry) if carry else []),
        compiler_params=_params(VMEM_BIG),
    )(dqt, dka, dva, f, bf, _mx(jnp.asarray(put)), *[c[1] for c in carry])
    return (*res[:5], [r.reshape(w.shape) for r, w in zip(res[5:], whole)])


def _fox_weights(w_in, w_o):
    wqkv = w_in[:, :3 * D_MODEL]
    wf = jnp.pad(w_in[:, 3 * D_MODEL:], ((0, 0), (0, LANES - FOX_HEADS)))
    wo_heads = w_o.reshape(FOX_HEADS, FOX_HEAD_DIM, D_MODEL)
    wo_a = jnp.pad(wo_heads, ((0, 0), (0, LANES - FOX_HEAD_DIM), (0, 0)))
    wo_rows = wo_a.reshape(FOX_HEADS * LANES, D_MODEL)
    return dict(wqkv=wqkv, wf=wf, wqkv_t=wqkv.T, wf_t=wf.T, wo_rows=wo_rows, wo_rows_t=wo_rows.T)


def _fox_out(ot, wo_rows):
    heads, nb, _, t = ot.shape

    def body(ot_ref, w_ref, y_ref):
        y_ref[...] = _dot(ot_ref[...].reshape(heads * LANES, t), w_ref[...], TN)

    return pl.pallas_call(
        body, name="fox_out", grid=(nb,),
        in_specs=[pl.BlockSpec((heads, None, LANES, t), lambda i: (0, i, 0, 0)),
                  pl.BlockSpec((heads * LANES, D_MODEL), lambda i: (0, 0))],
        out_specs=pl.BlockSpec((t, D_MODEL), lambda i: (i, 0)),
        out_shape=jax.ShapeDtypeStruct((nb * t, D_MODEL), F32),
        compiler_params=_params(VMEM_BIG),
    )(ot, wo_rows)


def _fox_do(dy, wo_rows_t, nb, t):
    heads = FOX_HEADS

    def body(dy_ref, w_ref, do_ref, dot_ref):
        val = _dot(dy_ref[...], w_ref[...])
        for h in range(heads):
            blk = val[:, h * LANES:(h + 1) * LANES]
            do_ref[h] = _mx(blk)
            dot_ref[h] = _mx(blk.T)

    return pl.pallas_call(
        body, name="fox_do", grid=(nb,),
        in_specs=[pl.BlockSpec((t, D_MODEL), lambda i: (i, 0)),
                  pl.BlockSpec((D_MODEL, heads * LANES), lambda i: (0, 0))],
        out_specs=[pl.BlockSpec((heads, None, t, LANES), lambda i: (0, i, 0, 0)),
                   pl.BlockSpec((heads, None, LANES, t), lambda i: (0, i, 0, 0))],
        out_shape=[jax.ShapeDtypeStruct((heads, nb, t, LANES), MXU_DTYPE),
                   jax.ShapeDtypeStruct((heads, nb, LANES, t), MXU_DTYPE)],
        compiler_params=_params(VMEM_BIG),
    )(dy, wo_rows_t)


def _fox_dwo(ot, dy):
    heads, nb, _, t = ot.shape

    def body(ot_ref, dy_ref, o_ref):
        part = _dot(ot_ref[...].reshape(heads * LANES, t), dy_ref[...])

        @pl.when(pl.program_id(0) == 0)
        def _():
            o_ref[...] = part

        @pl.when(pl.program_id(0) > 0)
        def _():
            o_ref[...] += part

    return pl.pallas_call(
        body, name="fox_dwo", grid=(nb,),
        in_specs=[pl.BlockSpec((heads, None, LANES, t), lambda i: (0, i, 0, 0)),
                  pl.BlockSpec((t, D_MODEL), lambda i: (i, 0))],
        out_specs=pl.BlockSpec((heads * LANES, D_MODEL), lambda i: (0, 0)),
        out_shape=jax.ShapeDtypeStruct((heads * LANES, D_MODEL), F32),
        compiler_params=_params(VMEM_BIG),
    )(ot, dy)


def _fox_fwd(u, w, bf, carry=()):
    qkv = _mm_nn("fox_qkv", u, w["wqkv"], MXU_DTYPE)
    f = _mm_nn("fox_f", u, w["wf"], F32)
    qa, qat, ka, kat, va, vat, stats, carried = _fox_prep(qkv, f, bf, carry)
    ot, lse, _ = _fox_attn_fwd(qat, ka, vat, _fox_active(stats))
    y = _fox_out(ot, w["wo_rows"])
    return y, dict(f=f, qa=qa, qat=qat, ka=ka, kat=kat, va=va, ot=ot, lse=lse, stats=stats), carried


def _fox_bwd(dy, u, w, bf, res, carry=()):
    heads, nb, t, _ = res["qa"].shape
    do, dot_ = _fox_do(dy, w["wo_rows_t"], nb, t)
    dwo_a = _fox_dwo(res["ot"], dy).reshape(heads, LANES, D_MODEL)
    dqt, dka, dva, _ = _fox_attn_bwd(res["qa"], res["qat"], res["ka"], res["kat"], res["va"], res["ot"],
                                     res["lse"], do, dot_, _fox_active(res["stats"]))
    dq, dk, dv, df, sums, carried = _fox_post(dqt, dka, dva, res["f"], bf, carry)
    dw_in = jnp.concatenate(
        [_mm_tn("fox_dw_q", u, dq), _mm_tn("fox_dw_k", u, dk), _mm_tn("fox_dw_v", u, dv),
         _mm_tn("fox_dw_f", u, df)[:, :FOX_HEADS]], axis=1)
    dw_o = dwo_a[:, :FOX_HEAD_DIM, :].reshape(D_MODEL, D_MODEL)
    wt = w["wqkv_t"]
    du, ex_o = _mm_nn("fox_du_q", dq, wt[:D_MODEL], F32, carry=[(True, _row_slots(dw_o[None]))])
    du, ex_in = _mm_nn("fox_du_k", dk, wt[D_MODEL:2 * D_MODEL], F32, addend=du,
                       carry=[(True, _col_slots(dw_in[None]))])
    du = _mm_nn("fox_du_v", dv, wt[2 * D_MODEL:], F32, addend=du)
    du = _mm_nn("fox_du_f", df, w["wf_t"], F32, addend=du)
    return du, sums, carried + [ex_in, ex_o]


def _dsilu(v):
    sg = jax.nn.sigmoid(v)
    return sg * (1.0 + v * (1.0 - sg))


def _conv_taps(scr_ref, w_ref, rows, base):
    acc = None
    for k in range(SSM_CONV):
        term = scr_ref[pl.ds(base - (SSM_CONV - 1) + k, rows), :] * w_ref[k:k + 1, :]
        acc = term if acc is None else acc + term
    return acc


def _conv_fwd(zx, cw, cb):
    s = zx.shape[0]
    tb = _tile(s, 512)
    half = SSM_CONV_DIM // 2
    hb = tb // SUBLANES

    def body(x_ref, halo_ref, w_ref, b_ref, o_ref, scr_ref):
        i = pl.program_id(0)
        scr_ref[pl.ds(0, SUBLANES), :] = jnp.where(i > 0, halo_ref[...], 0.0)
        scr_ref[pl.ds(SUBLANES, tb), :] = x_ref[...]
        o_ref[...] = jax.nn.silu(_conv_taps(scr_ref, w_ref, tb, SUBLANES) + b_ref[0:1, :])

    return pl.pallas_call(
        body, name="ssd_conv_fwd", grid=(s // tb, 2),
        in_specs=[pl.BlockSpec((tb, half), lambda i, j: (i, 1 + j)),
                  pl.BlockSpec((SUBLANES, half), lambda i, j: (jnp.maximum(i * hb - 1, 0), 1 + j)),
                  pl.BlockSpec((8, half), lambda i, j: (0, j)),
                  pl.BlockSpec((8, half), lambda i, j: (0, j))],
        out_specs=pl.BlockSpec((tb, half), lambda i, j: (i, j)),
        out_shape=jax.ShapeDtypeStruct((s, SSM_CONV_DIM), F32),
        scratch_shapes=[pltpu.VMEM((tb + SUBLANES, half), F32)],
    )(zx, zx, cw, cb)


def _conv_bwd_pre(zx, dxc, cw, cb):
    s = zx.shape[0]
    tb = _tile(s, 512)
    half = SSM_CONV_DIM // 2
    hb = tb // SUBLANES

    def body(x_ref, halo_ref, d_ref, w_ref, b_ref, o_ref, sums_ref, scr_ref):
        i = pl.program_id(1)

        @pl.when(i == 0)
        def _():
            sums_ref[...] = jnp.zeros_like(sums_ref)

        scr_ref[pl.ds(0, SUBLANES), :] = jnp.where(i > 0, halo_ref[...], 0.0)
        scr_ref[pl.ds(SUBLANES, tb), :] = x_ref[...]
        pre = _conv_taps(scr_ref, w_ref, tb, SUBLANES) + b_ref[0:1, :]
        dpre = d_ref[...] * _dsilu(pre)
        o_ref[...] = dpre
        for k in range(SSM_CONV):
            shifted = scr_ref[pl.ds(SUBLANES - (SSM_CONV - 1) + k, tb), :]
            sums_ref[k:k + 1, :] += jnp.sum(dpre * shifted, axis=0, keepdims=True)
        sums_ref[SSM_CONV:SSM_CONV + 1, :] += jnp.sum(dpre, axis=0, keepdims=True)

    return pl.pallas_call(
        body, name="ssd_conv_bwd_pre", grid=(2, s // tb),
        in_specs=[pl.BlockSpec((tb, half), lambda j, i: (i, 1 + j)),
                  pl.BlockSpec((SUBLANES, half), lambda j, i: (jnp.maximum(i * hb - 1, 0), 1 + j)),
                  pl.BlockSpec((tb, half), lambda j, i: (i, j)),
                  pl.BlockSpec((8, half), lambda j, i: (0, j)),
                  pl.BlockSpec((8, half), lambda j, i: (0, j))],
        out_specs=[pl.BlockSpec((tb, half), lambda j, i: (i, j)),
                   pl.BlockSpec((8, half), lambda j, i: (0, j))],
        out_shape=[jax.ShapeDtypeStruct((s, SSM_CONV_DIM), F32), jax.ShapeDtypeStruct((8, SSM_CONV_DIM), F32)],
        scratch_shapes=[pltpu.VMEM((tb + SUBLANES, half), F32)],
    )(zx, zx, dxc, cw, cb)


def _conv_bwd_x(dpre, cw):
    s = dpre.shape[0]
    tb = _tile(s, 512)
    hb = tb // SUBLANES
    nb = s // tb

    def body(d_ref, halo_ref, w_ref, o_ref, scr_ref):
        i = pl.program_id(0)
        scr_ref[pl.ds(0, tb), :] = d_ref[...]
        scr_ref[pl.ds(tb, SUBLANES), :] = jnp.where(i < nb - 1, halo_ref[...], 0.0)
        acc = None
        for k in range(SSM_CONV):
            term = scr_ref[pl.ds(SSM_CONV - 1 - k, tb), :] * w_ref[k:k + 1, :]
            acc = term if acc is None else acc + term
        o_ref[...] = _mx(acc)

    return pl.pallas_call(
        body, name="ssd_conv_bwd_x", grid=(nb,),
        in_specs=[pl.BlockSpec((tb, SSM_CONV_DIM), lambda i: (i, 0)),
                  pl.BlockSpec((SUBLANES, SSM_CONV_DIM), lambda i: (jnp.minimum((i + 1) * hb, s // SUBLANES - 1), 0)),
                  pl.BlockSpec((8, SSM_CONV_DIM), lambda i: (0, 0))],
        out_specs=pl.BlockSpec((tb, SSM_CONV_DIM), lambda i: (i, 0)),
        out_shape=jax.ShapeDtypeStruct((s, SSM_CONV_DIM), MXU_DTYPE),
        scratch_shapes=[pltpu.VMEM((tb + SUBLANES, SSM_CONV_DIM), F32)],
        compiler_params=_params(VMEM_BIG),
    )(dpre, dpre, cw)


def _expand_constants():
    ex = np.zeros((LANES, SSM_D_INNER), np.float32)
    for h in range(SSM_HEADS):
        ex[h, h * 64:(h + 1) * 64] = 1.0
    return ex, np.ascontiguousarray(ex.T)


def _ssd_common(dtr_ref, par_ref, ex_ref, xc_ref):
    lc = SSM_CHUNK
    lane = lax.broadcasted_iota(jnp.int32, (1, LANES), 1)
    is_head = lane < SSM_HEADS
    par = par_ref[...]
    pre = dtr_ref[...] + par[0:1, :]
    dt = jnp.where(is_head, jax.nn.softplus(pre), 0.0)
    a = jnp.where(is_head, -jnp.exp(par[1:2, :]), 0.0)
    tri_b = lax.broadcasted_iota(jnp.int32, (lc, lc), 0) >= lax.broadcasted_iota(jnp.int32, (lc, lc), 1)
    tri = tri_b.astype(F32)
    da = dt * a
    acs = _dot_onehot(tri, da, onehot="a")
    acs_t = _dot_onehot(da, tri, (((0,), (1,)), ((), ())))
    wide = _dot_onehot(jnp.concatenate([dt, acs, par], axis=0), ex_ref[...])
    dt_x, acs_x, d_x = wide[0:lc], wide[lc:2 * lc], wide[2 * lc + 2:2 * lc + 3]
    last_x = acs_x[lc - 1:lc, :]
    xs = xc_ref[:, 0:SSM_D_INNER]
    return dict(pre=pre, dt=dt, a=a, tri_b=tri_b, tri=tri, acs=acs, acs_t=acs_t, dt_x=dt_x, d_x=d_x, xs=xs,
                xdt=xs * dt_x, e_x=jnp.exp(acs_x), dte_x=jnp.exp(last_x - acs_x), cd_x=jnp.exp(last_x),
                is_head=is_head)


def _decay_in(q, h):
    seg = q["acs"][:, h:h + 1] - q["acs_t"][h:h + 1, :]
    return jnp.exp(jnp.where(q["tri_b"], seg, -jnp.inf))


def _ssd_scan_fwd(xc, dtr, par):
    s = xc.shape[0]
    lc = SSM_CHUNK
    nc = s // lc
    ex, _ = _expand_constants()

    def body(xc_ref, dtr_ref, par_ref, ex_ref, y_ref, prev_ref, st_ref):
        @pl.when(pl.program_id(0) == 0)
        def _():
            st_ref[...] = jnp.zeros_like(st_ref)

        q = _ssd_common(dtr_ref, par_ref, ex_ref, xc_ref)
        lane = lax.broadcasted_iota(jnp.int32, (1, LANES), 1)
        for g in range(SSM_GROUPS):
            sl = slice(g * GROUP_W, (g + 1) * GROUP_W)
            bg = _mx(xc_ref[:, SSM_D_INNER + g * SSM_STATE:SSM_D_INNER + (g + 1) * SSM_STATE])
            cg = _mx(xc_ref[:, SSM_D_INNER + (SSM_GROUPS + g) * SSM_STATE:SSM_D_INNER + (SSM_GROUPS + g + 1) * SSM_STATE])
            gm = _dot(cg, bg, NT)
            prev = st_ref[g]
            prev_ref[g] = prev
            yoff = _dot(cg, _mx(prev)) * q["e_x"][:, sl]
            st_ref[g] = prev * q["cd_x"][:, sl] + _dot(bg, _mx(q["xdt"][:, sl] * q["dte_x"][:, sl]), TN)
            pairs = []
            for pr in range(2):
                xp = _mx(q["xdt"][:, g * GROUP_W + pr * LANES:g * GROUP_W + (pr + 1) * LANES])
                both = [_dot(_mx(gm * _decay_in(q, 4 * g + 2 * pr + r2)), xp) for r2 in range(2)]
                pairs.append(jnp.where(lane < 64, both[0], both[1]))
            y_ref[:, sl] = jnp.concatenate(pairs, axis=1) + yoff + q["xs"][:, sl] * q["d_x"][:, sl]

    return pl.pallas_call(
        body, name="ssd_scan_fwd", grid=(nc,),
        in_specs=[pl.BlockSpec((lc, SSM_CONV_DIM), lambda c: (c, 0)),
                  pl.BlockSpec((lc, LANES), lambda c: (c, 0)),
                  pl.BlockSpec((8, LANES), lambda c: (0, 0)),
                  pl.BlockSpec((LANES, SSM_D_INNER), lambda c: (0, 0))],
        out_specs=[pl.BlockSpec((lc, SSM_D_INNER), lambda c: (c, 0)),
                   pl.BlockSpec((None, SSM_GROUPS, SSM_STATE, GROUP_W), lambda c: (c, 0, 0, 0))],
        out_shape=[jax.ShapeDtypeStruct((s, SSM_D_INNER), F32),
                   jax.ShapeDtypeStruct((nc, SSM_GROUPS, SSM_STATE, GROUP_W), F32)],
        scratch_shapes=[pltpu.VMEM((SSM_GROUPS, SSM_STATE, GROUP_W), F32)],
        compiler_params=_params(VMEM_BIG),
    )(xc, dtr, par, _mx(jnp.asarray(ex)))


def _ssd_scan_bwd(dy, xc, dtr, par, prev, carry=()):
    s = xc.shape[0]
    lc = SSM_CHUNK
    nc = s // lc
    ex, ex_t = _expand_constants()
    whole = _exchange_shapes(list(carry))
    carry = _in_pieces(list(carry), nc)
    anywhere = pl.BlockSpec(memory_space=pl.ANY)

    def body(dy_ref, xc_ref, dtr_ref, par_ref, prev_ref, ex_ref, ext_ref, dxc_ref, ddtr_ref, sums_ref,
             gst_ref, tacs_ref, tdt_ref, tdd_ref):
        @pl.when(pl.program_id(0) == 0)
        def _():
            gst_ref[...] = jnp.zeros_like(gst_ref)
            sums_ref[...] = jnp.zeros_like(sums_ref)

        q = _ssd_common(dtr_ref, par_ref, ex_ref, xc_ref)
        lane = lax.broadcasted_iota(jnp.int32, (1, LANES), 1)
        row = lax.broadcasted_iota(jnp.int32, (lc, 1), 0)
        dacs_rows = jnp.zeros((lc, LANES), F32)
        dacs_cols_t = jnp.zeros((LANES, lc), F32)
        for g in range(SSM_GROUPS):
            sl = slice(g * GROUP_W, (g + 1) * GROUP_W)
            b_lo = SSM_D_INNER + g * SSM_STATE
            c_lo = SSM_D_INNER + (SSM_GROUPS + g) * SSM_STATE
            bg = _mx(xc_ref[:, b_lo:b_lo + SSM_STATE])
            cg = _mx(xc_ref[:, c_lo:c_lo + SSM_STATE])
            dyg = dy_ref[:, sl]
            xsg, xdtg = q["xs"][:, sl], q["xdt"][:, sl]
            eg, dteg, cdg = q["e_x"][:, sl], q["dte_x"][:, sl], q["cd_x"][:, sl]
            prevg = prev_ref[g]
            gs = gst_ref[g]
            prevm, gsm = _mx(prevg), _mx(gs)
            tdd_ref[:, sl] = dyg * xsg
            dxs = dyg * q["d_x"][:, sl]
            t_acs = dyg * _dot(cg, prevm) * eg
            dcp = _mx(dyg * eg)
            dc = _dot(dcp, prevm, NT)
            dprev = _dot(cg, dcp, TN)
            db = _dot(_mx(xdtg * dteg), gsm, NT)
            dx2 = _dot(bg, gsm)
            dxdt = dx2 * dteg
            ddte = dx2 * xdtg * dteg
            t_acs = t_acs - ddte
            last = (jnp.sum(ddte, axis=0, keepdims=True)
                    + jnp.sum(gs * prevg, axis=0, keepdims=True) * cdg)
            gm = _dot(cg, bg, NT)
            dgm = jnp.zeros((lc, lc), F32)
            pair_dx = []
            for pr in range(2):
                lo = g * GROUP_W + pr * LANES
                xp = _mx(q["xdt"][:, lo:lo + LANES])
                dyp = dy_ref[:, lo:lo + LANES]
                both = []
                for r2 in range(2):
                    h = 4 * g + 2 * pr + r2
                    mine = (lane >= 64 * r2) & (lane < 64 * (r2 + 1))
                    lm = _decay_in(q, h)
                    m = gm * lm
                    dm = _dot(_mx(jnp.where(mine, dyp, 0.0)), xp, NT)
                    dgm = dgm + dm * lm
                    w = dm * m
                    dacs_rows = dacs_rows + jnp.sum(w, axis=1, keepdims=True) * (lane == h).astype(F32)
                    head_row = (lax.broadcasted_iota(jnp.int32, (LANES, 1), 0) == h).astype(F32)
                    dacs_cols_t = dacs_cols_t + head_row * jnp.sum(w, axis=0, keepdims=True)
                    both.append(_dot(_mx(m), _mx(dyp), TN))
                pair_dx.append(jnp.where(lane < 64, both[0], both[1]))
            dxdt = dxdt + jnp.concatenate(pair_dx, axis=1)
            dgmm = _mx(dgm)
            dc = dc + _dot(dgmm, bg)
            db = db + _dot(dgmm, cg, TN)
            dxs = dxs + dxdt * q["dt_x"][:, sl]
            tdt_ref[:, sl] = dxdt * xsg
            tacs_ref[:, sl] = t_acs + jnp.where(row == lc - 1, last, 0.0)
            dxc_ref[:, sl] = dxs
            dxc_ref[:, b_lo:b_lo + SSM_STATE] = db
            dxc_ref[:, c_lo:c_lo + SSM_STATE] = dc
            gst_ref[g] = gs * cdg + dprev
        tdd = jnp.broadcast_to(jnp.sum(tdd_ref[...], axis=0, keepdims=True), (8, SSM_D_INNER))
        heads_of = _dot_onehot(jnp.concatenate([tacs_ref[...], tdt_ref[...], tdd], axis=0), ext_ref[...])
        dacs = heads_of[0:lc] + dacs_rows - dacs_cols_t.T
        dda = _dot_onehot(q["tri"], dacs, TN, onehot="a")
        ddt = dda * q["a"] + heads_of[lc:2 * lc]
        ddtr = jnp.where(q["is_head"], ddt * jax.nn.sigmoid(q["pre"]), 0.0)
        ddtr_ref[...] = _mx(ddtr)
        sums_ref[0:1, :] += jnp.sum(ddtr, axis=0, keepdims=True)
        sums_ref[1:2, :] += jnp.sum(dda * q["dt"], axis=0, keepdims=True) * q["a"]
        sums_ref[2:3, :] += heads_of[2 * lc:2 * lc + 1]

    rev = lambda c: nc - 1 - c
    wide = pltpu.VMEM((lc, SSM_D_INNER), F32)
    res = pl.pallas_call(
        _carrying(body, 7, 3, 4, carry, (nc,)), name="ssd_scan_bwd", grid=(nc,),
        in_specs=[pl.BlockSpec((lc, SSM_D_INNER), lambda c: (rev(c), 0)),
                  pl.BlockSpec((lc, SSM_CONV_DIM), lambda c: (rev(c), 0)),
                  pl.BlockSpec((lc, LANES), lambda c: (rev(c), 0)),
                  pl.BlockSpec((8, LANES), lambda c: (0, 0)),
                  pl.BlockSpec((None, SSM_GROUPS, SSM_STATE, GROUP_W), lambda c: (rev(c), 0, 0, 0)),
                  pl.BlockSpec((LANES, SSM_D_INNER), lambda c: (0, 0)),
                  pl.BlockSpec((SSM_D_INNER, LANES), lambda c: (0, 0))] + [anywhere] * len(carry),
        out_specs=[pl.BlockSpec((lc, SSM_CONV_DIM), lambda c: (rev(c), 0)),
                   pl.BlockSpec((lc, LANES), lambda c: (rev(c), 0)),
                   pl.BlockSpec((8, LANES), lambda c: (0, 0))] + [anywhere] * len(carry),
        out_shape=[jax.ShapeDtypeStruct((s, SSM_CONV_DIM), F32), jax.ShapeDtypeStruct((s, LANES), MXU_DTYPE),
                   jax.ShapeDtypeStruct((8, LANES), F32)] + _exchange_shapes(carry),
        scratch_shapes=[pltpu.VMEM((SSM_GROUPS, SSM_STATE, GROUP_W), F32), wide, wide, wide]
        + (_exchange_sems(carry) if carry else []),
        compiler_params=_params(VMEM_BIG),
    )(dy, xc, dtr, par, prev, _mx(jnp.asarray(ex)), _mx(jnp.asarray(ex_t)), *[c[1] for c in carry])
    return (*res[:3], [r.reshape(w.shape) for r, w in zip(res[3:], whole)])


def _group_norm_parts(yv, zv):
    yg = yv * jax.nn.silu(zv)
    normed, rinvs = [], []
    for g in range(SSM_GROUPS):
        blk = yg[:, g * GROUP_W:(g + 1) * GROUP_W]
        rinv = lax.rsqrt(jnp.mean(blk * blk, axis=-1, keepdims=True) + RMS_EPS)
        normed.append(blk * rinv)
        rinvs.append(rinv)
    return normed, rinvs


def _gnorm_fwd(y, zx, nw):
    s = y.shape[0]
    tb = _tile(s, 512)

    def body(y_ref, z_ref, w_ref, o_ref):
        normed, _ = _group_norm_parts(y_ref[...], z_ref[...])
        for g in range(SSM_GROUPS):
            sl = slice(g * GROUP_W, (g + 1) * GROUP_W)
            o_ref[:, sl] = _mx(normed[g] * w_ref[0:1, sl])

    row = pl.BlockSpec((tb, SSM_D_INNER), lambda i: (i, 0))
    return pl.pallas_call(
        body, name="ssd_gnorm_fwd", grid=(s // tb,),
        in_specs=[row, row, pl.BlockSpec((8, SSM_D_INNER), lambda i: (0, 0))],
        out_specs=row, out_shape=jax.ShapeDtypeStruct((s, SSM_D_INNER), MXU_DTYPE),
    )(y, zx, nw)


def _gnorm_bwd(y, zx, nw, dyn):
    s = y.shape[0]
    tb = _tile(s, 512)

    def body(y_ref, z_ref, w_ref, d_ref, dy_ref, dz_ref, sums_ref):
        @pl.when(pl.program_id(0) == 0)
        def _():
            sums_ref[...] = jnp.zeros_like(sums_ref)

        yv, zv = y_ref[...], z_ref[...]
        normed, rinvs = _group_norm_parts(yv, zv)
        gate = jax.nn.silu(zv)
        dgate = _dsilu(zv)
        for g in range(SSM_GROUPS):
            sl = slice(g * GROUP_W, (g + 1) * GROUP_W)
            dv = d_ref[:, sl]
            n = normed[g]
            sums_ref[0:1, sl] += jnp.sum(dv * n, axis=0, keepdims=True)
            dn = dv * w_ref[0:1, sl]
            dyg = rinvs[g] * (dn - n * jnp.mean(dn * n, axis=-1, keepdims=True))
            dy_ref[:, sl] = dyg * gate[:, sl]
            dz_ref[:, sl] = _mx(dyg * yv[:, sl] * dgate[:, sl])

    row = pl.BlockSpec((tb, SSM_D_INNER), lambda i: (i, 0))
    par = pl.BlockSpec((8, SSM_D_INNER), lambda i: (0, 0))
    return pl.pallas_call(
        body, name="ssd_gnorm_bwd", grid=(s // tb,),
        in_specs=[row, row, par, row], out_specs=[row, row, par],
        out_shape=[jax.ShapeDtypeStruct((s, SSM_D_INNER), F32), jax.ShapeDtypeStruct((s, SSM_D_INNER), MXU_DTYPE),
                   jax.ShapeDtypeStruct((8, SSM_D_INNER), F32)],
    )(y, zx, nw, dyn)


def _rows8(v):
    v = v.reshape(1, -1)
    return jnp.pad(v, ((0, 7), (0, 0)))


def _ssd_weights(w_in, w_out):
    nzx = SSM_D_INNER + SSM_CONV_DIM
    wzx = w_in[:, :nzx]
    wdt = jnp.pad(w_in[:, nzx:], ((0, 0), (0, LANES - SSM_HEADS)))
    return dict(wzx=wzx, wdt=wdt, wzx_t=wzx.T, wdt_t=wdt.T, wout=w_out, wout_t=w_out.T)


def _ssd_fwd(u, w, cw, cb, par, nw):
    zx = _mm_nn("ssd_in_zx", u, w["wzx"], F32)
    dtr = _mm_nn("ssd_in_dt", u, w["wdt"], F32)
    xc = _conv_fwd(zx, cw, cb)
    y, prev = _ssd_scan_fwd(xc, dtr, par)
    yn = _gnorm_fwd(y, zx, nw)
    out = _mm_nn("ssd_out", yn, w["wout"], F32)
    return out, dict(zx=zx, dtr=dtr, xc=xc, y=y, prev=prev, yn=yn)


def _ssd_bwd(dy, u, w, cw, cb, par, nw, res, carry=()):
    dyn = _mm_nn("ssd_dyn", dy, w["wout_t"], F32)
    dw_out = _mm_tn("ssd_dw_out", res["yn"], dy)
    dys, dz, nsum = _gnorm_bwd(res["y"], res["zx"], nw, dyn)
    dxc, ddtr, ssum, carried = _ssd_scan_bwd(dys, res["xc"], res["dtr"], par, res["prev"],
                                             list(carry) + [(True, _row_slots(dw_out[None]))])
    dpre, csum = _conv_bwd_pre(res["zx"], dxc, cw, cb)
    dxbc = _conv_bwd_x(dpre, cw)
    wt = w["wzx_t"]
    du = _mm_nn("ssd_du_z", dz, wt[:SSM_D_INNER], F32)
    du = _mm_nn("ssd_du_x", dxbc, wt[SSM_D_INNER:], F32, addend=du)
    du = _mm_nn("ssd_du_dt", ddtr, w["wdt_t"], F32, addend=du)
    dw_in = jnp.concatenate(
        [_mm_tn("ssd_dw_z", u, dz), _mm_tn("ssd_dw_x", u, dxbc), _mm_tn("ssd_dw_dt", u, ddtr)[:, :SSM_HEADS]], axis=1)
    small = dict(conv_w=csum[:SSM_CONV], conv_b=csum[SSM_CONV], dt_bias=ssum[0, :SSM_HEADS],
                 a_log=ssum[1, :SSM_HEADS], d=ssum[2, :SSM_HEADS], norm_w=nsum[0])
    return du, dw_in, small, carried


def _ada_fwd(c_all, ada_w, ada_b_mine):
    nl, _, ncol = ada_w.shape

    def body(c_ref, w_ref, b_ref, o_ref):
        cond = _mx(jax.nn.silu(c_ref[...]))
        for i in range(nl):
            o_ref[i] = _dot(cond, _mx(w_ref[i])) + b_ref[i:i + 1, :]

    return pl.pallas_call(
        body, name="ada_fwd", out_shape=jax.ShapeDtypeStruct((nl, 2 * N_DEV, ncol), F32),
        compiler_params=_params(VMEM_BIG),
    )(c_all, ada_w, ada_b_mine)


def _ada_bwd(c_all, dmod_cols):
    nl, _, ncol = dmod_cols.shape

    def body(c_ref, d_ref, o_ref):
        cond = _mx(jax.nn.silu(c_ref[...]))
        for i in range(nl):
            o_ref[i] = _dot(cond, _mx(d_ref[i]), TN)

    return pl.pallas_call(
        body, name="ada_bwd", out_shape=jax.ShapeDtypeStruct((nl, D_MODEL, ncol), F32),
        compiler_params=_params(VMEM_BIG),
    )(c_all, dmod_cols)


def _adamw(gslots, w, m, v, name):
    k, r, c = gslots.shape
    tr = _tile(r, 256) if r % 256 == 0 else r
    c1 = 1.0 - ADAM_B1 ** ADAM_STEP
    c2 = 1.0 - ADAM_B2 ** ADAM_STEP

    def body(g_ref, w_ref, m_ref, v_ref, go_ref, d_ref, mo_ref, vo_ref):
        g = g_ref[0]
        for slot in range(1, k):
            g = g + g_ref[slot]
        mn = ADAM_B1 * m_ref[...] + (1.0 - ADAM_B1) * g
        vn = ADAM_B2 * v_ref[...] + (1.0 - ADAM_B2) * jnp.square(g)
        go_ref[...] = g
        mo_ref[...] = mn
        vo_ref[...] = vn
        d_ref[...] = -ADAM_LR * ((mn / c1) / (jnp.sqrt(vn / c2) + ADAM_EPS) + ADAM_WD * w_ref[...])

    row = pl.BlockSpec((tr, c), lambda i: (i, 0))
    shp = jax.ShapeDtypeStruct((r, c), F32)
    return pl.pallas_call(
        body, name=name, grid=(r // tr,),
        in_specs=[pl.BlockSpec((k, tr, c), lambda i: (0, i, 0)), row, row, row],
        out_specs=[row, row, row, row], out_shape=[shp, shp, shp, shp],
        compiler_params=_params(VMEM_BIG),
    )(gslots, w, m, v)


def _adamw_any(gslots, w, m, v, name):
    shape = w.shape
    two_d = (-1, shape[-1])
    k = gslots.shape[0]
    outs = _adamw(gslots.reshape((k,) + w.reshape(two_d).shape), w.reshape(two_d), m.reshape(two_d),
                  v.reshape(two_d), name)
    return tuple(o.reshape(shape) for o in outs)


def _cols_from_slots(g):
    g = jnp.moveaxis(g, 0, -2)
    return g.reshape(g.shape[:-2] + (g.shape[-2] * g.shape[-1],))


def _rows_from_slots(g):
    g = jnp.moveaxis(g, 0, -3)
    return g.reshape(g.shape[:-3] + (g.shape[-3] * g.shape[-2], g.shape[-1]))


def _col_slots(g):
    cs = g.shape[-1] // N_DEV
    return jnp.moveaxis(g.reshape(g.shape[:-1] + (N_DEV, cs)), -2, 0)


def _row_slots(g):
    rs = g.shape[-2] // N_DEV
    return jnp.moveaxis(g.reshape(g.shape[:-2] + (N_DEV, rs, g.shape[-1])), -3, 0)


def _gather_cols(w, name, dtype=None):
    return _cols_from_slots(_all_gather(w.astype(dtype or MXU_DTYPE), name))


def _gather_rows(w, name):
    return _rows_from_slots(_all_gather(_mx(w), name))


def _scatter_cols(g, name):
    return _all_to_all(_col_slots(g), name)


def _scatter_rows(g, name):
    return _all_to_all(_row_slots(g), name)


def kernel(x, c, ada_w, ada_b, ln_mix_g, ln_mix_b, ln_mlp_g, ln_mlp_b, mlp_w1, mlp_w2, fox_w_in, fox_b_f, fox_w_o, ssm_w_in, ssm_conv_w, ssm_conv_b, ssm_dt_bias, ssm_a_log, ssm_d, ssm_norm_w, ssm_w_out, loss_target, m_ada_w, m_ada_b, m_ln_mix_g, m_ln_mix_b, m_ln_mlp_g, m_ln_mlp_b, m_mlp_w1, m_mlp_w2, m_fox_w_in, m_fox_b_f, m_fox_w_o, m_ssm_w_in, m_ssm_conv_w, m_ssm_conv_b, m_ssm_dt_bias, m_ssm_a_log, m_ssm_d, m_ssm_norm_w, m_ssm_w_out, v_ada_w, v_ada_b, v_ln_mix_g, v_ln_mix_b, v_ln_mlp_g, v_ln_mlp_b, v_mlp_w1, v_mlp_w2, v_fox_w_in, v_fox_b_f, v_fox_w_o, v_ssm_w_in, v_ssm_conv_w, v_ssm_conv_b, v_ssm_dt_bias, v_ssm_a_log, v_ssm_d, v_ssm_norm_w, v_ssm_w_out):
    me = 4 * lax.axis_index("x") + 2 * lax.axis_index("y") + lax.axis_index("c")
    xs = x[0]
    target = loss_target[0]
    d = D_MODEL

    c_all = _all_gather(c, "gather_c").reshape(N_DEV, d)
    c_all = jnp.pad(c_all, ((0, N_DEV), (0, 0)))
    ncol = ada_w.shape[-1]
    ada_b_mine = lax.dynamic_slice_in_dim(ada_b, me * ncol, ncol, axis=1)
    mod_cols = _ada_fwd(c_all, ada_w, ada_b_mine)
    mod_all = _all_gather(mod_cols, "gather_mod")
    mod = lax.dynamic_index_in_dim(mod_all, me, axis=2, keepdims=False)
    mod = jnp.moveaxis(mod, 0, 1).reshape(DEPTH, 6, d)

    def pv_rows(*rows):
        return jnp.pad(jnp.stack(rows), ((0, 8 - len(rows)), (0, 0)))

    fw = _fox_weights(_gather_cols(fox_w_in, "gather_fox_in")[0], _gather_rows(fox_w_o, "gather_fox_o")[0])
    conv_w = _gather_cols(ssm_conv_w, "gather_conv_w", F32)[0]
    small_vec = jnp.concatenate([ssm_conv_b[0], ssm_norm_w[0]]).reshape(1, -1)
    small_all = _all_gather(small_vec.astype(F32), "gather_conv_b").reshape(N_DEV, -1)
    conv_b = small_all[:, :SSM_CONV_DIM // N_DEV].reshape(-1)
    norm_w = small_all[:, SSM_CONV_DIM // N_DEV:].reshape(-1)
    cw8 = jnp.pad(conv_w, ((0, 8 - SSM_CONV), (0, 0)))
    cb8 = _rows8(conv_b)
    nw8 = _rows8(norm_w)
    bf8 = _rows8(jnp.pad(fox_b_f[0], (0, LANES - FOX_HEADS)))
    par8 = jnp.pad(jnp.stack([jnp.pad(p[0], (0, LANES - SSM_HEADS)) for p in (ssm_dt_bias, ssm_a_log, ssm_d)]),
                   ((0, 5), (0, 0)))

    sh_a, sc_a, g_a, sh_m, sc_m, g_m = [mod[:, k] for k in range(6)]
    u0 = _modulate(xs, pv_rows(1.0 + sc_a[0], sh_a[0]), "modulate0")
    y0, fres, gathered = _fox_fwd(u0, fw, bf8, [(False, _mx(w)) for w in (mlp_w1, mlp_w2, ssm_w_in, ssm_w_out)])
    w1 = _cols_from_slots(gathered[0])
    w2 = _rows_from_slots(gathered[1])
    sw = _ssd_weights(_cols_from_slots(gathered[2])[0], _rows_from_slots(gathered[3])[0])
    pv0 = pv_rows(1.0 + g_a[0], ln_mix_g[0], ln_mix_b[0], 1.0 + sc_m[0], sh_m[0])
    x1, u1 = _ln_fwd(xs, y0, pv0, "ln_mix0")
    y1, (h0, a0) = _mlp_fwd(u1, w1[0], w2[0], "0")
    pv1 = pv_rows(1.0 + g_m[0], ln_mlp_g[0], ln_mlp_b[0], 1.0 + sc_a[1], sh_a[1])
    x2, u2 = _ln_fwd(x1, y1, pv1, "ln_mlp0")
    y2, sres = _ssd_fwd(u2, sw, cw8, cb8, par8, nw8)
    pv2 = pv_rows(1.0 + g_a[1], ln_mix_g[1], ln_mix_b[1], 1.0 + sc_m[1], sh_m[1])
    x3, u3 = _ln_fwd(x2, y2, pv2, "ln_mix1")
    y3, (h1, a1) = _mlp_fwd(u3, w1[1], w2[1], "1")
    pv3 = pv_rows(1.0 + g_m[1], ln_mlp_g[1], ln_mlp_b[1])

    dx3, dy3, s3 = _ln_bwd(x3, y3, pv3, "ln_mlp1_bwd", target=target)
    loss = lax.psum(s3[5, 0], ("x", "y", "c"))
    du3, dw1_1, dw2_1, _ = _mlp_bwd(dy3, u3, h1, a1, w1[1].T, w2[1].T, "1")
    dx2, dy2, s2 = _ln_bwd(x2, y2, pv2, "ln_mix1_bwd", dxo=dx3, du=du3)
    du2, d_ssm_in, ssm_small, ex_scan = _ssd_bwd(
        dy2, u2, sw, cw8, cb8, par8, nw8, sres,
        [(True, _col_slots(dw1_1[None])), (True, _row_slots(dw2_1[None]))])
    dx1, dy1, s1 = _ln_bwd(x1, y1, pv1, "ln_mlp0_bwd", dxo=dx2, du=du2)
    du1, dw1_0, dw2_0, ex_mlp = _mlp_bwd(dy1, u1, h0, a0, w1[0].T, w2[0].T, "0",
                                         [(True, _col_slots(d_ssm_in[None]))])
    dx0, dy0, s0 = _ln_bwd(xs, y0, pv0, "ln_mix0_bwd", dxo=dx1, du=du1)
    late = [(True, _col_slots(dw1_0[None])), (True, _row_slots(dw2_0[None])),
            (True, _col_slots(ssm_small["conv_w"][None])), (True, _col_slots(ssm_small["conv_b"][None])),
            (True, _col_slots(ssm_small["norm_w"][None]))]
    du0, fox_sums, ex_post = _fox_bwd(dy0, u0, fw, bf8, fres, late)
    grad_x, sx = _mod_bwd(dx0, du0, xs, pv_rows(1.0 + sc_a[0], sh_a[0]), "modulate0_bwd")

    dmod = jnp.stack([
        jnp.stack([sx[1], sx[0], s0[4], s0[1], s0[0], s1[4]]),
        jnp.stack([s1[1], s1[0], s2[4], s2[1], s2[0], s3[4]]),
    ]).reshape(DEPTH, 6 * d)

    def pad_rows(v):
        v = v.reshape(-1, LANES) if v.size % LANES == 0 else jnp.pad(v.reshape(-1), (0, LANES - v.size)).reshape(1, LANES)
        return jnp.pad(v, ((0, (-v.shape[0]) % 8), (0, 0)))

    small_parts = [dmod, jnp.stack([s0[2], s2[2]]), jnp.stack([s0[3], s2[3]]), jnp.stack([s1[2], s3[2]]),
                   jnp.stack([s1[3], s3[3]]), fox_sums[0, :FOX_HEADS], ssm_small["dt_bias"], ssm_small["a_log"],
                   ssm_small["d"]]
    packed = [pad_rows(p) for p in small_parts]
    offsets = np.cumsum([0] + [p.shape[0] for p in packed])
    small_all_g = _all_gather(jnp.concatenate(packed, axis=0), "gather_small_grads")

    def unpack(idx, shape):
        n = int(np.prod(shape))
        blk = small_all_g[:, offsets[idx]:offsets[idx + 1]].reshape(N_DEV, -1)[:, :n]
        return blk.reshape((N_DEV,) + tuple(shape))

    dmod_all = unpack(0, (DEPTH, 6 * d))
    dmod_cols = lax.dynamic_slice_in_dim(dmod_all, me * ncol, ncol, axis=2)
    dmod_cols = jnp.pad(jnp.moveaxis(dmod_cols, 0, 1), ((0, 0), (0, N_DEV), (0, 0)))
    g_ada_w = _ada_bwd(c_all, dmod_cols)

    shards = dict(
        mlp_w1=jnp.concatenate([ex_post[0], ex_scan[0]], axis=1),
        mlp_w2=jnp.concatenate([ex_post[1], ex_scan[1]], axis=1),
        ssm_w_in=ex_mlp[0], ssm_w_out=ex_scan[2],
        ssm_conv_w=ex_post[2], ssm_conv_b=ex_post[3], ssm_norm_w=ex_post[4],
        fox_w_in=ex_post[5], fox_w_o=ex_post[6],
        ada_w=g_ada_w[None], ada_b=dmod_all,
        ln_mix_g=unpack(1, (DEPTH, d)), ln_mix_b=unpack(2, (DEPTH, d)),
        ln_mlp_g=unpack(3, (DEPTH, d)), ln_mlp_b=unpack(4, (DEPTH, d)),
        fox_b_f=unpack(5, (1, FOX_HEADS)), ssm_dt_bias=unpack(6, (1, SSM_HEADS)),
        ssm_a_log=unpack(7, (1, SSM_HEADS)), ssm_d=unpack(8, (1, SSM_HEADS)),
    )
    weights = dict(ada_w=ada_w, ada_b=ada_b, ln_mix_g=ln_mix_g, ln_mix_b=ln_mix_b, ln_mlp_g=ln_mlp_g, ln_mlp_b=ln_mlp_b,
                   mlp_w1=mlp_w1, mlp_w2=mlp_w2, fox_w_in=fox_w_in, fox_b_f=fox_b_f, fox_w_o=fox_w_o, ssm_w_in=ssm_w_in,
                   ssm_conv_w=ssm_conv_w, ssm_conv_b=ssm_conv_b, ssm_dt_bias=ssm_dt_bias, ssm_a_log=ssm_a_log,
                   ssm_d=ssm_d, ssm_norm_w=ssm_norm_w, ssm_w_out=ssm_w_out)
    mom1 = dict(ada_w=m_ada_w, ada_b=m_ada_b, ln_mix_g=m_ln_mix_g, ln_mix_b=m_ln_mix_b, ln_mlp_g=m_ln_mlp_g,
                ln_mlp_b=m_ln_mlp_b, mlp_w1=m_mlp_w1, mlp_w2=m_mlp_w2, fox_w_in=m_fox_w_in, fox_b_f=m_fox_b_f,
                fox_w_o=m_fox_w_o, ssm_w_in=m_ssm_w_in, ssm_conv_w=m_ssm_conv_w, ssm_conv_b=m_ssm_conv_b,
                ssm_dt_bias=m_ssm_dt_bias, ssm_a_log=m_ssm_a_log, ssm_d=m_ssm_d, ssm_norm_w=m_ssm_norm_w,
                ssm_w_out=m_ssm_w_out)
    mom2 = dict(ada_w=v_ada_w, ada_b=v_ada_b, ln_mix_g=v_ln_mix_g, ln_mix_b=v_ln_mix_b, ln_mlp_g=v_ln_mlp_g,
                ln_mlp_b=v_ln_mlp_b, mlp_w1=v_mlp_w1, mlp_w2=v_mlp_w2, fox_w_in=v_fox_w_in, fox_b_f=v_fox_b_f,
                fox_w_o=v_fox_w_o, ssm_w_in=v_ssm_w_in, ssm_conv_w=v_ssm_conv_w, ssm_conv_b=v_ssm_conv_b,
                ssm_dt_bias=v_ssm_dt_bias, ssm_a_log=v_ssm_a_log, ssm_d=v_ssm_d, ssm_norm_w=v_ssm_norm_w,
                ssm_w_out=v_ssm_w_out)
    names = list(weights)
    stepped = {n: _adamw_any(shards[n], weights[n], mom1[n], mom2[n], f"adamw_{n}") for n in names}
    return (loss, grad_x[None], *[stepped[n][0] for n in names], *[stepped[n][1] for n in names],
            *[stepped[n][2] for n in names], *[stepped[n][3] for n in names])
```

```python
import numpy as np
import jax
import jax.numpy as jnp
from jax import lax
from jax.experimental import pallas as pl
from jax.experimental.pallas import tpu as pltpu

F32 = jnp.float32
MXU_DTYPE = jnp.bfloat16

N_DEV = 8
D_MODEL = 1024
DEPTH = 2
FOX_HEADS = 16
FOX_HEAD_DIM = 64
D_FF = 4096
SSM_D_INNER = 2048
SSM_HEADS = 32
SSM_GROUPS = 8
SSM_STATE = 128
SSM_CHUNK = 128
SSM_CONV = 4
SSM_CONV_DIM = 4096
GROUP_W = SSM_D_INNER // SSM_GROUPS
LN_EPS = 1e-5
RMS_EPS = 1e-5
ALPHA = (2.0 * DEPTH) ** 0.25
LANES = 128
SUBLANES = 8

ADAM_LR = 0.001
ADAM_B1 = 0.9
ADAM_B2 = 0.999
ADAM_EPS = 1e-08
ADAM_WD = 0.01
ADAM_STEP = 10

NN = (((1,), (0,)), ((), ()))
NT = (((1,), (1,)), ((), ()))
TN = (((0,), (0,)), ((), ()))

VMEM_BIG = 56 * 1024 * 1024
MM_ROWS = 2048
MM_DEPTH = 2048


def _dot(a, b, dims=NN, precision=None):
    return lax.dot_general(a, b, dims, precision=precision, preferred_element_type=F32)


def _mx(v):
    return v.astype(MXU_DTYPE)


def _pieces3(v):
    hi = _mx(v)
    r1 = v - hi.astype(F32)
    mid = _mx(r1)
    return hi, mid, _mx(r1 - mid.astype(F32))


def _dot_onehot(a, b, dims=NN, onehot="b"):
    if onehot == "b":
        return sum(_dot(p, _mx(b), dims) for p in _pieces3(a))
    return sum(_dot(_mx(a), p, dims) for p in _pieces3(b))


def _params(vmem=None):
    return pltpu.CompilerParams(vmem_limit_bytes=vmem) if vmem else None


def _all_gather(x, name):
    def body(x_ref, out_ref, send_sems, recv_sems, local_sem):
        xi, yi, ci = lax.axis_index("x"), lax.axis_index("y"), lax.axis_index("c")
        me, sibling = (xi, yi, ci), (xi, yi, 1 - ci)
        chips = [(1 - xi, yi), (xi, 1 - yi), (1 - xi, 1 - yi)]

        def slot(px, py, pc):
            return out_ref.at[4 * px + 2 * py + pc]

        def copy(k, block, to, src=None):
            return pltpu.make_async_remote_copy(
                src_ref=slot(*block) if src is None else src, dst_ref=slot(*block),
                send_sem=send_sems.at[k], recv_sem=recv_sems.at[k],
                device_id=to, device_id_type=pl.DeviceIdType.MESH)

        mine = pltpu.make_async_copy(x_ref, slot(*me), local_sem)
        mine.start()
        first = [copy(0, me, sibling, src=x_ref)]
        first += [copy(1 + j, me, (*chip, ci), src=x_ref) for j, chip in enumerate(chips)]
        for cp in first:
            cp.start()
        passed = [copy(4 + j, (*chip, ci), sibling) for j, chip in enumerate(chips)]
        for j, chip in enumerate(chips):
            copy(1 + j, (*chip, ci), me).wait_recv()
            passed[j].start()
        copy(0, sibling, me).wait_recv()
        for j, chip in enumerate(chips):
            copy(4 + j, (*chip, 1 - ci), me).wait_recv()
        for cp in first + passed:
            cp.wait_send()
        mine.wait()

    return pl.pallas_call(
        body, name=name,
        out_shape=jax.ShapeDtypeStruct((N_DEV,) + x.shape, x.dtype),
        in_specs=[pl.BlockSpec(memory_space=pl.ANY)],
        out_specs=pl.BlockSpec(memory_space=pl.ANY),
        scratch_shapes=[pltpu.SemaphoreType.DMA((7,)), pltpu.SemaphoreType.DMA((7,)),
                        pltpu.SemaphoreType.DMA],
    )(x)


def _direct_copies(scatter, x_ref, out_ref, send_sems, recv_sems, local_sems, n):
    xi, yi, ci = lax.axis_index("x"), lax.axis_index("y"), lax.axis_index("c")
    me = 4 * xi + 2 * yi + ci
    local = pltpu.make_async_copy(x_ref.at[me] if scatter else x_ref, out_ref.at[me], local_sems.at[n])
    remote = []
    for k in range(1, N_DEV):
        px = 1 - xi if k & 4 else xi
        py = 1 - yi if k & 2 else yi
        pc = 1 - ci if k & 1 else ci
        remote.append(pltpu.make_async_remote_copy(
            src_ref=x_ref.at[4 * px + 2 * py + pc] if scatter else x_ref, dst_ref=out_ref.at[me],
            send_sem=send_sems.at[7 * n + k - 1], recv_sem=recv_sems.at[7 * n + k - 1],
            device_id=(px, py, pc), device_id_type=pl.DeviceIdType.MESH))
    return local, remote


def _exchange_shapes(carry):
    return [jax.ShapeDtypeStruct(a.shape if scatter else (N_DEV,) + a.shape, a.dtype) for scatter, a in carry]


def _exchange_sems(carry):
    n = len(carry)
    return [pltpu.SemaphoreType.DMA((7 * n,)), pltpu.SemaphoreType.DMA((7 * n,)), pltpu.SemaphoreType.DMA((n,))]


def _carrying(body, n_in, n_out, n_scratch, carry, grid):
    nc = len(carry)

    def copies(refs):
        srcs = refs[n_in:n_in + nc]
        dsts = refs[n_in + nc + n_out:n_in + 2 * nc + n_out]
        sems = refs[n_in + 2 * nc + n_out + n_scratch:]
        return [_direct_copies(scatter, srcs[n], dsts[n], *sems, n) for n, (scatter, _) in enumerate(carry)]

    def wrapped(*refs):
        step = 0
        for axis, extent in enumerate(grid):
            step = step * extent + pl.program_id(axis)

        @pl.when(step == 0)
        def _():
            for local, remote in copies(refs):
                local.start()
                for cp in remote:
                    cp.start()

        body(*refs[:n_in], *refs[n_in + nc:n_in + nc + n_out],
             *refs[n_in + 2 * nc + n_out:n_in + 2 * nc + n_out + n_scratch])

        @pl.when(step == int(np.prod(grid)) - 1)
        def _():
            for local, remote in copies(refs):
                for cp in remote:
                    cp.wait()
                local.wait()

    return wrapped if nc else body


def _mm(name, a, b, *, grid, a_spec, b_spec, dims, k_axis, outs, acc=None, extras=(), epi=None, vmem=None,
        carry=()):
    nk = grid[k_axis]
    n_ex, n_out = len(extras), len(outs)
    carry = list(carry)
    anywhere = pl.BlockSpec(memory_space=pl.ANY)

    def body(*refs):
        a_ref, b_ref = refs[0], refs[1]
        ex = refs[2:2 + n_ex]
        out = refs[2 + n_ex:2 + n_ex + n_out]

        def finish(val):
            if epi is None:
                out[0][...] = val.astype(out[0].dtype)
            else:
                epi(val, ex, out)

        part = _dot(a_ref[...], b_ref[...], dims)
        if nk == 1:
            finish(part)
        else:
            acc_ref = refs[2 + n_ex + n_out]
            k = pl.program_id(k_axis)

            @pl.when(k == 0)
            def _():
                acc_ref[...] = part

            @pl.when(k > 0)
            def _():
                acc_ref[...] += part

            @pl.when(k == nk - 1)
            def _():
                finish(acc_ref[...])

    res = pl.pallas_call(
        _carrying(body, 2 + n_ex, n_out, 1 if nk > 1 else 0, carry, grid), name=name, grid=grid,
        in_specs=[a_spec, b_spec] + [s for _, s in extras] + [anywhere] * len(carry),
        out_specs=[s for _, s in outs] + [anywhere] * len(carry),
        out_shape=[o for o, _ in outs] + _exchange_shapes(carry),
        scratch_shapes=([pltpu.VMEM(acc, F32)] if nk > 1 else []) + (_exchange_sems(carry) if carry else []),
        compiler_params=_params(vmem),
    )(a, b, *[e for e, _ in extras], *[arr for _, arr in carry])
    return list(res)


def _tile(n, t):
    t = min(n, t)
    assert n % t == 0, (n, t)
    return t


def _mm_nn(name, a, b, out_dtype, *, tm=MM_ROWS, tn=1024, tk=1024, epi=None, extras=(), outs=None, carry=()):
    m, kk = a.shape
    n = b.shape[1]
    tm, tn, tk = _tile(m, tm), _tile(n, tn), _tile(kk, tk)
    if outs is None:
        outs = [(jax.ShapeDtypeStruct((m, n), out_dtype), pl.BlockSpec((tm, tn), lambda i, j, k: (i, j)))]
    res = _mm(name, a, b, grid=(m // tm, n // tn, kk // tk),
              a_spec=pl.BlockSpec((tm, tk), lambda i, j, k: (i, k)),
              b_spec=pl.BlockSpec((tk, tn), lambda i, j, k: (k, j)),
              dims=NN, k_axis=2, acc=(tm, tn), outs=outs, extras=list(extras), epi=epi, vmem=VMEM_BIG, carry=carry)
    return res[0] if len(res) == 1 else res


def _mm_sum(name, pairs, out_dtype=F32, *, tm=1024, tn=1024, carry=()):
    m, n = pairs[0][0].shape[0], pairs[0][1].shape[1]
    tm, tn = _tile(m, tm), _tile(n, tn)
    carry = list(carry)
    grid = (m // tm, n // tn)

    def body(*refs):
        out = refs[2 * len(pairs)]
        acc = _dot(refs[0][...], refs[1][...])
        for p in range(1, len(pairs)):
            acc = acc + _dot(refs[2 * p][...], refs[2 * p + 1][...])
        out[...] = acc.astype(out.dtype)

    in_specs = []
    for a, b in pairs:
        in_specs += [pl.BlockSpec((tm, a.shape[1]), lambda i, j: (i, 0)),
                     pl.BlockSpec((b.shape[0], tn), lambda i, j: (0, j))]
    anywhere = pl.BlockSpec(memory_space=pl.ANY)
    res = pl.pallas_call(
        _carrying(body, 2 * len(pairs), 1, 0, carry, grid), name=name, grid=grid,
        in_specs=in_specs + [anywhere] * len(carry),
        out_specs=[pl.BlockSpec((tm, tn), lambda i, j: (i, j))] + [anywhere] * len(carry),
        out_shape=[jax.ShapeDtypeStruct((m, n), out_dtype)] + _exchange_shapes(carry),
        scratch_shapes=_exchange_sems(carry) if carry else [],
        compiler_params=_params(VMEM_BIG),
    )(*[x for pair in pairs for x in pair], *[arr for _, arr in carry])
    return list(res) if carry else res[0]


def _mm_tn(name, a, b, out_dtype=F32, *, tm=1024, tn=1024, tk=MM_DEPTH):
    kk, m = a.shape
    n = b.shape[1]
    tm, tn, tk = _tile(m, tm), _tile(n, tn), _tile(kk, tk)
    res = _mm(name, a, b, grid=(m // tm, n // tn, kk // tk),
              a_spec=pl.BlockSpec((tk, tm), lambda i, j, k: (k, i)),
              b_spec=pl.BlockSpec((tk, tn), lambda i, j, k: (k, j)),
              dims=TN, k_axis=2, acc=(tm, tn),
              outs=[(jax.ShapeDtypeStruct((m, n), out_dtype), pl.BlockSpec((tm, tn), lambda i, j, k: (i, j)))],
              vmem=VMEM_BIG)
    return res[0]


def _row_block(s):
    return _tile(s, 512)


def _modulate(x, pv, name):
    s, d = x.shape
    tb = _row_block(s)

    def body(x_ref, pv_ref, u_ref):
        u_ref[...] = _mx(x_ref[...] * pv_ref[0:1, :] + pv_ref[1:2, :])

    return pl.pallas_call(
        body, name=name, grid=(s // tb,),
        in_specs=[pl.BlockSpec((tb, d), lambda i: (i, 0)), pl.BlockSpec((8, d), lambda i: (0, 0))],
        out_specs=pl.BlockSpec((tb, d), lambda i: (i, 0)),
        out_shape=jax.ShapeDtypeStruct((s, d), MXU_DTYPE),
    )(x, pv)


def _ln_stats(r):
    mu = jnp.mean(r, axis=-1, keepdims=True)
    xc = r - mu
    var = jnp.mean(xc * xc, axis=-1, keepdims=True)
    rstd = lax.rsqrt(var + LN_EPS)
    return xc * rstd, rstd


def _ln_fwd(xin, y, pv, name):
    s, d = xin.shape
    tb = _row_block(s)

    def body(x_ref, y_ref, pv_ref, xo_ref, u_ref):
        r = ALPHA * x_ref[...] + pv_ref[0:1, :] * y_ref[...]
        xhat, _ = _ln_stats(r)
        xo = xhat * pv_ref[1:2, :] + pv_ref[2:3, :]
        xo_ref[...] = xo
        u_ref[...] = _mx(xo * pv_ref[3:4, :] + pv_ref[4:5, :])

    row = pl.BlockSpec((tb, d), lambda i: (i, 0))
    return pl.pallas_call(
        body, name=name, grid=(s // tb,),
        in_specs=[row, row, pl.BlockSpec((8, d), lambda i: (0, 0))],
        out_specs=[row, row],
        out_shape=[jax.ShapeDtypeStruct((s, d), F32), jax.ShapeDtypeStruct((s, d), MXU_DTYPE)],
    )(xin, y, pv)


def _ln_bwd(xin, y, pv, name, *, dxo=None, du=None, target=None):
    s, d = xin.shape
    tb = _row_block(s)
    nb = s // tb
    loss_mode = target is not None

    def body(*refs):
        if loss_mode:
            x_ref, y_ref, pv_ref, t_ref, dxin_ref, dy_ref, sums_ref = refs
        else:
            x_ref, y_ref, pv_ref, dxo_ref, du_ref, dxin_ref, dy_ref, sums_ref = refs
        i = pl.program_id(0)

        @pl.when(i == 0)
        def _():
            sums_ref[...] = jnp.zeros_like(sums_ref)

        yv = y_ref[...]
        r = ALPHA * x_ref[...] + pv_ref[0:1, :] * yv
        xhat, rstd = _ln_stats(r)
        xo = xhat * pv_ref[1:2, :] + pv_ref[2:3, :]
        if loss_mode:
            diff = xo - t_ref[...]
            dxo_v = diff * (1.0 / d)
            sums_ref[5:6, :] += jnp.sum(diff * diff, axis=0, keepdims=True) * (0.5 / d)
        else:
            duv = du_ref[...]
            dxo_v = dxo_ref[...] + duv * pv_ref[3:4, :]
            sums_ref[0:1, :] += jnp.sum(duv * xo, axis=0, keepdims=True)
            sums_ref[1:2, :] += jnp.sum(duv, axis=0, keepdims=True)
        sums_ref[2:3, :] += jnp.sum(dxo_v * xhat, axis=0, keepdims=True)
        sums_ref[3:4, :] += jnp.sum(dxo_v, axis=0, keepdims=True)
        dxh = dxo_v * pv_ref[1:2, :]
        dr = rstd * (dxh - jnp.mean(dxh, axis=-1, keepdims=True)
                     - xhat * jnp.mean(dxh * xhat, axis=-1, keepdims=True))
        sums_ref[4:5, :] += jnp.sum(dr * yv, axis=0, keepdims=True)
        dxin_ref[...] = ALPHA * dr
        dy_ref[...] = _mx(pv_ref[0:1, :] * dr)
        if loss_mode:
            @pl.when(i == nb - 1)
            def _():
                sums_ref[5:6, :] = jnp.broadcast_to(jnp.sum(sums_ref[5:6, :], axis=-1, keepdims=True), (1, d))

    row = pl.BlockSpec((tb, d), lambda i: (i, 0))
    par = pl.BlockSpec((8, d), lambda i: (0, 0))
    ins = [xin, y, pv] + ([target] if loss_mode else [dxo, du])
    return pl.pallas_call(
        body, name=name, grid=(nb,),
        in_specs=[row, row, par] + [row] * (len(ins) - 3),
        out_specs=[row, row, par],
        out_shape=[jax.ShapeDtypeStruct((s, d), F32), jax.ShapeDtypeStruct((s, d), MXU_DTYPE),
                   jax.ShapeDtypeStruct((8, d), F32)],
    )(*ins)


def _mod_bwd(dx_direct, du, x, pv, name):
    s, d = x.shape
    tb = _row_block(s)

    def body(dxd_ref, du_ref, x_ref, pv_ref, dx_ref, sums_ref):
        @pl.when(pl.program_id(0) == 0)
        def _():
            sums_ref[...] = jnp.zeros_like(sums_ref)

        duv = du_ref[...]
        dx_ref[...] = dxd_ref[...] + duv * pv_ref[0:1, :]
        sums_ref[0:1, :] += jnp.sum(duv * x_ref[...], axis=0, keepdims=True)
        sums_ref[1:2, :] += jnp.sum(duv, axis=0, keepdims=True)

    row = pl.BlockSpec((tb, d), lambda i: (i, 0))
    par = pl.BlockSpec((8, d), lambda i: (0, 0))
    return pl.pallas_call(
        body, name=name, grid=(s // tb,),
        in_specs=[row, row, row, par], out_specs=[row, par],
        out_shape=[jax.ShapeDtypeStruct((s, d), F32), jax.ShapeDtypeStruct((8, d), F32)],
    )(dx_direct, du, x, pv)


def _mlp_fwd(u, w1, w2, tag):
    s = u.shape[0]

    def epi(val, ex, out):
        out[0][...] = _mx(val)
        out[1][...] = _mx(jnp.square(jnp.maximum(val, 0.0)))

    tm, tn = _tile(s, MM_ROWS), 1024
    spec = pl.BlockSpec((tm, tn), lambda i, j, k: (i, j))
    shp = jax.ShapeDtypeStruct((s, D_FF), MXU_DTYPE)
    h, a = _mm_nn(f"mlp_up{tag}", u, w1, None, epi=epi, outs=[(shp, spec), (shp, spec)], tn=tn)
    y = _mm_nn(f"mlp_down{tag}", a, w2, F32)
    return y, (h, a)


def _mlp_bwd(dy, u, h, a, w1t, w2t, tag, carry=()):
    s = u.shape[0]
    tm, tn = _tile(s, MM_ROWS), 1024
    spec = pl.BlockSpec((tm, tn), lambda i, j, k: (i, j))

    def epi(val, ex, out):
        out[0][...] = _mx(val * (2.0 * jnp.maximum(ex[0][...].astype(F32), 0.0)))

    got = _mm_nn(f"mlp_dh{tag}", dy, w2t, None, epi=epi, extras=[(h, spec)],
                 outs=[(jax.ShapeDtypeStruct((s, D_FF), MXU_DTYPE), spec)], tn=tn, carry=carry)
    dh, carried = (got[0], list(got[1:])) if carry else (got, [])
    du = _mm_nn(f"mlp_du{tag}", dh, w1t, F32)
    dw2 = _mm_tn(f"mlp_dw2{tag}", a, dy)
    dw1 = _mm_tn(f"mlp_dw1{tag}", u, dh)
    return du, dw1, dw2, carried


FOX_T = 1024
BIAS_Q = (64, 65, 66)
BIAS_K = (67, 68, 69)
SKIP_MARGIN = 110.0
ONES_V = 64

def _fox_constants():
    selq = np.zeros((FOX_HEADS, 512, LANES), np.float32)
    selk = np.zeros((FOX_HEADS, 512, LANES), np.float32)
    selv = np.zeros((2, LANES, LANES), np.float32)
    put = np.zeros((2, 2, LANES, LANES), np.float32)
    for h in range(FOX_HEADS):
        off = FOX_HEAD_DIM * (h % 2)
        for dd in range(FOX_HEAD_DIM):
            selq[h, off + dd, dd] = FOX_HEAD_DIM ** -0.5
            selk[h, off + dd, dd] = 1.0
        for piece in range(3):
            selq[h, LANES * (1 + piece) + h, BIAS_Q[piece]] = 1.0
            selk[h, LANES * (1 + piece) + h, BIAS_K[piece]] = -1.0
    for par in range(2):
        for dd in range(FOX_HEAD_DIM):
            selv[par, FOX_HEAD_DIM * par + dd, dd] = 1.0
            put[par, 0, dd, FOX_HEAD_DIM * par + dd] = FOX_HEAD_DIM ** -0.5
            put[par, 1, dd, FOX_HEAD_DIM * par + dd] = 1.0
    return selq, selk, selv, put


def _fox_prep(qkv, f, bf, carry=()):
    s = qkv.shape[0]
    t = _tile(s, FOX_T)
    nb = s // t
    selq, selk, selv, _ = _fox_constants()
    carry = list(carry)
    anywhere = pl.BlockSpec(memory_space=pl.ANY)

    def body(q_ref, k_ref, v_ref, f_ref, bf_ref, selq_ref, selk_ref, selv_ref,
             qa_ref, qat_ref, ka_ref, kat_ref, va_ref, vat_ref, stats_ref, parts_ref, carry_ref, cum_ref):
        i, h = pl.program_id(0), pl.program_id(1)
        lane = lax.broadcasted_iota(jnp.int32, (1, LANES), 1)

        @pl.when(h == 0)
        def _():
            @pl.when(i == 0)
            def _():
                carry_ref[...] = jnp.zeros_like(carry_ref)

            lf = jnp.where(lane < FOX_HEADS, jax.nn.log_sigmoid(f_ref[...] + bf_ref[0:1, :]), 0.0)
            tri = (lax.broadcasted_iota(jnp.int32, (t, t), 0) >= lax.broadcasted_iota(jnp.int32, (t, t), 1)).astype(F32)
            cum = _dot_onehot(tri, lf, onehot="a") + carry_ref[0:1, :]
            carry_ref[0:1, :] = cum[t - 1:t, :]
            cum_ref[...] = cum
            hi = _mx(cum)
            r1 = cum - hi.astype(F32)
            mid = _mx(r1)
            parts_ref[:, 0:LANES] = hi
            parts_ref[:, LANES:2 * LANES] = mid
            parts_ref[:, 2 * LANES:3 * LANES] = _mx(r1 - mid.astype(F32))

        parts = parts_ref[...]
        qa = _dot(jnp.concatenate([q_ref[...], parts], axis=1), selq_ref[...])
        qa = qa + jnp.where((lane >= BIAS_K[0]) & (lane <= BIAS_K[2]), 1.0, 0.0)
        ka = _dot(jnp.concatenate([k_ref[...], parts], axis=1), selk_ref[...])
        ka = ka + jnp.where((lane >= BIAS_Q[0]) & (lane <= BIAS_Q[2]), 1.0, 0.0)
        va = _dot(v_ref[...], selv_ref[...]) + jnp.where(lane == ONES_V, 1.0, 0.0)
        qa_ref[...] = _mx(qa)
        qat_ref[...] = _mx(qa.T)
        ka_ref[...] = _mx(ka)
        kat_ref[...] = _mx(ka.T)
        va_ref[...] = _mx(va)
        vat_ref[...] = _mx(va.T)

        def longest(rows_):
            sq = jnp.where(lane < FOX_HEAD_DIM, rows_ * rows_, 0.0)
            return jnp.sqrt(jnp.max(jnp.sum(sq, axis=1, keepdims=True), axis=0, keepdims=True))

        mine = lane == h
        cum = cum_ref[...]
        top = jnp.max(jnp.max(jnp.where(mine, cum, -jnp.inf), axis=1, keepdims=True), axis=0, keepdims=True)
        low = jnp.min(jnp.min(jnp.where(mine, cum, jnp.inf), axis=1, keepdims=True), axis=0, keepdims=True)
        row = lax.broadcasted_iota(jnp.int32, (8, LANES), 0)
        stats_ref[...] = jnp.where(row == 0, longest(qa), jnp.where(row == 1, longest(ka),
                                                                    jnp.where(row == 2, top, low)))

    rows = jax.ShapeDtypeStruct((FOX_HEADS, nb, t, LANES), MXU_DTYPE)
    cols = jax.ShapeDtypeStruct((FOX_HEADS, nb, LANES, t), MXU_DTYPE)
    rspec = pl.BlockSpec((None, None, t, LANES), lambda i, h: (h, i, 0, 0))
    cspec = pl.BlockSpec((None, None, LANES, t), lambda i, h: (h, i, 0, 0))
    npair = FOX_HEADS // 2
    res = pl.pallas_call(
        _carrying(body, 8, 7, 3, carry, (nb, FOX_HEADS)), name="fox_prep", grid=(nb, FOX_HEADS),
        in_specs=[pl.BlockSpec((t, LANES), lambda i, h: (i, h // 2)),
                  pl.BlockSpec((t, LANES), lambda i, h: (i, npair + h // 2)),
                  pl.BlockSpec((t, LANES), lambda i, h: (i, 2 * npair + h // 2)),
                  pl.BlockSpec((t, LANES), lambda i, h: (i, 0)),
                  pl.BlockSpec((8, LANES), lambda i, h: (0, 0)),
                  pl.BlockSpec((None, 512, LANES), lambda i, h: (h, 0, 0)),
                  pl.BlockSpec((None, 512, LANES), lambda i, h: (h, 0, 0)),
                  pl.BlockSpec((None, LANES, LANES), lambda i, h: (h % 2, 0, 0))] + [anywhere] * len(carry),
        out_specs=[rspec, cspec, rspec, cspec, rspec, cspec,
                   pl.BlockSpec((None, None, 8, LANES), lambda i, h: (h, i, 0, 0))] + [anywhere] * len(carry),
        out_shape=[rows, cols, rows, cols, rows, cols, jax.ShapeDtypeStruct((FOX_HEADS, nb, 8, LANES), F32)]
        + _exchange_shapes(carry),
        scratch_shapes=[pltpu.VMEM((t, 3 * LANES), MXU_DTYPE), pltpu.VMEM((8, LANES), F32),
                        pltpu.VMEM((t, LANES), F32)] + (_exchange_sems(carry) if carry else []),
        compiler_params=_params(VMEM_BIG),
    )(qkv, qkv, qkv, f, bf, _mx(jnp.asarray(selq)), _mx(jnp.asarray(selk)), _mx(jnp.asarray(selv)),
      *[arr for _, arr in carry])
    return (*res[:7], list(res[7:]))


def _fox_active(stats):
    qn, kn, top, low = (stats[:, :, r, 0] for r in range(4))
    nb = qn.shape[1]
    gap = qn[:, :, None] * kn[:, None, :] + top[:, :, None] - low[:, None, :] + (qn * kn)[:, :, None]
    keep = (gap > -SKIP_MARGIN) | jnp.eye(nb, dtype=bool)[None]
    return jnp.where(keep, 1.0, 0.0).astype(F32).reshape(qn.shape[0], nb * nb)


def _causal_allow(t):
    return lax.broadcasted_iota(jnp.int32, (t, t), 0) <= lax.broadcasted_iota(jnp.int32, (t, t), 1)


def _fox_attn_fwd(qat, ka, vat, active):
    heads, nb, _, t = qat.shape

    def body(act_ref, qat_ref, ka_ref, vat_ref, ot_ref, lse_ref, acc_ref, m_ref, kbuf_ref, vbuf_ref, fsems):
        h, i = pl.program_id(0), pl.program_id(1)
        m_ref[...] = jnp.full_like(m_ref, -jnp.inf)
        acc_ref[...] = jnp.zeros_like(acc_ref)

        def key_blocks(j, slot):
            return [pltpu.make_async_copy(ka_ref.at[h, j], kbuf_ref.at[slot], fsems.at[0, slot]),
                    pltpu.make_async_copy(vat_ref.at[h, j], vbuf_ref.at[slot], fsems.at[1, slot])]

        def runs(j):
            return (j == i) | (act_ref[h, i * nb + j] > 0.5)

        def step(slot, diagonal):
            st = _dot(kbuf_ref[slot], qat_ref[...])
            if diagonal:
                st = jnp.where(_causal_allow(t), st, -jnp.inf)
            m_old = m_ref[...]
            m_new = jnp.maximum(m_old, jnp.max(st, axis=0, keepdims=True))
            pt = jnp.exp(st - m_new)
            acc_ref[...] = acc_ref[...] * jnp.exp(m_old - m_new) + _dot(vbuf_ref[slot], _mx(pt))
            m_ref[...] = m_new

        @pl.when(runs(0))
        def _():
            for cp in key_blocks(0, 0):
                cp.start()

        def earlier(j, c):
            slot = j % 2

            @pl.when(runs(j + 1))
            def _():
                for cp in key_blocks(j + 1, 1 - slot):
                    cp.start()

            @pl.when(runs(j))
            def _():
                for cp in key_blocks(j, slot):
                    cp.wait()
                step(slot, False)

            return c

        lax.fori_loop(0, i, earlier, 0)
        for cp in key_blocks(i, i % 2):
            cp.wait()
        step(i % 2, True)
        acc = acc_ref[...]
        denom = acc[ONES_V:ONES_V + 1, :]
        ot_ref[...] = _mx(acc / denom)
        lse_ref[...] = m_ref[...] + jnp.log(denom)

    anywhere = pl.BlockSpec(memory_space=pl.ANY)
    qspec = pl.BlockSpec((None, None, LANES, t), lambda h, i: (h, i, 0, 0))
    return pl.pallas_call(
        body, name="fox_attn_fwd", grid=(heads, nb),
        in_specs=[pl.BlockSpec(memory_space=pltpu.SMEM), qspec, anywhere, anywhere],
        out_specs=[qspec, pl.BlockSpec((None, None, 1, t), lambda h, i: (h, i, 0, 0))],
        out_shape=[jax.ShapeDtypeStruct((heads, nb, LANES, t), MXU_DTYPE),
                   jax.ShapeDtypeStruct((heads, nb, 1, t), F32)],
        scratch_shapes=[pltpu.VMEM((LANES, t), F32), pltpu.VMEM((1, t), F32), pltpu.VMEM((2, t, LANES), MXU_DTYPE),
                        pltpu.VMEM((2, LANES, t), MXU_DTYPE), pltpu.SemaphoreType.DMA((2, 2))],
        compiler_params=_params(VMEM_BIG),
    )(active, qat, ka, vat)


def _fox_attn_bwd(qa, qat, ka, kat, va, ot, lse, do, dot_, active):
    heads, nb, t, _ = qa.shape

    def body(act_ref, qa_ref, qat_ref, ka_ref, kat_ref, va_ref, ot_ref, lse_ref, do_ref, dot_ref,
             dqt_ref, dka_ref, dva_ref, rows_ref, cols_ref, lseb_ref, fsems):
        h, j = pl.program_id(0), pl.program_id(1)

        @pl.when(j == 0)
        def _():
            dqt_ref[...] = jnp.zeros_like(dqt_ref)

        def query_blocks(i, slot):
            cps = [pltpu.make_async_copy(src.at[h, i], rows_ref.at[slot, n], fsems.at[n, slot])
                   for n, src in enumerate((qa_ref, do_ref))]
            cps += [pltpu.make_async_copy(src.at[h, i], cols_ref.at[slot, n], fsems.at[2 + n, slot])
                    for n, src in enumerate((qat_ref, ot_ref, dot_ref))]
            return cps + [pltpu.make_async_copy(lse_ref.at[h, i], lseb_ref.at[slot], fsems.at[5, slot])]

        def runs(i):
            return (i == j) | (act_ref[h, i * nb + j] > 0.5)

        def step(i, slot, diagonal):
            st = _dot(ka_ref[...], cols_ref[slot, 0])
            dot_v = cols_ref[slot, 2]
            delta = jnp.sum(cols_ref[slot, 1].astype(F32) * dot_v.astype(F32), axis=0, keepdims=True)
            pt = jnp.exp(st - lseb_ref[slot])
            if diagonal:
                pt = jnp.where(_causal_allow(t), pt, 0.0)
            dsm = _mx(pt * (_dot(va_ref[...], dot_v) - delta))
            upd_v = _dot(_mx(pt), rows_ref[slot, 1])
            upd_k = _dot(dsm, rows_ref[slot, 0])
            if diagonal:
                dva_ref[...] = upd_v
                dka_ref[...] = upd_k
            else:
                dva_ref[...] += upd_v
                dka_ref[...] += upd_k
            dqt_ref[i] += _dot(kat_ref[...], dsm)

        def visit(i, slot, diagonal):
            nxt = jnp.minimum(i + 1, nb - 1)

            @pl.when((i + 1 < nb) & runs(nxt))
            def _():
                for cp in query_blocks(nxt, 1 - slot):
                    cp.start()

            @pl.when(runs(i))
            def _():
                for cp in query_blocks(i, slot):
                    cp.wait()
                step(i, slot, diagonal)

        for cp in query_blocks(j, 0):
            cp.start()
        visit(j, 0, True)

        def later(i, c):
            visit(i, (i - j) % 2, False)
            return c

        lax.fori_loop(j + 1, nb, later, 0)

    def at_k(shape):
        return pl.BlockSpec((None, None) + shape, lambda h, j: (h, j, 0, 0))

    anywhere = pl.BlockSpec(memory_space=pl.ANY)
    return pl.pallas_call(
        body, name="fox_attn_bwd", grid=(heads, nb),
        in_specs=[pl.BlockSpec(memory_space=pltpu.SMEM),
                  anywhere, anywhere, at_k((t, LANES)), at_k((LANES, t)), at_k((t, LANES)),
                  anywhere, anywhere, anywhere, anywhere],
        out_specs=[pl.BlockSpec((None, nb, LANES, t), lambda h, j: (h, 0, 0, 0)), at_k((t, LANES)), at_k((t, LANES))],
        out_shape=[jax.ShapeDtypeStruct((heads, nb, LANES, t), F32),
                   jax.ShapeDtypeStruct((heads, nb, t, LANES), F32),
                   jax.ShapeDtypeStruct((heads, nb, t, LANES), F32)],
        scratch_shapes=[pltpu.VMEM((2, 2, t, LANES), MXU_DTYPE), pltpu.VMEM((2, 3, LANES, t), MXU_DTYPE),
                        pltpu.VMEM((2, 1, t), F32), pltpu.SemaphoreType.DMA((6, 2))],
        compiler_params=_params(VMEM_BIG),
    )(active, qa, qat, ka, kat, va, ot, lse, do, dot_)


def _fox_post(dqt, dka, dva, f, bf, carry=()):
    heads, nb, t, _ = dka.shape
    s = nb * t
    _, _, _, put = _fox_constants()
    carry = list(carry)
    anywhere = pl.BlockSpec(memory_space=pl.ANY)

    def body(dqt_ref, dka_ref, dva_ref, f_ref, bf_ref, put_ref, dq_ref, dk_ref, dv_ref, df_ref, sums_ref,
             dc_ref, carry_ref):
        i, h = pl.program_id(0), pl.program_id(1)

        @pl.when((i == 0) & (h == 0))
        def _():
            carry_ref[...] = jnp.zeros_like(carry_ref)
            sums_ref[...] = jnp.zeros_like(sums_ref)

        @pl.when(h == 0)
        def _():
            dc_ref[...] = jnp.zeros_like(dc_ref)

        dqt_v = dqt_ref[...]
        dka_v = dka_ref[...]
        term_q = _dot(_mx(dqt_v), put_ref[0], TN)
        term_k = _dot(_mx(dka_v), put_ref[1])
        term_v = _dot(_mx(dva_ref[...]), put_ref[1])

        @pl.when(h % 2 == 0)
        def _():
            dq_ref[...] = _mx(term_q)
            dk_ref[...] = _mx(term_k)
            dv_ref[...] = _mx(term_v)

        @pl.when(h % 2 == 1)
        def _():
            dq_ref[...] += _mx(term_q)
            dk_ref[...] += _mx(term_k)
            dv_ref[...] += _mx(term_v)

        dcum = dqt_v[BIAS_Q[0]:BIAS_Q[0] + 1, :] - dka_v.T[BIAS_K[0]:BIAS_K[0] + 1, :]
        head_row = lax.broadcasted_iota(jnp.int32, (heads, 1), 0) == h
        dc_ref[...] += jnp.where(head_row, dcum, 0.0)

        @pl.when(h == heads - 1)
        def _():
            later = (lax.broadcasted_iota(jnp.int32, (t, t), 0) >= lax.broadcasted_iota(jnp.int32, (t, t), 1)).astype(F32)
            dlf_t = _dot_onehot(dc_ref[...], later) + carry_ref[:, 0:1]
            carry_ref[...] = jnp.broadcast_to(dlf_t[:, 0:1], carry_ref.shape)
            dlf = jnp.concatenate([dlf_t, jnp.zeros((LANES - heads, t), F32)], axis=0).T
            lane = lax.broadcasted_iota(jnp.int32, (1, LANES), 1)
            df = jnp.where(lane < heads, dlf * jax.nn.sigmoid(-(f_ref[...] + bf_ref[0:1, :])), 0.0)
            df_ref[...] = _mx(df)
            sums_ref[0:1, :] += jnp.sum(df, axis=0, keepdims=True)

    rev = lambda i: nb - 1 - i
    pair_spec = pl.BlockSpec((t, LANES), lambda i, h: (rev(i), h // 2))
    blk = pl.BlockSpec((t, LANES), lambda i, h: (rev(i), 0))
    hd = jax.ShapeDtypeStruct((s, D_MODEL), MXU_DTYPE)
    res = pl.pallas_call(
        _carrying(body, 6, 5, 2, carry, (nb, heads)), name="fox_post", grid=(nb, heads),
        in_specs=[pl.BlockSpec((None, None, LANES, t), lambda i, h: (h, rev(i), 0, 0)),
                  pl.BlockSpec((None, None, t, LANES), lambda i, h: (h, rev(i), 0, 0)),
                  pl.BlockSpec((None, None, t, LANES), lambda i, h: (h, rev(i), 0, 0)),
                  blk, pl.BlockSpec((8, LANES), lambda i, h: (0, 0)),
                  pl.BlockSpec((None, 2, LANES, LANES), lambda i, h: (h % 2, 0, 0, 0))] + [anywhere] * len(carry),
        out_specs=[pair_spec, pair_spec, pair_spec, blk, pl.BlockSpec((8, LANES), lambda i, h: (0, 0))]
        + [anywhere] * len(carry),
        out_shape=[hd, hd, hd, jax.ShapeDtypeStruct((s, LANES), MXU_DTYPE), jax.ShapeDtypeStruct((8, LANES), F32)]
        + _exchange_shapes(carry),
        scratch_shapes=[pltpu.VMEM((heads, t), F32), pltpu.VMEM((heads, LANES), F32)]
        + (_exchange_sems(carry) if carry else []),
        compiler_params=_params(VMEM_BIG),
    )(dqt, dka, dva, f, bf, _mx(jnp.asarray(put)), *[arr for _, arr in carry])
    return (*res[:5], list(res[5:]))


def _fox_weights(w_in, w_o):
    wqkv = w_in[:, :3 * D_MODEL]
    wf = jnp.pad(w_in[:, 3 * D_MODEL:], ((0, 0), (0, LANES - FOX_HEADS)))
    wo_heads = w_o.reshape(FOX_HEADS, FOX_HEAD_DIM, D_MODEL)
    wo_a = jnp.pad(wo_heads, ((0, 0), (0, LANES - FOX_HEAD_DIM), (0, 0)))
    wo_rows = wo_a.reshape(FOX_HEADS * LANES, D_MODEL)
    return dict(wqkv=wqkv, wf=wf, wqkv_t=wqkv.T, wf_t=wf.T, wo_rows=wo_rows, wo_rows_t=wo_rows.T)


def _fox_out(ot, wo_rows):
    heads, nb, _, t = ot.shape

    def body(ot_ref, w_ref, y_ref):
        y_ref[...] = _dot(ot_ref[...].reshape(heads * LANES, t), w_ref[...], TN)

    return pl.pallas_call(
        body, name="fox_out", grid=(nb,),
        in_specs=[pl.BlockSpec((heads, None, LANES, t), lambda i: (0, i, 0, 0)),
                  pl.BlockSpec((heads * LANES, D_MODEL), lambda i: (0, 0))],
        out_specs=pl.BlockSpec((t, D_MODEL), lambda i: (i, 0)),
        out_shape=jax.ShapeDtypeStruct((nb * t, D_MODEL), F32),
        compiler_params=_params(VMEM_BIG),
    )(ot, wo_rows)


def _fox_do(dy, wo_rows_t, nb, t):
    heads = FOX_HEADS

    def body(dy_ref, w_ref, do_ref, dot_ref):
        val = _dot(dy_ref[...], w_ref[...])
        for h in range(heads):
            blk = val[:, h * LANES:(h + 1) * LANES]
            do_ref[h] = _mx(blk)
            dot_ref[h] = _mx(blk.T)

    return pl.pallas_call(
        body, name="fox_do", grid=(nb,),
        in_specs=[pl.BlockSpec((t, D_MODEL), lambda i: (i, 0)),
                  pl.BlockSpec((D_MODEL, heads * LANES), lambda i: (0, 0))],
        out_specs=[pl.BlockSpec((heads, None, t, LANES), lambda i: (0, i, 0, 0)),
                   pl.BlockSpec((heads, None, LANES, t), lambda i: (0, i, 0, 0))],
        out_shape=[jax.ShapeDtypeStruct((heads, nb, t, LANES), MXU_DTYPE),
                   jax.ShapeDtypeStruct((heads, nb, LANES, t), MXU_DTYPE)],
        compiler_params=_params(VMEM_BIG),
    )(dy, wo_rows_t)


def _fox_dwo(ot, dy):
    heads, nb, _, t = ot.shape

    def body(ot_ref, dy_ref, o_ref):
        part = _dot(ot_ref[...].reshape(heads * LANES, t), dy_ref[...])

        @pl.when(pl.program_id(0) == 0)
        def _():
            o_ref[...] = part

        @pl.when(pl.program_id(0) > 0)
        def _():
            o_ref[...] += part

    return pl.pallas_call(
        body, name="fox_dwo", grid=(nb,),
        in_specs=[pl.BlockSpec((heads, None, LANES, t), lambda i: (0, i, 0, 0)),
                  pl.BlockSpec((t, D_MODEL), lambda i: (i, 0))],
        out_specs=pl.BlockSpec((heads * LANES, D_MODEL), lambda i: (0, 0)),
        out_shape=jax.ShapeDtypeStruct((heads * LANES, D_MODEL), F32),
        compiler_params=_params(VMEM_BIG),
    )(ot, dy)


def _fox_fwd(u, w, bf, carry=()):
    qkv = _mm_nn("fox_qkv", u, w["wqkv"], MXU_DTYPE)
    f = _mm_nn("fox_f", u, w["wf"], F32)
    qa, qat, ka, kat, va, vat, stats, carried = _fox_prep(qkv, f, bf, carry)
    ot, lse = _fox_attn_fwd(qat, ka, vat, _fox_active(stats))
    y = _fox_out(ot, w["wo_rows"])
    return y, dict(f=f, qa=qa, qat=qat, ka=ka, kat=kat, va=va, ot=ot, lse=lse, stats=stats), carried


def _fox_bwd(dy, u, w, bf, res, carry=()):
    heads, nb, t, _ = res["qa"].shape
    do, dot_ = _fox_do(dy, w["wo_rows_t"], nb, t)
    dwo_a = _fox_dwo(res["ot"], dy).reshape(heads, LANES, D_MODEL)
    dqt, dka, dva = _fox_attn_bwd(res["qa"], res["qat"], res["ka"], res["kat"], res["va"], res["ot"],
                                  res["lse"], do, dot_, _fox_active(res["stats"]))
    dq, dk, dv, df, sums, carried = _fox_post(dqt, dka, dva, res["f"], bf, carry)
    dw_in = jnp.concatenate(
        [_mm_tn("fox_dw_q", u, dq), _mm_tn("fox_dw_k", u, dk), _mm_tn("fox_dw_v", u, dv),
         _mm_tn("fox_dw_f", u, df)[:, :FOX_HEADS]], axis=1)
    dw_o = dwo_a[:, :FOX_HEAD_DIM, :].reshape(D_MODEL, D_MODEL)
    wt = w["wqkv_t"]
    du, ex_in, ex_o = _mm_sum(
        "fox_du", [(dq, wt[:D_MODEL]), (dk, wt[D_MODEL:2 * D_MODEL]), (dv, wt[2 * D_MODEL:]), (df, w["wf_t"])],
        carry=[(True, _col_slots(dw_in[None])), (True, _row_slots(dw_o[None]))])
    return du, sums, carried + [ex_in, ex_o]


def _dsilu(v):
    sg = jax.nn.sigmoid(v)
    return sg * (1.0 + v * (1.0 - sg))


def _conv_taps(scr_ref, w_ref, rows, base):
    acc = None
    for k in range(SSM_CONV):
        term = scr_ref[pl.ds(base - (SSM_CONV - 1) + k, rows), :] * w_ref[k:k + 1, :]
        acc = term if acc is None else acc + term
    return acc


def _conv_fwd(zx, cw, cb):
    s = zx.shape[0]
    tb = _tile(s, 512)
    half = SSM_CONV_DIM // 2
    hb = tb // SUBLANES

    def body(x_ref, halo_ref, w_ref, b_ref, o_ref, scr_ref):
        i = pl.program_id(0)
        scr_ref[pl.ds(0, SUBLANES), :] = jnp.where(i > 0, halo_ref[...], 0.0)
        scr_ref[pl.ds(SUBLANES, tb), :] = x_ref[...]
        o_ref[...] = jax.nn.silu(_conv_taps(scr_ref, w_ref, tb, SUBLANES) + b_ref[0:1, :])

    return pl.pallas_call(
        body, name="ssd_conv_fwd", grid=(s // tb, 2),
        in_specs=[pl.BlockSpec((tb, half), lambda i, j: (i, 1 + j)),
                  pl.BlockSpec((SUBLANES, half), lambda i, j: (jnp.maximum(i * hb - 1, 0), 1 + j)),
                  pl.BlockSpec((8, half), lambda i, j: (0, j)),
                  pl.BlockSpec((8, half), lambda i, j: (0, j))],
        out_specs=pl.BlockSpec((tb, half), lambda i, j: (i, j)),
        out_shape=jax.ShapeDtypeStruct((s, SSM_CONV_DIM), F32),
        scratch_shapes=[pltpu.VMEM((tb + SUBLANES, half), F32)],
    )(zx, zx, cw, cb)


def _conv_bwd_pre(zx, dxc, cw, cb):
    s = zx.shape[0]
    tb = _tile(s, 512)
    half = SSM_CONV_DIM // 2
    hb = tb // SUBLANES

    def body(x_ref, halo_ref, d_ref, w_ref, b_ref, o_ref, sums_ref, scr_ref):
        i = pl.program_id(1)

        @pl.when(i == 0)
        def _():
            sums_ref[...] = jnp.zeros_like(sums_ref)

        scr_ref[pl.ds(0, SUBLANES), :] = jnp.where(i > 0, halo_ref[...], 0.0)
        scr_ref[pl.ds(SUBLANES, tb), :] = x_ref[...]
        pre = _conv_taps(scr_ref, w_ref, tb, SUBLANES) + b_ref[0:1, :]
        dpre = d_ref[...] * _dsilu(pre)
        o_ref[...] = dpre
        for k in range(SSM_CONV):
            shifted = scr_ref[pl.ds(SUBLANES - (SSM_CONV - 1) + k, tb), :]
            sums_ref[k:k + 1, :] += jnp.sum(dpre * shifted, axis=0, keepdims=True)
        sums_ref[SSM_CONV:SSM_CONV + 1, :] += jnp.sum(dpre, axis=0, keepdims=True)

    return pl.pallas_call(
        body, name="ssd_conv_bwd_pre", grid=(2, s // tb),
        in_specs=[pl.BlockSpec((tb, half), lambda j, i: (i, 1 + j)),
                  pl.BlockSpec((SUBLANES, half), lambda j, i: (jnp.maximum(i * hb - 1, 0), 1 + j)),
                  pl.BlockSpec((tb, half), lambda j, i: (i, j)),
                  pl.BlockSpec((8, half), lambda j, i: (0, j)),
                  pl.BlockSpec((8, half), lambda j, i: (0, j))],
        out_specs=[pl.BlockSpec((tb, half), lambda j, i: (i, j)),
                   pl.BlockSpec((8, half), lambda j, i: (0, j))],
        out_shape=[jax.ShapeDtypeStruct((s, SSM_CONV_DIM), F32), jax.ShapeDtypeStruct((8, SSM_CONV_DIM), F32)],
        scratch_shapes=[pltpu.VMEM((tb + SUBLANES, half), F32)],
    )(zx, zx, dxc, cw, cb)


def _conv_bwd_x(dpre, cw):
    s = dpre.shape[0]
    tb = _tile(s, 512)
    hb = tb // SUBLANES
    nb = s // tb

    def body(d_ref, halo_ref, w_ref, o_ref, scr_ref):
        i = pl.program_id(0)
        scr_ref[pl.ds(0, tb), :] = d_ref[...]
        scr_ref[pl.ds(tb, SUBLANES), :] = jnp.where(i < nb - 1, halo_ref[...], 0.0)
        acc = None
        for k in range(SSM_CONV):
            term = scr_ref[pl.ds(SSM_CONV - 1 - k, tb), :] * w_ref[k:k + 1, :]
            acc = term if acc is None else acc + term
        o_ref[...] = _mx(acc)

    return pl.pallas_call(
        body, name="ssd_conv_bwd_x", grid=(nb,),
        in_specs=[pl.BlockSpec((tb, SSM_CONV_DIM), lambda i: (i, 0)),
                  pl.BlockSpec((SUBLANES, SSM_CONV_DIM), lambda i: (jnp.minimum((i + 1) * hb, s // SUBLANES - 1), 0)),
                  pl.BlockSpec((8, SSM_CONV_DIM), lambda i: (0, 0))],
        out_specs=pl.BlockSpec((tb, SSM_CONV_DIM), lambda i: (i, 0)),
        out_shape=jax.ShapeDtypeStruct((s, SSM_CONV_DIM), MXU_DTYPE),
        scratch_shapes=[pltpu.VMEM((tb + SUBLANES, SSM_CONV_DIM), F32)],
        compiler_params=_params(VMEM_BIG),
    )(dpre, dpre, cw)


def _expand_constants():
    ex = np.zeros((LANES, SSM_D_INNER), np.float32)
    for h in range(SSM_HEADS):
        ex[h, h * 64:(h + 1) * 64] = 1.0
    return ex, np.ascontiguousarray(ex.T)


def _ssd_common(dtr_ref, par_ref, ex_ref, xc_ref):
    lc = SSM_CHUNK
    lane = lax.broadcasted_iota(jnp.int32, (1, LANES), 1)
    is_head = lane < SSM_HEADS
    par = par_ref[...]
    pre = dtr_ref[...] + par[0:1, :]
    dt = jnp.where(is_head, jax.nn.softplus(pre), 0.0)
    a = jnp.where(is_head, -jnp.exp(par[1:2, :]), 0.0)
    tri_b = lax.broadcasted_iota(jnp.int32, (lc, lc), 0) >= lax.broadcasted_iota(jnp.int32, (lc, lc), 1)
    tri = tri_b.astype(F32)
    da = dt * a
    acs = _dot_onehot(tri, da, onehot="a")
    acs_t = _dot_onehot(da, tri, (((0,), (1,)), ((), ())))
    wide = _dot_onehot(jnp.concatenate([dt, acs, par], axis=0), ex_ref[...])
    dt_x, acs_x, d_x = wide[0:lc], wide[lc:2 * lc], wide[2 * lc + 2:2 * lc + 3]
    last_x = acs_x[lc - 1:lc, :]
    xs = xc_ref[:, 0:SSM_D_INNER]
    return dict(pre=pre, dt=dt, a=a, tri_b=tri_b, tri=tri, acs=acs, acs_t=acs_t, dt_x=dt_x, d_x=d_x, xs=xs,
                xdt=xs * dt_x, e_x=jnp.exp(acs_x), dte_x=jnp.exp(last_x - acs_x), cd_x=jnp.exp(last_x),
                is_head=is_head)


def _decay_in(q, h):
    seg = q["acs"][:, h:h + 1] - q["acs_t"][h:h + 1, :]
    return jnp.exp(jnp.where(q["tri_b"], seg, -jnp.inf))


def _ssd_scan_fwd(xc, dtr, par):
    s = xc.shape[0]
    lc = SSM_CHUNK
    nc = s // lc
    ex, _ = _expand_constants()

    def body(xc_ref, dtr_ref, par_ref, ex_ref, y_ref, prev_ref, st_ref):
        @pl.when(pl.program_id(0) == 0)
        def _():
            st_ref[...] = jnp.zeros_like(st_ref)

        q = _ssd_common(dtr_ref, par_ref, ex_ref, xc_ref)
        lane = lax.broadcasted_iota(jnp.int32, (1, LANES), 1)
        for g in range(SSM_GROUPS):
            sl = slice(g * GROUP_W, (g + 1) * GROUP_W)
            bg = _mx(xc_ref[:, SSM_D_INNER + g * SSM_STATE:SSM_D_INNER + (g + 1) * SSM_STATE])
            cg = _mx(xc_ref[:, SSM_D_INNER + (SSM_GROUPS + g) * SSM_STATE:SSM_D_INNER + (SSM_GROUPS + g + 1) * SSM_STATE])
            gm = _dot(cg, bg, NT)
            prev = st_ref[g]
            prev_ref[g] = prev
            yoff = _dot(cg, _mx(prev)) * q["e_x"][:, sl]
            st_ref[g] = prev * q["cd_x"][:, sl] + _dot(bg, _mx(q["xdt"][:, sl] * q["dte_x"][:, sl]), TN)
            pairs = []
            for pr in range(2):
                xp = _mx(q["xdt"][:, g * GROUP_W + pr * LANES:g * GROUP_W + (pr + 1) * LANES])
                both = [_dot(_mx(gm * _decay_in(q, 4 * g + 2 * pr + r2)), xp) for r2 in range(2)]
                pairs.append(jnp.where(lane < 64, both[0], both[1]))
            y_ref[:, sl] = jnp.concatenate(pairs, axis=1) + yoff + q["xs"][:, sl] * q["d_x"][:, sl]

    return pl.pallas_call(
        body, name="ssd_scan_fwd", grid=(nc,),
        in_specs=[pl.BlockSpec((lc, SSM_CONV_DIM), lambda c: (c, 0)),
                  pl.BlockSpec((lc, LANES), lambda c: (c, 0)),
                  pl.BlockSpec((8, LANES), lambda c: (0, 0)),
                  pl.BlockSpec((LANES, SSM_D_INNER), lambda c: (0, 0))],
        out_specs=[pl.BlockSpec((lc, SSM_D_INNER), lambda c: (c, 0)),
                   pl.BlockSpec((None, SSM_GROUPS, SSM_STATE, GROUP_W), lambda c: (c, 0, 0, 0))],
        out_shape=[jax.ShapeDtypeStruct((s, SSM_D_INNER), F32),
                   jax.ShapeDtypeStruct((nc, SSM_GROUPS, SSM_STATE, GROUP_W), F32)],
        scratch_shapes=[pltpu.VMEM((SSM_GROUPS, SSM_STATE, GROUP_W), F32)],
        compiler_params=_params(VMEM_BIG),
    )(xc, dtr, par, _mx(jnp.asarray(ex)))


def _ssd_scan_bwd(dy, xc, dtr, par, prev, carry=()):
    s = xc.shape[0]
    lc = SSM_CHUNK
    nc = s // lc
    ex, ex_t = _expand_constants()
    carry = list(carry)
    anywhere = pl.BlockSpec(memory_space=pl.ANY)

    def body(dy_ref, xc_ref, dtr_ref, par_ref, prev_ref, ex_ref, ext_ref, dxc_ref, ddtr_ref, sums_ref,
             gst_ref, tacs_ref, tdt_ref, tdd_ref):
        @pl.when(pl.program_id(0) == 0)
        def _():
            gst_ref[...] = jnp.zeros_like(gst_ref)
            sums_ref[...] = jnp.zeros_like(sums_ref)

        q = _ssd_common(dtr_ref, par_ref, ex_ref, xc_ref)
        lane = lax.broadcasted_iota(jnp.int32, (1, LANES), 1)
        row = lax.broadcasted_iota(jnp.int32, (lc, 1), 0)
        dacs_rows = jnp.zeros((lc, LANES), F32)
        dacs_cols_t = jnp.zeros((LANES, lc), F32)
        for g in range(SSM_GROUPS):
            sl = slice(g * GROUP_W, (g + 1) * GROUP_W)
            b_lo = SSM_D_INNER + g * SSM_STATE
            c_lo = SSM_D_INNER + (SSM_GROUPS + g) * SSM_STATE
            bg = _mx(xc_ref[:, b_lo:b_lo + SSM_STATE])
            cg = _mx(xc_ref[:, c_lo:c_lo + SSM_STATE])
            dyg = dy_ref[:, sl]
            xsg, xdtg = q["xs"][:, sl], q["xdt"][:, sl]
            eg, dteg, cdg = q["e_x"][:, sl], q["dte_x"][:, sl], q["cd_x"][:, sl]
            prevg = prev_ref[g]
            gs = gst_ref[g]
            prevm, gsm = _mx(prevg), _mx(gs)
            tdd_ref[:, sl] = dyg * xsg
            dxs = dyg * q["d_x"][:, sl]
            t_acs = dyg * _dot(cg, prevm) * eg
            dcp = _mx(dyg * eg)
            dc = _dot(dcp, prevm, NT)
            dprev = _dot(cg, dcp, TN)
            db = _dot(_mx(xdtg * dteg), gsm, NT)
            dx2 = _dot(bg, gsm)
            dxdt = dx2 * dteg
            ddte = dx2 * xdtg * dteg
            t_acs = t_acs - ddte
            last = (jnp.sum(ddte, axis=0, keepdims=True)
                    + jnp.sum(gs * prevg, axis=0, keepdims=True) * cdg)
            gm = _dot(cg, bg, NT)
            dgm = jnp.zeros((lc, lc), F32)
            pair_dx = []
            for pr in range(2):
                lo = g * GROUP_W + pr * LANES
                xp = _mx(q["xdt"][:, lo:lo + LANES])
                dyp = dy_ref[:, lo:lo + LANES]
                both = []
                for r2 in range(2):
                    h = 4 * g + 2 * pr + r2
                    mine = (lane >= 64 * r2) & (lane < 64 * (r2 + 1))
                    lm = _decay_in(q, h)
                    m = gm * lm
                    dm = _dot(_mx(jnp.where(mine, dyp, 0.0)), xp, NT)
                    dgm = dgm + dm * lm
                    w = dm * m
                    dacs_rows = dacs_rows + jnp.sum(w, axis=1, keepdims=True) * (lane == h).astype(F32)
                    head_row = (lax.broadcasted_iota(jnp.int32, (LANES, 1), 0) == h).astype(F32)
                    dacs_cols_t = dacs_cols_t + head_row * jnp.sum(w, axis=0, keepdims=True)
                    both.append(_dot(_mx(m), _mx(dyp), TN))
                pair_dx.append(jnp.where(lane < 64, both[0], both[1]))
            dxdt = dxdt + jnp.concatenate(pair_dx, axis=1)
            dgmm = _mx(dgm)
            dc = dc + _dot(dgmm, bg)
            db = db + _dot(dgmm, cg, TN)
            dxs = dxs + dxdt * q["dt_x"][:, sl]
            tdt_ref[:, sl] = dxdt * xsg
            tacs_ref[:, sl] = t_acs + jnp.where(row == lc - 1, last, 0.0)
            dxc_ref[:, sl] = dxs
            dxc_ref[:, b_lo:b_lo + SSM_STATE] = db
            dxc_ref[:, c_lo:c_lo + SSM_STATE] = dc
            gst_ref[g] = gs * cdg + dprev
        tdd = jnp.broadcast_to(jnp.sum(tdd_ref[...], axis=0, keepdims=True), (8, SSM_D_INNER))
        heads_of = _dot_onehot(jnp.concatenate([tacs_ref[...], tdt_ref[...], tdd], axis=0), ext_ref[...])
        dacs = heads_of[0:lc] + dacs_rows - dacs_cols_t.T
        dda = _dot_onehot(q["tri"], dacs, TN, onehot="a")
        ddt = dda * q["a"] + heads_of[lc:2 * lc]
        ddtr = jnp.where(q["is_head"], ddt * jax.nn.sigmoid(q["pre"]), 0.0)
        ddtr_ref[...] = _mx(ddtr)
        sums_ref[0:1, :] += jnp.sum(ddtr, axis=0, keepdims=True)
        sums_ref[1:2, :] += jnp.sum(dda * q["dt"], axis=0, keepdims=True) * q["a"]
        sums_ref[2:3, :] += heads_of[2 * lc:2 * lc + 1]

    rev = lambda c: nc - 1 - c
    wide = pltpu.VMEM((lc, SSM_D_INNER), F32)
    res = pl.pallas_call(
        _carrying(body, 7, 3, 4, carry, (nc,)), name="ssd_scan_bwd", grid=(nc,),
        in_specs=[pl.BlockSpec((lc, SSM_D_INNER), lambda c: (rev(c), 0)),
                  pl.BlockSpec((lc, SSM_CONV_DIM), lambda c: (rev(c), 0)),
                  pl.BlockSpec((lc, LANES), lambda c: (rev(c), 0)),
                  pl.BlockSpec((8, LANES), lambda c: (0, 0)),
                  pl.BlockSpec((None, SSM_GROUPS, SSM_STATE, GROUP_W), lambda c: (rev(c), 0, 0, 0)),
                  pl.BlockSpec((LANES, SSM_D_INNER), lambda c: (0, 0)),
                  pl.BlockSpec((SSM_D_INNER, LANES), lambda c: (0, 0))] + [anywhere] * len(carry),
        out_specs=[pl.BlockSpec((lc, SSM_CONV_DIM), lambda c: (rev(c), 0)),
                   pl.BlockSpec((lc, LANES), lambda c: (rev(c), 0)),
                   pl.BlockSpec((8, LANES), lambda c: (0, 0))] + [anywhere] * len(carry),
        out_shape=[jax.ShapeDtypeStruct((s, SSM_CONV_DIM), F32), jax.ShapeDtypeStruct((s, LANES), MXU_DTYPE),
                   jax.ShapeDtypeStruct((8, LANES), F32)] + _exchange_shapes(carry),
        scratch_shapes=[pltpu.VMEM((SSM_GROUPS, SSM_STATE, GROUP_W), F32), wide, wide, wide]
        + (_exchange_sems(carry) if carry else []),
        compiler_params=_params(VMEM_BIG),
    )(dy, xc, dtr, par, prev, _mx(jnp.asarray(ex)), _mx(jnp.asarray(ex_t)), *[arr for _, arr in carry])
    return (*res[:3], list(res[3:]))


def _group_norm_parts(yv, zv):
    yg = yv * jax.nn.silu(zv)
    normed, rinvs = [], []
    for g in range(SSM_GROUPS):
        blk = yg[:, g * GROUP_W:(g + 1) * GROUP_W]
        rinv = lax.rsqrt(jnp.mean(blk * blk, axis=-1, keepdims=True) + RMS_EPS)
        normed.append(blk * rinv)
        rinvs.append(rinv)
    return normed, rinvs


def _gnorm_fwd(y, zx, nw):
    s = y.shape[0]
    tb = _tile(s, 512)

    def body(y_ref, z_ref, w_ref, o_ref):
        normed, _ = _group_norm_parts(y_ref[...], z_ref[...])
        for g in range(SSM_GROUPS):
            sl = slice(g * GROUP_W, (g + 1) * GROUP_W)
            o_ref[:, sl] = _mx(normed[g] * w_ref[0:1, sl])

    row = pl.BlockSpec((tb, SSM_D_INNER), lambda i: (i, 0))
    return pl.pallas_call(
        body, name="ssd_gnorm_fwd", grid=(s // tb,),
        in_specs=[row, row, pl.BlockSpec((8, SSM_D_INNER), lambda i: (0, 0))],
        out_specs=row, out_shape=jax.ShapeDtypeStruct((s, SSM_D_INNER), MXU_DTYPE),
    )(y, zx, nw)


def _gnorm_bwd(y, zx, nw, dyn):
    s = y.shape[0]
    tb = _tile(s, 512)

    def body(y_ref, z_ref, w_ref, d_ref, dy_ref, dz_ref, sums_ref):
        @pl.when(pl.program_id(0) == 0)
        def _():
            sums_ref[...] = jnp.zeros_like(sums_ref)

        yv, zv = y_ref[...], z_ref[...]
        normed, rinvs = _group_norm_parts(yv, zv)
        gate = jax.nn.silu(zv)
        dgate = _dsilu(zv)
        for g in range(SSM_GROUPS):
            sl = slice(g * GROUP_W, (g + 1) * GROUP_W)
            dv = d_ref[:, sl]
            n = normed[g]
            sums_ref[0:1, sl] += jnp.sum(dv * n, axis=0, keepdims=True)
            dn = dv * w_ref[0:1, sl]
            dyg = rinvs[g] * (dn - n * jnp.mean(dn * n, axis=-1, keepdims=True))
            dy_ref[:, sl] = dyg * gate[:, sl]
            dz_ref[:, sl] = _mx(dyg * yv[:, sl] * dgate[:, sl])

    row = pl.BlockSpec((tb, SSM_D_INNER), lambda i: (i, 0))
    par = pl.BlockSpec((8, SSM_D_INNER), lambda i: (0, 0))
    return pl.pallas_call(
        body, name="ssd_gnorm_bwd", grid=(s // tb,),
        in_specs=[row, row, par, row], out_specs=[row, row, par],
        out_shape=[jax.ShapeDtypeStruct((s, SSM_D_INNER), F32), jax.ShapeDtypeStruct((s, SSM_D_INNER), MXU_DTYPE),
                   jax.ShapeDtypeStruct((8, SSM_D_INNER), F32)],
    )(y, zx, nw, dyn)


def _rows8(v):
    v = v.reshape(1, -1)
    return jnp.pad(v, ((0, 7), (0, 0)))


def _ssd_weights(w_in, w_out):
    nzx = SSM_D_INNER + SSM_CONV_DIM
    wzx = w_in[:, :nzx]
    wdt = jnp.pad(w_in[:, nzx:], ((0, 0), (0, LANES - SSM_HEADS)))
    return dict(wzx=wzx, wdt=wdt, wzx_t=wzx.T, wdt_t=wdt.T, wout=w_out, wout_t=w_out.T)


def _ssd_fwd(u, w, cw, cb, par, nw):
    zx = _mm_nn("ssd_in_zx", u, w["wzx"], F32)
    dtr = _mm_nn("ssd_in_dt", u, w["wdt"], F32)
    xc = _conv_fwd(zx, cw, cb)
    y, prev = _ssd_scan_fwd(xc, dtr, par)
    yn = _gnorm_fwd(y, zx, nw)
    out = _mm_nn("ssd_out", yn, w["wout"], F32)
    return out, dict(zx=zx, dtr=dtr, xc=xc, y=y, prev=prev, yn=yn)


def _ssd_bwd(dy, u, w, cw, cb, par, nw, res, carry=()):
    dyn = _mm_nn("ssd_dyn", dy, w["wout_t"], F32)
    dw_out = _mm_tn("ssd_dw_out", res["yn"], dy)
    dys, dz, nsum = _gnorm_bwd(res["y"], res["zx"], nw, dyn)
    dxc, ddtr, ssum, carried = _ssd_scan_bwd(dys, res["xc"], res["dtr"], par, res["prev"],
                                             list(carry) + [(True, _row_slots(dw_out[None]))])
    dpre, csum = _conv_bwd_pre(res["zx"], dxc, cw, cb)
    dxbc = _conv_bwd_x(dpre, cw)
    wt = w["wzx_t"]
    du = _mm_sum("ssd_du", [(dz, wt[:SSM_D_INNER]), (dxbc, wt[SSM_D_INNER:]), (ddtr, w["wdt_t"])], tn=512)
    dw_in = jnp.concatenate(
        [_mm_tn("ssd_dw_z", u, dz), _mm_tn("ssd_dw_x", u, dxbc), _mm_tn("ssd_dw_dt", u, ddtr)[:, :SSM_HEADS]], axis=1)
    small = dict(conv_w=csum[:SSM_CONV], conv_b=csum[SSM_CONV], dt_bias=ssum[0, :SSM_HEADS],
                 a_log=ssum[1, :SSM_HEADS], d=ssum[2, :SSM_HEADS], norm_w=nsum[0])
    return du, dw_in, small, carried


def _ada_fwd(c_all, ada_w, ada_b_mine):
    nl, _, ncol = ada_w.shape

    def body(c_ref, w_ref, b_ref, o_ref):
        cond = _mx(jax.nn.silu(c_ref[...]))
        for i in range(nl):
            o_ref[i] = _dot(cond, _mx(w_ref[i])) + b_ref[i:i + 1, :]

    return pl.pallas_call(
        body, name="ada_fwd", out_shape=jax.ShapeDtypeStruct((nl, 2 * N_DEV, ncol), F32),
        compiler_params=_params(VMEM_BIG),
    )(c_all, ada_w, ada_b_mine)


def _ada_bwd(c_all, dmod_cols):
    nl, _, ncol = dmod_cols.shape

    def body(c_ref, d_ref, o_ref):
        cond = _mx(jax.nn.silu(c_ref[...]))
        for i in range(nl):
            o_ref[i] = _dot(cond, _mx(d_ref[i]), TN)

    return pl.pallas_call(
        body, name="ada_bwd", out_shape=jax.ShapeDtypeStruct((nl, D_MODEL, ncol), F32),
        compiler_params=_params(VMEM_BIG),
    )(c_all, dmod_cols)


def _adamw(gslots, w, m, v, name):
    k, r, c = gslots.shape
    tr = _tile(r, 256) if r % 256 == 0 else r
    c1 = 1.0 - ADAM_B1 ** ADAM_STEP
    c2 = 1.0 - ADAM_B2 ** ADAM_STEP

    def body(g_ref, w_ref, m_ref, v_ref, go_ref, d_ref, mo_ref, vo_ref):
        g = g_ref[0]
        for slot in range(1, k):
            g = g + g_ref[slot]
        mn = ADAM_B1 * m_ref[...] + (1.0 - ADAM_B1) * g
        vn = ADAM_B2 * v_ref[...] + (1.0 - ADAM_B2) * jnp.square(g)
        go_ref[...] = g
        mo_ref[...] = mn
        vo_ref[...] = vn
        d_ref[...] = -ADAM_LR * ((mn / c1) / (jnp.sqrt(vn / c2) + ADAM_EPS) + ADAM_WD * w_ref[...])

    row = pl.BlockSpec((tr, c), lambda i: (i, 0))
    shp = jax.ShapeDtypeStruct((r, c), F32)
    return pl.pallas_call(
        body, name=name, grid=(r // tr,),
        in_specs=[pl.BlockSpec((k, tr, c), lambda i: (0, i, 0)), row, row, row],
        out_specs=[row, row, row, row], out_shape=[shp, shp, shp, shp],
        compiler_params=_params(VMEM_BIG),
    )(gslots, w, m, v)


def _adamw_any(gslots, w, m, v, name):
    shape = w.shape
    two_d = (-1, shape[-1])
    k = gslots.shape[0]
    outs = _adamw(gslots.reshape((k,) + w.reshape(two_d).shape), w.reshape(two_d), m.reshape(two_d),
                  v.reshape(two_d), name)
    return tuple(o.reshape(shape) for o in outs)


def _cols_from_slots(g):
    g = jnp.moveaxis(g, 0, -2)
    return g.reshape(g.shape[:-2] + (g.shape[-2] * g.shape[-1],))


def _rows_from_slots(g):
    g = jnp.moveaxis(g, 0, -3)
    return g.reshape(g.shape[:-3] + (g.shape[-3] * g.shape[-2], g.shape[-1]))


def _col_slots(g):
    cs = g.shape[-1] // N_DEV
    return jnp.moveaxis(g.reshape(g.shape[:-1] + (N_DEV, cs)), -2, 0)


def _row_slots(g):
    rs = g.shape[-2] // N_DEV
    return jnp.moveaxis(g.reshape(g.shape[:-2] + (N_DEV, rs, g.shape[-1])), -3, 0)


def _gather_cols(w, name, dtype=None):
    return _cols_from_slots(_all_gather(w.astype(dtype or MXU_DTYPE), name))


def _gather_rows(w, name):
    return _rows_from_slots(_all_gather(_mx(w), name))


def kernel(x, c, ada_w, ada_b, ln_mix_g, ln_mix_b, ln_mlp_g, ln_mlp_b, mlp_w1, mlp_w2, fox_w_in, fox_b_f, fox_w_o, ssm_w_in, ssm_conv_w, ssm_conv_b, ssm_dt_bias, ssm_a_log, ssm_d, ssm_norm_w, ssm_w_out, loss_target, m_ada_w, m_ada_b, m_ln_mix_g, m_ln_mix_b, m_ln_mlp_g, m_ln_mlp_b, m_mlp_w1, m_mlp_w2, m_fox_w_in, m_fox_b_f, m_fox_w_o, m_ssm_w_in, m_ssm_conv_w, m_ssm_conv_b, m_ssm_dt_bias, m_ssm_a_log, m_ssm_d, m_ssm_norm_w, m_ssm_w_out, v_ada_w, v_ada_b, v_ln_mix_g, v_ln_mix_b, v_ln_mlp_g, v_ln_mlp_b, v_mlp_w1, v_mlp_w2, v_fox_w_in, v_fox_b_f, v_fox_w_o, v_ssm_w_in, v_ssm_conv_w, v_ssm_conv_b, v_ssm_dt_bias, v_ssm_a_log, v_ssm_d, v_ssm_norm_w, v_ssm_w_out):
    me = 4 * lax.axis_index("x") + 2 * lax.axis_index("y") + lax.axis_index("c")
    xs = x[0]
    target = loss_target[0]
    d = D_MODEL

    c_all = _all_gather(c, "gather_c").reshape(N_DEV, d)
    c_all = jnp.pad(c_all, ((0, N_DEV), (0, 0)))
    ncol = ada_w.shape[-1]
    ada_b_mine = lax.dynamic_slice_in_dim(ada_b, me * ncol, ncol, axis=1)
    mod_cols = _ada_fwd(c_all, ada_w, ada_b_mine)
    mod_all = _all_gather(mod_cols, "gather_mod")
    mod = lax.dynamic_index_in_dim(mod_all, me, axis=2, keepdims=False)
    mod = jnp.moveaxis(mod, 0, 1).reshape(DEPTH, 6, d)

    def pv_rows(*rows):
        return jnp.pad(jnp.stack(rows), ((0, 8 - len(rows)), (0, 0)))

    fw = _fox_weights(_gather_cols(fox_w_in, "gather_fox_in")[0], _gather_rows(fox_w_o, "gather_fox_o")[0])
    conv_w = _gather_cols(ssm_conv_w, "gather_conv_w", F32)[0]
    small_vec = jnp.concatenate([ssm_conv_b[0], ssm_norm_w[0]]).reshape(1, -1)
    small_all = _all_gather(small_vec.astype(F32), "gather_conv_b").reshape(N_DEV, -1)
    conv_b = small_all[:, :SSM_CONV_DIM // N_DEV].reshape(-1)
    norm_w = small_all[:, SSM_CONV_DIM // N_DEV:].reshape(-1)
    cw8 = jnp.pad(conv_w, ((0, 8 - SSM_CONV), (0, 0)))
    cb8 = _rows8(conv_b)
    nw8 = _rows8(norm_w)
    bf8 = _rows8(jnp.pad(fox_b_f[0], (0, LANES - FOX_HEADS)))
    par8 = jnp.pad(jnp.stack([jnp.pad(p[0], (0, LANES - SSM_HEADS)) for p in (ssm_dt_bias, ssm_a_log, ssm_d)]),
                   ((0, 5), (0, 0)))

    sh_a, sc_a, g_a, sh_m, sc_m, g_m = [mod[:, k] for k in range(6)]
    u0 = _modulate(xs, pv_rows(1.0 + sc_a[0], sh_a[0]), "modulate0")
    y0, fres, gathered = _fox_fwd(u0, fw, bf8, [(False, _mx(w)) for w in (mlp_w1, mlp_w2, ssm_w_in, ssm_w_out)])
    w1 = _cols_from_slots(gathered[0])
    w2 = _rows_from_slots(gathered[1])
    sw = _ssd_weights(_cols_from_slots(gathered[2])[0], _rows_from_slots(gathered[3])[0])
    pv0 = pv_rows(1.0 + g_a[0], ln_mix_g[0], ln_mix_b[0], 1.0 + sc_m[0], sh_m[0])
    x1, u1 = _ln_fwd(xs, y0, pv0, "ln_mix0")
    y1, (h0, a0) = _mlp_fwd(u1, w1[0], w2[0], "0")
    pv1 = pv_rows(1.0 + g_m[0], ln_mlp_g[0], ln_mlp_b[0], 1.0 + sc_a[1], sh_a[1])
    x2, u2 = _ln_fwd(x1, y1, pv1, "ln_mlp0")
    y2, sres = _ssd_fwd(u2, sw, cw8, cb8, par8, nw8)
    pv2 = pv_rows(1.0 + g_a[1], ln_mix_g[1], ln_mix_b[1], 1.0 + sc_m[1], sh_m[1])
    x3, u3 = _ln_fwd(x2, y2, pv2, "ln_mix1")
    y3, (h1, a1) = _mlp_fwd(u3, w1[1], w2[1], "1")
    pv3 = pv_rows(1.0 + g_m[1], ln_mlp_g[1], ln_mlp_b[1])

    dx3, dy3, s3 = _ln_bwd(x3, y3, pv3, "ln_mlp1_bwd", target=target)
    loss = lax.psum(s3[5, 0], ("x", "y", "c"))
    du3, dw1_1, dw2_1, _ = _mlp_bwd(dy3, u3, h1, a1, w1[1].T, w2[1].T, "1")
    dx2, dy2, s2 = _ln_bwd(x2, y2, pv2, "ln_mix1_bwd", dxo=dx3, du=du3)
    du2, d_ssm_in, ssm_small, ex_scan = _ssd_bwd(
        dy2, u2, sw, cw8, cb8, par8, nw8, sres,
        [(True, _col_slots(dw1_1[None])), (True, _row_slots(dw2_1[None]))])
    dx1, dy1, s1 = _ln_bwd(x1, y1, pv1, "ln_mlp0_bwd", dxo=dx2, du=du2)
    du1, dw1_0, dw2_0, ex_mlp = _mlp_bwd(dy1, u1, h0, a0, w1[0].T, w2[0].T, "0",
                                         [(True, _col_slots(d_ssm_in[None]))])
    dx0, dy0, s0 = _ln_bwd(xs, y0, pv0, "ln_mix0_bwd", dxo=dx1, du=du1)
    late = [(True, _col_slots(dw1_0[None])), (True, _row_slots(dw2_0[None])),
            (True, _col_slots(ssm_small["conv_w"][None])), (True, _col_slots(ssm_small["conv_b"][None])),
            (True, _col_slots(ssm_small["norm_w"][None]))]
    du0, fox_sums, ex_post = _fox_bwd(dy0, u0, fw, bf8, fres, late)
    grad_x, sx = _mod_bwd(dx0, du0, xs, pv_rows(1.0 + sc_a[0], sh_a[0]), "modulate0_bwd")

    dmod = jnp.stack([
        jnp.stack([sx[1], sx[0], s0[4], s0[1], s0[0], s1[4]]),
        jnp.stack([s1[1], s1[0], s2[4], s2[1], s2[0], s3[4]]),
    ]).reshape(DEPTH, 6 * d)

    def pad_rows(v):
        v = v.reshape(-1, LANES) if v.size % LANES == 0 else jnp.pad(v.reshape(-1), (0, LANES - v.size)).reshape(1, LANES)
        return jnp.pad(v, ((0, (-v.shape[0]) % 8), (0, 0)))

    small_parts = [dmod, jnp.stack([s0[2], s2[2]]), jnp.stack([s0[3], s2[3]]), jnp.stack([s1[2], s3[2]]),
                   jnp.stack([s1[3], s3[3]]), fox_sums[0, :FOX_HEADS], ssm_small["dt_bias"], ssm_small["a_log"],
                   ssm_small["d"]]
    packed = [pad_rows(p) for p in small_parts]
    offsets = np.cumsum([0] + [p.shape[0] for p in packed])
    small_all_g = _all_gather(jnp.concatenate(packed, axis=0), "gather_small_grads")

    def unpack(idx, shape):
        n = int(np.prod(shape))
        blk = small_all_g[:, offsets[idx]:offsets[idx + 1]].reshape(N_DEV, -1)[:, :n]
        return blk.reshape((N_DEV,) + tuple(shape))

    dmod_all = unpack(0, (DEPTH, 6 * d))
    dmod_cols = lax.dynamic_slice_in_dim(dmod_all, me * ncol, ncol, axis=2)
    dmod_cols = jnp.pad(jnp.moveaxis(dmod_cols, 0, 1), ((0, 0), (0, N_DEV), (0, 0)))
    g_ada_w = _ada_bwd(c_all, dmod_cols)

    shards = dict(
        mlp_w1=jnp.concatenate([ex_post[0], ex_scan[0]], axis=1),
        mlp_w2=jnp.concatenate([ex_post[1], ex_scan[1]], axis=1),
        ssm_w_in=ex_mlp[0], ssm_w_out=ex_scan[2],
        ssm_conv_w=ex_post[2], ssm_conv_b=ex_post[3], ssm_norm_w=ex_post[4],
        fox_w_in=ex_post[5], fox_w_o=ex_post[6],
        ada_w=g_ada_w[None], ada_b=dmod_all,
        ln_mix_g=unpack(1, (DEPTH, d)), ln_mix_b=unpack(2, (DEPTH, d)),
        ln_mlp_g=unpack(3, (DEPTH, d)), ln_mlp_b=unpack(4, (DEPTH, d)),
        fox_b_f=unpack(5, (1, FOX_HEADS)), ssm_dt_bias=unpack(6, (1, SSM_HEADS)),
        ssm_a_log=unpack(7, (1, SSM_HEADS)), ssm_d=unpack(8, (1, SSM_HEADS)),
    )
    weights = dict(ada_w=ada_w, ada_b=ada_b, ln_mix_g=ln_mix_g, ln_mix_b=ln_mix_b, ln_mlp_g=ln_mlp_g, ln_mlp_b=ln_mlp_b,
                   mlp_w1=mlp_w1, mlp_w2=mlp_w2, fox_w_in=fox_w_in, fox_b_f=fox_b_f, fox_w_o=fox_w_o, ssm_w_in=ssm_w_in,
                   ssm_conv_w=ssm_conv_w, ssm_conv_b=ssm_conv_b, ssm_dt_bias=ssm_dt_bias, ssm_a_log=ssm_a_log,
                   ssm_d=ssm_d, ssm_norm_w=ssm_norm_w, ssm_w_out=ssm_w_out)
    mom1 = dict(ada_w=m_ada_w, ada_b=m_ada_b, ln_mix_g=m_ln_mix_g, ln_mix_b=m_ln_mix_b, ln_mlp_g=m_ln_mlp_g,
                ln_mlp_b=m_ln_mlp_b, mlp_w1=m_mlp_w1, mlp_w2=m_mlp_w2, fox_w_in=m_fox_w_in, fox_b_f=m_fox_b_f,
                fox_w_o=m_fox_w_o, ssm_w_in=m_ssm_w_in, ssm_conv_w=m_ssm_conv_w, ssm_conv_b=m_ssm_conv_b,
                ssm_dt_bias=m_ssm_dt_bias, ssm_a_log=m_ssm_a_log, ssm_d=m_ssm_d, ssm_norm_w=m_ssm_norm_w,
                ssm_w_out=m_ssm_w_out)
    mom2 = dict(ada_w=v_ada_w, ada_b=v_ada_b, ln_mix_g=v_ln_mix_g, ln_mix_b=v_ln_mix_b, ln_mlp_g=v_ln_mlp_g,
                ln_mlp_b=v_ln_mlp_b, mlp_w1=v_mlp_w1, mlp_w2=v_mlp_w2, fox_w_in=v_fox_w_in, fox_b_f=v_fox_b_f,
                fox_w_o=v_fox_w_o, ssm_w_in=v_ssm_w_in, ssm_conv_w=v_ssm_conv_w, ssm_conv_b=v_ssm_conv_b,
                ssm_dt_bias=v_ssm_dt_bias, ssm_a_log=v_ssm_a_log, ssm_d=v_ssm_d, ssm_norm_w=v_ssm_norm_w,
                ssm_w_out=v_ssm_w_out)
    names = list(weights)
    stepped = {n: _adamw_any(shards[n], weights[n], mom1[n], mom2[n], f"adamw_{n}") for n in names}
    return (loss, grad_x[None], *[stepped[n][0] for n in names], *[stepped[n][1] for n in names],
            *[stepped[n][2] for n in names], *[stepped[n][3] for n in names])
```

```python
import numpy as np
import jax
import jax.numpy as jnp
from jax import lax
from jax.experimental import pallas as pl
from jax.experimental.pallas import tpu as pltpu

F32 = jnp.float32
MXU_DTYPE = jnp.bfloat16

N_DEV = 8
D_MODEL = 1024
DEPTH = 2
FOX_HEADS = 16
FOX_HEAD_DIM = 64
D_FF = 4096
SSM_D_INNER = 2048
SSM_HEADS = 32
SSM_GROUPS = 8
SSM_STATE = 128
SSM_CHUNK = 128
SSM_CONV = 4
SSM_CONV_DIM = 4096
GROUP_W = SSM_D_INNER // SSM_GROUPS
LN_EPS = 1e-5
RMS_EPS = 1e-5
ALPHA = (2.0 * DEPTH) ** 0.25
LANES = 128
SUBLANES = 8

ADAM_LR = 0.001
ADAM_B1 = 0.9
ADAM_B2 = 0.999
ADAM_EPS = 1e-08
ADAM_WD = 0.01
ADAM_STEP = 10

NN = (((1,), (0,)), ((), ()))
NT = (((1,), (1,)), ((), ()))
TN = (((0,), (0,)), ((), ()))

VMEM_BIG = 56 * 1024 * 1024
MM_ROWS = 2048
MM_DEPTH = 2048


def _dot(a, b, dims=NN, precision=None):
    return lax.dot_general(a, b, dims, precision=precision, preferred_element_type=F32)


def _mx(v):
    return v.astype(MXU_DTYPE)


def _pieces3(v):
    hi = _mx(v)
    r1 = v - hi.astype(F32)
    mid = _mx(r1)
    return hi, mid, _mx(r1 - mid.astype(F32))


def _dot_onehot(a, b, dims=NN, onehot="b"):
    if onehot == "b":
        return sum(_dot(p, _mx(b), dims) for p in _pieces3(a))
    return sum(_dot(_mx(a), p, dims) for p in _pieces3(b))


def _params(vmem=None):
    return pltpu.CompilerParams(vmem_limit_bytes=vmem) if vmem else None


def _all_gather(x, name):
    def body(x_ref, out_ref, send_sems, recv_sems, local_sem):
        xi, yi, ci = lax.axis_index("x"), lax.axis_index("y"), lax.axis_index("c")
        me, sibling = (xi, yi, ci), (xi, yi, 1 - ci)
        chips = [(1 - xi, yi), (xi, 1 - yi), (1 - xi, 1 - yi)]

        def slot(px, py, pc):
            return out_ref.at[4 * px + 2 * py + pc]

        def copy(k, block, to, src=None):
            return pltpu.make_async_remote_copy(
                src_ref=slot(*block) if src is None else src, dst_ref=slot(*block),
                send_sem=send_sems.at[k], recv_sem=recv_sems.at[k],
                device_id=to, device_id_type=pl.DeviceIdType.MESH)

        mine = pltpu.make_async_copy(x_ref, slot(*me), local_sem)
        mine.start()
        first = [copy(0, me, sibling, src=x_ref)]
        first += [copy(1 + j, me, (*chip, ci), src=x_ref) for j, chip in enumerate(chips)]
        for cp in first:
            cp.start()
        passed = [copy(4 + j, (*chip, ci), sibling) for j, chip in enumerate(chips)]
        for j, chip in enumerate(chips):
            copy(1 + j, (*chip, ci), me).wait_recv()
            passed[j].start()
        copy(0, sibling, me).wait_recv()
        for j, chip in enumerate(chips):
            copy(4 + j, (*chip, 1 - ci), me).wait_recv()
        for cp in first + passed:
            cp.wait_send()
        mine.wait()

    return pl.pallas_call(
        body, name=name,
        out_shape=jax.ShapeDtypeStruct((N_DEV,) + x.shape, x.dtype),
        in_specs=[pl.BlockSpec(memory_space=pl.ANY)],
        out_specs=pl.BlockSpec(memory_space=pl.ANY),
        scratch_shapes=[pltpu.SemaphoreType.DMA((7,)), pltpu.SemaphoreType.DMA((7,)),
                        pltpu.SemaphoreType.DMA],
    )(x)


def _direct_copies(scatter, x_ref, out_ref, send_sems, recv_sems, local_sems, n):
    xi, yi, ci = lax.axis_index("x"), lax.axis_index("y"), lax.axis_index("c")
    me = 4 * xi + 2 * yi + ci
    local = pltpu.make_async_copy(x_ref.at[me] if scatter else x_ref, out_ref.at[me], local_sems.at[n])
    remote = []
    for k in range(1, N_DEV):
        px = 1 - xi if k & 4 else xi
        py = 1 - yi if k & 2 else yi
        pc = 1 - ci if k & 1 else ci
        remote.append(pltpu.make_async_remote_copy(
            src_ref=x_ref.at[4 * px + 2 * py + pc] if scatter else x_ref, dst_ref=out_ref.at[me],
            send_sem=send_sems.at[7 * n + k - 1], recv_sem=recv_sems.at[7 * n + k - 1],
            device_id=(px, py, pc), device_id_type=pl.DeviceIdType.MESH))
    return local, remote


def _exchange_shapes(carry):
    return [jax.ShapeDtypeStruct(a.shape if scatter else (N_DEV,) + a.shape, a.dtype) for scatter, a in carry]


def _exchange_sems(carry):
    n = len(carry)
    return [pltpu.SemaphoreType.DMA((7 * n,)), pltpu.SemaphoreType.DMA((7 * n,)), pltpu.SemaphoreType.DMA((n,))]


def _carrying(body, n_in, n_out, n_scratch, carry, grid):
    nc = len(carry)

    def copies(refs):
        srcs = refs[n_in:n_in + nc]
        dsts = refs[n_in + nc + n_out:n_in + 2 * nc + n_out]
        sems = refs[n_in + 2 * nc + n_out + n_scratch:]
        return [_direct_copies(scatter, srcs[n], dsts[n], *sems, n) for n, (scatter, _) in enumerate(carry)]

    def wrapped(*refs):
        step = 0
        for axis, extent in enumerate(grid):
            step = step * extent + pl.program_id(axis)

        @pl.when(step == 0)
        def _():
            for local, remote in copies(refs):
                local.start()
                for cp in remote:
                    cp.start()

        body(*refs[:n_in], *refs[n_in + nc:n_in + nc + n_out],
             *refs[n_in + 2 * nc + n_out:n_in + 2 * nc + n_out + n_scratch])

        @pl.when(step == int(np.prod(grid)) - 1)
        def _():
            for local, remote in copies(refs):
                for cp in remote:
                    cp.wait()
                local.wait()

    return wrapped if nc else body


def _mm(name, a, b, *, grid, a_spec, b_spec, dims, k_axis, outs, acc=None, extras=(), epi=None, vmem=None,
        carry=()):
    nk = grid[k_axis]
    n_ex, n_out = len(extras), len(outs)
    carry = list(carry)
    anywhere = pl.BlockSpec(memory_space=pl.ANY)

    def body(*refs):
        a_ref, b_ref = refs[0], refs[1]
        ex = refs[2:2 + n_ex]
        out = refs[2 + n_ex:2 + n_ex + n_out]

        def finish(val):
            if epi is None:
                out[0][...] = val.astype(out[0].dtype)
            else:
                epi(val, ex, out)

        part = _dot(a_ref[...], b_ref[...], dims)
        if nk == 1:
            finish(part)
        else:
            acc_ref = refs[2 + n_ex + n_out]
            k = pl.program_id(k_axis)

            @pl.when(k == 0)
            def _():
                acc_ref[...] = part

            @pl.when(k > 0)
            def _():
                acc_ref[...] += part

            @pl.when(k == nk - 1)
            def _():
                finish(acc_ref[...])

    res = pl.pallas_call(
        _carrying(body, 2 + n_ex, n_out, 1 if nk > 1 else 0, carry, grid), name=name, grid=grid,
        in_specs=[a_spec, b_spec] + [s for _, s in extras] + [anywhere] * len(carry),
        out_specs=[s for _, s in outs] + [anywhere] * len(carry),
        out_shape=[o for o, _ in outs] + _exchange_shapes(carry),
        scratch_shapes=([pltpu.VMEM(acc, F32)] if nk > 1 else []) + (_exchange_sems(carry) if carry else []),
        compiler_params=_params(vmem),
    )(a, b, *[e for e, _ in extras], *[arr for _, arr in carry])
    return list(res)


def _tile(n, t):
    t = min(n, t)
    assert n % t == 0, (n, t)
    return t


def _mm_nn(name, a, b, out_dtype, *, tm=MM_ROWS, tn=1024, tk=1024, epi=None, extras=(), outs=None, carry=()):
    m, kk = a.shape
    n = b.shape[1]
    tm, tn, tk = _tile(m, tm), _tile(n, tn), _tile(kk, tk)
    if outs is None:
        outs = [(jax.ShapeDtypeStruct((m, n), out_dtype), pl.BlockSpec((tm, tn), lambda i, j, k: (i, j)))]
    res = _mm(name, a, b, grid=(m // tm, n // tn, kk // tk),
              a_spec=pl.BlockSpec((tm, tk), lambda i, j, k: (i, k)),
              b_spec=pl.BlockSpec((tk, tn), lambda i, j, k: (k, j)),
              dims=NN, k_axis=2, acc=(tm, tn), outs=outs, extras=list(extras), epi=epi, vmem=VMEM_BIG, carry=carry)
    return res[0] if len(res) == 1 else res


def _mm_sum(name, pairs, out_dtype=F32, *, tm=1024, tn=1024, carry=()):
    m, n = pairs[0][0].shape[0], pairs[0][1].shape[1]
    tm, tn = _tile(m, tm), _tile(n, tn)
    carry = list(carry)
    grid = (m // tm, n // tn)

    def body(*refs):
        out = refs[2 * len(pairs)]
        acc = _dot(refs[0][...], refs[1][...])
        for p in range(1, len(pairs)):
            acc = acc + _dot(refs[2 * p][...], refs[2 * p + 1][...])
        out[...] = acc.astype(out.dtype)

    in_specs = []
    for a, b in pairs:
        in_specs += [pl.BlockSpec((tm, a.shape[1]), lambda i, j: (i, 0)),
                     pl.BlockSpec((b.shape[0], tn), lambda i, j: (0, j))]
    anywhere = pl.BlockSpec(memory_space=pl.ANY)
    res = pl.pallas_call(
        _carrying(body, 2 * len(pairs), 1, 0, carry, grid), name=name, grid=grid,
        in_specs=in_specs + [anywhere] * len(carry),
        out_specs=[pl.BlockSpec((tm, tn), lambda i, j: (i, j))] + [anywhere] * len(carry),
        out_shape=[jax.ShapeDtypeStruct((m, n), out_dtype)] + _exchange_shapes(carry),
        scratch_shapes=_exchange_sems(carry) if carry else [],
        compiler_params=_params(VMEM_BIG),
    )(*[x for pair in pairs for x in pair], *[arr for _, arr in carry])
    return list(res) if carry else res[0]


def _mm_tn(name, a, b, out_dtype=F32, *, tm=1024, tn=1024, tk=MM_DEPTH):
    kk, m = a.shape
    n = b.shape[1]
    tm, tn, tk = _tile(m, tm), _tile(n, tn), _tile(kk, tk)
    res = _mm(name, a, b, grid=(m // tm, n // tn, kk // tk),
              a_spec=pl.BlockSpec((tk, tm), lambda i, j, k: (k, i)),
              b_spec=pl.BlockSpec((tk, tn), lambda i, j, k: (k, j)),
              dims=TN, k_axis=2, acc=(tm, tn),
              outs=[(jax.ShapeDtypeStruct((m, n), out_dtype), pl.BlockSpec((tm, tn), lambda i, j, k: (i, j)))],
              vmem=VMEM_BIG)
    return res[0]


def _row_block(s):
    return _tile(s, 512)


def _modulate(x, pv, name):
    s, d = x.shape
    tb = _row_block(s)

    def body(x_ref, pv_ref, u_ref):
        u_ref[...] = _mx(x_ref[...] * pv_ref[0:1, :] + pv_ref[1:2, :])

    return pl.pallas_call(
        body, name=name, grid=(s // tb,),
        in_specs=[pl.BlockSpec((tb, d), lambda i: (i, 0)), pl.BlockSpec((8, d), lambda i: (0, 0))],
        out_specs=pl.BlockSpec((tb, d), lambda i: (i, 0)),
        out_shape=jax.ShapeDtypeStruct((s, d), MXU_DTYPE),
    )(x, pv)


def _ln_stats(r):
    mu = jnp.mean(r, axis=-1, keepdims=True)
    xc = r - mu
    var = jnp.mean(xc * xc, axis=-1, keepdims=True)
    rstd = lax.rsqrt(var + LN_EPS)
    return xc * rstd, rstd


def _ln_fwd(xin, y, pv, name):
    s, d = xin.shape
    tb = _row_block(s)

    def body(x_ref, y_ref, pv_ref, xo_ref, u_ref):
        r = ALPHA * x_ref[...] + pv_ref[0:1, :] * y_ref[...]
        xhat, _ = _ln_stats(r)
        xo = xhat * pv_ref[1:2, :] + pv_ref[2:3, :]
        xo_ref[...] = xo
        u_ref[...] = _mx(xo * pv_ref[3:4, :] + pv_ref[4:5, :])

    row = pl.BlockSpec((tb, d), lambda i: (i, 0))
    return pl.pallas_call(
        body, name=name, grid=(s // tb,),
        in_specs=[row, row, pl.BlockSpec((8, d), lambda i: (0, 0))],
        out_specs=[row, row],
        out_shape=[jax.ShapeDtypeStruct((s, d), F32), jax.ShapeDtypeStruct((s, d), MXU_DTYPE)],
    )(xin, y, pv)


def _ln_bwd(xin, y, pv, name, *, dxo=None, du=None, target=None):
    s, d = xin.shape
    tb = _row_block(s)
    nb = s // tb
    loss_mode = target is not None

    def body(*refs):
        if loss_mode:
            x_ref, y_ref, pv_ref, t_ref, dxin_ref, dy_ref, sums_ref = refs
        else:
            x_ref, y_ref, pv_ref, dxo_ref, du_ref, dxin_ref, dy_ref, sums_ref = refs
        i = pl.program_id(0)

        @pl.when(i == 0)
        def _():
            sums_ref[...] = jnp.zeros_like(sums_ref)

        yv = y_ref[...]
        r = ALPHA * x_ref[...] + pv_ref[0:1, :] * yv
        xhat, rstd = _ln_stats(r)
        xo = xhat * pv_ref[1:2, :] + pv_ref[2:3, :]
        if loss_mode:
            diff = xo - t_ref[...]
            dxo_v = diff * (1.0 / d)
            sums_ref[5:6, :] += jnp.sum(diff * diff, axis=0, keepdims=True) * (0.5 / d)
        else:
            duv = du_ref[...]
            dxo_v = dxo_ref[...] + duv * pv_ref[3:4, :]
            sums_ref[0:1, :] += jnp.sum(duv * xo, axis=0, keepdims=True)
            sums_ref[1:2, :] += jnp.sum(duv, axis=0, keepdims=True)
        sums_ref[2:3, :] += jnp.sum(dxo_v * xhat, axis=0, keepdims=True)
        sums_ref[3:4, :] += jnp.sum(dxo_v, axis=0, keepdims=True)
        dxh = dxo_v * pv_ref[1:2, :]
        dr = rstd * (dxh - jnp.mean(dxh, axis=-1, keepdims=True)
                     - xhat * jnp.mean(dxh * xhat, axis=-1, keepdims=True))
        sums_ref[4:5, :] += jnp.sum(dr * yv, axis=0, keepdims=True)
        dxin_ref[...] = ALPHA * dr
        dy_ref[...] = _mx(pv_ref[0:1, :] * dr)
        if loss_mode:
            @pl.when(i == nb - 1)
            def _():
                sums_ref[5:6, :] = jnp.broadcast_to(jnp.sum(sums_ref[5:6, :], axis=-1, keepdims=True), (1, d))

    row = pl.BlockSpec((tb, d), lambda i: (i, 0))
    par = pl.BlockSpec((8, d), lambda i: (0, 0))
    ins = [xin, y, pv] + ([target] if loss_mode else [dxo, du])
    return pl.pallas_call(
        body, name=name, grid=(nb,),
        in_specs=[row, row, par] + [row] * (len(ins) - 3),
        out_specs=[row, row, par],
        out_shape=[jax.ShapeDtypeStruct((s, d), F32), jax.ShapeDtypeStruct((s, d), MXU_DTYPE),
                   jax.ShapeDtypeStruct((8, d), F32)],
    )(*ins)


def _mod_bwd(dx_direct, du, x, pv, name):
    s, d = x.shape
    tb = _row_block(s)

    def body(dxd_ref, du_ref, x_ref, pv_ref, dx_ref, sums_ref):
        @pl.when(pl.program_id(0) == 0)
        def _():
            sums_ref[...] = jnp.zeros_like(sums_ref)

        duv = du_ref[...]
        dx_ref[...] = dxd_ref[...] + duv * pv_ref[0:1, :]
        sums_ref[0:1, :] += jnp.sum(duv * x_ref[...], axis=0, keepdims=True)
        sums_ref[1:2, :] += jnp.sum(duv, axis=0, keepdims=True)

    row = pl.BlockSpec((tb, d), lambda i: (i, 0))
    par = pl.BlockSpec((8, d), lambda i: (0, 0))
    return pl.pallas_call(
        body, name=name, grid=(s // tb,),
        in_specs=[row, row, row, par], out_specs=[row, par],
        out_shape=[jax.ShapeDtypeStruct((s, d), F32), jax.ShapeDtypeStruct((8, d), F32)],
    )(dx_direct, du, x, pv)


def _mlp_fwd(u, w1, w2, tag, carry=()):
    s = u.shape[0]

    def epi(val, ex, out):
        out[0][...] = _mx(val)
        out[1][...] = _mx(jnp.square(jnp.maximum(val, 0.0)))

    tm, tn = _tile(s, MM_ROWS), 1024
    spec = pl.BlockSpec((tm, tn), lambda i, j, k: (i, j))
    shp = jax.ShapeDtypeStruct((s, D_FF), MXU_DTYPE)
    h, a, *carried = _mm_nn(f"mlp_up{tag}", u, w1, None, epi=epi, outs=[(shp, spec), (shp, spec)], tn=tn,
                            carry=carry)
    y = _mm_nn(f"mlp_down{tag}", a, w2, F32)
    return y, (h, a), carried


def _mlp_bwd(dy, u, h, a, w1t, w2t, tag, carry=(), carry_du=()):
    s = u.shape[0]
    tm, tn = _tile(s, MM_ROWS), 1024
    spec = pl.BlockSpec((tm, tn), lambda i, j, k: (i, j))

    def epi(val, ex, out):
        out[0][...] = _mx(val * (2.0 * jnp.maximum(ex[0][...].astype(F32), 0.0)))

    got = _mm_nn(f"mlp_dh{tag}", dy, w2t, None, epi=epi, extras=[(h, spec)],
                 outs=[(jax.ShapeDtypeStruct((s, D_FF), MXU_DTYPE), spec)], tn=tn, carry=carry)
    dh, carried = (got[0], list(got[1:])) if carry else (got, [])
    got = _mm_nn(f"mlp_du{tag}", dh, w1t, F32, carry=carry_du)
    du, carried = (got[0], carried + list(got[1:])) if carry_du else (got, carried)
    dw2 = _mm_tn(f"mlp_dw2{tag}", a, dy)
    dw1 = _mm_tn(f"mlp_dw1{tag}", u, dh)
    return du, dw1, dw2, carried


FOX_T = 1024
BIAS_Q = (64, 65, 66)
BIAS_K = (67, 68, 69)
SKIP_MARGIN = 110.0
ONES_V = 64

def _fox_constants():
    selq = np.zeros((FOX_HEADS, 512, LANES), np.float32)
    selk = np.zeros((FOX_HEADS, 512, LANES), np.float32)
    selv = np.zeros((2, LANES, LANES), np.float32)
    put = np.zeros((2, 2, LANES, LANES), np.float32)
    for h in range(FOX_HEADS):
        off = FOX_HEAD_DIM * (h % 2)
        for dd in range(FOX_HEAD_DIM):
            selq[h, off + dd, dd] = FOX_HEAD_DIM ** -0.5
            selk[h, off + dd, dd] = 1.0
        for piece in range(3):
            selq[h, LANES * (1 + piece) + h, BIAS_Q[piece]] = 1.0
            selk[h, LANES * (1 + piece) + h, BIAS_K[piece]] = -1.0
    for par in range(2):
        for dd in range(FOX_HEAD_DIM):
            selv[par, FOX_HEAD_DIM * par + dd, dd] = 1.0
            put[par, 0, dd, FOX_HEAD_DIM * par + dd] = FOX_HEAD_DIM ** -0.5
            put[par, 1, dd, FOX_HEAD_DIM * par + dd] = 1.0
    return selq, selk, selv, put


def _fox_prep(qkv, f, bf, carry=()):
    s = qkv.shape[0]
    t = _tile(s, FOX_T)
    nb = s // t
    selq, selk, selv, _ = _fox_constants()
    carry = list(carry)
    anywhere = pl.BlockSpec(memory_space=pl.ANY)

    def body(q_ref, k_ref, v_ref, f_ref, bf_ref, selq_ref, selk_ref, selv_ref,
             qa_ref, qat_ref, ka_ref, kat_ref, va_ref, vat_ref, stats_ref, parts_ref, carry_ref, cum_ref):
        i, h = pl.program_id(0), pl.program_id(1)
        lane = lax.broadcasted_iota(jnp.int32, (1, LANES), 1)

        @pl.when(h == 0)
        def _():
            @pl.when(i == 0)
            def _():
                carry_ref[...] = jnp.zeros_like(carry_ref)

            lf = jnp.where(lane < FOX_HEADS, jax.nn.log_sigmoid(f_ref[...] + bf_ref[0:1, :]), 0.0)
            tri = (lax.broadcasted_iota(jnp.int32, (t, t), 0) >= lax.broadcasted_iota(jnp.int32, (t, t), 1)).astype(F32)
            cum = _dot_onehot(tri, lf, onehot="a") + carry_ref[0:1, :]
            carry_ref[0:1, :] = cum[t - 1:t, :]
            cum_ref[...] = cum
            hi = _mx(cum)
            r1 = cum - hi.astype(F32)
            mid = _mx(r1)
            parts_ref[:, 0:LANES] = hi
            parts_ref[:, LANES:2 * LANES] = mid
            parts_ref[:, 2 * LANES:3 * LANES] = _mx(r1 - mid.astype(F32))

        parts = parts_ref[...]
        qa = _dot(jnp.concatenate([q_ref[...], parts], axis=1), selq_ref[...])
        qa = qa + jnp.where((lane >= BIAS_K[0]) & (lane <= BIAS_K[2]), 1.0, 0.0)
        ka = _dot(jnp.concatenate([k_ref[...], parts], axis=1), selk_ref[...])
        ka = ka + jnp.where((lane >= BIAS_Q[0]) & (lane <= BIAS_Q[2]), 1.0, 0.0)
        va = _dot(v_ref[...], selv_ref[...]) + jnp.where(lane == ONES_V, 1.0, 0.0)
        qa_ref[...] = _mx(qa)
        qat_ref[...] = _mx(qa.T)
        ka_ref[...] = _mx(ka)
        kat_ref[...] = _mx(ka.T)
        va_ref[...] = _mx(va)
        vat_ref[...] = _mx(va.T)

        def longest(rows_):
            sq = jnp.where(lane < FOX_HEAD_DIM, rows_ * rows_, 0.0)
            return jnp.sqrt(jnp.max(jnp.sum(sq, axis=1, keepdims=True), axis=0, keepdims=True))

        mine = lane == h
        cum = cum_ref[...]
        top = jnp.max(jnp.max(jnp.where(mine, cum, -jnp.inf), axis=1, keepdims=True), axis=0, keepdims=True)
        low = jnp.min(jnp.min(jnp.where(mine, cum, jnp.inf), axis=1, keepdims=True), axis=0, keepdims=True)
        row = lax.broadcasted_iota(jnp.int32, (8, LANES), 0)
        stats_ref[...] = jnp.where(row == 0, longest(qa), jnp.where(row == 1, longest(ka),
                                                                    jnp.where(row == 2, top, low)))

    rows = jax.ShapeDtypeStruct((FOX_HEADS, nb, t, LANES), MXU_DTYPE)
    cols = jax.ShapeDtypeStruct((FOX_HEADS, nb, LANES, t), MXU_DTYPE)
    rspec = pl.BlockSpec((None, None, t, LANES), lambda i, h: (h, i, 0, 0))
    cspec = pl.BlockSpec((None, None, LANES, t), lambda i, h: (h, i, 0, 0))
    npair = FOX_HEADS // 2
    res = pl.pallas_call(
        _carrying(body, 8, 7, 3, carry, (nb, FOX_HEADS)), name="fox_prep", grid=(nb, FOX_HEADS),
        in_specs=[pl.BlockSpec((t, LANES), lambda i, h: (i, h // 2)),
                  pl.BlockSpec((t, LANES), lambda i, h: (i, npair + h // 2)),
                  pl.BlockSpec((t, LANES), lambda i, h: (i, 2 * npair + h // 2)),
                  pl.BlockSpec((t, LANES), lambda i, h: (i, 0)),
                  pl.BlockSpec((8, LANES), lambda i, h: (0, 0)),
                  pl.BlockSpec((None, 512, LANES), lambda i, h: (h, 0, 0)),
                  pl.BlockSpec((None, 512, LANES), lambda i, h: (h, 0, 0)),
                  pl.BlockSpec((None, LANES, LANES), lambda i, h: (h % 2, 0, 0))] + [anywhere] * len(carry),
        out_specs=[rspec, cspec, rspec, cspec, rspec, cspec,
                   pl.BlockSpec((None, None, 8, LANES), lambda i, h: (h, i, 0, 0))] + [anywhere] * len(carry),
        out_shape=[rows, cols, rows, cols, rows, cols, jax.ShapeDtypeStruct((FOX_HEADS, nb, 8, LANES), F32)]
        + _exchange_shapes(carry),
        scratch_shapes=[pltpu.VMEM((t, 3 * LANES), MXU_DTYPE), pltpu.VMEM((8, LANES), F32),
                        pltpu.VMEM((t, LANES), F32)] + (_exchange_sems(carry) if carry else []),
        compiler_params=_params(VMEM_BIG),
    )(qkv, qkv, qkv, f, bf, _mx(jnp.asarray(selq)), _mx(jnp.asarray(selk)), _mx(jnp.asarray(selv)),
      *[arr for _, arr in carry])
    return (*res[:7], list(res[7:]))


def _fox_active(stats):
    qn, kn, top, low = (stats[:, :, r, 0] for r in range(4))
    nb = qn.shape[1]
    gap = qn[:, :, None] * kn[:, None, :] + top[:, :, None] - low[:, None, :] + (qn * kn)[:, :, None]
    keep = (gap > -SKIP_MARGIN) | jnp.eye(nb, dtype=bool)[None]
    return jnp.where(keep, 1.0, 0.0).astype(F32).reshape(qn.shape[0], nb * nb)


def _causal_allow(t):
    return lax.broadcasted_iota(jnp.int32, (t, t), 0) <= lax.broadcasted_iota(jnp.int32, (t, t), 1)


def _fox_attn_fwd(qat, ka, vat, active):
    heads, nb, _, t = qat.shape

    def body(act_ref, qat_ref, ka_ref, vat_ref, ot_ref, lse_ref, acc_ref, m_ref, kbuf_ref, vbuf_ref, fsems):
        h, i = pl.program_id(0), pl.program_id(1)
        m_ref[...] = jnp.full_like(m_ref, -jnp.inf)
        acc_ref[...] = jnp.zeros_like(acc_ref)

        def key_blocks(j, slot):
            return [pltpu.make_async_copy(ka_ref.at[h, j], kbuf_ref.at[slot], fsems.at[0, slot]),
                    pltpu.make_async_copy(vat_ref.at[h, j], vbuf_ref.at[slot], fsems.at[1, slot])]

        def runs(j):
            return (j == i) | (act_ref[h, i * nb + j] > 0.5)

        def step(slot, diagonal):
            st = _dot(kbuf_ref[slot], qat_ref[...])
            if diagonal:
                st = jnp.where(_causal_allow(t), st, -jnp.inf)
            m_old = m_ref[...]
            m_new = jnp.maximum(m_old, jnp.max(st, axis=0, keepdims=True))
            pt = jnp.exp(st - m_new)
            acc_ref[...] = acc_ref[...] * jnp.exp(m_old - m_new) + _dot(vbuf_ref[slot], _mx(pt))
            m_ref[...] = m_new

        @pl.when(runs(0))
        def _():
            for cp in key_blocks(0, 0):
                cp.start()

        def earlier(j, c):
            slot = j % 2

            @pl.when(runs(j + 1))
            def _():
                for cp in key_blocks(j + 1, 1 - slot):
                    cp.start()

            @pl.when(runs(j))
            def _():
                for cp in key_blocks(j, slot):
                    cp.wait()
                step(slot, False)

            return c

        lax.fori_loop(0, i, earlier, 0)
        for cp in key_blocks(i, i % 2):
            cp.wait()
        step(i % 2, True)
        acc = acc_ref[...]
        denom = acc[ONES_V:ONES_V + 1, :]
        ot_ref[...] = _mx(acc / denom)
        lse_ref[...] = m_ref[...] + jnp.log(denom)

    anywhere = pl.BlockSpec(memory_space=pl.ANY)
    qspec = pl.BlockSpec((None, None, LANES, t), lambda h, i: (h, i, 0, 0))
    return pl.pallas_call(
        body, name="fox_attn_fwd", grid=(heads, nb),
        in_specs=[pl.BlockSpec(memory_space=pltpu.SMEM), qspec, anywhere, anywhere],
        out_specs=[qspec, pl.BlockSpec((None, None, 1, t), lambda h, i: (h, i, 0, 0))],
        out_shape=[jax.ShapeDtypeStruct((heads, nb, LANES, t), MXU_DTYPE),
                   jax.ShapeDtypeStruct((heads, nb, 1, t), F32)],
        scratch_shapes=[pltpu.VMEM((LANES, t), F32), pltpu.VMEM((1, t), F32), pltpu.VMEM((2, t, LANES), MXU_DTYPE),
                        pltpu.VMEM((2, LANES, t), MXU_DTYPE), pltpu.SemaphoreType.DMA((2, 2))],
        compiler_params=_params(VMEM_BIG),
    )(active, qat, ka, vat)


def _fox_attn_bwd(qa, qat, ka, kat, va, ot, lse, do, dot_, active):
    heads, nb, t, _ = qa.shape

    def body(act_ref, qa_ref, qat_ref, ka_ref, kat_ref, va_ref, ot_ref, lse_ref, do_ref, dot_ref,
             dqt_ref, dka_ref, dva_ref, rows_ref, cols_ref, lseb_ref, fsems):
        h, j = pl.program_id(0), pl.program_id(1)

        @pl.when(j == 0)
        def _():
            dqt_ref[...] = jnp.zeros_like(dqt_ref)

        def query_blocks(i, slot):
            cps = [pltpu.make_async_copy(src.at[h, i], rows_ref.at[slot, n], fsems.at[n, slot])
                   for n, src in enumerate((qa_ref, do_ref))]
            cps += [pltpu.make_async_copy(src.at[h, i], cols_ref.at[slot, n], fsems.at[2 + n, slot])
                    for n, src in enumerate((qat_ref, ot_ref, dot_ref))]
            return cps + [pltpu.make_async_copy(lse_ref.at[h, i], lseb_ref.at[slot], fsems.at[5, slot])]

        def runs(i):
            return (i == j) | (act_ref[h, i * nb + j] > 0.5)

        def step(i, slot, diagonal):
            st = _dot(ka_ref[...], cols_ref[slot, 0])
            dot_v = cols_ref[slot, 2]
            delta = jnp.sum(cols_ref[slot, 1].astype(F32) * dot_v.astype(F32), axis=0, keepdims=True)
            pt = jnp.exp(st - lseb_ref[slot])
            if diagonal:
                pt = jnp.where(_causal_allow(t), pt, 0.0)
            dsm = _mx(pt * (_dot(va_ref[...], dot_v) - delta))
            upd_v = _dot(_mx(pt), rows_ref[slot, 1])
            upd_k = _dot(dsm, rows_ref[slot, 0])
            if diagonal:
                dva_ref[...] = upd_v
                dka_ref[...] = upd_k
            else:
                dva_ref[...] += upd_v
                dka_ref[...] += upd_k
            dqt_ref[i] += _dot(kat_ref[...], dsm)

        def visit(i, slot, diagonal):
            nxt = jnp.minimum(i + 1, nb - 1)

            @pl.when((i + 1 < nb) & runs(nxt))
            def _():
                for cp in query_blocks(nxt, 1 - slot):
                    cp.start()

            @pl.when(runs(i))
            def _():
                for cp in query_blocks(i, slot):
                    cp.wait()
                step(i, slot, diagonal)

        for cp in query_blocks(j, 0):
            cp.start()
        visit(j, 0, True)

        def later(i, c):
            visit(i, (i - j) % 2, False)
            return c

        lax.fori_loop(j + 1, nb, later, 0)

    def at_k(shape):
        return pl.BlockSpec((None, None) + shape, lambda h, j: (h, j, 0, 0))

    anywhere = pl.BlockSpec(memory_space=pl.ANY)
    return pl.pallas_call(
        body, name="fox_attn_bwd", grid=(heads, nb),
        in_specs=[pl.BlockSpec(memory_space=pltpu.SMEM),
                  anywhere, anywhere, at_k((t, LANES)), at_k((LANES, t)), at_k((t, LANES)),
                  anywhere, anywhere, anywhere, anywhere],
        out_specs=[pl.BlockSpec((None, nb, LANES, t), lambda h, j: (h, 0, 0, 0)), at_k((t, LANES)), at_k((t, LANES))],
        out_shape=[jax.ShapeDtypeStruct((heads, nb, LANES, t), F32),
                   jax.ShapeDtypeStruct((heads, nb, t, LANES), F32),
                   jax.ShapeDtypeStruct((heads, nb, t, LANES), F32)],
        scratch_shapes=[pltpu.VMEM((2, 2, t, LANES), MXU_DTYPE), pltpu.VMEM((2, 3, LANES, t), MXU_DTYPE),
                        pltpu.VMEM((2, 1, t), F32), pltpu.SemaphoreType.DMA((6, 2))],
        compiler_params=_params(VMEM_BIG),
    )(active, qa, qat, ka, kat, va, ot, lse, do, dot_)


def _fox_post(dqt, dka, dva, f, bf, carry=()):
    heads, nb, t, _ = dka.shape
    s = nb * t
    _, _, _, put = _fox_constants()
    carry = list(carry)
    anywhere = pl.BlockSpec(memory_space=pl.ANY)

    def body(dqt_ref, dka_ref, dva_ref, f_ref, bf_ref, put_ref, dq_ref, dk_ref, dv_ref, df_ref, sums_ref,
             dc_ref, carry_ref):
        i, h = pl.program_id(0), pl.program_id(1)

        @pl.when((i == 0) & (h == 0))
        def _():
            carry_ref[...] = jnp.zeros_like(carry_ref)
            sums_ref[...] = jnp.zeros_like(sums_ref)

        @pl.when(h == 0)
        def _():
            dc_ref[...] = jnp.zeros_like(dc_ref)

        dqt_v = dqt_ref[...]
        dka_v = dka_ref[...]
        term_q = _dot(_mx(dqt_v), put_ref[0], TN)
        term_k = _dot(_mx(dka_v), put_ref[1])
        term_v = _dot(_mx(dva_ref[...]), put_ref[1])

        @pl.when(h % 2 == 0)
        def _():
            dq_ref[...] = _mx(term_q)
            dk_ref[...] = _mx(term_k)
            dv_ref[...] = _mx(term_v)

        @pl.when(h % 2 == 1)
        def _():
            dq_ref[...] += _mx(term_q)
            dk_ref[...] += _mx(term_k)
            dv_ref[...] += _mx(term_v)

        dcum = dqt_v[BIAS_Q[0]:BIAS_Q[0] + 1, :] - dka_v.T[BIAS_K[0]:BIAS_K[0] + 1, :]
        head_row = lax.broadcasted_iota(jnp.int32, (heads, 1), 0) == h
        dc_ref[...] += jnp.where(head_row, dcum, 0.0)

        @pl.when(h == heads - 1)
        def _():
            later = (lax.broadcasted_iota(jnp.int32, (t, t), 0) >= lax.broadcasted_iota(jnp.int32, (t, t), 1)).astype(F32)
            dlf_t = _dot_onehot(dc_ref[...], later) + carry_ref[:, 0:1]
            carry_ref[...] = jnp.broadcast_to(dlf_t[:, 0:1], carry_ref.shape)
            dlf = jnp.concatenate([dlf_t, jnp.zeros((LANES - heads, t), F32)], axis=0).T
            lane = lax.broadcasted_iota(jnp.int32, (1, LANES), 1)
            df = jnp.where(lane < heads, dlf * jax.nn.sigmoid(-(f_ref[...] + bf_ref[0:1, :])), 0.0)
            df_ref[...] = _mx(df)
            sums_ref[0:1, :] += jnp.sum(df, axis=0, keepdims=True)

    rev = lambda i: nb - 1 - i
    pair_spec = pl.BlockSpec((t, LANES), lambda i, h: (rev(i), h // 2))
    blk = pl.BlockSpec((t, LANES), lambda i, h: (rev(i), 0))
    hd = jax.ShapeDtypeStruct((s, D_MODEL), MXU_DTYPE)
    res = pl.pallas_call(
        _carrying(body, 6, 5, 2, carry, (nb, heads)), name="fox_post", grid=(nb, heads),
        in_specs=[pl.BlockSpec((None, None, LANES, t), lambda i, h: (h, rev(i), 0, 0)),
                  pl.BlockSpec((None, None, t, LANES), lambda i, h: (h, rev(i), 0, 0)),
                  pl.BlockSpec((None, None, t, LANES), lambda i, h: (h, rev(i), 0, 0)),
                  blk, pl.BlockSpec((8, LANES), lambda i, h: (0, 0)),
                  pl.BlockSpec((None, 2, LANES, LANES), lambda i, h: (h % 2, 0, 0, 0))] + [anywhere] * len(carry),
        out_specs=[pair_spec, pair_spec, pair_spec, blk, pl.BlockSpec((8, LANES), lambda i, h: (0, 0))]
        + [anywhere] * len(carry),
        out_shape=[hd, hd, hd, jax.ShapeDtypeStruct((s, LANES), MXU_DTYPE), jax.ShapeDtypeStruct((8, LANES), F32)]
        + _exchange_shapes(carry),
        scratch_shapes=[pltpu.VMEM((heads, t), F32), pltpu.VMEM((heads, LANES), F32)]
        + (_exchange_sems(carry) if carry else []),
        compiler_params=_params(VMEM_BIG),
    )(dqt, dka, dva, f, bf, _mx(jnp.asarray(put)), *[arr for _, arr in carry])
    return (*res[:5], list(res[5:]))


def _fox_weights(w_in, w_o):
    wqkv = w_in[:, :3 * D_MODEL]
    wf = jnp.pad(w_in[:, 3 * D_MODEL:], ((0, 0), (0, LANES - FOX_HEADS)))
    wo_heads = w_o.reshape(FOX_HEADS, FOX_HEAD_DIM, D_MODEL)
    wo_a = jnp.pad(wo_heads, ((0, 0), (0, LANES - FOX_HEAD_DIM), (0, 0)))
    wo_rows = wo_a.reshape(FOX_HEADS * LANES, D_MODEL)
    return dict(wqkv=wqkv, wf=wf, wqkv_t=wqkv.T, wf_t=wf.T, wo_rows=wo_rows, wo_rows_t=wo_rows.T)


def _fox_out(ot, wo_rows):
    heads, nb, _, t = ot.shape

    def body(ot_ref, w_ref, y_ref):
        y_ref[...] = _dot(ot_ref[...].reshape(heads * LANES, t), w_ref[...], TN)

    return pl.pallas_call(
        body, name="fox_out", grid=(nb,),
        in_specs=[pl.BlockSpec((heads, None, LANES, t), lambda i: (0, i, 0, 0)),
                  pl.BlockSpec((heads * LANES, D_MODEL), lambda i: (0, 0))],
        out_specs=pl.BlockSpec((t, D_MODEL), lambda i: (i, 0)),
        out_shape=jax.ShapeDtypeStruct((nb * t, D_MODEL), F32),
        compiler_params=_params(VMEM_BIG),
    )(ot, wo_rows)


def _fox_do(dy, wo_rows_t, nb, t):
    heads = FOX_HEADS

    def body(dy_ref, w_ref, do_ref, dot_ref):
        val = _dot(dy_ref[...], w_ref[...])
        for h in range(heads):
            blk = val[:, h * LANES:(h + 1) * LANES]
            do_ref[h] = _mx(blk)
            dot_ref[h] = _mx(blk.T)

    return pl.pallas_call(
        body, name="fox_do", grid=(nb,),
        in_specs=[pl.BlockSpec((t, D_MODEL), lambda i: (i, 0)),
                  pl.BlockSpec((D_MODEL, heads * LANES), lambda i: (0, 0))],
        out_specs=[pl.BlockSpec((heads, None, t, LANES), lambda i: (0, i, 0, 0)),
                   pl.BlockSpec((heads, None, LANES, t), lambda i: (0, i, 0, 0))],
        out_shape=[jax.ShapeDtypeStruct((heads, nb, t, LANES), MXU_DTYPE),
                   jax.ShapeDtypeStruct((heads, nb, LANES, t), MXU_DTYPE)],
        compiler_params=_params(VMEM_BIG),
    )(dy, wo_rows_t)


def _fox_dwo(ot, dy):
    heads, nb, _, t = ot.shape

    def body(ot_ref, dy_ref, o_ref):
        part = _dot(ot_ref[...].reshape(heads * LANES, t), dy_ref[...])

        @pl.when(pl.program_id(0) == 0)
        def _():
            o_ref[...] = part

        @pl.when(pl.program_id(0) > 0)
        def _():
            o_ref[...] += part

    return pl.pallas_call(
        body, name="fox_dwo", grid=(nb,),
        in_specs=[pl.BlockSpec((heads, None, LANES, t), lambda i: (0, i, 0, 0)),
                  pl.BlockSpec((t, D_MODEL), lambda i: (i, 0))],
        out_specs=pl.BlockSpec((heads * LANES, D_MODEL), lambda i: (0, 0)),
        out_shape=jax.ShapeDtypeStruct((heads * LANES, D_MODEL), F32),
        compiler_params=_params(VMEM_BIG),
    )(ot, dy)


def _fox_fwd(u, w, bf, carry=()):
    qkv = _mm_nn("fox_qkv", u, w["wqkv"], MXU_DTYPE)
    f = _mm_nn("fox_f", u, w["wf"], F32)
    qa, qat, ka, kat, va, vat, stats, carried = _fox_prep(qkv, f, bf, carry)
    ot, lse = _fox_attn_fwd(qat, ka, vat, _fox_active(stats))
    y = _fox_out(ot, w["wo_rows"])
    return y, dict(f=f, qa=qa, qat=qat, ka=ka, kat=kat, va=va, ot=ot, lse=lse, stats=stats), carried


def _fox_bwd(dy, u, w, bf, res, carry=()):
    heads, nb, t, _ = res["qa"].shape
    do, dot_ = _fox_do(dy, w["wo_rows_t"], nb, t)
    dwo_a = _fox_dwo(res["ot"], dy).reshape(heads, LANES, D_MODEL)
    dqt, dka, dva = _fox_attn_bwd(res["qa"], res["qat"], res["ka"], res["kat"], res["va"], res["ot"],
                                  res["lse"], do, dot_, _fox_active(res["stats"]))
    dq, dk, dv, df, sums, carried = _fox_post(dqt, dka, dva, res["f"], bf, carry)
    dw_in = jnp.concatenate(
        [_mm_tn("fox_dw_q", u, dq), _mm_tn("fox_dw_k", u, dk), _mm_tn("fox_dw_v", u, dv),
         _mm_tn("fox_dw_f", u, df)[:, :FOX_HEADS]], axis=1)
    dw_o = dwo_a[:, :FOX_HEAD_DIM, :].reshape(D_MODEL, D_MODEL)
    wt = w["wqkv_t"]
    du, ex_in, ex_o = _mm_sum(
        "fox_du", [(dq, wt[:D_MODEL]), (dk, wt[D_MODEL:2 * D_MODEL]), (dv, wt[2 * D_MODEL:]), (df, w["wf_t"])],
        carry=[(True, _col_slots(dw_in[None])), (True, _row_slots(dw_o[None]))])
    return du, sums, carried + [ex_in, ex_o]


def _dsilu(v):
    sg = jax.nn.sigmoid(v)
    return sg * (1.0 + v * (1.0 - sg))


def _conv_taps(scr_ref, w_ref, rows, base):
    acc = None
    for k in range(SSM_CONV):
        term = scr_ref[pl.ds(base - (SSM_CONV - 1) + k, rows), :] * w_ref[k:k + 1, :]
        acc = term if acc is None else acc + term
    return acc


def _conv_fwd(zx, cw, cb):
    s = zx.shape[0]
    tb = _tile(s, 512)
    half = SSM_CONV_DIM // 2
    hb = tb // SUBLANES

    def body(x_ref, halo_ref, w_ref, b_ref, o_ref, scr_ref):
        i = pl.program_id(0)
        scr_ref[pl.ds(0, SUBLANES), :] = jnp.where(i > 0, halo_ref[...], 0.0)
        scr_ref[pl.ds(SUBLANES, tb), :] = x_ref[...]
        o_ref[...] = jax.nn.silu(_conv_taps(scr_ref, w_ref, tb, SUBLANES) + b_ref[0:1, :])

    return pl.pallas_call(
        body, name="ssd_conv_fwd", grid=(s // tb, 2),
        in_specs=[pl.BlockSpec((tb, half), lambda i, j: (i, 1 + j)),
                  pl.BlockSpec((SUBLANES, half), lambda i, j: (jnp.maximum(i * hb - 1, 0), 1 + j)),
                  pl.BlockSpec((8, half), lambda i, j: (0, j)),
                  pl.BlockSpec((8, half), lambda i, j: (0, j))],
        out_specs=pl.BlockSpec((tb, half), lambda i, j: (i, j)),
        out_shape=jax.ShapeDtypeStruct((s, SSM_CONV_DIM), F32),
        scratch_shapes=[pltpu.VMEM((tb + SUBLANES, half), F32)],
    )(zx, zx, cw, cb)


def _conv_bwd_pre(zx, dxc, cw, cb):
    s = zx.shape[0]
    tb = _tile(s, 512)
    half = SSM_CONV_DIM // 2
    hb = tb // SUBLANES

    def body(x_ref, halo_ref, d_ref, w_ref, b_ref, o_ref, sums_ref, scr_ref):
        i = pl.program_id(1)

        @pl.when(i == 0)
        def _():
            sums_ref[...] = jnp.zeros_like(sums_ref)

        scr_ref[pl.ds(0, SUBLANES), :] = jnp.where(i > 0, halo_ref[...], 0.0)
        scr_ref[pl.ds(SUBLANES, tb), :] = x_ref[...]
        pre = _conv_taps(scr_ref, w_ref, tb, SUBLANES) + b_ref[0:1, :]
        dpre = d_ref[...] * _dsilu(pre)
        o_ref[...] = dpre
        for k in range(SSM_CONV):
            shifted = scr_ref[pl.ds(SUBLANES - (SSM_CONV - 1) + k, tb), :]
            sums_ref[k:k + 1, :] += jnp.sum(dpre * shifted, axis=0, keepdims=True)
        sums_ref[SSM_CONV:SSM_CONV + 1, :] += jnp.sum(dpre, axis=0, keepdims=True)

    return pl.pallas_call(
        body, name="ssd_conv_bwd_pre", grid=(2, s // tb),
        in_specs=[pl.BlockSpec((tb, half), lambda j, i: (i, 1 + j)),
                  pl.BlockSpec((SUBLANES, half), lambda j, i: (jnp.maximum(i * hb - 1, 0), 1 + j)),
                  pl.BlockSpec((tb, half), lambda j, i: (i, j)),
                  pl.BlockSpec((8, half), lambda j, i: (0, j)),
                  pl.BlockSpec((8, half), lambda j, i: (0, j))],
        out_specs=[pl.BlockSpec((tb, half), lambda j, i: (i, j)),
                   pl.BlockSpec((8, half), lambda j, i: (0, j))],
        out_shape=[jax.ShapeDtypeStruct((s, SSM_CONV_DIM), F32), jax.ShapeDtypeStruct((8, SSM_CONV_DIM), F32)],
        scratch_shapes=[pltpu.VMEM((tb + SUBLANES, half), F32)],
    )(zx, zx, dxc, cw, cb)


def _conv_bwd_x(dpre, cw):
    s = dpre.shape[0]
    tb = _tile(s, 512)
    hb = tb // SUBLANES
    nb = s // tb

    def body(d_ref, halo_ref, w_ref, o_ref, scr_ref):
        i = pl.program_id(0)
        scr_ref[pl.ds(0, tb), :] = d_ref[...]
        scr_ref[pl.ds(tb, SUBLANES), :] = jnp.where(i < nb - 1, halo_ref[...], 0.0)
        acc = None
        for k in range(SSM_CONV):
            term = scr_ref[pl.ds(SSM_CONV - 1 - k, tb), :] * w_ref[k:k + 1, :]
            acc = term if acc is None else acc + term
        o_ref[...] = _mx(acc)

    return pl.pallas_call(
        body, name="ssd_conv_bwd_x", grid=(nb,),
        in_specs=[pl.BlockSpec((tb, SSM_CONV_DIM), lambda i: (i, 0)),
                  pl.BlockSpec((SUBLANES, SSM_CONV_DIM), lambda i: (jnp.minimum((i + 1) * hb, s // SUBLANES - 1), 0)),
                  pl.BlockSpec((8, SSM_CONV_DIM), lambda i: (0, 0))],
        out_specs=pl.BlockSpec((tb, SSM_CONV_DIM), lambda i: (i, 0)),
        out_shape=jax.ShapeDtypeStruct((s, SSM_CONV_DIM), MXU_DTYPE),
        scratch_shapes=[pltpu.VMEM((tb + SUBLANES, SSM_CONV_DIM), F32)],
        compiler_params=_params(VMEM_BIG),
    )(dpre, dpre, cw)


def _expand_constants():
    ex = np.zeros((LANES, SSM_D_INNER), np.float32)
    for h in range(SSM_HEADS):
        ex[h, h * 64:(h + 1) * 64] = 1.0
    return ex, np.ascontiguousarray(ex.T)


def _ssd_common(dtr_ref, par_ref, ex_ref, xc_ref):
    lc = SSM_CHUNK
    lane = lax.broadcasted_iota(jnp.int32, (1, LANES), 1)
    is_head = lane < SSM_HEADS
    par = par_ref[...]
    pre = dtr_ref[...] + par[0:1, :]
    dt = jnp.where(is_head, jax.nn.softplus(pre), 0.0)
    a = jnp.where(is_head, -jnp.exp(par[1:2, :]), 0.0)
    tri_b = lax.broadcasted_iota(jnp.int32, (lc, lc), 0) >= lax.broadcasted_iota(jnp.int32, (lc, lc), 1)
    tri = tri_b.astype(F32)
    da = dt * a
    acs = _dot_onehot(tri, da, onehot="a")
    acs_t = _dot_onehot(da, tri, (((0,), (1,)), ((), ())))
    wide = _dot_onehot(jnp.concatenate([dt, acs, par], axis=0), ex_ref[...])
    dt_x, acs_x, d_x = wide[0:lc], wide[lc:2 * lc], wide[2 * lc + 2:2 * lc + 3]
    last_x = acs_x[lc - 1:lc, :]
    xs = xc_ref[:, 0:SSM_D_INNER]
    return dict(pre=pre, dt=dt, a=a, tri_b=tri_b, tri=tri, acs=acs, acs_t=acs_t, dt_x=dt_x, d_x=d_x, xs=xs,
                xdt=xs * dt_x, e_x=jnp.exp(acs_x), dte_x=jnp.exp(last_x - acs_x), cd_x=jnp.exp(last_x),
                is_head=is_head)


def _decay_in(q, h):
    seg = q["acs"][:, h:h + 1] - q["acs_t"][h:h + 1, :]
    return jnp.exp(jnp.where(q["tri_b"], seg, -jnp.inf))


def _ssd_scan_fwd(xc, dtr, par):
    s = xc.shape[0]
    lc = SSM_CHUNK
    nc = s // lc
    ex, _ = _expand_constants()

    def body(xc_ref, dtr_ref, par_ref, ex_ref, y_ref, prev_ref, st_ref):
        @pl.when(pl.program_id(0) == 0)
        def _():
            st_ref[...] = jnp.zeros_like(st_ref)

        q = _ssd_common(dtr_ref, par_ref, ex_ref, xc_ref)
        lane = lax.broadcasted_iota(jnp.int32, (1, LANES), 1)
        for g in range(SSM_GROUPS):
            sl = slice(g * GROUP_W, (g + 1) * GROUP_W)
            bg = _mx(xc_ref[:, SSM_D_INNER + g * SSM_STATE:SSM_D_INNER + (g + 1) * SSM_STATE])
            cg = _mx(xc_ref[:, SSM_D_INNER + (SSM_GROUPS + g) * SSM_STATE:SSM_D_INNER + (SSM_GROUPS + g + 1) * SSM_STATE])
            gm = _dot(cg, bg, NT)
            prev = st_ref[g]
            prev_ref[g] = prev
            yoff = _dot(cg, _mx(prev)) * q["e_x"][:, sl]
            st_ref[g] = prev * q["cd_x"][:, sl] + _dot(bg, _mx(q["xdt"][:, sl] * q["dte_x"][:, sl]), TN)
            pairs = []
            for pr in range(2):
                xp = _mx(q["xdt"][:, g * GROUP_W + pr * LANES:g * GROUP_W + (pr + 1) * LANES])
                both = [_dot(_mx(gm * _decay_in(q, 4 * g + 2 * pr + r2)), xp) for r2 in range(2)]
                pairs.append(jnp.where(lane < 64, both[0], both[1]))
            y_ref[:, sl] = jnp.concatenate(pairs, axis=1) + yoff + q["xs"][:, sl] * q["d_x"][:, sl]

    return pl.pallas_call(
        body, name="ssd_scan_fwd", grid=(nc,),
        in_specs=[pl.BlockSpec((lc, SSM_CONV_DIM), lambda c: (c, 0)),
                  pl.BlockSpec((lc, LANES), lambda c: (c, 0)),
                  pl.BlockSpec((8, LANES), lambda c: (0, 0)),
                  pl.BlockSpec((LANES, SSM_D_INNER), lambda c: (0, 0))],
        out_specs=[pl.BlockSpec((lc, SSM_D_INNER), lambda c: (c, 0)),
                   pl.BlockSpec((None, SSM_GROUPS, SSM_STATE, GROUP_W), lambda c: (c, 0, 0, 0))],
        out_shape=[jax.ShapeDtypeStruct((s, SSM_D_INNER), F32),
                   jax.ShapeDtypeStruct((nc, SSM_GROUPS, SSM_STATE, GROUP_W), F32)],
        scratch_shapes=[pltpu.VMEM((SSM_GROUPS, SSM_STATE, GROUP_W), F32)],
        compiler_params=_params(VMEM_BIG),
    )(xc, dtr, par, _mx(jnp.asarray(ex)))


def _ssd_scan_bwd(dy, xc, dtr, par, prev, carry=()):
    s = xc.shape[0]
    lc = SSM_CHUNK
    nc = s // lc
    ex, ex_t = _expand_constants()
    carry = list(carry)
    anywhere = pl.BlockSpec(memory_space=pl.ANY)

    def body(dy_ref, xc_ref, dtr_ref, par_ref, prev_ref, ex_ref, ext_ref, dxc_ref, ddtr_ref, sums_ref,
             gst_ref, tacs_ref, tdt_ref, tdd_ref):
        @pl.when(pl.program_id(0) == 0)
        def _():
            gst_ref[...] = jnp.zeros_like(gst_ref)
            sums_ref[...] = jnp.zeros_like(sums_ref)

        q = _ssd_common(dtr_ref, par_ref, ex_ref, xc_ref)
        lane = lax.broadcasted_iota(jnp.int32, (1, LANES), 1)
        row = lax.broadcasted_iota(jnp.int32, (lc, 1), 0)
        dacs_rows = jnp.zeros((lc, LANES), F32)
        dacs_cols_t = jnp.zeros((LANES, lc), F32)
        for g in range(SSM_GROUPS):
            sl = slice(g * GROUP_W, (g + 1) * GROUP_W)
            b_lo = SSM_D_INNER + g * SSM_STATE
            c_lo = SSM_D_INNER + (SSM_GROUPS + g) * SSM_STATE
            bg = _mx(xc_ref[:, b_lo:b_lo + SSM_STATE])
            cg = _mx(xc_ref[:, c_lo:c_lo + SSM_STATE])
            dyg = dy_ref[:, sl]
            xsg, xdtg = q["xs"][:, sl], q["xdt"][:, sl]
            eg, dteg, cdg = q["e_x"][:, sl], q["dte_x"][:, sl], q["cd_x"][:, sl]
            prevg = prev_ref[g]
            gs = gst_ref[g]
            prevm, gsm = _mx(prevg), _mx(gs)
            tdd_ref[:, sl] = dyg * xsg
            dxs = dyg * q["d_x"][:, sl]
            t_acs = dyg * _dot(cg, prevm) * eg
            dcp = _mx(dyg * eg)
            dc = _dot(dcp, prevm, NT)
            dprev = _dot(cg, dcp, TN)
            db = _dot(_mx(xdtg * dteg), gsm, NT)
            dx2 = _dot(bg, gsm)
            dxdt = dx2 * dteg
            ddte = dx2 * xdtg * dteg
            t_acs = t_acs - ddte
            last = (jnp.sum(ddte, axis=0, keepdims=True)
                    + jnp.sum(gs * prevg, axis=0, keepdims=True) * cdg)
            gm = _dot(cg, bg, NT)
            dgm = jnp.zeros((lc, lc), F32)
            pair_dx = []
            for pr in range(2):
                lo = g * GROUP_W + pr * LANES
                xp = _mx(q["xdt"][:, lo:lo + LANES])
                dyp = dy_ref[:, lo:lo + LANES]
                both = []
                for r2 in range(2):
                    h = 4 * g + 2 * pr + r2
                    mine = (lane >= 64 * r2) & (lane < 64 * (r2 + 1))
                    lm = _decay_in(q, h)
                    m = gm * lm
                    dm = _dot(_mx(jnp.where(mine, dyp, 0.0)), xp, NT)
                    dgm = dgm + dm * lm
                    w = dm * m
                    dacs_rows = dacs_rows + jnp.sum(w, axis=1, keepdims=True) * (lane == h).astype(F32)
                    head_row = (lax.broadcasted_iota(jnp.int32, (LANES, 1), 0) == h).astype(F32)
                    dacs_cols_t = dacs_cols_t + head_row * jnp.sum(w, axis=0, keepdims=True)
                    both.append(_dot(_mx(m), _mx(dyp), TN))
                pair_dx.append(jnp.where(lane < 64, both[0], both[1]))
            dxdt = dxdt + jnp.concatenate(pair_dx, axis=1)
            dgmm = _mx(dgm)
            dc = dc + _dot(dgmm, bg)
            db = db + _dot(dgmm, cg, TN)
            dxs = dxs + dxdt * q["dt_x"][:, sl]
            tdt_ref[:, sl] = dxdt * xsg
            tacs_ref[:, sl] = t_acs + jnp.where(row == lc - 1, last, 0.0)
            dxc_ref[:, sl] = dxs
            dxc_ref[:, b_lo:b_lo + SSM_STATE] = db
            dxc_ref[:, c_lo:c_lo + SSM_STATE] = dc
            gst_ref[g] = gs * cdg + dprev
        tdd = jnp.broadcast_to(jnp.sum(tdd_ref[...], axis=0, keepdims=True), (8, SSM_D_INNER))
        heads_of = _dot_onehot(jnp.concatenate([tacs_ref[...], tdt_ref[...], tdd], axis=0), ext_ref[...])
        dacs = heads_of[0:lc] + dacs_rows - dacs_cols_t.T
        dda = _dot_onehot(q["tri"], dacs, TN, onehot="a")
        ddt = dda * q["a"] + heads_of[lc:2 * lc]
        ddtr = jnp.where(q["is_head"], ddt * jax.nn.sigmoid(q["pre"]), 0.0)
        ddtr_ref[...] = _mx(ddtr)
        sums_ref[0:1, :] += jnp.sum(ddtr, axis=0, keepdims=True)
        sums_ref[1:2, :] += jnp.sum(dda * q["dt"], axis=0, keepdims=True) * q["a"]
        sums_ref[2:3, :] += heads_of[2 * lc:2 * lc + 1]

    rev = lambda c: nc - 1 - c
    wide = pltpu.VMEM((lc, SSM_D_INNER), F32)
    res = pl.pallas_call(
        _carrying(body, 7, 3, 4, carry, (nc,)), name="ssd_scan_bwd", grid=(nc,),
        in_specs=[pl.BlockSpec((lc, SSM_D_INNER), lambda c: (rev(c), 0)),
                  pl.BlockSpec((lc, SSM_CONV_DIM), lambda c: (rev(c), 0)),
                  pl.BlockSpec((lc, LANES), lambda c: (rev(c), 0)),
                  pl.BlockSpec((8, LANES), lambda c: (0, 0)),
                  pl.BlockSpec((None, SSM_GROUPS, SSM_STATE, GROUP_W), lambda c: (rev(c), 0, 0, 0)),
                  pl.BlockSpec((LANES, SSM_D_INNER), lambda c: (0, 0)),
                  pl.BlockSpec((SSM_D_INNER, LANES), lambda c: (0, 0))] + [anywhere] * len(carry),
        out_specs=[pl.BlockSpec((lc, SSM_CONV_DIM), lambda c: (rev(c), 0)),
                   pl.BlockSpec((lc, LANES), lambda c: (rev(c), 0)),
                   pl.BlockSpec((8, LANES), lambda c: (0, 0))] + [anywhere] * len(carry),
        out_shape=[jax.ShapeDtypeStruct((s, SSM_CONV_DIM), F32), jax.ShapeDtypeStruct((s, LANES), MXU_DTYPE),
                   jax.ShapeDtypeStruct((8, LANES), F32)] + _exchange_shapes(carry),
        scratch_shapes=[pltpu.VMEM((SSM_GROUPS, SSM_STATE, GROUP_W), F32), wide, wide, wide]
        + (_exchange_sems(carry) if carry else []),
        compiler_params=_params(VMEM_BIG),
    )(dy, xc, dtr, par, prev, _mx(jnp.asarray(ex)), _mx(jnp.asarray(ex_t)), *[arr for _, arr in carry])
    return (*res[:3], list(res[3:]))


def _group_norm_parts(yv, zv):
    yg = yv * jax.nn.silu(zv)
    normed, rinvs = [], []
    for g in range(SSM_GROUPS):
        blk = yg[:, g * GROUP_W:(g + 1) * GROUP_W]
        rinv = lax.rsqrt(jnp.mean(blk * blk, axis=-1, keepdims=True) + RMS_EPS)
        normed.append(blk * rinv)
        rinvs.append(rinv)
    return normed, rinvs


def _gnorm_fwd(y, zx, nw):
    s = y.shape[0]
    tb = _tile(s, 512)

    def body(y_ref, z_ref, w_ref, o_ref):
        normed, _ = _group_norm_parts(y_ref[...], z_ref[...])
        for g in range(SSM_GROUPS):
            sl = slice(g * GROUP_W, (g + 1) * GROUP_W)
            o_ref[:, sl] = _mx(normed[g] * w_ref[0:1, sl])

    row = pl.BlockSpec((tb, SSM_D_INNER), lambda i: (i, 0))
    return pl.pallas_call(
        body, name="ssd_gnorm_fwd", grid=(s // tb,),
        in_specs=[row, row, pl.BlockSpec((8, SSM_D_INNER), lambda i: (0, 0))],
        out_specs=row, out_shape=jax.ShapeDtypeStruct((s, SSM_D_INNER), MXU_DTYPE),
    )(y, zx, nw)


def _gnorm_bwd(y, zx, nw, dyn):
    s = y.shape[0]
    tb = _tile(s, 512)

    def body(y_ref, z_ref, w_ref, d_ref, dy_ref, dz_ref, sums_ref):
        @pl.when(pl.program_id(0) == 0)
        def _():
            sums_ref[...] = jnp.zeros_like(sums_ref)

        yv, zv = y_ref[...], z_ref[...]
        normed, rinvs = _group_norm_parts(yv, zv)
        gate = jax.nn.silu(zv)
        dgate = _dsilu(zv)
        for g in range(SSM_GROUPS):
            sl = slice(g * GROUP_W, (g + 1) * GROUP_W)
            dv = d_ref[:, sl]
            n = normed[g]
            sums_ref[0:1, sl] += jnp.sum(dv * n, axis=0, keepdims=True)
            dn = dv * w_ref[0:1, sl]
            dyg = rinvs[g] * (dn - n * jnp.mean(dn * n, axis=-1, keepdims=True))
            dy_ref[:, sl] = dyg * gate[:, sl]
            dz_ref[:, sl] = _mx(dyg * yv[:, sl] * dgate[:, sl])

    row = pl.BlockSpec((tb, SSM_D_INNER), lambda i: (i, 0))
    par = pl.BlockSpec((8, SSM_D_INNER), lambda i: (0, 0))
    return pl.pallas_call(
        body, name="ssd_gnorm_bwd", grid=(s // tb,),
        in_specs=[row, row, par, row], out_specs=[row, row, par],
        out_shape=[jax.ShapeDtypeStruct((s, SSM_D_INNER), F32), jax.ShapeDtypeStruct((s, SSM_D_INNER), MXU_DTYPE),
                   jax.ShapeDtypeStruct((8, SSM_D_INNER), F32)],
    )(y, zx, nw, dyn)


def _rows8(v):
    v = v.reshape(1, -1)
    return jnp.pad(v, ((0, 7), (0, 0)))


def _ssd_weights(w_in, w_out):
    nzx = SSM_D_INNER + SSM_CONV_DIM
    wzx = w_in[:, :nzx]
    wdt = jnp.pad(w_in[:, nzx:], ((0, 0), (0, LANES - SSM_HEADS)))
    return dict(wzx=wzx, wdt=wdt, wzx_t=wzx.T, wdt_t=wdt.T, wout=w_out, wout_t=w_out.T)


def _ssd_fwd(u, w, cw, cb, par, nw):
    zx = _mm_nn("ssd_in_zx", u, w["wzx"], F32)
    dtr = _mm_nn("ssd_in_dt", u, w["wdt"], F32)
    xc = _conv_fwd(zx, cw, cb)
    y, prev = _ssd_scan_fwd(xc, dtr, par)
    yn = _gnorm_fwd(y, zx, nw)
    out = _mm_nn("ssd_out", yn, w["wout"], F32)
    return out, dict(zx=zx, dtr=dtr, xc=xc, y=y, prev=prev, yn=yn)


def _ssd_bwd(dy, u, w, cw, cb, par, nw, res, carry=()):
    dyn = _mm_nn("ssd_dyn", dy, w["wout_t"], F32)
    dw_out = _mm_tn("ssd_dw_out", res["yn"], dy)
    dys, dz, nsum = _gnorm_bwd(res["y"], res["zx"], nw, dyn)
    dxc, ddtr, ssum, carried = _ssd_scan_bwd(dys, res["xc"], res["dtr"], par, res["prev"],
                                             list(carry) + [(True, _row_slots(dw_out[None]))])
    dpre, csum = _conv_bwd_pre(res["zx"], dxc, cw, cb)
    dxbc = _conv_bwd_x(dpre, cw)
    wt = w["wzx_t"]
    du = _mm_sum("ssd_du", [(dz, wt[:SSM_D_INNER]), (dxbc, wt[SSM_D_INNER:]), (ddtr, w["wdt_t"])], tn=512)
    dw_in = jnp.concatenate(
        [_mm_tn("ssd_dw_z", u, dz), _mm_tn("ssd_dw_x", u, dxbc), _mm_tn("ssd_dw_dt", u, ddtr)[:, :SSM_HEADS]], axis=1)
    small = dict(conv_w=csum[:SSM_CONV], conv_b=csum[SSM_CONV], dt_bias=ssum[0, :SSM_HEADS],
                 a_log=ssum[1, :SSM_HEADS], d=ssum[2, :SSM_HEADS], norm_w=nsum[0])
    return du, dw_in, small, carried


def _ada_fwd(c_all, ada_w, ada_b_mine):
    nl, _, ncol = ada_w.shape

    def body(c_ref, w_ref, b_ref, o_ref):
        cond = _mx(jax.nn.silu(c_ref[...]))
        for i in range(nl):
            o_ref[i] = _dot(cond, _mx(w_ref[i])) + b_ref[i:i + 1, :]

    return pl.pallas_call(
        body, name="ada_fwd", out_shape=jax.ShapeDtypeStruct((nl, 2 * N_DEV, ncol), F32),
        compiler_params=_params(VMEM_BIG),
    )(c_all, ada_w, ada_b_mine)


def _ada_bwd(c_all, dmod_cols):
    nl, _, ncol = dmod_cols.shape

    def body(c_ref, d_ref, o_ref):
        cond = _mx(jax.nn.silu(c_ref[...]))
        for i in range(nl):
            o_ref[i] = _dot(cond, _mx(d_ref[i]), TN)

    return pl.pallas_call(
        body, name="ada_bwd", out_shape=jax.ShapeDtypeStruct((nl, D_MODEL, ncol), F32),
        compiler_params=_params(VMEM_BIG),
    )(c_all, dmod_cols)


def _adamw(gslots, w, m, v, name):
    k, r, c = gslots.shape
    tr = _tile(r, 256) if r % 256 == 0 else r
    c1 = 1.0 - ADAM_B1 ** ADAM_STEP
    c2 = 1.0 - ADAM_B2 ** ADAM_STEP

    def body(g_ref, w_ref, m_ref, v_ref, go_ref, d_ref, mo_ref, vo_ref):
        g = g_ref[0]
        for slot in range(1, k):
            g = g + g_ref[slot]
        mn = ADAM_B1 * m_ref[...] + (1.0 - ADAM_B1) * g
        vn = ADAM_B2 * v_ref[...] + (1.0 - ADAM_B2) * jnp.square(g)
        go_ref[...] = g
        mo_ref[...] = mn
        vo_ref[...] = vn
        d_ref[...] = -ADAM_LR * ((mn / c1) / (jnp.sqrt(vn / c2) + ADAM_EPS) + ADAM_WD * w_ref[...])

    row = pl.BlockSpec((tr, c), lambda i: (i, 0))
    shp = jax.ShapeDtypeStruct((r, c), F32)
    return pl.pallas_call(
        body, name=name, grid=(r // tr,),
        in_specs=[pl.BlockSpec((k, tr, c), lambda i: (0, i, 0)), row, row, row],
        out_specs=[row, row, row, row], out_shape=[shp, shp, shp, shp],
        compiler_params=_params(VMEM_BIG),
    )(gslots, w, m, v)


def _adamw_any(gslots, w, m, v, name):
    shape = w.shape
    two_d = (-1, shape[-1])
    k = gslots.shape[0]
    outs = _adamw(gslots.reshape((k,) + w.reshape(two_d).shape), w.reshape(two_d), m.reshape(two_d),
                  v.reshape(two_d), name)
    return tuple(o.reshape(shape) for o in outs)


def _cols_from_slots(g):
    g = jnp.moveaxis(g, 0, -2)
    return g.reshape(g.shape[:-2] + (g.shape[-2] * g.shape[-1],))


def _rows_from_slots(g):
    g = jnp.moveaxis(g, 0, -3)
    return g.reshape(g.shape[:-3] + (g.shape[-3] * g.shape[-2], g.shape[-1]))


def _col_slots(g):
    cs = g.shape[-1] // N_DEV
    return jnp.moveaxis(g.reshape(g.shape[:-1] + (N_DEV, cs)), -2, 0)


def _row_slots(g):
    rs = g.shape[-2] // N_DEV
    return jnp.moveaxis(g.reshape(g.shape[:-2] + (N_DEV, rs, g.shape[-1])), -3, 0)


def _gather_cols(w, name, dtype=None):
    return _cols_from_slots(_all_gather(w.astype(dtype or MXU_DTYPE), name))


def _gather_rows(w, name):
    return _rows_from_slots(_all_gather(_mx(w), name))


def kernel(x, c, ada_w, ada_b, ln_mix_g, ln_mix_b, ln_mlp_g, ln_mlp_b, mlp_w1, mlp_w2, fox_w_in, fox_b_f, fox_w_o, ssm_w_in, ssm_conv_w, ssm_conv_b, ssm_dt_bias, ssm_a_log, ssm_d, ssm_norm_w, ssm_w_out, loss_target, m_ada_w, m_ada_b, m_ln_mix_g, m_ln_mix_b, m_ln_mlp_g, m_ln_mlp_b, m_mlp_w1, m_mlp_w2, m_fox_w_in, m_fox_b_f, m_fox_w_o, m_ssm_w_in, m_ssm_conv_w, m_ssm_conv_b, m_ssm_dt_bias, m_ssm_a_log, m_ssm_d, m_ssm_norm_w, m_ssm_w_out, v_ada_w, v_ada_b, v_ln_mix_g, v_ln_mix_b, v_ln_mlp_g, v_ln_mlp_b, v_mlp_w1, v_mlp_w2, v_fox_w_in, v_fox_b_f, v_fox_w_o, v_ssm_w_in, v_ssm_conv_w, v_ssm_conv_b, v_ssm_dt_bias, v_ssm_a_log, v_ssm_d, v_ssm_norm_w, v_ssm_w_out):
    me = 4 * lax.axis_index("x") + 2 * lax.axis_index("y") + lax.axis_index("c")
    xs = x[0]
    target = loss_target[0]
    d = D_MODEL

    c_all = _all_gather(c, "gather_c").reshape(N_DEV, d)
    c_all = jnp.pad(c_all, ((0, N_DEV), (0, 0)))
    ncol = ada_w.shape[-1]
    ada_b_mine = lax.dynamic_slice_in_dim(ada_b, me * ncol, ncol, axis=1)
    mod_cols = _ada_fwd(c_all, ada_w, ada_b_mine)
    mod_all = _all_gather(mod_cols, "gather_mod")
    mod = lax.dynamic_index_in_dim(mod_all, me, axis=2, keepdims=False)
    mod = jnp.moveaxis(mod, 0, 1).reshape(DEPTH, 6, d)

    def pv_rows(*rows):
        return jnp.pad(jnp.stack(rows), ((0, 8 - len(rows)), (0, 0)))

    fw = _fox_weights(_gather_cols(fox_w_in, "gather_fox_in")[0], _gather_rows(fox_w_o, "gather_fox_o")[0])
    conv_w = _gather_cols(ssm_conv_w, "gather_conv_w", F32)[0]
    small_vec = jnp.concatenate([ssm_conv_b[0], ssm_norm_w[0]]).reshape(1, -1)
    small_all = _all_gather(small_vec.astype(F32), "gather_conv_b").reshape(N_DEV, -1)
    conv_b = small_all[:, :SSM_CONV_DIM // N_DEV].reshape(-1)
    norm_w = small_all[:, SSM_CONV_DIM // N_DEV:].reshape(-1)
    cw8 = jnp.pad(conv_w, ((0, 8 - SSM_CONV), (0, 0)))
    cb8 = _rows8(conv_b)
    nw8 = _rows8(norm_w)
    bf8 = _rows8(jnp.pad(fox_b_f[0], (0, LANES - FOX_HEADS)))
    par8 = jnp.pad(jnp.stack([jnp.pad(p[0], (0, LANES - SSM_HEADS)) for p in (ssm_dt_bias, ssm_a_log, ssm_d)]),
                   ((0, 5), (0, 0)))

    sh_a, sc_a, g_a, sh_m, sc_m, g_m = [mod[:, k] for k in range(6)]
    u0 = _modulate(xs, pv_rows(1.0 + sc_a[0], sh_a[0]), "modulate0")
    y0, fres, gathered = _fox_fwd(u0, fw, bf8, [(False, _mx(mlp_w1)), (False, _mx(mlp_w2))])
    w1 = _cols_from_slots(gathered[0])
    w2 = _rows_from_slots(gathered[1])
    pv0 = pv_rows(1.0 + g_a[0], ln_mix_g[0], ln_mix_b[0], 1.0 + sc_m[0], sh_m[0])
    x1, u1 = _ln_fwd(xs, y0, pv0, "ln_mix0")
    y1, (h0, a0), gathered = _mlp_fwd(u1, w1[0], w2[0], "0", [(False, _mx(ssm_w_in)), (False, _mx(ssm_w_out))])
    sw = _ssd_weights(_cols_from_slots(gathered[0])[0], _rows_from_slots(gathered[1])[0])
    pv1 = pv_rows(1.0 + g_m[0], ln_mlp_g[0], ln_mlp_b[0], 1.0 + sc_a[1], sh_a[1])
    x2, u2 = _ln_fwd(x1, y1, pv1, "ln_mlp0")
    y2, sres = _ssd_fwd(u2, sw, cw8, cb8, par8, nw8)
    pv2 = pv_rows(1.0 + g_a[1], ln_mix_g[1], ln_mix_b[1], 1.0 + sc_m[1], sh_m[1])
    x3, u3 = _ln_fwd(x2, y2, pv2, "ln_mix1")
    y3, (h1, a1), _ = _mlp_fwd(u3, w1[1], w2[1], "1")
    pv3 = pv_rows(1.0 + g_m[1], ln_mlp_g[1], ln_mlp_b[1])

    dx3, dy3, s3 = _ln_bwd(x3, y3, pv3, "ln_mlp1_bwd", target=target)
    loss = lax.psum(s3[5, 0], ("x", "y", "c"))
    du3, dw1_1, dw2_1, _ = _mlp_bwd(dy3, u3, h1, a1, w1[1].T, w2[1].T, "1")
    dx2, dy2, s2 = _ln_bwd(x2, y2, pv2, "ln_mix1_bwd", dxo=dx3, du=du3)
    du2, d_ssm_in, ssm_small, ex_scan = _ssd_bwd(
        dy2, u2, sw, cw8, cb8, par8, nw8, sres,
        [(True, _col_slots(dw1_1[None])), (True, _row_slots(dw2_1[None]))])
    dx1, dy1, s1 = _ln_bwd(x1, y1, pv1, "ln_mlp0_bwd", dxo=dx2, du=du2)
    half = d_ssm_in.shape[0] // 2
    du1, dw1_0, dw2_0, ex_mlp = _mlp_bwd(dy1, u1, h0, a0, w1[0].T, w2[0].T, "0",
                                         [(True, _col_slots(d_ssm_in[None, :half]))],
                                         [(True, _col_slots(d_ssm_in[None, half:]))])
    dx0, dy0, s0 = _ln_bwd(xs, y0, pv0, "ln_mix0_bwd", dxo=dx1, du=du1)
    late = [(True, _col_slots(dw1_0[None])), (True, _row_slots(dw2_0[None])),
            (True, _col_slots(ssm_small["conv_w"][None])), (True, _col_slots(ssm_small["conv_b"][None])),
            (True, _col_slots(ssm_small["norm_w"][None]))]
    du0, fox_sums, ex_post = _fox_bwd(dy0, u0, fw, bf8, fres, late)
    grad_x, sx = _mod_bwd(dx0, du0, xs, pv_rows(1.0 + sc_a[0], sh_a[0]), "modulate0_bwd")

    dmod = jnp.stack([
        jnp.stack([sx[1], sx[0], s0[4], s0[1], s0[0], s1[4]]),
        jnp.stack([s1[1], s1[0], s2[4], s2[1], s2[0], s3[4]]),
    ]).reshape(DEPTH, 6 * d)

    def pad_rows(v):
        v = v.reshape(-1, LANES) if v.size % LANES == 0 else jnp.pad(v.reshape(-1), (0, LANES - v.size)).reshape(1, LANES)
        return jnp.pad(v, ((0, (-v.shape[0]) % 8), (0, 0)))

    small_parts = [dmod, jnp.stack([s0[2], s2[2]]), jnp.stack([s0[3], s2[3]]), jnp.stack([s1[2], s3[2]]),
                   jnp.stack([s1[3], s3[3]]), fox_sums[0, :FOX_HEADS], ssm_small["dt_bias"], ssm_small["a_log"],
                   ssm_small["d"]]
    packed = [pad_rows(p) for p in small_parts]
    offsets = np.cumsum([0] + [p.shape[0] for p in packed])
    small_all_g = _all_gather(jnp.concatenate(packed, axis=0), "gather_small_grads")

    def unpack(idx, shape):
        n = int(np.prod(shape))
        blk = small_all_g[:, offsets[idx]:offsets[idx + 1]].reshape(N_DEV, -1)[:, :n]
        return blk.reshape((N_DEV,) + tuple(shape))

    dmod_all = unpack(0, (DEPTH, 6 * d))
    dmod_cols = lax.dynamic_slice_in_dim(dmod_all, me * ncol, ncol, axis=2)
    dmod_cols = jnp.pad(jnp.moveaxis(dmod_cols, 0, 1), ((0, 0), (0, N_DEV), (0, 0)))
    g_ada_w = _ada_bwd(c_all, dmod_cols)

    shards = dict(
        mlp_w1=jnp.concatenate([ex_post[0], ex_scan[0]], axis=1),
        mlp_w2=jnp.concatenate([ex_post[1], ex_scan[1]], axis=1),
        ssm_w_in=jnp.concatenate(ex_mlp, axis=2), ssm_w_out=ex_scan[2],
        ssm_conv_w=ex_post[2], ssm_conv_b=ex_post[3], ssm_norm_w=ex_post[4],
        fox_w_in=ex_post[5], fox_w_o=ex_post[6],
        ada_w=g_ada_w[None], ada_b=dmod_all,
        ln_mix_g=unpack(1, (DEPTH, d)), ln_mix_b=unpack(2, (DEPTH, d)),
        ln_mlp_g=unpack(3, (DEPTH, d)), ln_mlp_b=unpack(4, (DEPTH, d)),
        fox_b_f=unpack(5, (1, FOX_HEADS)), ssm_dt_bias=unpack(6, (1, SSM_HEADS)),
        ssm_a_log=unpack(7, (1, SSM_HEADS)), ssm_d=unpack(8, (1, SSM_HEADS)),
    )
    weights = dict(ada_w=ada_w, ada_b=ada_b, ln_mix_g=ln_mix_g, ln_mix_b=ln_mix_b, ln_mlp_g=ln_mlp_g, ln_mlp_b=ln_mlp_b,
                   mlp_w1=mlp_w1, mlp_w2=mlp_w2, fox_w_in=fox_w_in, fox_b_f=fox_b_f, fox_w_o=fox_w_o, ssm_w_in=ssm_w_in,
                   ssm_conv_w=ssm_conv_w, ssm_conv_b=ssm_conv_b, ssm_dt_bias=ssm_dt_bias, ssm_a_log=ssm_a_log,
                   ssm_d=ssm_d, ssm_norm_w=ssm_norm_w, ssm_w_out=ssm_w_out)
    mom1 = dict(ada_w=m_ada_w, ada_b=m_ada_b, ln_mix_g=m_ln_mix_g, ln_mix_b=m_ln_mix_b, ln_mlp_g=m_ln_mlp_g,
                ln_mlp_b=m_ln_mlp_b, mlp_w1=m_mlp_w1, mlp_w2=m_mlp_w2, fox_w_in=m_fox_w_in, fox_b_f=m_fox_b_f,
                fox_w_o=m_fox_w_o, ssm_w_in=m_ssm_w_in, ssm_conv_w=m_ssm_conv_w, ssm_conv_b=m_ssm_conv_b,
                ssm_dt_bias=m_ssm_dt_bias, ssm_a_log=m_ssm_a_log, ssm_d=m_ssm_d, ssm_norm_w=m_ssm_norm_w,
                ssm_w_out=m_ssm_w_out)
    mom2 = dict(ada_w=v_ada_w, ada_b=v_ada_b, ln_mix_g=v_ln_mix_g, ln_mix_b=v_ln_mix_b, ln_mlp_g=v_ln_mlp_g,
                ln_mlp_b=v_ln_mlp_b, mlp_w1=v_mlp_w1, mlp_w2=v_mlp_w2, fox_w_in=v_fox_w_in, fox_b_f=v_fox_b_f,
                fox_w_o=v_fox_w_o, ssm_w_in=v_ssm_w_in, ssm_conv_w=v_ssm_conv_w, ssm_conv_b=v_ssm_conv_b,
                ssm_dt_bias=v_ssm_dt_bias, ssm_a_log=v_ssm_a_log, ssm_d=v_ssm_d, ssm_norm_w=v_ssm_norm_w,
                ssm_w_out=v_ssm_w_out)
    names = list(weights)
    stepped = {n: _adamw_any(shards[n], weights[n], mom1[n], mom2[n], f"adamw_{n}") for n in names}
    return (loss, grad_x[None], *[stepped[n][0] for n in names], *[stepped[n][1] for n in names],
            *[stepped[n][2] for n in names], *[stepped[n][3] for n in names])
```

```python
import numpy as np
import jax
import jax.numpy as jnp
from jax import lax
from jax.experimental import pallas as pl
from jax.experimental.pallas import tpu as pltpu

F32 = jnp.float32
MXU_DTYPE = jnp.bfloat16

N_DEV = 8
D_MODEL = 1024
DEPTH = 2
FOX_HEADS = 16
FOX_HEAD_DIM = 64
D_FF = 4096
SSM_D_INNER = 2048
SSM_HEADS = 32
SSM_GROUPS = 8
SSM_STATE = 128
SSM_CHUNK = 128
SSM_CONV = 4
SSM_CONV_DIM = 4096
GROUP_W = SSM_D_INNER // SSM_GROUPS
LN_EPS = 1e-5
RMS_EPS = 1e-5
ALPHA = (2.0 * DEPTH) ** 0.25
LANES = 128
SUBLANES = 8

ADAM_LR = 0.001
ADAM_B1 = 0.9
ADAM_B2 = 0.999
ADAM_EPS = 1e-08
ADAM_WD = 0.01
ADAM_STEP = 10

NN = (((1,), (0,)), ((), ()))
NT = (((1,), (1,)), ((), ()))
TN = (((0,), (0,)), ((), ()))

VMEM_BIG = 56 * 1024 * 1024
MM_ROWS = 2048
MM_DEPTH = 2048


def _dot(a, b, dims=NN, precision=None):
    return lax.dot_general(a, b, dims, precision=precision, preferred_element_type=F32)


def _mx(v):
    return v.astype(MXU_DTYPE)


def _pieces3(v):
    hi = _mx(v)
    r1 = v - hi.astype(F32)
    mid = _mx(r1)
    return hi, mid, _mx(r1 - mid.astype(F32))


def _dot_onehot(a, b, dims=NN, onehot="b"):
    if onehot == "b":
        return sum(_dot(p, _mx(b), dims) for p in _pieces3(a))
    return sum(_dot(_mx(a), p, dims) for p in _pieces3(b))


def _params(vmem=None):
    return pltpu.CompilerParams(vmem_limit_bytes=vmem) if vmem else None


def _all_gather(x, name):
    def body(x_ref, out_ref, send_sems, recv_sems, local_sem):
        xi, yi, ci = lax.axis_index("x"), lax.axis_index("y"), lax.axis_index("c")
        me, sibling = (xi, yi, ci), (xi, yi, 1 - ci)
        chips = [(1 - xi, yi), (xi, 1 - yi), (1 - xi, 1 - yi)]

        def slot(px, py, pc):
            return out_ref.at[4 * px + 2 * py + pc]

        def copy(k, block, to, src=None):
            return pltpu.make_async_remote_copy(
                src_ref=slot(*block) if src is None else src, dst_ref=slot(*block),
                send_sem=send_sems.at[k], recv_sem=recv_sems.at[k],
                device_id=to, device_id_type=pl.DeviceIdType.MESH)

        mine = pltpu.make_async_copy(x_ref, slot(*me), local_sem)
        mine.start()
        first = [copy(0, me, sibling, src=x_ref)]
        first += [copy(1 + j, me, (*chip, ci), src=x_ref) for j, chip in enumerate(chips)]
        for cp in first:
            cp.start()
        passed = [copy(4 + j, (*chip, ci), sibling) for j, chip in enumerate(chips)]
        for j, chip in enumerate(chips):
            copy(1 + j, (*chip, ci), me).wait_recv()
            passed[j].start()
        copy(0, sibling, me).wait_recv()
        for j, chip in enumerate(chips):
            copy(4 + j, (*chip, 1 - ci), me).wait_recv()
        for cp in first + passed:
            cp.wait_send()
        mine.wait()

    return pl.pallas_call(
        body, name=name,
        out_shape=jax.ShapeDtypeStruct((N_DEV,) + x.shape, x.dtype),
        in_specs=[pl.BlockSpec(memory_space=pl.ANY)],
        out_specs=pl.BlockSpec(memory_space=pl.ANY),
        scratch_shapes=[pltpu.SemaphoreType.DMA((7,)), pltpu.SemaphoreType.DMA((7,)),
                        pltpu.SemaphoreType.DMA],
    )(x)


def _direct_copies(scatter, x_ref, out_ref, send_sems, recv_sems, local_sems, n):
    xi, yi, ci = lax.axis_index("x"), lax.axis_index("y"), lax.axis_index("c")
    me = 4 * xi + 2 * yi + ci
    local = pltpu.make_async_copy(x_ref.at[me] if scatter else x_ref, out_ref.at[me], local_sems.at[n])
    remote = []
    for k in range(1, N_DEV):
        px = 1 - xi if k & 4 else xi
        py = 1 - yi if k & 2 else yi
        pc = 1 - ci if k & 1 else ci
        remote.append(pltpu.make_async_remote_copy(
            src_ref=x_ref.at[4 * px + 2 * py + pc] if scatter else x_ref, dst_ref=out_ref.at[me],
            send_sem=send_sems.at[7 * n + k - 1], recv_sem=recv_sems.at[7 * n + k - 1],
            device_id=(px, py, pc), device_id_type=pl.DeviceIdType.MESH))
    return local, remote


def _exchange_shapes(carry):
    return [jax.ShapeDtypeStruct(a.shape if scatter else (N_DEV,) + a.shape, a.dtype) for scatter, a in carry]


def _exchange_sems(carry):
    n = len(carry)
    return [pltpu.SemaphoreType.DMA((7 * n,)), pltpu.SemaphoreType.DMA((7 * n,)), pltpu.SemaphoreType.DMA((n,))]


def _carrying(body, n_in, n_out, n_scratch, carry, grid):
    nc = len(carry)

    def copies(refs):
        srcs = refs[n_in:n_in + nc]
        dsts = refs[n_in + nc + n_out:n_in + 2 * nc + n_out]
        sems = refs[n_in + 2 * nc + n_out + n_scratch:]
        return [_direct_copies(scatter, srcs[n], dsts[n], *sems, n) for n, (scatter, _) in enumerate(carry)]

    def wrapped(*refs):
        step = 0
        for axis, extent in enumerate(grid):
            step = step * extent + pl.program_id(axis)

        @pl.when(step == 0)
        def _():
            for local, remote in copies(refs):
                local.start()
                for cp in remote:
                    cp.start()

        body(*refs[:n_in], *refs[n_in + nc:n_in + nc + n_out],
             *refs[n_in + 2 * nc + n_out:n_in + 2 * nc + n_out + n_scratch])

        @pl.when(step == int(np.prod(grid)) - 1)
        def _():
            for local, remote in copies(refs):
                for cp in remote:
                    cp.wait()
                local.wait()

    return wrapped if nc else body


def _mm(name, a, b, *, grid, a_spec, b_spec, dims, k_axis, outs, acc=None, extras=(), epi=None, vmem=None,
        carry=()):
    nk = grid[k_axis]
    n_ex, n_out = len(extras), len(outs)
    carry = list(carry)
    anywhere = pl.BlockSpec(memory_space=pl.ANY)

    def body(*refs):
        a_ref, b_ref = refs[0], refs[1]
        ex = refs[2:2 + n_ex]
        out = refs[2 + n_ex:2 + n_ex + n_out]

        def finish(val):
            if epi is None:
                out[0][...] = val.astype(out[0].dtype)
            else:
                epi(val, ex, out)

        part = _dot(a_ref[...], b_ref[...], dims)
        if nk == 1:
            finish(part)
        else:
            acc_ref = refs[2 + n_ex + n_out]
            k = pl.program_id(k_axis)

            @pl.when(k == 0)
            def _():
                acc_ref[...] = part

            @pl.when(k > 0)
            def _():
                acc_ref[...] += part

            @pl.when(k == nk - 1)
            def _():
                finish(acc_ref[...])

    res = pl.pallas_call(
        _carrying(body, 2 + n_ex, n_out, 1 if nk > 1 else 0, carry, grid), name=name, grid=grid,
        in_specs=[a_spec, b_spec] + [s for _, s in extras] + [anywhere] * len(carry),
        out_specs=[s for _, s in outs] + [anywhere] * len(carry),
        out_shape=[o for o, _ in outs] + _exchange_shapes(carry),
        scratch_shapes=([pltpu.VMEM(acc, F32)] if nk > 1 else []) + (_exchange_sems(carry) if carry else []),
        compiler_params=_params(vmem),
    )(a, b, *[e for e, _ in extras], *[arr for _, arr in carry])
    return list(res)


def _tile(n, t):
    t = min(n, t)
    assert n % t == 0, (n, t)
    return t


def _mm_nn(name, a, b, out_dtype, *, tm=MM_ROWS, tn=1024, tk=1024, epi=None, extras=(), outs=None, carry=()):
    m, kk = a.shape
    n = b.shape[1]
    tm, tn, tk = _tile(m, tm), _tile(n, tn), _tile(kk, tk)
    if outs is None:
        outs = [(jax.ShapeDtypeStruct((m, n), out_dtype), pl.BlockSpec((tm, tn), lambda i, j, k: (i, j)))]
    res = _mm(name, a, b, grid=(m // tm, n // tn, kk // tk),
              a_spec=pl.BlockSpec((tm, tk), lambda i, j, k: (i, k)),
              b_spec=pl.BlockSpec((tk, tn), lambda i, j, k: (k, j)),
              dims=NN, k_axis=2, acc=(tm, tn), outs=outs, extras=list(extras), epi=epi, vmem=VMEM_BIG, carry=carry)
    return res[0] if len(res) == 1 else res


def _mm_sum(name, pairs, out_dtype=F32, *, tm=1024, tn=1024, carry=()):
    m, n = pairs[0][0].shape[0], pairs[0][1].shape[1]
    tm, tn = _tile(m, tm), _tile(n, tn)
    carry = list(carry)
    grid = (m // tm, n // tn)

    def body(*refs):
        out = refs[2 * len(pairs)]
        acc = _dot(refs[0][...], refs[1][...])
        for p in range(1, len(pairs)):
            acc = acc + _dot(refs[2 * p][...], refs[2 * p + 1][...])
        out[...] = acc.astype(out.dtype)

    in_specs = []
    for a, b in pairs:
        in_specs += [pl.BlockSpec((tm, a.shape[1]), lambda i, j: (i, 0)),
                     pl.BlockSpec((b.shape[0], tn), lambda i, j: (0, j))]
    anywhere = pl.BlockSpec(memory_space=pl.ANY)
    res = pl.pallas_call(
        _carrying(body, 2 * len(pairs), 1, 0, carry, grid), name=name, grid=grid,
        in_specs=in_specs + [anywhere] * len(carry),
        out_specs=[pl.BlockSpec((tm, tn), lambda i, j: (i, j))] + [anywhere] * len(carry),
        out_shape=[jax.ShapeDtypeStruct((m, n), out_dtype)] + _exchange_shapes(carry),
        scratch_shapes=_exchange_sems(carry) if carry else [],
        compiler_params=_params(VMEM_BIG),
    )(*[x for pair in pairs for x in pair], *[arr for _, arr in carry])
    return list(res) if carry else res[0]


def _mm_tn(name, a, b, out_dtype=F32, *, tm=1024, tn=1024, tk=MM_DEPTH):
    kk, m = a.shape
    n = b.shape[1]
    tm, tn, tk = _tile(m, tm), _tile(n, tn), _tile(kk, tk)
    res = _mm(name, a, b, grid=(m // tm, n // tn, kk // tk),
              a_spec=pl.BlockSpec((tk, tm), lambda i, j, k: (k, i)),
              b_spec=pl.BlockSpec((tk, tn), lambda i, j, k: (k, j)),
              dims=TN, k_axis=2, acc=(tm, tn),
              outs=[(jax.ShapeDtypeStruct((m, n), out_dtype), pl.BlockSpec((tm, tn), lambda i, j, k: (i, j)))],
              vmem=VMEM_BIG)
    return res[0]


def _row_block(s):
    return _tile(s, 512)


def _modulate(x, pv, name):
    s, d = x.shape
    tb = _row_block(s)

    def body(x_ref, pv_ref, u_ref):
        u_ref[...] = _mx(x_ref[...] * pv_ref[0:1, :] + pv_ref[1:2, :])

    return pl.pallas_call(
        body, name=name, grid=(s // tb,),
        in_specs=[pl.BlockSpec((tb, d), lambda i: (i, 0)), pl.BlockSpec((8, d), lambda i: (0, 0))],
        out_specs=pl.BlockSpec((tb, d), lambda i: (i, 0)),
        out_shape=jax.ShapeDtypeStruct((s, d), MXU_DTYPE),
    )(x, pv)


def _ln_stats(r):
    mu = jnp.mean(r, axis=-1, keepdims=True)
    xc = r - mu
    var = jnp.mean(xc * xc, axis=-1, keepdims=True)
    rstd = lax.rsqrt(var + LN_EPS)
    return xc * rstd, rstd


def _ln_fwd(xin, y, pv, name):
    s, d = xin.shape
    tb = _row_block(s)

    def body(x_ref, y_ref, pv_ref, xo_ref, u_ref):
        r = ALPHA * x_ref[...] + pv_ref[0:1, :] * y_ref[...]
        xhat, _ = _ln_stats(r)
        xo = xhat * pv_ref[1:2, :] + pv_ref[2:3, :]
        xo_ref[...] = xo
        u_ref[...] = _mx(xo * pv_ref[3:4, :] + pv_ref[4:5, :])

    row = pl.BlockSpec((tb, d), lambda i: (i, 0))
    return pl.pallas_call(
        body, name=name, grid=(s // tb,),
        in_specs=[row, row, pl.BlockSpec((8, d), lambda i: (0, 0))],
        out_specs=[row, row],
        out_shape=[jax.ShapeDtypeStruct((s, d), F32), jax.ShapeDtypeStruct((s, d), MXU_DTYPE)],
    )(xin, y, pv)


def _ln_bwd(xin, y, pv, name, *, dxo=None, du=None, target=None):
    s, d = xin.shape
    tb = _row_block(s)
    nb = s // tb
    loss_mode = target is not None

    def body(*refs):
        if loss_mode:
            x_ref, y_ref, pv_ref, t_ref, dxin_ref, dy_ref, sums_ref = refs
        else:
            x_ref, y_ref, pv_ref, dxo_ref, du_ref, dxin_ref, dy_ref, sums_ref = refs
        i = pl.program_id(0)

        @pl.when(i == 0)
        def _():
            sums_ref[...] = jnp.zeros_like(sums_ref)

        yv = y_ref[...]
        r = ALPHA * x_ref[...] + pv_ref[0:1, :] * yv
        xhat, rstd = _ln_stats(r)
        xo = xhat * pv_ref[1:2, :] + pv_ref[2:3, :]
        if loss_mode:
            diff = xo - t_ref[...]
            dxo_v = diff * (1.0 / d)
            sums_ref[5:6, :] += jnp.sum(diff * diff, axis=0, keepdims=True) * (0.5 / d)
        else:
            duv = du_ref[...]
            dxo_v = dxo_ref[...] + duv * pv_ref[3:4, :]
            sums_ref[0:1, :] += jnp.sum(duv * xo, axis=0, keepdims=True)
            sums_ref[1:2, :] += jnp.sum(duv, axis=0, keepdims=True)
        sums_ref[2:3, :] += jnp.sum(dxo_v * xhat, axis=0, keepdims=True)
        sums_ref[3:4, :] += jnp.sum(dxo_v, axis=0, keepdims=True)
        dxh = dxo_v * pv_ref[1:2, :]
        dr = rstd * (dxh - jnp.mean(dxh, axis=-1, keepdims=True)
                     - xhat * jnp.mean(dxh * xhat, axis=-1, keepdims=True))
        sums_ref[4:5, :] += jnp.sum(dr * yv, axis=0, keepdims=True)
        dxin_ref[...] = ALPHA * dr
        dy_ref[...] = _mx(pv_ref[0:1, :] * dr)
        if loss_mode:
            @pl.when(i == nb - 1)
            def _():
                sums_ref[5:6, :] = jnp.broadcast_to(jnp.sum(sums_ref[5:6, :], axis=-1, keepdims=True), (1, d))

    row = pl.BlockSpec((tb, d), lambda i: (i, 0))
    par = pl.BlockSpec((8, d), lambda i: (0, 0))
    ins = [xin, y, pv] + ([target] if loss_mode else [dxo, du])
    return pl.pallas_call(
        body, name=name, grid=(nb,),
        in_specs=[row, row, par] + [row] * (len(ins) - 3),
        out_specs=[row, row, par],
        out_shape=[jax.ShapeDtypeStruct((s, d), F32), jax.ShapeDtypeStruct((s, d), MXU_DTYPE),
                   jax.ShapeDtypeStruct((8, d), F32)],
    )(*ins)


def _mod_bwd(dx_direct, du, x, pv, name):
    s, d = x.shape
    tb = _row_block(s)

    def body(dxd_ref, du_ref, x_ref, pv_ref, dx_ref, sums_ref):
        @pl.when(pl.program_id(0) == 0)
        def _():
            sums_ref[...] = jnp.zeros_like(sums_ref)

        duv = du_ref[...]
        dx_ref[...] = dxd_ref[...] + duv * pv_ref[0:1, :]
        sums_ref[0:1, :] += jnp.sum(duv * x_ref[...], axis=0, keepdims=True)
        sums_ref[1:2, :] += jnp.sum(duv, axis=0, keepdims=True)

    row = pl.BlockSpec((tb, d), lambda i: (i, 0))
    par = pl.BlockSpec((8, d), lambda i: (0, 0))
    return pl.pallas_call(
        body, name=name, grid=(s // tb,),
        in_specs=[row, row, row, par], out_specs=[row, par],
        out_shape=[jax.ShapeDtypeStruct((s, d), F32), jax.ShapeDtypeStruct((8, d), F32)],
    )(dx_direct, du, x, pv)


def _mlp_fwd(u, w1, w2, tag, carry=()):
    s = u.shape[0]

    def epi(val, ex, out):
        out[0][...] = _mx(val)
        out[1][...] = _mx(jnp.square(jnp.maximum(val, 0.0)))

    tm, tn = _tile(s, MM_ROWS), 1024
    spec = pl.BlockSpec((tm, tn), lambda i, j, k: (i, j))
    shp = jax.ShapeDtypeStruct((s, D_FF), MXU_DTYPE)
    h, a, *carried = _mm_nn(f"mlp_up{tag}", u, w1, None, epi=epi, outs=[(shp, spec), (shp, spec)], tn=tn,
                            carry=carry)
    y = _mm_nn(f"mlp_down{tag}", a, w2, F32)
    return y, (h, a), carried


def _mlp_bwd(dy, u, h, a, w1t, w2t, tag, carry=(), carry_du=()):
    s = u.shape[0]
    tm, tn = _tile(s, MM_ROWS), 1024
    spec = pl.BlockSpec((tm, tn), lambda i, j, k: (i, j))

    def epi(val, ex, out):
        out[0][...] = _mx(val * (2.0 * jnp.maximum(ex[0][...].astype(F32), 0.0)))

    got = _mm_nn(f"mlp_dh{tag}", dy, w2t, None, epi=epi, extras=[(h, spec)],
                 outs=[(jax.ShapeDtypeStruct((s, D_FF), MXU_DTYPE), spec)], tn=tn, carry=carry)
    dh, carried = (got[0], list(got[1:])) if carry else (got, [])
    got = _mm_nn(f"mlp_du{tag}", dh, w1t, F32, carry=carry_du)
    du, carried = (got[0], carried + list(got[1:])) if carry_du else (got, carried)
    dw2 = _mm_tn(f"mlp_dw2{tag}", a, dy)
    dw1 = _mm_tn(f"mlp_dw1{tag}", u, dh)
    return du, dw1, dw2, carried


FOX_T = 1024
BIAS_Q = (64, 65, 66)
BIAS_K = (67, 68, 69)
SKIP_MARGIN = 110.0
ONES_V = 64

def _fox_constants():
    selq = np.zeros((FOX_HEADS, 512, LANES), np.float32)
    selk = np.zeros((FOX_HEADS, 512, LANES), np.float32)
    selv = np.zeros((2, LANES, LANES), np.float32)
    put = np.zeros((2, 2, LANES, LANES), np.float32)
    for h in range(FOX_HEADS):
        off = FOX_HEAD_DIM * (h % 2)
        for dd in range(FOX_HEAD_DIM):
            selq[h, off + dd, dd] = FOX_HEAD_DIM ** -0.5
            selk[h, off + dd, dd] = 1.0
        for piece in range(3):
            selq[h, LANES * (1 + piece) + h, BIAS_Q[piece]] = 1.0
            selk[h, LANES * (1 + piece) + h, BIAS_K[piece]] = -1.0
    for par in range(2):
        for dd in range(FOX_HEAD_DIM):
            selv[par, FOX_HEAD_DIM * par + dd, dd] = 1.0
            put[par, 0, dd, FOX_HEAD_DIM * par + dd] = FOX_HEAD_DIM ** -0.5
            put[par, 1, dd, FOX_HEAD_DIM * par + dd] = 1.0
    return selq, selk, selv, put


def _fox_prep(qkv, f, bf, carry=()):
    s = qkv.shape[0]
    t = _tile(s, FOX_T)
    nb = s // t
    selq, selk, selv, _ = _fox_constants()
    carry = list(carry)
    anywhere = pl.BlockSpec(memory_space=pl.ANY)

    def body(q_ref, k_ref, v_ref, f_ref, bf_ref, selq_ref, selk_ref, selv_ref,
             qa_ref, qat_ref, ka_ref, kat_ref, va_ref, vat_ref, stats_ref, parts_ref, carry_ref, cum_ref):
        i, h = pl.program_id(0), pl.program_id(1)
        lane = lax.broadcasted_iota(jnp.int32, (1, LANES), 1)

        @pl.when(h == 0)
        def _():
            @pl.when(i == 0)
            def _():
                carry_ref[...] = jnp.zeros_like(carry_ref)

            lf = jnp.where(lane < FOX_HEADS, jax.nn.log_sigmoid(f_ref[...] + bf_ref[0:1, :]), 0.0)
            tri = (lax.broadcasted_iota(jnp.int32, (t, t), 0) >= lax.broadcasted_iota(jnp.int32, (t, t), 1)).astype(F32)
            cum = _dot_onehot(tri, lf, onehot="a") + carry_ref[0:1, :]
            carry_ref[0:1, :] = cum[t - 1:t, :]
            cum_ref[...] = cum
            hi = _mx(cum)
            r1 = cum - hi.astype(F32)
            mid = _mx(r1)
            parts_ref[:, 0:LANES] = hi
            parts_ref[:, LANES:2 * LANES] = mid
            parts_ref[:, 2 * LANES:3 * LANES] = _mx(r1 - mid.astype(F32))

        parts = parts_ref[...]
        qa = _dot(jnp.concatenate([q_ref[...], parts], axis=1), selq_ref[...])
        qa = qa + jnp.where((lane >= BIAS_K[0]) & (lane <= BIAS_K[2]), 1.0, 0.0)
        ka = _dot(jnp.concatenate([k_ref[...], parts], axis=1), selk_ref[...])
        ka = ka + jnp.where((lane >= BIAS_Q[0]) & (lane <= BIAS_Q[2]), 1.0, 0.0)
        va = _dot(v_ref[...], selv_ref[...]) + jnp.where(lane == ONES_V, 1.0, 0.0)
        qa_ref[...] = _mx(qa)
        qat_ref[...] = _mx(qa.T)
        ka_ref[...] = _mx(ka)
        kat_ref[...] = _mx(ka.T)
        va_ref[...] = _mx(va)
        vat_ref[...] = _mx(va.T)

        def longest(rows_):
            sq = jnp.where(lane < FOX_HEAD_DIM, rows_ * rows_, 0.0)
            return jnp.sqrt(jnp.max(jnp.sum(sq, axis=1, keepdims=True), axis=0, keepdims=True))

        mine = lane == h
        cum = cum_ref[...]
        top = jnp.max(jnp.max(jnp.where(mine, cum, -jnp.inf), axis=1, keepdims=True), axis=0, keepdims=True)
        low = jnp.min(jnp.min(jnp.where(mine, cum, jnp.inf), axis=1, keepdims=True), axis=0, keepdims=True)
        row = lax.broadcasted_iota(jnp.int32, (8, LANES), 0)
        stats_ref[...] = jnp.where(row == 0, longest(qa), jnp.where(row == 1, longest(ka),
                                                                    jnp.where(row == 2, top, low)))

    rows = jax.ShapeDtypeStruct((FOX_HEADS, nb, t, LANES), MXU_DTYPE)
    cols = jax.ShapeDtypeStruct((FOX_HEADS, nb, LANES, t), MXU_DTYPE)
    rspec = pl.BlockSpec((None, None, t, LANES), lambda i, h: (h, i, 0, 0))
    cspec = pl.BlockSpec((None, None, LANES, t), lambda i, h: (h, i, 0, 0))
    npair = FOX_HEADS // 2
    res = pl.pallas_call(
        _carrying(body, 8, 7, 3, carry, (nb, FOX_HEADS)), name="fox_prep", grid=(nb, FOX_HEADS),
        in_specs=[pl.BlockSpec((t, LANES), lambda i, h: (i, h // 2)),
                  pl.BlockSpec((t, LANES), lambda i, h: (i, npair + h // 2)),
                  pl.BlockSpec((t, LANES), lambda i, h: (i, 2 * npair + h // 2)),
                  pl.BlockSpec((t, LANES), lambda i, h: (i, 0)),
                  pl.BlockSpec((8, LANES), lambda i, h: (0, 0)),
                  pl.BlockSpec((None, 512, LANES), lambda i, h: (h, 0, 0)),
                  pl.BlockSpec((None, 512, LANES), lambda i, h: (h, 0, 0)),
                  pl.BlockSpec((None, LANES, LANES), lambda i, h: (h % 2, 0, 0))] + [anywhere] * len(carry),
        out_specs=[rspec, cspec, rspec, cspec, rspec, cspec,
                   pl.BlockSpec((None, None, 8, LANES), lambda i, h: (h, i, 0, 0))] + [anywhere] * len(carry),
        out_shape=[rows, cols, rows, cols, rows, cols, jax.ShapeDtypeStruct((FOX_HEADS, nb, 8, LANES), F32)]
        + _exchange_shapes(carry),
        scratch_shapes=[pltpu.VMEM((t, 3 * LANES), MXU_DTYPE), pltpu.VMEM((8, LANES), F32),
                        pltpu.VMEM((t, LANES), F32)] + (_exchange_sems(carry) if carry else []),
        compiler_params=_params(VMEM_BIG),
    )(qkv, qkv, qkv, f, bf, _mx(jnp.asarray(selq)), _mx(jnp.asarray(selk)), _mx(jnp.asarray(selv)),
      *[arr for _, arr in carry])
    return (*res[:7], list(res[7:]))


def _fox_active(stats):
    qn, kn, top, low = (stats[:, :, r, 0] for r in range(4))
    nb = qn.shape[1]
    gap = qn[:, :, None] * kn[:, None, :] + top[:, :, None] - low[:, None, :] + (qn * kn)[:, :, None]
    keep = (gap > -SKIP_MARGIN) | jnp.eye(nb, dtype=bool)[None]
    return jnp.where(keep, 1.0, 0.0).astype(F32).reshape(qn.shape[0], nb * nb)


def _causal_allow(t):
    return lax.broadcasted_iota(jnp.int32, (t, t), 0) <= lax.broadcasted_iota(jnp.int32, (t, t), 1)


def _fox_attn_fwd(qat, ka, vat, active):
    heads, nb, _, t = qat.shape

    def body(act_ref, qat_ref, ka_ref, vat_ref, ot_ref, lse_ref, acc_ref, m_ref, kbuf_ref, vbuf_ref, fsems):
        h, i = pl.program_id(0), pl.program_id(1)
        m_ref[...] = jnp.full_like(m_ref, -jnp.inf)
        acc_ref[...] = jnp.zeros_like(acc_ref)

        def key_blocks(j, slot):
            return [pltpu.make_async_copy(ka_ref.at[h, j], kbuf_ref.at[slot], fsems.at[0, slot]),
                    pltpu.make_async_copy(vat_ref.at[h, j], vbuf_ref.at[slot], fsems.at[1, slot])]

        def runs(j):
            return (j == i) | (act_ref[h, i * nb + j] > 0.5)

        def step(slot, diagonal):
            st = _dot(kbuf_ref[slot], qat_ref[...])
            if diagonal:
                st = jnp.where(_causal_allow(t), st, -jnp.inf)
            m_old = m_ref[...]
            m_new = jnp.maximum(m_old, jnp.max(st, axis=0, keepdims=True))
            pt = jnp.exp(st - m_new)
            acc_ref[...] = acc_ref[...] * jnp.exp(m_old - m_new) + _dot(vbuf_ref[slot], _mx(pt))
            m_ref[...] = m_new

        @pl.when(runs(0))
        def _():
            for cp in key_blocks(0, 0):
                cp.start()

        def earlier(j, c):
            slot = j % 2

            @pl.when(runs(j + 1))
            def _():
                for cp in key_blocks(j + 1, 1 - slot):
                    cp.start()

            @pl.when(runs(j))
            def _():
                for cp in key_blocks(j, slot):
                    cp.wait()
                step(slot, False)

            return c

        lax.fori_loop(0, i, earlier, 0)
        for cp in key_blocks(i, i % 2):
            cp.wait()
        step(i % 2, True)
        acc = acc_ref[...]
        denom = acc[ONES_V:ONES_V + 1, :]
        ot_ref[...] = _mx(acc / denom)
        lse_ref[...] = m_ref[...] + jnp.log(denom)

    anywhere = pl.BlockSpec(memory_space=pl.ANY)
    qspec = pl.BlockSpec((None, None, LANES, t), lambda h, i: (h, i, 0, 0))
    return pl.pallas_call(
        body, name="fox_attn_fwd", grid=(heads, nb),
        in_specs=[pl.BlockSpec(memory_space=pltpu.SMEM), qspec, anywhere, anywhere],
        out_specs=[qspec, pl.BlockSpec((None, None, 1, t), lambda h, i: (h, i, 0, 0))],
        out_shape=[jax.ShapeDtypeStruct((heads, nb, LANES, t), MXU_DTYPE),
                   jax.ShapeDtypeStruct((heads, nb, 1, t), F32)],
        scratch_shapes=[pltpu.VMEM((LANES, t), F32), pltpu.VMEM((1, t), F32), pltpu.VMEM((2, t, LANES), MXU_DTYPE),
                        pltpu.VMEM((2, LANES, t), MXU_DTYPE), pltpu.SemaphoreType.DMA((2, 2))],
        compiler_params=_params(VMEM_BIG),
    )(active, qat, ka, vat)


def _fox_attn_bwd(qa, qat, ka, kat, va, ot, lse, do, dot_, active):
    heads, nb, t, _ = qa.shape

    def body(act_ref, qa_ref, qat_ref, ka_ref, kat_ref, va_ref, ot_ref, lse_ref, do_ref, dot_ref,
             dqt_ref, dka_ref, dva_ref, rows_ref, cols_ref, lseb_ref, fsems):
        h, j = pl.program_id(0), pl.program_id(1)

        @pl.when(j == 0)
        def _():
            dqt_ref[...] = jnp.zeros_like(dqt_ref)

        def query_blocks(i, slot, head=None):
            hh = h if head is None else head
            cps = [pltpu.make_async_copy(src.at[hh, i], rows_ref.at[slot, n], fsems.at[n, slot])
                   for n, src in enumerate((qa_ref, do_ref))]
            cps += [pltpu.make_async_copy(src.at[hh, i], cols_ref.at[slot, n], fsems.at[2 + n, slot])
                    for n, src in enumerate((qat_ref, ot_ref, dot_ref))]
            return cps + [pltpu.make_async_copy(lse_ref.at[hh, i], lseb_ref.at[slot], fsems.at[5, slot])]

        def runs(i):
            return (i == j) | (act_ref[h, i * nb + j] > 0.5)

        def step(i, slot, diagonal):
            st = _dot(ka_ref[...], cols_ref[slot, 0])
            dot_v = cols_ref[slot, 2]
            delta = jnp.sum(cols_ref[slot, 1].astype(F32) * dot_v.astype(F32), axis=0, keepdims=True)
            pt = jnp.exp(st - lseb_ref[slot])
            if diagonal:
                pt = jnp.where(_causal_allow(t), pt, 0.0)
            dsm = _mx(pt * (_dot(va_ref[...], dot_v) - delta))
            upd_v = _dot(_mx(pt), rows_ref[slot, 1])
            upd_k = _dot(dsm, rows_ref[slot, 0])
            if diagonal:
                dva_ref[...] = upd_v
                dka_ref[...] = upd_k
            else:
                dva_ref[...] += upd_v
                dka_ref[...] += upd_k
            dqt_ref[i] += _dot(kat_ref[...], dsm)

        def visit(i, slot, after, diagonal):
            nxt = jnp.minimum(i + 1, nb - 1)

            @pl.when((i + 1 < nb) & runs(nxt))
            def _():
                for cp in query_blocks(nxt, after):
                    cp.start()

            @pl.when(runs(i))
            def _():
                for cp in query_blocks(i, slot):
                    cp.wait()
                step(i, slot, diagonal)

        first = (h == 0) & (j == 0)
        last = (h == heads - 1) & (j == nb - 1)

        @pl.when(first)
        def _():
            for cp in query_blocks(j, 2):
                cp.start()

        visit(j, 2, 0, True)

        @pl.when(jnp.logical_not(last))
        def _():
            wrap = j == nb - 1
            for cp in query_blocks(jnp.where(wrap, 0, j + 1), 2, jnp.where(wrap, h + 1, h)):
                cp.start()

        def later(i, c):
            slot = (i - j - 1) % 2
            visit(i, slot, 1 - slot, False)
            return c

        lax.fori_loop(j + 1, nb, later, 0)

    def at_k(shape):
        return pl.BlockSpec((None, None) + shape, lambda h, j: (h, j, 0, 0))

    anywhere = pl.BlockSpec(memory_space=pl.ANY)
    return pl.pallas_call(
        body, name="fox_attn_bwd", grid=(heads, nb),
        in_specs=[pl.BlockSpec(memory_space=pltpu.SMEM),
                  anywhere, anywhere, at_k((t, LANES)), at_k((LANES, t)), at_k((t, LANES)),
                  anywhere, anywhere, anywhere, anywhere],
        out_specs=[pl.BlockSpec((None, nb, LANES, t), lambda h, j: (h, 0, 0, 0)), at_k((t, LANES)), at_k((t, LANES))],
        out_shape=[jax.ShapeDtypeStruct((heads, nb, LANES, t), F32),
                   jax.ShapeDtypeStruct((heads, nb, t, LANES), F32),
                   jax.ShapeDtypeStruct((heads, nb, t, LANES), F32)],
        scratch_shapes=[pltpu.VMEM((3, 2, t, LANES), MXU_DTYPE), pltpu.VMEM((3, 3, LANES, t), MXU_DTYPE),
                        pltpu.VMEM((3, 1, t), F32), pltpu.SemaphoreType.DMA((6, 3))],
        compiler_params=_params(VMEM_BIG),
    )(active, qa, qat, ka, kat, va, ot, lse, do, dot_)


def _fox_post(dqt, dka, dva, f, bf, carry=()):
    heads, nb, t, _ = dka.shape
    s = nb * t
    _, _, _, put = _fox_constants()
    carry = list(carry)
    anywhere = pl.BlockSpec(memory_space=pl.ANY)

    def body(dqt_ref, dka_ref, dva_ref, f_ref, bf_ref, put_ref, dq_ref, dk_ref, dv_ref, df_ref, sums_ref,
             dc_ref, carry_ref):
        i, h = pl.program_id(0), pl.program_id(1)

        @pl.when((i == 0) & (h == 0))
        def _():
            carry_ref[...] = jnp.zeros_like(carry_ref)
            sums_ref[...] = jnp.zeros_like(sums_ref)

        @pl.when(h == 0)
        def _():
            dc_ref[...] = jnp.zeros_like(dc_ref)

        dqt_v = dqt_ref[...]
        dka_v = dka_ref[...]
        term_q = _dot(_mx(dqt_v), put_ref[0], TN)
        term_k = _dot(_mx(dka_v), put_ref[1])
        term_v = _dot(_mx(dva_ref[...]), put_ref[1])

        @pl.when(h % 2 == 0)
        def _():
            dq_ref[...] = _mx(term_q)
            dk_ref[...] = _mx(term_k)
            dv_ref[...] = _mx(term_v)

        @pl.when(h % 2 == 1)
        def _():
            dq_ref[...] += _mx(term_q)
            dk_ref[...] += _mx(term_k)
            dv_ref[...] += _mx(term_v)

        dcum = dqt_v[BIAS_Q[0]:BIAS_Q[0] + 1, :] - dka_v.T[BIAS_K[0]:BIAS_K[0] + 1, :]
        head_row = lax.broadcasted_iota(jnp.int32, (heads, 1), 0) == h
        dc_ref[...] += jnp.where(head_row, dcum, 0.0)

        @pl.when(h == heads - 1)
        def _():
            later = (lax.broadcasted_iota(jnp.int32, (t, t), 0) >= lax.broadcasted_iota(jnp.int32, (t, t), 1)).astype(F32)
            dlf_t = _dot_onehot(dc_ref[...], later) + carry_ref[:, 0:1]
            carry_ref[...] = jnp.broadcast_to(dlf_t[:, 0:1], carry_ref.shape)
            dlf = jnp.concatenate([dlf_t, jnp.zeros((LANES - heads, t), F32)], axis=0).T
            lane = lax.broadcasted_iota(jnp.int32, (1, LANES), 1)
            df = jnp.where(lane < heads, dlf * jax.nn.sigmoid(-(f_ref[...] + bf_ref[0:1, :])), 0.0)
            df_ref[...] = _mx(df)
            sums_ref[0:1, :] += jnp.sum(df, axis=0, keepdims=True)

    rev = lambda i: nb - 1 - i
    pair_spec = pl.BlockSpec((t, LANES), lambda i, h: (rev(i), h // 2))
    blk = pl.BlockSpec((t, LANES), lambda i, h: (rev(i), 0))
    hd = jax.ShapeDtypeStruct((s, D_MODEL), MXU_DTYPE)
    res = pl.pallas_call(
        _carrying(body, 6, 5, 2, carry, (nb, heads)), name="fox_post", grid=(nb, heads),
        in_specs=[pl.BlockSpec((None, None, LANES, t), lambda i, h: (h, rev(i), 0, 0)),
                  pl.BlockSpec((None, None, t, LANES), lambda i, h: (h, rev(i), 0, 0)),
                  pl.BlockSpec((None, None, t, LANES), lambda i, h: (h, rev(i), 0, 0)),
                  blk, pl.BlockSpec((8, LANES), lambda i, h: (0, 0)),
                  pl.BlockSpec((None, 2, LANES, LANES), lambda i, h: (h % 2, 0, 0, 0))] + [anywhere] * len(carry),
        out_specs=[pair_spec, pair_spec, pair_spec, blk, pl.BlockSpec((8, LANES), lambda i, h: (0, 0))]
        + [anywhere] * len(carry),
        out_shape=[hd, hd, hd, jax.ShapeDtypeStruct((s, LANES), MXU_DTYPE), jax.ShapeDtypeStruct((8, LANES), F32)]
        + _exchange_shapes(carry),
        scratch_shapes=[pltpu.VMEM((heads, t), F32), pltpu.VMEM((heads, LANES), F32)]
        + (_exchange_sems(carry) if carry else []),
        compiler_params=_params(VMEM_BIG),
    )(dqt, dka, dva, f, bf, _mx(jnp.asarray(put)), *[arr for _, arr in carry])
    return (*res[:5], list(res[5:]))


def _fox_weights(w_in, w_o):
    wqkv = w_in[:, :3 * D_MODEL]
    wf = jnp.pad(w_in[:, 3 * D_MODEL:], ((0, 0), (0, LANES - FOX_HEADS)))
    wo_heads = w_o.reshape(FOX_HEADS, FOX_HEAD_DIM, D_MODEL)
    wo_a = jnp.pad(wo_heads, ((0, 0), (0, LANES - FOX_HEAD_DIM), (0, 0)))
    wo_rows = wo_a.reshape(FOX_HEADS * LANES, D_MODEL)
    return dict(wqkv=wqkv, wf=wf, wqkv_t=wqkv.T, wf_t=wf.T, wo_rows=wo_rows, wo_rows_t=wo_rows.T)


def _fox_out(ot, wo_rows):
    heads, nb, _, t = ot.shape

    def body(ot_ref, w_ref, y_ref):
        y_ref[...] = _dot(ot_ref[...].reshape(heads * LANES, t), w_ref[...], TN)

    return pl.pallas_call(
        body, name="fox_out", grid=(nb,),
        in_specs=[pl.BlockSpec((heads, None, LANES, t), lambda i: (0, i, 0, 0)),
                  pl.BlockSpec((heads * LANES, D_MODEL), lambda i: (0, 0))],
        out_specs=pl.BlockSpec((t, D_MODEL), lambda i: (i, 0)),
        out_shape=jax.ShapeDtypeStruct((nb * t, D_MODEL), F32),
        compiler_params=_params(VMEM_BIG),
    )(ot, wo_rows)


def _fox_do(dy, wo_rows_t, nb, t):
    heads = FOX_HEADS

    def body(dy_ref, w_ref, do_ref, dot_ref):
        val = _dot(dy_ref[...], w_ref[...])
        for h in range(heads):
            blk = val[:, h * LANES:(h + 1) * LANES]
            do_ref[h] = _mx(blk)
            dot_ref[h] = _mx(blk.T)

    return pl.pallas_call(
        body, name="fox_do", grid=(nb,),
        in_specs=[pl.BlockSpec((t, D_MODEL), lambda i: (i, 0)),
                  pl.BlockSpec((D_MODEL, heads * LANES), lambda i: (0, 0))],
        out_specs=[pl.BlockSpec((heads, None, t, LANES), lambda i: (0, i, 0, 0)),
                   pl.BlockSpec((heads, None, LANES, t), lambda i: (0, i, 0, 0))],
        out_shape=[jax.ShapeDtypeStruct((heads, nb, t, LANES), MXU_DTYPE),
                   jax.ShapeDtypeStruct((heads, nb, LANES, t), MXU_DTYPE)],
        compiler_params=_params(VMEM_BIG),
    )(dy, wo_rows_t)


def _fox_dwo(ot, dy):
    heads, nb, _, t = ot.shape

    def body(ot_ref, dy_ref, o_ref):
        part = _dot(ot_ref[...].reshape(heads * LANES, t), dy_ref[...])

        @pl.when(pl.program_id(0) == 0)
        def _():
            o_ref[...] = part

        @pl.when(pl.program_id(0) > 0)
        def _():
            o_ref[...] += part

    return pl.pallas_call(
        body, name="fox_dwo", grid=(nb,),
        in_specs=[pl.BlockSpec((heads, None, LANES, t), lambda i: (0, i, 0, 0)),
                  pl.BlockSpec((t, D_MODEL), lambda i: (i, 0))],
        out_specs=pl.BlockSpec((heads * LANES, D_MODEL), lambda i: (0, 0)),
        out_shape=jax.ShapeDtypeStruct((heads * LANES, D_MODEL), F32),
        compiler_params=_params(VMEM_BIG),
    )(ot, dy)


def _fox_fwd(u, w, bf, carry=()):
    qkv = _mm_nn("fox_qkv", u, w["wqkv"], MXU_DTYPE)
    f = _mm_nn("fox_f", u, w["wf"], F32)
    qa, qat, ka, kat, va, vat, stats, carried = _fox_prep(qkv, f, bf, carry)
    ot, lse = _fox_attn_fwd(qat, ka, vat, _fox_active(stats))
    y = _fox_out(ot, w["wo_rows"])
    return y, dict(f=f, qa=qa, qat=qat, ka=ka, kat=kat, va=va, ot=ot, lse=lse, stats=stats), carried


def _fox_bwd(dy, u, w, bf, res, carry=()):
    heads, nb, t, _ = res["qa"].shape
    do, dot_ = _fox_do(dy, w["wo_rows_t"], nb, t)
    dwo_a = _fox_dwo(res["ot"], dy).reshape(heads, LANES, D_MODEL)
    dqt, dka, dva = _fox_attn_bwd(res["qa"], res["qat"], res["ka"], res["kat"], res["va"], res["ot"],
                                  res["lse"], do, dot_, _fox_active(res["stats"]))
    dq, dk, dv, df, sums, carried = _fox_post(dqt, dka, dva, res["f"], bf, carry)
    dw_in = jnp.concatenate(
        [_mm_tn("fox_dw_q", u, dq), _mm_tn("fox_dw_k", u, dk), _mm_tn("fox_dw_v", u, dv),
         _mm_tn("fox_dw_f", u, df)[:, :FOX_HEADS]], axis=1)
    dw_o = dwo_a[:, :FOX_HEAD_DIM, :].reshape(D_MODEL, D_MODEL)
    wt = w["wqkv_t"]
    du, ex_in, ex_o = _mm_sum(
        "fox_du", [(dq, wt[:D_MODEL]), (dk, wt[D_MODEL:2 * D_MODEL]), (dv, wt[2 * D_MODEL:]), (df, w["wf_t"])],
        carry=[(True, _col_slots(dw_in[None])), (True, _row_slots(dw_o[None]))])
    return du, sums, carried + [ex_in, ex_o]


def _dsilu(v):
    sg = jax.nn.sigmoid(v)
    return sg * (1.0 + v * (1.0 - sg))


def _conv_taps(scr_ref, w_ref, rows, base):
    acc = None
    for k in range(SSM_CONV):
        term = scr_ref[pl.ds(base - (SSM_CONV - 1) + k, rows), :] * w_ref[k:k + 1, :]
        acc = term if acc is None else acc + term
    return acc


def _conv_fwd(zx, cw, cb):
    s = zx.shape[0]
    tb = _tile(s, 512)
    half = SSM_CONV_DIM // 2
    hb = tb // SUBLANES

    def body(x_ref, halo_ref, w_ref, b_ref, o_ref, scr_ref):
        i = pl.program_id(0)
        scr_ref[pl.ds(0, SUBLANES), :] = jnp.where(i > 0, halo_ref[...], 0.0)
        scr_ref[pl.ds(SUBLANES, tb), :] = x_ref[...]
        o_ref[...] = jax.nn.silu(_conv_taps(scr_ref, w_ref, tb, SUBLANES) + b_ref[0:1, :])

    return pl.pallas_call(
        body, name="ssd_conv_fwd", grid=(s // tb, 2),
        in_specs=[pl.BlockSpec((tb, half), lambda i, j: (i, 1 + j)),
                  pl.BlockSpec((SUBLANES, half), lambda i, j: (jnp.maximum(i * hb - 1, 0), 1 + j)),
                  pl.BlockSpec((8, half), lambda i, j: (0, j)),
                  pl.BlockSpec((8, half), lambda i, j: (0, j))],
        out_specs=pl.BlockSpec((tb, half), lambda i, j: (i, j)),
        out_shape=jax.ShapeDtypeStruct((s, SSM_CONV_DIM), F32),
        scratch_shapes=[pltpu.VMEM((tb + SUBLANES, half), F32)],
    )(zx, zx, cw, cb)


def _conv_bwd_pre(zx, dxc, cw, cb):
    s = zx.shape[0]
    tb = _tile(s, 512)
    half = SSM_CONV_DIM // 2
    hb = tb // SUBLANES

    def body(x_ref, halo_ref, d_ref, w_ref, b_ref, o_ref, sums_ref, scr_ref):
        i = pl.program_id(1)

        @pl.when(i == 0)
        def _():
            sums_ref[...] = jnp.zeros_like(sums_ref)

        scr_ref[pl.ds(0, SUBLANES), :] = jnp.where(i > 0, halo_ref[...], 0.0)
        scr_ref[pl.ds(SUBLANES, tb), :] = x_ref[...]
        pre = _conv_taps(scr_ref, w_ref, tb, SUBLANES) + b_ref[0:1, :]
        dpre = d_ref[...] * _dsilu(pre)
        o_ref[...] = dpre
        for k in range(SSM_CONV):
            shifted = scr_ref[pl.ds(SUBLANES - (SSM_CONV - 1) + k, tb), :]
            sums_ref[k:k + 1, :] += jnp.sum(dpre * shifted, axis=0, keepdims=True)
        sums_ref[SSM_CONV:SSM_CONV + 1, :] += jnp.sum(dpre, axis=0, keepdims=True)

    return pl.pallas_call(
        body, name="ssd_conv_bwd_pre", grid=(2, s // tb),
        in_specs=[pl.BlockSpec((tb, half), lambda j, i: (i, 1 + j)),
                  pl.BlockSpec((SUBLANES, half), lambda j, i: (jnp.maximum(i * hb - 1, 0), 1 + j)),
                  pl.BlockSpec((tb, half), lambda j, i: (i, j)),
                  pl.BlockSpec((8, half), lambda j, i: (0, j)),
                  pl.BlockSpec((8, half), lambda j, i: (0, j))],
        out_specs=[pl.BlockSpec((tb, half), lambda j, i: (i, j)),
                   pl.BlockSpec((8, half), lambda j, i: (0, j))],
        out_shape=[jax.ShapeDtypeStruct((s, SSM_CONV_DIM), F32), jax.ShapeDtypeStruct((8, SSM_CONV_DIM), F32)],
        scratch_shapes=[pltpu.VMEM((tb + SUBLANES, half), F32)],
    )(zx, zx, dxc, cw, cb)


def _conv_bwd_x(dpre, cw):
    s = dpre.shape[0]
    tb = _tile(s, 512)
    hb = tb // SUBLANES
    nb = s // tb

    def body(d_ref, halo_ref, w_ref, o_ref, scr_ref):
        i = pl.program_id(0)
        scr_ref[pl.ds(0, tb), :] = d_ref[...]
        scr_ref[pl.ds(tb, SUBLANES), :] = jnp.where(i < nb - 1, halo_ref[...], 0.0)
        acc = None
        for k in range(SSM_CONV):
            term = scr_ref[pl.ds(SSM_CONV - 1 - k, tb), :] * w_ref[k:k + 1, :]
            acc = term if acc is None else acc + term
        o_ref[...] = _mx(acc)

    return pl.pallas_call(
        body, name="ssd_conv_bwd_x", grid=(nb,),
        in_specs=[pl.BlockSpec((tb, SSM_CONV_DIM), lambda i: (i, 0)),
                  pl.BlockSpec((SUBLANES, SSM_CONV_DIM), lambda i: (jnp.minimum((i + 1) * hb, s // SUBLANES - 1), 0)),
                  pl.BlockSpec((8, SSM_CONV_DIM), lambda i: (0, 0))],
        out_specs=pl.BlockSpec((tb, SSM_CONV_DIM), lambda i: (i, 0)),
        out_shape=jax.ShapeDtypeStruct((s, SSM_CONV_DIM), MXU_DTYPE),
        scratch_shapes=[pltpu.VMEM((tb + SUBLANES, SSM_CONV_DIM), F32)],
        compiler_params=_params(VMEM_BIG),
    )(dpre, dpre, cw)


def _expand_constants():
    ex = np.zeros((LANES, SSM_D_INNER), np.float32)
    for h in range(SSM_HEADS):
        ex[h, h * 64:(h + 1) * 64] = 1.0
    return ex, np.ascontiguousarray(ex.T)


def _ssd_common(dtr_ref, par_ref, ex_ref, xc_ref):
    lc = SSM_CHUNK
    lane = lax.broadcasted_iota(jnp.int32, (1, LANES), 1)
    is_head = lane < SSM_HEADS
    par = par_ref[...]
    pre = dtr_ref[...] + par[0:1, :]
    dt = jnp.where(is_head, jax.nn.softplus(pre), 0.0)
    a = jnp.where(is_head, -jnp.exp(par[1:2, :]), 0.0)
    tri_b = lax.broadcasted_iota(jnp.int32, (lc, lc), 0) >= lax.broadcasted_iota(jnp.int32, (lc, lc), 1)
    tri = tri_b.astype(F32)
    da = dt * a
    acs = _dot_onehot(tri, da, onehot="a")
    acs_t = _dot_onehot(da, tri, (((0,), (1,)), ((), ())))
    wide = _dot_onehot(jnp.concatenate([dt, acs, par], axis=0), ex_ref[...])
    dt_x, acs_x, d_x = wide[0:lc], wide[lc:2 * lc], wide[2 * lc + 2:2 * lc + 3]
    last_x = acs_x[lc - 1:lc, :]
    xs = xc_ref[:, 0:SSM_D_INNER]
    return dict(pre=pre, dt=dt, a=a, tri_b=tri_b, tri=tri, acs=acs, acs_t=acs_t, dt_x=dt_x, d_x=d_x, xs=xs,
                xdt=xs * dt_x, e_x=jnp.exp(acs_x), dte_x=jnp.exp(last_x - acs_x), cd_x=jnp.exp(last_x),
                is_head=is_head)


def _decay_in(q, h):
    seg = q["acs"][:, h:h + 1] - q["acs_t"][h:h + 1, :]
    return jnp.exp(jnp.where(q["tri_b"], seg, -jnp.inf))


def _ssd_scan_fwd(xc, dtr, par):
    s = xc.shape[0]
    lc = SSM_CHUNK
    nc = s // lc
    ex, _ = _expand_constants()

    def body(xc_ref, dtr_ref, par_ref, ex_ref, y_ref, prev_ref, st_ref):
        @pl.when(pl.program_id(0) == 0)
        def _():
            st_ref[...] = jnp.zeros_like(st_ref)

        q = _ssd_common(dtr_ref, par_ref, ex_ref, xc_ref)
        lane = lax.broadcasted_iota(jnp.int32, (1, LANES), 1)
        for g in range(SSM_GROUPS):
            sl = slice(g * GROUP_W, (g + 1) * GROUP_W)
            bg = _mx(xc_ref[:, SSM_D_INNER + g * SSM_STATE:SSM_D_INNER + (g + 1) * SSM_STATE])
            cg = _mx(xc_ref[:, SSM_D_INNER + (SSM_GROUPS + g) * SSM_STATE:SSM_D_INNER + (SSM_GROUPS + g + 1) * SSM_STATE])
            gm = _dot(cg, bg, NT)
            prev = st_ref[g]
            prev_ref[g] = prev
            yoff = _dot(cg, _mx(prev)) * q["e_x"][:, sl]
            st_ref[g] = prev * q["cd_x"][:, sl] + _dot(bg, _mx(q["xdt"][:, sl] * q["dte_x"][:, sl]), TN)
            pairs = []
            for pr in range(2):
                xp = _mx(q["xdt"][:, g * GROUP_W + pr * LANES:g * GROUP_W + (pr + 1) * LANES])
                both = [_dot(_mx(gm * _decay_in(q, 4 * g + 2 * pr + r2)), xp) for r2 in range(2)]
                pairs.append(jnp.where(lane < 64, both[0], both[1]))
            y_ref[:, sl] = jnp.concatenate(pairs, axis=1) + yoff + q["xs"][:, sl] * q["d_x"][:, sl]

    return pl.pallas_call(
        body, name="ssd_scan_fwd", grid=(nc,),
        in_specs=[pl.BlockSpec((lc, SSM_CONV_DIM), lambda c: (c, 0)),
                  pl.BlockSpec((lc, LANES), lambda c: (c, 0)),
                  pl.BlockSpec((8, LANES), lambda c: (0, 0)),
                  pl.BlockSpec((LANES, SSM_D_INNER), lambda c: (0, 0))],
        out_specs=[pl.BlockSpec((lc, SSM_D_INNER), lambda c: (c, 0)),
                   pl.BlockSpec((None, SSM_GROUPS, SSM_STATE, GROUP_W), lambda c: (c, 0, 0, 0))],
        out_shape=[jax.ShapeDtypeStruct((s, SSM_D_INNER), F32),
                   jax.ShapeDtypeStruct((nc, SSM_GROUPS, SSM_STATE, GROUP_W), F32)],
        scratch_shapes=[pltpu.VMEM((SSM_GROUPS, SSM_STATE, GROUP_W), F32)],
        compiler_params=_params(VMEM_BIG),
    )(xc, dtr, par, _mx(jnp.asarray(ex)))


def _ssd_scan_bwd(dy, xc, dtr, par, prev, carry=()):
    s = xc.shape[0]
    lc = SSM_CHUNK
    nc = s // lc
    ex, ex_t = _expand_constants()
    carry = list(carry)
    anywhere = pl.BlockSpec(memory_space=pl.ANY)

    def body(dy_ref, xc_ref, dtr_ref, par_ref, prev_ref, ex_ref, ext_ref, dxc_ref, ddtr_ref, sums_ref,
             gst_ref, tacs_ref, tdt_ref, tdd_ref):
        @pl.when(pl.program_id(0) == 0)
        def _():
            gst_ref[...] = jnp.zeros_like(gst_ref)
            sums_ref[...] = jnp.zeros_like(sums_ref)

        q = _ssd_common(dtr_ref, par_ref, ex_ref, xc_ref)
        lane = lax.broadcasted_iota(jnp.int32, (1, LANES), 1)
        row = lax.broadcasted_iota(jnp.int32, (lc, 1), 0)
        dacs_rows = jnp.zeros((lc, LANES), F32)
        dacs_cols_t = jnp.zeros((LANES, lc), F32)
        for g in range(SSM_GROUPS):
            sl = slice(g * GROUP_W, (g + 1) * GROUP_W)
            b_lo = SSM_D_INNER + g * SSM_STATE
            c_lo = SSM_D_INNER + (SSM_GROUPS + g) * SSM_STATE
            bg = _mx(xc_ref[:, b_lo:b_lo + SSM_STATE])
            cg = _mx(xc_ref[:, c_lo:c_lo + SSM_STATE])
            dyg = dy_ref[:, sl]
            xsg, xdtg = q["xs"][:, sl], q["xdt"][:, sl]
            eg, dteg, cdg = q["e_x"][:, sl], q["dte_x"][:, sl], q["cd_x"][:, sl]
            prevg = prev_ref[g]
            gs = gst_ref[g]
            prevm, gsm = _mx(prevg), _mx(gs)
            tdd_ref[:, sl] = dyg * xsg
            dxs = dyg * q["d_x"][:, sl]
            t_acs = dyg * _dot(cg, prevm) * eg
            dcp = _mx(dyg * eg)
            dc = _dot(dcp, prevm, NT)
            dprev = _dot(cg, dcp, TN)
            db = _dot(_mx(xdtg * dteg), gsm, NT)
            dx2 = _dot(bg, gsm)
            dxdt = dx2 * dteg
            ddte = dx2 * xdtg * dteg
            t_acs = t_acs - ddte
            last = (jnp.sum(ddte, axis=0, keepdims=True)
                    + jnp.sum(gs * prevg, axis=0, keepdims=True) * cdg)
            gm = _dot(cg, bg, NT)
            dgm = jnp.zeros((lc, lc), F32)
            pair_dx = []
            for pr in range(2):
                lo = g * GROUP_W + pr * LANES
                xp = _mx(q["xdt"][:, lo:lo + LANES])
                dyp = dy_ref[:, lo:lo + LANES]
                both = []
                for r2 in range(2):
                    h = 4 * g + 2 * pr + r2
                    mine = (lane >= 64 * r2) & (lane < 64 * (r2 + 1))
                    lm = _decay_in(q, h)
                    m = gm * lm
                    dm = _dot(_mx(jnp.where(mine, dyp, 0.0)), xp, NT)
                    dgm = dgm + dm * lm
                    w = dm * m
                    dacs_rows = dacs_rows + jnp.sum(w, axis=1, keepdims=True) * (lane == h).astype(F32)
                    head_row = (lax.broadcasted_iota(jnp.int32, (LANES, 1), 0) == h).astype(F32)
                    dacs_cols_t = dacs_cols_t + head_row * jnp.sum(w, axis=0, keepdims=True)
                    both.append(_dot(_mx(m), _mx(dyp), TN))
                pair_dx.append(jnp.where(lane < 64, both[0], both[1]))
            dxdt = dxdt + jnp.concatenate(pair_dx, axis=1)
            dgmm = _mx(dgm)
            dc = dc + _dot(dgmm, bg)
            db = db + _dot(dgmm, cg, TN)
            dxs = dxs + dxdt * q["dt_x"][:, sl]
            tdt_ref[:, sl] = dxdt * xsg
            tacs_ref[:, sl] = t_acs + jnp.where(row == lc - 1, last, 0.0)
            dxc_ref[:, sl] = dxs
            dxc_ref[:, b_lo:b_lo + SSM_STATE] = db
            dxc_ref[:, c_lo:c_lo + SSM_STATE] = dc
            gst_ref[g] = gs * cdg + dprev
        tdd = jnp.broadcast_to(jnp.sum(tdd_ref[...], axis=0, keepdims=True), (8, SSM_D_INNER))
        heads_of = _dot_onehot(jnp.concatenate([tacs_ref[...], tdt_ref[...], tdd], axis=0), ext_ref[...])
        dacs = heads_of[0:lc] + dacs_rows - dacs_cols_t.T
        dda = _dot_onehot(q["tri"], dacs, TN, onehot="a")
        ddt = dda * q["a"] + heads_of[lc:2 * lc]
        ddtr = jnp.where(q["is_head"], ddt * jax.nn.sigmoid(q["pre"]), 0.0)
        ddtr_ref[...] = _mx(ddtr)
        sums_ref[0:1, :] += jnp.sum(ddtr, axis=0, keepdims=True)
        sums_ref[1:2, :] += jnp.sum(dda * q["dt"], axis=0, keepdims=True) * q["a"]
        sums_ref[2:3, :] += heads_of[2 * lc:2 * lc + 1]

    rev = lambda c: nc - 1 - c
    wide = pltpu.VMEM((lc, SSM_D_INNER), F32)
    res = pl.pallas_call(
        _carrying(body, 7, 3, 4, carry, (nc,)), name="ssd_scan_bwd", grid=(nc,),
        in_specs=[pl.BlockSpec((lc, SSM_D_INNER), lambda c: (rev(c), 0)),
                  pl.BlockSpec((lc, SSM_CONV_DIM), lambda c: (rev(c), 0)),
                  pl.BlockSpec((lc, LANES), lambda c: (rev(c), 0)),
                  pl.BlockSpec((8, LANES), lambda c: (0, 0)),
                  pl.BlockSpec((None, SSM_GROUPS, SSM_STATE, GROUP_W), lambda c: (rev(c), 0, 0, 0)),
                  pl.BlockSpec((LANES, SSM_D_INNER), lambda c: (0, 0)),
                  pl.BlockSpec((SSM_D_INNER, LANES), lambda c: (0, 0))] + [anywhere] * len(carry),
        out_specs=[pl.BlockSpec((lc, SSM_CONV_DIM), lambda c: (rev(c), 0)),
                   pl.BlockSpec((lc, LANES), lambda c: (rev(c), 0)),
                   pl.BlockSpec((8, LANES), lambda c: (0, 0))] + [anywhere] * len(carry),
        out_shape=[jax.ShapeDtypeStruct((s, SSM_CONV_DIM), F32), jax.ShapeDtypeStruct((s, LANES), MXU_DTYPE),
                   jax.ShapeDtypeStruct((8, LANES), F32)] + _exchange_shapes(carry),
        scratch_shapes=[pltpu.VMEM((SSM_GROUPS, SSM_STATE, GROUP_W), F32), wide, wide, wide]
        + (_exchange_sems(carry) if carry else []),
        compiler_params=_params(VMEM_BIG),
    )(dy, xc, dtr, par, prev, _mx(jnp.asarray(ex)), _mx(jnp.asarray(ex_t)), *[arr for _, arr in carry])
    return (*res[:3], list(res[3:]))


def _group_norm_parts(yv, zv):
    yg = yv * jax.nn.silu(zv)
    normed, rinvs = [], []
    for g in range(SSM_GROUPS):
        blk = yg[:, g * GROUP_W:(g + 1) * GROUP_W]
        rinv = lax.rsqrt(jnp.mean(blk * blk, axis=-1, keepdims=True) + RMS_EPS)
        normed.append(blk * rinv)
        rinvs.append(rinv)
    return normed, rinvs


def _gnorm_fwd(y, zx, nw):
    s = y.shape[0]
    tb = _tile(s, 512)

    def body(y_ref, z_ref, w_ref, o_ref):
        normed, _ = _group_norm_parts(y_ref[...], z_ref[...])
        for g in range(SSM_GROUPS):
            sl = slice(g * GROUP_W, (g + 1) * GROUP_W)
            o_ref[:, sl] = _mx(normed[g] * w_ref[0:1, sl])

    row = pl.BlockSpec((tb, SSM_D_INNER), lambda i: (i, 0))
    return pl.pallas_call(
        body, name="ssd_gnorm_fwd", grid=(s // tb,),
        in_specs=[row, row, pl.BlockSpec((8, SSM_D_INNER), lambda i: (0, 0))],
        out_specs=row, out_shape=jax.ShapeDtypeStruct((s, SSM_D_INNER), MXU_DTYPE),
    )(y, zx, nw)


def _gnorm_bwd(y, zx, nw, dyn):
    s = y.shape[0]
    tb = _tile(s, 512)

    def body(y_ref, z_ref, w_ref, d_ref, dy_ref, dz_ref, sums_ref):
        @pl.when(pl.program_id(0) == 0)
        def _():
            sums_ref[...] = jnp.zeros_like(sums_ref)

        yv, zv = y_ref[...], z_ref[...]
        normed, rinvs = _group_norm_parts(yv, zv)
        gate = jax.nn.silu(zv)
        dgate = _dsilu(zv)
        for g in range(SSM_GROUPS):
            sl = slice(g * GROUP_W, (g + 1) * GROUP_W)
            dv = d_ref[:, sl]
            n = normed[g]
            sums_ref[0:1, sl] += jnp.sum(dv * n, axis=0, keepdims=True)
            dn = dv * w_ref[0:1, sl]
            dyg = rinvs[g] * (dn - n * jnp.mean(dn * n, axis=-1, keepdims=True))
            dy_ref[:, sl] = dyg * gate[:, sl]
            dz_ref[:, sl] = _mx(dyg * yv[:, sl] * dgate[:, sl])

    row = pl.BlockSpec((tb, SSM_D_INNER), lambda i: (i, 0))
    par = pl.BlockSpec((8, SSM_D_INNER), lambda i: (0, 0))
    return pl.pallas_call(
        body, name="ssd_gnorm_bwd", grid=(s // tb,),
        in_specs=[row, row, par, row], out_specs=[row, row, par],
        out_shape=[jax.ShapeDtypeStruct((s, SSM_D_INNER), F32), jax.ShapeDtypeStruct((s, SSM_D_INNER), MXU_DTYPE),
                   jax.ShapeDtypeStruct((8, SSM_D_INNER), F32)],
    )(y, zx, nw, dyn)


def _rows8(v):
    v = v.reshape(1, -1)
    return jnp.pad(v, ((0, 7), (0, 0)))


def _ssd_weights(w_in, w_out):
    nzx = SSM_D_INNER + SSM_CONV_DIM
    wzx = w_in[:, :nzx]
    wdt = jnp.pad(w_in[:, nzx:], ((0, 0), (0, LANES - SSM_HEADS)))
    return dict(wzx=wzx, wdt=wdt, wzx_t=wzx.T, wdt_t=wdt.T, wout=w_out, wout_t=w_out.T)


def _ssd_fwd(u, w, cw, cb, par, nw):
    zx = _mm_nn("ssd_in_zx", u, w["wzx"], F32)
    dtr = _mm_nn("ssd_in_dt", u, w["wdt"], F32)
    xc = _conv_fwd(zx, cw, cb)
    y, prev = _ssd_scan_fwd(xc, dtr, par)
    yn = _gnorm_fwd(y, zx, nw)
    out = _mm_nn("ssd_out", yn, w["wout"], F32)
    return out, dict(zx=zx, dtr=dtr, xc=xc, y=y, prev=prev, yn=yn)


def _ssd_bwd(dy, u, w, cw, cb, par, nw, res, carry=()):
    dyn = _mm_nn("ssd_dyn", dy, w["wout_t"], F32)
    dw_out = _mm_tn("ssd_dw_out", res["yn"], dy)
    dys, dz, nsum = _gnorm_bwd(res["y"], res["zx"], nw, dyn)
    dxc, ddtr, ssum, carried = _ssd_scan_bwd(dys, res["xc"], res["dtr"], par, res["prev"],
                                             list(carry) + [(True, _row_slots(dw_out[None]))])
    dpre, csum = _conv_bwd_pre(res["zx"], dxc, cw, cb)
    dxbc = _conv_bwd_x(dpre, cw)
    wt = w["wzx_t"]
    du = _mm_sum("ssd_du", [(dz, wt[:SSM_D_INNER]), (dxbc, wt[SSM_D_INNER:]), (ddtr, w["wdt_t"])], tn=512)
    dw_in = jnp.concatenate(
        [_mm_tn("ssd_dw_z", u, dz), _mm_tn("ssd_dw_x", u, dxbc), _mm_tn("ssd_dw_dt", u, ddtr)[:, :SSM_HEADS]], axis=1)
    small = dict(conv_w=csum[:SSM_CONV], conv_b=csum[SSM_CONV], dt_bias=ssum[0, :SSM_HEADS],
                 a_log=ssum[1, :SSM_HEADS], d=ssum[2, :SSM_HEADS], norm_w=nsum[0])
    return du, dw_in, small, carried


def _ada_fwd(c_all, ada_w, ada_b_mine):
    nl, _, ncol = ada_w.shape

    def body(c_ref, w_ref, b_ref, o_ref):
        cond = _mx(jax.nn.silu(c_ref[...]))
        for i in range(nl):
            o_ref[i] = _dot(cond, _mx(w_ref[i])) + b_ref[i:i + 1, :]

    return pl.pallas_call(
        body, name="ada_fwd", out_shape=jax.ShapeDtypeStruct((nl, 2 * N_DEV, ncol), F32),
        compiler_params=_params(VMEM_BIG),
    )(c_all, ada_w, ada_b_mine)


def _ada_bwd(c_all, dmod_cols):
    nl, _, ncol = dmod_cols.shape

    def body(c_ref, d_ref, o_ref):
        cond = _mx(jax.nn.silu(c_ref[...]))
        for i in range(nl):
            o_ref[i] = _dot(cond, _mx(d_ref[i]), TN)

    return pl.pallas_call(
        body, name="ada_bwd", out_shape=jax.ShapeDtypeStruct((nl, D_MODEL, ncol), F32),
        compiler_params=_params(VMEM_BIG),
    )(c_all, dmod_cols)


def _adamw(gslots, w, m, v, name):
    k, r, c = gslots.shape
    tr = _tile(r, 256) if r % 256 == 0 else r
    c1 = 1.0 - ADAM_B1 ** ADAM_STEP
    c2 = 1.0 - ADAM_B2 ** ADAM_STEP

    def body(g_ref, w_ref, m_ref, v_ref, go_ref, d_ref, mo_ref, vo_ref):
        g = g_ref[0]
        for slot in range(1, k):
            g = g + g_ref[slot]
        mn = ADAM_B1 * m_ref[...] + (1.0 - ADAM_B1) * g
        vn = ADAM_B2 * v_ref[...] + (1.0 - ADAM_B2) * jnp.square(g)
        go_ref[...] = g
        mo_ref[...] = mn
        vo_ref[...] = vn
        d_ref[...] = -ADAM_LR * ((mn / c1) / (jnp.sqrt(vn / c2) + ADAM_EPS) + ADAM_WD * w_ref[...])

    row = pl.BlockSpec((tr, c), lambda i: (i, 0))
    shp = jax.ShapeDtypeStruct((r, c), F32)
    return pl.pallas_call(
        body, name=name, grid=(r // tr,),
        in_specs=[pl.BlockSpec((k, tr, c), lambda i: (0, i, 0)), row, row, row],
        out_specs=[row, row, row, row], out_shape=[shp, shp, shp, shp],
        compiler_params=_params(VMEM_BIG),
    )(gslots, w, m, v)


def _adamw_any(gslots, w, m, v, name):
    shape = w.shape
    two_d = (-1, shape[-1])
    k = gslots.shape[0]
    outs = _adamw(gslots.reshape((k,) + w.reshape(two_d).shape), w.reshape(two_d), m.reshape(two_d),
                  v.reshape(two_d), name)
    return tuple(o.reshape(shape) for o in outs)


def _cols_from_slots(g):
    g = jnp.moveaxis(g, 0, -2)
    return g.reshape(g.shape[:-2] + (g.shape[-2] * g.shape[-1],))


def _rows_from_slots(g):
    g = jnp.moveaxis(g, 0, -3)
    return g.reshape(g.shape[:-3] + (g.shape[-3] * g.shape[-2], g.shape[-1]))


def _col_slots(g):
    cs = g.shape[-1] // N_DEV
    return jnp.moveaxis(g.reshape(g.shape[:-1] + (N_DEV, cs)), -2, 0)


def _row_slots(g):
    rs = g.shape[-2] // N_DEV
    return jnp.moveaxis(g.reshape(g.shape[:-2] + (N_DEV, rs, g.shape[-1])), -3, 0)


def _gather_cols(w, name, dtype=None):
    return _cols_from_slots(_all_gather(w.astype(dtype or MXU_DTYPE), name))


def _gather_rows(w, name):
    return _rows_from_slots(_all_gather(_mx(w), name))


def kernel(x, c, ada_w, ada_b, ln_mix_g, ln_mix_b, ln_mlp_g, ln_mlp_b, mlp_w1, mlp_w2, fox_w_in, fox_b_f, fox_w_o, ssm_w_in, ssm_conv_w, ssm_conv_b, ssm_dt_bias, ssm_a_log, ssm_d, ssm_norm_w, ssm_w_out, loss_target, m_ada_w, m_ada_b, m_ln_mix_g, m_ln_mix_b, m_ln_mlp_g, m_ln_mlp_b, m_mlp_w1, m_mlp_w2, m_fox_w_in, m_fox_b_f, m_fox_w_o, m_ssm_w_in, m_ssm_conv_w, m_ssm_conv_b, m_ssm_dt_bias, m_ssm_a_log, m_ssm_d, m_ssm_norm_w, m_ssm_w_out, v_ada_w, v_ada_b, v_ln_mix_g, v_ln_mix_b, v_ln_mlp_g, v_ln_mlp_b, v_mlp_w1, v_mlp_w2, v_fox_w_in, v_fox_b_f, v_fox_w_o, v_ssm_w_in, v_ssm_conv_w, v_ssm_conv_b, v_ssm_dt_bias, v_ssm_a_log, v_ssm_d, v_ssm_norm_w, v_ssm_w_out):
    me = 4 * lax.axis_index("x") + 2 * lax.axis_index("y") + lax.axis_index("c")
    xs = x[0]
    target = loss_target[0]
    d = D_MODEL

    c_all = _all_gather(c, "gather_c").reshape(N_DEV, d)
    c_all = jnp.pad(c_all, ((0, N_DEV), (0, 0)))
    ncol = ada_w.shape[-1]
    ada_b_mine = lax.dynamic_slice_in_dim(ada_b, me * ncol, ncol, axis=1)
    mod_cols = _ada_fwd(c_all, ada_w, ada_b_mine)
    mod_all = _all_gather(mod_cols, "gather_mod")
    mod = lax.dynamic_index_in_dim(mod_all, me, axis=2, keepdims=False)
    mod = jnp.moveaxis(mod, 0, 1).reshape(DEPTH, 6, d)

    def pv_rows(*rows):
        return jnp.pad(jnp.stack(rows), ((0, 8 - len(rows)), (0, 0)))

    fw = _fox_weights(_gather_cols(fox_w_in, "gather_fox_in")[0], _gather_rows(fox_w_o, "gather_fox_o")[0])
    conv_w = _gather_cols(ssm_conv_w, "gather_conv_w", F32)[0]
    small_vec = jnp.concatenate([ssm_conv_b[0], ssm_norm_w[0]]).reshape(1, -1)
    small_all = _all_gather(small_vec.astype(F32), "gather_conv_b").reshape(N_DEV, -1)
    conv_b = small_all[:, :SSM_CONV_DIM // N_DEV].reshape(-1)
    norm_w = small_all[:, SSM_CONV_DIM // N_DEV:].reshape(-1)
    cw8 = jnp.pad(conv_w, ((0, 8 - SSM_CONV), (0, 0)))
    cb8 = _rows8(conv_b)
    nw8 = _rows8(norm_w)
    bf8 = _rows8(jnp.pad(fox_b_f[0], (0, LANES - FOX_HEADS)))
    par8 = jnp.pad(jnp.stack([jnp.pad(p[0], (0, LANES - SSM_HEADS)) for p in (ssm_dt_bias, ssm_a_log, ssm_d)]),
                   ((0, 5), (0, 0)))

    sh_a, sc_a, g_a, sh_m, sc_m, g_m = [mod[:, k] for k in range(6)]
    u0 = _modulate(xs, pv_rows(1.0 + sc_a[0], sh_a[0]), "modulate0")
    y0, fres, gathered = _fox_fwd(u0, fw, bf8, [(False, _mx(mlp_w1)), (False, _mx(mlp_w2))])
    w1 = _cols_from_slots(gathered[0])
    w2 = _rows_from_slots(gathered[1])
    pv0 = pv_rows(1.0 + g_a[0], ln_mix_g[0], ln_mix_b[0], 1.0 + sc_m[0], sh_m[0])
    x1, u1 = _ln_fwd(xs, y0, pv0, "ln_mix0")
    y1, (h0, a0), gathered = _mlp_fwd(u1, w1[0], w2[0], "0", [(False, _mx(ssm_w_in)), (False, _mx(ssm_w_out))])
    sw = _ssd_weights(_cols_from_slots(gathered[0])[0], _rows_from_slots(gathered[1])[0])
    pv1 = pv_rows(1.0 + g_m[0], ln_mlp_g[0], ln_mlp_b[0], 1.0 + sc_a[1], sh_a[1])
    x2, u2 = _ln_fwd(x1, y1, pv1, "ln_mlp0")
    y2, sres = _ssd_fwd(u2, sw, cw8, cb8, par8, nw8)
    pv2 = pv_rows(1.0 + g_a[1], ln_mix_g[1], ln_mix_b[1], 1.0 + sc_m[1], sh_m[1])
    x3, u3 = _ln_fwd(x2, y2, pv2, "ln_mix1")
    y3, (h1, a1), _ = _mlp_fwd(u3, w1[1], w2[1], "1")
    pv3 = pv_rows(1.0 + g_m[1], ln_mlp_g[1], ln_mlp_b[1])

    dx3, dy3, s3 = _ln_bwd(x3, y3, pv3, "ln_mlp1_bwd", target=target)
    loss = lax.psum(s3[5, 0], ("x", "y", "c"))
    du3, dw1_1, dw2_1, _ = _mlp_bwd(dy3, u3, h1, a1, w1[1].T, w2[1].T, "1")
    dx2, dy2, s2 = _ln_bwd(x2, y2, pv2, "ln_mix1_bwd", dxo=dx3, du=du3)
    du2, d_ssm_in, ssm_small, ex_scan = _ssd_bwd(
        dy2, u2, sw, cw8, cb8, par8, nw8, sres,
        [(True, _col_slots(dw1_1[None])), (True, _row_slots(dw2_1[None]))])
    dx1, dy1, s1 = _ln_bwd(x1, y1, pv1, "ln_mlp0_bwd", dxo=dx2, du=du2)
    half = d_ssm_in.shape[0] // 2
    du1, dw1_0, dw2_0, ex_mlp = _mlp_bwd(dy1, u1, h0, a0, w1[0].T, w2[0].T, "0",
                                         [(True, _col_slots(d_ssm_in[None, :half]))],
                                         [(True, _col_slots(d_ssm_in[None, half:]))])
    dx0, dy0, s0 = _ln_bwd(xs, y0, pv0, "ln_mix0_bwd", dxo=dx1, du=du1)
    late = [(True, _col_slots(dw1_0[None])), (True, _row_slots(dw2_0[None])),
            (True, _col_slots(ssm_small["conv_w"][None])), (True, _col_slots(ssm_small["conv_b"][None])),
            (True, _col_slots(ssm_small["norm_w"][None]))]
    du0, fox_sums, ex_post = _fox_bwd(dy0, u0, fw, bf8, fres, late)
    grad_x, sx = _mod_bwd(dx0, du0, xs, pv_rows(1.0 + sc_a[0], sh_a[0]), "modulate0_bwd")

    dmod = jnp.stack([
        jnp.stack([sx[1], sx[0], s0[4], s0[1], s0[0], s1[4]]),
        jnp.stack([s1[1], s1[0], s2[4], s2[1], s2[0], s3[4]]),
    ]).reshape(DEPTH, 6 * d)

    def pad_rows(v):
        v = v.reshape(-1, LANES) if v.size % LANES == 0 else jnp.pad(v.reshape(-1), (0, LANES - v.size)).reshape(1, LANES)
        return jnp.pad(v, ((0, (-v.shape[0]) % 8), (0, 0)))

    small_parts = [dmod, jnp.stack([s0[2], s2[2]]), jnp.stack([s0[3], s2[3]]), jnp.stack([s1[2], s3[2]]),
                   jnp.stack([s1[3], s3[3]]), fox_sums[0, :FOX_HEADS], ssm_small["dt_bias"], ssm_small["a_log"],
                   ssm_small["d"]]
    packed = [pad_rows(p) for p in small_parts]
    offsets = np.cumsum([0] + [p.shape[0] for p in packed])
    small_all_g = _all_gather(jnp.concatenate(packed, axis=0), "gather_small_grads")

    def unpack(idx, shape):
        n = int(np.prod(shape))
        blk = small_all_g[:, offsets[idx]:offsets[idx + 1]].reshape(N_DEV, -1)[:, :n]
        return blk.reshape((N_DEV,) + tuple(shape))

    dmod_all = unpack(0, (DEPTH, 6 * d))
    dmod_cols = lax.dynamic_slice_in_dim(dmod_all, me * ncol, ncol, axis=2)
    dmod_cols = jnp.pad(jnp.moveaxis(dmod_cols, 0, 1), ((0, 0), (0, N_DEV), (0, 0)))
    g_ada_w = _ada_bwd(c_all, dmod_cols)

    shards = dict(
        mlp_w1=jnp.concatenate([ex_post[0], ex_scan[0]], axis=1),
        mlp_w2=jnp.concatenate([ex_post[1], ex_scan[1]], axis=1),
        ssm_w_in=jnp.concatenate(ex_mlp, axis=2), ssm_w_out=ex_scan[2],
        ssm_conv_w=ex_post[2], ssm_conv_b=ex_post[3], ssm_norm_w=ex_post[4],
        fox_w_in=ex_post[5], fox_w_o=ex_post[6],
        ada_w=g_ada_w[None], ada_b=dmod_all,
        ln_mix_g=unpack(1, (DEPTH, d)), ln_mix_b=unpack(2, (DEPTH, d)),
        ln_mlp_g=unpack(3, (DEPTH, d)), ln_mlp_b=unpack(4, (DEPTH, d)),
        fox_b_f=unpack(5, (1, FOX_HEADS)), ssm_dt_bias=unpack(6, (1, SSM_HEADS)),
        ssm_a_log=unpack(7, (1, SSM_HEADS)), ssm_d=unpack(8, (1, SSM_HEADS)),
    )
    weights = dict(ada_w=ada_w, ada_b=ada_b, ln_mix_g=ln_mix_g, ln_mix_b=ln_mix_b, ln_mlp_g=ln_mlp_g, ln_mlp_b=ln_mlp_b,
                   mlp_w1=mlp_w1, mlp_w2=mlp_w2, fox_w_in=fox_w_in, fox_b_f=fox_b_f, fox_w_o=fox_w_o, ssm_w_in=ssm_w_in,
                   ssm_conv_w=ssm_conv_w, ssm_conv_b=ssm_conv_b, ssm_dt_bias=ssm_dt_bias, ssm_a_log=ssm_a_log,
                   ssm_d=ssm_d, ssm_norm_w=ssm_norm_w, ssm_w_out=ssm_w_out)
    mom1 = dict(ada_w=m_ada_w, ada_b=m_ada_b, ln_mix_g=m_ln_mix_g, ln_mix_b=m_ln_mix_b, ln_mlp_g=m_ln_mlp_g,
                ln_mlp_b=m_ln_mlp_b, mlp_w1=m_mlp_w1, mlp_w2=m_mlp_w2, fox_w_in=m_fox_w_in, fox_b_f=m_fox_b_f,
                fox_w_o=m_fox_w_o, ssm_w_in=m_ssm_w_in, ssm_conv_w=m_ssm_conv_w, ssm_conv_b=m_ssm_conv_b,
                ssm_dt_bias=m_ssm_dt_bias, ssm_a_log=m_ssm_a_log, ssm_d=m_ssm_d, ssm_norm_w=m_ssm_norm_w,
                ssm_w_out=m_ssm_w_out)
    mom2 = dict(ada_w=v_ada_w, ada_b=v_ada_b, ln_mix_g=v_ln_mix_g, ln_mix_b=v_ln_mix_b, ln_mlp_g=v_ln_mlp_g,
                ln_mlp_b=v_ln_mlp_b, mlp_w1=v_mlp_w1, mlp_w2=v_mlp_w2, fox_w_in=v_fox_w_in, fox_b_f=v_fox_b_f,
                fox_w_o=v_fox_w_o, ssm_w_in=v_ssm_w_in, ssm_conv_w=v_ssm_conv_w, ssm_conv_b=v_ssm_conv_b,
                ssm_dt_bias=v_ssm_dt_bias, ssm_a_log=v_ssm_a_log, ssm_d=v_ssm_d, ssm_norm_w=v_ssm_norm_w,
                ssm_w_out=v_ssm_w_out)
    names = list(weights)
    stepped = {n: _adamw_any(shards[n], weights[n], mom1[n], mom2[n], f"adamw_{n}") for n in names}
    return (loss, grad_x[None], *[stepped[n][0] for n in names], *[stepped[n][1] for n in names],
            *[stepped[n][2] for n in names], *[stepped[n][3] for n in names])
```

```python
import numpy as np
import jax
import jax.numpy as jnp
from jax import lax
from jax.experimental import pallas as pl
from jax.experimental.pallas import tpu as pltpu

F32 = jnp.float32
MXU_DTYPE = jnp.bfloat16

N_DEV = 8
D_MODEL = 1024
DEPTH = 2
FOX_HEADS = 16
FOX_HEAD_DIM = 64
D_FF = 4096
SSM_D_INNER = 2048
SSM_HEADS = 32
SSM_GROUPS = 8
SSM_STATE = 128
SSM_CHUNK = 128
SSM_CONV = 4
SSM_CONV_DIM = 4096
GROUP_W = SSM_D_INNER // SSM_GROUPS
LN_EPS = 1e-5
RMS_EPS = 1e-5
ALPHA = (2.0 * DEPTH) ** 0.25
LANES = 128
SUBLANES = 8

ADAM_LR = 0.001
ADAM_B1 = 0.9
ADAM_B2 = 0.999
ADAM_EPS = 1e-08
ADAM_WD = 0.01
ADAM_STEP = 10

NN = (((1,), (0,)), ((), ()))
NT = (((1,), (1,)), ((), ()))
TN = (((0,), (0,)), ((), ()))

VMEM_BIG = 56 * 1024 * 1024
MM_ROWS = 2048
MM_DEPTH = 2048


def _dot(a, b, dims=NN, precision=None):
    return lax.dot_general(a, b, dims, precision=precision, preferred_element_type=F32)


def _mx(v):
    return v.astype(MXU_DTYPE)


def _pieces3(v):
    hi = _mx(v)
    r1 = v - hi.astype(F32)
    mid = _mx(r1)
    return hi, mid, _mx(r1 - mid.astype(F32))


def _dot_onehot(a, b, dims=NN, onehot="b"):
    if onehot == "b":
        return sum(_dot(p, _mx(b), dims) for p in _pieces3(a))
    return sum(_dot(_mx(a), p, dims) for p in _pieces3(b))


def _params(vmem=None):
    return pltpu.CompilerParams(vmem_limit_bytes=vmem) if vmem else None


def _all_gather(x, name):
    def body(x_ref, out_ref, send_sems, recv_sems, local_sem):
        xi, yi, ci = lax.axis_index("x"), lax.axis_index("y"), lax.axis_index("c")
        me, sibling = (xi, yi, ci), (xi, yi, 1 - ci)
        chips = [(1 - xi, yi), (xi, 1 - yi), (1 - xi, 1 - yi)]

        def slot(px, py, pc):
            return out_ref.at[4 * px + 2 * py + pc]

        def copy(k, block, to, src=None):
            return pltpu.make_async_remote_copy(
                src_ref=slot(*block) if src is None else src, dst_ref=slot(*block),
                send_sem=send_sems.at[k], recv_sem=recv_sems.at[k],
                device_id=to, device_id_type=pl.DeviceIdType.MESH)

        mine = pltpu.make_async_copy(x_ref, slot(*me), local_sem)
        mine.start()
        first = [copy(0, me, sibling, src=x_ref)]
        first += [copy(1 + j, me, (*chip, ci), src=x_ref) for j, chip in enumerate(chips)]
        for cp in first:
            cp.start()
        passed = [copy(4 + j, (*chip, ci), sibling) for j, chip in enumerate(chips)]
        for j, chip in enumerate(chips):
            copy(1 + j, (*chip, ci), me).wait_recv()
            passed[j].start()
        copy(0, sibling, me).wait_recv()
        for j, chip in enumerate(chips):
            copy(4 + j, (*chip, 1 - ci), me).wait_recv()
        for cp in first + passed:
            cp.wait_send()
        mine.wait()

    return pl.pallas_call(
        body, name=name,
        out_shape=jax.ShapeDtypeStruct((N_DEV,) + x.shape, x.dtype),
        in_specs=[pl.BlockSpec(memory_space=pl.ANY)],
        out_specs=pl.BlockSpec(memory_space=pl.ANY),
        scratch_shapes=[pltpu.SemaphoreType.DMA((7,)), pltpu.SemaphoreType.DMA((7,)),
                        pltpu.SemaphoreType.DMA],
    )(x)


def _direct_copies(scatter, x_ref, out_ref, send_sems, recv_sems, local_sems, n):
    xi, yi, ci = lax.axis_index("x"), lax.axis_index("y"), lax.axis_index("c")
    me = 4 * xi + 2 * yi + ci
    local = pltpu.make_async_copy(x_ref.at[me] if scatter else x_ref, out_ref.at[me], local_sems.at[n])
    remote = []
    for k in range(1, N_DEV):
        px = 1 - xi if k & 4 else xi
        py = 1 - yi if k & 2 else yi
        pc = 1 - ci if k & 1 else ci
        remote.append(pltpu.make_async_remote_copy(
            src_ref=x_ref.at[4 * px + 2 * py + pc] if scatter else x_ref, dst_ref=out_ref.at[me],
            send_sem=send_sems.at[7 * n + k - 1], recv_sem=recv_sems.at[7 * n + k - 1],
            device_id=(px, py, pc), device_id_type=pl.DeviceIdType.MESH))
    return local, remote


def _exchange_shapes(carry):
    return [jax.ShapeDtypeStruct(a.shape if scatter else (N_DEV,) + a.shape, a.dtype) for scatter, a in carry]


def _exchange_sems(carry):
    n = len(carry)
    return [pltpu.SemaphoreType.DMA((7 * n,)), pltpu.SemaphoreType.DMA((7 * n,)), pltpu.SemaphoreType.DMA((n,))]


def _carrying(body, n_in, n_out, n_scratch, carry, grid):
    nc = len(carry)

    def copies(refs):
        srcs = refs[n_in:n_in + nc]
        dsts = refs[n_in + nc + n_out:n_in + 2 * nc + n_out]
        sems = refs[n_in + 2 * nc + n_out + n_scratch:]
        return [_direct_copies(scatter, srcs[n], dsts[n], *sems, n) for n, (scatter, _) in enumerate(carry)]

    def wrapped(*refs):
        step = 0
        for axis, extent in enumerate(grid):
            step = step * extent + pl.program_id(axis)

        @pl.when(step == 0)
        def _():
            for local, remote in copies(refs):
                local.start()
                for cp in remote:
                    cp.start()

        body(*refs[:n_in], *refs[n_in + nc:n_in + nc + n_out],
             *refs[n_in + 2 * nc + n_out:n_in + 2 * nc + n_out + n_scratch])

        @pl.when(step == int(np.prod(grid)) - 1)
        def _():
            for local, remote in copies(refs):
                for cp in remote:
                    cp.wait()
                local.wait()

    return wrapped if nc else body


def _mm(name, a, b, *, grid, a_spec, b_spec, dims, k_axis, outs, acc=None, extras=(), epi=None, vmem=None,
        carry=()):
    nk = grid[k_axis]
    n_ex, n_out = len(extras), len(outs)
    carry = list(carry)
    anywhere = pl.BlockSpec(memory_space=pl.ANY)

    def body(*refs):
        a_ref, b_ref = refs[0], refs[1]
        ex = refs[2:2 + n_ex]
        out = refs[2 + n_ex:2 + n_ex + n_out]

        def finish(val):
            if epi is None:
                out[0][...] = val.astype(out[0].dtype)
            else:
                epi(val, ex, out)

        part = _dot(a_ref[...], b_ref[...], dims)
        if nk == 1:
            finish(part)
        else:
            acc_ref = refs[2 + n_ex + n_out]
            k = pl.program_id(k_axis)

            @pl.when(k == 0)
            def _():
                acc_ref[...] = part

            @pl.when(k > 0)
            def _():
                acc_ref[...] += part

            @pl.when(k == nk - 1)
            def _():
                finish(acc_ref[...])

    res = pl.pallas_call(
        _carrying(body, 2 + n_ex, n_out, 1 if nk > 1 else 0, carry, grid), name=name, grid=grid,
        in_specs=[a_spec, b_spec] + [s for _, s in extras] + [anywhere] * len(carry),
        out_specs=[s for _, s in outs] + [anywhere] * len(carry),
        out_shape=[o for o, _ in outs] + _exchange_shapes(carry),
        scratch_shapes=([pltpu.VMEM(acc, F32)] if nk > 1 else []) + (_exchange_sems(carry) if carry else []),
        compiler_params=_params(vmem),
    )(a, b, *[e for e, _ in extras], *[arr for _, arr in carry])
    return list(res)


def _tile(n, t):
    t = min(n, t)
    assert n % t == 0, (n, t)
    return t


def _mm_nn(name, a, b, out_dtype, *, tm=MM_ROWS, tn=1024, tk=1024, epi=None, extras=(), outs=None, carry=()):
    m, kk = a.shape
    n = b.shape[1]
    tm, tn, tk = _tile(m, tm), _tile(n, tn), _tile(kk, tk)
    if outs is None:
        outs = [(jax.ShapeDtypeStruct((m, n), out_dtype), pl.BlockSpec((tm, tn), lambda i, j, k: (i, j)))]
    res = _mm(name, a, b, grid=(m // tm, n // tn, kk // tk),
              a_spec=pl.BlockSpec((tm, tk), lambda i, j, k: (i, k)),
              b_spec=pl.BlockSpec((tk, tn), lambda i, j, k: (k, j)),
              dims=NN, k_axis=2, acc=(tm, tn), outs=outs, extras=list(extras), epi=epi, vmem=VMEM_BIG, carry=carry)
    return res[0] if len(res) == 1 else res


def _mm_sum(name, pairs, out_dtype=F32, *, tm=1024, tn=1024, carry=()):
    m, n = pairs[0][0].shape[0], pairs[0][1].shape[1]
    tm, tn = _tile(m, tm), _tile(n, tn)
    carry = list(carry)
    grid = (m // tm, n // tn)

    def body(*refs):
        out = refs[2 * len(pairs)]
        acc = _dot(refs[0][...], refs[1][...])
        for p in range(1, len(pairs)):
            acc = acc + _dot(refs[2 * p][...], refs[2 * p + 1][...])
        out[...] = acc.astype(out.dtype)

    in_specs = []
    for a, b in pairs:
        in_specs += [pl.BlockSpec((tm, a.shape[1]), lambda i, j: (i, 0)),
                     pl.BlockSpec((b.shape[0], tn), lambda i, j: (0, j))]
    anywhere = pl.BlockSpec(memory_space=pl.ANY)
    res = pl.pallas_call(
        _carrying(body, 2 * len(pairs), 1, 0, carry, grid), name=name, grid=grid,
        in_specs=in_specs + [anywhere] * len(carry),
        out_specs=[pl.BlockSpec((tm, tn), lambda i, j: (i, j))] + [anywhere] * len(carry),
        out_shape=[jax.ShapeDtypeStruct((m, n), out_dtype)] + _exchange_shapes(carry),
        scratch_shapes=_exchange_sems(carry) if carry else [],
        compiler_params=_params(VMEM_BIG),
    )(*[x for pair in pairs for x in pair], *[arr for _, arr in carry])
    return list(res) if carry else res[0]


def _mm_tn(name, a, b, out_dtype=F32, *, tm=1024, tn=1024, tk=MM_DEPTH):
    kk, m = a.shape
    n = b.shape[1]
    tm, tn, tk = _tile(m, tm), _tile(n, tn), _tile(kk, tk)
    res = _mm(name, a, b, grid=(m // tm, n // tn, kk // tk),
              a_spec=pl.BlockSpec((tk, tm), lambda i, j, k: (k, i)),
              b_spec=pl.BlockSpec((tk, tn), lambda i, j, k: (k, j)),
              dims=TN, k_axis=2, acc=(tm, tn),
              outs=[(jax.ShapeDtypeStruct((m, n), out_dtype), pl.BlockSpec((tm, tn), lambda i, j, k: (i, j)))],
              vmem=VMEM_BIG)
    return res[0]


def _row_block(s):
    return _tile(s, 512)


def _modulate(x, pv, name):
    s, d = x.shape
    tb = _row_block(s)

    def body(x_ref, pv_ref, u_ref):
        u_ref[...] = _mx(x_ref[...] * pv_ref[0:1, :] + pv_ref[1:2, :])

    return pl.pallas_call(
        body, name=name, grid=(s // tb,),
        in_specs=[pl.BlockSpec((tb, d), lambda i: (i, 0)), pl.BlockSpec((8, d), lambda i: (0, 0))],
        out_specs=pl.BlockSpec((tb, d), lambda i: (i, 0)),
        out_shape=jax.ShapeDtypeStruct((s, d), MXU_DTYPE),
    )(x, pv)


def _ln_stats(r):
    mu = jnp.mean(r, axis=-1, keepdims=True)
    xc = r - mu
    var = jnp.mean(xc * xc, axis=-1, keepdims=True)
    rstd = lax.rsqrt(var + LN_EPS)
    return xc * rstd, rstd


def _ln_fwd(xin, y, pv, name):
    s, d = xin.shape
    tb = _row_block(s)

    def body(x_ref, y_ref, pv_ref, xo_ref, u_ref):
        r = ALPHA * x_ref[...] + pv_ref[0:1, :] * y_ref[...]
        xhat, _ = _ln_stats(r)
        xo = xhat * pv_ref[1:2, :] + pv_ref[2:3, :]
        xo_ref[...] = xo
        u_ref[...] = _mx(xo * pv_ref[3:4, :] + pv_ref[4:5, :])

    row = pl.BlockSpec((tb, d), lambda i: (i, 0))
    return pl.pallas_call(
        body, name=name, grid=(s // tb,),
        in_specs=[row, row, pl.BlockSpec((8, d), lambda i: (0, 0))],
        out_specs=[row, row],
        out_shape=[jax.ShapeDtypeStruct((s, d), F32), jax.ShapeDtypeStruct((s, d), MXU_DTYPE)],
    )(xin, y, pv)


def _ln_bwd(xin, y, pv, name, *, dxo=None, du=None, target=None):
    s, d = xin.shape
    tb = _row_block(s)
    nb = s // tb
    loss_mode = target is not None

    def body(*refs):
        if loss_mode:
            x_ref, y_ref, pv_ref, t_ref, dxin_ref, dy_ref, sums_ref = refs
        else:
            x_ref, y_ref, pv_ref, dxo_ref, du_ref, dxin_ref, dy_ref, sums_ref = refs
        i = pl.program_id(0)

        @pl.when(i == 0)
        def _():
            sums_ref[...] = jnp.zeros_like(sums_ref)

        yv = y_ref[...]
        r = ALPHA * x_ref[...] + pv_ref[0:1, :] * yv
        xhat, rstd = _ln_stats(r)
        xo = xhat * pv_ref[1:2, :] + pv_ref[2:3, :]
        if loss_mode:
            diff = xo - t_ref[...]
            dxo_v = diff * (1.0 / d)
            sums_ref[5:6, :] += jnp.sum(diff * diff, axis=0, keepdims=True) * (0.5 / d)
        else:
            duv = du_ref[...]
            dxo_v = dxo_ref[...] + duv * pv_ref[3:4, :]
            sums_ref[0:1, :] += jnp.sum(duv * xo, axis=0, keepdims=True)
            sums_ref[1:2, :] += jnp.sum(duv, axis=0, keepdims=True)
        sums_ref[2:3, :] += jnp.sum(dxo_v * xhat, axis=0, keepdims=True)
        sums_ref[3:4, :] += jnp.sum(dxo_v, axis=0, keepdims=True)
        dxh = dxo_v * pv_ref[1:2, :]
        dr = rstd * (dxh - jnp.mean(dxh, axis=-1, keepdims=True)
                     - xhat * jnp.mean(dxh * xhat, axis=-1, keepdims=True))
        sums_ref[4:5, :] += jnp.sum(dr * yv, axis=0, keepdims=True)
        dxin_ref[...] = ALPHA * dr
        dy_ref[...] = _mx(pv_ref[0:1, :] * dr)
        if loss_mode:
            @pl.when(i == nb - 1)
            def _():
                sums_ref[5:6, :] = jnp.broadcast_to(jnp.sum(sums_ref[5:6, :], axis=-1, keepdims=True), (1, d))

    row = pl.BlockSpec((tb, d), lambda i: (i, 0))
    par = pl.BlockSpec((8, d), lambda i: (0, 0))
    ins = [xin, y, pv] + ([target] if loss_mode else [dxo, du])
    return pl.pallas_call(
        body, name=name, grid=(nb,),
        in_specs=[row, row, par] + [row] * (len(ins) - 3),
        out_specs=[row, row, par],
        out_shape=[jax.ShapeDtypeStruct((s, d), F32), jax.ShapeDtypeStruct((s, d), MXU_DTYPE),
                   jax.ShapeDtypeStruct((8, d), F32)],
    )(*ins)


def _mod_bwd(dx_direct, du, x, pv, name):
    s, d = x.shape
    tb = _row_block(s)

    def body(dxd_ref, du_ref, x_ref, pv_ref, dx_ref, sums_ref):
        @pl.when(pl.program_id(0) == 0)
        def _():
            sums_ref[...] = jnp.zeros_like(sums_ref)

        duv = du_ref[...]
        dx_ref[...] = dxd_ref[...] + duv * pv_ref[0:1, :]
        sums_ref[0:1, :] += jnp.sum(duv * x_ref[...], axis=0, keepdims=True)
        sums_ref[1:2, :] += jnp.sum(duv, axis=0, keepdims=True)

    row = pl.BlockSpec((tb, d), lambda i: (i, 0))
    par = pl.BlockSpec((8, d), lambda i: (0, 0))
    return pl.pallas_call(
        body, name=name, grid=(s // tb,),
        in_specs=[row, row, row, par], out_specs=[row, par],
        out_shape=[jax.ShapeDtypeStruct((s, d), F32), jax.ShapeDtypeStruct((8, d), F32)],
    )(dx_direct, du, x, pv)


def _mlp_fwd(u, w1, w2, tag, carry=()):
    s = u.shape[0]

    def epi(val, ex, out):
        out[0][...] = _mx(val)
        out[1][...] = _mx(jnp.square(jnp.maximum(val, 0.0)))

    tm, tn = _tile(s, MM_ROWS), 1024
    spec = pl.BlockSpec((tm, tn), lambda i, j, k: (i, j))
    shp = jax.ShapeDtypeStruct((s, D_FF), MXU_DTYPE)
    h, a, *carried = _mm_nn(f"mlp_up{tag}", u, w1, None, epi=epi, outs=[(shp, spec), (shp, spec)], tn=tn,
                            carry=carry)
    y = _mm_nn(f"mlp_down{tag}", a, w2, F32)
    return y, (h, a), carried


def _mlp_bwd(dy, u, h, a, w1t, w2t, tag, carry=(), carry_du=()):
    s = u.shape[0]
    tm, tn = _tile(s, MM_ROWS), 1024
    spec = pl.BlockSpec((tm, tn), lambda i, j, k: (i, j))

    def epi(val, ex, out):
        out[0][...] = _mx(val * (2.0 * jnp.maximum(ex[0][...].astype(F32), 0.0)))

    got = _mm_nn(f"mlp_dh{tag}", dy, w2t, None, epi=epi, extras=[(h, spec)],
                 outs=[(jax.ShapeDtypeStruct((s, D_FF), MXU_DTYPE), spec)], tn=tn, carry=carry)
    dh, carried = (got[0], list(got[1:])) if carry else (got, [])
    got = _mm_nn(f"mlp_du{tag}", dh, w1t, F32, carry=carry_du)
    du, carried = (got[0], carried + list(got[1:])) if carry_du else (got, carried)
    dw2 = _mm_tn(f"mlp_dw2{tag}", a, dy)
    dw1 = _mm_tn(f"mlp_dw1{tag}", u, dh)
    return du, dw1, dw2, carried


FOX_T = 1024
BIAS_Q = (64, 65, 66)
BIAS_K = (67, 68, 69)
SKIP_MARGIN = 110.0
ONES_V = 64

def _fox_constants():
    selq = np.zeros((FOX_HEADS, 512, LANES), np.float32)
    selk = np.zeros((FOX_HEADS, 512, LANES), np.float32)
    selv = np.zeros((2, LANES, LANES), np.float32)
    put = np.zeros((2, 2, LANES, LANES), np.float32)
    for h in range(FOX_HEADS):
        off = FOX_HEAD_DIM * (h % 2)
        for dd in range(FOX_HEAD_DIM):
            selq[h, off + dd, dd] = FOX_HEAD_DIM ** -0.5
            selk[h, off + dd, dd] = 1.0
        for piece in range(3):
            selq[h, LANES * (1 + piece) + h, BIAS_Q[piece]] = 1.0
            selk[h, LANES * (1 + piece) + h, BIAS_K[piece]] = -1.0
    for par in range(2):
        for dd in range(FOX_HEAD_DIM):
            selv[par, FOX_HEAD_DIM * par + dd, dd] = 1.0
            put[par, 0, dd, FOX_HEAD_DIM * par + dd] = FOX_HEAD_DIM ** -0.5
            put[par, 1, dd, FOX_HEAD_DIM * par + dd] = 1.0
    return selq, selk, selv, put


def _fox_prep(qkv, f, bf, carry=()):
    s = qkv.shape[0]
    t = _tile(s, FOX_T)
    nb = s // t
    selq, selk, selv, _ = _fox_constants()
    carry = list(carry)
    anywhere = pl.BlockSpec(memory_space=pl.ANY)

    def body(q_ref, k_ref, v_ref, f_ref, bf_ref, selq_ref, selk_ref, selv_ref,
             qa_ref, qat_ref, ka_ref, kat_ref, va_ref, vat_ref, stats_ref, parts_ref, carry_ref, cum_ref):
        i, h = pl.program_id(0), pl.program_id(1)
        lane = lax.broadcasted_iota(jnp.int32, (1, LANES), 1)

        @pl.when(h == 0)
        def _():
            @pl.when(i == 0)
            def _():
                carry_ref[...] = jnp.zeros_like(carry_ref)

            lf = jnp.where(lane < FOX_HEADS, jax.nn.log_sigmoid(f_ref[...] + bf_ref[0:1, :]), 0.0)
            tri = (lax.broadcasted_iota(jnp.int32, (t, t), 0) >= lax.broadcasted_iota(jnp.int32, (t, t), 1)).astype(F32)
            cum = _dot_onehot(tri, lf, onehot="a") + carry_ref[0:1, :]
            carry_ref[0:1, :] = cum[t - 1:t, :]
            cum_ref[...] = cum
            hi = _mx(cum)
            r1 = cum - hi.astype(F32)
            mid = _mx(r1)
            parts_ref[:, 0:LANES] = hi
            parts_ref[:, LANES:2 * LANES] = mid
            parts_ref[:, 2 * LANES:3 * LANES] = _mx(r1 - mid.astype(F32))

        parts = parts_ref[...]
        qa = _dot(jnp.concatenate([q_ref[...], parts], axis=1), selq_ref[...])
        qa = qa + jnp.where((lane >= BIAS_K[0]) & (lane <= BIAS_K[2]), 1.0, 0.0)
        ka = _dot(jnp.concatenate([k_ref[...], parts], axis=1), selk_ref[...])
        ka = ka + jnp.where((lane >= BIAS_Q[0]) & (lane <= BIAS_Q[2]), 1.0, 0.0)
        va = _dot(v_ref[...], selv_ref[...]) + jnp.where(lane == ONES_V, 1.0, 0.0)
        qa_ref[...] = _mx(qa)
        qat_ref[...] = _mx(qa.T)
        ka_ref[...] = _mx(ka)
        kat_ref[...] = _mx(ka.T)
        va_ref[...] = _mx(va)
        vat_ref[...] = _mx(va.T)

        def longest(rows_):
            sq = jnp.where(lane < FOX_HEAD_DIM, rows_ * rows_, 0.0)
            return jnp.sqrt(jnp.max(jnp.sum(sq, axis=1, keepdims=True), axis=0, keepdims=True))

        mine = lane == h
        cum = cum_ref[...]
        top = jnp.max(jnp.max(jnp.where(mine, cum, -jnp.inf), axis=1, keepdims=True), axis=0, keepdims=True)
        low = jnp.min(jnp.min(jnp.where(mine, cum, jnp.inf), axis=1, keepdims=True), axis=0, keepdims=True)
        row = lax.broadcasted_iota(jnp.int32, (8, LANES), 0)
        stats_ref[...] = jnp.where(row == 0, longest(qa), jnp.where(row == 1, longest(ka),
                                                                    jnp.where(row == 2, top, low)))

    rows = jax.ShapeDtypeStruct((FOX_HEADS, nb, t, LANES), MXU_DTYPE)
    cols = jax.ShapeDtypeStruct((FOX_HEADS, nb, LANES, t), MXU_DTYPE)
    rspec = pl.BlockSpec((None, None, t, LANES), lambda i, h: (h, i, 0, 0))
    cspec = pl.BlockSpec((None, None, LANES, t), lambda i, h: (h, i, 0, 0))
    npair = FOX_HEADS // 2
    res = pl.pallas_call(
        _carrying(body, 8, 7, 3, carry, (nb, FOX_HEADS)), name="fox_prep", grid=(nb, FOX_HEADS),
        in_specs=[pl.BlockSpec((t, LANES), lambda i, h: (i, h // 2)),
                  pl.BlockSpec((t, LANES), lambda i, h: (i, npair + h // 2)),
                  pl.BlockSpec((t, LANES), lambda i, h: (i, 2 * npair + h // 2)),
                  pl.BlockSpec((t, LANES), lambda i, h: (i, 0)),
                  pl.BlockSpec((8, LANES), lambda i, h: (0, 0)),
                  pl.BlockSpec((None, 512, LANES), lambda i, h: (h, 0, 0)),
                  pl.BlockSpec((None, 512, LANES), lambda i, h: (h, 0, 0)),
                  pl.BlockSpec((None, LANES, LANES), lambda i, h: (h % 2, 0, 0))] + [anywhere] * len(carry),
        out_specs=[rspec, cspec, rspec, cspec, rspec, cspec,
                   pl.BlockSpec((None, None, 8, LANES), lambda i, h: (h, i, 0, 0))] + [anywhere] * len(carry),
        out_shape=[rows, cols, rows, cols, rows, cols, jax.ShapeDtypeStruct((FOX_HEADS, nb, 8, LANES), F32)]
        + _exchange_shapes(carry),
        scratch_shapes=[pltpu.VMEM((t, 3 * LANES), MXU_DTYPE), pltpu.VMEM((8, LANES), F32),
                        pltpu.VMEM((t, LANES), F32)] + (_exchange_sems(carry) if carry else []),
        compiler_params=_params(VMEM_BIG),
    )(qkv, qkv, qkv, f, bf, _mx(jnp.asarray(selq)), _mx(jnp.asarray(selk)), _mx(jnp.asarray(selv)),
      *[arr for _, arr in carry])
    return (*res[:7], list(res[7:]))


def _fox_active(stats):
    qn, kn, top, low = (stats[:, :, r, 0] for r in range(4))
    nb = qn.shape[1]
    gap = qn[:, :, None] * kn[:, None, :] + top[:, :, None] - low[:, None, :] + (qn * kn)[:, :, None]
    keep = (gap > -SKIP_MARGIN) | jnp.eye(nb, dtype=bool)[None]
    return jnp.where(keep, 1.0, 0.0).astype(F32).reshape(qn.shape[0], nb * nb)


def _causal_allow(t):
    return lax.broadcasted_iota(jnp.int32, (t, t), 0) <= lax.broadcasted_iota(jnp.int32, (t, t), 1)


def _fox_attn_fwd(qat, ka, vat, active):
    heads, nb, _, t = qat.shape

    def body(act_ref, qat_ref, ka_ref, vat_ref, ot_ref, lse_ref, acc_ref, m_ref, kbuf_ref, vbuf_ref, fsems):
        h, i = pl.program_id(0), pl.program_id(1)
        m_ref[...] = jnp.full_like(m_ref, -jnp.inf)
        acc_ref[...] = jnp.zeros_like(acc_ref)

        def key_blocks(j, slot, head=None):
            hh = h if head is None else head
            return [pltpu.make_async_copy(ka_ref.at[hh, j], kbuf_ref.at[slot], fsems.at[0, slot]),
                    pltpu.make_async_copy(vat_ref.at[hh, j], vbuf_ref.at[slot], fsems.at[1, slot])]

        def runs(j):
            return act_ref[h, i * nb + j] > 0.5

        def step(slot, diagonal):
            st = _dot(kbuf_ref[slot], qat_ref[...])
            if diagonal:
                st = jnp.where(_causal_allow(t), st, -jnp.inf)
            m_old = m_ref[...]
            m_new = jnp.maximum(m_old, jnp.max(st, axis=0, keepdims=True))
            pt = jnp.exp(st - m_new)
            acc_ref[...] = acc_ref[...] * jnp.exp(m_old - m_new) + _dot(vbuf_ref[slot], _mx(pt))
            m_ref[...] = m_new

        first = (h == 0) & (i == 0)
        last = (h == heads - 1) & (i == nb - 1)

        @pl.when(first)
        def _():
            for cp in key_blocks(i, 2):
                cp.start()

        @pl.when((i > 0) & runs(0))
        def _():
            for cp in key_blocks(0, 0):
                cp.start()

        for cp in key_blocks(i, 2):
            cp.wait()
        step(2, True)

        @pl.when(jnp.logical_not(last))
        def _():
            wrap = i == nb - 1
            for cp in key_blocks(jnp.where(wrap, 0, i + 1), 2, jnp.where(wrap, h + 1, h)):
                cp.start()

        def earlier(j, c):
            slot = j % 2
            nxt = jnp.minimum(j + 1, nb - 1)

            @pl.when((j + 1 < i) & runs(nxt))
            def _():
                for cp in key_blocks(nxt, 1 - slot):
                    cp.start()

            @pl.when(runs(j))
            def _():
                for cp in key_blocks(j, slot):
                    cp.wait()
                step(slot, False)

            return c

        lax.fori_loop(0, i, earlier, 0)
        acc = acc_ref[...]
        denom = acc[ONES_V:ONES_V + 1, :]
        ot_ref[...] = _mx(acc / denom)
        lse_ref[...] = m_ref[...] + jnp.log(denom)

    anywhere = pl.BlockSpec(memory_space=pl.ANY)
    qspec = pl.BlockSpec((None, None, LANES, t), lambda h, i: (h, i, 0, 0))
    return pl.pallas_call(
        body, name="fox_attn_fwd", grid=(heads, nb),
        in_specs=[pl.BlockSpec(memory_space=pltpu.SMEM), qspec, anywhere, anywhere],
        out_specs=[qspec, pl.BlockSpec((None, None, 1, t), lambda h, i: (h, i, 0, 0))],
        out_shape=[jax.ShapeDtypeStruct((heads, nb, LANES, t), MXU_DTYPE),
                   jax.ShapeDtypeStruct((heads, nb, 1, t), F32)],
        scratch_shapes=[pltpu.VMEM((LANES, t), F32), pltpu.VMEM((1, t), F32), pltpu.VMEM((3, t, LANES), MXU_DTYPE),
                        pltpu.VMEM((3, LANES, t), MXU_DTYPE), pltpu.SemaphoreType.DMA((2, 3))],
        compiler_params=_params(VMEM_BIG),
    )(active, qat, ka, vat)


def _fox_attn_bwd(qa, qat, ka, kat, va, ot, lse, do, dot_, active):
    heads, nb, t, _ = qa.shape

    def body(act_ref, qa_ref, qat_ref, ka_ref, kat_ref, va_ref, ot_ref, lse_ref, do_ref, dot_ref,
             dqt_ref, dka_ref, dva_ref, rows_ref, cols_ref, lseb_ref, fsems):
        h, j = pl.program_id(0), pl.program_id(1)

        @pl.when(j == 0)
        def _():
            dqt_ref[...] = jnp.zeros_like(dqt_ref)

        def query_blocks(i, slot, head=None):
            hh = h if head is None else head
            cps = [pltpu.make_async_copy(src.at[hh, i], rows_ref.at[slot, n], fsems.at[n, slot])
                   for n, src in enumerate((qa_ref, do_ref))]
            cps += [pltpu.make_async_copy(src.at[hh, i], cols_ref.at[slot, n], fsems.at[2 + n, slot])
                    for n, src in enumerate((qat_ref, ot_ref, dot_ref))]
            return cps + [pltpu.make_async_copy(lse_ref.at[hh, i], lseb_ref.at[slot], fsems.at[5, slot])]

        def runs(i):
            return (i == j) | (act_ref[h, i * nb + j] > 0.5)

        def step(i, slot, diagonal):
            st = _dot(ka_ref[...], cols_ref[slot, 0])
            dot_v = cols_ref[slot, 2]
            delta = jnp.sum(cols_ref[slot, 1].astype(F32) * dot_v.astype(F32), axis=0, keepdims=True)
            pt = jnp.exp(st - lseb_ref[slot])
            if diagonal:
                pt = jnp.where(_causal_allow(t), pt, 0.0)
            dsm = _mx(pt * (_dot(va_ref[...], dot_v) - delta))
            upd_v = _dot(_mx(pt), rows_ref[slot, 1])
            upd_k = _dot(dsm, rows_ref[slot, 0])
            if diagonal:
                dva_ref[...] = upd_v
                dka_ref[...] = upd_k
            else:
                dva_ref[...] += upd_v
                dka_ref[...] += upd_k
            dqt_ref[i] += _dot(kat_ref[...], dsm)

        def visit(i, slot, after, diagonal):
            nxt = jnp.minimum(i + 1, nb - 1)

            @pl.when((i + 1 < nb) & runs(nxt))
            def _():
                for cp in query_blocks(nxt, after):
                    cp.start()

            @pl.when(runs(i))
            def _():
                for cp in query_blocks(i, slot):
                    cp.wait()
                step(i, slot, diagonal)

        first = (h == 0) & (j == 0)
        last = (h == heads - 1) & (j == nb - 1)

        @pl.when(first)
        def _():
            for cp in query_blocks(j, 2):
                cp.start()

        visit(j, 2, 0, True)

        @pl.when(jnp.logical_not(last))
        def _():
            wrap = j == nb - 1
            for cp in query_blocks(jnp.where(wrap, 0, j + 1), 2, jnp.where(wrap, h + 1, h)):
                cp.start()

        def later(i, c):
            slot = (i - j - 1) % 2
            visit(i, slot, 1 - slot, False)
            return c

        lax.fori_loop(j + 1, nb, later, 0)

    def at_k(shape):
        return pl.BlockSpec((None, None) + shape, lambda h, j: (h, j, 0, 0))

    anywhere = pl.BlockSpec(memory_space=pl.ANY)
    return pl.pallas_call(
        body, name="fox_attn_bwd", grid=(heads, nb),
        in_specs=[pl.BlockSpec(memory_space=pltpu.SMEM),
                  anywhere, anywhere, at_k((t, LANES)), at_k((LANES, t)), at_k((t, LANES)),
                  anywhere, anywhere, anywhere, anywhere],
        out_specs=[pl.BlockSpec((None, nb, LANES, t), lambda h, j: (h, 0, 0, 0)), at_k((t, LANES)), at_k((t, LANES))],
        out_shape=[jax.ShapeDtypeStruct((heads, nb, LANES, t), F32),
                   jax.ShapeDtypeStruct((heads, nb, t, LANES), F32),
                   jax.ShapeDtypeStruct((heads, nb, t, LANES), F32)],
        scratch_shapes=[pltpu.VMEM((3, 2, t, LANES), MXU_DTYPE), pltpu.VMEM((3, 3, LANES, t), MXU_DTYPE),
                        pltpu.VMEM((3, 1, t), F32), pltpu.SemaphoreType.DMA((6, 3))],
        compiler_params=_params(VMEM_BIG),
    )(active, qa, qat, ka, kat, va, ot, lse, do, dot_)


def _fox_post(dqt, dka, dva, f, bf, carry=()):
    heads, nb, t, _ = dka.shape
    s = nb * t
    _, _, _, put = _fox_constants()
    carry = list(carry)
    anywhere = pl.BlockSpec(memory_space=pl.ANY)

    def body(dqt_ref, dka_ref, dva_ref, f_ref, bf_ref, put_ref, dq_ref, dk_ref, dv_ref, df_ref, sums_ref,
             dc_ref, carry_ref):
        i, h = pl.program_id(0), pl.program_id(1)

        @pl.when((i == 0) & (h == 0))
        def _():
            carry_ref[...] = jnp.zeros_like(carry_ref)
            sums_ref[...] = jnp.zeros_like(sums_ref)

        @pl.when(h == 0)
        def _():
            dc_ref[...] = jnp.zeros_like(dc_ref)

        dqt_v = dqt_ref[...]
        dka_v = dka_ref[...]
        term_q = _dot(_mx(dqt_v), put_ref[0], TN)
        term_k = _dot(_mx(dka_v), put_ref[1])
        term_v = _dot(_mx(dva_ref[...]), put_ref[1])

        @pl.when(h % 2 == 0)
        def _():
            dq_ref[...] = _mx(term_q)
            dk_ref[...] = _mx(term_k)
            dv_ref[...] = _mx(term_v)

        @pl.when(h % 2 == 1)
        def _():
            dq_ref[...] += _mx(term_q)
            dk_ref[...] += _mx(term_k)
            dv_ref[...] += _mx(term_v)

        dcum = dqt_v[BIAS_Q[0]:BIAS_Q[0] + 1, :] - dka_v.T[BIAS_K[0]:BIAS_K[0] + 1, :]
        head_row = lax.broadcasted_iota(jnp.int32, (heads, 1), 0) == h
        dc_ref[...] += jnp.where(head_row, dcum, 0.0)

        @pl.when(h == heads - 1)
        def _():
            later = (lax.broadcasted_iota(jnp.int32, (t, t), 0) >= lax.broadcasted_iota(jnp.int32, (t, t), 1)).astype(F32)
            dlf_t = _dot_onehot(dc_ref[...], later) + carry_ref[:, 0:1]
            carry_ref[...] = jnp.broadcast_to(dlf_t[:, 0:1], carry_ref.shape)
            dlf = jnp.concatenate([dlf_t, jnp.zeros((LANES - heads, t), F32)], axis=0).T
            lane = lax.broadcasted_iota(jnp.int32, (1, LANES), 1)
            df = jnp.where(lane < heads, dlf * jax.nn.sigmoid(-(f_ref[...] + bf_ref[0:1, :])), 0.0)
            df_ref[...] = _mx(df)
            sums_ref[0:1, :] += jnp.sum(df, axis=0, keepdims=True)

    rev = lambda i: nb - 1 - i
    pair_spec = pl.BlockSpec((t, LANES), lambda i, h: (rev(i), h // 2))
    blk = pl.BlockSpec((t, LANES), lambda i, h: (rev(i), 0))
    hd = jax.ShapeDtypeStruct((s, D_MODEL), MXU_DTYPE)
    res = pl.pallas_call(
        _carrying(body, 6, 5, 2, carry, (nb, heads)), name="fox_post", grid=(nb, heads),
        in_specs=[pl.BlockSpec((None, None, LANES, t), lambda i, h: (h, rev(i), 0, 0)),
                  pl.BlockSpec((None, None, t, LANES), lambda i, h: (h, rev(i), 0, 0)),
                  pl.BlockSpec((None, None, t, LANES), lambda i, h: (h, rev(i), 0, 0)),
                  blk, pl.BlockSpec((8, LANES), lambda i, h: (0, 0)),
                  pl.BlockSpec((None, 2, LANES, LANES), lambda i, h: (h % 2, 0, 0, 0))] + [anywhere] * len(carry),
        out_specs=[pair_spec, pair_spec, pair_spec, blk, pl.BlockSpec((8, LANES), lambda i, h: (0, 0))]
        + [anywhere] * len(carry),
        out_shape=[hd, hd, hd, jax.ShapeDtypeStruct((s, LANES), MXU_DTYPE), jax.ShapeDtypeStruct((8, LANES), F32)]
        + _exchange_shapes(carry),
        scratch_shapes=[pltpu.VMEM((heads, t), F32), pltpu.VMEM((heads, LANES), F32)]
        + (_exchange_sems(carry) if carry else []),
        compiler_params=_params(VMEM_BIG),
    )(dqt, dka, dva, f, bf, _mx(jnp.asarray(put)), *[arr for _, arr in carry])
    return (*res[:5], list(res[5:]))


def _fox_weights(w_in, w_o):
    wqkv = w_in[:, :3 * D_MODEL]
    wf = jnp.pad(w_in[:, 3 * D_MODEL:], ((0, 0), (0, LANES - FOX_HEADS)))
    wo_heads = w_o.reshape(FOX_HEADS, FOX_HEAD_DIM, D_MODEL)
    wo_a = jnp.pad(wo_heads, ((0, 0), (0, LANES - FOX_HEAD_DIM), (0, 0)))
    wo_rows = wo_a.reshape(FOX_HEADS * LANES, D_MODEL)
    return dict(wqkv=wqkv, wf=wf, wqkv_t=wqkv.T, wf_t=wf.T, wo_rows=wo_rows, wo_rows_t=wo_rows.T)


def _fox_out(ot, wo_rows):
    heads, nb, _, t = ot.shape

    def body(ot_ref, w_ref, y_ref):
        y_ref[...] = _dot(ot_ref[...].reshape(heads * LANES, t), w_ref[...], TN)

    return pl.pallas_call(
        body, name="fox_out", grid=(nb,),
        in_specs=[pl.BlockSpec((heads, None, LANES, t), lambda i: (0, i, 0, 0)),
                  pl.BlockSpec((heads * LANES, D_MODEL), lambda i: (0, 0))],
        out_specs=pl.BlockSpec((t, D_MODEL), lambda i: (i, 0)),
        out_shape=jax.ShapeDtypeStruct((nb * t, D_MODEL), F32),
        compiler_params=_params(VMEM_BIG),
    )(ot, wo_rows)


def _fox_do(dy, wo_rows_t, nb, t):
    heads = FOX_HEADS

    def body(dy_ref, w_ref, do_ref, dot_ref):
        val = _dot(dy_ref[...], w_ref[...])
        for h in range(heads):
            blk = val[:, h * LANES:(h + 1) * LANES]
            do_ref[h] = _mx(blk)
            dot_ref[h] = _mx(blk.T)

    return pl.pallas_call(
        body, name="fox_do", grid=(nb,),
        in_specs=[pl.BlockSpec((t, D_MODEL), lambda i: (i, 0)),
                  pl.BlockSpec((D_MODEL, heads * LANES), lambda i: (0, 0))],
        out_specs=[pl.BlockSpec((heads, None, t, LANES), lambda i: (0, i, 0, 0)),
                   pl.BlockSpec((heads, None, LANES, t), lambda i: (0, i, 0, 0))],
        out_shape=[jax.ShapeDtypeStruct((heads, nb, t, LANES), MXU_DTYPE),
                   jax.ShapeDtypeStruct((heads, nb, LANES, t), MXU_DTYPE)],
        compiler_params=_params(VMEM_BIG),
    )(dy, wo_rows_t)


def _fox_dwo(ot, dy):
    heads, nb, _, t = ot.shape

    def body(ot_ref, dy_ref, o_ref):
        part = _dot(ot_ref[...].reshape(heads * LANES, t), dy_ref[...])

        @pl.when(pl.program_id(0) == 0)
        def _():
            o_ref[...] = part

        @pl.when(pl.program_id(0) > 0)
        def _():
            o_ref[...] += part

    return pl.pallas_call(
        body, name="fox_dwo", grid=(nb,),
        in_specs=[pl.BlockSpec((heads, None, LANES, t), lambda i: (0, i, 0, 0)),
                  pl.BlockSpec((t, D_MODEL), lambda i: (i, 0))],
        out_specs=pl.BlockSpec((heads * LANES, D_MODEL), lambda i: (0, 0)),
        out_shape=jax.ShapeDtypeStruct((heads * LANES, D_MODEL), F32),
        compiler_params=_params(VMEM_BIG),
    )(ot, dy)


def _fox_fwd(u, w, bf, carry=()):
    qkv = _mm_nn("fox_qkv", u, w["wqkv"], MXU_DTYPE)
    f = _mm_nn("fox_f", u, w["wf"], F32)
    qa, qat, ka, kat, va, vat, stats, carried = _fox_prep(qkv, f, bf, carry)
    ot, lse = _fox_attn_fwd(qat, ka, vat, _fox_active(stats))
    y = _fox_out(ot, w["wo_rows"])
    return y, dict(f=f, qa=qa, qat=qat, ka=ka, kat=kat, va=va, ot=ot, lse=lse, stats=stats), carried


def _fox_bwd(dy, u, w, bf, res, carry=()):
    heads, nb, t, _ = res["qa"].shape
    do, dot_ = _fox_do(dy, w["wo_rows_t"], nb, t)
    dwo_a = _fox_dwo(res["ot"], dy).reshape(heads, LANES, D_MODEL)
    dqt, dka, dva = _fox_attn_bwd(res["qa"], res["qat"], res["ka"], res["kat"], res["va"], res["ot"],
                                  res["lse"], do, dot_, _fox_active(res["stats"]))
    dq, dk, dv, df, sums, carried = _fox_post(dqt, dka, dva, res["f"], bf, carry)
    dw_in = jnp.concatenate(
        [_mm_tn("fox_dw_q", u, dq), _mm_tn("fox_dw_k", u, dk), _mm_tn("fox_dw_v", u, dv),
         _mm_tn("fox_dw_f", u, df)[:, :FOX_HEADS]], axis=1)
    dw_o = dwo_a[:, :FOX_HEAD_DIM, :].reshape(D_MODEL, D_MODEL)
    wt = w["wqkv_t"]
    du, ex_in, ex_o = _mm_sum(
        "fox_du", [(dq, wt[:D_MODEL]), (dk, wt[D_MODEL:2 * D_MODEL]), (dv, wt[2 * D_MODEL:]), (df, w["wf_t"])],
        carry=[(True, _col_slots(dw_in[None])), (True, _row_slots(dw_o[None]))])
    return du, sums, carried + [ex_in, ex_o]


def _dsilu(v):
    sg = jax.nn.sigmoid(v)
    return sg * (1.0 + v * (1.0 - sg))


def _conv_taps(scr_ref, w_ref, rows, base):
    acc = None
    for k in range(SSM_CONV):
        term = scr_ref[pl.ds(base - (SSM_CONV - 1) + k, rows), :] * w_ref[k:k + 1, :]
        acc = term if acc is None else acc + term
    return acc


def _conv_fwd(zx, cw, cb):
    s = zx.shape[0]
    tb = _tile(s, 512)
    half = SSM_CONV_DIM // 2
    hb = tb // SUBLANES

    def body(x_ref, halo_ref, w_ref, b_ref, o_ref, scr_ref):
        i = pl.program_id(0)
        scr_ref[pl.ds(0, SUBLANES), :] = jnp.where(i > 0, halo_ref[...], 0.0)
        scr_ref[pl.ds(SUBLANES, tb), :] = x_ref[...]
        o_ref[...] = jax.nn.silu(_conv_taps(scr_ref, w_ref, tb, SUBLANES) + b_ref[0:1, :])

    return pl.pallas_call(
        body, name="ssd_conv_fwd", grid=(s // tb, 2),
        in_specs=[pl.BlockSpec((tb, half), lambda i, j: (i, 1 + j)),
                  pl.BlockSpec((SUBLANES, half), lambda i, j: (jnp.maximum(i * hb - 1, 0), 1 + j)),
                  pl.BlockSpec((8, half), lambda i, j: (0, j)),
                  pl.BlockSpec((8, half), lambda i, j: (0, j))],
        out_specs=pl.BlockSpec((tb, half), lambda i, j: (i, j)),
        out_shape=jax.ShapeDtypeStruct((s, SSM_CONV_DIM), F32),
        scratch_shapes=[pltpu.VMEM((tb + SUBLANES, half), F32)],
    )(zx, zx, cw, cb)


def _conv_bwd_pre(zx, dxc, cw, cb):
    s = zx.shape[0]
    tb = _tile(s, 512)
    half = SSM_CONV_DIM // 2
    hb = tb // SUBLANES

    def body(x_ref, halo_ref, d_ref, w_ref, b_ref, o_ref, sums_ref, scr_ref):
        i = pl.program_id(1)

        @pl.when(i == 0)
        def _():
            sums_ref[...] = jnp.zeros_like(sums_ref)

        scr_ref[pl.ds(0, SUBLANES), :] = jnp.where(i > 0, halo_ref[...], 0.0)
        scr_ref[pl.ds(SUBLANES, tb), :] = x_ref[...]
        pre = _conv_taps(scr_ref, w_ref, tb, SUBLANES) + b_ref[0:1, :]
        dpre = d_ref[...] * _dsilu(pre)
        o_ref[...] = dpre
        for k in range(SSM_CONV):
            shifted = scr_ref[pl.ds(SUBLANES - (SSM_CONV - 1) + k, tb), :]
            sums_ref[k:k + 1, :] += jnp.sum(dpre * shifted, axis=0, keepdims=True)
        sums_ref[SSM_CONV:SSM_CONV + 1, :] += jnp.sum(dpre, axis=0, keepdims=True)

    return pl.pallas_call(
        body, name="ssd_conv_bwd_pre", grid=(2, s // tb),
        in_specs=[pl.BlockSpec((tb, half), lambda j, i: (i, 1 + j)),
                  pl.BlockSpec((SUBLANES, half), lambda j, i: (jnp.maximum(i * hb - 1, 0), 1 + j)),
                  pl.BlockSpec((tb, half), lambda j, i: (i, j)),
                  pl.BlockSpec((8, half), lambda j, i: (0, j)),
                  pl.BlockSpec((8, half), lambda j, i: (0, j))],
        out_specs=[pl.BlockSpec((tb, half), lambda j, i: (i, j)),
                   pl.BlockSpec((8, half), lambda j, i: (0, j))],
        out_shape=[jax.ShapeDtypeStruct((s, SSM_CONV_DIM), F32), jax.ShapeDtypeStruct((8, SSM_CONV_DIM), F32)],
        scratch_shapes=[pltpu.VMEM((tb + SUBLANES, half), F32)],
    )(zx, zx, dxc, cw, cb)


def _conv_bwd_x(dpre, cw):
    s = dpre.shape[0]
    tb = _tile(s, 512)
    hb = tb // SUBLANES
    nb = s // tb

    def body(d_ref, halo_ref, w_ref, o_ref, scr_ref):
        i = pl.program_id(0)
        scr_ref[pl.ds(0, tb), :] = d_ref[...]
        scr_ref[pl.ds(tb, SUBLANES), :] = jnp.where(i < nb - 1, halo_ref[...], 0.0)
        acc = None
        for k in range(SSM_CONV):
            term = scr_ref[pl.ds(SSM_CONV - 1 - k, tb), :] * w_ref[k:k + 1, :]
            acc = term if acc is None else acc + term
        o_ref[...] = _mx(acc)

    return pl.pallas_call(
        body, name="ssd_conv_bwd_x", grid=(nb,),
        in_specs=[pl.BlockSpec((tb, SSM_CONV_DIM), lambda i: (i, 0)),
                  pl.BlockSpec((SUBLANES, SSM_CONV_DIM), lambda i: (jnp.minimum((i + 1) * hb, s // SUBLANES - 1), 0)),
                  pl.BlockSpec((8, SSM_CONV_DIM), lambda i: (0, 0))],
        out_specs=pl.BlockSpec((tb, SSM_CONV_DIM), lambda i: (i, 0)),
        out_shape=jax.ShapeDtypeStruct((s, SSM_CONV_DIM), MXU_DTYPE),
        scratch_shapes=[pltpu.VMEM((tb + SUBLANES, SSM_CONV_DIM), F32)],
        compiler_params=_params(VMEM_BIG),
    )(dpre, dpre, cw)


def _expand_constants():
    ex = np.zeros((LANES, SSM_D_INNER), np.float32)
    for h in range(SSM_HEADS):
        ex[h, h * 64:(h + 1) * 64] = 1.0
    return ex, np.ascontiguousarray(ex.T)


def _ssd_common(dtr_ref, par_ref, ex_ref, xc_ref):
    lc = SSM_CHUNK
    lane = lax.broadcasted_iota(jnp.int32, (1, LANES), 1)
    is_head = lane < SSM_HEADS
    par = par_ref[...]
    pre = dtr_ref[...] + par[0:1, :]
    dt = jnp.where(is_head, jax.nn.softplus(pre), 0.0)
    a = jnp.where(is_head, -jnp.exp(par[1:2, :]), 0.0)
    tri_b = lax.broadcasted_iota(jnp.int32, (lc, lc), 0) >= lax.broadcasted_iota(jnp.int32, (lc, lc), 1)
    tri = tri_b.astype(F32)
    da = dt * a
    acs = _dot_onehot(tri, da, onehot="a")
    acs_t = _dot_onehot(da, tri, (((0,), (1,)), ((), ())))
    wide = _dot_onehot(jnp.concatenate([dt, acs, par], axis=0), ex_ref[...])
    dt_x, acs_x, d_x = wide[0:lc], wide[lc:2 * lc], wide[2 * lc + 2:2 * lc + 3]
    last_x = acs_x[lc - 1:lc, :]
    xs = xc_ref[:, 0:SSM_D_INNER]
    return dict(pre=pre, dt=dt, a=a, tri_b=tri_b, tri=tri, acs=acs, acs_t=acs_t, dt_x=dt_x, d_x=d_x, xs=xs,
                xdt=xs * dt_x, e_x=jnp.exp(acs_x), dte_x=jnp.exp(last_x - acs_x), cd_x=jnp.exp(last_x),
                is_head=is_head)


def _decay_in(q, h):
    seg = q["acs"][:, h:h + 1] - q["acs_t"][h:h + 1, :]
    return jnp.exp(jnp.where(q["tri_b"], seg, -jnp.inf))


def _ssd_scan_fwd(xc, dtr, par):
    s = xc.shape[0]
    lc = SSM_CHUNK
    nc = s // lc
    ex, _ = _expand_constants()

    def body(xc_ref, dtr_ref, par_ref, ex_ref, y_ref, prev_ref, st_ref):
        @pl.when(pl.program_id(0) == 0)
        def _():
            st_ref[...] = jnp.zeros_like(st_ref)

        q = _ssd_common(dtr_ref, par_ref, ex_ref, xc_ref)
        lane = lax.broadcasted_iota(jnp.int32, (1, LANES), 1)
        for g in range(SSM_GROUPS):
            sl = slice(g * GROUP_W, (g + 1) * GROUP_W)
            bg = _mx(xc_ref[:, SSM_D_INNER + g * SSM_STATE:SSM_D_INNER + (g + 1) * SSM_STATE])
            cg = _mx(xc_ref[:, SSM_D_INNER + (SSM_GROUPS + g) * SSM_STATE:SSM_D_INNER + (SSM_GROUPS + g + 1) * SSM_STATE])
            gm = _dot(cg, bg, NT)
            prev = st_ref[g]
            prev_ref[g] = prev
            yoff = _dot(cg, _mx(prev)) * q["e_x"][:, sl]
            st_ref[g] = prev * q["cd_x"][:, sl] + _dot(bg, _mx(q["xdt"][:, sl] * q["dte_x"][:, sl]), TN)
            pairs = []
            for pr in range(2):
                xp = _mx(q["xdt"][:, g * GROUP_W + pr * LANES:g * GROUP_W + (pr + 1) * LANES])
                both = [_dot(_mx(gm * _decay_in(q, 4 * g + 2 * pr + r2)), xp) for r2 in range(2)]
                pairs.append(jnp.where(lane < 64, both[0], both[1]))
            y_ref[:, sl] = jnp.concatenate(pairs, axis=1) + yoff + q["xs"][:, sl] * q["d_x"][:, sl]

    return pl.pallas_call(
        body, name="ssd_scan_fwd", grid=(nc,),
        in_specs=[pl.BlockSpec((lc, SSM_CONV_DIM), lambda c: (c, 0)),
                  pl.BlockSpec((lc, LANES), lambda c: (c, 0)),
                  pl.BlockSpec((8, LANES), lambda c: (0, 0)),
                  pl.BlockSpec((LANES, SSM_D_INNER), lambda c: (0, 0))],
        out_specs=[pl.BlockSpec((lc, SSM_D_INNER), lambda c: (c, 0)),
                   pl.BlockSpec((None, SSM_GROUPS, SSM_STATE, GROUP_W), lambda c: (c, 0, 0, 0))],
        out_shape=[jax.ShapeDtypeStruct((s, SSM_D_INNER), F32),
                   jax.ShapeDtypeStruct((nc, SSM_GROUPS, SSM_STATE, GROUP_W), F32)],
        scratch_shapes=[pltpu.VMEM((SSM_GROUPS, SSM_STATE, GROUP_W), F32)],
        compiler_params=_params(VMEM_BIG),
    )(xc, dtr, par, _mx(jnp.asarray(ex)))


def _ssd_scan_bwd(dy, xc, dtr, par, prev, carry=()):
    s = xc.shape[0]
    lc = SSM_CHUNK
    nc = s // lc
    ex, ex_t = _expand_constants()
    carry = list(carry)
    anywhere = pl.BlockSpec(memory_space=pl.ANY)

    def body(dy_ref, xc_ref, dtr_ref, par_ref, prev_ref, ex_ref, ext_ref, dxc_ref, ddtr_ref, sums_ref,
             gst_ref, tacs_ref, tdt_ref, tdd_ref):
        @pl.when(pl.program_id(0) == 0)
        def _():
            gst_ref[...] = jnp.zeros_like(gst_ref)
            sums_ref[...] = jnp.zeros_like(sums_ref)

        q = _ssd_common(dtr_ref, par_ref, ex_ref, xc_ref)
        lane = lax.broadcasted_iota(jnp.int32, (1, LANES), 1)
        row = lax.broadcasted_iota(jnp.int32, (lc, 1), 0)
        dacs_rows = jnp.zeros((lc, LANES), F32)
        dacs_cols_t = jnp.zeros((LANES, lc), F32)
        for g in range(SSM_GROUPS):
            sl = slice(g * GROUP_W, (g + 1) * GROUP_W)
            b_lo = SSM_D_INNER + g * SSM_STATE
            c_lo = SSM_D_INNER + (SSM_GROUPS + g) * SSM_STATE
            bg = _mx(xc_ref[:, b_lo:b_lo + SSM_STATE])
            cg = _mx(xc_ref[:, c_lo:c_lo + SSM_STATE])
            dyg = dy_ref[:, sl]
            xsg, xdtg = q["xs"][:, sl], q["xdt"][:, sl]
            eg, dteg, cdg = q["e_x"][:, sl], q["dte_x"][:, sl], q["cd_x"][:, sl]
            prevg = prev_ref[g]
            gs = gst_ref[g]
            prevm, gsm = _mx(prevg), _mx(gs)
            tdd_ref[:, sl] = dyg * xsg
            dxs = dyg * q["d_x"][:, sl]
            t_acs = dyg * _dot(cg, prevm) * eg
            dcp = _mx(dyg * eg)
            dc = _dot(dcp, prevm, NT)
            dprev = _dot(cg, dcp, TN)
            db = _dot(_mx(xdtg * dteg), gsm, NT)
            dx2 = _dot(bg, gsm)
            dxdt = dx2 * dteg
            ddte = dx2 * xdtg * dteg
            t_acs = t_acs - ddte
            last = (jnp.sum(ddte, axis=0, keepdims=True)
                    + jnp.sum(gs * prevg, axis=0, keepdims=True) * cdg)
            gm = _dot(cg, bg, NT)
            dgm = jnp.zeros((lc, lc), F32)
            pair_dx = []
            for pr in range(2):
                lo = g * GROUP_W + pr * LANES
                xp = _mx(q["xdt"][:, lo:lo + LANES])
                dyp = dy_ref[:, lo:lo + LANES]
                both = []
                for r2 in range(2):
                    h = 4 * g + 2 * pr + r2
                    mine = (lane >= 64 * r2) & (lane < 64 * (r2 + 1))
                    lm = _decay_in(q, h)
                    m = gm * lm
                    dm = _dot(_mx(jnp.where(mine, dyp, 0.0)), xp, NT)
                    dgm = dgm + dm * lm
                    w = dm * m
                    dacs_rows = dacs_rows + jnp.sum(w, axis=1, keepdims=True) * (lane == h).astype(F32)
                    head_row = (lax.broadcasted_iota(jnp.int32, (LANES, 1), 0) == h).astype(F32)
                    dacs_cols_t = dacs_cols_t + head_row * jnp.sum(w, axis=0, keepdims=True)
                    both.append(_dot(_mx(m), _mx(dyp), TN))
                pair_dx.append(jnp.where(lane < 64, both[0], both[1]))
            dxdt = dxdt + jnp.concatenate(pair_dx, axis=1)
            dgmm = _mx(dgm)
            dc = dc + _dot(dgmm, bg)
            db = db + _dot(dgmm, cg, TN)
            dxs = dxs + dxdt * q["dt_x"][:, sl]
            tdt_ref[:, sl] = dxdt * xsg
            tacs_ref[:, sl] = t_acs + jnp.where(row == lc - 1, last, 0.0)
            dxc_ref[:, sl] = dxs
            dxc_ref[:, b_lo:b_lo + SSM_STATE] = db
            dxc_ref[:, c_lo:c_lo + SSM_STATE] = dc
            gst_ref[g] = gs * cdg + dprev
        tdd = jnp.broadcast_to(jnp.sum(tdd_ref[...], axis=0, keepdims=True), (8, SSM_D_INNER))
        heads_of = _dot_onehot(jnp.concatenate([tacs_ref[...], tdt_ref[...], tdd], axis=0), ext_ref[...])
        dacs = heads_of[0:lc] + dacs_rows - dacs_cols_t.T
        dda = _dot_onehot(q["tri"], dacs, TN, onehot="a")
        ddt = dda * q["a"] + heads_of[lc:2 * lc]
        ddtr = jnp.where(q["is_head"], ddt * jax.nn.sigmoid(q["pre"]), 0.0)
        ddtr_ref[...] = _mx(ddtr)
        sums_ref[0:1, :] += jnp.sum(ddtr, axis=0, keepdims=True)
        sums_ref[1:2, :] += jnp.sum(dda * q["dt"], axis=0, keepdims=True) * q["a"]
        sums_ref[2:3, :] += heads_of[2 * lc:2 * lc + 1]

    rev = lambda c: nc - 1 - c
    wide = pltpu.VMEM((lc, SSM_D_INNER), F32)
    res = pl.pallas_call(
        _carrying(body, 7, 3, 4, carry, (nc,)), name="ssd_scan_bwd", grid=(nc,),
        in_specs=[pl.BlockSpec((lc, SSM_D_INNER), lambda c: (rev(c), 0)),
                  pl.BlockSpec((lc, SSM_CONV_DIM), lambda c: (rev(c), 0)),
                  pl.BlockSpec((lc, LANES), lambda c: (rev(c), 0)),
                  pl.BlockSpec((8, LANES), lambda c: (0, 0)),
                  pl.BlockSpec((None, SSM_GROUPS, SSM_STATE, GROUP_W), lambda c: (rev(c), 0, 0, 0)),
                  pl.BlockSpec((LANES, SSM_D_INNER), lambda c: (0, 0)),
                  pl.BlockSpec((SSM_D_INNER, LANES), lambda c: (0, 0))] + [anywhere] * len(carry),
        out_specs=[pl.BlockSpec((lc, SSM_CONV_DIM), lambda c: (rev(c), 0)),
                   pl.BlockSpec((lc, LANES), lambda c: (rev(c), 0)),
                   pl.BlockSpec((8, LANES), lambda c: (0, 0))] + [anywhere] * len(carry),
        out_shape=[jax.ShapeDtypeStruct((s, SSM_CONV_DIM), F32), jax.ShapeDtypeStruct((s, LANES), MXU_DTYPE),
                   jax.ShapeDtypeStruct((8, LANES), F32)] + _exchange_shapes(carry),
        scratch_shapes=[pltpu.VMEM((SSM_GROUPS, SSM_STATE, GROUP_W), F32), wide, wide, wide]
        + (_exchange_sems(carry) if carry else []),
        compiler_params=_params(VMEM_BIG),
    )(dy, xc, dtr, par, prev, _mx(jnp.asarray(ex)), _mx(jnp.asarray(ex_t)), *[arr for _, arr in carry])
    return (*res[:3], list(res[3:]))


def _group_norm_parts(yv, zv):
    yg = yv * jax.nn.silu(zv)
    normed, rinvs = [], []
    for g in range(SSM_GROUPS):
        blk = yg[:, g * GROUP_W:(g + 1) * GROUP_W]
        rinv = lax.rsqrt(jnp.mean(blk * blk, axis=-1, keepdims=True) + RMS_EPS)
        normed.append(blk * rinv)
        rinvs.append(rinv)
    return normed, rinvs


def _gnorm_fwd(y, zx, nw):
    s = y.shape[0]
    tb = _tile(s, 512)

    def body(y_ref, z_ref, w_ref, o_ref):
        normed, _ = _group_norm_parts(y_ref[...], z_ref[...])
        for g in range(SSM_GROUPS):
            sl = slice(g * GROUP_W, (g + 1) * GROUP_W)
            o_ref[:, sl] = _mx(normed[g] * w_ref[0:1, sl])

    row = pl.BlockSpec((tb, SSM_D_INNER), lambda i: (i, 0))
    return pl.pallas_call(
        body, name="ssd_gnorm_fwd", grid=(s // tb,),
        in_specs=[row, row, pl.BlockSpec((8, SSM_D_INNER), lambda i: (0, 0))],
        out_specs=row, out_shape=jax.ShapeDtypeStruct((s, SSM_D_INNER), MXU_DTYPE),
    )(y, zx, nw)


def _gnorm_bwd(y, zx, nw, dyn):
    s = y.shape[0]
    tb = _tile(s, 512)

    def body(y_ref, z_ref, w_ref, d_ref, dy_ref, dz_ref, sums_ref):
        @pl.when(pl.program_id(0) == 0)
        def _():
            sums_ref[...] = jnp.zeros_like(sums_ref)

        yv, zv = y_ref[...], z_ref[...]
        normed, rinvs = _group_norm_parts(yv, zv)
        gate = jax.nn.silu(zv)
        dgate = _dsilu(zv)
        for g in range(SSM_GROUPS):
            sl = slice(g * GROUP_W, (g + 1) * GROUP_W)
            dv = d_ref[:, sl]
            n = normed[g]
            sums_ref[0:1, sl] += jnp.sum(dv * n, axis=0, keepdims=True)
            dn = dv * w_ref[0:1, sl]
            dyg = rinvs[g] * (dn - n * jnp.mean(dn * n, axis=-1, keepdims=True))
            dy_ref[:, sl] = dyg * gate[:, sl]
            dz_ref[:, sl] = _mx(dyg * yv[:, sl] * dgate[:, sl])

    row = pl.BlockSpec((tb, SSM_D_INNER), lambda i: (i, 0))
    par = pl.BlockSpec((8, SSM_D_INNER), lambda i: (0, 0))
    return pl.pallas_call(
        body, name="ssd_gnorm_bwd", grid=(s // tb,),
        in_specs=[row, row, par, row], out_specs=[row, row, par],
        out_shape=[jax.ShapeDtypeStruct((s, SSM_D_INNER), F32), jax.ShapeDtypeStruct((s, SSM_D_INNER), MXU_DTYPE),
                   jax.ShapeDtypeStruct((8, SSM_D_INNER), F32)],
    )(y, zx, nw, dyn)


def _rows8(v):
    v = v.reshape(1, -1)
    return jnp.pad(v, ((0, 7), (0, 0)))


def _ssd_weights(w_in, w_out):
    nzx = SSM_D_INNER + SSM_CONV_DIM
    wzx = w_in[:, :nzx]
    wdt = jnp.pad(w_in[:, nzx:], ((0, 0), (0, LANES - SSM_HEADS)))
    return dict(wzx=wzx, wdt=wdt, wzx_t=wzx.T, wdt_t=wdt.T, wout=w_out, wout_t=w_out.T)


def _ssd_fwd(u, w, cw, cb, par, nw):
    zx = _mm_nn("ssd_in_zx", u, w["wzx"], F32)
    dtr = _mm_nn("ssd_in_dt", u, w["wdt"], F32)
    xc = _conv_fwd(zx, cw, cb)
    y, prev = _ssd_scan_fwd(xc, dtr, par)
    yn = _gnorm_fwd(y, zx, nw)
    out = _mm_nn("ssd_out", yn, w["wout"], F32)
    return out, dict(zx=zx, dtr=dtr, xc=xc, y=y, prev=prev, yn=yn)


def _ssd_bwd(dy, u, w, cw, cb, par, nw, res, carry=()):
    dyn = _mm_nn("ssd_dyn", dy, w["wout_t"], F32)
    dw_out = _mm_tn("ssd_dw_out", res["yn"], dy)
    dys, dz, nsum = _gnorm_bwd(res["y"], res["zx"], nw, dyn)
    dxc, ddtr, ssum, carried = _ssd_scan_bwd(dys, res["xc"], res["dtr"], par, res["prev"],
                                             list(carry) + [(True, _row_slots(dw_out[None]))])
    dpre, csum = _conv_bwd_pre(res["zx"], dxc, cw, cb)
    dxbc = _conv_bwd_x(dpre, cw)
    wt = w["wzx_t"]
    du = _mm_sum("ssd_du", [(dz, wt[:SSM_D_INNER]), (dxbc, wt[SSM_D_INNER:]), (ddtr, w["wdt_t"])], tn=512)
    dw_in = jnp.concatenate(
        [_mm_tn("ssd_dw_z", u, dz), _mm_tn("ssd_dw_x", u, dxbc), _mm_tn("ssd_dw_dt", u, ddtr)[:, :SSM_HEADS]], axis=1)
    small = dict(conv_w=csum[:SSM_CONV], conv_b=csum[SSM_CONV], dt_bias=ssum[0, :SSM_HEADS],
                 a_log=ssum[1, :SSM_HEADS], d=ssum[2, :SSM_HEADS], norm_w=nsum[0])
    return du, dw_in, small, carried


def _ada_fwd(c_all, ada_w, ada_b_mine):
    nl, _, ncol = ada_w.shape

    def body(c_ref, w_ref, b_ref, o_ref):
        cond = _mx(jax.nn.silu(c_ref[...]))
        for i in range(nl):
            o_ref[i] = _dot(cond, _mx(w_ref[i])) + b_ref[i:i + 1, :]

    return pl.pallas_call(
        body, name="ada_fwd", out_shape=jax.ShapeDtypeStruct((nl, 2 * N_DEV, ncol), F32),
        compiler_params=_params(VMEM_BIG),
    )(c_all, ada_w, ada_b_mine)


def _ada_bwd(c_all, dmod_cols):
    nl, _, ncol = dmod_cols.shape

    def body(c_ref, d_ref, o_ref):
        cond = _mx(jax.nn.silu(c_ref[...]))
        for i in range(nl):
            o_ref[i] = _dot(cond, _mx(d_ref[i]), TN)

    return pl.pallas_call(
        body, name="ada_bwd", out_shape=jax.ShapeDtypeStruct((nl, D_MODEL, ncol), F32),
        compiler_params=_params(VMEM_BIG),
    )(c_all, dmod_cols)


def _adamw(gslots, w, m, v, name):
    k, r, c = gslots.shape
    tr = _tile(r, 256) if r % 256 == 0 else r
    c1 = 1.0 - ADAM_B1 ** ADAM_STEP
    c2 = 1.0 - ADAM_B2 ** ADAM_STEP

    def body(g_ref, w_ref, m_ref, v_ref, go_ref, d_ref, mo_ref, vo_ref):
        g = g_ref[0]
        for slot in range(1, k):
            g = g + g_ref[slot]
        mn = ADAM_B1 * m_ref[...] + (1.0 - ADAM_B1) * g
        vn = ADAM_B2 * v_ref[...] + (1.0 - ADAM_B2) * jnp.square(g)
        go_ref[...] = g
        mo_ref[...] = mn
        vo_ref[...] = vn
        d_ref[...] = -ADAM_LR * ((mn / c1) / (jnp.sqrt(vn / c2) + ADAM_EPS) + ADAM_WD * w_ref[...])

    row = pl.BlockSpec((tr, c), lambda i: (i, 0))
    shp = jax.ShapeDtypeStruct((r, c), F32)
    return pl.pallas_call(
        body, name=name, grid=(r // tr,),
        in_specs=[pl.BlockSpec((k, tr, c), lambda i: (0, i, 0)), row, row, row],
        out_specs=[row, row, row, row], out_shape=[shp, shp, shp, shp],
        compiler_params=_params(VMEM_BIG),
    )(gslots, w, m, v)


def _adamw_any(gslots, w, m, v, name):
    shape = w.shape
    two_d = (-1, shape[-1])
    k = gslots.shape[0]
    outs = _adamw(gslots.reshape((k,) + w.reshape(two_d).shape), w.reshape(two_d), m.reshape(two_d),
                  v.reshape(two_d), name)
    return tuple(o.reshape(shape) for o in outs)


def _cols_from_slots(g):
    g = jnp.moveaxis(g, 0, -2)
    return g.reshape(g.shape[:-2] + (g.shape[-2] * g.shape[-1],))


def _rows_from_slots(g):
    g = jnp.moveaxis(g, 0, -3)
    return g.reshape(g.shape[:-3] + (g.shape[-3] * g.shape[-2], g.shape[-1]))


def _col_slots(g):
    cs = g.shape[-1] // N_DEV
    return jnp.moveaxis(g.reshape(g.shape[:-1] + (N_DEV, cs)), -2, 0)


def _row_slots(g):
    rs = g.shape[-2] // N_DEV
    return jnp.moveaxis(g.reshape(g.shape[:-2] + (N_DEV, rs, g.shape[-1])), -3, 0)


def _gather_cols(w, name, dtype=None):
    return _cols_from_slots(_all_gather(w.astype(dtype or MXU_DTYPE), name))


def _gather_rows(w, name):
    return _rows_from_slots(_all_gather(_mx(w), name))


def kernel(x, c, ada_w, ada_b, ln_mix_g, ln_mix_b, ln_mlp_g, ln_mlp_b, mlp_w1, mlp_w2, fox_w_in, fox_b_f, fox_w_o, ssm_w_in, ssm_conv_w, ssm_conv_b, ssm_dt_bias, ssm_a_log, ssm_d, ssm_norm_w, ssm_w_out, loss_target, m_ada_w, m_ada_b, m_ln_mix_g, m_ln_mix_b, m_ln_mlp_g, m_ln_mlp_b, m_mlp_w1, m_mlp_w2, m_fox_w_in, m_fox_b_f, m_fox_w_o, m_ssm_w_in, m_ssm_conv_w, m_ssm_conv_b, m_ssm_dt_bias, m_ssm_a_log, m_ssm_d, m_ssm_norm_w, m_ssm_w_out, v_ada_w, v_ada_b, v_ln_mix_g, v_ln_mix_b, v_ln_mlp_g, v_ln_mlp_b, v_mlp_w1, v_mlp_w2, v_fox_w_in, v_fox_b_f, v_fox_w_o, v_ssm_w_in, v_ssm_conv_w, v_ssm_conv_b, v_ssm_dt_bias, v_ssm_a_log, v_ssm_d, v_ssm_norm_w, v_ssm_w_out):
    me = 4 * lax.axis_index("x") + 2 * lax.axis_index("y") + lax.axis_index("c")
    xs = x[0]
    target = loss_target[0]
    d = D_MODEL

    c_all = _all_gather(c, "gather_c").reshape(N_DEV, d)
    c_all = jnp.pad(c_all, ((0, N_DEV), (0, 0)))
    ncol = ada_w.shape[-1]
    ada_b_mine = lax.dynamic_slice_in_dim(ada_b, me * ncol, ncol, axis=1)
    mod_cols = _ada_fwd(c_all, ada_w, ada_b_mine)
    mod_all = _all_gather(mod_cols, "gather_mod")
    mod = lax.dynamic_index_in_dim(mod_all, me, axis=2, keepdims=False)
    mod = jnp.moveaxis(mod, 0, 1).reshape(DEPTH, 6, d)

    def pv_rows(*rows):
        return jnp.pad(jnp.stack(rows), ((0, 8 - len(rows)), (0, 0)))

    fw = _fox_weights(_gather_cols(fox_w_in, "gather_fox_in")[0], _gather_rows(fox_w_o, "gather_fox_o")[0])
    conv_w = _gather_cols(ssm_conv_w, "gather_conv_w", F32)[0]
    small_vec = jnp.concatenate([ssm_conv_b[0], ssm_norm_w[0]]).reshape(1, -1)
    small_all = _all_gather(small_vec.astype(F32), "gather_conv_b").reshape(N_DEV, -1)
    conv_b = small_all[:, :SSM_CONV_DIM // N_DEV].reshape(-1)
    norm_w = small_all[:, SSM_CONV_DIM // N_DEV:].reshape(-1)
    cw8 = jnp.pad(conv_w, ((0, 8 - SSM_CONV), (0, 0)))
    cb8 = _rows8(conv_b)
    nw8 = _rows8(norm_w)
    bf8 = _rows8(jnp.pad(fox_b_f[0], (0, LANES - FOX_HEADS)))
    par8 = jnp.pad(jnp.stack([jnp.pad(p[0], (0, LANES - SSM_HEADS)) for p in (ssm_dt_bias, ssm_a_log, ssm_d)]),
                   ((0, 5), (0, 0)))

    sh_a, sc_a, g_a, sh_m, sc_m, g_m = [mod[:, k] for k in range(6)]
    u0 = _modulate(xs, pv_rows(1.0 + sc_a[0], sh_a[0]), "modulate0")
    y0, fres, gathered = _fox_fwd(u0, fw, bf8, [(False, _mx(mlp_w1)), (False, _mx(mlp_w2))])
    w1 = _cols_from_slots(gathered[0])
    w2 = _rows_from_slots(gathered[1])
    pv0 = pv_rows(1.0 + g_a[0], ln_mix_g[0], ln_mix_b[0], 1.0 + sc_m[0], sh_m[0])
    x1, u1 = _ln_fwd(xs, y0, pv0, "ln_mix0")
    y1, (h0, a0), gathered = _mlp_fwd(u1, w1[0], w2[0], "0", [(False, _mx(ssm_w_in)), (False, _mx(ssm_w_out))])
    sw = _ssd_weights(_cols_from_slots(gathered[0])[0], _rows_from_slots(gathered[1])[0])
    pv1 = pv_rows(1.0 + g_m[0], ln_mlp_g[0], ln_mlp_b[0], 1.0 + sc_a[1], sh_a[1])
    x2, u2 = _ln_fwd(x1, y1, pv1, "ln_mlp0")
    y2, sres = _ssd_fwd(u2, sw, cw8, cb8, par8, nw8)
    pv2 = pv_rows(1.0 + g_a[1], ln_mix_g[1], ln_mix_b[1], 1.0 + sc_m[1], sh_m[1])
    x3, u3 = _ln_fwd(x2, y2, pv2, "ln_mix1")
    y3, (h1, a1), _ = _mlp_fwd(u3, w1[1], w2[1], "1")
    pv3 = pv_rows(1.0 + g_m[1], ln_mlp_g[1], ln_mlp_b[1])

    dx3, dy3, s3 = _ln_bwd(x3, y3, pv3, "ln_mlp1_bwd", target=target)
    loss = lax.psum(s3[5, 0], ("x", "y", "c"))
    du3, dw1_1, dw2_1, _ = _mlp_bwd(dy3, u3, h1, a1, w1[1].T, w2[1].T, "1")
    dx2, dy2, s2 = _ln_bwd(x2, y2, pv2, "ln_mix1_bwd", dxo=dx3, du=du3)
    du2, d_ssm_in, ssm_small, ex_scan = _ssd_bwd(
        dy2, u2, sw, cw8, cb8, par8, nw8, sres,
        [(True, _col_slots(dw1_1[None])), (True, _row_slots(dw2_1[None]))])
    dx1, dy1, s1 = _ln_bwd(x1, y1, pv1, "ln_mlp0_bwd", dxo=dx2, du=du2)
    half = d_ssm_in.shape[0] // 2
    du1, dw1_0, dw2_0, ex_mlp = _mlp_bwd(dy1, u1, h0, a0, w1[0].T, w2[0].T, "0",
                                         [(True, _col_slots(d_ssm_in[None, :half]))],
                                         [(True, _col_slots(d_ssm_in[None, half:]))])
    dx0, dy0, s0 = _ln_bwd(xs, y0, pv0, "ln_mix0_bwd", dxo=dx1, du=du1)
    late = [(True, _col_slots(dw1_0[None])), (True, _row_slots(dw2_0[None])),
            (True, _col_slots(ssm_small["conv_w"][None])), (True, _col_slots(ssm_small["conv_b"][None])),
            (True, _col_slots(ssm_small["norm_w"][None]))]
    du0, fox_sums, ex_post = _fox_bwd(dy0, u0, fw, bf8, fres, late)
    grad_x, sx = _mod_bwd(dx0, du0, xs, pv_rows(1.0 + sc_a[0], sh_a[0]), "modulate0_bwd")

    dmod = jnp.stack([
        jnp.stack([sx[1], sx[0], s0[4], s0[1], s0[0], s1[4]]),
        jnp.stack([s1[1], s1[0], s2[4], s2[1], s2[0], s3[4]]),
    ]).reshape(DEPTH, 6 * d)

    def pad_rows(v):
        v = v.reshape(-1, LANES) if v.size % LANES == 0 else jnp.pad(v.reshape(-1), (0, LANES - v.size)).reshape(1, LANES)
        return jnp.pad(v, ((0, (-v.shape[0]) % 8), (0, 0)))

    small_parts = [dmod, jnp.stack([s0[2], s2[2]]), jnp.stack([s0[3], s2[3]]), jnp.stack([s1[2], s3[2]]),
                   jnp.stack([s1[3], s3[3]]), fox_sums[0, :FOX_HEADS], ssm_small["dt_bias"], ssm_small["a_log"],
                   ssm_small["d"]]
    packed = [pad_rows(p) for p in small_parts]
    offsets = np.cumsum([0] + [p.shape[0] for p in packed])
    small_all_g = _all_gather(jnp.concatenate(packed, axis=0), "gather_small_grads")

    def unpack(idx, shape):
        n = int(np.prod(shape))
        blk = small_all_g[:, offsets[idx]:offsets[idx + 1]].reshape(N_DEV, -1)[:, :n]
        return blk.reshape((N_DEV,) + tuple(shape))

    dmod_all = unpack(0, (DEPTH, 6 * d))
    dmod_cols = lax.dynamic_slice_in_dim(dmod_all, me * ncol, ncol, axis=2)
    dmod_cols = jnp.pad(jnp.moveaxis(dmod_cols, 0, 1), ((0, 0), (0, N_DEV), (0, 0)))
    g_ada_w = _ada_bwd(c_all, dmod_cols)

    shards = dict(
        mlp_w1=jnp.concatenate([ex_post[0], ex_scan[0]], axis=1),
        mlp_w2=jnp.concatenate([ex_post[1], ex_scan[1]], axis=1),
        ssm_w_in=jnp.concatenate(ex_mlp, axis=2), ssm_w_out=ex_scan[2],
        ssm_conv_w=ex_post[2], ssm_conv_b=ex_post[3], ssm_norm_w=ex_post[4],
        fox_w_in=ex_post[5], fox_w_o=ex_post[6],
        ada_w=g_ada_w[None], ada_b=dmod_all,
        ln_mix_g=unpack(1, (DEPTH, d)), ln_mix_b=unpack(2, (DEPTH, d)),
        ln_mlp_g=unpack(3, (DEPTH, d)), ln_mlp_b=unpack(4, (DEPTH, d)),
        fox_b_f=unpack(5, (1, FOX_HEADS)), ssm_dt_bias=unpack(6, (1, SSM_HEADS)),
        ssm_a_log=unpack(7, (1, SSM_HEADS)), ssm_d=unpack(8, (1, SSM_HEADS)),
    )
    weights = dict(ada_w=ada_w, ada_b=ada_b, ln_mix_g=ln_mix_g, ln_mix_b=ln_mix_b, ln_mlp_g=ln_mlp_g, ln_mlp_b=ln_mlp_b,
                   mlp_w1=mlp_w1, mlp_w2=mlp_w2, fox_w_in=fox_w_in, fox_b_f=fox_b_f, fox_w_o=fox_w_o, ssm_w_in=ssm_w_in,
                   ssm_conv_w=ssm_conv_w, ssm_conv_b=ssm_conv_b, ssm_dt_bias=ssm_dt_bias, ssm_a_log=ssm_a_log,
                   ssm_d=ssm_d, ssm_norm_w=ssm_norm_w, ssm_w_out=ssm_w_out)
    mom1 = dict(ada_w=m_ada_w, ada_b=m_ada_b, ln_mix_g=m_ln_mix_g, ln_mix_b=m_ln_mix_b, ln_mlp_g=m_ln_mlp_g,
                ln_mlp_b=m_ln_mlp_b, mlp_w1=m_mlp_w1, mlp_w2=m_mlp_w2, fox_w_in=m_fox_w_in, fox_b_f=m_fox_b_f,
                fox_w_o=m_fox_w_o, ssm_w_in=m_ssm_w_in, ssm_conv_w=m_ssm_conv_w, ssm_conv_b=m_ssm_conv_b,
                ssm_dt_bias=m_ssm_dt_bias, ssm_a_log=m_ssm_a_log, ssm_d=m_ssm_d, ssm_norm_w=m_ssm_norm_w,
                ssm_w_out=m_ssm_w_out)
    mom2 = dict(ada_w=v_ada_w, ada_b=v_ada_b, ln_mix_g=v_ln_mix_g, ln_mix_b=v_ln_mix_b, ln_mlp_g=v_ln_mlp_g,
                ln_mlp_b=v_ln_mlp_b, mlp_w1=v_mlp_w1, mlp_w2=v_mlp_w2, fox_w_in=v_fox_w_in, fox_b_f=v_fox_b_f,
                fox_w_o=v_fox_w_o, ssm_w_in=v_ssm_w_in, ssm_conv_w=v_ssm_conv_w, ssm_conv_b=v_ssm_conv_b,
                ssm_dt_bias=v_ssm_dt_bias, ssm_a_log=v_ssm_a_log, ssm_d=v_ssm_d, ssm_norm_w=v_ssm_norm_w,
                ssm_w_out=v_ssm_w_out)
    names = list(weights)
    stepped = {n: _adamw_any(shards[n], weights[n], mom1[n], mom2[n], f"adamw_{n}") for n in names}
    return (loss, grad_x[None], *[stepped[n][0] for n in names], *[stepped[n][1] for n in names],
            *[stepped[n][2] for n in names], *[stepped[n][3] for n in names])
```

```python
import numpy as np
import jax
import jax.numpy as jnp
from jax import lax
from jax.experimental import pallas as pl
from jax.experimental.pallas import tpu as pltpu

F32 = jnp.float32
MXU_DTYPE = jnp.bfloat16

N_DEV = 8
D_MODEL = 1024
DEPTH = 2
FOX_HEADS = 16
FOX_HEAD_DIM = 64
D_FF = 4096
SSM_D_INNER = 2048
SSM_HEADS = 32
SSM_GROUPS = 8
SSM_STATE = 128
SSM_CHUNK = 128
SSM_CONV = 4
SSM_CONV_DIM = 4096
GROUP_W = SSM_D_INNER // SSM_GROUPS
LN_EPS = 1e-5
RMS_EPS = 1e-5
ALPHA = (2.0 * DEPTH) ** 0.25
LANES = 128
SUBLANES = 8

ADAM_LR = 0.001
ADAM_B1 = 0.9
ADAM_B2 = 0.999
ADAM_EPS = 1e-08
ADAM_WD = 0.01
ADAM_STEP = 10

NN = (((1,), (0,)), ((), ()))
NT = (((1,), (1,)), ((), ()))
TN = (((0,), (0,)), ((), ()))

VMEM_BIG = 56 * 1024 * 1024
MM_ROWS = 2048
MM_DEPTH = 4096


def _dot(a, b, dims=NN, precision=None):
    return lax.dot_general(a, b, dims, precision=precision, preferred_element_type=F32)


def _mx(v):
    return v.astype(MXU_DTYPE)


def _pieces3(v):
    hi = _mx(v)
    r1 = v - hi.astype(F32)
    mid = _mx(r1)
    return hi, mid, _mx(r1 - mid.astype(F32))


def _dot_onehot(a, b, dims=NN, onehot="b"):
    if onehot == "b":
        return sum(_dot(p, _mx(b), dims) for p in _pieces3(a))
    return sum(_dot(_mx(a), p, dims) for p in _pieces3(b))


def _params(vmem=None):
    return pltpu.CompilerParams(vmem_limit_bytes=vmem) if vmem else None


def _all_gather(x, name):
    def body(x_ref, out_ref, send_sems, recv_sems, local_sem):
        xi, yi, ci = lax.axis_index("x"), lax.axis_index("y"), lax.axis_index("c")
        me, sibling = (xi, yi, ci), (xi, yi, 1 - ci)
        chips = [(1 - xi, yi), (xi, 1 - yi), (1 - xi, 1 - yi)]

        def slot(px, py, pc):
            return out_ref.at[4 * px + 2 * py + pc]

        def copy(k, block, to, src=None):
            return pltpu.make_async_remote_copy(
                src_ref=slot(*block) if src is None else src, dst_ref=slot(*block),
                send_sem=send_sems.at[k], recv_sem=recv_sems.at[k],
                device_id=to, device_id_type=pl.DeviceIdType.MESH)

        mine = pltpu.make_async_copy(x_ref, slot(*me), local_sem)
        mine.start()
        first = [copy(0, me, sibling, src=x_ref)]
        first += [copy(1 + j, me, (*chip, ci), src=x_ref) for j, chip in enumerate(chips)]
        for cp in first:
            cp.start()
        passed = [copy(4 + j, (*chip, ci), sibling) for j, chip in enumerate(chips)]
        for j, chip in enumerate(chips):
            copy(1 + j, (*chip, ci), me).wait_recv()
            passed[j].start()
        copy(0, sibling, me).wait_recv()
        for j, chip in enumerate(chips):
            copy(4 + j, (*chip, 1 - ci), me).wait_recv()
        for cp in first + passed:
            cp.wait_send()
        mine.wait()

    return pl.pallas_call(
        body, name=name,
        out_shape=jax.ShapeDtypeStruct((N_DEV,) + x.shape, x.dtype),
        in_specs=[pl.BlockSpec(memory_space=pl.ANY)],
        out_specs=pl.BlockSpec(memory_space=pl.ANY),
        scratch_shapes=[pltpu.SemaphoreType.DMA((7,)), pltpu.SemaphoreType.DMA((7,)),
                        pltpu.SemaphoreType.DMA],
    )(x)


def _direct_copies(scatter, x_ref, out_ref, send_sems, recv_sems, local_sems, n):
    xi, yi, ci = lax.axis_index("x"), lax.axis_index("y"), lax.axis_index("c")
    me = 4 * xi + 2 * yi + ci
    local = pltpu.make_async_copy(x_ref.at[me] if scatter else x_ref, out_ref.at[me], local_sems.at[n])
    remote = []
    for k in range(1, N_DEV):
        px = 1 - xi if k & 4 else xi
        py = 1 - yi if k & 2 else yi
        pc = 1 - ci if k & 1 else ci
        remote.append(pltpu.make_async_remote_copy(
            src_ref=x_ref.at[4 * px + 2 * py + pc] if scatter else x_ref, dst_ref=out_ref.at[me],
            send_sem=send_sems.at[7 * n + k - 1], recv_sem=recv_sems.at[7 * n + k - 1],
            device_id=(px, py, pc), device_id_type=pl.DeviceIdType.MESH))
    return local, remote


def _exchange_shapes(carry):
    return [jax.ShapeDtypeStruct(a.shape if scatter else (N_DEV,) + a.shape, a.dtype) for scatter, a in carry]


def _exchange_sems(carry):
    n = len(carry)
    return [pltpu.SemaphoreType.DMA((7 * n,)), pltpu.SemaphoreType.DMA((7 * n,)), pltpu.SemaphoreType.DMA((n,))]


def _carrying(body, n_in, n_out, n_scratch, carry, grid):
    nc = len(carry)

    def copies(refs):
        srcs = refs[n_in:n_in + nc]
        dsts = refs[n_in + nc + n_out:n_in + 2 * nc + n_out]
        sems = refs[n_in + 2 * nc + n_out + n_scratch:]
        return [_direct_copies(scatter, srcs[n], dsts[n], *sems, n) for n, (scatter, _) in enumerate(carry)]

    def wrapped(*refs):
        step = 0
        for axis, extent in enumerate(grid):
            step = step * extent + pl.program_id(axis)

        @pl.when(step == 0)
        def _():
            for local, remote in copies(refs):
                local.start()
                for cp in remote:
                    cp.start()

        body(*refs[:n_in], *refs[n_in + nc:n_in + nc + n_out],
             *refs[n_in + 2 * nc + n_out:n_in + 2 * nc + n_out + n_scratch])

        @pl.when(step == int(np.prod(grid)) - 1)
        def _():
            for local, remote in copies(refs):
                for cp in remote:
                    cp.wait()
                local.wait()

    return wrapped if nc else body


def _mm(name, a, b, *, grid, a_spec, b_spec, dims, k_axis, outs, acc=None, extras=(), epi=None, vmem=None,
        carry=()):
    nk = grid[k_axis]
    n_ex, n_out = len(extras), len(outs)
    carry = list(carry)
    anywhere = pl.BlockSpec(memory_space=pl.ANY)

    def body(*refs):
        a_ref, b_ref = refs[0], refs[1]
        ex = refs[2:2 + n_ex]
        out = refs[2 + n_ex:2 + n_ex + n_out]

        def finish(val):
            if epi is None:
                out[0][...] = val.astype(out[0].dtype)
            else:
                epi(val, ex, out)

        part = _dot(a_ref[...], b_ref[...], dims)
        if nk == 1:
            finish(part)
        else:
            acc_ref = refs[2 + n_ex + n_out]
            k = pl.program_id(k_axis)

            @pl.when(k == 0)
            def _():
                acc_ref[...] = part

            @pl.when(k > 0)
            def _():
                acc_ref[...] += part

            @pl.when(k == nk - 1)
            def _():
                finish(acc_ref[...])

    res = pl.pallas_call(
        _carrying(body, 2 + n_ex, n_out, 1 if nk > 1 else 0, carry, grid), name=name, grid=grid,
        in_specs=[a_spec, b_spec] + [s for _, s in extras] + [anywhere] * len(carry),
        out_specs=[s for _, s in outs] + [anywhere] * len(carry),
        out_shape=[o for o, _ in outs] + _exchange_shapes(carry),
        scratch_shapes=([pltpu.VMEM(acc, F32)] if nk > 1 else []) + (_exchange_sems(carry) if carry else []),
        compiler_params=_params(vmem),
    )(a, b, *[e for e, _ in extras], *[arr for _, arr in carry])
    return list(res)


def _tile(n, t):
    t = min(n, t)
    assert n % t == 0, (n, t)
    return t


def _mm_nn(name, a, b, out_dtype, *, tm=MM_ROWS, tn=1024, tk=1024, epi=None, extras=(), outs=None, carry=()):
    m, kk = a.shape
    n = b.shape[1]
    tm, tn, tk = _tile(m, tm), _tile(n, tn), _tile(kk, tk)
    if outs is None:
        outs = [(jax.ShapeDtypeStruct((m, n), out_dtype), pl.BlockSpec((tm, tn), lambda i, j, k: (i, j)))]
    res = _mm(name, a, b, grid=(m // tm, n // tn, kk // tk),
              a_spec=pl.BlockSpec((tm, tk), lambda i, j, k: (i, k)),
              b_spec=pl.BlockSpec((tk, tn), lambda i, j, k: (k, j)),
              dims=NN, k_axis=2, acc=(tm, tn), outs=outs, extras=list(extras), epi=epi, vmem=VMEM_BIG, carry=carry)
    return res[0] if len(res) == 1 else res


def _mm_sum(name, pairs, out_dtype=F32, *, tm=1024, tn=1024, carry=()):
    m, n = pairs[0][0].shape[0], pairs[0][1].shape[1]
    tm, tn = _tile(m, tm), _tile(n, tn)
    carry = list(carry)
    grid = (m // tm, n // tn)

    def body(*refs):
        out = refs[2 * len(pairs)]
        acc = _dot(refs[0][...], refs[1][...])
        for p in range(1, len(pairs)):
            acc = acc + _dot(refs[2 * p][...], refs[2 * p + 1][...])
        out[...] = acc.astype(out.dtype)

    in_specs = []
    for a, b in pairs:
        in_specs += [pl.BlockSpec((tm, a.shape[1]), lambda i, j: (i, 0)),
                     pl.BlockSpec((b.shape[0], tn), lambda i, j: (0, j))]
    anywhere = pl.BlockSpec(memory_space=pl.ANY)
    res = pl.pallas_call(
        _carrying(body, 2 * len(pairs), 1, 0, carry, grid), name=name, grid=grid,
        in_specs=in_specs + [anywhere] * len(carry),
        out_specs=[pl.BlockSpec((tm, tn), lambda i, j: (i, j))] + [anywhere] * len(carry),
        out_shape=[jax.ShapeDtypeStruct((m, n), out_dtype)] + _exchange_shapes(carry),
        scratch_shapes=_exchange_sems(carry) if carry else [],
        compiler_params=_params(VMEM_BIG),
    )(*[x for pair in pairs for x in pair], *[arr for _, arr in carry])
    return list(res) if carry else res[0]


def _mm_tn(name, a, b, out_dtype=F32, *, tm=1024, tn=1024, tk=MM_DEPTH):
    kk, m = a.shape
    n = b.shape[1]
    tm, tn, tk = _tile(m, tm), _tile(n, tn), _tile(kk, tk)
    res = _mm(name, a, b, grid=(m // tm, n // tn, kk // tk),
              a_spec=pl.BlockSpec((tk, tm), lambda i, j, k: (k, i)),
              b_spec=pl.BlockSpec((tk, tn), lambda i, j, k: (k, j)),
              dims=TN, k_axis=2, acc=(tm, tn),
              outs=[(jax.ShapeDtypeStruct((m, n), out_dtype), pl.BlockSpec((tm, tn), lambda i, j, k: (i, j)))],
              vmem=VMEM_BIG)
    return res[0]


def _row_block(s):
    return _tile(s, 512)


def _modulate(x, pv, name):
    s, d = x.shape
    tb = _row_block(s)

    def body(x_ref, pv_ref, u_ref):
        u_ref[...] = _mx(x_ref[...] * pv_ref[0:1, :] + pv_ref[1:2, :])

    return pl.pallas_call(
        body, name=name, grid=(s // tb,),
        in_specs=[pl.BlockSpec((tb, d), lambda i: (i, 0)), pl.BlockSpec((8, d), lambda i: (0, 0))],
        out_specs=pl.BlockSpec((tb, d), lambda i: (i, 0)),
        out_shape=jax.ShapeDtypeStruct((s, d), MXU_DTYPE),
    )(x, pv)


def _ln_stats(r):
    mu = jnp.mean(r, axis=-1, keepdims=True)
    xc = r - mu
    var = jnp.mean(xc * xc, axis=-1, keepdims=True)
    rstd = lax.rsqrt(var + LN_EPS)
    return xc * rstd, rstd


def _ln_fwd(xin, y, pv, name):
    s, d = xin.shape
    tb = _row_block(s)

    def body(x_ref, y_ref, pv_ref, xo_ref, u_ref):
        r = ALPHA * x_ref[...] + pv_ref[0:1, :] * y_ref[...]
        xhat, _ = _ln_stats(r)
        xo = xhat * pv_ref[1:2, :] + pv_ref[2:3, :]
        xo_ref[...] = xo
        u_ref[...] = _mx(xo * pv_ref[3:4, :] + pv_ref[4:5, :])

    row = pl.BlockSpec((tb, d), lambda i: (i, 0))
    return pl.pallas_call(
        body, name=name, grid=(s // tb,),
        in_specs=[row, row, pl.BlockSpec((8, d), lambda i: (0, 0))],
        out_specs=[row, row],
        out_shape=[jax.ShapeDtypeStruct((s, d), F32), jax.ShapeDtypeStruct((s, d), MXU_DTYPE)],
    )(xin, y, pv)


def _ln_bwd(xin, y, pv, name, *, dxo=None, du=None, target=None):
    s, d = xin.shape
    tb = _row_block(s)
    nb = s // tb
    loss_mode = target is not None

    def body(*refs):
        if loss_mode:
            x_ref, y_ref, pv_ref, t_ref, dxin_ref, dy_ref, sums_ref = refs
        else:
            x_ref, y_ref, pv_ref, dxo_ref, du_ref, dxin_ref, dy_ref, sums_ref = refs
        i = pl.program_id(0)

        @pl.when(i == 0)
        def _():
            sums_ref[...] = jnp.zeros_like(sums_ref)

        yv = y_ref[...]
        r = ALPHA * x_ref[...] + pv_ref[0:1, :] * yv
        xhat, rstd = _ln_stats(r)
        xo = xhat * pv_ref[1:2, :] + pv_ref[2:3, :]
        if loss_mode:
            diff = xo - t_ref[...]
            dxo_v = diff * (1.0 / d)
            sums_ref[5:6, :] += jnp.sum(diff * diff, axis=0, keepdims=True) * (0.5 / d)
        else:
            duv = du_ref[...]
            dxo_v = dxo_ref[...] + duv * pv_ref[3:4, :]
            sums_ref[0:1, :] += jnp.sum(duv * xo, axis=0, keepdims=True)
            sums_ref[1:2, :] += jnp.sum(duv, axis=0, keepdims=True)
        sums_ref[2:3, :] += jnp.sum(dxo_v * xhat, axis=0, keepdims=True)
        sums_ref[3:4, :] += jnp.sum(dxo_v, axis=0, keepdims=True)
        dxh = dxo_v * pv_ref[1:2, :]
        dr = rstd * (dxh - jnp.mean(dxh, axis=-1, keepdims=True)
                     - xhat * jnp.mean(dxh * xhat, axis=-1, keepdims=True))
        sums_ref[4:5, :] += jnp.sum(dr * yv, axis=0, keepdims=True)
        dxin_ref[...] = ALPHA * dr
        dy_ref[...] = _mx(pv_ref[0:1, :] * dr)
        if loss_mode:
            @pl.when(i == nb - 1)
            def _():
                sums_ref[5:6, :] = jnp.broadcast_to(jnp.sum(sums_ref[5:6, :], axis=-1, keepdims=True), (1, d))

    row = pl.BlockSpec((tb, d), lambda i: (i, 0))
    par = pl.BlockSpec((8, d), lambda i: (0, 0))
    ins = [xin, y, pv] + ([target] if loss_mode else [dxo, du])
    return pl.pallas_call(
        body, name=name, grid=(nb,),
        in_specs=[row, row, par] + [row] * (len(ins) - 3),
        out_specs=[row, row, par],
        out_shape=[jax.ShapeDtypeStruct((s, d), F32), jax.ShapeDtypeStruct((s, d), MXU_DTYPE),
                   jax.ShapeDtypeStruct((8, d), F32)],
    )(*ins)


def _mod_bwd(dx_direct, du, x, pv, name):
    s, d = x.shape
    tb = _row_block(s)

    def body(dxd_ref, du_ref, x_ref, pv_ref, dx_ref, sums_ref):
        @pl.when(pl.program_id(0) == 0)
        def _():
            sums_ref[...] = jnp.zeros_like(sums_ref)

        duv = du_ref[...]
        dx_ref[...] = dxd_ref[...] + duv * pv_ref[0:1, :]
        sums_ref[0:1, :] += jnp.sum(duv * x_ref[...], axis=0, keepdims=True)
        sums_ref[1:2, :] += jnp.sum(duv, axis=0, keepdims=True)

    row = pl.BlockSpec((tb, d), lambda i: (i, 0))
    par = pl.BlockSpec((8, d), lambda i: (0, 0))
    return pl.pallas_call(
        body, name=name, grid=(s // tb,),
        in_specs=[row, row, row, par], out_specs=[row, par],
        out_shape=[jax.ShapeDtypeStruct((s, d), F32), jax.ShapeDtypeStruct((8, d), F32)],
    )(dx_direct, du, x, pv)


def _mlp_fwd(u, w1, w2, tag, carry=()):
    s = u.shape[0]

    def epi(val, ex, out):
        out[0][...] = _mx(val)
        out[1][...] = _mx(jnp.square(jnp.maximum(val, 0.0)))

    tm, tn = _tile(s, MM_ROWS), 1024
    spec = pl.BlockSpec((tm, tn), lambda i, j, k: (i, j))
    shp = jax.ShapeDtypeStruct((s, D_FF), MXU_DTYPE)
    h, a, *carried = _mm_nn(f"mlp_up{tag}", u, w1, None, epi=epi, outs=[(shp, spec), (shp, spec)], tn=tn,
                            carry=carry)
    y = _mm_nn(f"mlp_down{tag}", a, w2, F32)
    return y, (h, a), carried


def _mlp_bwd(dy, u, h, a, w1t, w2t, tag, carry=(), carry_du=()):
    s = u.shape[0]
    tm, tn = _tile(s, MM_ROWS), 1024
    spec = pl.BlockSpec((tm, tn), lambda i, j, k: (i, j))

    def epi(val, ex, out):
        out[0][...] = _mx(val * (2.0 * jnp.maximum(ex[0][...].astype(F32), 0.0)))

    got = _mm_nn(f"mlp_dh{tag}", dy, w2t, None, epi=epi, extras=[(h, spec)],
                 outs=[(jax.ShapeDtypeStruct((s, D_FF), MXU_DTYPE), spec)], tn=tn, carry=carry)
    dh, carried = (got[0], list(got[1:])) if carry else (got, [])
    got = _mm_nn(f"mlp_du{tag}", dh, w1t, F32, carry=carry_du)
    du, carried = (got[0], carried + list(got[1:])) if carry_du else (got, carried)
    dw2 = _mm_tn(f"mlp_dw2{tag}", a, dy)
    dw1 = _mm_tn(f"mlp_dw1{tag}", u, dh)
    return du, dw1, dw2, carried


FOX_T = 1024
BIAS_Q = (64, 65, 66)
BIAS_K = (67, 68, 69)
SKIP_MARGIN = 110.0
ONES_V = 64

def _fox_constants():
    selq = np.zeros((FOX_HEADS, 512, LANES), np.float32)
    selk = np.zeros((FOX_HEADS, 512, LANES), np.float32)
    selv = np.zeros((2, LANES, LANES), np.float32)
    put = np.zeros((2, 2, LANES, LANES), np.float32)
    for h in range(FOX_HEADS):
        off = FOX_HEAD_DIM * (h % 2)
        for dd in range(FOX_HEAD_DIM):
            selq[h, off + dd, dd] = FOX_HEAD_DIM ** -0.5
            selk[h, off + dd, dd] = 1.0
        for piece in range(3):
            selq[h, LANES * (1 + piece) + h, BIAS_Q[piece]] = 1.0
            selk[h, LANES * (1 + piece) + h, BIAS_K[piece]] = -1.0
    for par in range(2):
        for dd in range(FOX_HEAD_DIM):
            selv[par, FOX_HEAD_DIM * par + dd, dd] = 1.0
            put[par, 0, dd, FOX_HEAD_DIM * par + dd] = FOX_HEAD_DIM ** -0.5
            put[par, 1, dd, FOX_HEAD_DIM * par + dd] = 1.0
    return selq, selk, selv, put


def _fox_prep(qkv, f, bf, carry=()):
    s = qkv.shape[0]
    t = _tile(s, FOX_T)
    nb = s // t
    selq, selk, selv, _ = _fox_constants()
    carry = list(carry)
    anywhere = pl.BlockSpec(memory_space=pl.ANY)

    def body(q_ref, k_ref, v_ref, f_ref, bf_ref, selq_ref, selk_ref, selv_ref,
             qa_ref, qat_ref, ka_ref, kat_ref, va_ref, vat_ref, stats_ref, parts_ref, carry_ref, cum_ref):
        i, h = pl.program_id(0), pl.program_id(1)
        lane = lax.broadcasted_iota(jnp.int32, (1, LANES), 1)

        @pl.when(h == 0)
        def _():
            @pl.when(i == 0)
            def _():
                carry_ref[...] = jnp.zeros_like(carry_ref)

            lf = jnp.where(lane < FOX_HEADS, jax.nn.log_sigmoid(f_ref[...] + bf_ref[0:1, :]), 0.0)
            tri = (lax.broadcasted_iota(jnp.int32, (t, t), 0) >= lax.broadcasted_iota(jnp.int32, (t, t), 1)).astype(F32)
            cum = _dot_onehot(tri, lf, onehot="a") + carry_ref[0:1, :]
            carry_ref[0:1, :] = cum[t - 1:t, :]
            cum_ref[...] = cum
            hi = _mx(cum)
            r1 = cum - hi.astype(F32)
            mid = _mx(r1)
            parts_ref[:, 0:LANES] = hi
            parts_ref[:, LANES:2 * LANES] = mid
            parts_ref[:, 2 * LANES:3 * LANES] = _mx(r1 - mid.astype(F32))

        parts = parts_ref[...]
        qa = _dot(jnp.concatenate([q_ref[...], parts], axis=1), selq_ref[...])
        qa = qa + jnp.where((lane >= BIAS_K[0]) & (lane <= BIAS_K[2]), 1.0, 0.0)
        ka = _dot(jnp.concatenate([k_ref[...], parts], axis=1), selk_ref[...])
        ka = ka + jnp.where((lane >= BIAS_Q[0]) & (lane <= BIAS_Q[2]), 1.0, 0.0)
        va = _dot(v_ref[...], selv_ref[...]) + jnp.where(lane == ONES_V, 1.0, 0.0)
        qa_ref[...] = _mx(qa)
        qat_ref[...] = _mx(qa.T)
        ka_ref[...] = _mx(ka)
        kat_ref[...] = _mx(ka.T)
        va_ref[...] = _mx(va)
        vat_ref[...] = _mx(va.T)

        def longest(rows_):
            sq = jnp.where(lane < FOX_HEAD_DIM, rows_ * rows_, 0.0)
            return jnp.sqrt(jnp.max(jnp.sum(sq, axis=1, keepdims=True), axis=0, keepdims=True))

        mine = lane == h
        cum = cum_ref[...]
        top = jnp.max(jnp.max(jnp.where(mine, cum, -jnp.inf), axis=1, keepdims=True), axis=0, keepdims=True)
        low = jnp.min(jnp.min(jnp.where(mine, cum, jnp.inf), axis=1, keepdims=True), axis=0, keepdims=True)
        row = lax.broadcasted_iota(jnp.int32, (8, LANES), 0)
        stats_ref[...] = jnp.where(row == 0, longest(qa), jnp.where(row == 1, longest(ka),
                                                                    jnp.where(row == 2, top, low)))

    rows = jax.ShapeDtypeStruct((FOX_HEADS, nb, t, LANES), MXU_DTYPE)
    cols = jax.ShapeDtypeStruct((FOX_HEADS, nb, LANES, t), MXU_DTYPE)
    rspec = pl.BlockSpec((None, None, t, LANES), lambda i, h: (h, i, 0, 0))
    cspec = pl.BlockSpec((None, None, LANES, t), lambda i, h: (h, i, 0, 0))
    npair = FOX_HEADS // 2
    res = pl.pallas_call(
        _carrying(body, 8, 7, 3, carry, (nb, FOX_HEADS)), name="fox_prep", grid=(nb, FOX_HEADS),
        in_specs=[pl.BlockSpec((t, LANES), lambda i, h: (i, h // 2)),
                  pl.BlockSpec((t, LANES), lambda i, h: (i, npair + h // 2)),
                  pl.BlockSpec((t, LANES), lambda i, h: (i, 2 * npair + h // 2)),
                  pl.BlockSpec((t, LANES), lambda i, h: (i, 0)),
                  pl.BlockSpec((8, LANES), lambda i, h: (0, 0)),
                  pl.BlockSpec((None, 512, LANES), lambda i, h: (h, 0, 0)),
                  pl.BlockSpec((None, 512, LANES), lambda i, h: (h, 0, 0)),
                  pl.BlockSpec((None, LANES, LANES), lambda i, h: (h % 2, 0, 0))] + [anywhere] * len(carry),
        out_specs=[rspec, cspec, rspec, cspec, rspec, cspec,
                   pl.BlockSpec((None, None, 8, LANES), lambda i, h: (h, i, 0, 0))] + [anywhere] * len(carry),
        out_shape=[rows, cols, rows, cols, rows, cols, jax.ShapeDtypeStruct((FOX_HEADS, nb, 8, LANES), F32)]
        + _exchange_shapes(carry),
        scratch_shapes=[pltpu.VMEM((t, 3 * LANES), MXU_DTYPE), pltpu.VMEM((8, LANES), F32),
                        pltpu.VMEM((t, LANES), F32)] + (_exchange_sems(carry) if carry else []),
        compiler_params=_params(VMEM_BIG),
    )(qkv, qkv, qkv, f, bf, _mx(jnp.asarray(selq)), _mx(jnp.asarray(selk)), _mx(jnp.asarray(selv)),
      *[arr for _, arr in carry])
    return (*res[:7], list(res[7:]))


def _fox_active(stats):
    qn, kn, top, low = (stats[:, :, r, 0] for r in range(4))
    nb = qn.shape[1]
    gap = qn[:, :, None] * kn[:, None, :] + top[:, :, None] - low[:, None, :] + (qn * kn)[:, :, None]
    keep = (gap > -SKIP_MARGIN) | jnp.eye(nb, dtype=bool)[None]
    return jnp.where(keep, 1.0, 0.0).astype(F32).reshape(qn.shape[0], nb * nb)


def _causal_allow(t):
    return lax.broadcasted_iota(jnp.int32, (t, t), 0) <= lax.broadcasted_iota(jnp.int32, (t, t), 1)


def _fox_attn_fwd(qat, ka, vat, active):
    heads, nb, _, t = qat.shape

    def body(act_ref, qat_ref, ka_ref, vat_ref, ot_ref, lse_ref, acc_ref, m_ref, kbuf_ref, vbuf_ref, fsems):
        h, i = pl.program_id(0), pl.program_id(1)
        m_ref[...] = jnp.full_like(m_ref, -jnp.inf)
        acc_ref[...] = jnp.zeros_like(acc_ref)

        def key_blocks(j, slot, head=None):
            hh = h if head is None else head
            return [pltpu.make_async_copy(ka_ref.at[hh, j], kbuf_ref.at[slot], fsems.at[0, slot]),
                    pltpu.make_async_copy(vat_ref.at[hh, j], vbuf_ref.at[slot], fsems.at[1, slot])]

        def runs(j):
            return act_ref[h, i * nb + j] > 0.5

        def step(slot, diagonal):
            st = _dot(kbuf_ref[slot], qat_ref[...])
            if diagonal:
                st = jnp.where(_causal_allow(t), st, -jnp.inf)
            m_old = m_ref[...]
            m_new = jnp.maximum(m_old, jnp.max(st, axis=0, keepdims=True))
            pt = jnp.exp(st - m_new)
            acc_ref[...] = acc_ref[...] * jnp.exp(m_old - m_new) + _dot(vbuf_ref[slot], _mx(pt))
            m_ref[...] = m_new

        first = (h == 0) & (i == 0)
        last = (h == heads - 1) & (i == nb - 1)

        @pl.when(first)
        def _():
            for cp in key_blocks(i, 2):
                cp.start()

        @pl.when((i > 0) & runs(0))
        def _():
            for cp in key_blocks(0, 0):
                cp.start()

        for cp in key_blocks(i, 2):
            cp.wait()
        step(2, True)

        @pl.when(jnp.logical_not(last))
        def _():
            wrap = i == nb - 1
            for cp in key_blocks(jnp.where(wrap, 0, i + 1), 2, jnp.where(wrap, h + 1, h)):
                cp.start()

        def earlier(j, c):
            slot = j % 2
            nxt = jnp.minimum(j + 1, nb - 1)

            @pl.when((j + 1 < i) & runs(nxt))
            def _():
                for cp in key_blocks(nxt, 1 - slot):
                    cp.start()

            @pl.when(runs(j))
            def _():
                for cp in key_blocks(j, slot):
                    cp.wait()
                step(slot, False)

            return c

        lax.fori_loop(0, i, earlier, 0)
        acc = acc_ref[...]
        denom = acc[ONES_V:ONES_V + 1, :]
        ot_ref[...] = _mx(acc / denom)
        lse_ref[...] = m_ref[...] + jnp.log(denom)

    anywhere = pl.BlockSpec(memory_space=pl.ANY)
    qspec = pl.BlockSpec((None, None, LANES, t), lambda h, i: (h, i, 0, 0))
    return pl.pallas_call(
        body, name="fox_attn_fwd", grid=(heads, nb),
        in_specs=[pl.BlockSpec(memory_space=pltpu.SMEM), qspec, anywhere, anywhere],
        out_specs=[qspec, pl.BlockSpec((None, None, 1, t), lambda h, i: (h, i, 0, 0))],
        out_shape=[jax.ShapeDtypeStruct((heads, nb, LANES, t), MXU_DTYPE),
                   jax.ShapeDtypeStruct((heads, nb, 1, t), F32)],
        scratch_shapes=[pltpu.VMEM((LANES, t), F32), pltpu.VMEM((1, t), F32), pltpu.VMEM((3, t, LANES), MXU_DTYPE),
                        pltpu.VMEM((3, LANES, t), MXU_DTYPE), pltpu.SemaphoreType.DMA((2, 3))],
        compiler_params=_params(VMEM_BIG),
    )(active, qat, ka, vat)


def _fox_attn_bwd(qa, qat, ka, kat, va, ot, lse, do, dot_, active):
    heads, nb, t, _ = qa.shape

    def body(act_ref, qa_ref, qat_ref, ka_ref, kat_ref, va_ref, ot_ref, lse_ref, do_ref, dot_ref,
             dqt_ref, dka_ref, dva_ref, rows_ref, cols_ref, lseb_ref, fsems):
        h, j = pl.program_id(0), pl.program_id(1)

        @pl.when(j == 0)
        def _():
            dqt_ref[...] = jnp.zeros_like(dqt_ref)

        def query_blocks(i, slot, head=None):
            hh = h if head is None else head
            cps = [pltpu.make_async_copy(src.at[hh, i], rows_ref.at[slot, n], fsems.at[n, slot])
                   for n, src in enumerate((qa_ref, do_ref))]
            cps += [pltpu.make_async_copy(src.at[hh, i], cols_ref.at[slot, n], fsems.at[2 + n, slot])
                    for n, src in enumerate((qat_ref, ot_ref, dot_ref))]
            return cps + [pltpu.make_async_copy(lse_ref.at[hh, i], lseb_ref.at[slot], fsems.at[5, slot])]

        def runs(i):
            return (i == j) | (act_ref[h, i * nb + j] > 0.5)

        def step(i, slot, diagonal):
            st = _dot(ka_ref[...], cols_ref[slot, 0])
            dot_v = cols_ref[slot, 2]
            delta = jnp.sum(cols_ref[slot, 1].astype(F32) * dot_v.astype(F32), axis=0, keepdims=True)
            pt = jnp.exp(st - lseb_ref[slot])
            if diagonal:
                pt = jnp.where(_causal_allow(t), pt, 0.0)
            dsm = _mx(pt * (_dot(va_ref[...], dot_v) - delta))
            upd_v = _dot(_mx(pt), rows_ref[slot, 1])
            upd_k = _dot(dsm, rows_ref[slot, 0])
            if diagonal:
                dva_ref[...] = upd_v
                dka_ref[...] = upd_k
            else:
                dva_ref[...] += upd_v
                dka_ref[...] += upd_k
            dqt_ref[i] += _dot(kat_ref[...], dsm)

        def visit(i, slot, after, diagonal):
            nxt = jnp.minimum(i + 1, nb - 1)

            @pl.when((i + 1 < nb) & runs(nxt))
            def _():
                for cp in query_blocks(nxt, after):
                    cp.start()

            @pl.when(runs(i))
            def _():
                for cp in query_blocks(i, slot):
                    cp.wait()
                step(i, slot, diagonal)

        first = (h == 0) & (j == 0)
        last = (h == heads - 1) & (j == nb - 1)

        @pl.when(first)
        def _():
            for cp in query_blocks(j, 2):
                cp.start()

        visit(j, 2, 0, True)

        @pl.when(jnp.logical_not(last))
        def _():
            wrap = j == nb - 1
            for cp in query_blocks(jnp.where(wrap, 0, j + 1), 2, jnp.where(wrap, h + 1, h)):
                cp.start()

        def later(i, c):
            slot = (i - j - 1) % 2
            visit(i, slot, 1 - slot, False)
            return c

        lax.fori_loop(j + 1, nb, later, 0)

    def at_k(shape):
        return pl.BlockSpec((None, None) + shape, lambda h, j: (h, j, 0, 0))

    anywhere = pl.BlockSpec(memory_space=pl.ANY)
    return pl.pallas_call(
        body, name="fox_attn_bwd", grid=(heads, nb),
        in_specs=[pl.BlockSpec(memory_space=pltpu.SMEM),
                  anywhere, anywhere, at_k((t, LANES)), at_k((LANES, t)), at_k((t, LANES)),
                  anywhere, anywhere, anywhere, anywhere],
        out_specs=[pl.BlockSpec((None, nb, LANES, t), lambda h, j: (h, 0, 0, 0)), at_k((t, LANES)), at_k((t, LANES))],
        out_shape=[jax.ShapeDtypeStruct((heads, nb, LANES, t), F32),
                   jax.ShapeDtypeStruct((heads, nb, t, LANES), F32),
                   jax.ShapeDtypeStruct((heads, nb, t, LANES), F32)],
        scratch_shapes=[pltpu.VMEM((3, 2, t, LANES), MXU_DTYPE), pltpu.VMEM((3, 3, LANES, t), MXU_DTYPE),
                        pltpu.VMEM((3, 1, t), F32), pltpu.SemaphoreType.DMA((6, 3))],
        compiler_params=_params(VMEM_BIG),
    )(active, qa, qat, ka, kat, va, ot, lse, do, dot_)


def _fox_post(dqt, dka, dva, f, bf, carry=()):
    heads, nb, t, _ = dka.shape
    s = nb * t
    _, _, _, put = _fox_constants()
    carry = list(carry)
    anywhere = pl.BlockSpec(memory_space=pl.ANY)

    def body(dqt_ref, dka_ref, dva_ref, f_ref, bf_ref, put_ref, dq_ref, dk_ref, dv_ref, df_ref, sums_ref,
             dc_ref, carry_ref):
        i, h = pl.program_id(0), pl.program_id(1)

        @pl.when((i == 0) & (h == 0))
        def _():
            carry_ref[...] = jnp.zeros_like(carry_ref)
            sums_ref[...] = jnp.zeros_like(sums_ref)

        @pl.when(h == 0)
        def _():
            dc_ref[...] = jnp.zeros_like(dc_ref)

        dqt_v = dqt_ref[...]
        dka_v = dka_ref[...]
        term_q = _dot(_mx(dqt_v), put_ref[0], TN)
        term_k = _dot(_mx(dka_v), put_ref[1])
        term_v = _dot(_mx(dva_ref[...]), put_ref[1])

        @pl.when(h % 2 == 0)
        def _():
            dq_ref[...] = _mx(term_q)
            dk_ref[...] = _mx(term_k)
            dv_ref[...] = _mx(term_v)

        @pl.when(h % 2 == 1)
        def _():
            dq_ref[...] += _mx(term_q)
            dk_ref[...] += _mx(term_k)
            dv_ref[...] += _mx(term_v)

        dcum = dqt_v[BIAS_Q[0]:BIAS_Q[0] + 1, :] - dka_v.T[BIAS_K[0]:BIAS_K[0] + 1, :]
        head_row = lax.broadcasted_iota(jnp.int32, (heads, 1), 0) == h
        dc_ref[...] += jnp.where(head_row, dcum, 0.0)

        @pl.when(h == heads - 1)
        def _():
            later = (lax.broadcasted_iota(jnp.int32, (t, t), 0) >= lax.broadcasted_iota(jnp.int32, (t, t), 1)).astype(F32)
            dlf_t = _dot_onehot(dc_ref[...], later) + carry_ref[:, 0:1]
            carry_ref[...] = jnp.broadcast_to(dlf_t[:, 0:1], carry_ref.shape)
            dlf = jnp.concatenate([dlf_t, jnp.zeros((LANES - heads, t), F32)], axis=0).T
            lane = lax.broadcasted_iota(jnp.int32, (1, LANES), 1)
            df = jnp.where(lane < heads, dlf * jax.nn.sigmoid(-(f_ref[...] + bf_ref[0:1, :])), 0.0)
            df_ref[...] = _mx(df)
            sums_ref[0:1, :] += jnp.sum(df, axis=0, keepdims=True)

    rev = lambda i: nb - 1 - i
    pair_spec = pl.BlockSpec((t, LANES), lambda i, h: (rev(i), h // 2))
    blk = pl.BlockSpec((t, LANES), lambda i, h: (rev(i), 0))
    hd = jax.ShapeDtypeStruct((s, D_MODEL), MXU_DTYPE)
    res = pl.pallas_call(
        _carrying(body, 6, 5, 2, carry, (nb, heads)), name="fox_post", grid=(nb, heads),
        in_specs=[pl.BlockSpec((None, None, LANES, t), lambda i, h: (h, rev(i), 0, 0)),
                  pl.BlockSpec((None, None, t, LANES), lambda i, h: (h, rev(i), 0, 0)),
                  pl.BlockSpec((None, None, t, LANES), lambda i, h: (h, rev(i), 0, 0)),
                  blk, pl.BlockSpec((8, LANES), lambda i, h: (0, 0)),
                  pl.BlockSpec((None, 2, LANES, LANES), lambda i, h: (h % 2, 0, 0, 0))] + [anywhere] * len(carry),
        out_specs=[pair_spec, pair_spec, pair_spec, blk, pl.BlockSpec((8, LANES), lambda i, h: (0, 0))]
        + [anywhere] * len(carry),
        out_shape=[hd, hd, hd, jax.ShapeDtypeStruct((s, LANES), MXU_DTYPE), jax.ShapeDtypeStruct((8, LANES), F32)]
        + _exchange_shapes(carry),
        scratch_shapes=[pltpu.VMEM((heads, t), F32), pltpu.VMEM((heads, LANES), F32)]
        + (_exchange_sems(carry) if carry else []),
        compiler_params=_params(VMEM_BIG),
    )(dqt, dka, dva, f, bf, _mx(jnp.asarray(put)), *[arr for _, arr in carry])
    return (*res[:5], list(res[5:]))


def _fox_weights(w_in, w_o):
    wqkv = w_in[:, :3 * D_MODEL]
    wf = jnp.pad(w_in[:, 3 * D_MODEL:], ((0, 0), (0, LANES - FOX_HEADS)))
    wo_heads = w_o.reshape(FOX_HEADS, FOX_HEAD_DIM, D_MODEL)
    wo_a = jnp.pad(wo_heads, ((0, 0), (0, LANES - FOX_HEAD_DIM), (0, 0)))
    wo_rows = wo_a.reshape(FOX_HEADS * LANES, D_MODEL)
    return dict(wqkv=wqkv, wf=wf, wqkv_t=wqkv.T, wf_t=wf.T, wo_rows=wo_rows, wo_rows_t=wo_rows.T)


def _fox_out(ot, wo_rows):
    heads, nb, _, t = ot.shape

    def body(ot_ref, w_ref, y_ref):
        y_ref[...] = _dot(ot_ref[...].reshape(heads * LANES, t), w_ref[...], TN)

    return pl.pallas_call(
        body, name="fox_out", grid=(nb,),
        in_specs=[pl.BlockSpec((heads, None, LANES, t), lambda i: (0, i, 0, 0)),
                  pl.BlockSpec((heads * LANES, D_MODEL), lambda i: (0, 0))],
        out_specs=pl.BlockSpec((t, D_MODEL), lambda i: (i, 0)),
        out_shape=jax.ShapeDtypeStruct((nb * t, D_MODEL), F32),
        compiler_params=_params(VMEM_BIG),
    )(ot, wo_rows)


def _fox_do(dy, wo_rows_t, nb, t):
    heads = FOX_HEADS

    def body(dy_ref, w_ref, do_ref, dot_ref):
        val = _dot(dy_ref[...], w_ref[...])
        for h in range(heads):
            blk = val[:, h * LANES:(h + 1) * LANES]
            do_ref[h] = _mx(blk)
            dot_ref[h] = _mx(blk.T)

    return pl.pallas_call(
        body, name="fox_do", grid=(nb,),
        in_specs=[pl.BlockSpec((t, D_MODEL), lambda i: (i, 0)),
                  pl.BlockSpec((D_MODEL, heads * LANES), lambda i: (0, 0))],
        out_specs=[pl.BlockSpec((heads, None, t, LANES), lambda i: (0, i, 0, 0)),
                   pl.BlockSpec((heads, None, LANES, t), lambda i: (0, i, 0, 0))],
        out_shape=[jax.ShapeDtypeStruct((heads, nb, t, LANES), MXU_DTYPE),
                   jax.ShapeDtypeStruct((heads, nb, LANES, t), MXU_DTYPE)],
        compiler_params=_params(VMEM_BIG),
    )(dy, wo_rows_t)


def _fox_dwo(ot, dy):
    heads, nb, _, t = ot.shape

    def body(ot_ref, dy_ref, o_ref):
        part = _dot(ot_ref[...].reshape(heads * LANES, t), dy_ref[...])

        @pl.when(pl.program_id(0) == 0)
        def _():
            o_ref[...] = part

        @pl.when(pl.program_id(0) > 0)
        def _():
            o_ref[...] += part

    return pl.pallas_call(
        body, name="fox_dwo", grid=(nb,),
        in_specs=[pl.BlockSpec((heads, None, LANES, t), lambda i: (0, i, 0, 0)),
                  pl.BlockSpec((t, D_MODEL), lambda i: (i, 0))],
        out_specs=pl.BlockSpec((heads * LANES, D_MODEL), lambda i: (0, 0)),
        out_shape=jax.ShapeDtypeStruct((heads * LANES, D_MODEL), F32),
        compiler_params=_params(VMEM_BIG),
    )(ot, dy)


def _fox_fwd(u, w, bf, carry=()):
    qkv = _mm_nn("fox_qkv", u, w["wqkv"], MXU_DTYPE)
    f = _mm_nn("fox_f", u, w["wf"], F32)
    qa, qat, ka, kat, va, vat, stats, carried = _fox_prep(qkv, f, bf, carry)
    ot, lse = _fox_attn_fwd(qat, ka, vat, _fox_active(stats))
    y = _fox_out(ot, w["wo_rows"])
    return y, dict(f=f, qa=qa, qat=qat, ka=ka, kat=kat, va=va, ot=ot, lse=lse, stats=stats), carried


def _fox_bwd(dy, u, w, bf, res, carry=()):
    heads, nb, t, _ = res["qa"].shape
    do, dot_ = _fox_do(dy, w["wo_rows_t"], nb, t)
    dwo_a = _fox_dwo(res["ot"], dy).reshape(heads, LANES, D_MODEL)
    dqt, dka, dva = _fox_attn_bwd(res["qa"], res["qat"], res["ka"], res["kat"], res["va"], res["ot"],
                                  res["lse"], do, dot_, _fox_active(res["stats"]))
    dq, dk, dv, df, sums, carried = _fox_post(dqt, dka, dva, res["f"], bf, carry)
    dw_in = jnp.concatenate(
        [_mm_tn("fox_dw_q", u, dq), _mm_tn("fox_dw_k", u, dk), _mm_tn("fox_dw_v", u, dv),
         _mm_tn("fox_dw_f", u, df)[:, :FOX_HEADS]], axis=1)
    dw_o = dwo_a[:, :FOX_HEAD_DIM, :].reshape(D_MODEL, D_MODEL)
    wt = w["wqkv_t"]
    du, ex_in, ex_o = _mm_sum(
        "fox_du", [(dq, wt[:D_MODEL]), (dk, wt[D_MODEL:2 * D_MODEL]), (dv, wt[2 * D_MODEL:]), (df, w["wf_t"])],
        carry=[(True, _col_slots(dw_in[None])), (True, _row_slots(dw_o[None]))])
    return du, sums, carried + [ex_in, ex_o]


def _dsilu(v):
    sg = jax.nn.sigmoid(v)
    return sg * (1.0 + v * (1.0 - sg))


def _conv_taps(scr_ref, w_ref, rows, base):
    acc = None
    for k in range(SSM_CONV):
        term = scr_ref[pl.ds(base - (SSM_CONV - 1) + k, rows), :] * w_ref[k:k + 1, :]
        acc = term if acc is None else acc + term
    return acc


def _conv_fwd(zx, cw, cb):
    s = zx.shape[0]
    tb = _tile(s, 512)
    half = SSM_CONV_DIM // 2
    hb = tb // SUBLANES

    def body(x_ref, halo_ref, w_ref, b_ref, o_ref, scr_ref):
        i = pl.program_id(0)
        scr_ref[pl.ds(0, SUBLANES), :] = jnp.where(i > 0, halo_ref[...], 0.0)
        scr_ref[pl.ds(SUBLANES, tb), :] = x_ref[...]
        o_ref[...] = jax.nn.silu(_conv_taps(scr_ref, w_ref, tb, SUBLANES) + b_ref[0:1, :])

    return pl.pallas_call(
        body, name="ssd_conv_fwd", grid=(s // tb, 2),
        in_specs=[pl.BlockSpec((tb, half), lambda i, j: (i, 1 + j)),
                  pl.BlockSpec((SUBLANES, half), lambda i, j: (jnp.maximum(i * hb - 1, 0), 1 + j)),
                  pl.BlockSpec((8, half), lambda i, j: (0, j)),
                  pl.BlockSpec((8, half), lambda i, j: (0, j))],
        out_specs=pl.BlockSpec((tb, half), lambda i, j: (i, j)),
        out_shape=jax.ShapeDtypeStruct((s, SSM_CONV_DIM), F32),
        scratch_shapes=[pltpu.VMEM((tb + SUBLANES, half), F32)],
    )(zx, zx, cw, cb)


def _conv_bwd_pre(zx, dxc, cw, cb):
    s = zx.shape[0]
    tb = _tile(s, 512)
    half = SSM_CONV_DIM // 2
    hb = tb // SUBLANES

    def body(x_ref, halo_ref, d_ref, w_ref, b_ref, o_ref, sums_ref, scr_ref):
        i = pl.program_id(1)

        @pl.when(i == 0)
        def _():
            sums_ref[...] = jnp.zeros_like(sums_ref)

        scr_ref[pl.ds(0, SUBLANES), :] = jnp.where(i > 0, halo_ref[...], 0.0)
        scr_ref[pl.ds(SUBLANES, tb), :] = x_ref[...]
        pre = _conv_taps(scr_ref, w_ref, tb, SUBLANES) + b_ref[0:1, :]
        dpre = d_ref[...] * _dsilu(pre)
        o_ref[...] = dpre
        for k in range(SSM_CONV):
            shifted = scr_ref[pl.ds(SUBLANES - (SSM_CONV - 1) + k, tb), :]
            sums_ref[k:k + 1, :] += jnp.sum(dpre * shifted, axis=0, keepdims=True)
        sums_ref[SSM_CONV:SSM_CONV + 1, :] += jnp.sum(dpre, axis=0, keepdims=True)

    return pl.pallas_call(
        body, name="ssd_conv_bwd_pre", grid=(2, s // tb),
        in_specs=[pl.BlockSpec((tb, half), lambda j, i: (i, 1 + j)),
                  pl.BlockSpec((SUBLANES, half), lambda j, i: (jnp.maximum(i * hb - 1, 0), 1 + j)),
                  pl.BlockSpec((tb, half), lambda j, i: (i, j)),
                  pl.BlockSpec((8, half), lambda j, i: (0, j)),
                  pl.BlockSpec((8, half), lambda j, i: (0, j))],
        out_specs=[pl.BlockSpec((tb, half), lambda j, i: (i, j)),
                   pl.BlockSpec((8, half), lambda j, i: (0, j))],
        out_shape=[jax.ShapeDtypeStruct((s, SSM_CONV_DIM), F32), jax.ShapeDtypeStruct((8, SSM_CONV_DIM), F32)],
        scratch_shapes=[pltpu.VMEM((tb + SUBLANES, half), F32)],
    )(zx, zx, dxc, cw, cb)


def _conv_bwd_x(dpre, cw):
    s = dpre.shape[0]
    tb = _tile(s, 512)
    hb = tb // SUBLANES
    nb = s // tb

    def body(d_ref, halo_ref, w_ref, o_ref, scr_ref):
        i = pl.program_id(0)
        scr_ref[pl.ds(0, tb), :] = d_ref[...]
        scr_ref[pl.ds(tb, SUBLANES), :] = jnp.where(i < nb - 1, halo_ref[...], 0.0)
        acc = None
        for k in range(SSM_CONV):
            term = scr_ref[pl.ds(SSM_CONV - 1 - k, tb), :] * w_ref[k:k + 1, :]
            acc = term if acc is None else acc + term
        o_ref[...] = _mx(acc)

    return pl.pallas_call(
        body, name="ssd_conv_bwd_x", grid=(nb,),
        in_specs=[pl.BlockSpec((tb, SSM_CONV_DIM), lambda i: (i, 0)),
                  pl.BlockSpec((SUBLANES, SSM_CONV_DIM), lambda i: (jnp.minimum((i + 1) * hb, s // SUBLANES - 1), 0)),
                  pl.BlockSpec((8, SSM_CONV_DIM), lambda i: (0, 0))],
        out_specs=pl.BlockSpec((tb, SSM_CONV_DIM), lambda i: (i, 0)),
        out_shape=jax.ShapeDtypeStruct((s, SSM_CONV_DIM), MXU_DTYPE),
        scratch_shapes=[pltpu.VMEM((tb + SUBLANES, SSM_CONV_DIM), F32)],
        compiler_params=_params(VMEM_BIG),
    )(dpre, dpre, cw)


def _expand_constants():
    ex = np.zeros((LANES, SSM_D_INNER), np.float32)
    for h in range(SSM_HEADS):
        ex[h, h * 64:(h + 1) * 64] = 1.0
    return ex, np.ascontiguousarray(ex.T)


def _ssd_common(dtr_ref, par_ref, ex_ref, xc_ref):
    lc = SSM_CHUNK
    lane = lax.broadcasted_iota(jnp.int32, (1, LANES), 1)
    is_head = lane < SSM_HEADS
    par = par_ref[...]
    pre = dtr_ref[...] + par[0:1, :]
    dt = jnp.where(is_head, jax.nn.softplus(pre), 0.0)
    a = jnp.where(is_head, -jnp.exp(par[1:2, :]), 0.0)
    tri_b = lax.broadcasted_iota(jnp.int32, (lc, lc), 0) >= lax.broadcasted_iota(jnp.int32, (lc, lc), 1)
    tri = tri_b.astype(F32)
    da = dt * a
    acs = _dot_onehot(tri, da, onehot="a")
    acs_t = _dot_onehot(da, tri, (((0,), (1,)), ((), ())))
    wide = _dot_onehot(jnp.concatenate([dt, acs, par], axis=0), ex_ref[...])
    dt_x, acs_x, d_x = wide[0:lc], wide[lc:2 * lc], wide[2 * lc + 2:2 * lc + 3]
    last_x = acs_x[lc - 1:lc, :]
    xs = xc_ref[:, 0:SSM_D_INNER]
    return dict(pre=pre, dt=dt, a=a, tri_b=tri_b, tri=tri, acs=acs, acs_t=acs_t, dt_x=dt_x, d_x=d_x, xs=xs,
                xdt=xs * dt_x, e_x=jnp.exp(acs_x), dte_x=jnp.exp(last_x - acs_x), cd_x=jnp.exp(last_x),
                is_head=is_head)


def _decay_in(q, h):
    seg = q["acs"][:, h:h + 1] - q["acs_t"][h:h + 1, :]
    return jnp.exp(jnp.where(q["tri_b"], seg, -jnp.inf))


def _ssd_scan_fwd(xc, dtr, par):
    s = xc.shape[0]
    lc = SSM_CHUNK
    nc = s // lc
    ex, _ = _expand_constants()

    def body(xc_ref, dtr_ref, par_ref, ex_ref, y_ref, prev_ref, st_ref):
        @pl.when(pl.program_id(0) == 0)
        def _():
            st_ref[...] = jnp.zeros_like(st_ref)

        q = _ssd_common(dtr_ref, par_ref, ex_ref, xc_ref)
        lane = lax.broadcasted_iota(jnp.int32, (1, LANES), 1)
        for g in range(SSM_GROUPS):
            sl = slice(g * GROUP_W, (g + 1) * GROUP_W)
            bg = _mx(xc_ref[:, SSM_D_INNER + g * SSM_STATE:SSM_D_INNER + (g + 1) * SSM_STATE])
            cg = _mx(xc_ref[:, SSM_D_INNER + (SSM_GROUPS + g) * SSM_STATE:SSM_D_INNER + (SSM_GROUPS + g + 1) * SSM_STATE])
            gm = _dot(cg, bg, NT)
            prev = st_ref[g]
            prev_ref[g] = prev
            yoff = _dot(cg, _mx(prev)) * q["e_x"][:, sl]
            st_ref[g] = prev * q["cd_x"][:, sl] + _dot(bg, _mx(q["xdt"][:, sl] * q["dte_x"][:, sl]), TN)
            pairs = []
            for pr in range(2):
                xp = _mx(q["xdt"][:, g * GROUP_W + pr * LANES:g * GROUP_W + (pr + 1) * LANES])
                both = [_dot(_mx(gm * _decay_in(q, 4 * g + 2 * pr + r2)), xp) for r2 in range(2)]
                pairs.append(jnp.where(lane < 64, both[0], both[1]))
            y_ref[:, sl] = jnp.concatenate(pairs, axis=1) + yoff + q["xs"][:, sl] * q["d_x"][:, sl]

    return pl.pallas_call(
        body, name="ssd_scan_fwd", grid=(nc,),
        in_specs=[pl.BlockSpec((lc, SSM_CONV_DIM), lambda c: (c, 0)),
                  pl.BlockSpec((lc, LANES), lambda c: (c, 0)),
                  pl.BlockSpec((8, LANES), lambda c: (0, 0)),
                  pl.BlockSpec((LANES, SSM_D_INNER), lambda c: (0, 0))],
        out_specs=[pl.BlockSpec((lc, SSM_D_INNER), lambda c: (c, 0)),
                   pl.BlockSpec((None, SSM_GROUPS, SSM_STATE, GROUP_W), lambda c: (c, 0, 0, 0))],
        out_shape=[jax.ShapeDtypeStruct((s, SSM_D_INNER), F32),
                   jax.ShapeDtypeStruct((nc, SSM_GROUPS, SSM_STATE, GROUP_W), F32)],
        scratch_shapes=[pltpu.VMEM((SSM_GROUPS, SSM_STATE, GROUP_W), F32)],
        compiler_params=_params(VMEM_BIG),
    )(xc, dtr, par, _mx(jnp.asarray(ex)))


def _ssd_scan_bwd(dy, xc, dtr, par, prev, carry=()):
    s = xc.shape[0]
    lc = SSM_CHUNK
    nc = s // lc
    ex, ex_t = _expand_constants()
    carry = list(carry)
    anywhere = pl.BlockSpec(memory_space=pl.ANY)

    def body(dy_ref, xc_ref, dtr_ref, par_ref, prev_ref, ex_ref, ext_ref, dxc_ref, ddtr_ref, sums_ref,
             gst_ref, tacs_ref, tdt_ref, tdd_ref):
        @pl.when(pl.program_id(0) == 0)
        def _():
            gst_ref[...] = jnp.zeros_like(gst_ref)
            sums_ref[...] = jnp.zeros_like(sums_ref)

        q = _ssd_common(dtr_ref, par_ref, ex_ref, xc_ref)
        lane = lax.broadcasted_iota(jnp.int32, (1, LANES), 1)
        row = lax.broadcasted_iota(jnp.int32, (lc, 1), 0)
        dacs_rows = jnp.zeros((lc, LANES), F32)
        dacs_cols_t = jnp.zeros((LANES, lc), F32)
        for g in range(SSM_GROUPS):
            sl = slice(g * GROUP_W, (g + 1) * GROUP_W)
            b_lo = SSM_D_INNER + g * SSM_STATE
            c_lo = SSM_D_INNER + (SSM_GROUPS + g) * SSM_STATE
            bg = _mx(xc_ref[:, b_lo:b_lo + SSM_STATE])
            cg = _mx(xc_ref[:, c_lo:c_lo + SSM_STATE])
            dyg = dy_ref[:, sl]
            xsg, xdtg = q["xs"][:, sl], q["xdt"][:, sl]
            eg, dteg, cdg = q["e_x"][:, sl], q["dte_x"][:, sl], q["cd_x"][:, sl]
            prevg = prev_ref[g]
            gs = gst_ref[g]
            prevm, gsm = _mx(prevg), _mx(gs)
            tdd_ref[:, sl] = dyg * xsg
            dxs = dyg * q["d_x"][:, sl]
            t_acs = dyg * _dot(cg, prevm) * eg
            dcp = _mx(dyg * eg)
            dc = _dot(dcp, prevm, NT)
            dprev = _dot(cg, dcp, TN)
            db = _dot(_mx(xdtg * dteg), gsm, NT)
            dx2 = _dot(bg, gsm)
            dxdt = dx2 * dteg
            ddte = dx2 * xdtg * dteg
            t_acs = t_acs - ddte
            last = (jnp.sum(ddte, axis=0, keepdims=True)
                    + jnp.sum(gs * prevg, axis=0, keepdims=True) * cdg)
            gm = _dot(cg, bg, NT)
            dgm = jnp.zeros((lc, lc), F32)
            pair_dx = []
            for pr in range(2):
                lo = g * GROUP_W + pr * LANES
                xp = _mx(q["xdt"][:, lo:lo + LANES])
                dyp = dy_ref[:, lo:lo + LANES]
                both = []
                for r2 in range(2):
                    h = 4 * g + 2 * pr + r2
                    mine = (lane >= 64 * r2) & (lane < 64 * (r2 + 1))
                    lm = _decay_in(q, h)
                    m = gm * lm
                    dm = _dot(_mx(jnp.where(mine, dyp, 0.0)), xp, NT)
                    dgm = dgm + dm * lm
                    w = dm * m
                    dacs_rows = dacs_rows + jnp.sum(w, axis=1, keepdims=True) * (lane == h).astype(F32)
                    head_row = (lax.broadcasted_iota(jnp.int32, (LANES, 1), 0) == h).astype(F32)
                    dacs_cols_t = dacs_cols_t + head_row * jnp.sum(w, axis=0, keepdims=True)
                    both.append(_dot(_mx(m), _mx(dyp), TN))
                pair_dx.append(jnp.where(lane < 64, both[0], both[1]))
            dxdt = dxdt + jnp.concatenate(pair_dx, axis=1)
            dgmm = _mx(dgm)
            dc = dc + _dot(dgmm, bg)
            db = db + _dot(dgmm, cg, TN)
            dxs = dxs + dxdt * q["dt_x"][:, sl]
            tdt_ref[:, sl] = dxdt * xsg
            tacs_ref[:, sl] = t_acs + jnp.where(row == lc - 1, last, 0.0)
            dxc_ref[:, sl] = dxs
            dxc_ref[:, b_lo:b_lo + SSM_STATE] = db
            dxc_ref[:, c_lo:c_lo + SSM_STATE] = dc
            gst_ref[g] = gs * cdg + dprev
        tdd = jnp.broadcast_to(jnp.sum(tdd_ref[...], axis=0, keepdims=True), (8, SSM_D_INNER))
        heads_of = _dot_onehot(jnp.concatenate([tacs_ref[...], tdt_ref[...], tdd], axis=0), ext_ref[...])
        dacs = heads_of[0:lc] + dacs_rows - dacs_cols_t.T
        dda = _dot_onehot(q["tri"], dacs, TN, onehot="a")
        ddt = dda * q["a"] + heads_of[lc:2 * lc]
        ddtr = jnp.where(q["is_head"], ddt * jax.nn.sigmoid(q["pre"]), 0.0)
        ddtr_ref[...] = _mx(ddtr)
        sums_ref[0:1, :] += jnp.sum(ddtr, axis=0, keepdims=True)
        sums_ref[1:2, :] += jnp.sum(dda * q["dt"], axis=0, keepdims=True) * q["a"]
        sums_ref[2:3, :] += heads_of[2 * lc:2 * lc + 1]

    rev = lambda c: nc - 1 - c
    wide = pltpu.VMEM((lc, SSM_D_INNER), F32)
    res = pl.pallas_call(
        _carrying(body, 7, 3, 4, carry, (nc,)), name="ssd_scan_bwd", grid=(nc,),
        in_specs=[pl.BlockSpec((lc, SSM_D_INNER), lambda c: (rev(c), 0)),
                  pl.BlockSpec((lc, SSM_CONV_DIM), lambda c: (rev(c), 0)),
                  pl.BlockSpec((lc, LANES), lambda c: (rev(c), 0)),
                  pl.BlockSpec((8, LANES), lambda c: (0, 0)),
                  pl.BlockSpec((None, SSM_GROUPS, SSM_STATE, GROUP_W), lambda c: (rev(c), 0, 0, 0)),
                  pl.BlockSpec((LANES, SSM_D_INNER), lambda c: (0, 0)),
                  pl.BlockSpec((SSM_D_INNER, LANES), lambda c: (0, 0))] + [anywhere] * len(carry),
        out_specs=[pl.BlockSpec((lc, SSM_CONV_DIM), lambda c: (rev(c), 0)),
                   pl.BlockSpec((lc, LANES), lambda c: (rev(c), 0)),
                   pl.BlockSpec((8, LANES), lambda c: (0, 0))] + [anywhere] * len(carry),
        out_shape=[jax.ShapeDtypeStruct((s, SSM_CONV_DIM), F32), jax.ShapeDtypeStruct((s, LANES), MXU_DTYPE),
                   jax.ShapeDtypeStruct((8, LANES), F32)] + _exchange_shapes(carry),
        scratch_shapes=[pltpu.VMEM((SSM_GROUPS, SSM_STATE, GROUP_W), F32), wide, wide, wide]
        + (_exchange_sems(carry) if carry else []),
        compiler_params=_params(VMEM_BIG),
    )(dy, xc, dtr, par, prev, _mx(jnp.asarray(ex)), _mx(jnp.asarray(ex_t)), *[arr for _, arr in carry])
    return (*res[:3], list(res[3:]))


def _group_norm_parts(yv, zv):
    yg = yv * jax.nn.silu(zv)
    normed, rinvs = [], []
    for g in range(SSM_GROUPS):
        blk = yg[:, g * GROUP_W:(g + 1) * GROUP_W]
        rinv = lax.rsqrt(jnp.mean(blk * blk, axis=-1, keepdims=True) + RMS_EPS)
        normed.append(blk * rinv)
        rinvs.append(rinv)
    return normed, rinvs


def _gnorm_fwd(y, zx, nw):
    s = y.shape[0]
    tb = _tile(s, 512)

    def body(y_ref, z_ref, w_ref, o_ref):
        normed, _ = _group_norm_parts(y_ref[...], z_ref[...])
        for g in range(SSM_GROUPS):
            sl = slice(g * GROUP_W, (g + 1) * GROUP_W)
            o_ref[:, sl] = _mx(normed[g] * w_ref[0:1, sl])

    row = pl.BlockSpec((tb, SSM_D_INNER), lambda i: (i, 0))
    return pl.pallas_call(
        body, name="ssd_gnorm_fwd", grid=(s // tb,),
        in_specs=[row, row, pl.BlockSpec((8, SSM_D_INNER), lambda i: (0, 0))],
        out_specs=row, out_shape=jax.ShapeDtypeStruct((s, SSM_D_INNER), MXU_DTYPE),
    )(y, zx, nw)


def _gnorm_bwd(y, zx, nw, dyn):
    s = y.shape[0]
    tb = _tile(s, 512)

    def body(y_ref, z_ref, w_ref, d_ref, dy_ref, dz_ref, sums_ref):
        @pl.when(pl.program_id(0) == 0)
        def _():
            sums_ref[...] = jnp.zeros_like(sums_ref)

        yv, zv = y_ref[...], z_ref[...]
        normed, rinvs = _group_norm_parts(yv, zv)
        gate = jax.nn.silu(zv)
        dgate = _dsilu(zv)
        for g in range(SSM_GROUPS):
            sl = slice(g * GROUP_W, (g + 1) * GROUP_W)
            dv = d_ref[:, sl]
            n = normed[g]
            sums_ref[0:1, sl] += jnp.sum(dv * n, axis=0, keepdims=True)
            dn = dv * w_ref[0:1, sl]
            dyg = rinvs[g] * (dn - n * jnp.mean(dn * n, axis=-1, keepdims=True))
            dy_ref[:, sl] = dyg * gate[:, sl]
            dz_ref[:, sl] = _mx(dyg * yv[:, sl] * dgate[:, sl])

    row = pl.BlockSpec((tb, SSM_D_INNER), lambda i: (i, 0))
    par = pl.BlockSpec((8, SSM_D_INNER), lambda i: (0, 0))
    return pl.pallas_call(
        body, name="ssd_gnorm_bwd", grid=(s // tb,),
        in_specs=[row, row, par, row], out_specs=[row, row, par],
        out_shape=[jax.ShapeDtypeStruct((s, SSM_D_INNER), F32), jax.ShapeDtypeStruct((s, SSM_D_INNER), MXU_DTYPE),
                   jax.ShapeDtypeStruct((8, SSM_D_INNER), F32)],
    )(y, zx, nw, dyn)


def _rows8(v):
    v = v.reshape(1, -1)
    return jnp.pad(v, ((0, 7), (0, 0)))


def _ssd_weights(w_in, w_out):
    nzx = SSM_D_INNER + SSM_CONV_DIM
    wzx = w_in[:, :nzx]
    wdt = jnp.pad(w_in[:, nzx:], ((0, 0), (0, LANES - SSM_HEADS)))
    return dict(wzx=wzx, wdt=wdt, wzx_t=wzx.T, wdt_t=wdt.T, wout=w_out, wout_t=w_out.T)


def _ssd_fwd(u, w, cw, cb, par, nw):
    zx = _mm_nn("ssd_in_zx", u, w["wzx"], F32)
    dtr = _mm_nn("ssd_in_dt", u, w["wdt"], F32)
    xc = _conv_fwd(zx, cw, cb)
    y, prev = _ssd_scan_fwd(xc, dtr, par)
    yn = _gnorm_fwd(y, zx, nw)
    out = _mm_nn("ssd_out", yn, w["wout"], F32)
    return out, dict(zx=zx, dtr=dtr, xc=xc, y=y, prev=prev, yn=yn)


def _ssd_bwd(dy, u, w, cw, cb, par, nw, res, carry=()):
    dyn = _mm_nn("ssd_dyn", dy, w["wout_t"], F32)
    dw_out = _mm_tn("ssd_dw_out", res["yn"], dy)
    dys, dz, nsum = _gnorm_bwd(res["y"], res["zx"], nw, dyn)
    dxc, ddtr, ssum, carried = _ssd_scan_bwd(dys, res["xc"], res["dtr"], par, res["prev"],
                                             list(carry) + [(True, _row_slots(dw_out[None]))])
    dpre, csum = _conv_bwd_pre(res["zx"], dxc, cw, cb)
    dxbc = _conv_bwd_x(dpre, cw)
    wt = w["wzx_t"]
    du = _mm_sum("ssd_du", [(dz, wt[:SSM_D_INNER]), (dxbc, wt[SSM_D_INNER:]), (ddtr, w["wdt_t"])], tn=512)
    dw_in = jnp.concatenate(
        [_mm_tn("ssd_dw_z", u, dz), _mm_tn("ssd_dw_x", u, dxbc), _mm_tn("ssd_dw_dt", u, ddtr)[:, :SSM_HEADS]], axis=1)
    small = dict(conv_w=csum[:SSM_CONV], conv_b=csum[SSM_CONV], dt_bias=ssum[0, :SSM_HEADS],
                 a_log=ssum[1, :SSM_HEADS], d=ssum[2, :SSM_HEADS], norm_w=nsum[0])
    return du, dw_in, small, carried


def _ada_fwd(c_all, ada_w, ada_b_mine):
    nl, _, ncol = ada_w.shape

    def body(c_ref, w_ref, b_ref, o_ref):
        cond = _mx(jax.nn.silu(c_ref[...]))
        for i in range(nl):
            o_ref[i] = _dot(cond, _mx(w_ref[i])) + b_ref[i:i + 1, :]

    return pl.pallas_call(
        body, name="ada_fwd", out_shape=jax.ShapeDtypeStruct((nl, 2 * N_DEV, ncol), F32),
        compiler_params=_params(VMEM_BIG),
    )(c_all, ada_w, ada_b_mine)


def _ada_bwd(c_all, dmod_cols):
    nl, _, ncol = dmod_cols.shape

    def body(c_ref, d_ref, o_ref):
        cond = _mx(jax.nn.silu(c_ref[...]))
        for i in range(nl):
            o_ref[i] = _dot(cond, _mx(d_ref[i]), TN)

    return pl.pallas_call(
        body, name="ada_bwd", out_shape=jax.ShapeDtypeStruct((nl, D_MODEL, ncol), F32),
        compiler_params=_params(VMEM_BIG),
    )(c_all, dmod_cols)


def _adamw(gslots, w, m, v, name):
    k, r, c = gslots.shape
    tr = _tile(r, 256) if r % 256 == 0 else r
    c1 = 1.0 - ADAM_B1 ** ADAM_STEP
    c2 = 1.0 - ADAM_B2 ** ADAM_STEP

    def body(g_ref, w_ref, m_ref, v_ref, go_ref, d_ref, mo_ref, vo_ref):
        g = g_ref[0]
        for slot in range(1, k):
            g = g + g_ref[slot]
        mn = ADAM_B1 * m_ref[...] + (1.0 - ADAM_B1) * g
        vn = ADAM_B2 * v_ref[...] + (1.0 - ADAM_B2) * jnp.square(g)
        go_ref[...] = g
        mo_ref[...] = mn
        vo_ref[...] = vn
        d_ref[...] = -ADAM_LR * ((mn / c1) / (jnp.sqrt(vn / c2) + ADAM_EPS) + ADAM_WD * w_ref[...])

    row = pl.BlockSpec((tr, c), lambda i: (i, 0))
    shp = jax.ShapeDtypeStruct((r, c), F32)
    return pl.pallas_call(
        body, name=name, grid=(r // tr,),
        in_specs=[pl.BlockSpec((k, tr, c), lambda i: (0, i, 0)), row, row, row],
        out_specs=[row, row, row, row], out_shape=[shp, shp, shp, shp],
        compiler_params=_params(VMEM_BIG),
    )(gslots, w, m, v)


def _adamw_any(gslots, w, m, v, name):
    shape = w.shape
    two_d = (-1, shape[-1])
    k = gslots.shape[0]
    outs = _adamw(gslots.reshape((k,) + w.reshape(two_d).shape), w.reshape(two_d), m.reshape(two_d),
                  v.reshape(two_d), name)
    return tuple(o.reshape(shape) for o in outs)


def _cols_from_slots(g):
    g = jnp.moveaxis(g, 0, -2)
    return g.reshape(g.shape[:-2] + (g.shape[-2] * g.shape[-1],))


def _rows_from_slots(g):
    g = jnp.moveaxis(g, 0, -3)
    return g.reshape(g.shape[:-3] + (g.shape[-3] * g.shape[-2], g.shape[-1]))


def _col_slots(g):
    cs = g.shape[-1] // N_DEV
    return jnp.moveaxis(g.reshape(g.shape[:-1] + (N_DEV, cs)), -2, 0)


def _row_slots(g):
    rs = g.shape[-2] // N_DEV
    return jnp.moveaxis(g.reshape(g.shape[:-2] + (N_DEV, rs, g.shape[-1])), -3, 0)


def _gather_cols(w, name, dtype=None):
    return _cols_from_slots(_all_gather(w.astype(dtype or MXU_DTYPE), name))


def _gather_rows(w, name):
    return _rows_from_slots(_all_gather(_mx(w), name))


def kernel(x, c, ada_w, ada_b, ln_mix_g, ln_mix_b, ln_mlp_g, ln_mlp_b, mlp_w1, mlp_w2, fox_w_in, fox_b_f, fox_w_o, ssm_w_in, ssm_conv_w, ssm_conv_b, ssm_dt_bias, ssm_a_log, ssm_d, ssm_norm_w, ssm_w_out, loss_target, m_ada_w, m_ada_b, m_ln_mix_g, m_ln_mix_b, m_ln_mlp_g, m_ln_mlp_b, m_mlp_w1, m_mlp_w2, m_fox_w_in, m_fox_b_f, m_fox_w_o, m_ssm_w_in, m_ssm_conv_w, m_ssm_conv_b, m_ssm_dt_bias, m_ssm_a_log, m_ssm_d, m_ssm_norm_w, m_ssm_w_out, v_ada_w, v_ada_b, v_ln_mix_g, v_ln_mix_b, v_ln_mlp_g, v_ln_mlp_b, v_mlp_w1, v_mlp_w2, v_fox_w_in, v_fox_b_f, v_fox_w_o, v_ssm_w_in, v_ssm_conv_w, v_ssm_conv_b, v_ssm_dt_bias, v_ssm_a_log, v_ssm_d, v_ssm_norm_w, v_ssm_w_out):
    me = 4 * lax.axis_index("x") + 2 * lax.axis_index("y") + lax.axis_index("c")
    xs = x[0]
    target = loss_target[0]
    d = D_MODEL

    c_all = _all_gather(c, "gather_c").reshape(N_DEV, d)
    c_all = jnp.pad(c_all, ((0, N_DEV), (0, 0)))
    ncol = ada_w.shape[-1]
    ada_b_mine = lax.dynamic_slice_in_dim(ada_b, me * ncol, ncol, axis=1)
    mod_cols = _ada_fwd(c_all, ada_w, ada_b_mine)
    mod_all = _all_gather(mod_cols, "gather_mod")
    mod = lax.dynamic_index_in_dim(mod_all, me, axis=2, keepdims=False)
    mod = jnp.moveaxis(mod, 0, 1).reshape(DEPTH, 6, d)

    def pv_rows(*rows):
        return jnp.pad(jnp.stack(rows), ((0, 8 - len(rows)), (0, 0)))

    fw = _fox_weights(_gather_cols(fox_w_in, "gather_fox_in")[0], _gather_rows(fox_w_o, "gather_fox_o")[0])
    conv_w = _gather_cols(ssm_conv_w, "gather_conv_w", F32)[0]
    small_vec = jnp.concatenate([ssm_conv_b[0], ssm_norm_w[0]]).reshape(1, -1)
    small_all = _all_gather(small_vec.astype(F32), "gather_conv_b").reshape(N_DEV, -1)
    conv_b = small_all[:, :SSM_CONV_DIM // N_DEV].reshape(-1)
    norm_w = small_all[:, SSM_CONV_DIM // N_DEV:].reshape(-1)
    cw8 = jnp.pad(conv_w, ((0, 8 - SSM_CONV), (0, 0)))
    cb8 = _rows8(conv_b)
    nw8 = _rows8(norm_w)
    bf8 = _rows8(jnp.pad(fox_b_f[0], (0, LANES - FOX_HEADS)))
    par8 = jnp.pad(jnp.stack([jnp.pad(p[0], (0, LANES - SSM_HEADS)) for p in (ssm_dt_bias, ssm_a_log, ssm_d)]),
                   ((0, 5), (0, 0)))

    sh_a, sc_a, g_a, sh_m, sc_m, g_m = [mod[:, k] for k in range(6)]
    u0 = _modulate(xs, pv_rows(1.0 + sc_a[0], sh_a[0]), "modulate0")
    y0, fres, gathered = _fox_fwd(u0, fw, bf8, [(False, _mx(mlp_w1)), (False, _mx(mlp_w2))])
    w1 = _cols_from_slots(gathered[0])
    w2 = _rows_from_slots(gathered[1])
    pv0 = pv_rows(1.0 + g_a[0], ln_mix_g[0], ln_mix_b[0], 1.0 + sc_m[0], sh_m[0])
    x1, u1 = _ln_fwd(xs, y0, pv0, "ln_mix0")
    y1, (h0, a0), gathered = _mlp_fwd(u1, w1[0], w2[0], "0", [(False, _mx(ssm_w_in)), (False, _mx(ssm_w_out))])
    sw = _ssd_weights(_cols_from_slots(gathered[0])[0], _rows_from_slots(gathered[1])[0])
    pv1 = pv_rows(1.0 + g_m[0], ln_mlp_g[0], ln_mlp_b[0], 1.0 + sc_a[1], sh_a[1])
    x2, u2 = _ln_fwd(x1, y1, pv1, "ln_mlp0")
    y2, sres = _ssd_fwd(u2, sw, cw8, cb8, par8, nw8)
    pv2 = pv_rows(1.0 + g_a[1], ln_mix_g[1], ln_mix_b[1], 1.0 + sc_m[1], sh_m[1])
    x3, u3 = _ln_fwd(x2, y2, pv2, "ln_mix1")
    y3, (h1, a1), _ = _mlp_fwd(u3, w1[1], w2[1], "1")
    pv3 = pv_rows(1.0 + g_m[1], ln_mlp_g[1], ln_mlp_b[1])

    dx3, dy3, s3 = _ln_bwd(x3, y3, pv3, "ln_mlp1_bwd", target=target)
    loss = lax.psum(s3[5, 0], ("x", "y", "c"))
    du3, dw1_1, dw2_1, _ = _mlp_bwd(dy3, u3, h1, a1, w1[1].T, w2[1].T, "1")
    dx2, dy2, s2 = _ln_bwd(x2, y2, pv2, "ln_mix1_bwd", dxo=dx3, du=du3)
    du2, d_ssm_in, ssm_small, ex_scan = _ssd_bwd(
        dy2, u2, sw, cw8, cb8, par8, nw8, sres,
        [(True, _col_slots(dw1_1[None])), (True, _row_slots(dw2_1[None]))])
    dx1, dy1, s1 = _ln_bwd(x1, y1, pv1, "ln_mlp0_bwd", dxo=dx2, du=du2)
    half = d_ssm_in.shape[0] // 2
    du1, dw1_0, dw2_0, ex_mlp = _mlp_bwd(dy1, u1, h0, a0, w1[0].T, w2[0].T, "0",
                                         [(True, _col_slots(d_ssm_in[None, :half]))],
                                         [(True, _col_slots(d_ssm_in[None, half:]))])
    dx0, dy0, s0 = _ln_bwd(xs, y0, pv0, "ln_mix0_bwd", dxo=dx1, du=du1)
    late = [(True, _col_slots(dw1_0[None])), (True, _row_slots(dw2_0[None])),
            (True, _col_slots(ssm_small["conv_w"][None])), (True, _col_slots(ssm_small["conv_b"][None])),
            (True, _col_slots(ssm_small["norm_w"][None]))]
    du0, fox_sums, ex_post = _fox_bwd(dy0, u0, fw, bf8, fres, late)
    grad_x, sx = _mod_bwd(dx0, du0, xs, pv_rows(1.0 + sc_a[0], sh_a[0]), "modulate0_bwd")

    dmod = jnp.stack([
        jnp.stack([sx[1], sx[0], s0[4], s0[1], s0[0], s1[4]]),
        jnp.stack([s1[1], s1[0], s2[4], s2[1], s2[0], s3[4]]),
    ]).reshape(DEPTH, 6 * d)

    def pad_rows(v):
        v = v.reshape(-1, LANES) if v.size % LANES == 0 else jnp.pad(v.reshape(-1), (0, LANES - v.size)).reshape(1, LANES)
        return jnp.pad(v, ((0, (-v.shape[0]) % 8), (0, 0)))

    small_parts = [dmod, jnp.stack([s0[2], s2[2]]), jnp.stack([s0[3], s2[3]]), jnp.stack([s1[2], s3[2]]),
                   jnp.stack([s1[3], s3[3]]), fox_sums[0, :FOX_HEADS], ssm_small["dt_bias"], ssm_small["a_log"],
                   ssm_small["d"]]
    packed = [pad_rows(p) for p in small_parts]
    offsets = np.cumsum([0] + [p.shape[0] for p in packed])
    small_all_g = _all_gather(jnp.concatenate(packed, axis=0), "gather_small_grads")

    def unpack(idx, shape):
        n = int(np.prod(shape))
        blk = small_all_g[:, offsets[idx]:offsets[idx + 1]].reshape(N_DEV, -1)[:, :n]
        return blk.reshape((N_DEV,) + tuple(shape))

    dmod_all = unpack(0, (DEPTH, 6 * d))
    dmod_cols = lax.dynamic_slice_in_dim(dmod_all, me * ncol, ncol, axis=2)
    dmod_cols = jnp.pad(jnp.moveaxis(dmod_cols, 0, 1), ((0, 0), (0, N_DEV), (0, 0)))
    g_ada_w = _ada_bwd(c_all, dmod_cols)

    shards = dict(
        mlp_w1=jnp.concatenate([ex_post[0], ex_scan[0]], axis=1),
        mlp_w2=jnp.concatenate([ex_post[1], ex_scan[1]], axis=1),
        ssm_w_in=jnp.concatenate(ex_mlp, axis=2), ssm_w_out=ex_scan[2],
        ssm_conv_w=ex_post[2], ssm_conv_b=ex_post[3], ssm_norm_w=ex_post[4],
        fox_w_in=ex_post[5], fox_w_o=ex_post[6],
        ada_w=g_ada_w[None], ada_b=dmod_all,
        ln_mix_g=unpack(1, (DEPTH, d)), ln_mix_b=unpack(2, (DEPTH, d)),
        ln_mlp_g=unpack(3, (DEPTH, d)), ln_mlp_b=unpack(4, (DEPTH, d)),
        fox_b_f=unpack(5, (1, FOX_HEADS)), ssm_dt_bias=unpack(6, (1, SSM_HEADS)),
        ssm_a_log=unpack(7, (1, SSM_HEADS)), ssm_d=unpack(8, (1, SSM_HEADS)),
    )
    weights = dict(ada_w=ada_w, ada_b=ada_b, ln_mix_g=ln_mix_g, ln_mix_b=ln_mix_b, ln_mlp_g=ln_mlp_g, ln_mlp_b=ln_mlp_b,
                   mlp_w1=mlp_w1, mlp_w2=mlp_w2, fox_w_in=fox_w_in, fox_b_f=fox_b_f, fox_w_o=fox_w_o, ssm_w_in=ssm_w_in,
                   ssm_conv_w=ssm_conv_w, ssm_conv_b=ssm_conv_b, ssm_dt_bias=ssm_dt_bias, ssm_a_log=ssm_a_log,
                   ssm_d=ssm_d, ssm_norm_w=ssm_norm_w, ssm_w_out=ssm_w_out)
    mom1 = dict(ada_w=m_ada_w, ada_b=m_ada_b, ln_mix_g=m_ln_mix_g, ln_mix_b=m_ln_mix_b, ln_mlp_g=m_ln_mlp_g,
                ln_mlp_b=m_ln_mlp_b, mlp_w1=m_mlp_w1, mlp_w2=m_mlp_w2, fox_w_in=m_fox_w_in, fox_b_f=m_fox_b_f,
                fox_w_o=m_fox_w_o, ssm_w_in=m_ssm_w_in, ssm_conv_w=m_ssm_conv_w, ssm_conv_b=m_ssm_conv_b,
                ssm_dt_bias=m_ssm_dt_bias, ssm_a_log=m_ssm_a_log, ssm_d=m_ssm_d, ssm_norm_w=m_ssm_norm_w,
                ssm_w_out=m_ssm_w_out)
    mom2 = dict(ada_w=v_ada_w, ada_b=v_ada_b, ln_mix_g=v_ln_mix_g, ln_mix_b=v_ln_mix_b, ln_mlp_g=v_ln_mlp_g,
                ln_mlp_b=v_ln_mlp_b, mlp_w1=v_mlp_w1, mlp_w2=v_mlp_w2, fox_w_in=v_fox_w_in, fox_b_f=v_fox_b_f,
                fox_w_o=v_fox_w_o, ssm_w_in=v_ssm_w_in, ssm_conv_w=v_ssm_conv_w, ssm_conv_b=v_ssm_conv_b,
                ssm_dt_bias=v_ssm_dt_bias, ssm_a_log=v_ssm_a_log, ssm_d=v_ssm_d, ssm_norm_w=v_ssm_norm_w,
                ssm_w_out=v_ssm_w_out)
    names = list(weights)
    stepped = {n: _adamw_any(shards[n], weights[n], mom1[n], mom2[n], f"adamw_{n}") for n in names}
    return (loss, grad_x[None], *[stepped[n][0] for n in names], *[stepped[n][1] for n in names],
            *[stepped[n][2] for n in names], *[stepped[n][3] for n in names])
```
